```python
import math
import jax
import jax.numpy as jnp
from jax import lax
import numpy as np

D_MODEL = 2048
BATCH = 8
SEQ = 4096
DEPTH = 2

CHUNK = 64
N_LEFT_CHUNKS = 8
BAND = (N_LEFT_CHUNKS + 1) * CHUNK
HEAD_DIM = 64
D_ATT = D_MODEL // 2
N_HEADS_ATT = D_ATT // HEAD_DIM
REL_CLIP = 128
N_REL = (CHUNK - 1) + REL_CLIP + 1
D_RWKV = D_MODEL // 2
N_HEADS_RWKV = D_RWKV // HEAD_DIM
DECAY_LORA = 64
AAA_LORA = 64
GATE_LORA = 128
N_B_IN = 3 * D_RWKV + DECAY_LORA + AAA_LORA + GATE_LORA
N_IN_AB = 3 * D_ATT + N_B_IN
D_SSM = D_MODEL // 2
SSM_GROUP = 16
N_SSM_GROUPS = D_SSM // SSM_GROUP
SSM_STATE = 64
D_FF = 5632
D_PLE = 256
RMS_EPS = 1e-6
GN_EPS = 64e-5

kernel_name = 'hybrid_chunk_causal_encoder'


def rms_norm(x, g):
    xf = x.astype(jnp.float32)
    y = xf * lax.rsqrt(jnp.mean(xf * xf, axis=-1, keepdims=True) + RMS_EPS)
    return (y * g.astype(jnp.float32)).astype(x.dtype)


def swiglu_ffn(x, w_gate, w_up, w_down):
    return (jax.nn.silu(x @ w_gate) * (x @ w_up)) @ w_down


def rel_bias_index():
    i = np.arange(CHUNK)[:, None]
    j = np.arange(BAND)[None, :]
    dist = i + N_LEFT_CHUNKS * CHUNK - j
    return np.clip(dist, -(CHUNK - 1), REL_CLIP) + (CHUNK - 1)


def chunked_band_attention(q, k, v, q_gain, k_gain, rel_bias):
    bsz, t, h, dh = q.shape
    n_chunks = t // CHUNK
    pad = N_LEFT_CHUNKS * CHUNK
    q = rms_norm(q, q_gain) * (dh ** -0.5)
    k = rms_norm(k, k_gain)
    k_pad = jnp.pad(k, ((0, 0), (pad, 0), (0, 0), (0, 0)))
    v_pad = jnp.pad(v, ((0, 0), (pad, 0), (0, 0), (0, 0)))
    bias = rel_bias.astype(jnp.float32)[:, rel_bias_index()]
    q_chunks = jnp.swapaxes(q.reshape(bsz, n_chunks, CHUNK, h, dh), 0, 1)
    band_pos = jnp.arange(BAND)

    def one_chunk(args):
        c, q_c = args
        start = c * CHUNK
        k_b = lax.dynamic_slice_in_dim(k_pad, start, BAND, axis=1)
        v_b = lax.dynamic_slice_in_dim(v_pad, start, BAND, axis=1)
        s = jnp.einsum('bqhd,bkhd->bhqk', q_c, k_b).astype(jnp.float32) + bias
        valid = (start + band_pos) >= pad
        s = jnp.where(valid, s, -jnp.inf)
        prob = jax.nn.softmax(s, axis=-1).astype(v.dtype)
        return jnp.einsum('bhqk,bkhd->bqhd', prob, v_b)

    out = lax.map(one_chunk, (jnp.arange(n_chunks), q_chunks))
    return jnp.swapaxes(out, 0, 1).reshape(bsz, t, h, dh)


def token_shift(z, mu):
    prev = jnp.pad(z[:, :-1], ((0, 0), (1, 0), (0, 0)))
    return z + (prev - z) * mu


def rwkv7_time_mix(z, mu, w0, w_up, a0, a_up, g_up, k_k, k_a, r_k, lnx_w, lnx_b):
    f32 = jnp.float32
    bsz, t, _ = z.shape
    z = token_shift(z.astype(f32), mu.astype(f32))
    o1, o2, o3 = D_RWKV, 2 * D_RWKV, 3 * D_RWKV
    o4 = o3 + DECAY_LORA
    o5 = o4 + AAA_LORA
    r, k, v = z[..., :o1], z[..., o1:o2], z[..., o2:o3]
    xw, xa, xg = z[..., o3:o4], z[..., o4:o5], z[..., o5:]
    w_log = -jax.nn.softplus(-(w0.astype(f32) + jnp.tanh(xw) @ w_up.astype(f32))) - 0.5
    decay = jnp.exp(-jnp.exp(w_log))
    a = jax.nn.sigmoid(a0.astype(f32) + xa @ a_up.astype(f32))
    g = jax.nn.sigmoid(xg) @ g_up.astype(f32)

    def heads(u):
        return u.reshape(bsz, t, N_HEADS_RWKV, HEAD_DIM)

    kk = heads(k * k_k.astype(f32))
    kk = kk / jnp.maximum(jnp.sqrt(jnp.sum(kk * kk, axis=-1, keepdims=True)), 1e-12)
    k = k * (1.0 + (a - 1.0) * k_a.astype(f32))
    r_h, k_h, v_h, w_h, a_h = heads(r), heads(k), heads(v), heads(decay), heads(a)

    def step(state, inp):
        r_t, w_t, k_t, v_t, ia_t, ib_t = inp
        sa = jnp.einsum('bhvk,bhk->bhv', state, ia_t)
        state = (state * w_t[:, :, None, :] + sa[..., None] * ib_t[:, :, None, :]
                 + v_t[..., None] * k_t[:, :, None, :])
        return state, jnp.einsum('bhvk,bhk->bhv', state, r_t)

    def tm(u):
        return jnp.swapaxes(u, 0, 1)

    s0 = jnp.zeros((bsz, N_HEADS_RWKV, HEAD_DIM, HEAD_DIM), f32)
    _, y = lax.scan(step, s0, (tm(r_h), tm(w_h), tm(k_h), tm(v_h), tm(-kk), tm(kk * a_h)))
    y = tm(y)
    mean = jnp.mean(y, axis=-1, keepdims=True)
    var = jnp.mean(jnp.square(y - mean), axis=-1, keepdims=True)
    y = ((y - mean) * lax.rsqrt(var + GN_EPS)).reshape(bsz, t, D_RWKV)
    y = y * lnx_w.astype(f32) + lnx_b.astype(f32)
    bonus = jnp.sum(r_h * k_h * r_k.astype(f32), axis=-1, keepdims=True) * v_h
    return (y + bonus.reshape(bsz, t, D_RWKV)) * g


def attn_rwkv_mixer(h, w_in, q_gain, k_gain, rel_bias, mu, w0, w_up, a0, a_up, g_up,
                    k_k, k_a, r_k, lnx_w, lnx_b, w_out):
    bsz, t, _ = h.shape
    proj = h @ w_in

    def heads(u):
        return u.reshape(bsz, t, N_HEADS_ATT, HEAD_DIM)

    q = heads(proj[..., :D_ATT])
    k = heads(proj[..., D_ATT:2 * D_ATT])
    v = heads(proj[..., 2 * D_ATT:3 * D_ATT])
    att = chunked_band_attention(q, k, v, q_gain, k_gain, rel_bias).reshape(bsz, t, D_ATT)
    rw = rwkv7_time_mix(proj[..., 3 * D_ATT:], mu, w0, w_up, a0, a_up, g_up,
                        k_k, k_a, r_k, lnx_w, lnx_b).astype(att.dtype)
    return jnp.concatenate([att, rw], axis=-1) @ w_out


def s5_ssm(u, lam_re, lam_im, log_dt, b_re, b_im, c_re, c_im, d_skip):
    f32 = jnp.float32
    bsz, t, _ = u.shape
    G, P, GS = N_SSM_GROUPS, SSM_STATE, SSM_GROUP
    uf = u.astype(f32).reshape(bsz, t, G, GS)
    lr, li = lam_re.astype(f32), lam_im.astype(f32)
    dt = jnp.exp(log_dt.astype(f32))[:, None]
    mag = jnp.exp(lr * dt)
    ab_re, ab_im = mag * jnp.cos(li * dt), mag * jnp.sin(li * dt)
    denom = lr * lr + li * li
    z_re = ((ab_re - 1.0) * lr + ab_im * li) / denom
    z_im = (ab_im * lr - (ab_re - 1.0) * li) / denom
    br, bi = b_re.astype(f32), b_im.astype(f32)
    bb_re = z_re[..., None] * br - z_im[..., None] * bi
    bb_im = z_re[..., None] * bi + z_im[..., None] * br
    bu_re = jnp.einsum('gpc,btgc->btgp', bb_re, uf)
    bu_im = jnp.einsum('gpc,btgc->btgp', bb_im, uf)
    a_re = jnp.broadcast_to(ab_re[None, None], (1, t, G, P))
    a_im = jnp.broadcast_to(ab_im[None, None], (1, t, G, P))

    def combine(left, right):
        al_re, al_im, bl_re, bl_im = left
        ar_re, ar_im, br_re, br_im = right
        return (ar_re * al_re - ar_im * al_im,
                ar_re * al_im + ar_im * al_re,
                ar_re * bl_re - ar_im * bl_im + br_re,
                ar_re * bl_im + ar_im * bl_re + br_im)

    _, _, h_re, h_im = lax.associative_scan(combine, (a_re, a_im, bu_re, bu_im), axis=1)
    y = (jnp.einsum('gcp,btgp->btgc', c_re.astype(f32), h_re)
         - jnp.einsum('gcp,btgp->btgc', c_im.astype(f32), h_im))
    y = y + d_skip.astype(f32).reshape(G, GS) * uf
    return y.reshape(bsz, t, D_SSM).astype(u.dtype)


def s5_mixer(h, w_in, lam_re, lam_im, log_dt, b_re, b_im, c_re, c_im, d_skip, w_out):
    y = jax.nn.gelu(s5_ssm(h @ w_in, lam_re, lam_im, log_dt, b_re, b_im, c_re, c_im, d_skip))
    z = y @ w_out
    return z[..., :D_MODEL] * jax.nn.sigmoid(z[..., D_MODEL:])


def _fwd_setup_inputs(seed: int = 0) -> dict:
    key = jax.random.key(seed)
    ks = iter(jax.random.split(key, 48))
    f32 = jnp.float32
    ne, no = (DEPTH + 1) // 2, DEPTH // 2
    G, P, GS = N_SSM_GROUPS, SSM_STATE, SSM_GROUP

    def normal(shape, scale):
        return jax.random.normal(next(ks), shape, f32) * scale

    def gain(shape):
        return 1.0 + normal(shape, 0.02)

    x = normal((BATCH, SEQ, D_MODEL), 1.0)
    p = normal((DEPTH, BATCH, SEQ, D_PLE), 1.0)
    ffn1_norm = gain((DEPTH, D_MODEL))
    ffn1_w_gate = normal((DEPTH, D_MODEL, D_FF), D_MODEL ** -0.5)
    ffn1_w_up = normal((DEPTH, D_MODEL, D_FF), D_MODEL ** -0.5)
    ffn1_w_down = normal((DEPTH, D_FF, D_MODEL), D_FF ** -0.5)
    mix_norm = gain((DEPTH, D_MODEL))
    ffn2_norm = gain((DEPTH, D_MODEL))
    ffn2_w_gate = normal((DEPTH, D_MODEL, D_FF), D_MODEL ** -0.5)
    ffn2_w_up = normal((DEPTH, D_MODEL, D_FF), D_MODEL ** -0.5)
    ffn2_w_down = normal((DEPTH, D_FF, D_MODEL), D_FF ** -0.5)
    ple_norm = gain((DEPTH, D_MODEL))
    ple_w_gate = normal((DEPTH, D_MODEL, D_MODEL), D_MODEL ** -0.5)
    ple_w_proj = normal((DEPTH, D_PLE, D_MODEL), D_PLE ** -0.5)
    ab_w_in = normal((ne, D_MODEL, N_IN_AB), D_MODEL ** -0.5)
    att_q_gain = gain((ne, HEAD_DIM))
    att_k_gain = gain((ne, HEAD_DIM))
    att_rel_bias = normal((ne, N_HEADS_ATT, N_REL), 0.1)
    rwkv_mu = jax.random.uniform(next(ks), (ne, N_B_IN), f32)
    rwkv_w0 = jnp.linspace(-6.0, -1.0, D_RWKV, dtype=f32) + normal((ne, D_RWKV), 0.1)
    rwkv_w_up = normal((ne, DECAY_LORA, D_RWKV), 0.1 * DECAY_LORA ** -0.5)
    rwkv_a0 = normal((ne, D_RWKV), 0.1)
    rwkv_a_up = normal((ne, AAA_LORA, D_RWKV), 0.5 * AAA_LORA ** -0.5)
    rwkv_g_up = normal((ne, GATE_LORA, D_RWKV), GATE_LORA ** -0.5)
    rwkv_k_k = 0.85 + normal((ne, D_RWKV), 0.02)
    rwkv_k_a = gain((ne, D_RWKV))
    rwkv_r_k = normal((ne, N_HEADS_RWKV, HEAD_DIM), 0.1)
    rwkv_lnx_w = gain((ne, D_RWKV))
    rwkv_lnx_b = normal((ne, D_RWKV), 0.02)
    ab_w_out = normal((ne, D_ATT + D_RWKV, D_MODEL), (D_ATT + D_RWKV) ** -0.5)
    ssm_w_in = normal((no, D_MODEL, D_SSM), D_MODEL ** -0.5)
    ssm_lambda_re = -0.5 + normal((no, G, P), 0.01)
    ssm_lambda_im = math.pi * jnp.arange(P, dtype=f32) + normal((no, G, P), 0.01)
    ssm_log_dt = jax.random.uniform(next(ks), (no, G), f32, math.log(1e-3), math.log(1e-1))
    ssm_b_re = normal((no, G, P, GS), (2 * GS) ** -0.5)
    ssm_b_im = normal((no, G, P, GS), (2 * GS) ** -0.5)
    ssm_c_re = normal((no, G, GS, P), (2 * P) ** -0.5)
    ssm_c_im = normal((no, G, GS, P), (2 * P) ** -0.5)
    ssm_d = normal((no, D_SSM), 1.0)
    ssm_w_out = normal((no, D_SSM, 2 * D_MODEL), D_SSM ** -0.5)
    return {
        'x': x, 'p': p,
        'ffn1_norm': ffn1_norm, 'ffn1_w_gate': ffn1_w_gate, 'ffn1_w_up': ffn1_w_up,
        'ffn1_w_down': ffn1_w_down, 'mix_norm': mix_norm,
        'ffn2_norm': ffn2_norm, 'ffn2_w_gate': ffn2_w_gate, 'ffn2_w_up': ffn2_w_up,
        'ffn2_w_down': ffn2_w_down,
        'ple_norm': ple_norm, 'ple_w_gate': ple_w_gate, 'ple_w_proj': ple_w_proj,
        'ab_w_in': ab_w_in, 'att_q_gain': att_q_gain, 'att_k_gain': att_k_gain,
        'att_rel_bias': att_rel_bias, 'rwkv_mu': rwkv_mu, 'rwkv_w0': rwkv_w0,
        'rwkv_w_up': rwkv_w_up, 'rwkv_a0': rwkv_a0, 'rwkv_a_up': rwkv_a_up,
        'rwkv_g_up': rwkv_g_up, 'rwkv_k_k': rwkv_k_k, 'rwkv_k_a': rwkv_k_a,
        'rwkv_r_k': rwkv_r_k, 'rwkv_lnx_w': rwkv_lnx_w, 'rwkv_lnx_b': rwkv_lnx_b,
        'ab_w_out': ab_w_out,
        'ssm_w_in': ssm_w_in, 'ssm_lambda_re': ssm_lambda_re, 'ssm_lambda_im': ssm_lambda_im,
        'ssm_log_dt': ssm_log_dt, 'ssm_b_re': ssm_b_re, 'ssm_b_im': ssm_b_im,
        'ssm_c_re': ssm_c_re, 'ssm_c_im': ssm_c_im, 'ssm_d': ssm_d, 'ssm_w_out': ssm_w_out,
    }


def _fwd_reference(x, p, ffn1_norm, ffn1_w_gate, ffn1_w_up, ffn1_w_down, mix_norm,
              ffn2_norm, ffn2_w_gate, ffn2_w_up, ffn2_w_down,
              ple_norm, ple_w_gate, ple_w_proj,
              ab_w_in, att_q_gain, att_k_gain, att_rel_bias, rwkv_mu, rwkv_w0,
              rwkv_w_up, rwkv_a0, rwkv_a_up, rwkv_g_up, rwkv_k_k, rwkv_k_a,
              rwkv_r_k, rwkv_lnx_w, rwkv_lnx_b, ab_w_out,
              ssm_w_in, ssm_lambda_re, ssm_lambda_im, ssm_log_dt, ssm_b_re, ssm_b_im,
              ssm_c_re, ssm_c_im, ssm_d, ssm_w_out):
    h = x
    for i in range(DEPTH):
        j = i // 2
        h = h + 0.5 * swiglu_ffn(rms_norm(h, ffn1_norm[i]), ffn1_w_gate[i],
                                 ffn1_w_up[i], ffn1_w_down[i])
        hn = rms_norm(h, mix_norm[i])
        if i % 2 == 0:
            mix = attn_rwkv_mixer(hn, ab_w_in[j], att_q_gain[j], att_k_gain[j],
                                  att_rel_bias[j], rwkv_mu[j], rwkv_w0[j], rwkv_w_up[j],
                                  rwkv_a0[j], rwkv_a_up[j], rwkv_g_up[j], rwkv_k_k[j],
                                  rwkv_k_a[j], rwkv_r_k[j], rwkv_lnx_w[j], rwkv_lnx_b[j],
                                  ab_w_out[j])
        else:
            mix = s5_mixer(hn, ssm_w_in[j], ssm_lambda_re[j], ssm_lambda_im[j],
                           ssm_log_dt[j], ssm_b_re[j], ssm_b_im[j], ssm_c_re[j],
                           ssm_c_im[j], ssm_d[j], ssm_w_out[j])
        h = h + mix
        h = h + 0.5 * swiglu_ffn(rms_norm(h, ffn2_norm[i]), ffn2_w_gate[i],
                                 ffn2_w_up[i], ffn2_w_down[i])
        gate = jax.nn.sigmoid(rms_norm(h, ple_norm[i]) @ ple_w_gate[i])
        h = h + gate * (p[i] @ ple_w_proj[i])
    return h


import jax as _jax
import jax.numpy as _jnp

TWIN_FORMAT = 'train_step'
FWD_PARAMS = ['x', 'p', 'ffn1_norm', 'ffn1_w_gate', 'ffn1_w_up', 'ffn1_w_down', 'mix_norm', 'ffn2_norm', 'ffn2_w_gate', 'ffn2_w_up', 'ffn2_w_down', 'ple_norm', 'ple_w_gate', 'ple_w_proj', 'ab_w_in', 'att_q_gain', 'att_k_gain', 'att_rel_bias', 'rwkv_mu', 'rwkv_w0', 'rwkv_w_up', 'rwkv_a0', 'rwkv_a_up', 'rwkv_g_up', 'rwkv_k_k', 'rwkv_k_a', 'rwkv_r_k', 'rwkv_lnx_w', 'rwkv_lnx_b', 'ab_w_out', 'ssm_w_in', 'ssm_lambda_re', 'ssm_lambda_im', 'ssm_log_dt', 'ssm_b_re', 'ssm_b_im', 'ssm_c_re', 'ssm_c_im', 'ssm_d', 'ssm_w_out']
TWIN_WEIGHTS = ['ffn1_norm', 'ffn1_w_gate', 'ffn1_w_up', 'ffn1_w_down', 'mix_norm', 'ffn2_norm', 'ffn2_w_gate', 'ffn2_w_up', 'ffn2_w_down', 'ple_norm', 'ple_w_gate', 'ple_w_proj', 'ab_w_in', 'att_q_gain', 'att_k_gain', 'att_rel_bias', 'rwkv_mu', 'rwkv_w0', 'rwkv_w_up', 'rwkv_a0', 'rwkv_a_up', 'rwkv_g_up', 'rwkv_k_k', 'rwkv_k_a', 'rwkv_r_k', 'rwkv_lnx_w', 'rwkv_lnx_b', 'ab_w_out', 'ssm_w_in', 'ssm_lambda_re', 'ssm_lambda_im', 'ssm_log_dt', 'ssm_b_re', 'ssm_b_im', 'ssm_c_re', 'ssm_c_im', 'ssm_d', 'ssm_w_out']
TWIN_DIFF_INPUT = 'x'
TWIN_INPUTS = ['x', 'p', 'ffn1_norm', 'ffn1_w_gate', 'ffn1_w_up', 'ffn1_w_down', 'mix_norm', 'ffn2_norm', 'ffn2_w_gate', 'ffn2_w_up', 'ffn2_w_down', 'ple_norm', 'ple_w_gate', 'ple_w_proj', 'ab_w_in', 'att_q_gain', 'att_k_gain', 'att_rel_bias', 'rwkv_mu', 'rwkv_w0', 'rwkv_w_up', 'rwkv_a0', 'rwkv_a_up', 'rwkv_g_up', 'rwkv_k_k', 'rwkv_k_a', 'rwkv_r_k', 'rwkv_lnx_w', 'rwkv_lnx_b', 'ab_w_out', 'ssm_w_in', 'ssm_lambda_re', 'ssm_lambda_im', 'ssm_log_dt', 'ssm_b_re', 'ssm_b_im', 'ssm_c_re', 'ssm_c_im', 'ssm_d', 'ssm_w_out', 'loss_target', 'm_ffn1_norm', 'm_ffn1_w_gate', 'm_ffn1_w_up', 'm_ffn1_w_down', 'm_mix_norm', 'm_ffn2_norm', 'm_ffn2_w_gate', 'm_ffn2_w_up', 'm_ffn2_w_down', 'm_ple_norm', 'm_ple_w_gate', 'm_ple_w_proj', 'm_ab_w_in', 'm_att_q_gain', 'm_att_k_gain', 'm_att_rel_bias', 'm_rwkv_mu', 'm_rwkv_w0', 'm_rwkv_w_up', 'm_rwkv_a0', 'm_rwkv_a_up', 'm_rwkv_g_up', 'm_rwkv_k_k', 'm_rwkv_k_a', 'm_rwkv_r_k', 'm_rwkv_lnx_w', 'm_rwkv_lnx_b', 'm_ab_w_out', 'm_ssm_w_in', 'm_ssm_lambda_re', 'm_ssm_lambda_im', 'm_ssm_log_dt', 'm_ssm_b_re', 'm_ssm_b_im', 'm_ssm_c_re', 'm_ssm_c_im', 'm_ssm_d', 'm_ssm_w_out', 'v_ffn1_norm', 'v_ffn1_w_gate', 'v_ffn1_w_up', 'v_ffn1_w_down', 'v_mix_norm', 'v_ffn2_norm', 'v_ffn2_w_gate', 'v_ffn2_w_up', 'v_ffn2_w_down', 'v_ple_norm', 'v_ple_w_gate', 'v_ple_w_proj', 'v_ab_w_in', 'v_att_q_gain', 'v_att_k_gain', 'v_att_rel_bias', 'v_rwkv_mu', 'v_rwkv_w0', 'v_rwkv_w_up', 'v_rwkv_a0', 'v_rwkv_a_up', 'v_rwkv_g_up', 'v_rwkv_k_k', 'v_rwkv_k_a', 'v_rwkv_r_k', 'v_rwkv_lnx_w', 'v_rwkv_lnx_b', 'v_ab_w_out', 'v_ssm_w_in', 'v_ssm_lambda_re', 'v_ssm_lambda_im', 'v_ssm_log_dt', 'v_ssm_b_re', 'v_ssm_b_im', 'v_ssm_c_re', 'v_ssm_c_im', 'v_ssm_d', 'v_ssm_w_out']
TWIN_OUTPUTS = ['loss', 'grad_x', 'grad_ffn1_norm', 'grad_ffn1_w_gate', 'grad_ffn1_w_up', 'grad_ffn1_w_down', 'grad_mix_norm', 'grad_ffn2_norm', 'grad_ffn2_w_gate', 'grad_ffn2_w_up', 'grad_ffn2_w_down', 'grad_ple_norm', 'grad_ple_w_gate', 'grad_ple_w_proj', 'grad_ab_w_in', 'grad_att_q_gain', 'grad_att_k_gain', 'grad_att_rel_bias', 'grad_rwkv_mu', 'grad_rwkv_w0', 'grad_rwkv_w_up', 'grad_rwkv_a0', 'grad_rwkv_a_up', 'grad_rwkv_g_up', 'grad_rwkv_k_k', 'grad_rwkv_k_a', 'grad_rwkv_r_k', 'grad_rwkv_lnx_w', 'grad_rwkv_lnx_b', 'grad_ab_w_out', 'grad_ssm_w_in', 'grad_ssm_lambda_re', 'grad_ssm_lambda_im', 'grad_ssm_log_dt', 'grad_ssm_b_re', 'grad_ssm_b_im', 'grad_ssm_c_re', 'grad_ssm_c_im', 'grad_ssm_d', 'grad_ssm_w_out', 'delta_ffn1_norm', 'delta_ffn1_w_gate', 'delta_ffn1_w_up', 'delta_ffn1_w_down', 'delta_mix_norm', 'delta_ffn2_norm', 'delta_ffn2_w_gate', 'delta_ffn2_w_up', 'delta_ffn2_w_down', 'delta_ple_norm', 'delta_ple_w_gate', 'delta_ple_w_proj', 'delta_ab_w_in', 'delta_att_q_gain', 'delta_att_k_gain', 'delta_att_rel_bias', 'delta_rwkv_mu', 'delta_rwkv_w0', 'delta_rwkv_w_up', 'delta_rwkv_a0', 'delta_rwkv_a_up', 'delta_rwkv_g_up', 'delta_rwkv_k_k', 'delta_rwkv_k_a', 'delta_rwkv_r_k', 'delta_rwkv_lnx_w', 'delta_rwkv_lnx_b', 'delta_ab_w_out', 'delta_ssm_w_in', 'delta_ssm_lambda_re', 'delta_ssm_lambda_im', 'delta_ssm_log_dt', 'delta_ssm_b_re', 'delta_ssm_b_im', 'delta_ssm_c_re', 'delta_ssm_c_im', 'delta_ssm_d', 'delta_ssm_w_out', 'new_m_ffn1_norm', 'new_m_ffn1_w_gate', 'new_m_ffn1_w_up', 'new_m_ffn1_w_down', 'new_m_mix_norm', 'new_m_ffn2_norm', 'new_m_ffn2_w_gate', 'new_m_ffn2_w_up', 'new_m_ffn2_w_down', 'new_m_ple_norm', 'new_m_ple_w_gate', 'new_m_ple_w_proj', 'new_m_ab_w_in', 'new_m_att_q_gain', 'new_m_att_k_gain', 'new_m_att_rel_bias', 'new_m_rwkv_mu', 'new_m_rwkv_w0', 'new_m_rwkv_w_up', 'new_m_rwkv_a0', 'new_m_rwkv_a_up', 'new_m_rwkv_g_up', 'new_m_rwkv_k_k', 'new_m_rwkv_k_a', 'new_m_rwkv_r_k', 'new_m_rwkv_lnx_w', 'new_m_rwkv_lnx_b', 'new_m_ab_w_out', 'new_m_ssm_w_in', 'new_m_ssm_lambda_re', 'new_m_ssm_lambda_im', 'new_m_ssm_log_dt', 'new_m_ssm_b_re', 'new_m_ssm_b_im', 'new_m_ssm_c_re', 'new_m_ssm_c_im', 'new_m_ssm_d', 'new_m_ssm_w_out', 'new_v_ffn1_norm', 'new_v_ffn1_w_gate', 'new_v_ffn1_w_up', 'new_v_ffn1_w_down', 'new_v_mix_norm', 'new_v_ffn2_norm', 'new_v_ffn2_w_gate', 'new_v_ffn2_w_up', 'new_v_ffn2_w_down', 'new_v_ple_norm', 'new_v_ple_w_gate', 'new_v_ple_w_proj', 'new_v_ab_w_in', 'new_v_att_q_gain', 'new_v_att_k_gain', 'new_v_att_rel_bias', 'new_v_rwkv_mu', 'new_v_rwkv_w0', 'new_v_rwkv_w_up', 'new_v_rwkv_a0', 'new_v_rwkv_a_up', 'new_v_rwkv_g_up', 'new_v_rwkv_k_k', 'new_v_rwkv_k_a', 'new_v_rwkv_r_k', 'new_v_rwkv_lnx_w', 'new_v_rwkv_lnx_b', 'new_v_ab_w_out', 'new_v_ssm_w_in', 'new_v_ssm_lambda_re', 'new_v_ssm_lambda_im', 'new_v_ssm_log_dt', 'new_v_ssm_b_re', 'new_v_ssm_b_im', 'new_v_ssm_c_re', 'new_v_ssm_c_im', 'new_v_ssm_d', 'new_v_ssm_w_out']
TWIN_LEAF_KINDS = {'loss': 'loss', 'grad_x': 'grad_x', 'grad_ffn1_norm': 'grad_w', 'grad_ffn1_w_gate': 'grad_w', 'grad_ffn1_w_up': 'grad_w', 'grad_ffn1_w_down': 'grad_w', 'grad_mix_norm': 'grad_w', 'grad_ffn2_norm': 'grad_w', 'grad_ffn2_w_gate': 'grad_w', 'grad_ffn2_w_up': 'grad_w', 'grad_ffn2_w_down': 'grad_w', 'grad_ple_norm': 'grad_w', 'grad_ple_w_gate': 'grad_w', 'grad_ple_w_proj': 'grad_w', 'grad_ab_w_in': 'grad_w', 'grad_att_q_gain': 'grad_w', 'grad_att_k_gain': 'grad_w', 'grad_att_rel_bias': 'grad_w', 'grad_rwkv_mu': 'grad_w', 'grad_rwkv_w0': 'grad_w', 'grad_rwkv_w_up': 'grad_w', 'grad_rwkv_a0': 'grad_w', 'grad_rwkv_a_up': 'grad_w', 'grad_rwkv_g_up': 'grad_w', 'grad_rwkv_k_k': 'grad_w', 'grad_rwkv_k_a': 'grad_w', 'grad_rwkv_r_k': 'grad_w', 'grad_rwkv_lnx_w': 'grad_w', 'grad_rwkv_lnx_b': 'grad_w', 'grad_ab_w_out': 'grad_w', 'grad_ssm_w_in': 'grad_w', 'grad_ssm_lambda_re': 'grad_w', 'grad_ssm_lambda_im': 'grad_w', 'grad_ssm_log_dt': 'grad_w', 'grad_ssm_b_re': 'grad_w', 'grad_ssm_b_im': 'grad_w', 'grad_ssm_c_re': 'grad_w', 'grad_ssm_c_im': 'grad_w', 'grad_ssm_d': 'grad_w', 'grad_ssm_w_out': 'grad_w', 'delta_ffn1_norm': 'delta_w', 'delta_ffn1_w_gate': 'delta_w', 'delta_ffn1_w_up': 'delta_w', 'delta_ffn1_w_down': 'delta_w', 'delta_mix_norm': 'delta_w', 'delta_ffn2_norm': 'delta_w', 'delta_ffn2_w_gate': 'delta_w', 'delta_ffn2_w_up': 'delta_w', 'delta_ffn2_w_down': 'delta_w', 'delta_ple_norm': 'delta_w', 'delta_ple_w_gate': 'delta_w', 'delta_ple_w_proj': 'delta_w', 'delta_ab_w_in': 'delta_w', 'delta_att_q_gain': 'delta_w', 'delta_att_k_gain': 'delta_w', 'delta_att_rel_bias': 'delta_w', 'delta_rwkv_mu': 'delta_w', 'delta_rwkv_w0': 'delta_w', 'delta_rwkv_w_up': 'delta_w', 'delta_rwkv_a0': 'delta_w', 'delta_rwkv_a_up': 'delta_w', 'delta_rwkv_g_up': 'delta_w', 'delta_rwkv_k_k': 'delta_w', 'delta_rwkv_k_a': 'delta_w', 'delta_rwkv_r_k': 'delta_w', 'delta_rwkv_lnx_w': 'delta_w', 'delta_rwkv_lnx_b': 'delta_w', 'delta_ab_w_out': 'delta_w', 'delta_ssm_w_in': 'delta_w', 'delta_ssm_lambda_re': 'delta_w', 'delta_ssm_lambda_im': 'delta_w', 'delta_ssm_log_dt': 'delta_w', 'delta_ssm_b_re': 'delta_w', 'delta_ssm_b_im': 'delta_w', 'delta_ssm_c_re': 'delta_w', 'delta_ssm_c_im': 'delta_w', 'delta_ssm_d': 'delta_w', 'delta_ssm_w_out': 'delta_w', 'new_m_ffn1_norm': 'new_m', 'new_m_ffn1_w_gate': 'new_m', 'new_m_ffn1_w_up': 'new_m', 'new_m_ffn1_w_down': 'new_m', 'new_m_mix_norm': 'new_m', 'new_m_ffn2_norm': 'new_m', 'new_m_ffn2_w_gate': 'new_m', 'new_m_ffn2_w_up': 'new_m', 'new_m_ffn2_w_down': 'new_m', 'new_m_ple_norm': 'new_m', 'new_m_ple_w_gate': 'new_m', 'new_m_ple_w_proj': 'new_m', 'new_m_ab_w_in': 'new_m', 'new_m_att_q_gain': 'new_m', 'new_m_att_k_gain': 'new_m', 'new_m_att_rel_bias': 'new_m', 'new_m_rwkv_mu': 'new_m', 'new_m_rwkv_w0': 'new_m', 'new_m_rwkv_w_up': 'new_m', 'new_m_rwkv_a0': 'new_m', 'new_m_rwkv_a_up': 'new_m', 'new_m_rwkv_g_up': 'new_m', 'new_m_rwkv_k_k': 'new_m', 'new_m_rwkv_k_a': 'new_m', 'new_m_rwkv_r_k': 'new_m', 'new_m_rwkv_lnx_w': 'new_m', 'new_m_rwkv_lnx_b': 'new_m', 'new_m_ab_w_out': 'new_m', 'new_m_ssm_w_in': 'new_m', 'new_m_ssm_lambda_re': 'new_m', 'new_m_ssm_lambda_im': 'new_m', 'new_m_ssm_log_dt': 'new_m', 'new_m_ssm_b_re': 'new_m', 'new_m_ssm_b_im': 'new_m', 'new_m_ssm_c_re': 'new_m', 'new_m_ssm_c_im': 'new_m', 'new_m_ssm_d': 'new_m', 'new_m_ssm_w_out': 'new_m', 'new_v_ffn1_norm': 'new_v', 'new_v_ffn1_w_gate': 'new_v', 'new_v_ffn1_w_up': 'new_v', 'new_v_ffn1_w_down': 'new_v', 'new_v_mix_norm': 'new_v', 'new_v_ffn2_norm': 'new_v', 'new_v_ffn2_w_gate': 'new_v', 'new_v_ffn2_w_up': 'new_v', 'new_v_ffn2_w_down': 'new_v', 'new_v_ple_norm': 'new_v', 'new_v_ple_w_gate': 'new_v', 'new_v_ple_w_proj': 'new_v', 'new_v_ab_w_in': 'new_v', 'new_v_att_q_gain': 'new_v', 'new_v_att_k_gain': 'new_v', 'new_v_att_rel_bias': 'new_v', 'new_v_rwkv_mu': 'new_v', 'new_v_rwkv_w0': 'new_v', 'new_v_rwkv_w_up': 'new_v', 'new_v_rwkv_a0': 'new_v', 'new_v_rwkv_a_up': 'new_v', 'new_v_rwkv_g_up': 'new_v', 'new_v_rwkv_k_k': 'new_v', 'new_v_rwkv_k_a': 'new_v', 'new_v_rwkv_r_k': 'new_v', 'new_v_rwkv_lnx_w': 'new_v', 'new_v_rwkv_lnx_b': 'new_v', 'new_v_ab_w_out': 'new_v', 'new_v_ssm_w_in': 'new_v', 'new_v_ssm_lambda_re': 'new_v', 'new_v_ssm_lambda_im': 'new_v', 'new_v_ssm_log_dt': 'new_v', 'new_v_ssm_b_re': 'new_v', 'new_v_ssm_b_im': 'new_v', 'new_v_ssm_c_re': 'new_v', 'new_v_ssm_c_im': 'new_v', 'new_v_ssm_d': 'new_v', 'new_v_ssm_w_out': 'new_v'}


def _forward(args):
    return _fwd_reference(*[args[k] for k in FWD_PARAMS])


def _output_shape():
    def fwd():
        inp = _fwd_setup_inputs(0)
        return _fwd_reference(*[inp[k] for k in FWD_PARAMS])
    out = _jax.eval_shape(fwd)
    return out.shape, out.dtype

N_MICROBATCH = 1
ADAM_LR = 0.001
ADAM_B1 = 0.9
ADAM_B2 = 0.999
ADAM_EPS = 1e-08
ADAM_WD = 0.01
ADAM_STEP = 10
PER_EXAMPLE_BATCH_AXIS = {'x': 0, 'p': 1, 'loss_target': 0}
SHARED_INPUTS = []
_WEIGHT_DTYPES = {'ffn1_norm': _jnp.float32, 'ffn1_w_gate': _jnp.float32, 'ffn1_w_up': _jnp.float32, 'ffn1_w_down': _jnp.float32, 'mix_norm': _jnp.float32, 'ffn2_norm': _jnp.float32, 'ffn2_w_gate': _jnp.float32, 'ffn2_w_up': _jnp.float32, 'ffn2_w_down': _jnp.float32, 'ple_norm': _jnp.float32, 'ple_w_gate': _jnp.float32, 'ple_w_proj': _jnp.float32, 'ab_w_in': _jnp.float32, 'att_q_gain': _jnp.float32, 'att_k_gain': _jnp.float32, 'att_rel_bias': _jnp.float32, 'rwkv_mu': _jnp.float32, 'rwkv_w0': _jnp.float32, 'rwkv_w_up': _jnp.float32, 'rwkv_a0': _jnp.float32, 'rwkv_a_up': _jnp.float32, 'rwkv_g_up': _jnp.float32, 'rwkv_k_k': _jnp.float32, 'rwkv_k_a': _jnp.float32, 'rwkv_r_k': _jnp.float32, 'rwkv_lnx_w': _jnp.float32, 'rwkv_lnx_b': _jnp.float32, 'ab_w_out': _jnp.float32, 'ssm_w_in': _jnp.float32, 'ssm_lambda_re': _jnp.float32, 'ssm_lambda_im': _jnp.float32, 'ssm_log_dt': _jnp.float32, 'ssm_b_re': _jnp.float32, 'ssm_b_im': _jnp.float32, 'ssm_c_re': _jnp.float32, 'ssm_c_im': _jnp.float32, 'ssm_d': _jnp.float32, 'ssm_w_out': _jnp.float32}
MOMENT_SCALE = {'ffn1_norm': 3.083402e+00, 'ffn1_w_gate': 4.574734e-02, 'ffn1_w_up': 4.859903e-02, 'ffn1_w_down': 7.977053e-02, 'mix_norm': 1.535639e+00, 'ffn2_norm': 3.078810e+00, 'ffn2_w_gate': 4.407754e-02, 'ffn2_w_up': 4.328754e-02, 'ffn2_w_down': 7.066038e-02, 'ple_norm': 5.036998e-01, 'ple_w_gate': 1.139483e-01, 'ple_w_proj': 2.616517e-01, 'ab_w_in': 1.050201e-01, 'att_q_gain': 1.083002e+00, 'att_k_gain': 1.082993e+00, 'att_rel_bias': 1.752185e-02, 'rwkv_mu': 2.350279e+00, 'rwkv_w0': 6.741685e-02, 'rwkv_w_up': 9.291413e-03, 'rwkv_a0': 3.038799e-01, 'rwkv_a_up': 5.223467e-02, 'rwkv_g_up': 4.412150e+00, 'rwkv_k_k': 1.418699e-01, 'rwkv_k_a': 6.113976e-01, 'rwkv_r_k': 3.547327e+00, 'rwkv_lnx_w': 7.545203e+00, 'rwkv_lnx_b': 2.472403e+00, 'ab_w_out': 1.332293e-01, 'ssm_w_in': 2.367147e-01, 'ssm_lambda_re': 1.145236e-02, 'ssm_lambda_im': 7.623750e-03, 'ssm_log_dt': 4.235319e+00, 'ssm_b_re': 7.324898e-03, 'ssm_b_im': 6.905895e-03, 'ssm_c_re': 1.419755e-02, 'ssm_c_im': 1.461579e-02, 'ssm_d': 4.489553e+00, 'ssm_w_out': 3.911980e-01}


def _to_microbatches(a, axis):
    t = _jnp.moveaxis(a, axis, 0)
    t = t.reshape((N_MICROBATCH, t.shape[0] // N_MICROBATCH) + t.shape[1:])
    return _jnp.moveaxis(t, 1, axis + 1)


def setup_inputs(seed: int = 0) -> dict:
    inp = _fwd_setup_inputs(seed)
    key = _jax.random.fold_in(_jax.random.key(seed), 7919)
    shape, _ = _output_shape()
    out = dict(inp)
    out["loss_target"] = _jax.random.normal(_jax.random.fold_in(key, 0), shape, _jnp.float32)
    for i, name in enumerate(TWIN_WEIGHTS):
        w = inp[name].astype(_jnp.float32)
        if MOMENT_SCALE is None:
            s = _jnp.sqrt(_jnp.mean(_jnp.square(w)) + 1e-30)
        else:
            s = MOMENT_SCALE[name]
        km, kv = _jax.random.split(_jax.random.fold_in(key, i + 1))
        out[name] = w
        out["m_" + name] = s * _jax.random.normal(km, w.shape, _jnp.float32)
        out["v_" + name] = (s * s) * _jax.random.uniform(kv, w.shape, _jnp.float32, 0.5, 1.5)
    if N_MICROBATCH > 1:
        for name, axis in PER_EXAMPLE_BATCH_AXIS.items():
            out[name] = _to_microbatches(out[name], axis)
    return {'x': out['x'], 'p': out['p'], 'ffn1_norm': out['ffn1_norm'], 'ffn1_w_gate': out['ffn1_w_gate'], 'ffn1_w_up': out['ffn1_w_up'], 'ffn1_w_down': out['ffn1_w_down'], 'mix_norm': out['mix_norm'], 'ffn2_norm': out['ffn2_norm'], 'ffn2_w_gate': out['ffn2_w_gate'], 'ffn2_w_up': out['ffn2_w_up'], 'ffn2_w_down': out['ffn2_w_down'], 'ple_norm': out['ple_norm'], 'ple_w_gate': out['ple_w_gate'], 'ple_w_proj': out['ple_w_proj'], 'ab_w_in': out['ab_w_in'], 'att_q_gain': out['att_q_gain'], 'att_k_gain': out['att_k_gain'], 'att_rel_bias': out['att_rel_bias'], 'rwkv_mu': out['rwkv_mu'], 'rwkv_w0': out['rwkv_w0'], 'rwkv_w_up': out['rwkv_w_up'], 'rwkv_a0': out['rwkv_a0'], 'rwkv_a_up': out['rwkv_a_up'], 'rwkv_g_up': out['rwkv_g_up'], 'rwkv_k_k': out['rwkv_k_k'], 'rwkv_k_a': out['rwkv_k_a'], 'rwkv_r_k': out['rwkv_r_k'], 'rwkv_lnx_w': out['rwkv_lnx_w'], 'rwkv_lnx_b': out['rwkv_lnx_b'], 'ab_w_out': out['ab_w_out'], 'ssm_w_in': out['ssm_w_in'], 'ssm_lambda_re': out['ssm_lambda_re'], 'ssm_lambda_im': out['ssm_lambda_im'], 'ssm_log_dt': out['ssm_log_dt'], 'ssm_b_re': out['ssm_b_re'], 'ssm_b_im': out['ssm_b_im'], 'ssm_c_re': out['ssm_c_re'], 'ssm_c_im': out['ssm_c_im'], 'ssm_d': out['ssm_d'], 'ssm_w_out': out['ssm_w_out'], 'loss_target': out['loss_target'], 'm_ffn1_norm': out['m_ffn1_norm'], 'm_ffn1_w_gate': out['m_ffn1_w_gate'], 'm_ffn1_w_up': out['m_ffn1_w_up'], 'm_ffn1_w_down': out['m_ffn1_w_down'], 'm_mix_norm': out['m_mix_norm'], 'm_ffn2_norm': out['m_ffn2_norm'], 'm_ffn2_w_gate': out['m_ffn2_w_gate'], 'm_ffn2_w_up': out['m_ffn2_w_up'], 'm_ffn2_w_down': out['m_ffn2_w_down'], 'm_ple_norm': out['m_ple_norm'], 'm_ple_w_gate': out['m_ple_w_gate'], 'm_ple_w_proj': out['m_ple_w_proj'], 'm_ab_w_in': out['m_ab_w_in'], 'm_att_q_gain': out['m_att_q_gain'], 'm_att_k_gain': out['m_att_k_gain'], 'm_att_rel_bias': out['m_att_rel_bias'], 'm_rwkv_mu': out['m_rwkv_mu'], 'm_rwkv_w0': out['m_rwkv_w0'], 'm_rwkv_w_up': out['m_rwkv_w_up'], 'm_rwkv_a0': out['m_rwkv_a0'], 'm_rwkv_a_up': out['m_rwkv_a_up'], 'm_rwkv_g_up': out['m_rwkv_g_up'], 'm_rwkv_k_k': out['m_rwkv_k_k'], 'm_rwkv_k_a': out['m_rwkv_k_a'], 'm_rwkv_r_k': out['m_rwkv_r_k'], 'm_rwkv_lnx_w': out['m_rwkv_lnx_w'], 'm_rwkv_lnx_b': out['m_rwkv_lnx_b'], 'm_ab_w_out': out['m_ab_w_out'], 'm_ssm_w_in': out['m_ssm_w_in'], 'm_ssm_lambda_re': out['m_ssm_lambda_re'], 'm_ssm_lambda_im': out['m_ssm_lambda_im'], 'm_ssm_log_dt': out['m_ssm_log_dt'], 'm_ssm_b_re': out['m_ssm_b_re'], 'm_ssm_b_im': out['m_ssm_b_im'], 'm_ssm_c_re': out['m_ssm_c_re'], 'm_ssm_c_im': out['m_ssm_c_im'], 'm_ssm_d': out['m_ssm_d'], 'm_ssm_w_out': out['m_ssm_w_out'], 'v_ffn1_norm': out['v_ffn1_norm'], 'v_ffn1_w_gate': out['v_ffn1_w_gate'], 'v_ffn1_w_up': out['v_ffn1_w_up'], 'v_ffn1_w_down': out['v_ffn1_w_down'], 'v_mix_norm': out['v_mix_norm'], 'v_ffn2_norm': out['v_ffn2_norm'], 'v_ffn2_w_gate': out['v_ffn2_w_gate'], 'v_ffn2_w_up': out['v_ffn2_w_up'], 'v_ffn2_w_down': out['v_ffn2_w_down'], 'v_ple_norm': out['v_ple_norm'], 'v_ple_w_gate': out['v_ple_w_gate'], 'v_ple_w_proj': out['v_ple_w_proj'], 'v_ab_w_in': out['v_ab_w_in'], 'v_att_q_gain': out['v_att_q_gain'], 'v_att_k_gain': out['v_att_k_gain'], 'v_att_rel_bias': out['v_att_rel_bias'], 'v_rwkv_mu': out['v_rwkv_mu'], 'v_rwkv_w0': out['v_rwkv_w0'], 'v_rwkv_w_up': out['v_rwkv_w_up'], 'v_rwkv_a0': out['v_rwkv_a0'], 'v_rwkv_a_up': out['v_rwkv_a_up'], 'v_rwkv_g_up': out['v_rwkv_g_up'], 'v_rwkv_k_k': out['v_rwkv_k_k'], 'v_rwkv_k_a': out['v_rwkv_k_a'], 'v_rwkv_r_k': out['v_rwkv_r_k'], 'v_rwkv_lnx_w': out['v_rwkv_lnx_w'], 'v_rwkv_lnx_b': out['v_rwkv_lnx_b'], 'v_ab_w_out': out['v_ab_w_out'], 'v_ssm_w_in': out['v_ssm_w_in'], 'v_ssm_lambda_re': out['v_ssm_lambda_re'], 'v_ssm_lambda_im': out['v_ssm_lambda_im'], 'v_ssm_log_dt': out['v_ssm_log_dt'], 'v_ssm_b_re': out['v_ssm_b_re'], 'v_ssm_b_im': out['v_ssm_b_im'], 'v_ssm_c_re': out['v_ssm_c_re'], 'v_ssm_c_im': out['v_ssm_c_im'], 'v_ssm_d': out['v_ssm_d'], 'v_ssm_w_out': out['v_ssm_w_out']}


def _loss(weights, diff, rest, loss_target):
    with _jax.named_scope("forward"):
        args = {**rest, TWIN_DIFF_INPUT: diff, **{k: w.astype(_WEIGHT_DTYPES[k]) for k, w in weights.items()}}
        y = _forward(args)
    with _jax.named_scope("loss_head"):
        err = _jnp.square(y.astype(_jnp.float32) - loss_target)
        return 0.5 * _jnp.sum(_jnp.mean(err, axis=-1)) if err.ndim else 0.5 * err


def _adamw(w, g, m, v):
    m = ADAM_B1 * m + (1.0 - ADAM_B1) * g
    v = ADAM_B2 * v + (1.0 - ADAM_B2) * _jnp.square(g)
    m_hat = m / (1.0 - ADAM_B1 ** ADAM_STEP)
    v_hat = v / (1.0 - ADAM_B2 ** ADAM_STEP)
    delta = -ADAM_LR * (m_hat / (_jnp.sqrt(v_hat) + ADAM_EPS) + ADAM_WD * w)
    return delta, m, v


def reference(x, p, ffn1_norm, ffn1_w_gate, ffn1_w_up, ffn1_w_down, mix_norm, ffn2_norm, ffn2_w_gate, ffn2_w_up, ffn2_w_down, ple_norm, ple_w_gate, ple_w_proj, ab_w_in, att_q_gain, att_k_gain, att_rel_bias, rwkv_mu, rwkv_w0, rwkv_w_up, rwkv_a0, rwkv_a_up, rwkv_g_up, rwkv_k_k, rwkv_k_a, rwkv_r_k, rwkv_lnx_w, rwkv_lnx_b, ab_w_out, ssm_w_in, ssm_lambda_re, ssm_lambda_im, ssm_log_dt, ssm_b_re, ssm_b_im, ssm_c_re, ssm_c_im, ssm_d, ssm_w_out, loss_target, m_ffn1_norm, m_ffn1_w_gate, m_ffn1_w_up, m_ffn1_w_down, m_mix_norm, m_ffn2_norm, m_ffn2_w_gate, m_ffn2_w_up, m_ffn2_w_down, m_ple_norm, m_ple_w_gate, m_ple_w_proj, m_ab_w_in, m_att_q_gain, m_att_k_gain, m_att_rel_bias, m_rwkv_mu, m_rwkv_w0, m_rwkv_w_up, m_rwkv_a0, m_rwkv_a_up, m_rwkv_g_up, m_rwkv_k_k, m_rwkv_k_a, m_rwkv_r_k, m_rwkv_lnx_w, m_rwkv_lnx_b, m_ab_w_out, m_ssm_w_in, m_ssm_lambda_re, m_ssm_lambda_im, m_ssm_log_dt, m_ssm_b_re, m_ssm_b_im, m_ssm_c_re, m_ssm_c_im, m_ssm_d, m_ssm_w_out, v_ffn1_norm, v_ffn1_w_gate, v_ffn1_w_up, v_ffn1_w_down, v_mix_norm, v_ffn2_norm, v_ffn2_w_gate, v_ffn2_w_up, v_ffn2_w_down, v_ple_norm, v_ple_w_gate, v_ple_w_proj, v_ab_w_in, v_att_q_gain, v_att_k_gain, v_att_rel_bias, v_rwkv_mu, v_rwkv_w0, v_rwkv_w_up, v_rwkv_a0, v_rwkv_a_up, v_rwkv_g_up, v_rwkv_k_k, v_rwkv_k_a, v_rwkv_r_k, v_rwkv_lnx_w, v_rwkv_lnx_b, v_ab_w_out, v_ssm_w_in, v_ssm_lambda_re, v_ssm_lambda_im, v_ssm_log_dt, v_ssm_b_re, v_ssm_b_im, v_ssm_c_re, v_ssm_c_im, v_ssm_d, v_ssm_w_out):
    given = dict(x=x, p=p, ffn1_norm=ffn1_norm, ffn1_w_gate=ffn1_w_gate, ffn1_w_up=ffn1_w_up, ffn1_w_down=ffn1_w_down, mix_norm=mix_norm, ffn2_norm=ffn2_norm, ffn2_w_gate=ffn2_w_gate, ffn2_w_up=ffn2_w_up, ffn2_w_down=ffn2_w_down, ple_norm=ple_norm, ple_w_gate=ple_w_gate, ple_w_proj=ple_w_proj, ab_w_in=ab_w_in, att_q_gain=att_q_gain, att_k_gain=att_k_gain, att_rel_bias=att_rel_bias, rwkv_mu=rwkv_mu, rwkv_w0=rwkv_w0, rwkv_w_up=rwkv_w_up, rwkv_a0=rwkv_a0, rwkv_a_up=rwkv_a_up, rwkv_g_up=rwkv_g_up, rwkv_k_k=rwkv_k_k, rwkv_k_a=rwkv_k_a, rwkv_r_k=rwkv_r_k, rwkv_lnx_w=rwkv_lnx_w, rwkv_lnx_b=rwkv_lnx_b, ab_w_out=ab_w_out, ssm_w_in=ssm_w_in, ssm_lambda_re=ssm_lambda_re, ssm_lambda_im=ssm_lambda_im, ssm_log_dt=ssm_log_dt, ssm_b_re=ssm_b_re, ssm_b_im=ssm_b_im, ssm_c_re=ssm_c_re, ssm_c_im=ssm_c_im, ssm_d=ssm_d, ssm_w_out=ssm_w_out, loss_target=loss_target, m_ffn1_norm=m_ffn1_norm, m_ffn1_w_gate=m_ffn1_w_gate, m_ffn1_w_up=m_ffn1_w_up, m_ffn1_w_down=m_ffn1_w_down, m_mix_norm=m_mix_norm, m_ffn2_norm=m_ffn2_norm, m_ffn2_w_gate=m_ffn2_w_gate, m_ffn2_w_up=m_ffn2_w_up, m_ffn2_w_down=m_ffn2_w_down, m_ple_norm=m_ple_norm, m_ple_w_gate=m_ple_w_gate, m_ple_w_proj=m_ple_w_proj, m_ab_w_in=m_ab_w_in, m_att_q_gain=m_att_q_gain, m_att_k_gain=m_att_k_gain, m_att_rel_bias=m_att_rel_bias, m_rwkv_mu=m_rwkv_mu, m_rwkv_w0=m_rwkv_w0, m_rwkv_w_up=m_rwkv_w_up, m_rwkv_a0=m_rwkv_a0, m_rwkv_a_up=m_rwkv_a_up, m_rwkv_g_up=m_rwkv_g_up, m_rwkv_k_k=m_rwkv_k_k, m_rwkv_k_a=m_rwkv_k_a, m_rwkv_r_k=m_rwkv_r_k, m_rwkv_lnx_w=m_rwkv_lnx_w, m_rwkv_lnx_b=m_rwkv_lnx_b, m_ab_w_out=m_ab_w_out, m_ssm_w_in=m_ssm_w_in, m_ssm_lambda_re=m_ssm_lambda_re, m_ssm_lambda_im=m_ssm_lambda_im, m_ssm_log_dt=m_ssm_log_dt, m_ssm_b_re=m_ssm_b_re, m_ssm_b_im=m_ssm_b_im, m_ssm_c_re=m_ssm_c_re, m_ssm_c_im=m_ssm_c_im, m_ssm_d=m_ssm_d, m_ssm_w_out=m_ssm_w_out, v_ffn1_norm=v_ffn1_norm, v_ffn1_w_gate=v_ffn1_w_gate, v_ffn1_w_up=v_ffn1_w_up, v_ffn1_w_down=v_ffn1_w_down, v_mix_norm=v_mix_norm, v_ffn2_norm=v_ffn2_norm, v_ffn2_w_gate=v_ffn2_w_gate, v_ffn2_w_up=v_ffn2_w_up, v_ffn2_w_down=v_ffn2_w_down, v_ple_norm=v_ple_norm, v_ple_w_gate=v_ple_w_gate, v_ple_w_proj=v_ple_w_proj, v_ab_w_in=v_ab_w_in, v_att_q_gain=v_att_q_gain, v_att_k_gain=v_att_k_gain, v_att_rel_bias=v_att_rel_bias, v_rwkv_mu=v_rwkv_mu, v_rwkv_w0=v_rwkv_w0, v_rwkv_w_up=v_rwkv_w_up, v_rwkv_a0=v_rwkv_a0, v_rwkv_a_up=v_rwkv_a_up, v_rwkv_g_up=v_rwkv_g_up, v_rwkv_k_k=v_rwkv_k_k, v_rwkv_k_a=v_rwkv_k_a, v_rwkv_r_k=v_rwkv_r_k, v_rwkv_lnx_w=v_rwkv_lnx_w, v_rwkv_lnx_b=v_rwkv_lnx_b, v_ab_w_out=v_ab_w_out, v_ssm_w_in=v_ssm_w_in, v_ssm_lambda_re=v_ssm_lambda_re, v_ssm_lambda_im=v_ssm_lambda_im, v_ssm_log_dt=v_ssm_log_dt, v_ssm_b_re=v_ssm_b_re, v_ssm_b_im=v_ssm_b_im, v_ssm_c_re=v_ssm_c_re, v_ssm_c_im=v_ssm_c_im, v_ssm_d=v_ssm_d, v_ssm_w_out=v_ssm_w_out)
    weights = {n: given[n] for n in TWIN_WEIGHTS}
    shared = {n: given[n] for n in SHARED_INPUTS}
    per_example = {n: given[n] for n in ['x', 'p']}
    grad_fn = _jax.value_and_grad(_loss, argnums=(0, 1))

    def one_microbatch(ex, loss_target):
        ex = dict(ex)
        diff = ex.pop(TWIN_DIFF_INPUT)
        return grad_fn(weights, diff, {**shared, **ex}, loss_target)

    if N_MICROBATCH == 1:
        loss, (grad_w, grad_x) = one_microbatch(per_example, given["loss_target"])
    else:
        def body(carry, xs):
            loss_sum, grad_sum = carry
            l_k, (gw_k, gx_k) = one_microbatch(xs[0], xs[1])
            with _jax.named_scope("update"):
                return (loss_sum + l_k, _jax.tree.map(_jnp.add, grad_sum, gw_k)), gx_k

        init = (_jnp.zeros((), _jnp.float32), _jax.tree.map(_jnp.zeros_like, weights))
        (loss, grad_w), grad_x = _jax.lax.scan(body, init, (per_example, given["loss_target"]))
    with _jax.named_scope("update"):
        delta_w, new_m, new_v = {}, {}, {}
        for n in TWIN_WEIGHTS:
            delta_w[n], new_m[n], new_v[n] = _adamw(weights[n], grad_w[n], given["m_" + n], given["v_" + n])
    return (loss, grad_x, *[grad_w[n] for n in TWIN_WEIGHTS], *[delta_w[n] for n in TWIN_WEIGHTS],
            *[new_m[n] for n in TWIN_WEIGHTS], *[new_v[n] for n in TWIN_WEIGHTS])
```

```python
import functools
import math

import jax
import jax.numpy as jnp
import numpy as np
from jax import lax
from jax.experimental import pallas as pl
from jax.experimental.pallas import tpu as pltpu

F32 = jnp.float32
BF16 = jnp.bfloat16
HIGHEST = lax.Precision.HIGHEST
MESH_AXES = ("x", "y", "c")
N_DEV = 8

CHUNK = 64
N_LEFT_CHUNKS = 8
BAND = (N_LEFT_CHUNKS + 1) * CHUNK
HEAD_DIM = 64
REL_CLIP = 128
N_REL = (CHUNK - 1) + REL_CLIP + 1
DECAY_LORA = 64
AAA_LORA = 64
GATE_LORA = 128
SSM_GROUP = 16
SSM_STATE = 64
RMS_EPS = 1e-6
GN_EPS = 64e-5
ADAM_LR = 0.001
ADAM_B1 = 0.9
ADAM_B2 = 0.999
ADAM_EPS = 1e-08
ADAM_WD = 0.01
ADAM_STEP = 10

RWKV_CHUNK = 64
VMEM_LIMIT = 56 * 1024 * 1024
LANES = 128


def _params(semantics):
    return pltpu.CompilerParams(dimension_semantics=semantics, vmem_limit_bytes=VMEM_LIMIT)


def _pick(n, prefs):
    for t in prefs:
        if n % t == 0:
            return t
    return n


def _dot(a, b, dims):
    return lax.dot_general(a, b, (dims, ((), ())), precision=HIGHEST, preferred_element_type=F32)


def _mm(a, b):
    return _dot(a, b, ((1,), (0,)))


def _mm_nt(a, b):
    return _dot(a, b, ((1,), (1,)))


def _mm_tn(a, b):
    return _dot(a, b, ((0,), (0,)))


def _sigmoid(x):
    return 1.0 / (1.0 + jnp.exp(-x))


def _softplus(x):
    return jnp.maximum(x, 0.0) + jnp.log(1.0 + jnp.exp(-jnp.abs(x)))


def _gelu_tanh(x):
    return 0.5 * x * (1.0 + jnp.tanh(math.sqrt(2.0 / math.pi) * (x + 0.044715 * (x * x * x))))


def _seg_indicator(n, seg):
    r = lax.broadcasted_iota(jnp.int32, (n, n // seg), 0)
    c = lax.broadcasted_iota(jnp.int32, (n, n // seg), 1)
    return jnp.where((r >= c * seg) & (r < (c + 1) * seg), 1.0, 0.0).astype(F32)


def _seg_indicator_t(n, seg):
    c = lax.broadcasted_iota(jnp.int32, (n // seg, n), 0)
    r = lax.broadcasted_iota(jnp.int32, (n // seg, n), 1)
    return jnp.where((r >= c * seg) & (r < (c + 1) * seg), 1.0, 0.0).astype(F32)


def _seg_sum(x, seg):
    return _mm(x, _seg_indicator(x.shape[1], seg))


def _seg_expand(s, seg):
    return _mm(s, _seg_indicator_t(s.shape[1] * seg, seg))


def _tile_lanes(g, reps):
    n = g.shape[1]
    r = lax.broadcasted_iota(jnp.int32, (n, n * reps), 0)
    c = lax.broadcasted_iota(jnp.int32, (n, n * reps), 1)
    return _mm(g, jnp.where((c & (n - 1)) == r, 1.0, 0.0).astype(F32))


def _rms(x, g):
    return x * lax.rsqrt(jnp.mean(x * x, axis=-1, keepdims=True) + RMS_EPS) * g


def matmul(name, a, b, *, ta=False, tb=False, alpha=1.0, res=None, out_dtype=F32):
    m, k = (a.shape[1], a.shape[0]) if ta else a.shape
    n = b.shape[0] if tb else b.shape[1]
    assert k == (b.shape[1] if tb else b.shape[0]), (name, a.shape, b.shape)
    tm = _pick(m, (512, 640, 384, 256, 128))
    tn = _pick(n, (512, 640, 384, 256, 128))
    tk = _pick(k, (2048, 1024, 512, 640, 384, 256, 128))
    nk = k // tk
    dims = ((0 if ta else 1,), (1 if tb else 0,))

    def body(*refs):
        if res is None:
            a_ref, b_ref, o_ref, acc_ref = refs
            res_ref = None
        else:
            a_ref, b_ref, res_ref, o_ref, acc_ref = refs
        kk = pl.program_id(2)

        @pl.when(kk == 0)
        def _():
            acc_ref[...] = jnp.zeros_like(acc_ref)

        acc_ref[...] += lax.dot_general(a_ref[...].astype(BF16), b_ref[...].astype(BF16), (dims, ((), ())),
                                        preferred_element_type=F32)

        @pl.when(kk == nk - 1)
        def _():
            out = acc_ref[...] * alpha
            if res_ref is not None:
                out = out + res_ref[...].astype(F32)
            o_ref[...] = out.astype(o_ref.dtype)

    a_spec = pl.BlockSpec((tk, tm), lambda i, j, kk: (kk, i)) if ta else pl.BlockSpec((tm, tk), lambda i, j, kk: (i, kk))
    b_spec = pl.BlockSpec((tn, tk), lambda i, j, kk: (j, kk)) if tb else pl.BlockSpec((tk, tn), lambda i, j, kk: (kk, j))
    o_spec = pl.BlockSpec((tm, tn), lambda i, j, kk: (i, j))
    in_specs = [a_spec, b_spec] + ([o_spec] if res is not None else [])
    args = (a, b) + ((res,) if res is not None else ())
    return pl.pallas_call(
        body, name=name, grid=(m // tm, n // tn, nk), in_specs=in_specs, out_specs=o_spec,
        out_shape=jax.ShapeDtypeStruct((m, n), out_dtype), scratch_shapes=[pltpu.VMEM((tm, tn), F32)],
        compiler_params=_params(("parallel", "parallel", "arbitrary")),
    )(*args)


def _row_specs(arrays, tr):
    return [pl.BlockSpec((tr, a.shape[1]), lambda i: (i, 0)) for a in arrays]


def _whole_specs(arrays):
    return [pl.BlockSpec(a.shape, lambda i: (0, 0)) for a in arrays]


def rowwise(name, fn, rows, consts, outs, tr):
    t = rows[0].shape[0]
    nr, nc = len(rows), len(consts)

    def body(*refs):
        vals = [r[...] for r in refs[:nr + nc]]
        res = fn(*vals)
        for o_ref, o in zip(refs[nr + nc:], res):
            o_ref[...] = o.astype(o_ref.dtype)

    out_shape = [jax.ShapeDtypeStruct((t, w), d) for w, d in outs]
    return pl.pallas_call(
        body, name=name, grid=(t // tr,), in_specs=_row_specs(rows, tr) + _whole_specs(consts),
        out_specs=_row_specs(out_shape, tr), out_shape=out_shape, compiler_params=_params(("parallel",)),
    )(*rows, *consts)


def rowwise_vjp(name, fn, rows, consts, cots, tr, row_grad, const_grad, add_to=None, grad_dtypes=None):
    t = rows[0].shape[0]
    nr, nc = len(rows), len(consts)
    cot_groups = [c if isinstance(c, (tuple, list)) else (c,) for c in cots]
    flat_cots = [a for g in cot_groups for a in g]
    add_to = {i: (a if isinstance(a, (tuple, list)) else (a,)) for i, a in (add_to or {}).items()}
    add_idx = [(i, q) for i in sorted(add_to) for q in range(len(add_to[i]))]
    add_arrays = [add_to[i][q] for i, q in add_idx]
    r_idx = [i for i in range(nr) if row_grad[i]]
    c_idx = [i for i in range(nc) if const_grad[i]]
    grad_dtypes = grad_dtypes or {}
    n_in = nr + nc + len(flat_cots) + len(add_arrays)

    def body(*refs):
        vals = [r[...] for r in refs[:nr + nc]]
        pos = nr + nc
        cts = []
        for g in cot_groups:
            s = refs[pos][...].astype(F32)
            for q in range(1, len(g)):
                s = s + refs[pos + q][...].astype(F32)
            cts.append(s)
            pos += len(g)
        adds = {}
        for n_add, (i, _) in enumerate(add_idx):
            term = refs[pos + n_add][...].astype(F32)
            adds[i] = adds[i] + term if i in adds else term
        diff =[vals[i] for i in r_idx] + [vals[nr + i] for i in c_idx]

        def f(*d):
            full = list(vals)
            for q, i in enumerate(r_idx):
                full[i] = d[q]
            for q, i in enumerate(c_idx):
                full[nr + i] = d[len(r_idx) + q]
            return tuple(fn(*full))

        prim, vjp = jax.vjp(f, *diff)
        grads = vjp(tuple(c.astype(p.dtype) for c, p in zip(cts, prim)))
        o_refs = refs[n_in:]
        for q, i in enumerate(r_idx):
            g = grads[q].astype(F32)
            if i in adds:
                g = g + adds[i].astype(F32)
            o_refs[q][...] = g.astype(o_refs[q].dtype)
        step = pl.program_id(0)
        for q, i in enumerate(c_idx):
            o_ref = o_refs[len(r_idx) + q]

            @pl.when(step == 0)
            def _(o_ref=o_ref):
                o_ref[...] = jnp.zeros_like(o_ref)

            o_ref[...] += grads[len(r_idx) + q].astype(F32)

    row_out = [jax.ShapeDtypeStruct(rows[i].shape, grad_dtypes.get(i, F32)) for i in r_idx]
    const_out = [jax.ShapeDtypeStruct(consts[i].shape, F32) for i in c_idx]
    outs = pl.pallas_call(
        body, name=name, grid=(t // tr,),
        in_specs=_row_specs(rows, tr) + _whole_specs(consts) + _row_specs(flat_cots, tr) + _row_specs(add_arrays, tr),
        out_specs=_row_specs(row_out, tr) + _whole_specs(const_out), out_shape=row_out + const_out,
        compiler_params=_params(("arbitrary",)),
    )(*rows, *consts, *flat_cots, *add_arrays)
    return list(outs)


def loss_head(y, target):
    t, d = y.shape
    tr = _pick(t, (256, 128, 64, 32, 16, 8))

    def body(y_ref, t_ref, dy_ref, l_ref):
        diff = y_ref[...] - t_ref[...]
        dy_ref[...] = diff * (1.0 / d)

        @pl.when(pl.program_id(0) == 0)
        def _():
            l_ref[...] = jnp.zeros_like(l_ref)

        l_ref[...] += 0.5 * jnp.sum(jnp.mean(diff * diff, axis=-1, keepdims=True), axis=0, keepdims=True)

    dy, l = pl.pallas_call(
        body, name="loss_head", grid=(t // tr,), in_specs=_row_specs([y, target], tr),
        out_specs=[pl.BlockSpec((tr, d), lambda i: (i, 0)), pl.BlockSpec((1, 1), lambda i: (0, 0))],
        out_shape=[jax.ShapeDtypeStruct((t, d), F32), jax.ShapeDtypeStruct((1, 1), F32)],
        compiler_params=_params(("arbitrary",)),
    )(y, target)
    return dy, l


ATT_PAD = N_LEFT_CHUNKS * CHUNK
MASKED = -1e30


def _attn_chunk(q_c, k_b, v_b, bias, valid):
    s = jnp.where(valid, _mm_nt(q_c, k_b) + bias, MASKED)
    e = jnp.exp(s - lax.stop_gradient(jnp.max(s, axis=-1, keepdims=True)))
    return _mm(e / jnp.sum(e, axis=-1, keepdims=True), v_b)


def _band_valid(c):
    return (c * CHUNK + lax.broadcasted_iota(jnp.int32, (1, BAND), 1)) >= ATT_PAD


def _head_spec(t):
    return pl.BlockSpec((None, t, HEAD_DIM), lambda h: (h, 0, 0))


def attention_fwd(q, k, v, bias):
    nh, t, _ = q.shape
    nchunks = t // CHUNK

    def body(q_ref, k_ref, v_ref, b_ref, o_ref, kp, vp):
        zeros = jnp.zeros((ATT_PAD, HEAD_DIM), F32)
        kp[pl.ds(0, ATT_PAD), :] = zeros
        vp[pl.ds(0, ATT_PAD), :] = zeros
        kp[pl.ds(ATT_PAD, t), :] = k_ref[...]
        vp[pl.ds(ATT_PAD, t), :] = v_ref[...]
        bias_h = b_ref[...]

        def step(c, carry):
            q0 = pl.multiple_of(c * CHUNK, CHUNK)
            o_ref[pl.ds(q0, CHUNK), :] = _attn_chunk(q_ref[pl.ds(q0, CHUNK), :], kp[pl.ds(q0, BAND), :],
                                                    vp[pl.ds(q0, BAND), :], bias_h, _band_valid(c))
            return carry

        lax.fori_loop(0, nchunks, step, 0)

    return pl.pallas_call(
        body, name="attention_fwd", grid=(nh,),
        in_specs=[_head_spec(t)] * 3 + [pl.BlockSpec((None, CHUNK, BAND), lambda h: (h, 0, 0))],
        out_specs=_head_spec(t), out_shape=jax.ShapeDtypeStruct((nh, t, HEAD_DIM), F32),
        scratch_shapes=[pltpu.VMEM((t + ATT_PAD, HEAD_DIM), F32)] * 2, compiler_params=_params(("parallel",)),
    )(q, k, v, bias)


def attention_bwd(q, k, v, bias, dout):
    nh, t, _ = q.shape
    nchunks = t // CHUNK

    def body(q_ref, k_ref, v_ref, b_ref, do_ref, dq_ref, dk_ref, dv_ref, db_ref, kp, vp, dkp, dvp):
        zeros = jnp.zeros((ATT_PAD, HEAD_DIM), F32)
        kp[pl.ds(0, ATT_PAD), :] = zeros
        vp[pl.ds(0, ATT_PAD), :] = zeros
        kp[pl.ds(ATT_PAD, t), :] = k_ref[...]
        vp[pl.ds(ATT_PAD, t), :] = v_ref[...]
        dkp[...] = jnp.zeros_like(dkp)
        dvp[...] = jnp.zeros_like(dvp)
        db_ref[...] = jnp.zeros_like(db_ref)
        bias_h = b_ref[...]

        def step(c, carry):
            q0 = pl.multiple_of(c * CHUNK, CHUNK)
            valid = _band_valid(c)
            _, vjp = jax.vjp(lambda a, b, cc, d: _attn_chunk(a, b, cc, d, valid), q_ref[pl.ds(q0, CHUNK), :],
                             kp[pl.ds(q0, BAND), :], vp[pl.ds(q0, BAND), :], bias_h)
            dq, dk, dv, db = vjp(do_ref[pl.ds(q0, CHUNK), :])
            dq_ref[pl.ds(q0, CHUNK), :] = dq
            dkp[pl.ds(q0, BAND), :] += dk
            dvp[pl.ds(q0, BAND), :] += dv
            db_ref[...] += db
            return carry

        lax.fori_loop(0, nchunks, step, 0)
        dk_ref[...] = dkp[pl.ds(ATT_PAD, t), :]
        dv_ref[...] = dvp[pl.ds(ATT_PAD, t), :]

    bias_spec = pl.BlockSpec((None, CHUNK, BAND), lambda h: (h, 0, 0))
    hs = jax.ShapeDtypeStruct((nh, t, HEAD_DIM), F32)
    return pl.pallas_call(
        body, name="attention_bwd", grid=(nh,), in_specs=[_head_spec(t)] * 3 + [bias_spec, _head_spec(t)],
        out_specs=[_head_spec(t)] * 3 + [bias_spec],
        out_shape=[hs, hs, hs, jax.ShapeDtypeStruct(bias.shape, F32)],
        scratch_shapes=[pltpu.VMEM((t + ATT_PAD, HEAD_DIM), F32)] * 4, compiler_params=_params(("parallel",)),
    )(q, k, v, bias, dout)


def _rel_onehot_t(i):
    j = lax.broadcasted_iota(jnp.int32, (N_REL, BAND), 1)
    r = lax.broadcasted_iota(jnp.int32, (N_REL, BAND), 0)
    idx = jnp.clip(i + ATT_PAD - j, -(CHUNK - 1), REL_CLIP) + (CHUNK - 1)
    return jnp.where(r == idx, 1.0, 0.0).astype(F32)


def bias_expand(rel):
    nh = rel.shape[0]

    def body(rel_ref, o_ref):
        o_ref[...] = _mm(rel_ref[...], _rel_onehot_t(pl.program_id(0)))

    return pl.pallas_call(
        body, name="bias_expand", grid=(CHUNK,), in_specs=[pl.BlockSpec((nh, N_REL), lambda i: (0, 0))],
        out_specs=pl.BlockSpec((None, nh, BAND), lambda i: (i, 0, 0)),
        out_shape=jax.ShapeDtypeStruct((CHUNK, nh, BAND), F32), compiler_params=_params(("parallel",)),
    )(rel)


def bias_reduce(dbias):
    nh = dbias.shape[1]

    def body(d_ref, o_ref):
        @pl.when(pl.program_id(0) == 0)
        def _():
            o_ref[...] = jnp.zeros_like(o_ref)

        o_ref[...] += _mm_nt(d_ref[...], _rel_onehot_t(pl.program_id(0)))

    return pl.pallas_call(
        body, name="bias_reduce", grid=(CHUNK,), in_specs=[pl.BlockSpec((None, nh, BAND), lambda i: (i, 0, 0))],
        out_specs=pl.BlockSpec((nh, N_REL), lambda i: (0, 0)), out_shape=jax.ShapeDtypeStruct((nh, N_REL), F32),
        compiler_params=_params(("arbitrary",)),
    )(dbias)


def _tri(n, strict):
    r = lax.broadcasted_iota(jnp.int32, (n, n), 0)
    c = lax.broadcasted_iota(jnp.int32, (n, n), 1)
    return (c < r) if strict else (c <= r)


def _rwkv_chunk(s, r, k, v, w, a, b):
    n = r.shape[0]
    lw = jnp.log(w)
    cum = _mm(jnp.where(_tri(n, False), 1.0, 0.0).astype(F32), lw)
    p_incl = jnp.exp(cum)
    p_inv = jnp.exp(-cum)
    a_t = a * jnp.exp(cum - lw)
    r_t = r * p_incl
    b_t = b * p_inv
    k_t = k * p_inv
    strict, incl = _tri(n, True), _tri(n, False)
    a_ab = jnp.where(strict, _mm_nt(a_t, b_t), 0.0)
    a_ak = jnp.where(strict, _mm_nt(a_t, k_t), 0.0)
    eye = jnp.where(incl & ~strict, 1.0, 0.0).astype(F32)
    inv = eye + a_ab
    power = a_ab
    for _ in range(int(math.log2(n)) - 1):
        power = _mm(power, power)
        inv = inv + _mm(inv, power)
    sa = _mm(inv, _mm_nt(a_t, s) + _mm(a_ak, v))
    y = (_mm_nt(r_t, s) + _mm(jnp.where(incl, _mm_nt(r_t, b_t), 0.0), sa)
         + _mm(jnp.where(incl, _mm_nt(r_t, k_t), 0.0), v))
    s_new = (s + _mm_tn(sa, b_t) + _mm_tn(v, k_t)) * p_incl[n - 1:n, :]
    return s_new, y


RWKV_TILE = 1024


def _rwkv_specs(t, reverse):
    tile = min(RWKV_TILE, t)
    nt = t // tile
    per = tile // RWKV_CHUNK
    pos = (lambda h, i: (h, nt - 1 - i, 0)) if reverse else (lambda h, i: (h, i, 0))
    pos4 = (lambda h, i: (h, nt - 1 - i, 0, 0)) if reverse else (lambda h, i: (h, i, 0, 0))
    return (tile, nt, per, pl.BlockSpec((None, tile, HEAD_DIM), pos),
            pl.BlockSpec((None, per, HEAD_DIM, HEAD_DIM), pos4))


def rwkv_fwd(r, k, v, w, a, b):
    nh, t, _ = r.shape
    tile, nt, per, row_spec, s_spec = _rwkv_specs(t, False)

    def body(r_ref, k_ref, v_ref, w_ref, a_ref, b_ref, y_ref, s_ref, state):
        @pl.when(pl.program_id(1) == 0)
        def _():
            state[...] = jnp.zeros_like(state)

        def step(c, s):
            rows = pl.ds(pl.multiple_of(c * RWKV_CHUNK, RWKV_CHUNK), RWKV_CHUNK)
            s_ref[c] = s
            s_new, y = _rwkv_chunk(s, r_ref[rows, :], k_ref[rows, :], v_ref[rows, :], w_ref[rows, :], a_ref[rows, :],
                                   b_ref[rows, :])
            y_ref[rows, :] = y
            return s_new

        state[...] = lax.fori_loop(0, per, step, state[...])

    return pl.pallas_call(
        body, name="rwkv_fwd", grid=(nh, nt), in_specs=[row_spec] * 6, out_specs=[row_spec, s_spec],
        out_shape=[jax.ShapeDtypeStruct((nh, t, HEAD_DIM), F32),
                   jax.ShapeDtypeStruct((nh, t // RWKV_CHUNK, HEAD_DIM, HEAD_DIM), F32)],
        scratch_shapes=[pltpu.VMEM((HEAD_DIM, HEAD_DIM), F32)], compiler_params=_params(("parallel", "arbitrary")),
    )(r, k, v, w, a, b)


def rwkv_bwd(r, k, v, w, a, b, states, dy):
    nh, t, _ = r.shape
    tile, nt, per, row_spec, s_spec = _rwkv_specs(t, True)

    def body(r_ref, k_ref, v_ref, w_ref, a_ref, b_ref, s_ref, dy_ref, *rest):
        d_refs, dstate = rest[:6], rest[6]

        @pl.when(pl.program_id(1) == 0)
        def _():
            dstate[...] = jnp.zeros_like(dstate)

        def step(i, ds):
            c = per - 1 - i
            rows = pl.ds(pl.multiple_of(c * RWKV_CHUNK, RWKV_CHUNK), RWKV_CHUNK)
            _, vjp = jax.vjp(_rwkv_chunk, s_ref[c], r_ref[rows, :], k_ref[rows, :], v_ref[rows, :], w_ref[rows, :],
                             a_ref[rows, :], b_ref[rows, :])
            grads = vjp((ds, dy_ref[rows, :]))
            for d_ref, g in zip(d_refs, grads[1:]):
                d_ref[rows, :] = g
            return grads[0]

        dstate[...] = lax.fori_loop(0, per, step, dstate[...])

    hs = jax.ShapeDtypeStruct((nh, t, HEAD_DIM), F32)
    return pl.pallas_call(
        body, name="rwkv_bwd", grid=(nh, nt), in_specs=[row_spec] * 6 + [s_spec, row_spec],
        out_specs=[row_spec] * 6, out_shape=[hs] * 6, scratch_shapes=[pltpu.VMEM((HEAD_DIM, HEAD_DIM), F32)],
        compiler_params=_params(("parallel", "arbitrary")),
    )(r, k, v, w, a, b, states, dy)


SUBLANES = 8


def s5_scan(bu_re, bu_im, a_re, a_im, *, reverse=False, h_prev=None):
    t, n = bu_re.shape
    tt = _pick(t, (512, 256, 128, 64, 32, 16, 8))
    tc = _pick(n, (1024, 512, 256, 128))
    nt = t // tt
    with_da = h_prev is not None
    sign = -1.0 if reverse else 1.0

    def body(*refs):
        if with_da:
            br, bi, ar_ref, ai_ref, pr, pi, hr_ref, hi_ref, dar_ref, dai_ref, sr, si = refs
        else:
            br, bi, ar_ref, ai_ref, hr_ref, hi_ref, sr, si = refs
        ti = pl.program_id(1)

        @pl.when(ti == 0)
        def _():
            sr[...] = jnp.zeros_like(sr)
            si[...] = jnp.zeros_like(si)
            if with_da:
                dar_ref[...] = jnp.zeros_like(dar_ref)
                dai_ref[...] = jnp.zeros_like(dai_ref)

        ar = ar_ref[...]
        ai = ai_ref[...] * sign

        def group(gi, carry):
            hr, hi, dar, dai = carry
            g0 = pl.multiple_of((tt // SUBLANES - 1 - gi if reverse else gi) * SUBLANES, SUBLANES)
            rows = pl.ds(g0, SUBLANES)
            xr, xi = br[rows, :], bi[rows, :]
            if with_da:
                qr, qi = pr[rows, :], pi[rows, :]
            out_r, out_i = [None] * SUBLANES, [None] * SUBLANES
            for s in (range(SUBLANES - 1, -1, -1) if reverse else range(SUBLANES)):
                hr, hi = ar * hr - ai * hi + xr[s:s + 1, :], ar * hi + ai * hr + xi[s:s + 1, :]
                out_r[s], out_i[s] = hr, hi
                if with_da:
                    dar = dar + hr * qr[s:s + 1, :] + hi * qi[s:s + 1, :]
                    dai = dai + hi * qr[s:s + 1, :] - hr * qi[s:s + 1, :]
            hr_ref[rows, :] = jnp.concatenate(out_r, axis=0)
            hi_ref[rows, :] = jnp.concatenate(out_i, axis=0)
            return hr, hi, dar, dai

        zero = jnp.zeros((1, tc), F32)
        hr, hi, dar, dai = lax.fori_loop(0, tt // SUBLANES, group, (sr[...], si[...], zero, zero))
        sr[...] = hr
        si[...] = hi
        if with_da:
            dar_ref[...] += dar
            dai_ref[...] += dai

    tile = pl.BlockSpec((tt, tc), (lambda ci, ti: (nt - 1 - ti, ci)) if reverse else (lambda ci, ti: (ti, ci)))
    col = pl.BlockSpec((1, tc), lambda ci, ti: (0, ci))
    hs = jax.ShapeDtypeStruct((t, n), F32)
    cs = jax.ShapeDtypeStruct((1, n), F32)
    ins = [bu_re, bu_im, a_re, a_im] + (list(h_prev) if with_da else [])
    return pl.pallas_call(
        body, name="s5_scan_bwd" if reverse else "s5_scan_fwd", grid=(n // tc, nt),
        in_specs=[tile, tile, col, col] + ([tile, tile] if with_da else []),
        out_specs=[tile, tile] + ([col, col] if with_da else []), out_shape=[hs, hs] + ([cs, cs] if with_da else []),
        scratch_shapes=[pltpu.VMEM((1, tc), F32)] * 2, compiler_params=_params(("parallel", "arbitrary")),
    )(*ins)


def exchange(name, gathers, scatters):
    arrays = list(gathers) + list(scatters)
    nb, ng = len(arrays), len(gathers)

    def body(*refs):
        ins, outs = refs[:nb], refs[nb:2 * nb]
        send_sems, recv_sems, local_sems = refs[2 * nb:]
        x, y, c = (lax.axis_index(n) for n in MESH_AXES)
        me = 4 * x + 2 * y + c
        started = []
        for b in range(nb):
            mine = ins[b] if b < ng else ins[b].at[me]
            local = pltpu.make_async_copy(mine, outs[b].at[me], local_sems.at[b])
            local.start()
            started.append(local)
        remote = []
        for k in range(1, N_DEV):
            px, py, pc = x ^ ((k >> 2) & 1), y ^ ((k >> 1) & 1), c ^ (k & 1)
            for b in range(nb):
                src = ins[b] if b < ng else ins[b].at[4 * px + 2 * py + pc]
                copy = pltpu.make_async_remote_copy(
                    src_ref=src, dst_ref=outs[b].at[me], send_sem=send_sems.at[b, k - 1], recv_sem=recv_sems.at[b, k - 1],
                    device_id=(px, py, pc), device_id_type=pl.DeviceIdType.MESH)
                copy.start()
                remote.append(copy)
        for copy in remote:
            copy.wait_send()
        for copy in remote:
            copy.wait_recv()
        for local in started:
            local.wait()

    out_shape = [jax.ShapeDtypeStruct((N_DEV,) + tuple(a.shape[-2:]), a.dtype) for a in arrays]
    hbm = pl.BlockSpec(memory_space=pl.ANY)
    return pl.pallas_call(
        body, name=name, in_specs=[hbm] * nb, out_specs=[hbm] * nb, out_shape=out_shape,
        scratch_shapes=[pltpu.SemaphoreType.DMA((nb, N_DEV - 1)), pltpu.SemaphoreType.DMA((nb, N_DEV - 1)),
                        pltpu.SemaphoreType.DMA((nb,))],
        compiler_params=pltpu.CompilerParams(has_side_effects=True),
    )(*arrays)


def adam_update(name, pieces, w, m, v):
    r, c = w.shape
    tr = _pick(r, (256, 192, 128, 64, 32, 16))

    def body(p_ref, w_ref, m_ref, v_ref, g_ref, d_ref, mo_ref, vo_ref):
        g = p_ref[0].astype(F32)
        for j in range(1, N_DEV):
            g = g + p_ref[j].astype(F32)
        m_new = ADAM_B1 * m_ref[...] + (1.0 - ADAM_B1) * g
        v_new = ADAM_B2 * v_ref[...] + (1.0 - ADAM_B2) * (g * g)
        m_hat = m_new / (1.0 - ADAM_B1 ** ADAM_STEP)
        v_hat = v_new / (1.0 - ADAM_B2 ** ADAM_STEP)
        g_ref[...] = g
        d_ref[...] = -ADAM_LR * (m_hat / (jnp.sqrt(v_hat) + ADAM_EPS) + ADAM_WD * w_ref[...])
        mo_ref[...] = m_new
        vo_ref[...] = v_new

    row = pl.BlockSpec((tr, c), lambda i: (i, 0))
    out = jax.ShapeDtypeStruct((r, c), F32)
    return pl.pallas_call(
        body, name=name, grid=(r // tr,), in_specs=[pl.BlockSpec((N_DEV, tr, c), lambda i: (0, i, 0)), row, row, row],
        out_specs=[row] * 4, out_shape=[out] * 4, compiler_params=_params(("parallel",)),
    )(pieces, w, m, v)


PACK_WIDTH = 1024
PACK_ROWS = 64


def _pack(arrays, dtype, lead=0):
    parts = []
    for a in arrays:
        head = a.shape[:lead]
        f = a.reshape(head + (-1,)).astype(dtype)
        pad = (-f.shape[-1]) % PACK_WIDTH
        if pad:
            f = jnp.pad(f, [(0, 0)] * lead + [(0, pad)])
        parts.append(f.reshape(head + (-1, PACK_WIDTH)))
    out = jnp.concatenate(parts, axis=lead)
    pad = (-out.shape[lead]) % PACK_ROWS
    if pad:
        out = jnp.pad(out, [(0, 0)] * lead + [(0, pad), (0, 0)])
    return out


def _unpack(packed, shapes, lead=0):
    head = packed.shape[:lead]
    out, row = [], 0
    for s in shapes:
        n = int(np.prod(s))
        rows = -(-n // PACK_WIDTH)
        chunk = lax.slice_in_dim(packed, row, row + rows, axis=lead).reshape(head + (-1,))
        out.append(lax.slice_in_dim(chunk, 0, n, axis=lead).reshape(head + tuple(s)))
        row += rows
    return out


def _to_natural(stacked, kind):
    if kind == "col":
        m = jnp.moveaxis(stacked, 0, -2)
        return m.reshape(m.shape[:-2] + (m.shape[-2] * m.shape[-1],))
    m = jnp.moveaxis(stacked, 0, -3)
    return m.reshape(m.shape[:-3] + (m.shape[-3] * m.shape[-2], m.shape[-1]))


def _to_stacked(natural, kind):
    if kind == "col":
        m = natural.reshape(natural.shape[:-1] + (N_DEV, natural.shape[-1] // N_DEV))
        return jnp.moveaxis(m, -2, 0)
    m = natural.reshape(natural.shape[:-2] + (N_DEV, natural.shape[-2] // N_DEV, natural.shape[-1]))
    return jnp.moveaxis(m, -3, 0)


def _f_rms(h, g):
    return (_rms(h, g).astype(BF16),)


def _f_swiglu(g, u):
    return ((g * _sigmoid(g) * u).astype(BF16),)


def _f_ple(h, pre, pp):
    return (h + _sigmoid(pre) * pp,)


def _head_rms(x, g):
    ms = _seg_sum(x * x, HEAD_DIM) * (1.0 / HEAD_DIM)
    return x * _seg_expand(lax.rsqrt(ms + RMS_EPS), HEAD_DIM) * _tile_lanes(g, x.shape[1] // HEAD_DIM)


def _f_attpre(q, k, q_gain, k_gain):
    return _head_rms(q, q_gain) * (HEAD_DIM ** -0.5), _head_rms(k, k_gain)


def _f_shift(z, z_prev, mu):
    return (z + (z_prev - z) * mu,)


def _f_rwkvpre(r, k, v, xw, xa, xg, w0, w_up, a0, a_up, g_up, k_k, k_a):
    del r, v
    w_log = -_softplus(-(w0 + _mm(jnp.tanh(xw), w_up))) - 0.5
    decay = jnp.exp(-jnp.exp(w_log))
    a = _sigmoid(a0 + _mm(xa, a_up))
    g = _mm(_sigmoid(xg), g_up)
    kk = k * k_k
    norm = jnp.sqrt(_seg_expand(_seg_sum(kk * kk, HEAD_DIM), HEAD_DIM))
    kk = kk / jnp.maximum(norm, 1e-12)
    return k * (1.0 + (a - 1.0) * k_a), decay, -kk, kk * a, g


def _f_rwkvpost(y, r, k, v, g, lnx_w, lnx_b, r_k):
    mean = _seg_expand(_seg_sum(y, HEAD_DIM) * (1.0 / HEAD_DIM), HEAD_DIM)
    yc = y - mean
    var = _seg_expand(_seg_sum(yc * yc, HEAD_DIM) * (1.0 / HEAD_DIM), HEAD_DIM)
    yn = yc * lax.rsqrt(var + GN_EPS) * lnx_w + lnx_b
    bonus = _seg_expand(_seg_sum(r * k * r_k, HEAD_DIM), HEAD_DIM) * v
    return ((yn + bonus) * g,)


def _f_s5disc(lam_re, lam_im, log_dt):
    dt = jnp.exp(log_dt)
    mag = jnp.exp(lam_re * dt)
    ab_re, ab_im = mag * jnp.cos(lam_im * dt), mag * jnp.sin(lam_im * dt)
    denom = lam_re * lam_re + lam_im * lam_im
    z_re = ((ab_re - 1.0) * lam_re + ab_im * lam_im) / denom
    z_im = (ab_im * lam_re - (ab_re - 1.0) * lam_im) / denom
    return ab_re, ab_im, z_re, z_im


def _f_s5b(z_re, z_im, b_re, b_im):
    return z_re * b_re - z_im * b_im, z_re * b_im + z_im * b_re


def _f_s5post(ypre, u, d_skip):
    return (_gelu_tanh(ypre + d_skip * u).astype(BF16),)


def _f_glu(h, z1, z2):
    return (h + z1 * _sigmoid(z2),)


def _heads(x):
    t = x.shape[0]
    return jnp.transpose(x.reshape(t, -1, HEAD_DIM), (1, 0, 2))


def _unheads(x):
    return jnp.transpose(x, (1, 0, 2)).reshape(x.shape[1], -1)


def _shift_down(x):
    return jnp.pad(x[:-1], ((1, 0), (0, 0)))


def _shift_up(x):
    return jnp.pad(x[1:], ((0, 1), (0, 0)))


def _block_diag(blocks):
    g, a, b = blocks.shape
    eye = jnp.eye(g, dtype=blocks.dtype)
    return (blocks[:, :, None, :] * eye[:, None, :, None]).reshape(g * a, g * b)


def _diag_blocks(dense, g):
    a, b = dense.shape[0] // g, dense.shape[1] // g
    return jnp.stack([dense[i * a:(i + 1) * a, i * b:(i + 1) * b] for i in range(g)], axis=0)


def _row_tile(t, width_bytes, budget=2 * 1024 * 1024):
    for tr in (512, 256, 128, 64, 32, 16, 8):
        if t % tr == 0 and tr * width_bytes <= budget:
            return tr
    return t


def _ffn_fwd(tag, h, norm_w, w_gate, w_up, w_down):
    t, d = h.shape
    f = w_gate.shape[1]
    n = rowwise(f"{tag}_norm", _f_rms, [h], [norm_w], [(d, BF16)], _row_tile(t, 4 * d))[0]
    g = matmul(f"{tag}_gate", n, w_gate)
    u = matmul(f"{tag}_up", n, w_up)
    a = rowwise(f"{tag}_act", _f_swiglu, [g, u], [], [(f, BF16)], _row_tile(t, 4 * f))[0]
    return matmul(f"{tag}_down", a, w_down, alpha=0.5, res=h), (h, n, g, u, a)


def _ffn_bwd(tag, dh2, saved, norm_w, w_gate, w_up, w_down):
    h, n, g, u, a = saved
    t, d = h.shape
    f = w_gate.shape[1]
    da = matmul(f"{tag}_da", dh2, w_down, tb=True, alpha=0.5)
    d_down = matmul(f"{tag}_dwdown", a, dh2, ta=True, alpha=0.5, out_dtype=BF16)
    dg, du = rowwise_vjp(f"{tag}_dact", _f_swiglu, [g, u], [], [da], _row_tile(t, 4 * f, 1 << 20), [True, True], [],
                         grad_dtypes={0: BF16, 1: BF16})
    dn = matmul(f"{tag}_dn_gate", dg, w_gate, tb=True)
    dn = matmul(f"{tag}_dn_up", du, w_up, tb=True, res=dn)
    d_gate = matmul(f"{tag}_dwgate", n, dg, ta=True, out_dtype=BF16)
    d_up = matmul(f"{tag}_dwup", n, du, ta=True, out_dtype=BF16)
    dh, d_norm = rowwise_vjp(f"{tag}_dnorm", _f_rms, [h], [norm_w], [dn], _row_tile(t, 4 * d), [True], [True],
                             add_to={0: dh2})
    return dh, d_norm, d_gate, d_up, d_down


def _ple_fwd(tag, h, norm_w, w_gate, w_proj, p_i):
    t, d = h.shape
    n = rowwise(f"{tag}_norm", _f_rms, [h], [norm_w], [(d, BF16)], _row_tile(t, 4 * d))[0]
    pre = matmul(f"{tag}_gate", n, w_gate)
    pp = matmul(f"{tag}_proj", p_i, w_proj)
    h2 = rowwise(f"{tag}_out", _f_ple, [h, pre, pp], [], [(d, F32)], _row_tile(t, 4 * d))[0]
    return h2, (h, n, pre, pp)


def _ple_bwd(tag, dh2, saved, norm_w, w_gate, w_proj, p_i):
    h, n, pre, pp = saved
    t, d = h.shape
    dpre, dpp = rowwise_vjp(f"{tag}_dout", _f_ple, [h, pre, pp], [], [dh2], _row_tile(t, 4 * d), [False, True, True], [],
                            grad_dtypes={1: BF16, 2: BF16})
    d_proj = matmul(f"{tag}_dwproj", p_i, dpp, ta=True, out_dtype=BF16)
    d_gate = matmul(f"{tag}_dwgate", n, dpre, ta=True, out_dtype=BF16)
    dn = matmul(f"{tag}_dn", dpre, w_gate, tb=True)
    dh, d_norm = rowwise_vjp(f"{tag}_dnorm", _f_rms, [h], [norm_w], [dn], _row_tile(t, 4 * d), [True], [True],
                             add_to={0: dh2})
    return dh, d_norm, d_gate, d_proj


def _ab_fwd(h, w):
    t, d = h.shape
    da = d // 2
    n = rowwise("ab_norm", _f_rms, [h], [w["mix_norm"]], [(d, BF16)], _row_tile(t, 4 * d))[0]
    proj = matmul("ab_in", n, w["ab_w_in"])
    q_raw, k_raw, v_att, z = proj[:, :da], proj[:, da:2 * da], proj[:, 2 * da:3 * da], proj[:, 3 * da:]
    tr = _row_tile(t, 4 * da, 1 << 19)
    qn, kn = rowwise("att_pre", _f_attpre, [q_raw, k_raw], [w["att_q_gain"], w["att_k_gain"]], [(da, F32)] * 2, tr)
    bias = jnp.transpose(bias_expand(w["att_rel_bias"]), (1, 0, 2))
    qh, kh, vh = _heads(qn), _heads(kn), _heads(v_att)
    att = _unheads(attention_fwd(qh, kh, vh, bias))
    z_prev = _shift_down(z)
    zz = rowwise("rwkv_shift", _f_shift, [z, z_prev], [w["rwkv_mu"]], [(z.shape[1], F32)], _row_tile(t, 4 * z.shape[1]))[0]
    o = [0, da, 2 * da, 3 * da, 3 * da + DECAY_LORA, 3 * da + DECAY_LORA + AAA_LORA, z.shape[1]]
    r, k, v, xw, xa, xg = (zz[:, o[i]:o[i + 1]] for i in range(6))
    pre_consts = [w[nm] for nm in ("rwkv_w0", "rwkv_w_up", "rwkv_a0", "rwkv_a_up", "rwkv_g_up", "rwkv_k_k", "rwkv_k_a")]
    k2, decay, ia, ib, g = rowwise("rwkv_pre", _f_rwkvpre, [r, k, v, xw, xa, xg], pre_consts, [(da, F32)] * 5, tr)
    heads = [_heads(a) for a in (r, k2, v, decay, ia, ib)]
    y_h, states = rwkv_fwd(*heads)
    y = _unheads(y_h)
    post_consts = [w["rwkv_lnx_w"], w["rwkv_lnx_b"], w["rwkv_r_k"]]
    rw = rowwise("rwkv_post", _f_rwkvpost, [y, r, k2, v, g], post_consts, [(da, F32)], tr)[0]
    cat = jnp.concatenate([att, rw], axis=1).astype(BF16)
    h2 = matmul("ab_out", cat, w["ab_w_out"], res=h)
    saved = dict(h=h, n=n, q_raw=q_raw, k_raw=k_raw, qh=qh, kh=kh, vh=vh, bias=bias, z=z, z_prev=z_prev,
                 rows=(r, k, v, xw, xa, xg), pre_consts=pre_consts, post=(y, r, k2, v, g), post_consts=post_consts,
                 heads=heads, states=states, cat=cat, tr=tr)
    return h2, saved


def _ab_bwd(dh2, s, w):
    h = s["h"]
    t, d = h.shape
    da = d // 2
    tr = s["tr"]
    grads = {}
    dcat = matmul("ab_dcat", dh2, w["ab_w_out"], tb=True)
    grads["ab_w_out"] = matmul("ab_dwout", s["cat"], dh2, ta=True, out_dtype=BF16)
    d_att, d_rw = dcat[:, :da], dcat[:, da:]
    dy, dr1, dk1, dv1, dg, grads["rwkv_lnx_w"], grads["rwkv_lnx_b"], grads["rwkv_r_k"] = rowwise_vjp(
        "rwkv_dpost", _f_rwkvpost, list(s["post"]), s["post_consts"], [d_rw], tr, [True] * 5, [True] * 3)
    d_heads = rwkv_bwd(*s["heads"], s["states"], _heads(dy))
    dr2, dk2, dv2, ddecay, dia, dib = (_unheads(a) for a in d_heads)
    pre = rowwise_vjp("rwkv_dpre", _f_rwkvpre, list(s["rows"]), s["pre_consts"], [(dk1, dk2), ddecay, dia, dib, dg], tr,
                      [True] * 6, [True] * 7, add_to={0: (dr1, dr2), 2: (dv1, dv2)})
    for nm, g in zip(("rwkv_w0", "rwkv_w_up", "rwkv_a0", "rwkv_a_up", "rwkv_g_up", "rwkv_k_k", "rwkv_k_a"), pre[6:]):
        grads[nm] = g
    dzz = jnp.concatenate(pre[:6], axis=1)
    trz = _row_tile(t, 4 * dzz.shape[1])
    grads["rwkv_mu"] = rowwise_vjp("rwkv_dmu", _f_shift, [s["z"], s["z_prev"]], [w["rwkv_mu"]], [dzz], trz,
                                   [False, False], [True])[0]
    dz = rowwise("rwkv_dshift", _f_shift, [dzz, _shift_up(dzz)], [w["rwkv_mu"]], [(dzz.shape[1], F32)], trz)[0]
    dqh, dkh, dvh, dbias = attention_bwd(s["qh"], s["kh"], s["vh"], s["bias"], _heads(d_att))
    grads["att_rel_bias"] = bias_reduce(jnp.transpose(dbias, (1, 0, 2)))
    dq_raw, dk_raw, grads["att_q_gain"], grads["att_k_gain"] = rowwise_vjp(
        "att_dpre", _f_attpre, [s["q_raw"], s["k_raw"]], [w["att_q_gain"], w["att_k_gain"]],
        [_unheads(dqh), _unheads(dkh)], tr, [True, True], [True, True])
    dproj = jnp.concatenate([dq_raw, dk_raw, _unheads(dvh), dz], axis=1).astype(BF16)
    dn = matmul("ab_dn", dproj, w["ab_w_in"], tb=True)
    grads["ab_w_in"] = matmul("ab_dwin", s["n"], dproj, ta=True, out_dtype=BF16)
    dh, grads["mix_norm"] = rowwise_vjp("ab_dnorm", _f_rms, [h], [w["mix_norm"]], [dn], _row_tile(t, 4 * d), [True], [True],
                                        add_to={0: dh2})
    return dh, grads


def _s5_fwd(h, w):
    t, d = h.shape
    n_groups, n_state = w["ssm_lambda_re"].shape
    gp = n_groups * n_state
    n = rowwise("s5_norm", _f_rms, [h], [w["mix_norm"]], [(d, BF16)], _row_tile(t, 4 * d))[0]
    u = matmul("s5_in", n, w["ssm_w_in"])
    disc_rows = [w["ssm_lambda_re"], w["ssm_lambda_im"], w["ssm_log_dt"]]
    ab_re, ab_im, z_re, z_im = rowwise("s5_disc", _f_s5disc, disc_rows, [], [(n_state, F32)] * 4, n_groups)
    b_rows = [z_re.reshape(gp, 1), z_im.reshape(gp, 1), w["ssm_b_re"], w["ssm_b_im"]]
    trb = _row_tile(gp, 512)
    bb_re, bb_im = rowwise("s5_bbar", _f_s5b, b_rows, [], [(SSM_GROUP, F32)] * 2, trb)
    to_dense = lambda bb: _block_diag(jnp.transpose(bb.reshape(n_groups, n_state, SSM_GROUP), (0, 2, 1)))
    bd_re, bd_im = to_dense(bb_re), to_dense(bb_im)
    cd_re = _block_diag(jnp.transpose(w["ssm_c_re"], (0, 2, 1)))
    cd_im = -_block_diag(jnp.transpose(w["ssm_c_im"], (0, 2, 1)))
    a_re, a_im = ab_re.reshape(1, gp), ab_im.reshape(1, gp)
    bu_re = matmul("s5_bu_re", u, bd_re)
    bu_im = matmul("s5_bu_im", u, bd_im)
    h_re, h_im = s5_scan(bu_re, bu_im, a_re, a_im)
    ypre = matmul("s5_y_re", h_re, cd_re)
    ypre = matmul("s5_y_im", h_im, cd_im, res=ypre)
    tru = _row_tile(t, 4 * u.shape[1])
    yg = rowwise("s5_post", _f_s5post, [ypre, u], [w["ssm_d"]], [(u.shape[1], BF16)], tru)[0]
    w_out1, w_out2 = w["ssm_w_out"][:, :d], w["ssm_w_out"][:, d:]
    z1 = matmul("s5_out1", yg, w_out1)
    z2 = matmul("s5_out2", yg, w_out2)
    h2 = rowwise("s5_glu", _f_glu, [h, z1, z2], [], [(d, F32)], _row_tile(t, 4 * d))[0]
    saved = dict(h=h, n=n, u=u, disc_rows=disc_rows, b_rows=b_rows, trb=trb, bd=(bd_re, bd_im), cd=(cd_re, cd_im),
                 a=(a_re, a_im), hs=(h_re, h_im), ypre=ypre, yg=yg, w_out=(w_out1, w_out2), z=(z1, z2), tru=tru)
    return h2, saved


def _s5_bwd(dh2, s, w):
    h, u = s["h"], s["u"]
    t, d = h.shape
    n_groups, n_state = w["ssm_lambda_re"].shape
    gp = n_groups * n_state
    grads = {}
    z1, z2 = s["z"]
    w_out1, w_out2 = s["w_out"]
    dz1, dz2 = rowwise_vjp("s5_dglu", _f_glu, [h, z1, z2], [], [dh2], _row_tile(t, 4 * d), [False, True, True], [],
                           grad_dtypes={1: BF16, 2: BF16})
    dyg = matmul("s5_dyg1", dz1, w_out1, tb=True)
    dyg = matmul("s5_dyg2", dz2, w_out2, tb=True, res=dyg)
    grads["ssm_w_out"] = jnp.concatenate([matmul("s5_dwout1", s["yg"], dz1, ta=True, out_dtype=BF16),
                                          matmul("s5_dwout2", s["yg"], dz2, ta=True, out_dtype=BF16)], axis=1)
    dypre, du1, grads["ssm_d"] = rowwise_vjp("s5_dpost", _f_s5post, [s["ypre"], u], [w["ssm_d"]], [dyg], s["tru"],
                                             [True, True], [True])
    cd_re, cd_im = s["cd"]
    h_re, h_im = s["hs"]
    dh_re = matmul("s5_dh_re", dypre, cd_re, tb=True)
    dh_im = matmul("s5_dh_im", dypre, cd_im, tb=True)
    dcd_re = matmul("s5_dc_re", h_re, dypre, ta=True)
    dcd_im = matmul("s5_dc_im", h_im, dypre, ta=True)
    a_re, a_im = s["a"]
    g_re, g_im, da_re, da_im = s5_scan(dh_re, dh_im, a_re, a_im, reverse=True,
                                       h_prev=(_shift_down(h_re), _shift_down(h_im)))
    bd_re, bd_im = s["bd"]
    du = matmul("s5_du_re", g_re, bd_re, tb=True, res=du1)
    du = matmul("s5_du_im", g_im, bd_im, tb=True, res=du)
    dbd_re = matmul("s5_db_re", u, g_re, ta=True)
    dbd_im = matmul("s5_db_im", u, g_im, ta=True)
    grads["ssm_w_in"] = matmul("s5_dwin", s["n"], du, ta=True, out_dtype=BF16)
    dn = matmul("s5_dn", du, w["ssm_w_in"], tb=True)
    dh, grads["mix_norm"] = rowwise_vjp("s5_dnorm", _f_rms, [h], [w["mix_norm"]], [dn], _row_tile(t, 4 * d), [True], [True],
                                        add_to={0: dh2})
    from_dense = lambda m: jnp.transpose(_diag_blocks(m, n_groups), (0, 2, 1)).reshape(gp, SSM_GROUP)
    dz_re, dz_im, grads["ssm_b_re"], grads["ssm_b_im"] = rowwise_vjp(
        "s5_dbbar", _f_s5b, s["b_rows"], [], [from_dense(dbd_re), from_dense(dbd_im)], s["trb"], [True] * 4, [])
    disc_cots = [da_re.reshape(n_groups, n_state), da_im.reshape(n_groups, n_state),
                 dz_re.reshape(n_groups, n_state), dz_im.reshape(n_groups, n_state)]
    grads["ssm_lambda_re"], grads["ssm_lambda_im"], grads["ssm_log_dt"] = rowwise_vjp(
        "s5_ddisc", _f_s5disc, s["disc_rows"], [], disc_cots, n_groups, [True] * 3, [])
    grads["ssm_c_re"] = jnp.transpose(_diag_blocks(dcd_re, n_groups), (0, 2, 1))
    grads["ssm_c_im"] = -jnp.transpose(_diag_blocks(dcd_im, n_groups), (0, 2, 1))
    return dh, grads


WEIGHTS = ["ffn1_norm", "ffn1_w_gate", "ffn1_w_up", "ffn1_w_down", "mix_norm", "ffn2_norm", "ffn2_w_gate", "ffn2_w_up",
           "ffn2_w_down", "ple_norm", "ple_w_gate", "ple_w_proj", "ab_w_in", "att_q_gain", "att_k_gain", "att_rel_bias",
           "rwkv_mu", "rwkv_w0", "rwkv_w_up", "rwkv_a0", "rwkv_a_up", "rwkv_g_up", "rwkv_k_k", "rwkv_k_a", "rwkv_r_k",
           "rwkv_lnx_w", "rwkv_lnx_b", "ab_w_out", "ssm_w_in", "ssm_lambda_re", "ssm_lambda_im", "ssm_log_dt", "ssm_b_re",
           "ssm_b_im", "ssm_c_re", "ssm_c_im", "ssm_d", "ssm_w_out"]
BIG = {"ffn1_w_gate": "col", "ffn1_w_up": "col", "ffn1_w_down": "row", "ffn2_w_gate": "col", "ffn2_w_up": "col",
       "ffn2_w_down": "row", "ple_w_gate": "row", "ple_w_proj": "col", "ab_w_in": "col", "ab_w_out": "row",
       "ssm_w_in": "row", "ssm_w_out": "col"}
SMALL_CUT = {"rwkv_w_up": "col", "rwkv_a_up": "col", "rwkv_g_up": "col", "ssm_d": "col"}
REPLICATED = [n for n in WEIGHTS if n not in BIG and n not in SMALL_CUT]


def kernel(x, p, ffn1_norm, ffn1_w_gate, ffn1_w_up, ffn1_w_down, mix_norm, ffn2_norm, ffn2_w_gate, ffn2_w_up, ffn2_w_down, ple_norm, ple_w_gate, ple_w_proj, ab_w_in, att_q_gain, att_k_gain, att_rel_bias, rwkv_mu, rwkv_w0, rwkv_w_up, rwkv_a0, rwkv_a_up, rwkv_g_up, rwkv_k_k, rwkv_k_a, rwkv_r_k, rwkv_lnx_w, rwkv_lnx_b, ab_w_out, ssm_w_in, ssm_lambda_re, ssm_lambda_im, ssm_log_dt, ssm_b_re, ssm_b_im, ssm_c_re, ssm_c_im, ssm_d, ssm_w_out, loss_target, m_ffn1_norm, m_ffn1_w_gate, m_ffn1_w_up, m_ffn1_w_down, m_mix_norm, m_ffn2_norm, m_ffn2_w_gate, m_ffn2_w_up, m_ffn2_w_down, m_ple_norm, m_ple_w_gate, m_ple_w_proj, m_ab_w_in, m_att_q_gain, m_att_k_gain, m_att_rel_bias, m_rwkv_mu, m_rwkv_w0, m_rwkv_w_up, m_rwkv_a0, m_rwkv_a_up, m_rwkv_g_up, m_rwkv_k_k, m_rwkv_k_a, m_rwkv_r_k, m_rwkv_lnx_w, m_rwkv_lnx_b, m_ab_w_out, m_ssm_w_in, m_ssm_lambda_re, m_ssm_lambda_im, m_ssm_log_dt, m_ssm_b_re, m_ssm_b_im, m_ssm_c_re, m_ssm_c_im, m_ssm_d, m_ssm_w_out, v_ffn1_norm, v_ffn1_w_gate, v_ffn1_w_up, v_ffn1_w_down, v_mix_norm, v_ffn2_norm, v_ffn2_w_gate, v_ffn2_w_up, v_ffn2_w_down, v_ple_norm, v_ple_w_gate, v_ple_w_proj, v_ab_w_in, v_att_q_gain, v_att_k_gain, v_att_rel_bias, v_rwkv_mu, v_rwkv_w0, v_rwkv_w_up, v_rwkv_a0, v_rwkv_a_up, v_rwkv_g_up, v_rwkv_k_k, v_rwkv_k_a, v_rwkv_r_k, v_rwkv_lnx_w, v_rwkv_lnx_b, v_ab_w_out, v_ssm_w_in, v_ssm_lambda_re, v_ssm_lambda_im, v_ssm_log_dt, v_ssm_b_re, v_ssm_b_im, v_ssm_c_re, v_ssm_c_im, v_ssm_d, v_ssm_w_out):
    vals = dict(locals())
    depth = ffn1_norm.shape[0]
    n_groups, n_state = ssm_lambda_re.shape[1:]

    big_names, small_names = list(BIG), list(SMALL_CUT)
    got_big, got_small = exchange("gather_weights", [_pack([vals[n] for n in big_names], BF16),
                                                     _pack([vals[n] for n in small_names], F32)], [])
    full = {}
    for names, got, kinds in ((big_names, got_big, BIG), (small_names, got_small, SMALL_CUT)):
        for n, a in zip(names, _unpack(got, [vals[n].shape for n in names], lead=1)):
            full[n] = _to_natural(a, kinds[n])

    def row(name, j):
        return vals[name][j].reshape(1, -1)

    def ffn_weights(which, i):
        return (row(f"{which}_norm", i), full[f"{which}_w_gate"][i], full[f"{which}_w_up"][i], full[f"{which}_w_down"][i])

    def mixer_weights(i):
        j = i // 2
        if i % 2 == 0:
            w = {n: row(n, j) for n in ("att_q_gain", "att_k_gain", "rwkv_mu", "rwkv_w0", "rwkv_a0", "rwkv_k_k", "rwkv_k_a",
                                        "rwkv_r_k", "rwkv_lnx_w", "rwkv_lnx_b")}
            w.update({n: full[n][j] for n in ("ab_w_in", "ab_w_out", "rwkv_w_up", "rwkv_a_up", "rwkv_g_up")})
            w["att_rel_bias"] = att_rel_bias[j]
        else:
            w = {"ssm_lambda_re": ssm_lambda_re[j], "ssm_lambda_im": ssm_lambda_im[j],
                 "ssm_log_dt": ssm_log_dt[j].reshape(n_groups, 1),
                 "ssm_b_re": ssm_b_re[j].reshape(n_groups * n_state, -1),
                 "ssm_b_im": ssm_b_im[j].reshape(n_groups * n_state, -1),
                 "ssm_c_re": ssm_c_re[j], "ssm_c_im": ssm_c_im[j], "ssm_d": full["ssm_d"][j].reshape(1, -1),
                 "ssm_w_in": full["ssm_w_in"][j], "ssm_w_out": full["ssm_w_out"][j]}
        w["mix_norm"] = row("mix_norm", i)
        return w

    def ple_weights(i):
        return (row("ple_norm", i), full["ple_w_gate"][i], full["ple_w_proj"][i], p[i, 0])

    h = x[0]
    saved = []
    for i in range(depth):
        h, s1 = _ffn_fwd(f"l{i}_ffn1", h, *ffn_weights("ffn1", i))
        h, sm = (_ab_fwd if i % 2 == 0 else _s5_fwd)(h, mixer_weights(i))
        h, s2 = _ffn_fwd(f"l{i}_ffn2", h, *ffn_weights("ffn2", i))
        h, sp = _ple_fwd(f"l{i}_ple", h, *ple_weights(i))
        saved.append((s1, sm, s2, sp))
    dh, loss_part = loss_head(h, loss_target[0])

    per_layer = {n: [] for n in WEIGHTS}
    for i in reversed(range(depth)):
        s1, sm, s2, sp = saved[i]
        dh, d_norm, d_gate, d_proj = _ple_bwd(f"l{i}_ple", dh, sp, *ple_weights(i))
        for n, g in (("ple_norm", d_norm), ("ple_w_gate", d_gate), ("ple_w_proj", d_proj)):
            per_layer[n].insert(0, g)
        dh, d_norm, d_gate, d_up, d_down = _ffn_bwd(f"l{i}_ffn2", dh, s2, *ffn_weights("ffn2", i))
        for n, g in (("ffn2_norm", d_norm), ("ffn2_w_gate", d_gate), ("ffn2_w_up", d_up), ("ffn2_w_down", d_down)):
            per_layer[n].insert(0, g)
        dh, mixer_grads = (_ab_bwd if i % 2 == 0 else _s5_bwd)(dh, sm, mixer_weights(i))
        for n, g in mixer_grads.items():
            per_layer[n].insert(0, g)
        dh, d_norm, d_gate, d_up, d_down = _ffn_bwd(f"l{i}_ffn1", dh, s1, *ffn_weights("ffn1", i))
        for n, g in (("ffn1_norm", d_norm), ("ffn1_w_gate", d_gate), ("ffn1_w_up", d_up), ("ffn1_w_down", d_down)):
            per_layer[n].insert(0, g)
    grad_x = dh[None]

    def whole_grad(n):
        g = jnp.stack(per_layer[n], axis=0)
        lead = vals[n].shape[0]
        if n in BIG or n in SMALL_CUT:
            return g.reshape((lead,) + g.shape[1:]) if n != "ssm_d" else g.reshape(lead, -1)
        return g.reshape(vals[n].shape)

    big_pieces = _pack([_to_stacked(whole_grad(n), BIG[n]) for n in big_names], BF16, lead=1)
    small_pieces = _pack([_to_stacked(whole_grad(n), SMALL_CUT[n]) for n in small_names], F32, lead=1)
    rep_mine = _pack([whole_grad(n) for n in REPLICATED] + [loss_part], F32)
    got_rep, got_big, got_small = exchange("reduce_grads", [rep_mine], [big_pieces, small_pieces])

    out = {}
    zero = jnp.zeros((1, 1), F32)
    for tag, names, pieces, extra in (("big", big_names, got_big, []), ("small", small_names, got_small, []),
                                      ("rep", REPLICATED, got_rep, [zero])):
        state = [_pack([vals[pre + n] for n in names] + extra, F32) for pre in ("", "m_", "v_")]
        shapes = [vals[n].shape for n in names] + [e.shape for e in extra]
        results = [_unpack(r, shapes) for r in adam_update(f"adam_{tag}", pieces, *state)]
        for q, n in enumerate(names):
            out[n] = tuple(r[q] for r in results)
        if extra:
            loss = results[0][-1].reshape(())
    return (loss, grad_x, *[out[n][0] for n in WEIGHTS], *[out[n][1] for n in WEIGHTS], *[out[n][2] for n in WEIGHTS],
            *[out[n][3] for n in WEIGHTS])
```

```python
import functools
import math

import jax
import jax.numpy as jnp
import numpy as np
from jax import lax
from jax.experimental import pallas as pl
from jax.experimental.pallas import tpu as pltpu

F32 = jnp.float32
BF16 = jnp.bfloat16
HIGHEST = lax.Precision.HIGHEST
MESH_AXES = ("x", "y", "c")
N_DEV = 8

CHUNK = 64
N_LEFT_CHUNKS = 8
BAND = (N_LEFT_CHUNKS + 1) * CHUNK
HEAD_DIM = 64
REL_CLIP = 128
N_REL = (CHUNK - 1) + REL_CLIP + 1
DECAY_LORA = 64
AAA_LORA = 64
GATE_LORA = 128
SSM_GROUP = 16
SSM_STATE = 64
RMS_EPS = 1e-6
GN_EPS = 64e-5
ADAM_LR = 0.001
ADAM_B1 = 0.9
ADAM_B2 = 0.999
ADAM_EPS = 1e-08
ADAM_WD = 0.01
ADAM_STEP = 10

RWKV_CHUNK = 64
VMEM_LIMIT = 56 * 1024 * 1024
LANES = 128


def _params(semantics):
    return pltpu.CompilerParams(dimension_semantics=semantics, vmem_limit_bytes=VMEM_LIMIT)


def _pick(n, prefs):
    for t in prefs:
        if n % t == 0:
            return t
    return n


def _dot(a, b, dims):
    return lax.dot_general(a, b, (dims, ((), ())), precision=HIGHEST, preferred_element_type=F32)


def _mm(a, b):
    return _dot(a, b, ((1,), (0,)))


def _mm_nt(a, b):
    return _dot(a, b, ((1,), (1,)))


def _mm_tn(a, b):
    return _dot(a, b, ((0,), (0,)))


def _split2(x):
    hi = x.astype(BF16)
    return hi, (x - hi.astype(F32)).astype(BF16)


def _dot3_raw(a, b, dims):
    a_hi, a_lo = _split2(a)
    b_hi, b_lo = _split2(b)
    dot = lambda p, q: lax.dot_general(p, q, (dims, ((), ())), preferred_element_type=F32)
    return dot(a_hi, b_hi) + (dot(a_hi, b_lo) + dot(a_lo, b_hi))


def _dot1_raw(a, b, dims):
    return lax.dot_general(a.astype(BF16), b.astype(BF16), (dims, ((), ())), preferred_element_type=F32)


NN, NT, TN = ((1,), (0,)), ((1,), (1,)), ((0,), (0,))


def _make_dot(raw):
    @functools.partial(jax.custom_vjp, nondiff_argnums=(2,))
    def dot(a, b, dims):
        return raw(a, b, dims)

    def fwd(a, b, dims):
        return raw(a, b, dims), (a, b)

    def bwd(dims, saved, g):
        a, b = saved
        if dims == NN:
            return raw(g, b, NT), raw(a, g, TN)
        if dims == NT:
            return raw(g, b, NN), raw(g, a, TN)
        return raw(b, g, NT), raw(a, g, NN)

    dot.defvjp(fwd, bwd)
    return dot


_dot3 = _make_dot(_dot3_raw)
_dot1 = _make_dot(_dot1_raw)


def _sigmoid(x):
    return 1.0 / (1.0 + jnp.exp(-x))


def _softplus(x):
    return jnp.maximum(x, 0.0) + jnp.log(1.0 + jnp.exp(-jnp.abs(x)))


def _gelu_tanh(x):
    return 0.5 * x * (1.0 + jnp.tanh(math.sqrt(2.0 / math.pi) * (x + 0.044715 * (x * x * x))))


def _seg_indicator(n, seg):
    r = lax.broadcasted_iota(jnp.int32, (n, n // seg), 0)
    c = lax.broadcasted_iota(jnp.int32, (n, n // seg), 1)
    return jnp.where((r >= c * seg) & (r < (c + 1) * seg), 1.0, 0.0).astype(F32)


def _seg_indicator_t(n, seg):
    c = lax.broadcasted_iota(jnp.int32, (n // seg, n), 0)
    r = lax.broadcasted_iota(jnp.int32, (n // seg, n), 1)
    return jnp.where((r >= c * seg) & (r < (c + 1) * seg), 1.0, 0.0).astype(F32)


def _seg_sum(x, seg):
    return _mm(x, _seg_indicator(x.shape[1], seg))


def _seg_expand(s, seg):
    return _mm(s, _seg_indicator_t(s.shape[1] * seg, seg))


def _tile_lanes(g, reps):
    n = g.shape[1]
    r = lax.broadcasted_iota(jnp.int32, (n, n * reps), 0)
    c = lax.broadcasted_iota(jnp.int32, (n, n * reps), 1)
    return _mm(g, jnp.where((c & (n - 1)) == r, 1.0, 0.0).astype(F32))


def _rms(x, g):
    return x * lax.rsqrt(jnp.mean(x * x, axis=-1, keepdims=True) + RMS_EPS) * g


MATMUL_VMEM_BUDGET = 40 * 1024 * 1024
MATMUL_MAX_TILE = 2048
HBM_BYTES_PER_S = 1.2e12
MXU_FLOPS_PER_S = 8e14
GRID_STEP_S = 0.35e-6


def _tile_candidates(dim):
    c = [d for d in range(LANES, min(dim, MATMUL_MAX_TILE) + 1, LANES) if dim % d == 0]
    return c or [dim]


def _matmul_tiles(m, n, k, a_bytes, b_bytes, out_bytes, res_bytes):
    best = None
    for tm in _tile_candidates(m):
        for tn in _tile_candidates(n):
            for tk in _tile_candidates(k):
                casts = (tm * tk * 2 if a_bytes > 2 else 0) + (tk * tn * 2 if b_bytes > 2 else 0)
                vmem = (2 * (tm * tk * a_bytes + tk * tn * b_bytes + tm * tn * (out_bytes + res_bytes))
                        + 2 * tm * tn * 4 + casts)
                if vmem > MATMUL_VMEM_BUDGET:
                    continue
                steps = (m // tm) * (n // tn) * (k // tk)
                a_reads = 1 if k == tk else n // tn
                traffic = (m * k * a_bytes * a_reads + k * n * b_bytes * (m // tm) + m * n * (out_bytes + res_bytes))
                dma_s = traffic / HBM_BYTES_PER_S
                cost = max(dma_s, 2.0 * m * n * k / MXU_FLOPS_PER_S) + 0.2 * dma_s + steps * GRID_STEP_S
                if best is None or cost < best[0]:
                    best = (cost, tm, tn, tk)
    return best[1:]


def matmul(name, a, b, *, ta=False, tb=False, alpha=1.0, res=None, out_dtype=F32):
    m, k = (a.shape[1], a.shape[0]) if ta else a.shape
    n = b.shape[0] if tb else b.shape[1]
    assert k == (b.shape[1] if tb else b.shape[0]), (name, a.shape, b.shape)
    tm, tn, tk = _matmul_tiles(m, n, k, a.dtype.itemsize, b.dtype.itemsize, jnp.dtype(out_dtype).itemsize,
                               0 if res is None else res.dtype.itemsize)
    nk = k // tk
    dims = ((0 if ta else 1,), (1 if tb else 0,))

    def body(*refs):
        if res is None:
            a_ref, b_ref, o_ref, acc_ref = refs
            res_ref = None
        else:
            a_ref, b_ref, res_ref, o_ref, acc_ref = refs
        kk = pl.program_id(2)

        @pl.when(kk == 0)
        def _():
            acc_ref[...] = jnp.zeros_like(acc_ref)

        acc_ref[...] += lax.dot_general(a_ref[...].astype(BF16), b_ref[...].astype(BF16), (dims, ((), ())),
                                        preferred_element_type=F32)

        @pl.when(kk == nk - 1)
        def _():
            out = acc_ref[...] * alpha
            if res_ref is not None:
                out = out + res_ref[...].astype(F32)
            o_ref[...] = out.astype(o_ref.dtype)

    a_spec = pl.BlockSpec((tk, tm), lambda i, j, kk: (kk, i)) if ta else pl.BlockSpec((tm, tk), lambda i, j, kk: (i, kk))
    b_spec = pl.BlockSpec((tn, tk), lambda i, j, kk: (j, kk)) if tb else pl.BlockSpec((tk, tn), lambda i, j, kk: (kk, j))
    o_spec = pl.BlockSpec((tm, tn), lambda i, j, kk: (i, j))
    in_specs = [a_spec, b_spec] + ([o_spec] if res is not None else [])
    args = (a, b) + ((res,) if res is not None else ())
    return pl.pallas_call(
        body, name=name, grid=(m // tm, n // tn, nk), in_specs=in_specs, out_specs=o_spec,
        out_shape=jax.ShapeDtypeStruct((m, n), out_dtype), scratch_shapes=[pltpu.VMEM((tm, tn), F32)],
        compiler_params=_params(("parallel", "parallel", "arbitrary")),
    )(*args)


def _row_specs(arrays, tr):
    return [pl.BlockSpec((tr, a.shape[1]), lambda i: (i, 0)) for a in arrays]


def _whole_specs(arrays):
    return [pl.BlockSpec(a.shape, lambda i: (0, 0)) for a in arrays]


def rowwise(name, fn, rows, consts, outs, tr):
    t = rows[0].shape[0]
    nr, nc = len(rows), len(consts)

    def body(*refs):
        vals = [r[...] for r in refs[:nr + nc]]
        res = fn(*vals)
        for o_ref, o in zip(refs[nr + nc:], res):
            o_ref[...] = o.astype(o_ref.dtype)

    out_shape = [jax.ShapeDtypeStruct((t, w), d) for w, d in outs]
    return pl.pallas_call(
        body, name=name, grid=(t // tr,), in_specs=_row_specs(rows, tr) + _whole_specs(consts),
        out_specs=_row_specs(out_shape, tr), out_shape=out_shape, compiler_params=_params(("parallel",)),
    )(*rows, *consts)


def rowwise_vjp(name, fn, rows, consts, cots, tr, row_grad, const_grad, add_to=None, grad_dtypes=None):
    t = rows[0].shape[0]
    nr, nc = len(rows), len(consts)
    cot_groups = [c if isinstance(c, (tuple, list)) else (c,) for c in cots]
    flat_cots = [a for g in cot_groups for a in g]
    add_to = {i: (a if isinstance(a, (tuple, list)) else (a,)) for i, a in (add_to or {}).items()}
    add_idx = [(i, q) for i in sorted(add_to) for q in range(len(add_to[i]))]
    add_arrays = [add_to[i][q] for i, q in add_idx]
    r_idx = [i for i in range(nr) if row_grad[i]]
    c_idx = [i for i in range(nc) if const_grad[i]]
    grad_dtypes = grad_dtypes or {}
    n_in = nr + nc + len(flat_cots) + len(add_arrays)

    def body(*refs):
        vals = [r[...] for r in refs[:nr + nc]]
        pos = nr + nc
        cts = []
        for g in cot_groups:
            s = refs[pos][...].astype(F32)
            for q in range(1, len(g)):
                s = s + refs[pos + q][...].astype(F32)
            cts.append(s)
            pos += len(g)
        adds = {}
        for n_add, (i, _) in enumerate(add_idx):
            term = refs[pos + n_add][...].astype(F32)
            adds[i] = adds[i] + term if i in adds else term
        diff =[vals[i] for i in r_idx] + [vals[nr + i] for i in c_idx]

        def f(*d):
            full = list(vals)
            for q, i in enumerate(r_idx):
                full[i] = d[q]
            for q, i in enumerate(c_idx):
                full[nr + i] = d[len(r_idx) + q]
            return tuple(fn(*full))

        prim, vjp = jax.vjp(f, *diff)
        grads = vjp(tuple(c.astype(p.dtype) for c, p in zip(cts, prim)))
        o_refs = refs[n_in:]
        for q, i in enumerate(r_idx):
            g = grads[q].astype(F32)
            if i in adds:
                g = g + adds[i].astype(F32)
            o_refs[q][...] = g.astype(o_refs[q].dtype)
        step = pl.program_id(0)
        for q, i in enumerate(c_idx):
            o_ref = o_refs[len(r_idx) + q]

            @pl.when(step == 0)
            def _(o_ref=o_ref):
                o_ref[...] = jnp.zeros_like(o_ref)

            o_ref[...] += grads[len(r_idx) + q].astype(F32)

    row_out = [jax.ShapeDtypeStruct(rows[i].shape, grad_dtypes.get(i, F32)) for i in r_idx]
    const_out = [jax.ShapeDtypeStruct(consts[i].shape, F32) for i in c_idx]
    outs = pl.pallas_call(
        body, name=name, grid=(t // tr,),
        in_specs=_row_specs(rows, tr) + _whole_specs(consts) + _row_specs(flat_cots, tr) + _row_specs(add_arrays, tr),
        out_specs=_row_specs(row_out, tr) + _whole_specs(const_out), out_shape=row_out + const_out,
        compiler_params=_params(("arbitrary",)),
    )(*rows, *consts, *flat_cots, *add_arrays)
    return list(outs)


def loss_head(y, target):
    t, d = y.shape
    tr = _pick(t, (256, 128, 64, 32, 16, 8))

    def body(y_ref, t_ref, dy_ref, l_ref):
        diff = y_ref[...] - t_ref[...]
        dy_ref[...] = diff * (1.0 / d)

        @pl.when(pl.program_id(0) == 0)
        def _():
            l_ref[...] = jnp.zeros_like(l_ref)

        l_ref[...] += 0.5 * jnp.sum(jnp.mean(diff * diff, axis=-1, keepdims=True), axis=0, keepdims=True)

    dy, l = pl.pallas_call(
        body, name="loss_head", grid=(t // tr,), in_specs=_row_specs([y, target], tr),
        out_specs=[pl.BlockSpec((tr, d), lambda i: (i, 0)), pl.BlockSpec((1, 1), lambda i: (0, 0))],
        out_shape=[jax.ShapeDtypeStruct((t, d), F32), jax.ShapeDtypeStruct((1, 1), F32)],
        compiler_params=_params(("arbitrary",)),
    )(y, target)
    return dy, l


ATT_PAD = N_LEFT_CHUNKS * CHUNK
MASKED = -1e30
ATT_UNROLL = 2


def _attn_chunk(q_c, k_b, v_b, bias, valid):
    s = jnp.where(valid, _dot1(q_c, k_b, NT) + bias, MASKED)
    e = jnp.exp(s - lax.stop_gradient(jnp.max(s, axis=-1, keepdims=True)))
    return _dot1(e / jnp.sum(e, axis=-1, keepdims=True), v_b, NN)


def _band_valid(c):
    return (c * CHUNK + lax.broadcasted_iota(jnp.int32, (1, BAND), 1)) >= ATT_PAD


def _head_spec(t):
    return pl.BlockSpec((None, t, HEAD_DIM), lambda h: (h, 0, 0))


def attention_fwd(q, k, v, bias):
    nh, t, _ = q.shape
    nchunks = t // CHUNK

    def body(q_ref, k_ref, v_ref, b_ref, o_ref, kp, vp):
        zeros = jnp.zeros((ATT_PAD, HEAD_DIM), F32)
        kp[pl.ds(0, ATT_PAD), :] = zeros
        vp[pl.ds(0, ATT_PAD), :] = zeros
        kp[pl.ds(ATT_PAD, t), :] = k_ref[...]
        vp[pl.ds(ATT_PAD, t), :] = v_ref[...]
        bias_h = b_ref[...]

        def step(c, carry):
            q0 = pl.multiple_of(c * CHUNK, CHUNK)
            o_ref[pl.ds(q0, CHUNK), :] = _attn_chunk(q_ref[pl.ds(q0, CHUNK), :], kp[pl.ds(q0, BAND), :],
                                                    vp[pl.ds(q0, BAND), :], bias_h, _band_valid(c))
            return carry

        lax.fori_loop(0, nchunks, step, 0, unroll=ATT_UNROLL)

    return pl.pallas_call(
        body, name="attention_fwd", grid=(nh,),
        in_specs=[_head_spec(t)] * 3 + [pl.BlockSpec((None, CHUNK, BAND), lambda h: (h, 0, 0))],
        out_specs=_head_spec(t), out_shape=jax.ShapeDtypeStruct((nh, t, HEAD_DIM), F32),
        scratch_shapes=[pltpu.VMEM((t + ATT_PAD, HEAD_DIM), F32)] * 2, compiler_params=_params(("parallel",)),
    )(q, k, v, bias)


def attention_bwd(q, k, v, bias, dout):
    nh, t, _ = q.shape
    nchunks = t // CHUNK

    def body(q_ref, k_ref, v_ref, b_ref, do_ref, dq_ref, dk_ref, dv_ref, db_ref, kp, vp, dkp, dvp):
        zeros = jnp.zeros((ATT_PAD, HEAD_DIM), F32)
        kp[pl.ds(0, ATT_PAD), :] = zeros
        vp[pl.ds(0, ATT_PAD), :] = zeros
        kp[pl.ds(ATT_PAD, t), :] = k_ref[...]
        vp[pl.ds(ATT_PAD, t), :] = v_ref[...]
        dkp[...] = jnp.zeros_like(dkp)
        dvp[...] = jnp.zeros_like(dvp)
        db_ref[...] = jnp.zeros_like(db_ref)
        bias_h = b_ref[...]

        def step(c, carry):
            q0 = pl.multiple_of(c * CHUNK, CHUNK)
            valid = _band_valid(c)
            _, vjp = jax.vjp(lambda a, b, cc, d: _attn_chunk(a, b, cc, d, valid), q_ref[pl.ds(q0, CHUNK), :],
                             kp[pl.ds(q0, BAND), :], vp[pl.ds(q0, BAND), :], bias_h)
            dq, dk, dv, db = vjp(do_ref[pl.ds(q0, CHUNK), :])
            dq_ref[pl.ds(q0, CHUNK), :] = dq
            dkp[pl.ds(q0, BAND), :] += dk
            dvp[pl.ds(q0, BAND), :] += dv
            db_ref[...] += db
            return carry

        lax.fori_loop(0, nchunks, step, 0, unroll=ATT_UNROLL)
        dk_ref[...] = dkp[pl.ds(ATT_PAD, t), :]
        dv_ref[...] = dvp[pl.ds(ATT_PAD, t), :]

    bias_spec = pl.BlockSpec((None, CHUNK, BAND), lambda h: (h, 0, 0))
    hs = jax.ShapeDtypeStruct((nh, t, HEAD_DIM), F32)
    return pl.pallas_call(
        body, name="attention_bwd", grid=(nh,), in_specs=[_head_spec(t)] * 3 + [bias_spec, _head_spec(t)],
        out_specs=[_head_spec(t)] * 3 + [bias_spec],
        out_shape=[hs, hs, hs, jax.ShapeDtypeStruct(bias.shape, F32)],
        scratch_shapes=[pltpu.VMEM((t + ATT_PAD, HEAD_DIM), F32)] * 4, compiler_params=_params(("parallel",)),
    )(q, k, v, bias, dout)


def _rel_onehot_t(i):
    j = lax.broadcasted_iota(jnp.int32, (N_REL, BAND), 1)
    r = lax.broadcasted_iota(jnp.int32, (N_REL, BAND), 0)
    idx = jnp.clip(i + ATT_PAD - j, -(CHUNK - 1), REL_CLIP) + (CHUNK - 1)
    return jnp.where(r == idx, 1.0, 0.0).astype(F32)


def bias_expand(rel):
    nh = rel.shape[0]

    def body(rel_ref, o_ref):
        o_ref[...] = _mm(rel_ref[...], _rel_onehot_t(pl.program_id(0)))

    return pl.pallas_call(
        body, name="bias_expand", grid=(CHUNK,), in_specs=[pl.BlockSpec((nh, N_REL), lambda i: (0, 0))],
        out_specs=pl.BlockSpec((None, nh, BAND), lambda i: (i, 0, 0)),
        out_shape=jax.ShapeDtypeStruct((CHUNK, nh, BAND), F32), compiler_params=_params(("parallel",)),
    )(rel)


def bias_reduce(dbias):
    nh = dbias.shape[1]

    def body(d_ref, o_ref):
        @pl.when(pl.program_id(0) == 0)
        def _():
            o_ref[...] = jnp.zeros_like(o_ref)

        o_ref[...] += _mm_nt(d_ref[...], _rel_onehot_t(pl.program_id(0)))

    return pl.pallas_call(
        body, name="bias_reduce", grid=(CHUNK,), in_specs=[pl.BlockSpec((None, nh, BAND), lambda i: (i, 0, 0))],
        out_specs=pl.BlockSpec((nh, N_REL), lambda i: (0, 0)), out_shape=jax.ShapeDtypeStruct((nh, N_REL), F32),
        compiler_params=_params(("arbitrary",)),
    )(dbias)


def _tri(n, strict):
    r = lax.broadcasted_iota(jnp.int32, (n, n), 0)
    c = lax.broadcasted_iota(jnp.int32, (n, n), 1)
    return (c < r) if strict else (c <= r)


def _rwkv_chunk(s, r, k, v, w, a, b):
    n = r.shape[0]
    lw = jnp.log(w)
    cum = _mm(jnp.where(_tri(n, False), 1.0, 0.0).astype(F32), lw)
    p_incl = jnp.exp(cum)
    p_inv = jnp.exp(-cum)
    a_t = a * jnp.exp(cum - lw)
    r_t = r * p_incl
    b_t = b * p_inv
    k_t = k * p_inv
    strict, incl = _tri(n, True), _tri(n, False)
    a_ab = jnp.where(strict, _dot3(a_t, b_t, NT), 0.0)
    a_ak = jnp.where(strict, _dot3(a_t, k_t, NT), 0.0)
    eye = jnp.where(incl & ~strict, 1.0, 0.0).astype(F32)
    inv = eye + a_ab
    power = a_ab
    for _ in range(int(math.log2(n)) - 1):
        power = _dot3(power, power, NN)
        inv = inv + _dot3(inv, power, NN)
    sa = _dot3(inv, _dot3(a_t, s, NT) + _dot3(a_ak, v, NN), NN)
    y = (_dot3(r_t, s, NT) + _dot3(jnp.where(incl, _dot3(r_t, b_t, NT), 0.0), sa, NN)
         + _dot3(jnp.where(incl, _dot3(r_t, k_t, NT), 0.0), v, NN))
    s_new = (s + _dot3(sa, b_t, TN) + _dot3(v, k_t, TN)) * p_incl[n - 1:n, :]
    return s_new, y


RWKV_TILE = 512
RWKV_HEADS = 4


def _rwkv_specs(nh, t, reverse):
    tile = min(RWKV_TILE, t)
    hb = RWKV_HEADS if nh % RWKV_HEADS == 0 else 1
    nt = t // tile
    per = tile // RWKV_CHUNK
    pos = (lambda h, i: (h, nt - 1 - i, 0)) if reverse else (lambda h, i: (h, i, 0))
    pos4 = (lambda h, i: (h, nt - 1 - i, 0, 0)) if reverse else (lambda h, i: (h, i, 0, 0))
    return (hb, nt, per, pl.BlockSpec((hb, tile, HEAD_DIM), pos), pl.BlockSpec((hb, per, HEAD_DIM, HEAD_DIM), pos4))


def rwkv_fwd(r, k, v, w, a, b):
    nh, t, _ = r.shape
    hb, nt, per, row_spec, s_spec = _rwkv_specs(nh, t, False)

    def body(r_ref, k_ref, v_ref, w_ref, a_ref, b_ref, y_ref, s_ref, state):
        @pl.when(pl.program_id(1) == 0)
        def _():
            state[...] = jnp.zeros_like(state)

        def step(c, states):
            rows = pl.ds(pl.multiple_of(c * RWKV_CHUNK, RWKV_CHUNK), RWKV_CHUNK)
            out = []
            for h in range(hb):
                s_ref[h, c] = states[h]
                s_new, y = _rwkv_chunk(states[h], r_ref[h, rows, :], k_ref[h, rows, :], v_ref[h, rows, :],
                                       w_ref[h, rows, :], a_ref[h, rows, :], b_ref[h, rows, :])
                y_ref[h, rows, :] = y
                out.append(s_new)
            return tuple(out)

        final = lax.fori_loop(0, per, step, tuple(state[h] for h in range(hb)))
        for h in range(hb):
            state[h] = final[h]

    return pl.pallas_call(
        body, name="rwkv_fwd", grid=(nh // hb, nt), in_specs=[row_spec] * 6, out_specs=[row_spec, s_spec],
        out_shape=[jax.ShapeDtypeStruct((nh, t, HEAD_DIM), F32),
                   jax.ShapeDtypeStruct((nh, t // RWKV_CHUNK, HEAD_DIM, HEAD_DIM), F32)],
        scratch_shapes=[pltpu.VMEM((hb, HEAD_DIM, HEAD_DIM), F32)], compiler_params=_params(("parallel", "arbitrary")),
    )(r, k, v, w, a, b)


def rwkv_bwd(r, k, v, w, a, b, states, dy):
    nh, t, _ = r.shape
    hb, nt, per, row_spec, s_spec = _rwkv_specs(nh, t, True)

    def body(r_ref, k_ref, v_ref, w_ref, a_ref, b_ref, s_ref, dy_ref, *rest):
        d_refs, dstate = rest[:6], rest[6]

        @pl.when(pl.program_id(1) == 0)
        def _():
            dstate[...] = jnp.zeros_like(dstate)

        def step(i, ds):
            c = per - 1 - i
            rows = pl.ds(pl.multiple_of(c * RWKV_CHUNK, RWKV_CHUNK), RWKV_CHUNK)
            out = []
            for h in range(hb):
                _, vjp = jax.vjp(_rwkv_chunk, s_ref[h, c], r_ref[h, rows, :], k_ref[h, rows, :], v_ref[h, rows, :],
                                 w_ref[h, rows, :], a_ref[h, rows, :], b_ref[h, rows, :])
                grads = vjp((ds[h], dy_ref[h, rows, :]))
                for d_ref, g in zip(d_refs, grads[1:]):
                    d_ref[h, rows, :] = g
                out.append(grads[0])
            return tuple(out)

        final = lax.fori_loop(0, per, step, tuple(dstate[h] for h in range(hb)))
        for h in range(hb):
            dstate[h] = final[h]

    hs = jax.ShapeDtypeStruct((nh, t, HEAD_DIM), F32)
    return pl.pallas_call(
        body, name="rwkv_bwd", grid=(nh // hb, nt), in_specs=[row_spec] * 6 + [s_spec, row_spec],
        out_specs=[row_spec] * 6, out_shape=[hs] * 6, scratch_shapes=[pltpu.VMEM((hb, HEAD_DIM, HEAD_DIM), F32)],
        compiler_params=_params(("parallel", "arbitrary")),
    )(r, k, v, w, a, b, states, dy)


SUBLANES = 8


def s5_scan(bu_re, bu_im, a_re, a_im, *, reverse=False, h_prev=None):
    t, n = bu_re.shape
    tt = _pick(t, (512, 256, 128, 64, 32, 16, 8))
    tc = _pick(n, (1024, 512, 256, 128))
    nt = t // tt
    with_da = h_prev is not None
    sign = -1.0 if reverse else 1.0

    def body(*refs):
        if with_da:
            br, bi, ar_ref, ai_ref, pr, pi, hr_ref, hi_ref, dar_ref, dai_ref, sr, si = refs
        else:
            br, bi, ar_ref, ai_ref, hr_ref, hi_ref, sr, si = refs
        ti = pl.program_id(1)

        @pl.when(ti == 0)
        def _():
            sr[...] = jnp.zeros_like(sr)
            si[...] = jnp.zeros_like(si)
            if with_da:
                dar_ref[...] = jnp.zeros_like(dar_ref)
                dai_ref[...] = jnp.zeros_like(dai_ref)

        ar = ar_ref[...]
        ai = ai_ref[...] * sign

        def group(gi, carry):
            hr, hi, dar, dai = carry
            g0 = pl.multiple_of((tt // SUBLANES - 1 - gi if reverse else gi) * SUBLANES, SUBLANES)
            rows = pl.ds(g0, SUBLANES)
            xr, xi = br[rows, :], bi[rows, :]
            if with_da:
                qr, qi = pr[rows, :], pi[rows, :]
            out_r, out_i = [None] * SUBLANES, [None] * SUBLANES
            for s in (range(SUBLANES - 1, -1, -1) if reverse else range(SUBLANES)):
                hr, hi = ar * hr - ai * hi + xr[s:s + 1, :], ar * hi + ai * hr + xi[s:s + 1, :]
                out_r[s], out_i[s] = hr, hi
                if with_da:
                    dar = dar + hr * qr[s:s + 1, :] + hi * qi[s:s + 1, :]
                    dai = dai + hi * qr[s:s + 1, :] - hr * qi[s:s + 1, :]
            hr_ref[rows, :] = jnp.concatenate(out_r, axis=0)
            hi_ref[rows, :] = jnp.concatenate(out_i, axis=0)
            return hr, hi, dar, dai

        zero = jnp.zeros((1, tc), F32)
        hr, hi, dar, dai = lax.fori_loop(0, tt // SUBLANES, group, (sr[...], si[...], zero, zero))
        sr[...] = hr
        si[...] = hi
        if with_da:
            dar_ref[...] += dar
            dai_ref[...] += dai

    tile = pl.BlockSpec((tt, tc), (lambda ci, ti: (nt - 1 - ti, ci)) if reverse else (lambda ci, ti: (ti, ci)))
    col = pl.BlockSpec((1, tc), lambda ci, ti: (0, ci))
    hs = jax.ShapeDtypeStruct((t, n), F32)
    cs = jax.ShapeDtypeStruct((1, n), F32)
    ins = [bu_re, bu_im, a_re, a_im] + (list(h_prev) if with_da else [])
    return pl.pallas_call(
        body, name="s5_scan_bwd" if reverse else "s5_scan_fwd", grid=(n // tc, nt),
        in_specs=[tile, tile, col, col] + ([tile, tile] if with_da else []),
        out_specs=[tile, tile] + ([col, col] if with_da else []), out_shape=[hs, hs] + ([cs, cs] if with_da else []),
        scratch_shapes=[pltpu.VMEM((1, tc), F32)] * 2, compiler_params=_params(("parallel", "arbitrary")),
    )(*ins)


N_CHIPS = 4
HBM_SPEC = pl.BlockSpec(memory_space=pl.ANY)


def _remote(src, dst, send_sem, recv_sem, device):
    return pltpu.make_async_remote_copy(src_ref=src, dst_ref=dst, send_sem=send_sem, recv_sem=recv_sem,
                                        device_id=device, device_id_type=pl.DeviceIdType.MESH)


def _comm_call(name, body, arrays, out_shape, n_sems):
    nb = len(arrays)
    return pl.pallas_call(
        body, name=name, in_specs=[HBM_SPEC] * nb, out_specs=[HBM_SPEC] * nb, out_shape=out_shape,
        scratch_shapes=[pltpu.SemaphoreType.DMA((nb, n_sems)), pltpu.SemaphoreType.DMA((nb, n_sems)),
                        pltpu.SemaphoreType.DMA((nb,))],
        compiler_params=pltpu.CompilerParams(has_side_effects=True),
    )(*arrays)


def gather_all(name, arrays):
    nb = len(arrays)

    def body(*refs):
        ins, outs = refs[:nb], refs[nb:2 * nb]
        send_sems, recv_sems, local_sems = refs[2 * nb:]
        x, y, c = (lax.axis_index(n) for n in MESH_AXES)
        me, sibling = 4 * x + 2 * y + c, 4 * x + 2 * y + 1 - c
        chips = [(1 - x, y), (x, 1 - y), (1 - x, 1 - y)]
        local = [pltpu.make_async_copy(ins[b], outs[b].at[me], local_sems.at[b]) for b in range(nb)]
        for cp in local:
            cp.start()

        def copy(b, k, slot, to, src=None):
            block = outs[b].at[slot]
            return _remote(block if src is None else src, block, send_sems.at[b, k], recv_sems.at[b, k], to)

        sent = []
        for b in range(nb):
            sent.append(copy(b, 0, me, (x, y, 1 - c), src=ins[b]))
            sent += [copy(b, 1 + j, me, (px, py, c), src=ins[b]) for j, (px, py) in enumerate(chips)]
        for cp in sent:
            cp.start()
        for j, (px, py) in enumerate(chips):
            slot = 4 * px + 2 * py + c
            for b in range(nb):
                copy(b, 1 + j, slot, (px, py, c)).wait_recv()
                passed = copy(b, 4 + j, slot, (x, y, 1 - c))
                passed.start()
                sent.append(passed)
        for b in range(nb):
            copy(b, 0, sibling, (x, y, 1 - c)).wait_recv()
            for j, (px, py) in enumerate(chips):
                copy(b, 4 + j, 4 * px + 2 * py + 1 - c, (x, y, 1 - c)).wait_recv()
        for cp in sent:
            cp.wait_send()
        for cp in local:
            cp.wait()

    out_shape = [jax.ShapeDtypeStruct((N_DEV,) + tuple(a.shape), a.dtype) for a in arrays]
    return _comm_call(name, body, arrays, out_shape, N_DEV - 1)


def pair_exchange(name, arrays):
    nb = len(arrays)

    def body(*refs):
        ins, outs = refs[:nb], refs[nb:2 * nb]
        send_sems, recv_sems, _ = refs[2 * nb:]
        x, y, c = (lax.axis_index(n) for n in MESH_AXES)
        copies = [_remote(ins[b].at[2 * z + 1 - c], outs[b].at[z], send_sems.at[b, z], recv_sems.at[b, z], (x, y, 1 - c))
                  for b in range(nb) for z in range(N_CHIPS)]
        for cp in copies:
            cp.start()
        for cp in copies:
            cp.wait_send()
        for cp in copies:
            cp.wait_recv()

    out_shape = [jax.ShapeDtypeStruct((N_CHIPS,) + tuple(a.shape[1:]), a.dtype) for a in arrays]
    return _comm_call(name, body, arrays, out_shape, N_CHIPS)


def pair_sum(name, pieces, from_sibling):
    _, r, c = pieces.shape
    tr = _row_tile(r, c * pieces.dtype.itemsize, 1 << 20)

    def body(p_ref, s_ref, o_ref):
        mine = p_ref[lax.axis_index("c")]
        o_ref[...] = (mine.astype(F32) + s_ref[...].astype(F32)).astype(o_ref.dtype)

    return pl.pallas_call(
        body, name=name, grid=(N_CHIPS, r // tr),
        in_specs=[pl.BlockSpec((None, 2, tr, c), lambda z, i: (z, 0, i, 0)), pl.BlockSpec((None, tr, c), lambda z, i: (z, i, 0))],
        out_specs=pl.BlockSpec((None, tr, c), lambda z, i: (z, i, 0)),
        out_shape=jax.ShapeDtypeStruct((N_CHIPS, r, c), pieces.dtype), compiler_params=_params(("parallel", "parallel")),
    )(pieces.reshape(N_CHIPS, 2, r, c), from_sibling)


def chip_exchange(name, arrays):
    nb = len(arrays)

    def body(*refs):
        ins, outs = refs[:nb], refs[nb:2 * nb]
        send_sems, recv_sems, local_sems = refs[2 * nb:]
        x, y, c = (lax.axis_index(n) for n in MESH_AXES)
        my_chip = 2 * x + y
        chips = [(1 - x, y), (x, 1 - y), (1 - x, 1 - y)]
        local = [pltpu.make_async_copy(ins[b].at[my_chip], outs[b].at[my_chip], local_sems.at[b]) for b in range(nb)]
        for cp in local:
            cp.start()
        copies = [_remote(ins[b].at[2 * px + py], outs[b].at[my_chip], send_sems.at[b, j], recv_sems.at[b, j], (px, py, c))
                  for b in range(nb) for j, (px, py) in enumerate(chips)]
        for cp in copies:
            cp.start()
        for cp in copies:
            cp.wait_send()
        for cp in copies:
            cp.wait_recv()
        for cp in local:
            cp.wait()

    out_shape = [jax.ShapeDtypeStruct(a.shape, a.dtype) for a in arrays]
    return _comm_call(name, body, arrays, out_shape, N_CHIPS - 1)


def adam_update(name, pieces, w, m, v):
    r, c = w.shape
    n_pieces = pieces.shape[0]
    tr = _row_tile(r, 4 * c, 1 << 19)

    def body(p_ref, w_ref, m_ref, v_ref, g_ref, d_ref, mo_ref, vo_ref):
        g = p_ref[0].astype(F32)
        for j in range(1, n_pieces):
            g = g + p_ref[j].astype(F32)
        m_new = ADAM_B1 * m_ref[...] + (1.0 - ADAM_B1) * g
        v_new = ADAM_B2 * v_ref[...] + (1.0 - ADAM_B2) * (g * g)
        m_hat = m_new / (1.0 - ADAM_B1 ** ADAM_STEP)
        v_hat = v_new / (1.0 - ADAM_B2 ** ADAM_STEP)
        g_ref[...] = g
        d_ref[...] = -ADAM_LR * (m_hat / (jnp.sqrt(v_hat) + ADAM_EPS) + ADAM_WD * w_ref[...])
        mo_ref[...] = m_new
        vo_ref[...] = v_new

    row = pl.BlockSpec((tr, c), lambda i: (i, 0))
    out = jax.ShapeDtypeStruct((r, c), F32)
    return pl.pallas_call(
        body, name=name, grid=(r // tr,), in_specs=[pl.BlockSpec((n_pieces, tr, c), lambda i: (0, i, 0)), row, row, row],
        out_specs=[row] * 4, out_shape=[out] * 4, compiler_params=_params(("parallel",)),
    )(pieces, w, m, v)


PACK_WIDTH = 1024
PACK_ROWS = 64


def _pack(arrays, dtype, lead=0):
    parts = []
    for a in arrays:
        head = a.shape[:lead]
        f = a.reshape(head + (-1,)).astype(dtype)
        pad = (-f.shape[-1]) % PACK_WIDTH
        if pad:
            f = jnp.pad(f, [(0, 0)] * lead + [(0, pad)])
        parts.append(f.reshape(head + (-1, PACK_WIDTH)))
    out = jnp.concatenate(parts, axis=lead)
    pad = (-out.shape[lead]) % PACK_ROWS
    if pad:
        out = jnp.pad(out, [(0, 0)] * lead + [(0, pad), (0, 0)])
    return out


def _unpack(packed, shapes, lead=0):
    head = packed.shape[:lead]
    out, row = [], 0
    for s in shapes:
        n = int(np.prod(s))
        rows = -(-n // PACK_WIDTH)
        chunk = lax.slice_in_dim(packed, row, row + rows, axis=lead).reshape(head + (-1,))
        out.append(lax.slice_in_dim(chunk, 0, n, axis=lead).reshape(head + tuple(s)))
        row += rows
    return out


def _to_natural(stacked, kind):
    if kind == "col":
        m = jnp.moveaxis(stacked, 0, -2)
        return m.reshape(m.shape[:-2] + (m.shape[-2] * m.shape[-1],))
    m = jnp.moveaxis(stacked, 0, -3)
    return m.reshape(m.shape[:-3] + (m.shape[-3] * m.shape[-2], m.shape[-1]))


def _to_stacked(natural, kind):
    if kind == "col":
        m = natural.reshape(natural.shape[:-1] + (N_DEV, natural.shape[-1] // N_DEV))
        return jnp.moveaxis(m, -2, 0)
    m = natural.reshape(natural.shape[:-2] + (N_DEV, natural.shape[-2] // N_DEV, natural.shape[-1]))
    return jnp.moveaxis(m, -3, 0)


def _f_rms(h, g):
    return (_rms(h, g).astype(BF16),)


def _f_swiglu(g, u):
    g, u = g.astype(F32), u.astype(F32)
    return ((g * _sigmoid(g) * u).astype(BF16),)


def _f_ple(h, pre, pp):
    return (h + _sigmoid(pre) * pp,)


def _head_rms(x, g):
    ms = _seg_sum(x * x, HEAD_DIM) * (1.0 / HEAD_DIM)
    return x * _seg_expand(lax.rsqrt(ms + RMS_EPS), HEAD_DIM) * _tile_lanes(g, x.shape[1] // HEAD_DIM)


def _f_attpre(q, k, q_gain, k_gain):
    return _head_rms(q, q_gain) * (HEAD_DIM ** -0.5), _head_rms(k, k_gain)


def _f_shift(z, z_prev, mu):
    return (z + (z_prev - z) * mu,)


def _f_rwkvpre(r, k, v, xw, xa, xg, w0, w_up, a0, a_up, g_up, k_k, k_a):
    del r, v
    w_log = -_softplus(-(w0 + _mm(jnp.tanh(xw), w_up))) - 0.5
    decay = jnp.exp(-jnp.exp(w_log))
    a = _sigmoid(a0 + _mm(xa, a_up))
    g = _mm(_sigmoid(xg), g_up)
    kk = k * k_k
    norm = jnp.sqrt(_seg_expand(_seg_sum(kk * kk, HEAD_DIM), HEAD_DIM))
    kk = kk / jnp.maximum(norm, 1e-12)
    return k * (1.0 + (a - 1.0) * k_a), decay, -kk, kk * a, g


def _f_rwkvpost(y, r, k, v, g, lnx_w, lnx_b, r_k):
    mean = _seg_expand(_seg_sum(y, HEAD_DIM) * (1.0 / HEAD_DIM), HEAD_DIM)
    yc = y - mean
    var = _seg_expand(_seg_sum(yc * yc, HEAD_DIM) * (1.0 / HEAD_DIM), HEAD_DIM)
    yn = yc * lax.rsqrt(var + GN_EPS) * lnx_w + lnx_b
    bonus = _seg_expand(_seg_sum(r * k * r_k, HEAD_DIM), HEAD_DIM) * v
    return ((yn + bonus) * g,)


def _f_s5disc(lam_re, lam_im, log_dt):
    dt = jnp.exp(log_dt)
    mag = jnp.exp(lam_re * dt)
    ab_re, ab_im = mag * jnp.cos(lam_im * dt), mag * jnp.sin(lam_im * dt)
    denom = lam_re * lam_re + lam_im * lam_im
    z_re = ((ab_re - 1.0) * lam_re + ab_im * lam_im) / denom
    z_im = (ab_im * lam_re - (ab_re - 1.0) * lam_im) / denom
    return ab_re, ab_im, z_re, z_im


def _f_s5b(z_re, z_im, b_re, b_im):
    return z_re * b_re - z_im * b_im, z_re * b_im + z_im * b_re


def _f_s5post(ypre, u, d_skip):
    return (_gelu_tanh(ypre + d_skip * u).astype(BF16),)


def _f_glu(h, z1, z2):
    return (h + z1 * _sigmoid(z2),)


def _heads(x):
    t = x.shape[0]
    return jnp.transpose(x.reshape(t, -1, HEAD_DIM), (1, 0, 2))


def _unheads(x):
    return jnp.transpose(x, (1, 0, 2)).reshape(x.shape[1], -1)


def _shift_down(x):
    return jnp.pad(x[:-1], ((1, 0), (0, 0)))


def _shift_up(x):
    return jnp.pad(x[1:], ((0, 1), (0, 0)))


def _block_diag(blocks):
    g, a, b = blocks.shape
    eye = jnp.eye(g, dtype=blocks.dtype)
    return (blocks[:, :, None, :] * eye[:, None, :, None]).reshape(g * a, g * b)


def _diag_blocks(dense, g):
    a, b = dense.shape[0] // g, dense.shape[1] // g
    return jnp.stack([dense[i * a:(i + 1) * a, i * b:(i + 1) * b] for i in range(g)], axis=0)


def _row_tile(t, width_bytes, budget=2 * 1024 * 1024):
    for tr in (512, 256, 128, 64, 32, 16, 8):
        if t % tr == 0 and tr * width_bytes <= budget:
            return tr
    return t


def _ffn_fwd(tag, h, norm_w, w_gate, w_up, w_down):
    t, d = h.shape
    f = w_gate.shape[1]
    n = rowwise(f"{tag}_norm", _f_rms, [h], [norm_w], [(d, BF16)], _row_tile(t, 4 * d))[0]
    g = matmul(f"{tag}_gate", n, w_gate, out_dtype=BF16)
    u = matmul(f"{tag}_up", n, w_up, out_dtype=BF16)
    a = rowwise(f"{tag}_act", _f_swiglu, [g, u], [], [(f, BF16)], _row_tile(t, 4 * f))[0]
    return matmul(f"{tag}_down", a, w_down, alpha=0.5, res=h), (h, n, g, u, a)


def _ffn_bwd(tag, dh2, saved, norm_w, w_gate, w_up, w_down):
    h, n, g, u, a = saved
    t, d = h.shape
    f = w_gate.shape[1]
    da = matmul(f"{tag}_da", dh2, w_down, tb=True, alpha=0.5, out_dtype=BF16)
    d_down = matmul(f"{tag}_dwdown", a, dh2, ta=True, alpha=0.5, out_dtype=BF16)
    dg, du = rowwise_vjp(f"{tag}_dact", _f_swiglu, [g, u], [], [da], _row_tile(t, 4 * f, 1 << 20), [True, True], [],
                         grad_dtypes={0: BF16, 1: BF16})
    dn = matmul(f"{tag}_dn_gate", dg, w_gate, tb=True)
    dn = matmul(f"{tag}_dn_up", du, w_up, tb=True, res=dn)
    d_gate = matmul(f"{tag}_dwgate", n, dg, ta=True, out_dtype=BF16)
    d_up = matmul(f"{tag}_dwup", n, du, ta=True, out_dtype=BF16)
    dh, d_norm = rowwise_vjp(f"{tag}_dnorm", _f_rms, [h], [norm_w], [dn], _row_tile(t, 4 * d), [True], [True],
                             add_to={0: dh2})
    return dh, d_norm, d_gate, d_up, d_down


def _ple_fwd(tag, h, norm_w, w_gate, w_proj, p_i):
    t, d = h.shape
    n = rowwise(f"{tag}_norm", _f_rms, [h], [norm_w], [(d, BF16)], _row_tile(t, 4 * d))[0]
    pre = matmul(f"{tag}_gate", n, w_gate)
    pp = matmul(f"{tag}_proj", p_i, w_proj)
    h2 = rowwise(f"{tag}_out", _f_ple, [h, pre, pp], [], [(d, F32)], _row_tile(t, 4 * d))[0]
    return h2, (h, n, pre, pp)


def _ple_bwd(tag, dh2, saved, norm_w, w_gate, w_proj, p_i):
    h, n, pre, pp = saved
    t, d = h.shape
    dpre, dpp = rowwise_vjp(f"{tag}_dout", _f_ple, [h, pre, pp], [], [dh2], _row_tile(t, 4 * d), [False, True, True], [],
                            grad_dtypes={1: BF16, 2: BF16})
    d_proj = matmul(f"{tag}_dwproj", p_i, dpp, ta=True, out_dtype=BF16)
    d_gate = matmul(f"{tag}_dwgate", n, dpre, ta=True, out_dtype=BF16)
    dn = matmul(f"{tag}_dn", dpre, w_gate, tb=True)
    dh, d_norm = rowwise_vjp(f"{tag}_dnorm", _f_rms, [h], [norm_w], [dn], _row_tile(t, 4 * d), [True], [True],
                             add_to={0: dh2})
    return dh, d_norm, d_gate, d_proj


def _ab_fwd(h, w):
    t, d = h.shape
    da = d // 2
    n = rowwise("ab_norm", _f_rms, [h], [w["mix_norm"]], [(d, BF16)], _row_tile(t, 4 * d))[0]
    proj = matmul("ab_in", n, w["ab_w_in"])
    q_raw, k_raw, v_att, z = proj[:, :da], proj[:, da:2 * da], proj[:, 2 * da:3 * da], proj[:, 3 * da:]
    tr = _row_tile(t, 4 * da, 1 << 19)
    qn, kn = rowwise("att_pre", _f_attpre, [q_raw, k_raw], [w["att_q_gain"], w["att_k_gain"]], [(da, F32)] * 2, tr)
    bias = jnp.transpose(bias_expand(w["att_rel_bias"]), (1, 0, 2))
    qh, kh, vh = _heads(qn), _heads(kn), _heads(v_att)
    att = _unheads(attention_fwd(qh, kh, vh, bias))
    z_prev = _shift_down(z)
    zz = rowwise("rwkv_shift", _f_shift, [z, z_prev], [w["rwkv_mu"]], [(z.shape[1], F32)], _row_tile(t, 4 * z.shape[1]))[0]
    o = [0, da, 2 * da, 3 * da, 3 * da + DECAY_LORA, 3 * da + DECAY_LORA + AAA_LORA, z.shape[1]]
    r, k, v, xw, xa, xg = (zz[:, o[i]:o[i + 1]] for i in range(6))
    pre_consts = [w[nm] for nm in ("rwkv_w0", "rwkv_w_up", "rwkv_a0", "rwkv_a_up", "rwkv_g_up", "rwkv_k_k", "rwkv_k_a")]
    k2, decay, ia, ib, g = rowwise("rwkv_pre", _f_rwkvpre, [r, k, v, xw, xa, xg], pre_consts, [(da, F32)] * 5, tr)
    heads = [_heads(a) for a in (r, k2, v, decay, ia, ib)]
    y_h, states = rwkv_fwd(*heads)
    y = _unheads(y_h)
    post_consts = [w["rwkv_lnx_w"], w["rwkv_lnx_b"], w["rwkv_r_k"]]
    rw = rowwise("rwkv_post", _f_rwkvpost, [y, r, k2, v, g], post_consts, [(da, F32)], tr)[0]
    cat = jnp.concatenate([att, rw], axis=1).astype(BF16)
    h2 = matmul("ab_out", cat, w["ab_w_out"], res=h)
    saved = dict(h=h, n=n, q_raw=q_raw, k_raw=k_raw, qh=qh, kh=kh, vh=vh, bias=bias, z=z, z_prev=z_prev,
                 rows=(r, k, v, xw, xa, xg), pre_consts=pre_consts, post=(y, r, k2, v, g), post_consts=post_consts,
                 heads=heads, states=states, cat=cat, tr=tr)
    return h2, saved


def _ab_bwd(dh2, s, w):
    h = s["h"]
    t, d = h.shape
    da = d // 2
    tr = s["tr"]
    grads = {}
    dcat = matmul("ab_dcat", dh2, w["ab_w_out"], tb=True)
    grads["ab_w_out"] = matmul("ab_dwout", s["cat"], dh2, ta=True, out_dtype=BF16)
    d_att, d_rw = dcat[:, :da], dcat[:, da:]
    dy, dr1, dk1, dv1, dg, grads["rwkv_lnx_w"], grads["rwkv_lnx_b"], grads["rwkv_r_k"] = rowwise_vjp(
        "rwkv_dpost", _f_rwkvpost, list(s["post"]), s["post_consts"], [d_rw], tr, [True] * 5, [True] * 3)
    d_heads = rwkv_bwd(*s["heads"], s["states"], _heads(dy))
    dr2, dk2, dv2, ddecay, dia, dib = (_unheads(a) for a in d_heads)
    pre = rowwise_vjp("rwkv_dpre", _f_rwkvpre, list(s["rows"]), s["pre_consts"], [(dk1, dk2), ddecay, dia, dib, dg], tr,
                      [True] * 6, [True] * 7, add_to={0: (dr1, dr2), 2: (dv1, dv2)})
    for nm, g in zip(("rwkv_w0", "rwkv_w_up", "rwkv_a0", "rwkv_a_up", "rwkv_g_up", "rwkv_k_k", "rwkv_k_a"), pre[6:]):
        grads[nm] = g
    dzz = jnp.concatenate(pre[:6], axis=1)
    trz = _row_tile(t, 4 * dzz.shape[1])
    grads["rwkv_mu"] = rowwise_vjp("rwkv_dmu", _f_shift, [s["z"], s["z_prev"]], [w["rwkv_mu"]], [dzz], trz,
                                   [False, False], [True])[0]
    dz = rowwise("rwkv_dshift", _f_shift, [dzz, _shift_up(dzz)], [w["rwkv_mu"]], [(dzz.shape[1], F32)], trz)[0]
    dqh, dkh, dvh, dbias = attention_bwd(s["qh"], s["kh"], s["vh"], s["bias"], _heads(d_att))
    grads["att_rel_bias"] = bias_reduce(jnp.transpose(dbias, (1, 0, 2)))
    dq_raw, dk_raw, grads["att_q_gain"], grads["att_k_gain"] = rowwise_vjp(
        "att_dpre", _f_attpre, [s["q_raw"], s["k_raw"]], [w["att_q_gain"], w["att_k_gain"]],
        [_unheads(dqh), _unheads(dkh)], tr, [True, True], [True, True])
    dproj = jnp.concatenate([dq_raw, dk_raw, _unheads(dvh), dz], axis=1).astype(BF16)
    dn = matmul("ab_dn", dproj, w["ab_w_in"], tb=True)
    grads["ab_w_in"] = matmul("ab_dwin", s["n"], dproj, ta=True, out_dtype=BF16)
    dh, grads["mix_norm"] = rowwise_vjp("ab_dnorm", _f_rms, [h], [w["mix_norm"]], [dn], _row_tile(t, 4 * d), [True], [True],
                                        add_to={0: dh2})
    return dh, grads


def _s5_fwd(h, w):
    t, d = h.shape
    n_groups, n_state = w["ssm_lambda_re"].shape
    gp = n_groups * n_state
    n = rowwise("s5_norm", _f_rms, [h], [w["mix_norm"]], [(d, BF16)], _row_tile(t, 4 * d))[0]
    u = matmul("s5_in", n, w["ssm_w_in"])
    disc_rows = [w["ssm_lambda_re"], w["ssm_lambda_im"], w["ssm_log_dt"]]
    ab_re, ab_im, z_re, z_im = rowwise("s5_disc", _f_s5disc, disc_rows, [], [(n_state, F32)] * 4, n_groups)
    b_rows = [z_re.reshape(gp, 1), z_im.reshape(gp, 1), w["ssm_b_re"], w["ssm_b_im"]]
    trb = _row_tile(gp, 512)
    bb_re, bb_im = rowwise("s5_bbar", _f_s5b, b_rows, [], [(SSM_GROUP, F32)] * 2, trb)
    to_dense = lambda bb: _block_diag(jnp.transpose(bb.reshape(n_groups, n_state, SSM_GROUP), (0, 2, 1)))
    bd_re, bd_im = to_dense(bb_re), to_dense(bb_im)
    cd_re = _block_diag(jnp.transpose(w["ssm_c_re"], (0, 2, 1)))
    cd_im = -_block_diag(jnp.transpose(w["ssm_c_im"], (0, 2, 1)))
    a_re, a_im = ab_re.reshape(1, gp), ab_im.reshape(1, gp)
    bu_re = matmul("s5_bu_re", u, bd_re)
    bu_im = matmul("s5_bu_im", u, bd_im)
    h_re, h_im = s5_scan(bu_re, bu_im, a_re, a_im)
    ypre = matmul("s5_y_re", h_re, cd_re)
    ypre = matmul("s5_y_im", h_im, cd_im, res=ypre)
    tru = _row_tile(t, 4 * u.shape[1])
    yg = rowwise("s5_post", _f_s5post, [ypre, u], [w["ssm_d"]], [(u.shape[1], BF16)], tru)[0]
    w_out1, w_out2 = w["ssm_w_out"][:, :d], w["ssm_w_out"][:, d:]
    z1 = matmul("s5_out1", yg, w_out1)
    z2 = matmul("s5_out2", yg, w_out2)
    h2 = rowwise("s5_glu", _f_glu, [h, z1, z2], [], [(d, F32)], _row_tile(t, 4 * d))[0]
    saved = dict(h=h, n=n, u=u, disc_rows=disc_rows, b_rows=b_rows, trb=trb, bd=(bd_re, bd_im), cd=(cd_re, cd_im),
                 a=(a_re, a_im), hs=(h_re, h_im), ypre=ypre, yg=yg, w_out=(w_out1, w_out2), z=(z1, z2), tru=tru)
    return h2, saved


def _s5_bwd(dh2, s, w):
    h, u = s["h"], s["u"]
    t, d = h.shape
    n_groups, n_state = w["ssm_lambda_re"].shape
    gp = n_groups * n_state
    grads = {}
    z1, z2 = s["z"]
    w_out1, w_out2 = s["w_out"]
    dz1, dz2 = rowwise_vjp("s5_dglu", _f_glu, [h, z1, z2], [], [dh2], _row_tile(t, 4 * d), [False, True, True], [],
                           grad_dtypes={1: BF16, 2: BF16})
    dyg = matmul("s5_dyg1", dz1, w_out1, tb=True)
    dyg = matmul("s5_dyg2", dz2, w_out2, tb=True, res=dyg)
    grads["ssm_w_out"] = jnp.concatenate([matmul("s5_dwout1", s["yg"], dz1, ta=True, out_dtype=BF16),
                                          matmul("s5_dwout2", s["yg"], dz2, ta=True, out_dtype=BF16)], axis=1)
    dypre, du1, grads["ssm_d"] = rowwise_vjp("s5_dpost", _f_s5post, [s["ypre"], u], [w["ssm_d"]], [dyg], s["tru"],
                                             [True, True], [True])
    cd_re, cd_im = s["cd"]
    h_re, h_im = s["hs"]
    dh_re = matmul("s5_dh_re", dypre, cd_re, tb=True)
    dh_im = matmul("s5_dh_im", dypre, cd_im, tb=True)
    dcd_re = matmul("s5_dc_re", h_re, dypre, ta=True)
    dcd_im = matmul("s5_dc_im", h_im, dypre, ta=True)
    a_re, a_im = s["a"]
    g_re, g_im, da_re, da_im = s5_scan(dh_re, dh_im, a_re, a_im, reverse=True,
                                       h_prev=(_shift_down(h_re), _shift_down(h_im)))
    bd_re, bd_im = s["bd"]
    du = matmul("s5_du_re", g_re, bd_re, tb=True, res=du1)
    du = matmul("s5_du_im", g_im, bd_im, tb=True, res=du)
    dbd_re = matmul("s5_db_re", u, g_re, ta=True)
    dbd_im = matmul("s5_db_im", u, g_im, ta=True)
    grads["ssm_w_in"] = matmul("s5_dwin", s["n"], du, ta=True, out_dtype=BF16)
    dn = matmul("s5_dn", du, w["ssm_w_in"], tb=True)
    dh, grads["mix_norm"] = rowwise_vjp("s5_dnorm", _f_rms, [h], [w["mix_norm"]], [dn], _row_tile(t, 4 * d), [True], [True],
                                        add_to={0: dh2})
    from_dense = lambda m: jnp.transpose(_diag_blocks(m, n_groups), (0, 2, 1)).reshape(gp, SSM_GROUP)
    dz_re, dz_im, grads["ssm_b_re"], grads["ssm_b_im"] = rowwise_vjp(
        "s5_dbbar", _f_s5b, s["b_rows"], [], [from_dense(dbd_re), from_dense(dbd_im)], s["trb"], [True] * 4, [])
    disc_cots = [da_re.reshape(n_groups, n_state), da_im.reshape(n_groups, n_state),
                 dz_re.reshape(n_groups, n_state), dz_im.reshape(n_groups, n_state)]
    grads["ssm_lambda_re"], grads["ssm_lambda_im"], grads["ssm_log_dt"] = rowwise_vjp(
        "s5_ddisc", _f_s5disc, s["disc_rows"], [], disc_cots, n_groups, [True] * 3, [])
    grads["ssm_c_re"] = jnp.transpose(_diag_blocks(dcd_re, n_groups), (0, 2, 1))
    grads["ssm_c_im"] = -jnp.transpose(_diag_blocks(dcd_im, n_groups), (0, 2, 1))
    return dh, grads


WEIGHTS = ["ffn1_norm", "ffn1_w_gate", "ffn1_w_up", "ffn1_w_down", "mix_norm", "ffn2_norm", "ffn2_w_gate", "ffn2_w_up",
           "ffn2_w_down", "ple_norm", "ple_w_gate", "ple_w_proj", "ab_w_in", "att_q_gain", "att_k_gain", "att_rel_bias",
           "rwkv_mu", "rwkv_w0", "rwkv_w_up", "rwkv_a0", "rwkv_a_up", "rwkv_g_up", "rwkv_k_k", "rwkv_k_a", "rwkv_r_k",
           "rwkv_lnx_w", "rwkv_lnx_b", "ab_w_out", "ssm_w_in", "ssm_lambda_re", "ssm_lambda_im", "ssm_log_dt", "ssm_b_re",
           "ssm_b_im", "ssm_c_re", "ssm_c_im", "ssm_d", "ssm_w_out"]
BIG = {"ffn1_w_gate": "col", "ffn1_w_up": "col", "ffn1_w_down": "row", "ffn2_w_gate": "col", "ffn2_w_up": "col",
       "ffn2_w_down": "row", "ple_w_gate": "row", "ple_w_proj": "col", "ab_w_in": "col", "ab_w_out": "row",
       "ssm_w_in": "row", "ssm_w_out": "col"}
SMALL_CUT = {"rwkv_w_up": "col", "rwkv_a_up": "col", "rwkv_g_up": "col", "ssm_d": "col"}
REPLICATED = [n for n in WEIGHTS if n not in BIG and n not in SMALL_CUT]


def kernel(x, p, ffn1_norm, ffn1_w_gate, ffn1_w_up, ffn1_w_down, mix_norm, ffn2_norm, ffn2_w_gate, ffn2_w_up, ffn2_w_down, ple_norm, ple_w_gate, ple_w_proj, ab_w_in, att_q_gain, att_k_gain, att_rel_bias, rwkv_mu, rwkv_w0, rwkv_w_up, rwkv_a0, rwkv_a_up, rwkv_g_up, rwkv_k_k, rwkv_k_a, rwkv_r_k, rwkv_lnx_w, rwkv_lnx_b, ab_w_out, ssm_w_in, ssm_lambda_re, ssm_lambda_im, ssm_log_dt, ssm_b_re, ssm_b_im, ssm_c_re, ssm_c_im, ssm_d, ssm_w_out, loss_target, m_ffn1_norm, m_ffn1_w_gate, m_ffn1_w_up, m_ffn1_w_down, m_mix_norm, m_ffn2_norm, m_ffn2_w_gate, m_ffn2_w_up, m_ffn2_w_down, m_ple_norm, m_ple_w_gate, m_ple_w_proj, m_ab_w_in, m_att_q_gain, m_att_k_gain, m_att_rel_bias, m_rwkv_mu, m_rwkv_w0, m_rwkv_w_up, m_rwkv_a0, m_rwkv_a_up, m_rwkv_g_up, m_rwkv_k_k, m_rwkv_k_a, m_rwkv_r_k, m_rwkv_lnx_w, m_rwkv_lnx_b, m_ab_w_out, m_ssm_w_in, m_ssm_lambda_re, m_ssm_lambda_im, m_ssm_log_dt, m_ssm_b_re, m_ssm_b_im, m_ssm_c_re, m_ssm_c_im, m_ssm_d, m_ssm_w_out, v_ffn1_norm, v_ffn1_w_gate, v_ffn1_w_up, v_ffn1_w_down, v_mix_norm, v_ffn2_norm, v_ffn2_w_gate, v_ffn2_w_up, v_ffn2_w_down, v_ple_norm, v_ple_w_gate, v_ple_w_proj, v_ab_w_in, v_att_q_gain, v_att_k_gain, v_att_rel_bias, v_rwkv_mu, v_rwkv_w0, v_rwkv_w_up, v_rwkv_a0, v_rwkv_a_up, v_rwkv_g_up, v_rwkv_k_k, v_rwkv_k_a, v_rwkv_r_k, v_rwkv_lnx_w, v_rwkv_lnx_b, v_ab_w_out, v_ssm_w_in, v_ssm_lambda_re, v_ssm_lambda_im, v_ssm_log_dt, v_ssm_b_re, v_ssm_b_im, v_ssm_c_re, v_ssm_c_im, v_ssm_d, v_ssm_w_out):
    vals = dict(locals())
    depth = ffn1_norm.shape[0]
    n_groups, n_state = ssm_lambda_re.shape[1:]

    cut_names = list(BIG) + list(SMALL_CUT)
    kinds = {**BIG, **SMALL_CUT}
    travel = {n: (BF16 if n in BIG else F32) for n in cut_names}

    def rows(a, lead=0):
        return a.reshape(a.shape[:lead] + (-1, a.shape[-1]))

    gathered = gather_all("gather_weights", [rows(vals[n]).astype(travel[n]) for n in cut_names])
    full = {n: _to_natural(g.reshape((N_DEV,) + vals[n].shape), kinds[n]) for n, g in zip(cut_names, gathered)}

    def row(name, j):
        return vals[name][j].reshape(1, -1)

    def ffn_weights(which, i):
        return (row(f"{which}_norm", i), full[f"{which}_w_gate"][i], full[f"{which}_w_up"][i], full[f"{which}_w_down"][i])

    def mixer_weights(i):
        j = i // 2
        if i % 2 == 0:
            w = {n: row(n, j) for n in ("att_q_gain", "att_k_gain", "rwkv_mu", "rwkv_w0", "rwkv_a0", "rwkv_k_k", "rwkv_k_a",
                                        "rwkv_r_k", "rwkv_lnx_w", "rwkv_lnx_b")}
            w.update({n: full[n][j] for n in ("ab_w_in", "ab_w_out", "rwkv_w_up", "rwkv_a_up", "rwkv_g_up")})
            w["att_rel_bias"] = att_rel_bias[j]
        else:
            w = {"ssm_lambda_re": ssm_lambda_re[j], "ssm_lambda_im": ssm_lambda_im[j],
                 "ssm_log_dt": ssm_log_dt[j].reshape(n_groups, 1),
                 "ssm_b_re": ssm_b_re[j].reshape(n_groups * n_state, -1),
                 "ssm_b_im": ssm_b_im[j].reshape(n_groups * n_state, -1),
                 "ssm_c_re": ssm_c_re[j], "ssm_c_im": ssm_c_im[j], "ssm_d": full["ssm_d"][j].reshape(1, -1),
                 "ssm_w_in": full["ssm_w_in"][j], "ssm_w_out": full["ssm_w_out"][j]}
        w["mix_norm"] = row("mix_norm", i)
        return w

    def ple_weights(i):
        return (row("ple_norm", i), full["ple_w_gate"][i], full["ple_w_proj"][i], p[i, 0])

    h = x[0]
    saved = []
    for i in range(depth):
        h, s1 = _ffn_fwd(f"l{i}_ffn1", h, *ffn_weights("ffn1", i))
        h, sm = (_ab_fwd if i % 2 == 0 else _s5_fwd)(h, mixer_weights(i))
        h, s2 = _ffn_fwd(f"l{i}_ffn2", h, *ffn_weights("ffn2", i))
        h, sp = _ple_fwd(f"l{i}_ple", h, *ple_weights(i))
        saved.append((s1, sm, s2, sp))
    dh, loss_part = loss_head(h, loss_target[0])

    per_layer = {n: [] for n in WEIGHTS}
    for i in reversed(range(depth)):
        s1, sm, s2, sp = saved[i]
        dh, d_norm, d_gate, d_proj = _ple_bwd(f"l{i}_ple", dh, sp, *ple_weights(i))
        for n, g in (("ple_norm", d_norm), ("ple_w_gate", d_gate), ("ple_w_proj", d_proj)):
            per_layer[n].insert(0, g)
        dh, d_norm, d_gate, d_up, d_down = _ffn_bwd(f"l{i}_ffn2", dh, s2, *ffn_weights("ffn2", i))
        for n, g in (("ffn2_norm", d_norm), ("ffn2_w_gate", d_gate), ("ffn2_w_up", d_up), ("ffn2_w_down", d_down)):
            per_layer[n].insert(0, g)
        dh, mixer_grads = (_ab_bwd if i % 2 == 0 else _s5_bwd)(dh, sm, mixer_weights(i))
        for n, g in mixer_grads.items():
            per_layer[n].insert(0, g)
        dh, d_norm, d_gate, d_up, d_down = _ffn_bwd(f"l{i}_ffn1", dh, s1, *ffn_weights("ffn1", i))
        for n, g in (("ffn1_norm", d_norm), ("ffn1_w_gate", d_gate), ("ffn1_w_up", d_up), ("ffn1_w_down", d_down)):
            per_layer[n].insert(0, g)
    grad_x = dh[None]

    def whole_grad(n):
        g = jnp.stack(per_layer[n], axis=0)
        lead = vals[n].shape[0]
        if n in BIG or n in SMALL_CUT:
            return g.reshape((lead,) + g.shape[1:]) if n != "ssm_d" else g.reshape(lead, -1)
        return g.reshape(vals[n].shape)

    pieces = [rows(_to_stacked(whole_grad(n), kinds[n]).astype(travel[n]), lead=1) for n in cut_names]
    from_sibling = pair_exchange("reduce_pair", pieces)
    chip_sums = [pair_sum(f"pair_sum_{n}", a, b) for n, a, b in zip(cut_names, pieces, from_sibling)]
    per_chip = chip_exchange("reduce_chips", chip_sums)
    rep_all = gather_all("gather_replicated", [_pack([whole_grad(n) for n in REPLICATED] + [loss_part], F32)])[0]

    out = {}
    for n, got in zip(cut_names, per_chip):
        state = [rows(vals[pre + n]) for pre in ("", "m_", "v_")]
        out[n] = tuple(r.reshape(vals[n].shape) for r in adam_update(f"adam_{n}", got, *state))
    zero = jnp.zeros((1, 1), F32)
    state = [_pack([vals[pre + n] for n in REPLICATED] + [zero], F32) for pre in ("", "m_", "v_")]
    shapes = [vals[n].shape for n in REPLICATED] + [zero.shape]
    results = [_unpack(r, shapes) for r in adam_update("adam_replicated", rep_all, *state)]
    for q, n in enumerate(REPLICATED):
        out[n] = tuple(r[q] for r in results)
    loss = results[0][-1].reshape(())
    return (loss, grad_x, *[out[n][0] for n in WEIGHTS], *[out[n][1] for n in WEIGHTS], *[out[n][2] for n in WEIGHTS],
            *[out[n][3] for n in WEIGHTS])
```

```python
import functools
import math

import jax
import jax.numpy as jnp
import numpy as np
from jax import lax
from jax.experimental import pallas as pl
from jax.experimental.pallas import tpu as pltpu

F32 = jnp.float32
BF16 = jnp.bfloat16
HIGHEST = lax.Precision.HIGHEST
MESH_AXES = ("x", "y", "c")
N_DEV = 8

CHUNK = 64
N_LEFT_CHUNKS = 8
BAND = (N_LEFT_CHUNKS + 1) * CHUNK
HEAD_DIM = 64
REL_CLIP = 128
N_REL = (CHUNK - 1) + REL_CLIP + 1
DECAY_LORA = 64
AAA_LORA = 64
GATE_LORA = 128
SSM_GROUP = 16
SSM_STATE = 64
RMS_EPS = 1e-6
GN_EPS = 64e-5
ADAM_LR = 0.001
ADAM_B1 = 0.9
ADAM_B2 = 0.999
ADAM_EPS = 1e-08
ADAM_WD = 0.01
ADAM_STEP = 10

RWKV_CHUNK = 64
VMEM_LIMIT = 56 * 1024 * 1024
LANES = 128


def _params(semantics):
    return pltpu.CompilerParams(dimension_semantics=semantics, vmem_limit_bytes=VMEM_LIMIT)


def _pick(n, prefs):
    for t in prefs:
        if n % t == 0:
            return t
    return n


def _dot(a, b, dims):
    return lax.dot_general(a, b, (dims, ((), ())), precision=HIGHEST, preferred_element_type=F32)


def _mm(a, b):
    return _dot(a, b, ((1,), (0,)))


def _mm_nt(a, b):
    return _dot(a, b, ((1,), (1,)))


def _mm_tn(a, b):
    return _dot(a, b, ((0,), (0,)))


def _split2(x):
    hi = x.astype(BF16)
    return hi, (x - hi.astype(F32)).astype(BF16)


def _dot3_raw(a, b, dims):
    a_hi, a_lo = _split2(a)
    b_hi, b_lo = _split2(b)
    dot = lambda p, q: lax.dot_general(p, q, (dims, ((), ())), preferred_element_type=F32)
    return dot(a_hi, b_hi) + (dot(a_hi, b_lo) + dot(a_lo, b_hi))


def _dot1_raw(a, b, dims):
    return lax.dot_general(a.astype(BF16), b.astype(BF16), (dims, ((), ())), preferred_element_type=F32)


NN, NT, TN = ((1,), (0,)), ((1,), (1,)), ((0,), (0,))


def _make_dot(raw):
    @functools.partial(jax.custom_vjp, nondiff_argnums=(2,))
    def dot(a, b, dims):
        return raw(a, b, dims)

    def fwd(a, b, dims):
        return raw(a, b, dims), (a, b)

    def bwd(dims, saved, g):
        a, b = saved
        if dims == NN:
            return raw(g, b, NT), raw(a, g, TN)
        if dims == NT:
            return raw(g, b, NN), raw(g, a, TN)
        return raw(b, g, NT), raw(a, g, NN)

    dot.defvjp(fwd, bwd)
    return dot


_dot3 = _make_dot(_dot3_raw)
_dot1 = _make_dot(_dot1_raw)


def _dot_ind_raw(x, ind, dims):
    hi = x.astype(BF16)
    rest = x - hi.astype(F32)
    mid = rest.astype(BF16)
    lo = (rest - mid.astype(F32)).astype(BF16)
    ind = ind.astype(BF16)
    dot = lambda p: lax.dot_general(p, ind, (dims, ((), ())), preferred_element_type=F32)
    return dot(hi) + (dot(mid) + dot(lo))


@jax.custom_vjp
def _mm_ind(x, ind):
    return _dot_ind_raw(x, ind, NN)


def _mm_ind_fwd(x, ind):
    return _dot_ind_raw(x, ind, NN), ind


def _mm_ind_bwd(ind, g):
    return _dot_ind_raw(g, ind, NT), jnp.zeros_like(ind)


_mm_ind.defvjp(_mm_ind_fwd, _mm_ind_bwd)


def _sigmoid(x):
    return 1.0 / (1.0 + jnp.exp(-x))


def _softplus(x):
    return jnp.maximum(x, 0.0) + jnp.log(1.0 + jnp.exp(-jnp.abs(x)))


def _gelu_tanh(x):
    return 0.5 * x * (1.0 + jnp.tanh(math.sqrt(2.0 / math.pi) * (x + 0.044715 * (x * x * x))))


def _seg_indicator(n, seg):
    r = lax.broadcasted_iota(jnp.int32, (n, n // seg), 0)
    c = lax.broadcasted_iota(jnp.int32, (n, n // seg), 1)
    return jnp.where((r >= c * seg) & (r < (c + 1) * seg), 1.0, 0.0).astype(F32)


def _seg_indicator_t(n, seg):
    c = lax.broadcasted_iota(jnp.int32, (n // seg, n), 0)
    r = lax.broadcasted_iota(jnp.int32, (n // seg, n), 1)
    return jnp.where((r >= c * seg) & (r < (c + 1) * seg), 1.0, 0.0).astype(F32)


def _seg_sum(x, seg):
    return _mm_ind(x, _seg_indicator(x.shape[1], seg))


def _seg_expand(s, seg):
    return _mm_ind(s, _seg_indicator_t(s.shape[1] * seg, seg))


def _tile_lanes(g, reps):
    n = g.shape[1]
    r = lax.broadcasted_iota(jnp.int32, (n, n * reps), 0)
    c = lax.broadcasted_iota(jnp.int32, (n, n * reps), 1)
    return _mm_ind(g, jnp.where((c & (n - 1)) == r, 1.0, 0.0).astype(F32))


def _rms(x, g):
    return x * lax.rsqrt(jnp.mean(x * x, axis=-1, keepdims=True) + RMS_EPS) * g


MATMUL_VMEM_BUDGET = 40 * 1024 * 1024
MATMUL_MAX_TILE = 2048
HBM_BYTES_PER_S = 1.2e12
MXU_FLOPS_PER_S = 8e14
GRID_STEP_S = 0.35e-6


def _tile_candidates(dim):
    c = [d for d in range(LANES, min(dim, MATMUL_MAX_TILE) + 1, LANES) if dim % d == 0]
    return c or [dim]


def _matmul_tiles(m, n, k, a_bytes, b_bytes, out_bytes, res_bytes):
    best = None
    for tm in _tile_candidates(m):
        for tn in _tile_candidates(n):
            for tk in _tile_candidates(k):
                casts = (tm * tk * 2 if a_bytes > 2 else 0) + (tk * tn * 2 if b_bytes > 2 else 0)
                vmem = (2 * (tm * tk * a_bytes + tk * tn * b_bytes + tm * tn * (out_bytes + res_bytes))
                        + 2 * tm * tn * 4 + casts)
                if vmem > MATMUL_VMEM_BUDGET:
                    continue
                steps = (m // tm) * (n // tn) * (k // tk)
                a_reads = 1 if k == tk else n // tn
                traffic = (m * k * a_bytes * a_reads + k * n * b_bytes * (m // tm) + m * n * (out_bytes + res_bytes))
                dma_s = traffic / HBM_BYTES_PER_S
                cost = max(dma_s, 2.0 * m * n * k / MXU_FLOPS_PER_S) + 0.2 * dma_s + steps * GRID_STEP_S
                if best is None or cost < best[0]:
                    best = (cost, tm, tn, tk)
    return best[1:]


def matmul(name, a, b, *, ta=False, tb=False, alpha=1.0, res=None, out_dtype=F32):
    m, k = (a.shape[1], a.shape[0]) if ta else a.shape
    n = b.shape[0] if tb else b.shape[1]
    assert k == (b.shape[1] if tb else b.shape[0]), (name, a.shape, b.shape)
    tm, tn, tk = _matmul_tiles(m, n, k, a.dtype.itemsize, b.dtype.itemsize, jnp.dtype(out_dtype).itemsize,
                               0 if res is None else res.dtype.itemsize)
    nk = k // tk
    dims = ((0 if ta else 1,), (1 if tb else 0,))

    def body(*refs):
        if res is None:
            a_ref, b_ref, o_ref, acc_ref = refs
            res_ref = None
        else:
            a_ref, b_ref, res_ref, o_ref, acc_ref = refs
        kk = pl.program_id(2)

        @pl.when(kk == 0)
        def _():
            acc_ref[...] = jnp.zeros_like(acc_ref)

        acc_ref[...] += lax.dot_general(a_ref[...].astype(BF16), b_ref[...].astype(BF16), (dims, ((), ())),
                                        preferred_element_type=F32)

        @pl.when(kk == nk - 1)
        def _():
            out = acc_ref[...] * alpha
            if res_ref is not None:
                out = out + res_ref[...].astype(F32)
            o_ref[...] = out.astype(o_ref.dtype)

    a_spec = pl.BlockSpec((tk, tm), lambda i, j, kk: (kk, i)) if ta else pl.BlockSpec((tm, tk), lambda i, j, kk: (i, kk))
    b_spec = pl.BlockSpec((tn, tk), lambda i, j, kk: (j, kk)) if tb else pl.BlockSpec((tk, tn), lambda i, j, kk: (kk, j))
    o_spec = pl.BlockSpec((tm, tn), lambda i, j, kk: (i, j))
    in_specs = [a_spec, b_spec] + ([o_spec] if res is not None else [])
    args = (a, b) + ((res,) if res is not None else ())
    return pl.pallas_call(
        body, name=name, grid=(m // tm, n // tn, nk), in_specs=in_specs, out_specs=o_spec,
        out_shape=jax.ShapeDtypeStruct((m, n), out_dtype), scratch_shapes=[pltpu.VMEM((tm, tn), F32)],
        compiler_params=_params(("parallel", "parallel", "arbitrary")),
    )(*args)


def _row_specs(arrays, tr):
    return [pl.BlockSpec((tr, a.shape[1]), lambda i: (i, 0)) for a in arrays]


def _whole_specs(arrays):
    return [pl.BlockSpec(a.shape, lambda i: (0, 0)) for a in arrays]


def rowwise(name, fn, rows, consts, outs, tr):
    t = rows[0].shape[0]
    nr, nc = len(rows), len(consts)

    def body(*refs):
        vals = [r[...] for r in refs[:nr + nc]]
        res = fn(*vals)
        for o_ref, o in zip(refs[nr + nc:], res):
            o_ref[...] = o.astype(o_ref.dtype)

    out_shape = [jax.ShapeDtypeStruct((t, w), d) for w, d in outs]
    return pl.pallas_call(
        body, name=name, grid=(t // tr,), in_specs=_row_specs(rows, tr) + _whole_specs(consts),
        out_specs=_row_specs(out_shape, tr), out_shape=out_shape, compiler_params=_params(("parallel",)),
    )(*rows, *consts)


def rowwise_vjp(name, fn, rows, consts, cots, tr, row_grad, const_grad, add_to=None, grad_dtypes=None):
    t = rows[0].shape[0]
    nr, nc = len(rows), len(consts)
    cot_groups = [c if isinstance(c, (tuple, list)) else (c,) for c in cots]
    flat_cots = [a for g in cot_groups for a in g]
    add_to = {i: (a if isinstance(a, (tuple, list)) else (a,)) for i, a in (add_to or {}).items()}
    add_idx = [(i, q) for i in sorted(add_to) for q in range(len(add_to[i]))]
    add_arrays = [add_to[i][q] for i, q in add_idx]
    r_idx = [i for i in range(nr) if row_grad[i]]
    c_idx = [i for i in range(nc) if const_grad[i]]
    grad_dtypes = grad_dtypes or {}
    n_in = nr + nc + len(flat_cots) + len(add_arrays)

    def body(*refs):
        vals = [r[...] for r in refs[:nr + nc]]
        pos = nr + nc
        cts = []
        for g in cot_groups:
            s = refs[pos][...].astype(F32)
            for q in range(1, len(g)):
                s = s + refs[pos + q][...].astype(F32)
            cts.append(s)
            pos += len(g)
        adds = {}
        for n_add, (i, _) in enumerate(add_idx):
            term = refs[pos + n_add][...].astype(F32)
            adds[i] = adds[i] + term if i in adds else term
        diff =[vals[i] for i in r_idx] + [vals[nr + i] for i in c_idx]

        def f(*d):
            full = list(vals)
            for q, i in enumerate(r_idx):
                full[i] = d[q]
            for q, i in enumerate(c_idx):
                full[nr + i] = d[len(r_idx) + q]
            return tuple(fn(*full))

        prim, vjp = jax.vjp(f, *diff)
        grads = vjp(tuple(c.astype(p.dtype) for c, p in zip(cts, prim)))
        o_refs = refs[n_in:]
        for q, i in enumerate(r_idx):
            g = grads[q].astype(F32)
            if i in adds:
                g = g + adds[i].astype(F32)
            o_refs[q][...] = g.astype(o_refs[q].dtype)
        step = pl.program_id(0)
        for q, i in enumerate(c_idx):
            o_ref = o_refs[len(r_idx) + q]

            @pl.when(step == 0)
            def _(o_ref=o_ref):
                o_ref[...] = jnp.zeros_like(o_ref)

            o_ref[...] += grads[len(r_idx) + q].astype(F32)

    row_out = [jax.ShapeDtypeStruct(rows[i].shape, grad_dtypes.get(i, F32)) for i in r_idx]
    const_out = [jax.ShapeDtypeStruct(consts[i].shape, F32) for i in c_idx]
    outs = pl.pallas_call(
        body, name=name, grid=(t // tr,),
        in_specs=_row_specs(rows, tr) + _whole_specs(consts) + _row_specs(flat_cots, tr) + _row_specs(add_arrays, tr),
        out_specs=_row_specs(row_out, tr) + _whole_specs(const_out), out_shape=row_out + const_out,
        compiler_params=_params(("arbitrary",)),
    )(*rows, *consts, *flat_cots, *add_arrays)
    return list(outs)


def loss_head(y, target):
    t, d = y.shape
    tr = _pick(t, (256, 128, 64, 32, 16, 8))

    def body(y_ref, t_ref, dy_ref, l_ref):
        diff = y_ref[...] - t_ref[...]
        dy_ref[...] = diff * (1.0 / d)

        @pl.when(pl.program_id(0) == 0)
        def _():
            l_ref[...] = jnp.zeros_like(l_ref)

        l_ref[...] += 0.5 * jnp.sum(jnp.mean(diff * diff, axis=-1, keepdims=True), axis=0, keepdims=True)

    dy, l = pl.pallas_call(
        body, name="loss_head", grid=(t // tr,), in_specs=_row_specs([y, target], tr),
        out_specs=[pl.BlockSpec((tr, d), lambda i: (i, 0)), pl.BlockSpec((1, 1), lambda i: (0, 0))],
        out_shape=[jax.ShapeDtypeStruct((t, d), F32), jax.ShapeDtypeStruct((1, 1), F32)],
        compiler_params=_params(("arbitrary",)),
    )(y, target)
    return dy, l


ATT_PAD = N_LEFT_CHUNKS * CHUNK
MASKED = -1e30
ATT_UNROLL = 2


def _attn_chunk(q_c, k_b, v_b, bias, valid):
    s = jnp.where(valid, _dot1(q_c, k_b, NT) + bias, MASKED)
    e = jnp.exp(s - lax.stop_gradient(jnp.max(s, axis=-1, keepdims=True)))
    return _dot1(e / jnp.sum(e, axis=-1, keepdims=True), v_b, NN)


def _band_valid(c):
    return (c * CHUNK + lax.broadcasted_iota(jnp.int32, (1, BAND), 1)) >= ATT_PAD


def _head_spec(t):
    return pl.BlockSpec((None, t, HEAD_DIM), lambda h: (h, 0, 0))


def attention_fwd(q, k, v, bias):
    nh, t, _ = q.shape
    nchunks = t // CHUNK

    def body(q_ref, k_ref, v_ref, b_ref, o_ref, kp, vp):
        zeros = jnp.zeros((ATT_PAD, HEAD_DIM), F32)
        kp[pl.ds(0, ATT_PAD), :] = zeros
        vp[pl.ds(0, ATT_PAD), :] = zeros
        kp[pl.ds(ATT_PAD, t), :] = k_ref[...]
        vp[pl.ds(ATT_PAD, t), :] = v_ref[...]
        bias_h = b_ref[...]

        def step(c, carry):
            q0 = pl.multiple_of(c * CHUNK, CHUNK)
            o_ref[pl.ds(q0, CHUNK), :] = _attn_chunk(q_ref[pl.ds(q0, CHUNK), :], kp[pl.ds(q0, BAND), :],
                                                    vp[pl.ds(q0, BAND), :], bias_h, _band_valid(c))
            return carry

        lax.fori_loop(0, nchunks, step, 0, unroll=ATT_UNROLL)

    return pl.pallas_call(
        body, name="attention_fwd", grid=(nh,),
        in_specs=[_head_spec(t)] * 3 + [pl.BlockSpec((None, CHUNK, BAND), lambda h: (h, 0, 0))],
        out_specs=_head_spec(t), out_shape=jax.ShapeDtypeStruct((nh, t, HEAD_DIM), F32),
        scratch_shapes=[pltpu.VMEM((t + ATT_PAD, HEAD_DIM), F32)] * 2, compiler_params=_params(("parallel",)),
    )(q, k, v, bias)


def attention_bwd(q, k, v, bias, dout):
    nh, t, _ = q.shape
    nchunks = t // CHUNK

    def body(q_ref, k_ref, v_ref, b_ref, do_ref, dq_ref, dk_ref, dv_ref, db_ref, kp, vp, dkp, dvp):
        zeros = jnp.zeros((ATT_PAD, HEAD_DIM), F32)
        kp[pl.ds(0, ATT_PAD), :] = zeros
        vp[pl.ds(0, ATT_PAD), :] = zeros
        kp[pl.ds(ATT_PAD, t), :] = k_ref[...]
        vp[pl.ds(ATT_PAD, t), :] = v_ref[...]
        dkp[...] = jnp.zeros_like(dkp)
        dvp[...] = jnp.zeros_like(dvp)
        db_ref[...] = jnp.zeros_like(db_ref)
        bias_h = b_ref[...]

        def step(c, carry):
            q0 = pl.multiple_of(c * CHUNK, CHUNK)
            valid = _band_valid(c)
            _, vjp = jax.vjp(lambda a, b, cc, d: _attn_chunk(a, b, cc, d, valid), q_ref[pl.ds(q0, CHUNK), :],
                             kp[pl.ds(q0, BAND), :], vp[pl.ds(q0, BAND), :], bias_h)
            dq, dk, dv, db = vjp(do_ref[pl.ds(q0, CHUNK), :])
            dq_ref[pl.ds(q0, CHUNK), :] = dq
            dkp[pl.ds(q0, BAND), :] += dk
            dvp[pl.ds(q0, BAND), :] += dv
            db_ref[...] += db
            return carry

        lax.fori_loop(0, nchunks, step, 0, unroll=ATT_UNROLL)
        dk_ref[...] = dkp[pl.ds(ATT_PAD, t), :]
        dv_ref[...] = dvp[pl.ds(ATT_PAD, t), :]

    bias_spec = pl.BlockSpec((None, CHUNK, BAND), lambda h: (h, 0, 0))
    hs = jax.ShapeDtypeStruct((nh, t, HEAD_DIM), F32)
    return pl.pallas_call(
        body, name="attention_bwd", grid=(nh,), in_specs=[_head_spec(t)] * 3 + [bias_spec, _head_spec(t)],
        out_specs=[_head_spec(t)] * 3 + [bias_spec],
        out_shape=[hs, hs, hs, jax.ShapeDtypeStruct(bias.shape, F32)],
        scratch_shapes=[pltpu.VMEM((t + ATT_PAD, HEAD_DIM), F32)] * 4, compiler_params=_params(("parallel",)),
    )(q, k, v, bias, dout)


def _rel_onehot_t(i):
    j = lax.broadcasted_iota(jnp.int32, (N_REL, BAND), 1)
    r = lax.broadcasted_iota(jnp.int32, (N_REL, BAND), 0)
    idx = jnp.clip(i + ATT_PAD - j, -(CHUNK - 1), REL_CLIP) + (CHUNK - 1)
    return jnp.where(r == idx, 1.0, 0.0).astype(F32)


def bias_expand(rel):
    nh = rel.shape[0]

    def body(rel_ref, o_ref):
        o_ref[...] = _mm(rel_ref[...], _rel_onehot_t(pl.program_id(0)))

    return pl.pallas_call(
        body, name="bias_expand", grid=(CHUNK,), in_specs=[pl.BlockSpec((nh, N_REL), lambda i: (0, 0))],
        out_specs=pl.BlockSpec((None, nh, BAND), lambda i: (i, 0, 0)),
        out_shape=jax.ShapeDtypeStruct((CHUNK, nh, BAND), F32), compiler_params=_params(("parallel",)),
    )(rel)


def bias_reduce(dbias):
    nh = dbias.shape[1]

    def body(d_ref, o_ref):
        @pl.when(pl.program_id(0) == 0)
        def _():
            o_ref[...] = jnp.zeros_like(o_ref)

        o_ref[...] += _mm_nt(d_ref[...], _rel_onehot_t(pl.program_id(0)))

    return pl.pallas_call(
        body, name="bias_reduce", grid=(CHUNK,), in_specs=[pl.BlockSpec((None, nh, BAND), lambda i: (i, 0, 0))],
        out_specs=pl.BlockSpec((nh, N_REL), lambda i: (0, 0)), out_shape=jax.ShapeDtypeStruct((nh, N_REL), F32),
        compiler_params=_params(("arbitrary",)),
    )(dbias)


def _tri(n, strict):
    r = lax.broadcasted_iota(jnp.int32, (n, n), 0)
    c = lax.broadcasted_iota(jnp.int32, (n, n), 1)
    return (c < r) if strict else (c <= r)


def _rwkv_chunk(s, r, k, v, w, a, b):
    n = r.shape[0]
    lw = jnp.log(w)
    cum = _mm(jnp.where(_tri(n, False), 1.0, 0.0).astype(F32), lw)
    p_incl = jnp.exp(cum)
    p_inv = jnp.exp(-cum)
    a_t = a * jnp.exp(cum - lw)
    r_t = r * p_incl
    b_t = b * p_inv
    k_t = k * p_inv
    strict, incl = _tri(n, True), _tri(n, False)
    a_ab = jnp.where(strict, _dot3(a_t, b_t, NT), 0.0)
    a_ak = jnp.where(strict, _dot3(a_t, k_t, NT), 0.0)
    eye = jnp.where(incl & ~strict, 1.0, 0.0).astype(F32)
    inv = eye + a_ab
    power = a_ab
    for _ in range(int(math.log2(n)) - 1):
        power = _dot3(power, power, NN)
        inv = inv + _dot3(inv, power, NN)
    sa = _dot3(inv, _dot3(a_t, s, NT) + _dot3(a_ak, v, NN), NN)
    y = (_dot3(r_t, s, NT) + _dot3(jnp.where(incl, _dot3(r_t, b_t, NT), 0.0), sa, NN)
         + _dot3(jnp.where(incl, _dot3(r_t, k_t, NT), 0.0), v, NN))
    s_new = (s + _dot3(sa, b_t, TN) + _dot3(v, k_t, TN)) * p_incl[n - 1:n, :]
    return s_new, y


RWKV_TILE = 512
RWKV_HEADS = 4


def _rwkv_specs(nh, t, reverse):
    tile = min(RWKV_TILE, t)
    hb = RWKV_HEADS if nh % RWKV_HEADS == 0 else 1
    nt = t // tile
    per = tile // RWKV_CHUNK
    pos = (lambda h, i: (h, nt - 1 - i, 0)) if reverse else (lambda h, i: (h, i, 0))
    pos4 = (lambda h, i: (h, nt - 1 - i, 0, 0)) if reverse else (lambda h, i: (h, i, 0, 0))
    return (hb, nt, per, pl.BlockSpec((hb, tile, HEAD_DIM), pos), pl.BlockSpec((hb, per, HEAD_DIM, HEAD_DIM), pos4))


def _hosted_steps(exchange, refs_in, refs_out, sems, n_steps, forward_at):
    if exchange is None:
        return
    start, forward, finish = exchange.plan(refs_in, refs_out, *sems)
    step = pl.program_id(0) * pl.num_programs(1) + pl.program_id(1)
    pl.when(step == 0)(start)
    pl.when(step == forward_at)(forward)
    pl.when(step == n_steps - 1)(finish)


def _hosted_call_args(exchange):
    if exchange is None:
        return [], [], [], []
    nb = len(exchange.arrays)
    return exchange.arrays, [HBM_SPEC] * nb, exchange.out_shape, exchange.scratch()


def rwkv_fwd(r, k, v, w, a, b, hosted=None):
    nh, t, _ = r.shape
    hb, nt, per, row_spec, s_spec = _rwkv_specs(nh, t, False)
    h_arrays, h_specs, h_shapes, h_scratch = _hosted_call_args(hosted)
    nhosted = len(h_arrays)
    n_steps = (nh // hb) * nt

    def body(r_ref, k_ref, v_ref, w_ref, a_ref, b_ref, *rest):
        h_in, (y_ref, s_ref), h_out = rest[:nhosted], rest[nhosted:nhosted + 2], rest[nhosted + 2:2 * nhosted + 2]
        state, sems = rest[2 * nhosted + 2], rest[2 * nhosted + 3:]
        _hosted_steps(hosted, h_in, h_out, sems, n_steps, (3 * n_steps) // 4)

        @pl.when(pl.program_id(1) == 0)
        def _():
            state[...] = jnp.zeros_like(state)

        def step(c, states):
            rows = pl.ds(pl.multiple_of(c * RWKV_CHUNK, RWKV_CHUNK), RWKV_CHUNK)
            out = []
            for h in range(hb):
                s_ref[h, c] = states[h]
                s_new, y = _rwkv_chunk(states[h], r_ref[h, rows, :], k_ref[h, rows, :], v_ref[h, rows, :],
                                       w_ref[h, rows, :], a_ref[h, rows, :], b_ref[h, rows, :])
                y_ref[h, rows, :] = y
                out.append(s_new)
            return tuple(out)

        final = lax.fori_loop(0, per, step, tuple(state[h] for h in range(hb)))
        for h in range(hb):
            state[h] = final[h]

    return pl.pallas_call(
        body, name="rwkv_fwd", grid=(nh // hb, nt), in_specs=[row_spec] * 6 + h_specs,
        out_specs=[row_spec, s_spec] + h_specs,
        out_shape=[jax.ShapeDtypeStruct((nh, t, HEAD_DIM), F32),
                   jax.ShapeDtypeStruct((nh, t // RWKV_CHUNK, HEAD_DIM, HEAD_DIM), F32)] + h_shapes,
        scratch_shapes=[pltpu.VMEM((hb, HEAD_DIM, HEAD_DIM), F32)] + h_scratch,
        compiler_params=_params(("arbitrary", "arbitrary")),
    )(r, k, v, w, a, b, *h_arrays)


def rwkv_bwd(r, k, v, w, a, b, states, dy, hosted=None):
    nh, t, _ = r.shape
    hb, nt, per, row_spec, s_spec = _rwkv_specs(nh, t, True)
    h_arrays, h_specs, h_shapes, h_scratch = _hosted_call_args(hosted)
    nhosted = len(h_arrays)
    n_steps = (nh // hb) * nt

    def body(r_ref, k_ref, v_ref, w_ref, a_ref, b_ref, s_ref, dy_ref, *rest):
        h_in, d_refs, h_out = rest[:nhosted], rest[nhosted:nhosted + 6], rest[nhosted + 6:2 * nhosted + 6]
        dstate, sems = rest[2 * nhosted + 6], rest[2 * nhosted + 7:]
        _hosted_steps(hosted, h_in, h_out, sems, n_steps, n_steps // 2)

        @pl.when(pl.program_id(1) == 0)
        def _():
            dstate[...] = jnp.zeros_like(dstate)

        def step(i, ds):
            c = per - 1 - i
            rows = pl.ds(pl.multiple_of(c * RWKV_CHUNK, RWKV_CHUNK), RWKV_CHUNK)
            out = []
            for h in range(hb):
                _, vjp = jax.vjp(_rwkv_chunk, s_ref[h, c], r_ref[h, rows, :], k_ref[h, rows, :], v_ref[h, rows, :],
                                 w_ref[h, rows, :], a_ref[h, rows, :], b_ref[h, rows, :])
                grads = vjp((ds[h], dy_ref[h, rows, :]))
                for d_ref, g in zip(d_refs, grads[1:]):
                    d_ref[h, rows, :] = g
                out.append(grads[0])
            return tuple(out)

        final = lax.fori_loop(0, per, step, tuple(dstate[h] for h in range(hb)))
        for h in range(hb):
            dstate[h] = final[h]

    hs = jax.ShapeDtypeStruct((nh, t, HEAD_DIM), F32)
    return pl.pallas_call(
        body, name="rwkv_bwd", grid=(nh // hb, nt), in_specs=[row_spec] * 6 + [s_spec, row_spec] + h_specs,
        out_specs=[row_spec] * 6 + h_specs, out_shape=[hs] * 6 + h_shapes,
        scratch_shapes=[pltpu.VMEM((hb, HEAD_DIM, HEAD_DIM), F32)] + h_scratch,
        compiler_params=_params(("arbitrary", "arbitrary")),
    )(r, k, v, w, a, b, states, dy, *h_arrays)


SUBLANES = 8


def s5_scan(bu_re, bu_im, a_re, a_im, *, reverse=False, h_prev=None):
    t, n = bu_re.shape
    tt = _pick(t, (512, 256, 128, 64, 32, 16, 8))
    tc = _pick(n, (1024, 512, 256, 128))
    nt = t // tt
    with_da = h_prev is not None
    sign = -1.0 if reverse else 1.0

    def body(*refs):
        if with_da:
            br, bi, ar_ref, ai_ref, pr, pi, hr_ref, hi_ref, dar_ref, dai_ref, sr, si = refs
        else:
            br, bi, ar_ref, ai_ref, hr_ref, hi_ref, sr, si = refs
        ti = pl.program_id(1)

        @pl.when(ti == 0)
        def _():
            sr[...] = jnp.zeros_like(sr)
            si[...] = jnp.zeros_like(si)
            if with_da:
                dar_ref[...] = jnp.zeros_like(dar_ref)
                dai_ref[...] = jnp.zeros_like(dai_ref)

        ar = ar_ref[...]
        ai = ai_ref[...] * sign

        def group(gi, carry):
            hr, hi, dar, dai = carry
            g0 = pl.multiple_of((tt // SUBLANES - 1 - gi if reverse else gi) * SUBLANES, SUBLANES)
            rows = pl.ds(g0, SUBLANES)
            xr, xi = br[rows, :], bi[rows, :]
            if with_da:
                qr, qi = pr[rows, :], pi[rows, :]
            out_r, out_i = [None] * SUBLANES, [None] * SUBLANES
            for s in (range(SUBLANES - 1, -1, -1) if reverse else range(SUBLANES)):
                hr, hi = ar * hr - ai * hi + xr[s:s + 1, :], ar * hi + ai * hr + xi[s:s + 1, :]
                out_r[s], out_i[s] = hr, hi
                if with_da:
                    dar = dar + hr * qr[s:s + 1, :] + hi * qi[s:s + 1, :]
                    dai = dai + hi * qr[s:s + 1, :] - hr * qi[s:s + 1, :]
            hr_ref[rows, :] = jnp.concatenate(out_r, axis=0)
            hi_ref[rows, :] = jnp.concatenate(out_i, axis=0)
            return hr, hi, dar, dai

        zero = jnp.zeros((1, tc), F32)
        hr, hi, dar, dai = lax.fori_loop(0, tt // SUBLANES, group, (sr[...], si[...], zero, zero))
        sr[...] = hr
        si[...] = hi
        if with_da:
            dar_ref[...] += dar
            dai_ref[...] += dai

    tile = pl.BlockSpec((tt, tc), (lambda ci, ti: (nt - 1 - ti, ci)) if reverse else (lambda ci, ti: (ti, ci)))
    col = pl.BlockSpec((1, tc), lambda ci, ti: (0, ci))
    hs = jax.ShapeDtypeStruct((t, n), F32)
    cs = jax.ShapeDtypeStruct((1, n), F32)
    ins = [bu_re, bu_im, a_re, a_im] + (list(h_prev) if with_da else [])
    return pl.pallas_call(
        body, name="s5_scan_bwd" if reverse else "s5_scan_fwd", grid=(n // tc, nt),
        in_specs=[tile, tile, col, col] + ([tile, tile] if with_da else []),
        out_specs=[tile, tile] + ([col, col] if with_da else []), out_shape=[hs, hs] + ([cs, cs] if with_da else []),
        scratch_shapes=[pltpu.VMEM((1, tc), F32)] * 2, compiler_params=_params(("parallel", "arbitrary")),
    )(*ins)


N_CHIPS = 4
HBM_SPEC = pl.BlockSpec(memory_space=pl.ANY)


def _remote(src, dst, send_sem, recv_sem, device):
    return pltpu.make_async_remote_copy(src_ref=src, dst_ref=dst, send_sem=send_sem, recv_sem=recv_sem,
                                        device_id=device, device_id_type=pl.DeviceIdType.MESH)


class Exchange:
    def __init__(self, arrays, out_shape, n_sems, plan):
        self.arrays, self.out_shape, self.n_sems, self.plan = list(arrays), out_shape, n_sems, plan

    def scratch(self):
        nb = len(self.arrays)
        return [pltpu.SemaphoreType.DMA((nb, self.n_sems)), pltpu.SemaphoreType.DMA((nb, self.n_sems)),
                pltpu.SemaphoreType.DMA((nb,))]

    def run(self, name):
        nb = len(self.arrays)

        def body(*refs):
            for step in self.plan(refs[:nb], refs[nb:2 * nb], *refs[2 * nb:]):
                step()

        return pl.pallas_call(
            body, name=name, in_specs=[HBM_SPEC] * nb, out_specs=[HBM_SPEC] * nb, out_shape=self.out_shape,
            scratch_shapes=self.scratch(), compiler_params=pltpu.CompilerParams(has_side_effects=True),
        )(*self.arrays)


def gather_all(arrays):
    nb = len(arrays)

    def plan(ins, outs, send_sems, recv_sems, local_sems):
        x, y, c = (lax.axis_index(n) for n in MESH_AXES)
        me, sibling = 4 * x + 2 * y + c, 4 * x + 2 * y + 1 - c
        chips = [(1 - x, y), (x, 1 - y), (1 - x, 1 - y)]

        def copy(b, k, slot, to, src=None):
            block = outs[b].at[slot]
            return _remote(block if src is None else src, block, send_sems.at[b, k], recv_sems.at[b, k], to)

        def local():
            return [pltpu.make_async_copy(ins[b], outs[b].at[me], local_sems.at[b]) for b in range(nb)]

        def first():
            return [cp for b in range(nb) for cp in
                    [copy(b, 0, me, (x, y, 1 - c), src=ins[b])]
                    + [copy(b, 1 + j, me, (px, py, c), src=ins[b]) for j, (px, py) in enumerate(chips)]]

        def passed():
            return [copy(b, 4 + j, 4 * px + 2 * py + c, (x, y, 1 - c)) for j, (px, py) in enumerate(chips) for b in range(nb)]

        def start():
            for cp in local() + first():
                cp.start()

        def forward():
            for j, (px, py) in enumerate(chips):
                for b in range(nb):
                    copy(b, 1 + j, 4 * px + 2 * py + c, (px, py, c)).wait_recv()
                    copy(b, 4 + j, 4 * px + 2 * py + c, (x, y, 1 - c)).start()

        def finish():
            for b in range(nb):
                copy(b, 0, sibling, (x, y, 1 - c)).wait_recv()
                for j, (px, py) in enumerate(chips):
                    copy(b, 4 + j, 4 * px + 2 * py + 1 - c, (x, y, 1 - c)).wait_recv()
            for cp in first() + passed():
                cp.wait_send()
            for cp in local():
                cp.wait()

        return start, forward, finish

    out_shape = [jax.ShapeDtypeStruct((N_DEV,) + tuple(a.shape), a.dtype) for a in arrays]
    return Exchange(arrays, out_shape, N_DEV - 1, plan)


def _nothing():
    pass


def pair_exchange(arrays):
    nb = len(arrays)

    def plan(ins, outs, send_sems, recv_sems, local_sems):
        x, y, c = (lax.axis_index(n) for n in MESH_AXES)

        def copies():
            return [_remote(ins[b].at[2 * z + 1 - c], outs[b].at[z], send_sems.at[b, z], recv_sems.at[b, z], (x, y, 1 - c))
                    for b in range(nb) for z in range(N_CHIPS)]

        def start():
            for cp in copies():
                cp.start()

        def finish():
            for cp in copies():
                cp.wait_send()
            for cp in copies():
                cp.wait_recv()

        return start, _nothing, finish

    out_shape = [jax.ShapeDtypeStruct((N_CHIPS,) + tuple(a.shape[1:]), a.dtype) for a in arrays]
    return Exchange(arrays, out_shape, N_CHIPS, plan)


def pair_sum(name, pieces, from_sibling):
    _, r, c = pieces.shape
    tr = _row_tile(r, c * pieces.dtype.itemsize, 1 << 20)

    def body(p_ref, s_ref, o_ref):
        mine = p_ref[lax.axis_index("c")]
        o_ref[...] = (mine.astype(F32) + s_ref[...].astype(F32)).astype(o_ref.dtype)

    return pl.pallas_call(
        body, name=name, grid=(N_CHIPS, r // tr),
        in_specs=[pl.BlockSpec((None, 2, tr, c), lambda z, i: (z, 0, i, 0)), pl.BlockSpec((None, tr, c), lambda z, i: (z, i, 0))],
        out_specs=pl.BlockSpec((None, tr, c), lambda z, i: (z, i, 0)),
        out_shape=jax.ShapeDtypeStruct((N_CHIPS, r, c), pieces.dtype), compiler_params=_params(("parallel", "parallel")),
    )(pieces.reshape(N_CHIPS, 2, r, c), from_sibling)


def chip_exchange(arrays):
    nb = len(arrays)

    def plan(ins, outs, send_sems, recv_sems, local_sems):
        x, y, c = (lax.axis_index(n) for n in MESH_AXES)
        my_chip = 2 * x + y
        chips = [(1 - x, y), (x, 1 - y), (1 - x, 1 - y)]

        def local():
            return [pltpu.make_async_copy(ins[b].at[my_chip], outs[b].at[my_chip], local_sems.at[b]) for b in range(nb)]

        def copies():
            return [_remote(ins[b].at[2 * px + py], outs[b].at[my_chip], send_sems.at[b, j], recv_sems.at[b, j], (px, py, c))
                    for b in range(nb) for j, (px, py) in enumerate(chips)]

        def start():
            for cp in local() + copies():
                cp.start()

        def finish():
            for cp in copies():
                cp.wait_send()
            for cp in copies():
                cp.wait_recv()
            for cp in local():
                cp.wait()

        return start, _nothing, finish

    out_shape = [jax.ShapeDtypeStruct(a.shape, a.dtype) for a in arrays]
    return Exchange(arrays, out_shape, N_CHIPS - 1, plan)


def adam_update(name, pieces, w, m, v):
    r, c = w.shape
    n_pieces = pieces.shape[0]
    tr = _row_tile(r, 4 * c, 1 << 19)

    def body(p_ref, w_ref, m_ref, v_ref, g_ref, d_ref, mo_ref, vo_ref):
        g = p_ref[0].astype(F32)
        for j in range(1, n_pieces):
            g = g + p_ref[j].astype(F32)
        m_new = ADAM_B1 * m_ref[...] + (1.0 - ADAM_B1) * g
        v_new = ADAM_B2 * v_ref[...] + (1.0 - ADAM_B2) * (g * g)
        m_hat = m_new / (1.0 - ADAM_B1 ** ADAM_STEP)
        v_hat = v_new / (1.0 - ADAM_B2 ** ADAM_STEP)
        g_ref[...] = g
        d_ref[...] = -ADAM_LR * (m_hat / (jnp.sqrt(v_hat) + ADAM_EPS) + ADAM_WD * w_ref[...])
        mo_ref[...] = m_new
        vo_ref[...] = v_new

    row = pl.BlockSpec((tr, c), lambda i: (i, 0))
    out = jax.ShapeDtypeStruct((r, c), F32)
    return pl.pallas_call(
        body, name=name, grid=(r // tr,), in_specs=[pl.BlockSpec((n_pieces, tr, c), lambda i: (0, i, 0)), row, row, row],
        out_specs=[row] * 4, out_shape=[out] * 4, compiler_params=_params(("parallel",)),
    )(pieces, w, m, v)


PACK_WIDTH = 1024
PACK_ROWS = 64


def _pack(arrays, dtype, lead=0):
    parts = []
    for a in arrays:
        head = a.shape[:lead]
        f = a.reshape(head + (-1,)).astype(dtype)
        pad = (-f.shape[-1]) % PACK_WIDTH
        if pad:
            f = jnp.pad(f, [(0, 0)] * lead + [(0, pad)])
        parts.append(f.reshape(head + (-1, PACK_WIDTH)))
    out = jnp.concatenate(parts, axis=lead)
    pad = (-out.shape[lead]) % PACK_ROWS
    if pad:
        out = jnp.pad(out, [(0, 0)] * lead + [(0, pad), (0, 0)])
    return out


def _unpack(packed, shapes, lead=0):
    head = packed.shape[:lead]
    out, row = [], 0
    for s in shapes:
        n = int(np.prod(s))
        rows = -(-n // PACK_WIDTH)
        chunk = lax.slice_in_dim(packed, row, row + rows, axis=lead).reshape(head + (-1,))
        out.append(lax.slice_in_dim(chunk, 0, n, axis=lead).reshape(head + tuple(s)))
        row += rows
    return out


def _to_natural(stacked, kind):
    if kind == "col":
        m = jnp.moveaxis(stacked, 0, -2)
        return m.reshape(m.shape[:-2] + (m.shape[-2] * m.shape[-1],))
    m = jnp.moveaxis(stacked, 0, -3)
    return m.reshape(m.shape[:-3] + (m.shape[-3] * m.shape[-2], m.shape[-1]))


def _to_stacked(natural, kind):
    if kind == "col":
        m = natural.reshape(natural.shape[:-1] + (N_DEV, natural.shape[-1] // N_DEV))
        return jnp.moveaxis(m, -2, 0)
    m = natural.reshape(natural.shape[:-2] + (N_DEV, natural.shape[-2] // N_DEV, natural.shape[-1]))
    return jnp.moveaxis(m, -3, 0)


def _f_rms(h, g):
    return (_rms(h, g).astype(BF16),)


def _f_swiglu(g, u):
    g, u = g.astype(F32), u.astype(F32)
    return ((g * _sigmoid(g) * u).astype(BF16),)


def _f_ple(h, pre, pp):
    return (h + _sigmoid(pre) * pp,)


def _head_rms(x, g):
    ms = _seg_sum(x * x, HEAD_DIM) * (1.0 / HEAD_DIM)
    return x * _seg_expand(lax.rsqrt(ms + RMS_EPS), HEAD_DIM) * _tile_lanes(g, x.shape[1] // HEAD_DIM)


def _f_attpre(q, k, q_gain, k_gain):
    return _head_rms(q, q_gain) * (HEAD_DIM ** -0.5), _head_rms(k, k_gain)


def _f_shift(z, z_prev, mu):
    return (z + (z_prev - z) * mu,)


def _f_rwkvpre(r, k, v, xw, xa, xg, w0, w_up, a0, a_up, g_up, k_k, k_a):
    del r, v
    w_log = -_softplus(-(w0 + _mm(jnp.tanh(xw), w_up))) - 0.5
    decay = jnp.exp(-jnp.exp(w_log))
    a = _sigmoid(a0 + _mm(xa, a_up))
    g = _mm(_sigmoid(xg), g_up)
    kk = k * k_k
    norm = jnp.sqrt(_seg_expand(_seg_sum(kk * kk, HEAD_DIM), HEAD_DIM))
    kk = kk / jnp.maximum(norm, 1e-12)
    return k * (1.0 + (a - 1.0) * k_a), decay, -kk, kk * a, g


def _f_rwkvpost(y, r, k, v, g, lnx_w, lnx_b, r_k):
    mean = _seg_expand(_seg_sum(y, HEAD_DIM) * (1.0 / HEAD_DIM), HEAD_DIM)
    yc = y - mean
    var = _seg_expand(_seg_sum(yc * yc, HEAD_DIM) * (1.0 / HEAD_DIM), HEAD_DIM)
    yn = yc * lax.rsqrt(var + GN_EPS) * lnx_w + lnx_b
    bonus = _seg_expand(_seg_sum(r * k * r_k, HEAD_DIM), HEAD_DIM) * v
    return ((yn + bonus) * g,)


def _f_s5disc(lam_re, lam_im, log_dt):
    dt = jnp.exp(log_dt)
    mag = jnp.exp(lam_re * dt)
    ab_re, ab_im = mag * jnp.cos(lam_im * dt), mag * jnp.sin(lam_im * dt)
    denom = lam_re * lam_re + lam_im * lam_im
    z_re = ((ab_re - 1.0) * lam_re + ab_im * lam_im) / denom
    z_im = (ab_im * lam_re - (ab_re - 1.0) * lam_im) / denom
    return ab_re, ab_im, z_re, z_im


def _f_s5b(z_re, z_im, b_re, b_im):
    return z_re * b_re - z_im * b_im, z_re * b_im + z_im * b_re


def _f_s5post(ypre, u, d_skip):
    return (_gelu_tanh(ypre + d_skip * u).astype(BF16),)


def _f_glu(h, z1, z2):
    return (h + z1 * _sigmoid(z2),)


def _heads(x):
    t = x.shape[0]
    return jnp.transpose(x.reshape(t, -1, HEAD_DIM), (1, 0, 2))


def _unheads(x):
    return jnp.transpose(x, (1, 0, 2)).reshape(x.shape[1], -1)


def _shift_down(x):
    return jnp.pad(x[:-1], ((1, 0), (0, 0)))


def _shift_up(x):
    return jnp.pad(x[1:], ((0, 1), (0, 0)))


def _block_diag(blocks):
    g, a, b = blocks.shape
    eye = jnp.eye(g, dtype=blocks.dtype)
    return (blocks[:, :, None, :] * eye[:, None, :, None]).reshape(g * a, g * b)


def _diag_blocks(dense, g):
    a, b = dense.shape[0] // g, dense.shape[1] // g
    return jnp.stack([dense[i * a:(i + 1) * a, i * b:(i + 1) * b] for i in range(g)], axis=0)


def _row_tile(t, width_bytes, budget=2 * 1024 * 1024):
    for tr in (512, 256, 128, 64, 32, 16, 8):
        if t % tr == 0 and tr * width_bytes <= budget:
            return tr
    return t


def _ffn_fwd(tag, h, norm_w, w_gate, w_up, w_down):
    t, d = h.shape
    f = w_gate.shape[1]
    n = rowwise(f"{tag}_norm", _f_rms, [h], [norm_w], [(d, BF16)], _row_tile(t, 4 * d))[0]
    g = matmul(f"{tag}_gate", n, w_gate, out_dtype=BF16)
    u = matmul(f"{tag}_up", n, w_up, out_dtype=BF16)
    a = rowwise(f"{tag}_act", _f_swiglu, [g, u], [], [(f, BF16)], _row_tile(t, 4 * f))[0]
    return matmul(f"{tag}_down", a, w_down, alpha=0.5, res=h), (h, n, g, u, a)


def _ffn_bwd(tag, dh2, saved, norm_w, w_gate, w_up, w_down):
    h, n, g, u, a = saved
    t, d = h.shape
    f = w_gate.shape[1]
    da = matmul(f"{tag}_da", dh2, w_down, tb=True, alpha=0.5, out_dtype=BF16)
    d_down = matmul(f"{tag}_dwdown", a, dh2, ta=True, alpha=0.5, out_dtype=BF16)
    dg, du = rowwise_vjp(f"{tag}_dact", _f_swiglu, [g, u], [], [da], _row_tile(t, 4 * f, 1 << 20), [True, True], [],
                         grad_dtypes={0: BF16, 1: BF16})
    dn = matmul(f"{tag}_dn_gate", dg, w_gate, tb=True)
    dn = matmul(f"{tag}_dn_up", du, w_up, tb=True, res=dn)
    d_gate = matmul(f"{tag}_dwgate", n, dg, ta=True, out_dtype=BF16)
    d_up = matmul(f"{tag}_dwup", n, du, ta=True, out_dtype=BF16)
    dh, d_norm = rowwise_vjp(f"{tag}_dnorm", _f_rms, [h], [norm_w], [dn], _row_tile(t, 4 * d), [True], [True],
                             add_to={0: dh2})
    return dh, d_norm, d_gate, d_up, d_down


def _ple_fwd(tag, h, norm_w, w_gate, w_proj, p_i):
    t, d = h.shape
    n = rowwise(f"{tag}_norm", _f_rms, [h], [norm_w], [(d, BF16)], _row_tile(t, 4 * d))[0]
    pre = matmul(f"{tag}_gate", n, w_gate)
    pp = matmul(f"{tag}_proj", p_i, w_proj)
    h2 = rowwise(f"{tag}_out", _f_ple, [h, pre, pp], [], [(d, F32)], _row_tile(t, 4 * d))[0]
    return h2, (h, n, pre, pp)


def _ple_bwd(tag, dh2, saved, norm_w, w_gate, w_proj, p_i):
    h, n, pre, pp = saved
    t, d = h.shape
    dpre, dpp = rowwise_vjp(f"{tag}_dout", _f_ple, [h, pre, pp], [], [dh2], _row_tile(t, 4 * d), [False, True, True], [],
                            grad_dtypes={1: BF16, 2: BF16})
    d_proj = matmul(f"{tag}_dwproj", p_i, dpp, ta=True, out_dtype=BF16)
    d_gate = matmul(f"{tag}_dwgate", n, dpre, ta=True, out_dtype=BF16)
    dn = matmul(f"{tag}_dn", dpre, w_gate, tb=True)
    dh, d_norm = rowwise_vjp(f"{tag}_dnorm", _f_rms, [h], [norm_w], [dn], _row_tile(t, 4 * d), [True], [True],
                             add_to={0: dh2})
    return dh, d_norm, d_gate, d_proj


def _ab_fwd(h, w, hosted=None, hosted_done=None):
    t, d = h.shape
    da = d // 2
    n = rowwise("ab_norm", _f_rms, [h], [w["mix_norm"]], [(d, BF16)], _row_tile(t, 4 * d))[0]
    proj = matmul("ab_in", n, w["ab_w_in"])
    q_raw, k_raw, v_att, z = proj[:, :da], proj[:, da:2 * da], proj[:, 2 * da:3 * da], proj[:, 3 * da:]
    tr = _row_tile(t, 4 * da, 1 << 19)
    qn, kn = rowwise("att_pre", _f_attpre, [q_raw, k_raw], [w["att_q_gain"], w["att_k_gain"]], [(da, F32)] * 2, tr)
    bias = jnp.transpose(bias_expand(w["att_rel_bias"]), (1, 0, 2))
    qh, kh, vh = _heads(qn), _heads(kn), _heads(v_att)
    att = _unheads(attention_fwd(qh, kh, vh, bias))
    z_prev = _shift_down(z)
    zz = rowwise("rwkv_shift", _f_shift, [z, z_prev], [w["rwkv_mu"]], [(z.shape[1], F32)], _row_tile(t, 4 * z.shape[1]))[0]
    o = [0, da, 2 * da, 3 * da, 3 * da + DECAY_LORA, 3 * da + DECAY_LORA + AAA_LORA, z.shape[1]]
    r, k, v, xw, xa, xg = (zz[:, o[i]:o[i + 1]] for i in range(6))
    pre_consts = [w[nm] for nm in ("rwkv_w0", "rwkv_w_up", "rwkv_a0", "rwkv_a_up", "rwkv_g_up", "rwkv_k_k", "rwkv_k_a")]
    k2, decay, ia, ib, g = rowwise("rwkv_pre", _f_rwkvpre, [r, k, v, xw, xa, xg], pre_consts, [(da, F32)] * 5, tr)
    heads = [_heads(a) for a in (r, k2, v, decay, ia, ib)]
    y_h, states, *travelled = rwkv_fwd(*heads, hosted=hosted)
    w_out = w["ab_w_out"] if hosted is None else hosted_done(travelled)
    y = _unheads(y_h)
    post_consts = [w["rwkv_lnx_w"], w["rwkv_lnx_b"], w["rwkv_r_k"]]
    rw = rowwise("rwkv_post", _f_rwkvpost, [y, r, k2, v, g], post_consts, [(da, F32)], tr)[0]
    cat = jnp.concatenate([att, rw], axis=1).astype(BF16)
    h2 = matmul("ab_out", cat, w_out, res=h)
    saved = dict(h=h, n=n, q_raw=q_raw, k_raw=k_raw, qh=qh, kh=kh, vh=vh, bias=bias, z=z, z_prev=z_prev,
                 rows=(r, k, v, xw, xa, xg), pre_consts=pre_consts, post=(y, r, k2, v, g), post_consts=post_consts,
                 heads=heads, states=states, cat=cat, tr=tr)
    return h2, saved


def _ab_bwd(dh2, s, w, make_hosted=None, hosted_done=None):
    h = s["h"]
    t, d = h.shape
    da = d // 2
    tr = s["tr"]
    grads = {}
    dcat = matmul("ab_dcat", dh2, w["ab_w_out"], tb=True)
    grads["ab_w_out"] = matmul("ab_dwout", s["cat"], dh2, ta=True, out_dtype=BF16)
    d_att, d_rw = dcat[:, :da], dcat[:, da:]
    dy, dr1, dk1, dv1, dg, grads["rwkv_lnx_w"], grads["rwkv_lnx_b"], grads["rwkv_r_k"] = rowwise_vjp(
        "rwkv_dpost", _f_rwkvpost, list(s["post"]), s["post_consts"], [d_rw], tr, [True] * 5, [True] * 3)
    hosted = None if make_hosted is None else make_hosted(grads["ab_w_out"])
    *d_heads, = rwkv_bwd(*s["heads"], s["states"], _heads(dy), hosted=hosted)
    if hosted is not None:
        hosted_done(d_heads[6:])
        d_heads = d_heads[:6]
    dr2, dk2, dv2, ddecay, dia, dib = (_unheads(a) for a in d_heads)
    pre = rowwise_vjp("rwkv_dpre", _f_rwkvpre, list(s["rows"]), s["pre_consts"], [(dk1, dk2), ddecay, dia, dib, dg], tr,
                      [True] * 6, [True] * 7, add_to={0: (dr1, dr2), 2: (dv1, dv2)})
    for nm, g in zip(("rwkv_w0", "rwkv_w_up", "rwkv_a0", "rwkv_a_up", "rwkv_g_up", "rwkv_k_k", "rwkv_k_a"), pre[6:]):
        grads[nm] = g
    dzz = jnp.concatenate(pre[:6], axis=1)
    trz = _row_tile(t, 4 * dzz.shape[1])
    grads["rwkv_mu"] = rowwise_vjp("rwkv_dmu", _f_shift, [s["z"], s["z_prev"]], [w["rwkv_mu"]], [dzz], trz,
                                   [False, False], [True])[0]
    dz = rowwise("rwkv_dshift", _f_shift, [dzz, _shift_up(dzz)], [w["rwkv_mu"]], [(dzz.shape[1], F32)], trz)[0]
    dqh, dkh, dvh, dbias = attention_bwd(s["qh"], s["kh"], s["vh"], s["bias"], _heads(d_att))
    grads["att_rel_bias"] = bias_reduce(jnp.transpose(dbias, (1, 0, 2)))
    dq_raw, dk_raw, grads["att_q_gain"], grads["att_k_gain"] = rowwise_vjp(
        "att_dpre", _f_attpre, [s["q_raw"], s["k_raw"]], [w["att_q_gain"], w["att_k_gain"]],
        [_unheads(dqh), _unheads(dkh)], tr, [True, True], [True, True])
    dproj = jnp.concatenate([dq_raw, dk_raw, _unheads(dvh), dz], axis=1).astype(BF16)
    dn = matmul("ab_dn", dproj, w["ab_w_in"], tb=True)
    grads["ab_w_in"] = matmul("ab_dwin", s["n"], dproj, ta=True, out_dtype=BF16)
    dh, grads["mix_norm"] = rowwise_vjp("ab_dnorm", _f_rms, [h], [w["mix_norm"]], [dn], _row_tile(t, 4 * d), [True], [True],
                                        add_to={0: dh2})
    return dh, grads


def _s5_fwd(h, w):
    t, d = h.shape
    n_groups, n_state = w["ssm_lambda_re"].shape
    gp = n_groups * n_state
    n = rowwise("s5_norm", _f_rms, [h], [w["mix_norm"]], [(d, BF16)], _row_tile(t, 4 * d))[0]
    u = matmul("s5_in", n, w["ssm_w_in"])
    disc_rows = [w["ssm_lambda_re"], w["ssm_lambda_im"], w["ssm_log_dt"]]
    ab_re, ab_im, z_re, z_im = rowwise("s5_disc", _f_s5disc, disc_rows, [], [(n_state, F32)] * 4, n_groups)
    b_rows = [z_re.reshape(gp, 1), z_im.reshape(gp, 1), w["ssm_b_re"], w["ssm_b_im"]]
    trb = _row_tile(gp, 512)
    bb_re, bb_im = rowwise("s5_bbar", _f_s5b, b_rows, [], [(SSM_GROUP, F32)] * 2, trb)
    to_dense = lambda bb: _block_diag(jnp.transpose(bb.reshape(n_groups, n_state, SSM_GROUP), (0, 2, 1)))
    bd_re, bd_im = to_dense(bb_re), to_dense(bb_im)
    cd_re = _block_diag(jnp.transpose(w["ssm_c_re"], (0, 2, 1)))
    cd_im = -_block_diag(jnp.transpose(w["ssm_c_im"], (0, 2, 1)))
    a_re, a_im = ab_re.reshape(1, gp), ab_im.reshape(1, gp)
    bu_re = matmul("s5_bu_re", u, bd_re)
    bu_im = matmul("s5_bu_im", u, bd_im)
    h_re, h_im = s5_scan(bu_re, bu_im, a_re, a_im)
    ypre = matmul("s5_y_re", h_re, cd_re)
    ypre = matmul("s5_y_im", h_im, cd_im, res=ypre)
    tru = _row_tile(t, 4 * u.shape[1])
    yg = rowwise("s5_post", _f_s5post, [ypre, u], [w["ssm_d"]], [(u.shape[1], BF16)], tru)[0]
    w_out1, w_out2 = w["ssm_w_out"][:, :d], w["ssm_w_out"][:, d:]
    z1 = matmul("s5_out1", yg, w_out1)
    z2 = matmul("s5_out2", yg, w_out2)
    h2 = rowwise("s5_glu", _f_glu, [h, z1, z2], [], [(d, F32)], _row_tile(t, 4 * d))[0]
    saved = dict(h=h, n=n, u=u, disc_rows=disc_rows, b_rows=b_rows, trb=trb, bd=(bd_re, bd_im), cd=(cd_re, cd_im),
                 a=(a_re, a_im), hs=(h_re, h_im), ypre=ypre, yg=yg, w_out=(w_out1, w_out2), z=(z1, z2), tru=tru)
    return h2, saved


def _s5_bwd(dh2, s, w):
    h, u = s["h"], s["u"]
    t, d = h.shape
    n_groups, n_state = w["ssm_lambda_re"].shape
    gp = n_groups * n_state
    grads = {}
    z1, z2 = s["z"]
    w_out1, w_out2 = s["w_out"]
    dz1, dz2 = rowwise_vjp("s5_dglu", _f_glu, [h, z1, z2], [], [dh2], _row_tile(t, 4 * d), [False, True, True], [],
                           grad_dtypes={1: BF16, 2: BF16})
    dyg = matmul("s5_dyg1", dz1, w_out1, tb=True)
    dyg = matmul("s5_dyg2", dz2, w_out2, tb=True, res=dyg)
    grads["ssm_w_out"] = jnp.concatenate([matmul("s5_dwout1", s["yg"], dz1, ta=True, out_dtype=BF16),
                                          matmul("s5_dwout2", s["yg"], dz2, ta=True, out_dtype=BF16)], axis=1)
    dypre, du1, grads["ssm_d"] = rowwise_vjp("s5_dpost", _f_s5post, [s["ypre"], u], [w["ssm_d"]], [dyg], s["tru"],
                                             [True, True], [True])
    cd_re, cd_im = s["cd"]
    h_re, h_im = s["hs"]
    dh_re = matmul("s5_dh_re", dypre, cd_re, tb=True)
    dh_im = matmul("s5_dh_im", dypre, cd_im, tb=True)
    dcd_re = matmul("s5_dc_re", h_re, dypre, ta=True)
    dcd_im = matmul("s5_dc_im", h_im, dypre, ta=True)
    a_re, a_im = s["a"]
    g_re, g_im, da_re, da_im = s5_scan(dh_re, dh_im, a_re, a_im, reverse=True,
                                       h_prev=(_shift_down(h_re), _shift_down(h_im)))
    bd_re, bd_im = s["bd"]
    du = matmul("s5_du_re", g_re, bd_re, tb=True, res=du1)
    du = matmul("s5_du_im", g_im, bd_im, tb=True, res=du)
    dbd_re = matmul("s5_db_re", u, g_re, ta=True)
    dbd_im = matmul("s5_db_im", u, g_im, ta=True)
    grads["ssm_w_in"] = matmul("s5_dwin", s["n"], du, ta=True, out_dtype=BF16)
    dn = matmul("s5_dn", du, w["ssm_w_in"], tb=True)
    dh, grads["mix_norm"] = rowwise_vjp("s5_dnorm", _f_rms, [h], [w["mix_norm"]], [dn], _row_tile(t, 4 * d), [True], [True],
                                        add_to={0: dh2})
    from_dense = lambda m: jnp.transpose(_diag_blocks(m, n_groups), (0, 2, 1)).reshape(gp, SSM_GROUP)
    dz_re, dz_im, grads["ssm_b_re"], grads["ssm_b_im"] = rowwise_vjp(
        "s5_dbbar", _f_s5b, s["b_rows"], [], [from_dense(dbd_re), from_dense(dbd_im)], s["trb"], [True] * 4, [])
    disc_cots = [da_re.reshape(n_groups, n_state), da_im.reshape(n_groups, n_state),
                 dz_re.reshape(n_groups, n_state), dz_im.reshape(n_groups, n_state)]
    grads["ssm_lambda_re"], grads["ssm_lambda_im"], grads["ssm_log_dt"] = rowwise_vjp(
        "s5_ddisc", _f_s5disc, s["disc_rows"], [], disc_cots, n_groups, [True] * 3, [])
    grads["ssm_c_re"] = jnp.transpose(_diag_blocks(dcd_re, n_groups), (0, 2, 1))
    grads["ssm_c_im"] = -jnp.transpose(_diag_blocks(dcd_im, n_groups), (0, 2, 1))
    return dh, grads


WEIGHTS = ["ffn1_norm", "ffn1_w_gate", "ffn1_w_up", "ffn1_w_down", "mix_norm", "ffn2_norm", "ffn2_w_gate", "ffn2_w_up",
           "ffn2_w_down", "ple_norm", "ple_w_gate", "ple_w_proj", "ab_w_in", "att_q_gain", "att_k_gain", "att_rel_bias",
           "rwkv_mu", "rwkv_w0", "rwkv_w_up", "rwkv_a0", "rwkv_a_up", "rwkv_g_up", "rwkv_k_k", "rwkv_k_a", "rwkv_r_k",
           "rwkv_lnx_w", "rwkv_lnx_b", "ab_w_out", "ssm_w_in", "ssm_lambda_re", "ssm_lambda_im", "ssm_log_dt", "ssm_b_re",
           "ssm_b_im", "ssm_c_re", "ssm_c_im", "ssm_d", "ssm_w_out"]
BIG = {"ffn1_w_gate": "col", "ffn1_w_up": "col", "ffn1_w_down": "row", "ffn2_w_gate": "col", "ffn2_w_up": "col",
       "ffn2_w_down": "row", "ple_w_gate": "row", "ple_w_proj": "col", "ab_w_in": "col", "ab_w_out": "row",
       "ssm_w_in": "row", "ssm_w_out": "col"}
SMALL_CUT = {"rwkv_w_up": "col", "rwkv_a_up": "col", "rwkv_g_up": "col", "ssm_d": "col"}
REPLICATED = [n for n in WEIGHTS if n not in BIG and n not in SMALL_CUT]


def kernel(x, p, ffn1_norm, ffn1_w_gate, ffn1_w_up, ffn1_w_down, mix_norm, ffn2_norm, ffn2_w_gate, ffn2_w_up, ffn2_w_down, ple_norm, ple_w_gate, ple_w_proj, ab_w_in, att_q_gain, att_k_gain, att_rel_bias, rwkv_mu, rwkv_w0, rwkv_w_up, rwkv_a0, rwkv_a_up, rwkv_g_up, rwkv_k_k, rwkv_k_a, rwkv_r_k, rwkv_lnx_w, rwkv_lnx_b, ab_w_out, ssm_w_in, ssm_lambda_re, ssm_lambda_im, ssm_log_dt, ssm_b_re, ssm_b_im, ssm_c_re, ssm_c_im, ssm_d, ssm_w_out, loss_target, m_ffn1_norm, m_ffn1_w_gate, m_ffn1_w_up, m_ffn1_w_down, m_mix_norm, m_ffn2_norm, m_ffn2_w_gate, m_ffn2_w_up, m_ffn2_w_down, m_ple_norm, m_ple_w_gate, m_ple_w_proj, m_ab_w_in, m_att_q_gain, m_att_k_gain, m_att_rel_bias, m_rwkv_mu, m_rwkv_w0, m_rwkv_w_up, m_rwkv_a0, m_rwkv_a_up, m_rwkv_g_up, m_rwkv_k_k, m_rwkv_k_a, m_rwkv_r_k, m_rwkv_lnx_w, m_rwkv_lnx_b, m_ab_w_out, m_ssm_w_in, m_ssm_lambda_re, m_ssm_lambda_im, m_ssm_log_dt, m_ssm_b_re, m_ssm_b_im, m_ssm_c_re, m_ssm_c_im, m_ssm_d, m_ssm_w_out, v_ffn1_norm, v_ffn1_w_gate, v_ffn1_w_up, v_ffn1_w_down, v_mix_norm, v_ffn2_norm, v_ffn2_w_gate, v_ffn2_w_up, v_ffn2_w_down, v_ple_norm, v_ple_w_gate, v_ple_w_proj, v_ab_w_in, v_att_q_gain, v_att_k_gain, v_att_rel_bias, v_rwkv_mu, v_rwkv_w0, v_rwkv_w_up, v_rwkv_a0, v_rwkv_a_up, v_rwkv_g_up, v_rwkv_k_k, v_rwkv_k_a, v_rwkv_r_k, v_rwkv_lnx_w, v_rwkv_lnx_b, v_ab_w_out, v_ssm_w_in, v_ssm_lambda_re, v_ssm_lambda_im, v_ssm_log_dt, v_ssm_b_re, v_ssm_b_im, v_ssm_c_re, v_ssm_c_im, v_ssm_d, v_ssm_w_out):
    vals = dict(locals())
    depth = ffn1_norm.shape[0]
    n_groups, n_state = ssm_lambda_re.shape[1:]

    kinds = {**BIG, **SMALL_CUT}
    travel = {n: (BF16 if n in BIG else F32) for n in kinds}

    def rows(a, lead=0):
        return a.reshape(a.shape[:lead] + (-1, a.shape[-1]))

    def model_layer(n, j):
        if vals[n].shape[0] == depth:
            return j
        return 2 * j if n.startswith(("ab_", "rwkv_")) else 2 * j + 1

    units = [(n, j) for n in kinds for j in range(vals[n].shape[0])]
    early_names = ("ffn1_w_gate", "ffn1_w_up", "ffn1_w_down", "ab_w_in", "rwkv_w_up", "rwkv_a_up", "rwkv_g_up")
    early = [u for u in units if model_layer(*u) == 0 and u[0] in early_names]
    late = [u for u in units if u not in early]
    full = {}

    def shard(unit):
        n, j = unit
        return rows(vals[n][j]).astype(travel[n])

    def take_gathered(which, got):
        for (n, j), g in zip(which, got):
            full[(n, j)] = _to_natural(g.reshape((N_DEV,) + vals[n].shape[1:]), kinds[n])

    take_gathered(early, gather_all([shard(u) for u in early]).run("gather_early"))

    def late_gather_done(got):
        take_gathered(late, got)
        return full[("ab_w_out", 0)]

    def row(name, j):
        return vals[name][j].reshape(1, -1)

    def ffn_weights(which, i):
        return (row(f"{which}_norm", i), full[(f"{which}_w_gate", i)], full[(f"{which}_w_up", i)],
                full[(f"{which}_w_down", i)])

    def mixer_weights(i):
        j = i // 2
        if i % 2 == 0:
            w = {n: row(n, j) for n in ("att_q_gain", "att_k_gain", "rwkv_mu", "rwkv_w0", "rwkv_a0", "rwkv_k_k", "rwkv_k_a",
                                        "rwkv_r_k", "rwkv_lnx_w", "rwkv_lnx_b")}
            w.update({n: full.get((n, j)) for n in ("ab_w_in", "ab_w_out", "rwkv_w_up", "rwkv_a_up", "rwkv_g_up")})
            w["att_rel_bias"] = att_rel_bias[j]
        else:
            w = {"ssm_lambda_re": ssm_lambda_re[j], "ssm_lambda_im": ssm_lambda_im[j],
                 "ssm_log_dt": ssm_log_dt[j].reshape(n_groups, 1),
                 "ssm_b_re": ssm_b_re[j].reshape(n_groups * n_state, -1),
                 "ssm_b_im": ssm_b_im[j].reshape(n_groups * n_state, -1),
                 "ssm_c_re": ssm_c_re[j], "ssm_c_im": ssm_c_im[j], "ssm_d": full[("ssm_d", j)].reshape(1, -1),
                 "ssm_w_in": full[("ssm_w_in", j)], "ssm_w_out": full[("ssm_w_out", j)]}
        w["mix_norm"] = row("mix_norm", i)
        return w

    def ple_weights(i):
        return (row("ple_norm", i), full[("ple_w_gate", i)], full[("ple_w_proj", i)], p[i, 0])

    h = x[0]
    saved = []
    for i in range(depth):
        h, s1 = _ffn_fwd(f"l{i}_ffn1", h, *ffn_weights("ffn1", i))
        if i == 0:
            h, sm = _ab_fwd(h, mixer_weights(i), gather_all([shard(u) for u in late]), late_gather_done)
        else:
            h, sm = (_ab_fwd if i % 2 == 0 else _s5_fwd)(h, mixer_weights(i))
        h, s2 = _ffn_fwd(f"l{i}_ffn2", h, *ffn_weights("ffn2", i))
        h, sp = _ple_fwd(f"l{i}_ple", h, *ple_weights(i))
        saved.append((s1, sm, s2, sp))
    dh, loss_part = loss_head(h, loss_target[0])

    per_layer = {n: [] for n in WEIGHTS}
    received = {}

    def chip_sums(which, tag):
        pieces = [rows(_to_stacked(per_layer[n][j - vals[n].shape[0]], kinds[n]).astype(travel[n]), lead=1)
                  for n, j in which]
        from_sibling = pair_exchange(pieces).run(f"reduce_pair_{tag}")
        return [pair_sum(f"pair_sum_{n}{j}", a, b) for (n, j), a, b in zip(which, pieces, from_sibling)]

    def late_reduce(d_ab_w_out):
        per_layer["ab_w_out"].insert(0, d_ab_w_out)
        return chip_exchange(chip_sums(late, "late"))

    def late_reduce_done(got):
        received.update(zip(late, got))

    for i in reversed(range(depth)):
        s1, sm, s2, sp = saved[i]
        dh, d_norm, d_gate, d_proj = _ple_bwd(f"l{i}_ple", dh, sp, *ple_weights(i))
        for n, g in (("ple_norm", d_norm), ("ple_w_gate", d_gate), ("ple_w_proj", d_proj)):
            per_layer[n].insert(0, g)
        dh, d_norm, d_gate, d_up, d_down = _ffn_bwd(f"l{i}_ffn2", dh, s2, *ffn_weights("ffn2", i))
        for n, g in (("ffn2_norm", d_norm), ("ffn2_w_gate", d_gate), ("ffn2_w_up", d_up), ("ffn2_w_down", d_down)):
            per_layer[n].insert(0, g)
        if i == 0:
            dh, mixer_grads = _ab_bwd(dh, sm, mixer_weights(i), late_reduce, late_reduce_done)
        else:
            dh, mixer_grads = (_ab_bwd if i % 2 == 0 else _s5_bwd)(dh, sm, mixer_weights(i))
        for n, g in mixer_grads.items():
            if not (i == 0 and n == "ab_w_out"):
                per_layer[n].insert(0, g)
        dh, d_norm, d_gate, d_up, d_down = _ffn_bwd(f"l{i}_ffn1", dh, s1, *ffn_weights("ffn1", i))
        for n, g in (("ffn1_norm", d_norm), ("ffn1_w_gate", d_gate), ("ffn1_w_up", d_up), ("ffn1_w_down", d_down)):
            per_layer[n].insert(0, g)
    grad_x = dh[None]

    received.update(zip(early, chip_exchange(chip_sums(early, "early")).run("reduce_chips_early")))

    rep_mine = _pack([jnp.stack(per_layer[n], axis=0).reshape(vals[n].shape) for n in REPLICATED] + [loss_part], F32)
    rep_all = gather_all([rep_mine]).run("gather_replicated")[0]

    out = {}
    for n in kinds:
        got = [received[(n, j)] for j in range(vals[n].shape[0])]
        got = got[0] if len(got) == 1 else jnp.concatenate(got, axis=1)
        state = [rows(vals[pre + n]) for pre in ("", "m_", "v_")]
        out[n] = tuple(r.reshape(vals[n].shape) for r in adam_update(f"adam_{n}", got, *state))
    zero = jnp.zeros((1, 1), F32)
    state = [_pack([vals[pre + n] for n in REPLICATED] + [zero], F32) for pre in ("", "m_", "v_")]
    shapes = [vals[n].shape for n in REPLICATED] + [zero.shape]
    results = [_unpack(r, shapes) for r in adam_update("adam_replicated", rep_all, *state)]
    for q, n in enumerate(REPLICATED):
        out[n] = tuple(r[q] for r in results)
    loss = results[0][-1].reshape(())
    return (loss, grad_x, *[out[n][0] for n in WEIGHTS], *[out[n][1] for n in WEIGHTS], *[out[n][2] for n in WEIGHTS],
            *[out[n][3] for n in WEIGHTS])
```

```python
import functools
import math

import jax
import jax.numpy as jnp
import numpy as np
from jax import lax
from jax.experimental import pallas as pl
from jax.experimental.pallas import tpu as pltpu

F32 = jnp.float32
BF16 = jnp.bfloat16
HIGHEST = lax.Precision.HIGHEST
MESH_AXES = ("x", "y", "c")
N_DEV = 8

CHUNK = 64
N_LEFT_CHUNKS = 8
BAND = (N_LEFT_CHUNKS + 1) * CHUNK
HEAD_DIM = 64
REL_CLIP = 128
N_REL = (CHUNK - 1) + REL_CLIP + 1
DECAY_LORA = 64
AAA_LORA = 64
GATE_LORA = 128
SSM_GROUP = 16
SSM_STATE = 64
RMS_EPS = 1e-6
GN_EPS = 64e-5
ADAM_LR = 0.001
ADAM_B1 = 0.9
ADAM_B2 = 0.999
ADAM_EPS = 1e-08
ADAM_WD = 0.01
ADAM_STEP = 10

RWKV_CHUNK = 64
VMEM_LIMIT = 56 * 1024 * 1024
LANES = 128


def _params(semantics):
    return pltpu.CompilerParams(dimension_semantics=semantics, vmem_limit_bytes=VMEM_LIMIT)


def _pick(n, prefs):
    for t in prefs:
        if n % t == 0:
            return t
    return n


def _dot(a, b, dims):
    return lax.dot_general(a, b, (dims, ((), ())), precision=HIGHEST, preferred_element_type=F32)


def _mm(a, b):
    return _dot(a, b, ((1,), (0,)))


def _mm_nt(a, b):
    return _dot(a, b, ((1,), (1,)))


def _mm_tn(a, b):
    return _dot(a, b, ((0,), (0,)))


def _split2(x):
    hi = x.astype(BF16)
    return hi, (x - hi.astype(F32)).astype(BF16)


def _dot3_raw(a, b, dims):
    a_hi, a_lo = _split2(a)
    b_hi, b_lo = _split2(b)
    dot = lambda p, q: lax.dot_general(p, q, (dims, ((), ())), preferred_element_type=F32)
    return dot(a_hi, b_hi) + (dot(a_hi, b_lo) + dot(a_lo, b_hi))


def _dot1_raw(a, b, dims):
    return lax.dot_general(a.astype(BF16), b.astype(BF16), (dims, ((), ())), preferred_element_type=F32)


NN, NT, TN = ((1,), (0,)), ((1,), (1,)), ((0,), (0,))


def _make_dot(raw):
    @functools.partial(jax.custom_vjp, nondiff_argnums=(2,))
    def dot(a, b, dims):
        return raw(a, b, dims)

    def fwd(a, b, dims):
        return raw(a, b, dims), (a, b)

    def bwd(dims, saved, g):
        a, b = saved
        if dims == NN:
            return raw(g, b, NT), raw(a, g, TN)
        if dims == NT:
            return raw(g, b, NN), raw(g, a, TN)
        return raw(b, g, NT), raw(a, g, NN)

    dot.defvjp(fwd, bwd)
    return dot


_dot3 = _make_dot(_dot3_raw)
_dot1 = _make_dot(_dot1_raw)


def _dot_ind_raw(x, ind, dims):
    hi = x.astype(BF16)
    rest = x - hi.astype(F32)
    mid = rest.astype(BF16)
    lo = (rest - mid.astype(F32)).astype(BF16)
    ind = ind.astype(BF16)
    dot = lambda p: lax.dot_general(p, ind, (dims, ((), ())), preferred_element_type=F32)
    return dot(hi) + (dot(mid) + dot(lo))


@jax.custom_vjp
def _mm_ind(x, ind):
    return _dot_ind_raw(x, ind, NN)


def _mm_ind_fwd(x, ind):
    return _dot_ind_raw(x, ind, NN), ind


def _mm_ind_bwd(ind, g):
    return _dot_ind_raw(g, ind, NT), jnp.zeros_like(ind)


_mm_ind.defvjp(_mm_ind_fwd, _mm_ind_bwd)


def _sigmoid(x):
    return 1.0 / (1.0 + jnp.exp(-x))


def _softplus(x):
    return jnp.maximum(x, 0.0) + jnp.log(1.0 + jnp.exp(-jnp.abs(x)))


def _gelu_tanh(x):
    return 0.5 * x * (1.0 + jnp.tanh(math.sqrt(2.0 / math.pi) * (x + 0.044715 * (x * x * x))))


def _seg_indicator(n, seg):
    r = lax.broadcasted_iota(jnp.int32, (n, n // seg), 0)
    c = lax.broadcasted_iota(jnp.int32, (n, n // seg), 1)
    return jnp.where((r >= c * seg) & (r < (c + 1) * seg), 1.0, 0.0).astype(F32)


def _seg_indicator_t(n, seg):
    c = lax.broadcasted_iota(jnp.int32, (n // seg, n), 0)
    r = lax.broadcasted_iota(jnp.int32, (n // seg, n), 1)
    return jnp.where((r >= c * seg) & (r < (c + 1) * seg), 1.0, 0.0).astype(F32)


def _seg_sum(x, seg):
    return _mm_ind(x, _seg_indicator(x.shape[1], seg))


def _seg_expand(s, seg):
    return _mm_ind(s, _seg_indicator_t(s.shape[1] * seg, seg))


def _tile_lanes(g, reps):
    n = g.shape[1]
    r = lax.broadcasted_iota(jnp.int32, (n, n * reps), 0)
    c = lax.broadcasted_iota(jnp.int32, (n, n * reps), 1)
    return _mm_ind(g, jnp.where((c & (n - 1)) == r, 1.0, 0.0).astype(F32))


def _rms(x, g):
    return x * lax.rsqrt(jnp.mean(x * x, axis=-1, keepdims=True) + RMS_EPS) * g


MATMUL_VMEM_BUDGET = 40 * 1024 * 1024
MATMUL_MAX_TILE = 2048
HBM_BYTES_PER_S = 1.2e12
MXU_FLOPS_PER_S = 8e14
GRID_STEP_S = 0.35e-6


def _tile_candidates(dim):
    c = [d for d in range(LANES, min(dim, MATMUL_MAX_TILE) + 1, LANES) if dim % d == 0]
    return c or [dim]


def _matmul_tiles(m, n, k, a_bytes, b_bytes, out_bytes, res_bytes):
    best = None
    for tm in _tile_candidates(m):
        for tn in _tile_candidates(n):
            for tk in _tile_candidates(k):
                casts = (tm * tk * 2 if a_bytes > 2 else 0) + (tk * tn * 2 if b_bytes > 2 else 0)
                vmem = (2 * (tm * tk * a_bytes + tk * tn * b_bytes + tm * tn * (out_bytes + res_bytes))
                        + 2 * tm * tn * 4 + casts)
                if vmem > MATMUL_VMEM_BUDGET:
                    continue
                steps = (m // tm) * (n // tn) * (k // tk)
                a_reads = 1 if k == tk else n // tn
                traffic = (m * k * a_bytes * a_reads + k * n * b_bytes * (m // tm) + m * n * (out_bytes + res_bytes))
                dma_s = traffic / HBM_BYTES_PER_S
                cost = max(dma_s, 2.0 * m * n * k / MXU_FLOPS_PER_S) + 0.2 * dma_s + steps * GRID_STEP_S
                if best is None or cost < best[0]:
                    best = (cost, tm, tn, tk)
    return best[1:]


def matmul(name, a, b, *, ta=False, tb=False, alpha=1.0, res=None, out_dtype=F32):
    m, k = (a.shape[1], a.shape[0]) if ta else a.shape
    n = b.shape[0] if tb else b.shape[1]
    assert k == (b.shape[1] if tb else b.shape[0]), (name, a.shape, b.shape)
    tm, tn, tk = _matmul_tiles(m, n, k, a.dtype.itemsize, b.dtype.itemsize, jnp.dtype(out_dtype).itemsize,
                               0 if res is None else res.dtype.itemsize)
    nk = k // tk
    dims = ((0 if ta else 1,), (1 if tb else 0,))

    def body(*refs):
        if res is None:
            a_ref, b_ref, o_ref, acc_ref = refs
            res_ref = None
        else:
            a_ref, b_ref, res_ref, o_ref, acc_ref = refs
        kk = pl.program_id(2)

        @pl.when(kk == 0)
        def _():
            acc_ref[...] = jnp.zeros_like(acc_ref)

        acc_ref[...] += lax.dot_general(a_ref[...].astype(BF16), b_ref[...].astype(BF16), (dims, ((), ())),
                                        preferred_element_type=F32)

        @pl.when(kk == nk - 1)
        def _():
            out = acc_ref[...] * alpha
            if res_ref is not None:
                out = out + res_ref[...].astype(F32)
            o_ref[...] = out.astype(o_ref.dtype)

    a_spec = pl.BlockSpec((tk, tm), lambda i, j, kk: (kk, i)) if ta else pl.BlockSpec((tm, tk), lambda i, j, kk: (i, kk))
    b_spec = pl.BlockSpec((tn, tk), lambda i, j, kk: (j, kk)) if tb else pl.BlockSpec((tk, tn), lambda i, j, kk: (kk, j))
    o_spec = pl.BlockSpec((tm, tn), lambda i, j, kk: (i, j))
    in_specs = [a_spec, b_spec] + ([o_spec] if res is not None else [])
    args = (a, b) + ((res,) if res is not None else ())
    return pl.pallas_call(
        body, name=name, grid=(m // tm, n // tn, nk), in_specs=in_specs, out_specs=o_spec,
        out_shape=jax.ShapeDtypeStruct((m, n), out_dtype), scratch_shapes=[pltpu.VMEM((tm, tn), F32)],
        compiler_params=_params(("parallel", "parallel", "arbitrary")),
    )(*args)


def _row_specs(arrays, tr):
    return [pl.BlockSpec((tr, a.shape[1]), lambda i: (i, 0)) for a in arrays]


def _whole_specs(arrays):
    return [pl.BlockSpec(a.shape, lambda i: (0, 0)) for a in arrays]


def rowwise(name, fn, rows, consts, outs, tr):
    t = rows[0].shape[0]
    nr, nc = len(rows), len(consts)

    def body(*refs):
        vals = [r[...] for r in refs[:nr + nc]]
        res = fn(*vals)
        for o_ref, o in zip(refs[nr + nc:], res):
            o_ref[...] = o.astype(o_ref.dtype)

    out_shape = [jax.ShapeDtypeStruct((t, w), d) for w, d in outs]
    return pl.pallas_call(
        body, name=name, grid=(t // tr,), in_specs=_row_specs(rows, tr) + _whole_specs(consts),
        out_specs=_row_specs(out_shape, tr), out_shape=out_shape, compiler_params=_params(("parallel",)),
    )(*rows, *consts)


def rowwise_vjp(name, fn, rows, consts, cots, tr, row_grad, const_grad, add_to=None, grad_dtypes=None):
    t = rows[0].shape[0]
    nr, nc = len(rows), len(consts)
    cot_groups = [c if isinstance(c, (tuple, list)) else (c,) for c in cots]
    flat_cots = [a for g in cot_groups for a in g]
    add_to = {i: (a if isinstance(a, (tuple, list)) else (a,)) for i, a in (add_to or {}).items()}
    add_idx = [(i, q) for i in sorted(add_to) for q in range(len(add_to[i]))]
    add_arrays = [add_to[i][q] for i, q in add_idx]
    r_idx = [i for i in range(nr) if row_grad[i]]
    c_idx = [i for i in range(nc) if const_grad[i]]
    grad_dtypes = grad_dtypes or {}
    n_in = nr + nc + len(flat_cots) + len(add_arrays)

    def body(*refs):
        vals = [r[...] for r in refs[:nr + nc]]
        pos = nr + nc
        cts = []
        for g in cot_groups:
            s = refs[pos][...].astype(F32)
            for q in range(1, len(g)):
                s = s + refs[pos + q][...].astype(F32)
            cts.append(s)
            pos += len(g)
        adds = {}
        for n_add, (i, _) in enumerate(add_idx):
            term = refs[pos + n_add][...].astype(F32)
            adds[i] = adds[i] + term if i in adds else term
        diff =[vals[i] for i in r_idx] + [vals[nr + i] for i in c_idx]

        def f(*d):
            full = list(vals)
            for q, i in enumerate(r_idx):
                full[i] = d[q]
            for q, i in enumerate(c_idx):
                full[nr + i] = d[len(r_idx) + q]
            return tuple(fn(*full))

        prim, vjp = jax.vjp(f, *diff)
        grads = vjp(tuple(c.astype(p.dtype) for c, p in zip(cts, prim)))
        o_refs = refs[n_in:]
        for q, i in enumerate(r_idx):
            g = grads[q].astype(F32)
            if i in adds:
                g = g + adds[i].astype(F32)
            o_refs[q][...] = g.astype(o_refs[q].dtype)
        step = pl.program_id(0)
        for q, i in enumerate(c_idx):
            o_ref = o_refs[len(r_idx) + q]

            @pl.when(step == 0)
            def _(o_ref=o_ref):
                o_ref[...] = jnp.zeros_like(o_ref)

            o_ref[...] += grads[len(r_idx) + q].astype(F32)

    row_out = [jax.ShapeDtypeStruct(rows[i].shape, grad_dtypes.get(i, F32)) for i in r_idx]
    const_out = [jax.ShapeDtypeStruct(consts[i].shape, F32) for i in c_idx]
    outs = pl.pallas_call(
        body, name=name, grid=(t // tr,),
        in_specs=_row_specs(rows, tr) + _whole_specs(consts) + _row_specs(flat_cots, tr) + _row_specs(add_arrays, tr),
        out_specs=_row_specs(row_out, tr) + _whole_specs(const_out), out_shape=row_out + const_out,
        compiler_params=_params(("arbitrary",)),
    )(*rows, *consts, *flat_cots, *add_arrays)
    return list(outs)


def loss_head(y, target):
    t, d = y.shape
    tr = _pick(t, (256, 128, 64, 32, 16, 8))

    def body(y_ref, t_ref, dy_ref, l_ref):
        diff = y_ref[...] - t_ref[...]
        dy_ref[...] = diff * (1.0 / d)

        @pl.when(pl.program_id(0) == 0)
        def _():
            l_ref[...] = jnp.zeros_like(l_ref)

        l_ref[...] += 0.5 * jnp.sum(jnp.mean(diff * diff, axis=-1, keepdims=True), axis=0, keepdims=True)

    dy, l = pl.pallas_call(
        body, name="loss_head", grid=(t // tr,), in_specs=_row_specs([y, target], tr),
        out_specs=[pl.BlockSpec((tr, d), lambda i: (i, 0)), pl.BlockSpec((1, 1), lambda i: (0, 0))],
        out_shape=[jax.ShapeDtypeStruct((t, d), F32), jax.ShapeDtypeStruct((1, 1), F32)],
        compiler_params=_params(("arbitrary",)),
    )(y, target)
    return dy, l


ATT_PAD = N_LEFT_CHUNKS * CHUNK
MASKED = -1e30
ATT_GROUP = 4


def _attn_chunks(q_c, k_b, v_b, bias, valid):
    s = [jnp.where(ok, _dot1(q, k, NT) + bias, MASKED) for q, k, ok in zip(q_c, k_b, valid)]
    e = [jnp.exp(x - lax.stop_gradient(jnp.max(x, axis=-1, keepdims=True))) for x in s]
    p = [x / jnp.sum(x, axis=-1, keepdims=True) for x in e]
    return tuple(_dot1(x, v, NN) for x, v in zip(p, v_b))


def _band_valid(c):
    return (c * CHUNK + lax.broadcasted_iota(jnp.int32, (1, BAND), 1)) >= ATT_PAD


def _head_spec(t):
    return pl.BlockSpec((None, t, HEAD_DIM), lambda h: (h, 0, 0))


def attention_fwd(q, k, v, bias):
    nh, t, _ = q.shape
    nchunks = t // CHUNK
    group = _pick(nchunks, (ATT_GROUP, 2, 1))

    def body(q_ref, k_ref, v_ref, b_ref, o_ref, kp, vp):
        zeros = jnp.zeros((ATT_PAD, HEAD_DIM), F32)
        kp[pl.ds(0, ATT_PAD), :] = zeros
        vp[pl.ds(0, ATT_PAD), :] = zeros
        kp[pl.ds(ATT_PAD, t), :] = k_ref[...]
        vp[pl.ds(ATT_PAD, t), :] = v_ref[...]
        bias_h = b_ref[...]

        def step(g, carry):
            cs = [g * group + u for u in range(group)]
            q0 = [pl.multiple_of(c * CHUNK, CHUNK) for c in cs]
            outs = _attn_chunks(tuple(q_ref[pl.ds(s, CHUNK), :] for s in q0), tuple(kp[pl.ds(s, BAND), :] for s in q0),
                                tuple(vp[pl.ds(s, BAND), :] for s in q0), bias_h, tuple(_band_valid(c) for c in cs))
            for s, o in zip(q0, outs):
                o_ref[pl.ds(s, CHUNK), :] = o
            return carry

        lax.fori_loop(0, nchunks // group, step, 0)

    return pl.pallas_call(
        body, name="attention_fwd", grid=(nh,),
        in_specs=[_head_spec(t)] * 3 + [pl.BlockSpec((None, CHUNK, BAND), lambda h: (h, 0, 0))],
        out_specs=_head_spec(t), out_shape=jax.ShapeDtypeStruct((nh, t, HEAD_DIM), F32),
        scratch_shapes=[pltpu.VMEM((t + ATT_PAD, HEAD_DIM), F32)] * 2, compiler_params=_params(("parallel",)),
    )(q, k, v, bias)


def attention_bwd(q, k, v, bias, dout):
    nh, t, _ = q.shape
    nchunks = t // CHUNK
    group = _pick(nchunks, (ATT_GROUP, 2, 1))

    def body(q_ref, k_ref, v_ref, b_ref, do_ref, dq_ref, dk_ref, dv_ref, db_ref, kp, vp, dkp, dvp):
        zeros = jnp.zeros((ATT_PAD, HEAD_DIM), F32)
        kp[pl.ds(0, ATT_PAD), :] = zeros
        vp[pl.ds(0, ATT_PAD), :] = zeros
        kp[pl.ds(ATT_PAD, t), :] = k_ref[...]
        vp[pl.ds(ATT_PAD, t), :] = v_ref[...]
        dkp[...] = jnp.zeros_like(dkp)
        dvp[...] = jnp.zeros_like(dvp)
        db_ref[...] = jnp.zeros_like(db_ref)
        bias_h = b_ref[...]

        def step(g, carry):
            cs = [g * group + u for u in range(group)]
            q0 = [pl.multiple_of(c * CHUNK, CHUNK) for c in cs]
            valid = tuple(_band_valid(c) for c in cs)
            _, vjp = jax.vjp(lambda a, b, cc, d: _attn_chunks(a, b, cc, d, valid),
                             tuple(q_ref[pl.ds(s, CHUNK), :] for s in q0), tuple(kp[pl.ds(s, BAND), :] for s in q0),
                             tuple(vp[pl.ds(s, BAND), :] for s in q0), bias_h)
            dq, dk, dv, db = vjp(tuple(do_ref[pl.ds(s, CHUNK), :] for s in q0))
            for u, s in enumerate(q0):
                dq_ref[pl.ds(s, CHUNK), :] = dq[u]
                dkp[pl.ds(s, BAND), :] += dk[u]
                dvp[pl.ds(s, BAND), :] += dv[u]
            db_ref[...] += db
            return carry

        lax.fori_loop(0, nchunks // group, step, 0)
        dk_ref[...] = dkp[pl.ds(ATT_PAD, t), :]
        dv_ref[...] = dvp[pl.ds(ATT_PAD, t), :]

    bias_spec = pl.BlockSpec((None, CHUNK, BAND), lambda h: (h, 0, 0))
    hs = jax.ShapeDtypeStruct((nh, t, HEAD_DIM), F32)
    return pl.pallas_call(
        body, name="attention_bwd", grid=(nh,), in_specs=[_head_spec(t)] * 3 + [bias_spec, _head_spec(t)],
        out_specs=[_head_spec(t)] * 3 + [bias_spec],
        out_shape=[hs, hs, hs, jax.ShapeDtypeStruct(bias.shape, F32)],
        scratch_shapes=[pltpu.VMEM((t + ATT_PAD, HEAD_DIM), F32)] * 4, compiler_params=_params(("parallel",)),
    )(q, k, v, bias, dout)


def _rel_onehot_t(i):
    j = lax.broadcasted_iota(jnp.int32, (N_REL, BAND), 1)
    r = lax.broadcasted_iota(jnp.int32, (N_REL, BAND), 0)
    idx = jnp.clip(i + ATT_PAD - j, -(CHUNK - 1), REL_CLIP) + (CHUNK - 1)
    return jnp.where(r == idx, 1.0, 0.0).astype(F32)


def bias_expand(rel):
    nh = rel.shape[0]

    def body(rel_ref, o_ref):
        o_ref[...] = _mm(rel_ref[...], _rel_onehot_t(pl.program_id(0)))

    return pl.pallas_call(
        body, name="bias_expand", grid=(CHUNK,), in_specs=[pl.BlockSpec((nh, N_REL), lambda i: (0, 0))],
        out_specs=pl.BlockSpec((None, nh, BAND), lambda i: (i, 0, 0)),
        out_shape=jax.ShapeDtypeStruct((CHUNK, nh, BAND), F32), compiler_params=_params(("parallel",)),
    )(rel)


def bias_reduce(dbias):
    nh = dbias.shape[1]

    def body(d_ref, o_ref):
        @pl.when(pl.program_id(0) == 0)
        def _():
            o_ref[...] = jnp.zeros_like(o_ref)

        o_ref[...] += _mm_nt(d_ref[...], _rel_onehot_t(pl.program_id(0)))

    return pl.pallas_call(
        body, name="bias_reduce", grid=(CHUNK,), in_specs=[pl.BlockSpec((None, nh, BAND), lambda i: (i, 0, 0))],
        out_specs=pl.BlockSpec((nh, N_REL), lambda i: (0, 0)), out_shape=jax.ShapeDtypeStruct((nh, N_REL), F32),
        compiler_params=_params(("arbitrary",)),
    )(dbias)


def _tri(n, strict):
    r = lax.broadcasted_iota(jnp.int32, (n, n), 0)
    c = lax.broadcasted_iota(jnp.int32, (n, n), 1)
    return (c < r) if strict else (c <= r)


def _each(f, *lists):
    return [f(*args) for args in zip(*lists)]


def _rwkv_chunk(s, r, k, v, w, a, b):
    n = r[0].shape[0]
    strict, incl = _tri(n, True), _tri(n, False)
    ones = jnp.where(incl, 1.0, 0.0).astype(F32)
    eye = jnp.where(incl & ~strict, 1.0, 0.0).astype(F32)
    lw = _each(jnp.log, w)
    cum = _each(lambda x: _mm(ones, x), lw)
    p_incl = _each(jnp.exp, cum)
    p_inv = _each(lambda x: jnp.exp(-x), cum)
    a_t = _each(lambda x, c, l: x * jnp.exp(c - l), a, cum, lw)
    r_t = _each(jnp.multiply, r, p_incl)
    b_t = _each(jnp.multiply, b, p_inv)
    k_t = _each(jnp.multiply, k, p_inv)
    a_ab = _each(lambda x, y: jnp.where(strict, _dot3(x, y, NT), 0.0), a_t, b_t)
    a_ak = _each(lambda x, y: jnp.where(strict, _dot3(x, y, NT), 0.0), a_t, k_t)
    r_b = _each(lambda x, y: jnp.where(incl, _dot3(x, y, NT), 0.0), r_t, b_t)
    r_k = _each(lambda x, y: jnp.where(incl, _dot3(x, y, NT), 0.0), r_t, k_t)
    a_s = _each(lambda x, y: _dot3(x, y, NT), a_t, s)
    r_s = _each(lambda x, y: _dot3(x, y, NT), r_t, s)
    a_kv = _each(lambda x, y: _dot3(x, y, NN), a_ak, v)
    total = _each(lambda x: eye + x, a_ab)
    power = _each(lambda x: _dot3(x, x, NN), a_ab)
    rounds = int(math.log2(n)) - 1
    for i in range(rounds):
        total = _each(lambda t, p: t + _dot3(t, p, NN), total, power)
        if i < rounds - 1:
            power = _each(lambda p: _dot3(p, p, NN), power)
    sa = _each(lambda t, x, y: _dot3(t, x + y, NN), total, a_s, a_kv)
    y = _each(lambda rs, rb, x, rk, vv: rs + _dot3(rb, x, NN) + _dot3(rk, vv, NN), r_s, r_b, sa, r_k, v)
    s_new = _each(lambda ss, x, bt, vv, kt, p: (ss + _dot3(x, bt, TN) + _dot3(vv, kt, TN)) * p[n - 1:n, :],
                  s, sa, b_t, v, k_t, p_incl)
    return tuple(s_new), tuple(y)


RWKV_TILE = 512
RWKV_HEADS = 4


def _rwkv_specs(nh, t, reverse):
    tile = min(RWKV_TILE, t)
    hb = RWKV_HEADS if nh % RWKV_HEADS == 0 else 1
    nt = t // tile
    per = tile // RWKV_CHUNK
    pos = (lambda h, i: (h, nt - 1 - i, 0)) if reverse else (lambda h, i: (h, i, 0))
    pos4 = (lambda h, i: (h, nt - 1 - i, 0, 0)) if reverse else (lambda h, i: (h, i, 0, 0))
    return (hb, nt, per, pl.BlockSpec((hb, tile, HEAD_DIM), pos), pl.BlockSpec((hb, per, HEAD_DIM, HEAD_DIM), pos4))


def _hosted_steps(exchange, refs_in, refs_out, sems, n_steps, forward_at):
    if exchange is None:
        return
    start, forward, finish = exchange.plan(refs_in, refs_out, *sems)
    step = pl.program_id(0) * pl.num_programs(1) + pl.program_id(1)
    pl.when(step == 0)(start)
    pl.when(step == forward_at)(forward)
    pl.when(step == n_steps - 1)(finish)


def _hosted_call_args(exchange):
    if exchange is None:
        return [], [], [], []
    nb = len(exchange.arrays)
    return exchange.arrays, [HBM_SPEC] * nb, exchange.out_shape, exchange.scratch()


def rwkv_fwd(r, k, v, w, a, b, hosted=None):
    nh, t, _ = r.shape
    hb, nt, per, row_spec, s_spec = _rwkv_specs(nh, t, False)
    h_arrays, h_specs, h_shapes, h_scratch = _hosted_call_args(hosted)
    nhosted = len(h_arrays)
    n_steps = (nh // hb) * nt

    def body(r_ref, k_ref, v_ref, w_ref, a_ref, b_ref, *rest):
        h_in, (y_ref, s_ref), h_out = rest[:nhosted], rest[nhosted:nhosted + 2], rest[nhosted + 2:2 * nhosted + 2]
        state, sems = rest[2 * nhosted + 2], rest[2 * nhosted + 3:]
        _hosted_steps(hosted, h_in, h_out, sems, n_steps, (3 * n_steps) // 4)

        @pl.when(pl.program_id(1) == 0)
        def _():
            state[...] = jnp.zeros_like(state)

        def step(c, states):
            rows = pl.ds(pl.multiple_of(c * RWKV_CHUNK, RWKV_CHUNK), RWKV_CHUNK)
            for h in range(hb):
                s_ref[h, c] = states[h]
            s_new, y = _rwkv_chunk(states, *(tuple(ref[h, rows, :] for h in range(hb))
                                             for ref in (r_ref, k_ref, v_ref, w_ref, a_ref, b_ref)))
            for h in range(hb):
                y_ref[h, rows, :] = y[h]
            return s_new

        final = lax.fori_loop(0, per, step, tuple(state[h] for h in range(hb)))
        for h in range(hb):
            state[h] = final[h]

    return pl.pallas_call(
        body, name="rwkv_fwd", grid=(nh // hb, nt), in_specs=[row_spec] * 6 + h_specs,
        out_specs=[row_spec, s_spec] + h_specs,
        out_shape=[jax.ShapeDtypeStruct((nh, t, HEAD_DIM), F32),
                   jax.ShapeDtypeStruct((nh, t // RWKV_CHUNK, HEAD_DIM, HEAD_DIM), F32)] + h_shapes,
        scratch_shapes=[pltpu.VMEM((hb, HEAD_DIM, HEAD_DIM), F32)] + h_scratch,
        compiler_params=_params(("arbitrary", "arbitrary")),
    )(r, k, v, w, a, b, *h_arrays)


def rwkv_bwd(r, k, v, w, a, b, states, dy, hosted=None):
    nh, t, _ = r.shape
    hb, nt, per, row_spec, s_spec = _rwkv_specs(nh, t, True)
    h_arrays, h_specs, h_shapes, h_scratch = _hosted_call_args(hosted)
    nhosted = len(h_arrays)
    n_steps = (nh // hb) * nt

    def body(r_ref, k_ref, v_ref, w_ref, a_ref, b_ref, s_ref, dy_ref, *rest):
        h_in, d_refs, h_out = rest[:nhosted], rest[nhosted:nhosted + 6], rest[nhosted + 6:2 * nhosted + 6]
        dstate, sems = rest[2 * nhosted + 6], rest[2 * nhosted + 7:]
        _hosted_steps(hosted, h_in, h_out, sems, n_steps, n_steps // 2)

        @pl.when(pl.program_id(1) == 0)
        def _():
            dstate[...] = jnp.zeros_like(dstate)

        def step(i, ds):
            c = per - 1 - i
            rows = pl.ds(pl.multiple_of(c * RWKV_CHUNK, RWKV_CHUNK), RWKV_CHUNK)
            _, vjp = jax.vjp(_rwkv_chunk, tuple(s_ref[h, c] for h in range(hb)),
                             *(tuple(ref[h, rows, :] for h in range(hb))
                               for ref in (r_ref, k_ref, v_ref, w_ref, a_ref, b_ref)))
            grads = vjp((ds, tuple(dy_ref[h, rows, :] for h in range(hb))))
            for d_ref, g in zip(d_refs, grads[1:]):
                for h in range(hb):
                    d_ref[h, rows, :] = g[h]
            return grads[0]

        final = lax.fori_loop(0, per, step, tuple(dstate[h] for h in range(hb)))
        for h in range(hb):
            dstate[h] = final[h]

    hs = jax.ShapeDtypeStruct((nh, t, HEAD_DIM), F32)
    return pl.pallas_call(
        body, name="rwkv_bwd", grid=(nh // hb, nt), in_specs=[row_spec] * 6 + [s_spec, row_spec] + h_specs,
        out_specs=[row_spec] * 6 + h_specs, out_shape=[hs] * 6 + h_shapes,
        scratch_shapes=[pltpu.VMEM((hb, HEAD_DIM, HEAD_DIM), F32)] + h_scratch,
        compiler_params=_params(("arbitrary", "arbitrary")),
    )(r, k, v, w, a, b, states, dy, *h_arrays)


SUBLANES = 8


def s5_scan(bu_re, bu_im, a_re, a_im, *, reverse=False, h_prev=None):
    t, n = bu_re.shape
    tt = _pick(t, (512, 256, 128, 64, 32, 16, 8))
    tc = _pick(n, (1024, 512, 256, 128))
    nt = t // tt
    with_da = h_prev is not None
    sign = -1.0 if reverse else 1.0

    def body(*refs):
        if with_da:
            br, bi, ar_ref, ai_ref, pr, pi, hr_ref, hi_ref, dar_ref, dai_ref, sr, si = refs
        else:
            br, bi, ar_ref, ai_ref, hr_ref, hi_ref, sr, si = refs
        ti = pl.program_id(1)

        @pl.when(ti == 0)
        def _():
            sr[...] = jnp.zeros_like(sr)
            si[...] = jnp.zeros_like(si)
            if with_da:
                dar_ref[...] = jnp.zeros_like(dar_ref)
                dai_ref[...] = jnp.zeros_like(dai_ref)

        ar = ar_ref[...]
        ai = ai_ref[...] * sign

        def group(gi, carry):
            hr, hi, dar, dai = carry
            g0 = pl.multiple_of((tt // SUBLANES - 1 - gi if reverse else gi) * SUBLANES, SUBLANES)
            rows = pl.ds(g0, SUBLANES)
            xr, xi = br[rows, :], bi[rows, :]
            if with_da:
                qr, qi = pr[rows, :], pi[rows, :]
            out_r, out_i = [None] * SUBLANES, [None] * SUBLANES
            for s in (range(SUBLANES - 1, -1, -1) if reverse else range(SUBLANES)):
                hr, hi = ar * hr - ai * hi + xr[s:s + 1, :], ar * hi + ai * hr + xi[s:s + 1, :]
                out_r[s], out_i[s] = hr, hi
                if with_da:
                    dar = dar + hr * qr[s:s + 1, :] + hi * qi[s:s + 1, :]
                    dai = dai + hi * qr[s:s + 1, :] - hr * qi[s:s + 1, :]
            hr_ref[rows, :] = jnp.concatenate(out_r, axis=0)
            hi_ref[rows, :] = jnp.concatenate(out_i, axis=0)
            return hr, hi, dar, dai

        zero = jnp.zeros((1, tc), F32)
        hr, hi, dar, dai = lax.fori_loop(0, tt // SUBLANES, group, (sr[...], si[...], zero, zero))
        sr[...] = hr
        si[...] = hi
        if with_da:
            dar_ref[...] += dar
            dai_ref[...] += dai

    tile = pl.BlockSpec((tt, tc), (lambda ci, ti: (nt - 1 - ti, ci)) if reverse else (lambda ci, ti: (ti, ci)))
    col = pl.BlockSpec((1, tc), lambda ci, ti: (0, ci))
    hs = jax.ShapeDtypeStruct((t, n), F32)
    cs = jax.ShapeDtypeStruct((1, n), F32)
    ins = [bu_re, bu_im, a_re, a_im] + (list(h_prev) if with_da else [])
    return pl.pallas_call(
        body, name="s5_scan_bwd" if reverse else "s5_scan_fwd", grid=(n // tc, nt),
        in_specs=[tile, tile, col, col] + ([tile, tile] if with_da else []),
        out_specs=[tile, tile] + ([col, col] if with_da else []), out_shape=[hs, hs] + ([cs, cs] if with_da else []),
        scratch_shapes=[pltpu.VMEM((1, tc), F32)] * 2, compiler_params=_params(("parallel", "arbitrary")),
    )(*ins)


N_CHIPS = 4
HBM_SPEC = pl.BlockSpec(memory_space=pl.ANY)


def _remote(src, dst, send_sem, recv_sem, device):
    return pltpu.make_async_remote_copy(src_ref=src, dst_ref=dst, send_sem=send_sem, recv_sem=recv_sem,
                                        device_id=device, device_id_type=pl.DeviceIdType.MESH)


class Exchange:
    def __init__(self, arrays, out_shape, n_sems, plan):
        self.arrays, self.out_shape, self.n_sems, self.plan = list(arrays), out_shape, n_sems, plan

    def scratch(self):
        nb = len(self.arrays)
        return [pltpu.SemaphoreType.DMA((nb, self.n_sems)), pltpu.SemaphoreType.DMA((nb, self.n_sems)),
                pltpu.SemaphoreType.DMA((nb,))]

    def run(self, name):
        nb = len(self.arrays)

        def body(*refs):
            for step in self.plan(refs[:nb], refs[nb:2 * nb], *refs[2 * nb:]):
                step()

        return pl.pallas_call(
            body, name=name, in_specs=[HBM_SPEC] * nb, out_specs=[HBM_SPEC] * nb, out_shape=self.out_shape,
            scratch_shapes=self.scratch(), compiler_params=pltpu.CompilerParams(has_side_effects=True),
        )(*self.arrays)


def gather_all(arrays):
    nb = len(arrays)

    def plan(ins, outs, send_sems, recv_sems, local_sems):
        x, y, c = (lax.axis_index(n) for n in MESH_AXES)
        me, sibling = 4 * x + 2 * y + c, 4 * x + 2 * y + 1 - c
        chips = [(1 - x, y), (x, 1 - y), (1 - x, 1 - y)]

        def copy(b, k, slot, to, src=None):
            block = outs[b].at[slot]
            return _remote(block if src is None else src, block, send_sems.at[b, k], recv_sems.at[b, k], to)

        def local():
            return [pltpu.make_async_copy(ins[b], outs[b].at[me], local_sems.at[b]) for b in range(nb)]

        def first():
            return [cp for b in range(nb) for cp in
                    [copy(b, 0, me, (x, y, 1 - c), src=ins[b])]
                    + [copy(b, 1 + j, me, (px, py, c), src=ins[b]) for j, (px, py) in enumerate(chips)]]

        def passed():
            return [copy(b, 4 + j, 4 * px + 2 * py + c, (x, y, 1 - c)) for j, (px, py) in enumerate(chips) for b in range(nb)]

        def start():
            for cp in local() + first():
                cp.start()

        def forward():
            for j, (px, py) in enumerate(chips):
                for b in range(nb):
                    copy(b, 1 + j, 4 * px + 2 * py + c, (px, py, c)).wait_recv()
                    copy(b, 4 + j, 4 * px + 2 * py + c, (x, y, 1 - c)).start()

        def finish():
            for b in range(nb):
                copy(b, 0, sibling, (x, y, 1 - c)).wait_recv()
                for j, (px, py) in enumerate(chips):
                    copy(b, 4 + j, 4 * px + 2 * py + 1 - c, (x, y, 1 - c)).wait_recv()
            for cp in first() + passed():
                cp.wait_send()
            for cp in local():
                cp.wait()

        return start, forward, finish

    out_shape = [jax.ShapeDtypeStruct((N_DEV,) + tuple(a.shape), a.dtype) for a in arrays]
    return Exchange(arrays, out_shape, N_DEV - 1, plan)


def _nothing():
    pass


def pair_exchange(arrays):
    nb = len(arrays)

    def plan(ins, outs, send_sems, recv_sems, local_sems):
        x, y, c = (lax.axis_index(n) for n in MESH_AXES)

        def copies():
            return [_remote(ins[b].at[2 * z + 1 - c], outs[b].at[z], send_sems.at[b, z], recv_sems.at[b, z], (x, y, 1 - c))
                    for b in range(nb) for z in range(N_CHIPS)]

        def start():
            for cp in copies():
                cp.start()

        def finish():
            for cp in copies():
                cp.wait_send()
            for cp in copies():
                cp.wait_recv()

        return start, _nothing, finish

    out_shape = [jax.ShapeDtypeStruct((N_CHIPS,) + tuple(a.shape[1:]), a.dtype) for a in arrays]
    return Exchange(arrays, out_shape, N_CHIPS, plan)


def pair_sum(name, pieces, from_sibling):
    _, r, c = pieces.shape
    tr = _row_tile(r, c * pieces.dtype.itemsize, 1 << 20)

    def body(p_ref, s_ref, o_ref):
        mine = p_ref[lax.axis_index("c")]
        o_ref[...] = (mine.astype(F32) + s_ref[...].astype(F32)).astype(o_ref.dtype)

    return pl.pallas_call(
        body, name=name, grid=(N_CHIPS, r // tr),
        in_specs=[pl.BlockSpec((None, 2, tr, c), lambda z, i: (z, 0, i, 0)), pl.BlockSpec((None, tr, c), lambda z, i: (z, i, 0))],
        out_specs=pl.BlockSpec((None, tr, c), lambda z, i: (z, i, 0)),
        out_shape=jax.ShapeDtypeStruct((N_CHIPS, r, c), pieces.dtype), compiler_params=_params(("parallel", "parallel")),
    )(pieces.reshape(N_CHIPS, 2, r, c), from_sibling)


def chip_exchange(arrays):
    nb = len(arrays)

    def plan(ins, outs, send_sems, recv_sems, local_sems):
        x, y, c = (lax.axis_index(n) for n in MESH_AXES)
        my_chip = 2 * x + y
        chips = [(1 - x, y), (x, 1 - y), (1 - x, 1 - y)]

        def local():
            return [pltpu.make_async_copy(ins[b].at[my_chip], outs[b].at[my_chip], local_sems.at[b]) for b in range(nb)]

        def copies():
            return [_remote(ins[b].at[2 * px + py], outs[b].at[my_chip], send_sems.at[b, j], recv_sems.at[b, j], (px, py, c))
                    for b in range(nb) for j, (px, py) in enumerate(chips)]

        def start():
            for cp in local() + copies():
                cp.start()

        def finish():
            for cp in copies():
                cp.wait_send()
            for cp in copies():
                cp.wait_recv()
            for cp in local():
                cp.wait()

        return start, _nothing, finish

    out_shape = [jax.ShapeDtypeStruct(a.shape, a.dtype) for a in arrays]
    return Exchange(arrays, out_shape, N_CHIPS - 1, plan)


def adam_update(name, pieces, w, m, v):
    r, c = w.shape
    n_pieces = pieces.shape[0]
    tr = _row_tile(r, 4 * c, 1 << 19)

    def body(p_ref, w_ref, m_ref, v_ref, g_ref, d_ref, mo_ref, vo_ref):
        g = p_ref[0].astype(F32)
        for j in range(1, n_pieces):
            g = g + p_ref[j].astype(F32)
        m_new = ADAM_B1 * m_ref[...] + (1.0 - ADAM_B1) * g
        v_new = ADAM_B2 * v_ref[...] + (1.0 - ADAM_B2) * (g * g)
        m_hat = m_new / (1.0 - ADAM_B1 ** ADAM_STEP)
        v_hat = v_new / (1.0 - ADAM_B2 ** ADAM_STEP)
        g_ref[...] = g
        d_ref[...] = -ADAM_LR * (m_hat / (jnp.sqrt(v_hat) + ADAM_EPS) + ADAM_WD * w_ref[...])
        mo_ref[...] = m_new
        vo_ref[...] = v_new

    row = pl.BlockSpec((tr, c), lambda i: (i, 0))
    out = jax.ShapeDtypeStruct((r, c), F32)
    return pl.pallas_call(
        body, name=name, grid=(r // tr,), in_specs=[pl.BlockSpec((n_pieces, tr, c), lambda i: (0, i, 0)), row, row, row],
        out_specs=[row] * 4, out_shape=[out] * 4, compiler_params=_params(("parallel",)),
    )(pieces, w, m, v)


PACK_WIDTH = 1024
PACK_ROWS = 64


def _pack(arrays, dtype, lead=0):
    parts = []
    for a in arrays:
        head = a.shape[:lead]
        f = a.reshape(head + (-1,)).astype(dtype)
        pad = (-f.shape[-1]) % PACK_WIDTH
        if pad:
            f = jnp.pad(f, [(0, 0)] * lead + [(0, pad)])
        parts.append(f.reshape(head + (-1, PACK_WIDTH)))
    out = jnp.concatenate(parts, axis=lead)
    pad = (-out.shape[lead]) % PACK_ROWS
    if pad:
        out = jnp.pad(out, [(0, 0)] * lead + [(0, pad), (0, 0)])
    return out


def _unpack(packed, shapes, lead=0):
    head = packed.shape[:lead]
    out, row = [], 0
    for s in shapes:
        n = int(np.prod(s))
        rows = -(-n // PACK_WIDTH)
        chunk = lax.slice_in_dim(packed, row, row + rows, axis=lead).reshape(head + (-1,))
        out.append(lax.slice_in_dim(chunk, 0, n, axis=lead).reshape(head + tuple(s)))
        row += rows
    return out


def _to_natural(stacked, kind):
    if kind == "col":
        m = jnp.moveaxis(stacked, 0, -2)
        return m.reshape(m.shape[:-2] + (m.shape[-2] * m.shape[-1],))
    m = jnp.moveaxis(stacked, 0, -3)
    return m.reshape(m.shape[:-3] + (m.shape[-3] * m.shape[-2], m.shape[-1]))


def _to_stacked(natural, kind):
    if kind == "col":
        m = natural.reshape(natural.shape[:-1] + (N_DEV, natural.shape[-1] // N_DEV))
        return jnp.moveaxis(m, -2, 0)
    m = natural.reshape(natural.shape[:-2] + (N_DEV, natural.shape[-2] // N_DEV, natural.shape[-1]))
    return jnp.moveaxis(m, -3, 0)


def _f_rms(h, g):
    return (_rms(h, g).astype(BF16),)


def _f_swiglu(g, u):
    g, u = g.astype(F32), u.astype(F32)
    return ((g * _sigmoid(g) * u).astype(BF16),)


def _f_ple(h, pre, pp):
    return (h + _sigmoid(pre) * pp,)


def _head_rms(x, g):
    ms = _seg_sum(x * x, HEAD_DIM) * (1.0 / HEAD_DIM)
    return x * _seg_expand(lax.rsqrt(ms + RMS_EPS), HEAD_DIM) * _tile_lanes(g, x.shape[1] // HEAD_DIM)


def _f_attpre(q, k, q_gain, k_gain):
    return _head_rms(q, q_gain) * (HEAD_DIM ** -0.5), _head_rms(k, k_gain)


def _f_shift(z, z_prev, mu):
    return (z + (z_prev - z) * mu,)


def _f_rwkvpre(r, k, v, xw, xa, xg, w0, w_up, a0, a_up, g_up, k_k, k_a):
    del r, v
    w_log = -_softplus(-(w0 + _mm(jnp.tanh(xw), w_up))) - 0.5
    decay = jnp.exp(-jnp.exp(w_log))
    a = _sigmoid(a0 + _mm(xa, a_up))
    g = _mm(_sigmoid(xg), g_up)
    kk = k * k_k
    norm = jnp.sqrt(_seg_expand(_seg_sum(kk * kk, HEAD_DIM), HEAD_DIM))
    kk = kk / jnp.maximum(norm, 1e-12)
    return k * (1.0 + (a - 1.0) * k_a), decay, -kk, kk * a, g


def _f_rwkvpost(y, r, k, v, g, lnx_w, lnx_b, r_k):
    mean = _seg_expand(_seg_sum(y, HEAD_DIM) * (1.0 / HEAD_DIM), HEAD_DIM)
    yc = y - mean
    var = _seg_expand(_seg_sum(yc * yc, HEAD_DIM) * (1.0 / HEAD_DIM), HEAD_DIM)
    yn = yc * lax.rsqrt(var + GN_EPS) * lnx_w + lnx_b
    bonus = _seg_expand(_seg_sum(r * k * r_k, HEAD_DIM), HEAD_DIM) * v
    return ((yn + bonus) * g,)


def _f_s5disc(lam_re, lam_im, log_dt):
    dt = jnp.exp(log_dt)
    mag = jnp.exp(lam_re * dt)
    ab_re, ab_im = mag * jnp.cos(lam_im * dt), mag * jnp.sin(lam_im * dt)
    denom = lam_re * lam_re + lam_im * lam_im
    z_re = ((ab_re - 1.0) * lam_re + ab_im * lam_im) / denom
    z_im = (ab_im * lam_re - (ab_re - 1.0) * lam_im) / denom
    return ab_re, ab_im, z_re, z_im


def _f_s5b(z_re, z_im, b_re, b_im):
    return z_re * b_re - z_im * b_im, z_re * b_im + z_im * b_re


def _f_s5post(ypre, u, d_skip):
    return (_gelu_tanh(ypre + d_skip * u).astype(BF16),)


def _f_glu(h, z1, z2):
    return (h + z1 * _sigmoid(z2),)


def _heads(x):
    t = x.shape[0]
    return jnp.transpose(x.reshape(t, -1, HEAD_DIM), (1, 0, 2))


def _unheads(x):
    return jnp.transpose(x, (1, 0, 2)).reshape(x.shape[1], -1)


def _shift_down(x):
    return jnp.pad(x[:-1], ((1, 0), (0, 0)))


def _shift_up(x):
    return jnp.pad(x[1:], ((0, 1), (0, 0)))


def _block_diag(blocks):
    g, a, b = blocks.shape
    eye = jnp.eye(g, dtype=blocks.dtype)
    return (blocks[:, :, None, :] * eye[:, None, :, None]).reshape(g * a, g * b)


def _diag_blocks(dense, g):
    a, b = dense.shape[0] // g, dense.shape[1] // g
    return jnp.stack([dense[i * a:(i + 1) * a, i * b:(i + 1) * b] for i in range(g)], axis=0)


def _row_tile(t, width_bytes, budget=2 * 1024 * 1024):
    for tr in (512, 256, 128, 64, 32, 16, 8):
        if t % tr == 0 and tr * width_bytes <= budget:
            return tr
    return t


def _ffn_fwd(tag, h, norm_w, w_gate, w_up, w_down):
    t, d = h.shape
    f = w_gate.shape[1]
    n = rowwise(f"{tag}_norm", _f_rms, [h], [norm_w], [(d, BF16)], _row_tile(t, 4 * d))[0]
    g = matmul(f"{tag}_gate", n, w_gate, out_dtype=BF16)
    u = matmul(f"{tag}_up", n, w_up, out_dtype=BF16)
    a = rowwise(f"{tag}_act", _f_swiglu, [g, u], [], [(f, BF16)], _row_tile(t, 4 * f))[0]
    return matmul(f"{tag}_down", a, w_down, alpha=0.5, res=h), (h, n, g, u, a)


def _ffn_bwd(tag, dh2, saved, norm_w, w_gate, w_up, w_down):
    h, n, g, u, a = saved
    t, d = h.shape
    f = w_gate.shape[1]
    da = matmul(f"{tag}_da", dh2, w_down, tb=True, alpha=0.5, out_dtype=BF16)
    d_down = matmul(f"{tag}_dwdown", a, dh2, ta=True, alpha=0.5, out_dtype=BF16)
    dg, du = rowwise_vjp(f"{tag}_dact", _f_swiglu, [g, u], [], [da], _row_tile(t, 4 * f, 1 << 20), [True, True], [],
                         grad_dtypes={0: BF16, 1: BF16})
    dn = matmul(f"{tag}_dn_gate", dg, w_gate, tb=True)
    dn = matmul(f"{tag}_dn_up", du, w_up, tb=True, res=dn)
    d_gate = matmul(f"{tag}_dwgate", n, dg, ta=True, out_dtype=BF16)
    d_up = matmul(f"{tag}_dwup", n, du, ta=True, out_dtype=BF16)
    dh, d_norm = rowwise_vjp(f"{tag}_dnorm", _f_rms, [h], [norm_w], [dn], _row_tile(t, 4 * d), [True], [True],
                             add_to={0: dh2})
    return dh, d_norm, d_gate, d_up, d_down


def _ple_fwd(tag, h, norm_w, w_gate, w_proj, p_i):
    t, d = h.shape
    n = rowwise(f"{tag}_norm", _f_rms, [h], [norm_w], [(d, BF16)], _row_tile(t, 4 * d))[0]
    pre = matmul(f"{tag}_gate", n, w_gate)
    pp = matmul(f"{tag}_proj", p_i, w_proj)
    h2 = rowwise(f"{tag}_out", _f_ple, [h, pre, pp], [], [(d, F32)], _row_tile(t, 4 * d))[0]
    return h2, (h, n, pre, pp)


def _ple_bwd(tag, dh2, saved, norm_w, w_gate, w_proj, p_i):
    h, n, pre, pp = saved
    t, d = h.shape
    dpre, dpp = rowwise_vjp(f"{tag}_dout", _f_ple, [h, pre, pp], [], [dh2], _row_tile(t, 4 * d), [False, True, True], [],
                            grad_dtypes={1: BF16, 2: BF16})
    d_proj = matmul(f"{tag}_dwproj", p_i, dpp, ta=True, out_dtype=BF16)
    d_gate = matmul(f"{tag}_dwgate", n, dpre, ta=True, out_dtype=BF16)
    dn = matmul(f"{tag}_dn", dpre, w_gate, tb=True)
    dh, d_norm = rowwise_vjp(f"{tag}_dnorm", _f_rms, [h], [norm_w], [dn], _row_tile(t, 4 * d), [True], [True],
                             add_to={0: dh2})
    return dh, d_norm, d_gate, d_proj


def _ab_fwd(h, w, hosted=None, hosted_done=None):
    t, d = h.shape
    da = d // 2
    n = rowwise("ab_norm", _f_rms, [h], [w["mix_norm"]], [(d, BF16)], _row_tile(t, 4 * d))[0]
    proj = matmul("ab_in", n, w["ab_w_in"])
    q_raw, k_raw, v_att, z = proj[:, :da], proj[:, da:2 * da], proj[:, 2 * da:3 * da], proj[:, 3 * da:]
    tr = _row_tile(t, 4 * da, 1 << 19)
    qn, kn = rowwise("att_pre", _f_attpre, [q_raw, k_raw], [w["att_q_gain"], w["att_k_gain"]], [(da, F32)] * 2, tr)
    bias = jnp.transpose(bias_expand(w["att_rel_bias"]), (1, 0, 2))
    qh, kh, vh = _heads(qn), _heads(kn), _heads(v_att)
    att = _unheads(attention_fwd(qh, kh, vh, bias))
    z_prev = _shift_down(z)
    zz = rowwise("rwkv_shift", _f_shift, [z, z_prev], [w["rwkv_mu"]], [(z.shape[1], F32)], _row_tile(t, 4 * z.shape[1]))[0]
    o = [0, da, 2 * da, 3 * da, 3 * da + DECAY_LORA, 3 * da + DECAY_LORA + AAA_LORA, z.shape[1]]
    r, k, v, xw, xa, xg = (zz[:, o[i]:o[i + 1]] for i in range(6))
    pre_consts = [w[nm] for nm in ("rwkv_w0", "rwkv_w_up", "rwkv_a0", "rwkv_a_up", "rwkv_g_up", "rwkv_k_k", "rwkv_k_a")]
    k2, decay, ia, ib, g = rowwise("rwkv_pre", _f_rwkvpre, [r, k, v, xw, xa, xg], pre_consts, [(da, F32)] * 5, tr)
    heads = [_heads(a) for a in (r, k2, v, decay, ia, ib)]
    y_h, states, *travelled = rwkv_fwd(*heads, hosted=hosted)
    w_out = w["ab_w_out"] if hosted is None else hosted_done(travelled)
    y = _unheads(y_h)
    post_consts = [w["rwkv_lnx_w"], w["rwkv_lnx_b"], w["rwkv_r_k"]]
    rw = rowwise("rwkv_post", _f_rwkvpost, [y, r, k2, v, g], post_consts, [(da, F32)], tr)[0]
    cat = jnp.concatenate([att, rw], axis=1).astype(BF16)
    h2 = matmul("ab_out", cat, w_out, res=h)
    saved = dict(h=h, n=n, q_raw=q_raw, k_raw=k_raw, qh=qh, kh=kh, vh=vh, bias=bias, z=z, z_prev=z_prev,
                 rows=(r, k, v, xw, xa, xg), pre_consts=pre_consts, post=(y, r, k2, v, g), post_consts=post_consts,
                 heads=heads, states=states, cat=cat, tr=tr)
    return h2, saved


def _ab_bwd(dh2, s, w, make_hosted=None, hosted_done=None):
    h = s["h"]
    t, d = h.shape
    da = d // 2
    tr = s["tr"]
    grads = {}
    dcat = matmul("ab_dcat", dh2, w["ab_w_out"], tb=True)
    grads["ab_w_out"] = matmul("ab_dwout", s["cat"], dh2, ta=True, out_dtype=BF16)
    d_att, d_rw = dcat[:, :da], dcat[:, da:]
    dy, dr1, dk1, dv1, dg, grads["rwkv_lnx_w"], grads["rwkv_lnx_b"], grads["rwkv_r_k"] = rowwise_vjp(
        "rwkv_dpost", _f_rwkvpost, list(s["post"]), s["post_consts"], [d_rw], tr, [True] * 5, [True] * 3)
    hosted = None if make_hosted is None else make_hosted(grads["ab_w_out"])
    *d_heads, = rwkv_bwd(*s["heads"], s["states"], _heads(dy), hosted=hosted)
    if hosted is not None:
        hosted_done(d_heads[6:])
        d_heads = d_heads[:6]
    dr2, dk2, dv2, ddecay, dia, dib = (_unheads(a) for a in d_heads)
    pre = rowwise_vjp("rwkv_dpre", _f_rwkvpre, list(s["rows"]), s["pre_consts"], [(dk1, dk2), ddecay, dia, dib, dg], tr,
                      [True] * 6, [True] * 7, add_to={0: (dr1, dr2), 2: (dv1, dv2)})
    for nm, g in zip(("rwkv_w0", "rwkv_w_up", "rwkv_a0", "rwkv_a_up", "rwkv_g_up", "rwkv_k_k", "rwkv_k_a"), pre[6:]):
        grads[nm] = g
    dzz = jnp.concatenate(pre[:6], axis=1)
    trz = _row_tile(t, 4 * dzz.shape[1])
    grads["rwkv_mu"] = rowwise_vjp("rwkv_dmu", _f_shift, [s["z"], s["z_prev"]], [w["rwkv_mu"]], [dzz], trz,
                                   [False, False], [True])[0]
    dz = rowwise("rwkv_dshift", _f_shift, [dzz, _shift_up(dzz)], [w["rwkv_mu"]], [(dzz.shape[1], F32)], trz)[0]
    dqh, dkh, dvh, dbias = attention_bwd(s["qh"], s["kh"], s["vh"], s["bias"], _heads(d_att))
    grads["att_rel_bias"] = bias_reduce(jnp.transpose(dbias, (1, 0, 2)))
    dq_raw, dk_raw, grads["att_q_gain"], grads["att_k_gain"] = rowwise_vjp(
        "att_dpre", _f_attpre, [s["q_raw"], s["k_raw"]], [w["att_q_gain"], w["att_k_gain"]],
        [_unheads(dqh), _unheads(dkh)], tr, [True, True], [True, True])
    dproj = jnp.concatenate([dq_raw, dk_raw, _unheads(dvh), dz], axis=1).astype(BF16)
    dn = matmul("ab_dn", dproj, w["ab_w_in"], tb=True)
    grads["ab_w_in"] = matmul("ab_dwin", s["n"], dproj, ta=True, out_dtype=BF16)
    dh, grads["mix_norm"] = rowwise_vjp("ab_dnorm", _f_rms, [h], [w["mix_norm"]], [dn], _row_tile(t, 4 * d), [True], [True],
                                        add_to={0: dh2})
    return dh, grads


def _s5_fwd(h, w):
    t, d = h.shape
    n_groups, n_state = w["ssm_lambda_re"].shape
    gp = n_groups * n_state
    n = rowwise("s5_norm", _f_rms, [h], [w["mix_norm"]], [(d, BF16)], _row_tile(t, 4 * d))[0]
    u = matmul("s5_in", n, w["ssm_w_in"])
    disc_rows = [w["ssm_lambda_re"], w["ssm_lambda_im"], w["ssm_log_dt"]]
    ab_re, ab_im, z_re, z_im = rowwise("s5_disc", _f_s5disc, disc_rows, [], [(n_state, F32)] * 4, n_groups)
    b_rows = [z_re.reshape(gp, 1), z_im.reshape(gp, 1), w["ssm_b_re"], w["ssm_b_im"]]
    trb = _row_tile(gp, 512)
    bb_re, bb_im = rowwise("s5_bbar", _f_s5b, b_rows, [], [(SSM_GROUP, F32)] * 2, trb)
    to_dense = lambda bb: _block_diag(jnp.transpose(bb.reshape(n_groups, n_state, SSM_GROUP), (0, 2, 1)))
    bd_re, bd_im = to_dense(bb_re), to_dense(bb_im)
    cd_re = _block_diag(jnp.transpose(w["ssm_c_re"], (0, 2, 1)))
    cd_im = -_block_diag(jnp.transpose(w["ssm_c_im"], (0, 2, 1)))
    a_re, a_im = ab_re.reshape(1, gp), ab_im.reshape(1, gp)
    bu_re = matmul("s5_bu_re", u, bd_re)
    bu_im = matmul("s5_bu_im", u, bd_im)
    h_re, h_im = s5_scan(bu_re, bu_im, a_re, a_im)
    ypre = matmul("s5_y_re", h_re, cd_re)
    ypre = matmul("s5_y_im", h_im, cd_im, res=ypre)
    tru = _row_tile(t, 4 * u.shape[1])
    yg = rowwise("s5_post", _f_s5post, [ypre, u], [w["ssm_d"]], [(u.shape[1], BF16)], tru)[0]
    w_out1, w_out2 = w["ssm_w_out"][:, :d], w["ssm_w_out"][:, d:]
    z1 = matmul("s5_out1", yg, w_out1)
    z2 = matmul("s5_out2", yg, w_out2)
    h2 = rowwise("s5_glu", _f_glu, [h, z1, z2], [], [(d, F32)], _row_tile(t, 4 * d))[0]
    saved = dict(h=h, n=n, u=u, disc_rows=disc_rows, b_rows=b_rows, trb=trb, bd=(bd_re, bd_im), cd=(cd_re, cd_im),
                 a=(a_re, a_im), hs=(h_re, h_im), ypre=ypre, yg=yg, w_out=(w_out1, w_out2), z=(z1, z2), tru=tru)
    return h2, saved


def _s5_bwd(dh2, s, w):
    h, u = s["h"], s["u"]
    t, d = h.shape
    n_groups, n_state = w["ssm_lambda_re"].shape
    gp = n_groups * n_state
    grads = {}
    z1, z2 = s["z"]
    w_out1, w_out2 = s["w_out"]
    dz1, dz2 = rowwise_vjp("s5_dglu", _f_glu, [h, z1, z2], [], [dh2], _row_tile(t, 4 * d), [False, True, True], [],
                           grad_dtypes={1: BF16, 2: BF16})
    dyg = matmul("s5_dyg1", dz1, w_out1, tb=True)
    dyg = matmul("s5_dyg2", dz2, w_out2, tb=True, res=dyg)
    grads["ssm_w_out"] = jnp.concatenate([matmul("s5_dwout1", s["yg"], dz1, ta=True, out_dtype=BF16),
                                          matmul("s5_dwout2", s["yg"], dz2, ta=True, out_dtype=BF16)], axis=1)
    dypre, du1, grads["ssm_d"] = rowwise_vjp("s5_dpost", _f_s5post, [s["ypre"], u], [w["ssm_d"]], [dyg], s["tru"],
                                             [True, True], [True])
    cd_re, cd_im = s["cd"]
    h_re, h_im = s["hs"]
    dh_re = matmul("s5_dh_re", dypre, cd_re, tb=True)
    dh_im = matmul("s5_dh_im", dypre, cd_im, tb=True)
    dcd_re = matmul("s5_dc_re", h_re, dypre, ta=True)
    dcd_im = matmul("s5_dc_im", h_im, dypre, ta=True)
    a_re, a_im = s["a"]
    g_re, g_im, da_re, da_im = s5_scan(dh_re, dh_im, a_re, a_im, reverse=True,
                                       h_prev=(_shift_down(h_re), _shift_down(h_im)))
    bd_re, bd_im = s["bd"]
    du = matmul("s5_du_re", g_re, bd_re, tb=True, res=du1)
    du = matmul("s5_du_im", g_im, bd_im, tb=True, res=du)
    dbd_re = matmul("s5_db_re", u, g_re, ta=True)
    dbd_im = matmul("s5_db_im", u, g_im, ta=True)
    grads["ssm_w_in"] = matmul("s5_dwin", s["n"], du, ta=True, out_dtype=BF16)
    dn = matmul("s5_dn", du, w["ssm_w_in"], tb=True)
    dh, grads["mix_norm"] = rowwise_vjp("s5_dnorm", _f_rms, [h], [w["mix_norm"]], [dn], _row_tile(t, 4 * d), [True], [True],
                                        add_to={0: dh2})
    from_dense = lambda m: jnp.transpose(_diag_blocks(m, n_groups), (0, 2, 1)).reshape(gp, SSM_GROUP)
    dz_re, dz_im, grads["ssm_b_re"], grads["ssm_b_im"] = rowwise_vjp(
        "s5_dbbar", _f_s5b, s["b_rows"], [], [from_dense(dbd_re), from_dense(dbd_im)], s["trb"], [True] * 4, [])
    disc_cots = [da_re.reshape(n_groups, n_state), da_im.reshape(n_groups, n_state),
                 dz_re.reshape(n_groups, n_state), dz_im.reshape(n_groups, n_state)]
    grads["ssm_lambda_re"], grads["ssm_lambda_im"], grads["ssm_log_dt"] = rowwise_vjp(
        "s5_ddisc", _f_s5disc, s["disc_rows"], [], disc_cots, n_groups, [True] * 3, [])
    grads["ssm_c_re"] = jnp.transpose(_diag_blocks(dcd_re, n_groups), (0, 2, 1))
    grads["ssm_c_im"] = -jnp.transpose(_diag_blocks(dcd_im, n_groups), (0, 2, 1))
    return dh, grads


WEIGHTS = ["ffn1_norm", "ffn1_w_gate", "ffn1_w_up", "ffn1_w_down", "mix_norm", "ffn2_norm", "ffn2_w_gate", "ffn2_w_up",
           "ffn2_w_down", "ple_norm", "ple_w_gate", "ple_w_proj", "ab_w_in", "att_q_gain", "att_k_gain", "att_rel_bias",
           "rwkv_mu", "rwkv_w0", "rwkv_w_up", "rwkv_a0", "rwkv_a_up", "rwkv_g_up", "rwkv_k_k", "rwkv_k_a", "rwkv_r_k",
           "rwkv_lnx_w", "rwkv_lnx_b", "ab_w_out", "ssm_w_in", "ssm_lambda_re", "ssm_lambda_im", "ssm_log_dt", "ssm_b_re",
           "ssm_b_im", "ssm_c_re", "ssm_c_im", "ssm_d", "ssm_w_out"]
BIG = {"ffn1_w_gate": "col", "ffn1_w_up": "col", "ffn1_w_down": "row", "ffn2_w_gate": "col", "ffn2_w_up": "col",
       "ffn2_w_down": "row", "ple_w_gate": "row", "ple_w_proj": "col", "ab_w_in": "col", "ab_w_out": "row",
       "ssm_w_in": "row", "ssm_w_out": "col"}
SMALL_CUT = {"rwkv_w_up": "col", "rwkv_a_up": "col", "rwkv_g_up": "col", "ssm_d": "col"}
REPLICATED = [n for n in WEIGHTS if n not in BIG and n not in SMALL_CUT]


def kernel(x, p, ffn1_norm, ffn1_w_gate, ffn1_w_up, ffn1_w_down, mix_norm, ffn2_norm, ffn2_w_gate, ffn2_w_up, ffn2_w_down, ple_norm, ple_w_gate, ple_w_proj, ab_w_in, att_q_gain, att_k_gain, att_rel_bias, rwkv_mu, rwkv_w0, rwkv_w_up, rwkv_a0, rwkv_a_up, rwkv_g_up, rwkv_k_k, rwkv_k_a, rwkv_r_k, rwkv_lnx_w, rwkv_lnx_b, ab_w_out, ssm_w_in, ssm_lambda_re, ssm_lambda_im, ssm_log_dt, ssm_b_re, ssm_b_im, ssm_c_re, ssm_c_im, ssm_d, ssm_w_out, loss_target, m_ffn1_norm, m_ffn1_w_gate, m_ffn1_w_up, m_ffn1_w_down, m_mix_norm, m_ffn2_norm, m_ffn2_w_gate, m_ffn2_w_up, m_ffn2_w_down, m_ple_norm, m_ple_w_gate, m_ple_w_proj, m_ab_w_in, m_att_q_gain, m_att_k_gain, m_att_rel_bias, m_rwkv_mu, m_rwkv_w0, m_rwkv_w_up, m_rwkv_a0, m_rwkv_a_up, m_rwkv_g_up, m_rwkv_k_k, m_rwkv_k_a, m_rwkv_r_k, m_rwkv_lnx_w, m_rwkv_lnx_b, m_ab_w_out, m_ssm_w_in, m_ssm_lambda_re, m_ssm_lambda_im, m_ssm_log_dt, m_ssm_b_re, m_ssm_b_im, m_ssm_c_re, m_ssm_c_im, m_ssm_d, m_ssm_w_out, v_ffn1_norm, v_ffn1_w_gate, v_ffn1_w_up, v_ffn1_w_down, v_mix_norm, v_ffn2_norm, v_ffn2_w_gate, v_ffn2_w_up, v_ffn2_w_down, v_ple_norm, v_ple_w_gate, v_ple_w_proj, v_ab_w_in, v_att_q_gain, v_att_k_gain, v_att_rel_bias, v_rwkv_mu, v_rwkv_w0, v_rwkv_w_up, v_rwkv_a0, v_rwkv_a_up, v_rwkv_g_up, v_rwkv_k_k, v_rwkv_k_a, v_rwkv_r_k, v_rwkv_lnx_w, v_rwkv_lnx_b, v_ab_w_out, v_ssm_w_in, v_ssm_lambda_re, v_ssm_lambda_im, v_ssm_log_dt, v_ssm_b_re, v_ssm_b_im, v_ssm_c_re, v_ssm_c_im, v_ssm_d, v_ssm_w_out):
    vals = dict(locals())
    depth = ffn1_norm.shape[0]
    n_groups, n_state = ssm_lambda_re.shape[1:]

    kinds = {**BIG, **SMALL_CUT}
    travel = {n: (BF16 if n in BIG else F32) for n in kinds}

    def rows(a, lead=0):
        return a.reshape(a.shape[:lead] + (-1, a.shape[-1]))

    def model_layer(n, j):
        if vals[n].shape[0] == depth:
            return j
        return 2 * j if n.startswith(("ab_", "rwkv_")) else 2 * j + 1

    units = [(n, j) for n in kinds for j in range(vals[n].shape[0])]
    early_names = ("ffn1_w_gate", "ffn1_w_up", "ffn1_w_down", "ab_w_in", "rwkv_w_up", "rwkv_a_up", "rwkv_g_up")
    early = [u for u in units if model_layer(*u) == 0 and u[0] in early_names]
    late = [u for u in units if u not in early]
    full = {}

    def shard(unit):
        n, j = unit
        return rows(vals[n][j]).astype(travel[n])

    def take_gathered(which, got):
        for (n, j), g in zip(which, got):
            full[(n, j)] = _to_natural(g.reshape((N_DEV,) + vals[n].shape[1:]), kinds[n])

    take_gathered(early, gather_all([shard(u) for u in early]).run("gather_early"))

    def late_gather_done(got):
        take_gathered(late, got)
        return full[("ab_w_out", 0)]

    def row(name, j):
        return vals[name][j].reshape(1, -1)

    def ffn_weights(which, i):
        return (row(f"{which}_norm", i), full[(f"{which}_w_gate", i)], full[(f"{which}_w_up", i)],
                full[(f"{which}_w_down", i)])

    def mixer_weights(i):
        j = i // 2
        if i % 2 == 0:
            w = {n: row(n, j) for n in ("att_q_gain", "att_k_gain", "rwkv_mu", "rwkv_w0", "rwkv_a0", "rwkv_k_k", "rwkv_k_a",
                                        "rwkv_r_k", "rwkv_lnx_w", "rwkv_lnx_b")}
            w.update({n: full.get((n, j)) for n in ("ab_w_in", "ab_w_out", "rwkv_w_up", "rwkv_a_up", "rwkv_g_up")})
            w["att_rel_bias"] = att_rel_bias[j]
        else:
            w = {"ssm_lambda_re": ssm_lambda_re[j], "ssm_lambda_im": ssm_lambda_im[j],
                 "ssm_log_dt": ssm_log_dt[j].reshape(n_groups, 1),
                 "ssm_b_re": ssm_b_re[j].reshape(n_groups * n_state, -1),
                 "ssm_b_im": ssm_b_im[j].reshape(n_groups * n_state, -1),
                 "ssm_c_re": ssm_c_re[j], "ssm_c_im": ssm_c_im[j], "ssm_d": full[("ssm_d", j)].reshape(1, -1),
                 "ssm_w_in": full[("ssm_w_in", j)], "ssm_w_out": full[("ssm_w_out", j)]}
        w["mix_norm"] = row("mix_norm", i)
        return w

    def ple_weights(i):
        return (row("ple_norm", i), full[("ple_w_gate", i)], full[("ple_w_proj", i)], p[i, 0])

    h = x[0]
    saved = []
    for i in range(depth):
        h, s1 = _ffn_fwd(f"l{i}_ffn1", h, *ffn_weights("ffn1", i))
        if i == 0:
            h, sm = _ab_fwd(h, mixer_weights(i), gather_all([shard(u) for u in late]), late_gather_done)
        else:
            h, sm = (_ab_fwd if i % 2 == 0 else _s5_fwd)(h, mixer_weights(i))
        h, s2 = _ffn_fwd(f"l{i}_ffn2", h, *ffn_weights("ffn2", i))
        h, sp = _ple_fwd(f"l{i}_ple", h, *ple_weights(i))
        saved.append((s1, sm, s2, sp))
    dh, loss_part = loss_head(h, loss_target[0])

    per_layer = {n: [] for n in WEIGHTS}
    received = {}

    def chip_sums(which, tag):
        pieces = [rows(_to_stacked(per_layer[n][j - vals[n].shape[0]], kinds[n]).astype(travel[n]), lead=1)
                  for n, j in which]
        from_sibling = pair_exchange(pieces).run(f"reduce_pair_{tag}")
        return [pair_sum(f"pair_sum_{n}{j}", a, b) for (n, j), a, b in zip(which, pieces, from_sibling)]

    def late_reduce(d_ab_w_out):
        per_layer["ab_w_out"].insert(0, d_ab_w_out)
        return chip_exchange(chip_sums(late, "late"))

    def late_reduce_done(got):
        received.update(zip(late, got))

    for i in reversed(range(depth)):
        s1, sm, s2, sp = saved[i]
        dh, d_norm, d_gate, d_proj = _ple_bwd(f"l{i}_ple", dh, sp, *ple_weights(i))
        for n, g in (("ple_norm", d_norm), ("ple_w_gate", d_gate), ("ple_w_proj", d_proj)):
            per_layer[n].insert(0, g)
        dh, d_norm, d_gate, d_up, d_down = _ffn_bwd(f"l{i}_ffn2", dh, s2, *ffn_weights("ffn2", i))
        for n, g in (("ffn2_norm", d_norm), ("ffn2_w_gate", d_gate), ("ffn2_w_up", d_up), ("ffn2_w_down", d_down)):
            per_layer[n].insert(0, g)
        if i == 0:
            dh, mixer_grads = _ab_bwd(dh, sm, mixer_weights(i), late_reduce, late_reduce_done)
        else:
            dh, mixer_grads = (_ab_bwd if i % 2 == 0 else _s5_bwd)(dh, sm, mixer_weights(i))
        for n, g in mixer_grads.items():
            if not (i == 0 and n == "ab_w_out"):
                per_layer[n].insert(0, g)
        dh, d_norm, d_gate, d_up, d_down = _ffn_bwd(f"l{i}_ffn1", dh, s1, *ffn_weights("ffn1", i))
        for n, g in (("ffn1_norm", d_norm), ("ffn1_w_gate", d_gate), ("ffn1_w_up", d_up), ("ffn1_w_down", d_down)):
            per_layer[n].insert(0, g)
    grad_x = dh[None]

    received.update(zip(early, chip_exchange(chip_sums(early, "early")).run("reduce_chips_early")))

    rep_mine = _pack([jnp.stack(per_layer[n], axis=0).reshape(vals[n].shape) for n in REPLICATED] + [loss_part], F32)
    rep_all = gather_all([rep_mine]).run("gather_replicated")[0]

    out = {}
    for n in kinds:
        got = [received[(n, j)] for j in range(vals[n].shape[0])]
        got = got[0] if len(got) == 1 else jnp.concatenate(got, axis=1)
        state = [rows(vals[pre + n]) for pre in ("", "m_", "v_")]
        out[n] = tuple(r.reshape(vals[n].shape) for r in adam_update(f"adam_{n}", got, *state))
    zero = jnp.zeros((1, 1), F32)
    state = [_pack([vals[pre + n] for n in REPLICATED] + [zero], F32) for pre in ("", "m_", "v_")]
    shapes = [vals[n].shape for n in REPLICATED] + [zero.shape]
    results = [_unpack(r, shapes) for r in adam_update("adam_replicated", rep_all, *state)]
    for q, n in enumerate(REPLICATED):
        out[n] = tuple(r[q] for r in results)
    loss = results[0][-1].reshape(())
    return (loss, grad_x, *[out[n][0] for n in WEIGHTS], *[out[n][1] for n in WEIGHTS], *[out[n][2] for n in WEIGHTS],
            *[out[n][3] for n in WEIGHTS])
```

```python
import functools
import math

import jax
import jax.numpy as jnp
import numpy as np
from jax import lax
from jax.experimental import pallas as pl
from jax.experimental.pallas import tpu as pltpu

F32 = jnp.float32
BF16 = jnp.bfloat16
HIGHEST = lax.Precision.HIGHEST
MESH_AXES = ("x", "y", "c")
N_DEV = 8

CHUNK = 64
N_LEFT_CHUNKS = 8
BAND = (N_LEFT_CHUNKS + 1) * CHUNK
HEAD_DIM = 64
REL_CLIP = 128
N_REL = (CHUNK - 1) + REL_CLIP + 1
DECAY_LORA = 64
AAA_LORA = 64
GATE_LORA = 128
SSM_GROUP = 16
SSM_STATE = 64
RMS_EPS = 1e-6
GN_EPS = 64e-5
ADAM_LR = 0.001
ADAM_B1 = 0.9
ADAM_B2 = 0.999
ADAM_EPS = 1e-08
ADAM_WD = 0.01
ADAM_STEP = 10

RWKV_CHUNK = 64
VMEM_LIMIT = 56 * 1024 * 1024
LANES = 128


def _params(semantics):
    return pltpu.CompilerParams(dimension_semantics=semantics, vmem_limit_bytes=VMEM_LIMIT)


def _pick(n, prefs):
    for t in prefs:
        if n % t == 0:
            return t
    return n


def _dot(a, b, dims):
    return lax.dot_general(a, b, (dims, ((), ())), precision=HIGHEST, preferred_element_type=F32)


def _mm(a, b):
    return _dot(a, b, ((1,), (0,)))


def _mm_nt(a, b):
    return _dot(a, b, ((1,), (1,)))


def _mm_tn(a, b):
    return _dot(a, b, ((0,), (0,)))


def _split2(x):
    hi = x.astype(BF16)
    return hi, (x - hi.astype(F32)).astype(BF16)


def _dot3_raw(a, b, dims):
    a_hi, a_lo = _split2(a)
    b_hi, b_lo = _split2(b)
    dot = lambda p, q: lax.dot_general(p, q, (dims, ((), ())), preferred_element_type=F32)
    return dot(a_hi, b_hi) + (dot(a_hi, b_lo) + dot(a_lo, b_hi))


def _dot1_raw(a, b, dims):
    return lax.dot_general(a.astype(BF16), b.astype(BF16), (dims, ((), ())), preferred_element_type=F32)


NN, NT, TN = ((1,), (0,)), ((1,), (1,)), ((0,), (0,))


def _make_dot(raw):
    @functools.partial(jax.custom_vjp, nondiff_argnums=(2,))
    def dot(a, b, dims):
        return raw(a, b, dims)

    def fwd(a, b, dims):
        return raw(a, b, dims), (a, b)

    def bwd(dims, saved, g):
        a, b = saved
        if dims == NN:
            return raw(g, b, NT), raw(a, g, TN)
        if dims == NT:
            return raw(g, b, NN), raw(g, a, TN)
        return raw(b, g, NT), raw(a, g, NN)

    dot.defvjp(fwd, bwd)
    return dot


_dot3 = _make_dot(_dot3_raw)
_dot1 = _make_dot(_dot1_raw)


def _dot_ind_raw(x, ind, dims):
    hi = x.astype(BF16)
    rest = x - hi.astype(F32)
    mid = rest.astype(BF16)
    lo = (rest - mid.astype(F32)).astype(BF16)
    ind = ind.astype(BF16)
    dot = lambda p: lax.dot_general(p, ind, (dims, ((), ())), preferred_element_type=F32)
    return dot(hi) + (dot(mid) + dot(lo))


@jax.custom_vjp
def _mm_ind(x, ind):
    return _dot_ind_raw(x, ind, NN)


def _mm_ind_fwd(x, ind):
    return _dot_ind_raw(x, ind, NN), ind


def _mm_ind_bwd(ind, g):
    return _dot_ind_raw(g, ind, NT), jnp.zeros_like(ind)


_mm_ind.defvjp(_mm_ind_fwd, _mm_ind_bwd)


def _sigmoid(x):
    return 1.0 / (1.0 + jnp.exp(-x))


def _softplus(x):
    return jnp.maximum(x, 0.0) + jnp.log(1.0 + jnp.exp(-jnp.abs(x)))


def _gelu_tanh(x):
    return 0.5 * x * (1.0 + jnp.tanh(math.sqrt(2.0 / math.pi) * (x + 0.044715 * (x * x * x))))


def _seg_indicator(n, seg):
    r = lax.broadcasted_iota(jnp.int32, (n, n // seg), 0)
    c = lax.broadcasted_iota(jnp.int32, (n, n // seg), 1)
    return jnp.where((r >= c * seg) & (r < (c + 1) * seg), 1.0, 0.0).astype(F32)


def _seg_indicator_t(n, seg):
    c = lax.broadcasted_iota(jnp.int32, (n // seg, n), 0)
    r = lax.broadcasted_iota(jnp.int32, (n // seg, n), 1)
    return jnp.where((r >= c * seg) & (r < (c + 1) * seg), 1.0, 0.0).astype(F32)


def _seg_sum(x, seg):
    return _mm_ind(x, _seg_indicator(x.shape[1], seg))


def _seg_expand(s, seg):
    return _mm_ind(s, _seg_indicator_t(s.shape[1] * seg, seg))


def _tile_lanes(g, reps):
    n = g.shape[1]
    r = lax.broadcasted_iota(jnp.int32, (n, n * reps), 0)
    c = lax.broadcasted_iota(jnp.int32, (n, n * reps), 1)
    return _mm_ind(g, jnp.where((c & (n - 1)) == r, 1.0, 0.0).astype(F32))


def _rms(x, g):
    return x * lax.rsqrt(jnp.mean(x * x, axis=-1, keepdims=True) + RMS_EPS) * g


MATMUL_VMEM_BUDGET = 40 * 1024 * 1024
MATMUL_MAX_TILE = 2048
HBM_BYTES_PER_S = 1.2e12
MXU_FLOPS_PER_S = 8e14
GRID_STEP_S = 0.35e-6


def _tile_candidates(dim):
    c = [d for d in range(LANES, min(dim, MATMUL_MAX_TILE) + 1, LANES) if dim % d == 0]
    return c or [dim]


def _matmul_tiles(m, n, k, a_bytes, b_bytes, out_bytes, res_bytes):
    best = None
    for tm in _tile_candidates(m):
        for tn in _tile_candidates(n):
            for tk in _tile_candidates(k):
                casts = (tm * tk * 2 if a_bytes > 2 else 0) + (tk * tn * 2 if b_bytes > 2 else 0)
                vmem = (2 * (tm * tk * a_bytes + tk * tn * b_bytes + tm * tn * (out_bytes + res_bytes))
                        + 2 * tm * tn * 4 + casts)
                if vmem > MATMUL_VMEM_BUDGET:
                    continue
                steps = (m // tm) * (n // tn) * (k // tk)
                a_reads = 1 if k == tk else n // tn
                traffic = (m * k * a_bytes * a_reads + k * n * b_bytes * (m // tm) + m * n * (out_bytes + res_bytes))
                dma_s = traffic / HBM_BYTES_PER_S
                ends_s = (tm * tk * a_bytes + tk * tn * b_bytes + tm * tn * out_bytes) / HBM_BYTES_PER_S
                cost = max(dma_s, 2.0 * m * n * k / MXU_FLOPS_PER_S) + 0.2 * dma_s + steps * GRID_STEP_S + ends_s
                if best is None or cost < best[0]:
                    best = (cost, tm, tn, tk)
    return best[1:]


def matmul(name, a, b, *, ta=False, tb=False, alpha=1.0, res=None, out_dtype=F32, hosted=None):
    m, k = (a.shape[1], a.shape[0]) if ta else a.shape
    n = b.shape[0] if tb else b.shape[1]
    assert k == (b.shape[1] if tb else b.shape[0]), (name, a.shape, b.shape)
    tm, tn, tk = _matmul_tiles(m, n, k, a.dtype.itemsize, b.dtype.itemsize, jnp.dtype(out_dtype).itemsize,
                               0 if res is None else res.dtype.itemsize)
    nk = k // tk
    grid = (m // tm, n // tn, nk)
    dims = ((0 if ta else 1,), (1 if tb else 0,))
    h_arrays, h_specs, h_shapes, h_scratch = _hosted_call_args(hosted)
    nhosted = len(h_arrays)
    n_in = 2 + (res is not None)

    def body(*refs):
        a_ref, b_ref = refs[:2]
        res_ref = refs[2] if res is not None else None
        h_in, o_ref, h_out = refs[n_in:n_in + nhosted], refs[n_in + nhosted], refs[n_in + nhosted + 1:n_in + 2 * nhosted + 1]
        acc_ref, sems = refs[n_in + 2 * nhosted + 1], refs[n_in + 2 * nhosted + 2:]
        _hosted_steps(hosted, h_in, h_out, sems, grid, 0.6)
        kk = pl.program_id(2)

        @pl.when(kk == 0)
        def _():
            acc_ref[...] = jnp.zeros_like(acc_ref)

        acc_ref[...] += lax.dot_general(a_ref[...].astype(BF16), b_ref[...].astype(BF16), (dims, ((), ())),
                                        preferred_element_type=F32)

        @pl.when(kk == nk - 1)
        def _():
            out = acc_ref[...] * alpha
            if res_ref is not None:
                out = out + res_ref[...].astype(F32)
            o_ref[...] = out.astype(o_ref.dtype)

    a_spec = pl.BlockSpec((tk, tm), lambda i, j, kk: (kk, i)) if ta else pl.BlockSpec((tm, tk), lambda i, j, kk: (i, kk))
    b_spec = pl.BlockSpec((tn, tk), lambda i, j, kk: (j, kk)) if tb else pl.BlockSpec((tk, tn), lambda i, j, kk: (kk, j))
    o_spec = pl.BlockSpec((tm, tn), lambda i, j, kk: (i, j))
    in_specs = [a_spec, b_spec] + ([o_spec] if res is not None else [])
    args = (a, b) + ((res,) if res is not None else ())
    out_shape = jax.ShapeDtypeStruct((m, n), out_dtype)
    if hosted is None:
        return pl.pallas_call(
            body, name=name, grid=grid, in_specs=in_specs, out_specs=o_spec, out_shape=out_shape,
            scratch_shapes=[pltpu.VMEM((tm, tn), F32)], compiler_params=_params(("parallel", "parallel", "arbitrary")),
        )(*args)
    out, *travelled = pl.pallas_call(
        body, name=name, grid=grid, in_specs=in_specs + h_specs, out_specs=[o_spec] + h_specs,
        out_shape=[out_shape] + h_shapes, scratch_shapes=[pltpu.VMEM((tm, tn), F32)] + h_scratch,
        compiler_params=_params(("arbitrary", "arbitrary", "arbitrary")),
    )(*args, *h_arrays)
    return out, travelled


def _row_specs(arrays, tr):
    return [pl.BlockSpec((tr, a.shape[1]), lambda i: (i, 0)) for a in arrays]


def _whole_specs(arrays):
    return [pl.BlockSpec(a.shape, lambda i: (0, 0)) for a in arrays]


def rowwise(name, fn, rows, consts, outs, tr):
    t = rows[0].shape[0]
    nr, nc = len(rows), len(consts)

    def body(*refs):
        vals = [r[...] for r in refs[:nr + nc]]
        res = fn(*vals)
        for o_ref, o in zip(refs[nr + nc:], res):
            o_ref[...] = o.astype(o_ref.dtype)

    out_shape = [jax.ShapeDtypeStruct((t, w), d) for w, d in outs]
    return pl.pallas_call(
        body, name=name, grid=(t // tr,), in_specs=_row_specs(rows, tr) + _whole_specs(consts),
        out_specs=_row_specs(out_shape, tr), out_shape=out_shape, compiler_params=_params(("parallel",)),
    )(*rows, *consts)


def rowwise_vjp(name, fn, rows, consts, cots, tr, row_grad, const_grad, add_to=None, grad_dtypes=None):
    t = rows[0].shape[0]
    nr, nc = len(rows), len(consts)
    cot_groups = [c if isinstance(c, (tuple, list)) else (c,) for c in cots]
    flat_cots = [a for g in cot_groups for a in g]
    add_to = {i: (a if isinstance(a, (tuple, list)) else (a,)) for i, a in (add_to or {}).items()}
    add_idx = [(i, q) for i in sorted(add_to) for q in range(len(add_to[i]))]
    add_arrays = [add_to[i][q] for i, q in add_idx]
    r_idx = [i for i in range(nr) if row_grad[i]]
    c_idx = [i for i in range(nc) if const_grad[i]]
    grad_dtypes = grad_dtypes or {}
    n_in = nr + nc + len(flat_cots) + len(add_arrays)

    def body(*refs):
        vals = [r[...] for r in refs[:nr + nc]]
        pos = nr + nc
        cts = []
        for g in cot_groups:
            s = refs[pos][...].astype(F32)
            for q in range(1, len(g)):
                s = s + refs[pos + q][...].astype(F32)
            cts.append(s)
            pos += len(g)
        adds = {}
        for n_add, (i, _) in enumerate(add_idx):
            term = refs[pos + n_add][...].astype(F32)
            adds[i] = adds[i] + term if i in adds else term
        diff =[vals[i] for i in r_idx] + [vals[nr + i] for i in c_idx]

        def f(*d):
            full = list(vals)
            for q, i in enumerate(r_idx):
                full[i] = d[q]
            for q, i in enumerate(c_idx):
                full[nr + i] = d[len(r_idx) + q]
            return tuple(fn(*full))

        prim, vjp = jax.vjp(f, *diff)
        grads = vjp(tuple(c.astype(p.dtype) for c, p in zip(cts, prim)))
        o_refs = refs[n_in:]
        for q, i in enumerate(r_idx):
            g = grads[q].astype(F32)
            if i in adds:
                g = g + adds[i].astype(F32)
            o_refs[q][...] = g.astype(o_refs[q].dtype)
        step = pl.program_id(0)
        for q, i in enumerate(c_idx):
            o_ref = o_refs[len(r_idx) + q]

            @pl.when(step == 0)
            def _(o_ref=o_ref):
                o_ref[...] = jnp.zeros_like(o_ref)

            o_ref[...] += grads[len(r_idx) + q].astype(F32)

    row_out = [jax.ShapeDtypeStruct(rows[i].shape, grad_dtypes.get(i, F32)) for i in r_idx]
    const_out = [jax.ShapeDtypeStruct(consts[i].shape, F32) for i in c_idx]
    outs = pl.pallas_call(
        body, name=name, grid=(t // tr,),
        in_specs=_row_specs(rows, tr) + _whole_specs(consts) + _row_specs(flat_cots, tr) + _row_specs(add_arrays, tr),
        out_specs=_row_specs(row_out, tr) + _whole_specs(const_out), out_shape=row_out + const_out,
        compiler_params=_params(("arbitrary",)),
    )(*rows, *consts, *flat_cots, *add_arrays)
    return list(outs)


def loss_head(y, target):
    t, d = y.shape
    tr = _pick(t, (256, 128, 64, 32, 16, 8))

    def body(y_ref, t_ref, dy_ref, l_ref):
        diff = y_ref[...] - t_ref[...]
        dy_ref[...] = diff * (1.0 / d)

        @pl.when(pl.program_id(0) == 0)
        def _():
            l_ref[...] = jnp.zeros_like(l_ref)

        l_ref[...] += 0.5 * jnp.sum(jnp.mean(diff * diff, axis=-1, keepdims=True), axis=0, keepdims=True)

    dy, l = pl.pallas_call(
        body, name="loss_head", grid=(t // tr,), in_specs=_row_specs([y, target], tr),
        out_specs=[pl.BlockSpec((tr, d), lambda i: (i, 0)), pl.BlockSpec((1, 1), lambda i: (0, 0))],
        out_shape=[jax.ShapeDtypeStruct((t, d), F32), jax.ShapeDtypeStruct((1, 1), F32)],
        compiler_params=_params(("arbitrary",)),
    )(y, target)
    return dy, l


ATT_PAD = N_LEFT_CHUNKS * CHUNK
MASKED = -1e30
ATT_GROUP = 4


def _attn_chunks(q_c, k_b, v_b, bias, valid):
    s = [jnp.where(ok, _dot1(q, k, NT) + bias, MASKED) for q, k, ok in zip(q_c, k_b, valid)]
    e = [jnp.exp(x - lax.stop_gradient(jnp.max(x, axis=-1, keepdims=True))) for x in s]
    p = [x / jnp.sum(x, axis=-1, keepdims=True) for x in e]
    return tuple(_dot1(x, v, NN) for x, v in zip(p, v_b))


def _band_valid(c):
    return (c * CHUNK + lax.broadcasted_iota(jnp.int32, (1, BAND), 1)) >= ATT_PAD


def _head_spec(t):
    return pl.BlockSpec((None, t, HEAD_DIM), lambda h: (h, 0, 0))


def attention_fwd(q, k, v, bias, hosted=None):
    nh, t, _ = q.shape
    nchunks = t // CHUNK
    group = _pick(nchunks, (ATT_GROUP, 2, 1))
    h_arrays, h_specs, h_shapes, h_scratch = _hosted_call_args(hosted)
    nhosted = len(h_arrays)

    def body(q_ref, k_ref, v_ref, b_ref, *rest):
        h_in, o_ref, h_out = rest[:nhosted], rest[nhosted], rest[nhosted + 1:2 * nhosted + 1]
        kp, vp, sems = rest[2 * nhosted + 1], rest[2 * nhosted + 2], rest[2 * nhosted + 3:]
        _hosted_steps(hosted, h_in, h_out, sems, (nh,), 0.6)
        zeros = jnp.zeros((ATT_PAD, HEAD_DIM), F32)
        kp[pl.ds(0, ATT_PAD), :] = zeros
        vp[pl.ds(0, ATT_PAD), :] = zeros
        kp[pl.ds(ATT_PAD, t), :] = k_ref[...]
        vp[pl.ds(ATT_PAD, t), :] = v_ref[...]
        bias_h = b_ref[...]

        def step(g, carry):
            cs = [g * group + u for u in range(group)]
            q0 = [pl.multiple_of(c * CHUNK, CHUNK) for c in cs]
            outs = _attn_chunks(tuple(q_ref[pl.ds(s, CHUNK), :] for s in q0), tuple(kp[pl.ds(s, BAND), :] for s in q0),
                                tuple(vp[pl.ds(s, BAND), :] for s in q0), bias_h, tuple(_band_valid(c) for c in cs))
            for s, o in zip(q0, outs):
                o_ref[pl.ds(s, CHUNK), :] = o
            return carry

        lax.fori_loop(0, nchunks // group, step, 0)

    out, *travelled = pl.pallas_call(
        body, name="attention_fwd", grid=(nh,),
        in_specs=[_head_spec(t)] * 3 + [pl.BlockSpec((None, CHUNK, BAND), lambda h: (h, 0, 0))] + h_specs,
        out_specs=[_head_spec(t)] + h_specs, out_shape=[jax.ShapeDtypeStruct((nh, t, HEAD_DIM), F32)] + h_shapes,
        scratch_shapes=[pltpu.VMEM((t + ATT_PAD, HEAD_DIM), F32)] * 2 + h_scratch,
        compiler_params=_params(("arbitrary",)),
    )(q, k, v, bias, *h_arrays)
    return out if hosted is None else (out, travelled)


def attention_bwd(q, k, v, bias, dout):
    nh, t, _ = q.shape
    nchunks = t // CHUNK
    group = _pick(nchunks, (ATT_GROUP, 2, 1))

    def body(q_ref, k_ref, v_ref, b_ref, do_ref, dq_ref, dk_ref, dv_ref, db_ref, kp, vp, dkp, dvp):
        zeros = jnp.zeros((ATT_PAD, HEAD_DIM), F32)
        kp[pl.ds(0, ATT_PAD), :] = zeros
        vp[pl.ds(0, ATT_PAD), :] = zeros
        kp[pl.ds(ATT_PAD, t), :] = k_ref[...]
        vp[pl.ds(ATT_PAD, t), :] = v_ref[...]
        dkp[...] = jnp.zeros_like(dkp)
        dvp[...] = jnp.zeros_like(dvp)
        db_ref[...] = jnp.zeros_like(db_ref)
        bias_h = b_ref[...]

        def step(g, carry):
            cs = [g * group + u for u in range(group)]
            q0 = [pl.multiple_of(c * CHUNK, CHUNK) for c in cs]
            valid = tuple(_band_valid(c) for c in cs)
            _, vjp = jax.vjp(lambda a, b, cc, d: _attn_chunks(a, b, cc, d, valid),
                             tuple(q_ref[pl.ds(s, CHUNK), :] for s in q0), tuple(kp[pl.ds(s, BAND), :] for s in q0),
                             tuple(vp[pl.ds(s, BAND), :] for s in q0), bias_h)
            dq, dk, dv, db = vjp(tuple(do_ref[pl.ds(s, CHUNK), :] for s in q0))
            for u, s in enumerate(q0):
                dq_ref[pl.ds(s, CHUNK), :] = dq[u]
                dkp[pl.ds(s, BAND), :] += dk[u]
                dvp[pl.ds(s, BAND), :] += dv[u]
            db_ref[...] += db
            return carry

        lax.fori_loop(0, nchunks // group, step, 0)
        dk_ref[...] = dkp[pl.ds(ATT_PAD, t), :]
        dv_ref[...] = dvp[pl.ds(ATT_PAD, t), :]

    bias_spec = pl.BlockSpec((None, CHUNK, BAND), lambda h: (h, 0, 0))
    hs = jax.ShapeDtypeStruct((nh, t, HEAD_DIM), F32)
    return pl.pallas_call(
        body, name="attention_bwd", grid=(nh,), in_specs=[_head_spec(t)] * 3 + [bias_spec, _head_spec(t)],
        out_specs=[_head_spec(t)] * 3 + [bias_spec],
        out_shape=[hs, hs, hs, jax.ShapeDtypeStruct(bias.shape, F32)],
        scratch_shapes=[pltpu.VMEM((t + ATT_PAD, HEAD_DIM), F32)] * 4, compiler_params=_params(("parallel",)),
    )(q, k, v, bias, dout)


def _rel_onehot_t(i):
    j = lax.broadcasted_iota(jnp.int32, (N_REL, BAND), 1)
    r = lax.broadcasted_iota(jnp.int32, (N_REL, BAND), 0)
    idx = jnp.clip(i + ATT_PAD - j, -(CHUNK - 1), REL_CLIP) + (CHUNK - 1)
    return jnp.where(r == idx, 1.0, 0.0).astype(F32)


def bias_expand(rel):
    nh = rel.shape[0]

    def body(rel_ref, o_ref):
        o_ref[...] = _mm(rel_ref[...], _rel_onehot_t(pl.program_id(0)))

    return pl.pallas_call(
        body, name="bias_expand", grid=(CHUNK,), in_specs=[pl.BlockSpec((nh, N_REL), lambda i: (0, 0))],
        out_specs=pl.BlockSpec((None, nh, BAND), lambda i: (i, 0, 0)),
        out_shape=jax.ShapeDtypeStruct((CHUNK, nh, BAND), F32), compiler_params=_params(("parallel",)),
    )(rel)


def bias_reduce(dbias):
    nh = dbias.shape[1]

    def body(d_ref, o_ref):
        @pl.when(pl.program_id(0) == 0)
        def _():
            o_ref[...] = jnp.zeros_like(o_ref)

        o_ref[...] += _mm_nt(d_ref[...], _rel_onehot_t(pl.program_id(0)))

    return pl.pallas_call(
        body, name="bias_reduce", grid=(CHUNK,), in_specs=[pl.BlockSpec((None, nh, BAND), lambda i: (i, 0, 0))],
        out_specs=pl.BlockSpec((nh, N_REL), lambda i: (0, 0)), out_shape=jax.ShapeDtypeStruct((nh, N_REL), F32),
        compiler_params=_params(("arbitrary",)),
    )(dbias)


def _tri(n, strict):
    r = lax.broadcasted_iota(jnp.int32, (n, n), 0)
    c = lax.broadcasted_iota(jnp.int32, (n, n), 1)
    return (c < r) if strict else (c <= r)


def _each(f, *lists):
    return [f(*args) for args in zip(*lists)]


def _rwkv_chunk(s, r, k, v, w, a, b):
    n = r[0].shape[0]
    strict, incl = _tri(n, True), _tri(n, False)
    ones = jnp.where(incl, 1.0, 0.0).astype(F32)
    eye = jnp.where(incl & ~strict, 1.0, 0.0).astype(F32)
    lw = _each(jnp.log, w)
    cum = _each(lambda x: _mm(ones, x), lw)
    p_incl = _each(jnp.exp, cum)
    p_inv = _each(lambda x: jnp.exp(-x), cum)
    a_t = _each(lambda x, c, l: x * jnp.exp(c - l), a, cum, lw)
    r_t = _each(jnp.multiply, r, p_incl)
    b_t = _each(jnp.multiply, b, p_inv)
    k_t = _each(jnp.multiply, k, p_inv)
    a_ab = _each(lambda x, y: jnp.where(strict, _dot3(x, y, NT), 0.0), a_t, b_t)
    a_ak = _each(lambda x, y: jnp.where(strict, _dot3(x, y, NT), 0.0), a_t, k_t)
    r_b = _each(lambda x, y: jnp.where(incl, _dot3(x, y, NT), 0.0), r_t, b_t)
    r_k = _each(lambda x, y: jnp.where(incl, _dot3(x, y, NT), 0.0), r_t, k_t)
    a_s = _each(lambda x, y: _dot3(x, y, NT), a_t, s)
    r_s = _each(lambda x, y: _dot3(x, y, NT), r_t, s)
    a_kv = _each(lambda x, y: _dot3(x, y, NN), a_ak, v)
    total = _each(lambda x: eye + x, a_ab)
    power = _each(lambda x: _dot3(x, x, NN), a_ab)
    rounds = int(math.log2(n)) - 1
    for i in range(rounds):
        total = _each(lambda t, p: t + _dot3(t, p, NN), total, power)
        if i < rounds - 1:
            power = _each(lambda p: _dot3(p, p, NN), power)
    sa = _each(lambda t, x, y: _dot3(t, x + y, NN), total, a_s, a_kv)
    y = _each(lambda rs, rb, x, rk, vv: rs + _dot3(rb, x, NN) + _dot3(rk, vv, NN), r_s, r_b, sa, r_k, v)
    s_new = _each(lambda ss, x, bt, vv, kt, p: (ss + _dot3(x, bt, TN) + _dot3(vv, kt, TN)) * p[n - 1:n, :],
                  s, sa, b_t, v, k_t, p_incl)
    return tuple(s_new), tuple(y)


RWKV_TILE = 512
RWKV_HEADS = 4


def _rwkv_specs(nh, t, reverse):
    tile = min(RWKV_TILE, t)
    hb = RWKV_HEADS if nh % RWKV_HEADS == 0 else 1
    nt = t // tile
    per = tile // RWKV_CHUNK
    pos = (lambda h, i: (h, nt - 1 - i, 0)) if reverse else (lambda h, i: (h, i, 0))
    pos4 = (lambda h, i: (h, nt - 1 - i, 0, 0)) if reverse else (lambda h, i: (h, i, 0, 0))
    return (hb, nt, per, pl.BlockSpec((hb, tile, HEAD_DIM), pos), pl.BlockSpec((hb, per, HEAD_DIM, HEAD_DIM), pos4))


def _hosted_steps(exchange, refs_in, refs_out, sems, grid, forward_share):
    if exchange is None:
        return
    start, forward, finish = exchange.plan(refs_in, refs_out, *sems)
    step, n_steps = 0, 1
    for axis, size in enumerate(grid):
        step = step * size + pl.program_id(axis)
        n_steps *= size
    pl.when(step == 0)(start)
    pl.when(step == int(forward_share * (n_steps - 1)))(forward)
    pl.when(step == n_steps - 1)(finish)


def _hosted_call_args(exchange):
    if exchange is None:
        return [], [], [], []
    nb = len(exchange.arrays)
    return exchange.arrays, [HBM_SPEC] * nb, exchange.out_shape, exchange.scratch()


def rwkv_fwd(r, k, v, w, a, b, hosted=None):
    nh, t, _ = r.shape
    hb, nt, per, row_spec, s_spec = _rwkv_specs(nh, t, False)
    h_arrays, h_specs, h_shapes, h_scratch = _hosted_call_args(hosted)
    nhosted = len(h_arrays)
    n_steps = (nh // hb) * nt

    def body(r_ref, k_ref, v_ref, w_ref, a_ref, b_ref, *rest):
        h_in, (y_ref, s_ref), h_out = rest[:nhosted], rest[nhosted:nhosted + 2], rest[nhosted + 2:2 * nhosted + 2]
        state, sems = rest[2 * nhosted + 2], rest[2 * nhosted + 3:]
        _hosted_steps(hosted, h_in, h_out, sems, (nh // hb, nt), 0.75)

        @pl.when(pl.program_id(1) == 0)
        def _():
            state[...] = jnp.zeros_like(state)

        def step(c, states):
            rows = pl.ds(pl.multiple_of(c * RWKV_CHUNK, RWKV_CHUNK), RWKV_CHUNK)
            for h in range(hb):
                s_ref[h, c] = states[h]
            s_new, y = _rwkv_chunk(states, *(tuple(ref[h, rows, :] for h in range(hb))
                                             for ref in (r_ref, k_ref, v_ref, w_ref, a_ref, b_ref)))
            for h in range(hb):
                y_ref[h, rows, :] = y[h]
            return s_new

        final = lax.fori_loop(0, per, step, tuple(state[h] for h in range(hb)))
        for h in range(hb):
            state[h] = final[h]

    return pl.pallas_call(
        body, name="rwkv_fwd", grid=(nh // hb, nt), in_specs=[row_spec] * 6 + h_specs,
        out_specs=[row_spec, s_spec] + h_specs,
        out_shape=[jax.ShapeDtypeStruct((nh, t, HEAD_DIM), F32),
                   jax.ShapeDtypeStruct((nh, t // RWKV_CHUNK, HEAD_DIM, HEAD_DIM), F32)] + h_shapes,
        scratch_shapes=[pltpu.VMEM((hb, HEAD_DIM, HEAD_DIM), F32)] + h_scratch,
        compiler_params=_params(("arbitrary", "arbitrary")),
    )(r, k, v, w, a, b, *h_arrays)


def rwkv_bwd(r, k, v, w, a, b, states, dy, hosted=None):
    nh, t, _ = r.shape
    hb, nt, per, row_spec, s_spec = _rwkv_specs(nh, t, True)
    h_arrays, h_specs, h_shapes, h_scratch = _hosted_call_args(hosted)
    nhosted = len(h_arrays)
    n_steps = (nh // hb) * nt

    def body(r_ref, k_ref, v_ref, w_ref, a_ref, b_ref, s_ref, dy_ref, *rest):
        h_in, d_refs, h_out = rest[:nhosted], rest[nhosted:nhosted + 6], rest[nhosted + 6:2 * nhosted + 6]
        dstate, sems = rest[2 * nhosted + 6], rest[2 * nhosted + 7:]
        _hosted_steps(hosted, h_in, h_out, sems, (nh // hb, nt), 0.5)

        @pl.when(pl.program_id(1) == 0)
        def _():
            dstate[...] = jnp.zeros_like(dstate)

        def step(i, ds):
            c = per - 1 - i
            rows = pl.ds(pl.multiple_of(c * RWKV_CHUNK, RWKV_CHUNK), RWKV_CHUNK)
            _, vjp = jax.vjp(_rwkv_chunk, tuple(s_ref[h, c] for h in range(hb)),
                             *(tuple(ref[h, rows, :] for h in range(hb))
                               for ref in (r_ref, k_ref, v_ref, w_ref, a_ref, b_ref)))
            grads = vjp((ds, tuple(dy_ref[h, rows, :] for h in range(hb))))
            for d_ref, g in zip(d_refs, grads[1:]):
                for h in range(hb):
                    d_ref[h, rows, :] = g[h]
            return grads[0]

        final = lax.fori_loop(0, per, step, tuple(dstate[h] for h in range(hb)))
        for h in range(hb):
            dstate[h] = final[h]

    hs = jax.ShapeDtypeStruct((nh, t, HEAD_DIM), F32)
    return pl.pallas_call(
        body, name="rwkv_bwd", grid=(nh // hb, nt), in_specs=[row_spec] * 6 + [s_spec, row_spec] + h_specs,
        out_specs=[row_spec] * 6 + h_specs, out_shape=[hs] * 6 + h_shapes,
        scratch_shapes=[pltpu.VMEM((hb, HEAD_DIM, HEAD_DIM), F32)] + h_scratch,
        compiler_params=_params(("arbitrary", "arbitrary")),
    )(r, k, v, w, a, b, states, dy, *h_arrays)


SUBLANES = 8


def s5_scan(bu_re, bu_im, a_re, a_im, *, reverse=False, h_prev=None):
    t, n = bu_re.shape
    tt = _pick(t, (512, 256, 128, 64, 32, 16, 8))
    tc = _pick(n, (1024, 512, 256, 128))
    nt = t // tt
    with_da = h_prev is not None
    sign = -1.0 if reverse else 1.0

    def body(*refs):
        if with_da:
            br, bi, ar_ref, ai_ref, pr, pi, hr_ref, hi_ref, dar_ref, dai_ref, sr, si = refs
        else:
            br, bi, ar_ref, ai_ref, hr_ref, hi_ref, sr, si = refs
        ti = pl.program_id(1)

        @pl.when(ti == 0)
        def _():
            sr[...] = jnp.zeros_like(sr)
            si[...] = jnp.zeros_like(si)
            if with_da:
                dar_ref[...] = jnp.zeros_like(dar_ref)
                dai_ref[...] = jnp.zeros_like(dai_ref)

        ar = ar_ref[...]
        ai = ai_ref[...] * sign

        def group(gi, carry):
            hr, hi, dar, dai = carry
            g0 = pl.multiple_of((tt // SUBLANES - 1 - gi if reverse else gi) * SUBLANES, SUBLANES)
            rows = pl.ds(g0, SUBLANES)
            xr, xi = br[rows, :], bi[rows, :]
            if with_da:
                qr, qi = pr[rows, :], pi[rows, :]
            out_r, out_i = [None] * SUBLANES, [None] * SUBLANES
            for s in (range(SUBLANES - 1, -1, -1) if reverse else range(SUBLANES)):
                hr, hi = ar * hr - ai * hi + xr[s:s + 1, :], ar * hi + ai * hr + xi[s:s + 1, :]
                out_r[s], out_i[s] = hr, hi
                if with_da:
                    dar = dar + hr * qr[s:s + 1, :] + hi * qi[s:s + 1, :]
                    dai = dai + hi * qr[s:s + 1, :] - hr * qi[s:s + 1, :]
            hr_ref[rows, :] = jnp.concatenate(out_r, axis=0)
            hi_ref[rows, :] = jnp.concatenate(out_i, axis=0)
            return hr, hi, dar, dai

        zero = jnp.zeros((1, tc), F32)
        hr, hi, dar, dai = lax.fori_loop(0, tt // SUBLANES, group, (sr[...], si[...], zero, zero))
        sr[...] = hr
        si[...] = hi
        if with_da:
            dar_ref[...] += dar
            dai_ref[...] += dai

    tile = pl.BlockSpec((tt, tc), (lambda ci, ti: (nt - 1 - ti, ci)) if reverse else (lambda ci, ti: (ti, ci)))
    col = pl.BlockSpec((1, tc), lambda ci, ti: (0, ci))
    hs = jax.ShapeDtypeStruct((t, n), F32)
    cs = jax.ShapeDtypeStruct((1, n), F32)
    ins = [bu_re, bu_im, a_re, a_im] + (list(h_prev) if with_da else [])
    return pl.pallas_call(
        body, name="s5_scan_bwd" if reverse else "s5_scan_fwd", grid=(n // tc, nt),
        in_specs=[tile, tile, col, col] + ([tile, tile] if with_da else []),
        out_specs=[tile, tile] + ([col, col] if with_da else []), out_shape=[hs, hs] + ([cs, cs] if with_da else []),
        scratch_shapes=[pltpu.VMEM((1, tc), F32)] * 2, compiler_params=_params(("parallel", "arbitrary")),
    )(*ins)


N_CHIPS = 4
HBM_SPEC = pl.BlockSpec(memory_space=pl.ANY)


def _remote(src, dst, send_sem, recv_sem, device):
    return pltpu.make_async_remote_copy(src_ref=src, dst_ref=dst, send_sem=send_sem, recv_sem=recv_sem,
                                        device_id=device, device_id_type=pl.DeviceIdType.MESH)


class Exchange:
    def __init__(self, arrays, out_shape, n_sems, plan):
        self.arrays, self.out_shape, self.n_sems, self.plan = list(arrays), out_shape, n_sems, plan

    def scratch(self):
        nb = len(self.arrays)
        return [pltpu.SemaphoreType.DMA((nb, self.n_sems)), pltpu.SemaphoreType.DMA((nb, self.n_sems)),
                pltpu.SemaphoreType.DMA((nb,))]

    def run(self, name):
        nb = len(self.arrays)

        def body(*refs):
            for step in self.plan(refs[:nb], refs[nb:2 * nb], *refs[2 * nb:]):
                step()

        return pl.pallas_call(
            body, name=name, in_specs=[HBM_SPEC] * nb, out_specs=[HBM_SPEC] * nb, out_shape=self.out_shape,
            scratch_shapes=self.scratch(), compiler_params=pltpu.CompilerParams(has_side_effects=True),
        )(*self.arrays)


def gather_all(arrays):
    nb = len(arrays)

    def plan(ins, outs, send_sems, recv_sems, local_sems):
        x, y, c = (lax.axis_index(n) for n in MESH_AXES)
        me, sibling = 4 * x + 2 * y + c, 4 * x + 2 * y + 1 - c
        chips = [(1 - x, y), (x, 1 - y), (1 - x, 1 - y)]

        def copy(b, k, slot, to, src=None):
            block = outs[b].at[slot]
            return _remote(block if src is None else src, block, send_sems.at[b, k], recv_sems.at[b, k], to)

        def local():
            return [pltpu.make_async_copy(ins[b], outs[b].at[me], local_sems.at[b]) for b in range(nb)]

        def first():
            return [cp for b in range(nb) for cp in
                    [copy(b, 0, me, (x, y, 1 - c), src=ins[b])]
                    + [copy(b, 1 + j, me, (px, py, c), src=ins[b]) for j, (px, py) in enumerate(chips)]]

        def passed():
            return [copy(b, 4 + j, 4 * px + 2 * py + c, (x, y, 1 - c)) for j, (px, py) in enumerate(chips) for b in range(nb)]

        def start():
            for cp in local() + first():
                cp.start()

        def forward():
            for j, (px, py) in enumerate(chips):
                for b in range(nb):
                    copy(b, 1 + j, 4 * px + 2 * py + c, (px, py, c)).wait_recv()
                    copy(b, 4 + j, 4 * px + 2 * py + c, (x, y, 1 - c)).start()

        def finish():
            for b in range(nb):
                copy(b, 0, sibling, (x, y, 1 - c)).wait_recv()
                for j, (px, py) in enumerate(chips):
                    copy(b, 4 + j, 4 * px + 2 * py + 1 - c, (x, y, 1 - c)).wait_recv()
            for cp in first() + passed():
                cp.wait_send()
            for cp in local():
                cp.wait()

        return start, forward, finish

    out_shape = [jax.ShapeDtypeStruct((N_DEV,) + tuple(a.shape), a.dtype) for a in arrays]
    return Exchange(arrays, out_shape, N_DEV - 1, plan)


def _nothing():
    pass


def pair_exchange(arrays):
    nb = len(arrays)

    def plan(ins, outs, send_sems, recv_sems, local_sems):
        x, y, c = (lax.axis_index(n) for n in MESH_AXES)

        def copies():
            return [_remote(ins[b].at[2 * z + 1 - c], outs[b].at[z], send_sems.at[b, z], recv_sems.at[b, z], (x, y, 1 - c))
                    for b in range(nb) for z in range(N_CHIPS)]

        def start():
            for cp in copies():
                cp.start()

        def finish():
            for cp in copies():
                cp.wait_send()
            for cp in copies():
                cp.wait_recv()

        return start, _nothing, finish

    out_shape = [jax.ShapeDtypeStruct((N_CHIPS,) + tuple(a.shape[1:]), a.dtype) for a in arrays]
    return Exchange(arrays, out_shape, N_CHIPS, plan)


def pair_sum(name, pieces, from_sibling):
    _, r, c = pieces.shape
    tr = _row_tile(r, c * pieces.dtype.itemsize, 1 << 20)

    def body(p_ref, s_ref, o_ref):
        mine = p_ref[lax.axis_index("c")]
        o_ref[...] = (mine.astype(F32) + s_ref[...].astype(F32)).astype(o_ref.dtype)

    return pl.pallas_call(
        body, name=name, grid=(N_CHIPS, r // tr),
        in_specs=[pl.BlockSpec((None, 2, tr, c), lambda z, i: (z, 0, i, 0)), pl.BlockSpec((None, tr, c), lambda z, i: (z, i, 0))],
        out_specs=pl.BlockSpec((None, tr, c), lambda z, i: (z, i, 0)),
        out_shape=jax.ShapeDtypeStruct((N_CHIPS, r, c), pieces.dtype), compiler_params=_params(("parallel", "parallel")),
    )(pieces.reshape(N_CHIPS, 2, r, c), from_sibling)


def chip_exchange(arrays):
    nb = len(arrays)

    def plan(ins, outs, send_sems, recv_sems, local_sems):
        x, y, c = (lax.axis_index(n) for n in MESH_AXES)
        my_chip = 2 * x + y
        chips = [(1 - x, y), (x, 1 - y), (1 - x, 1 - y)]

        def local():
            return [pltpu.make_async_copy(ins[b].at[my_chip], outs[b].at[my_chip], local_sems.at[b]) for b in range(nb)]

        def copies():
            return [_remote(ins[b].at[2 * px + py], outs[b].at[my_chip], send_sems.at[b, j], recv_sems.at[b, j], (px, py, c))
                    for b in range(nb) for j, (px, py) in enumerate(chips)]

        def start():
            for cp in local() + copies():
                cp.start()

        def finish():
            for cp in copies():
                cp.wait_send()
            for cp in copies():
                cp.wait_recv()
            for cp in local():
                cp.wait()

        return start, _nothing, finish

    out_shape = [jax.ShapeDtypeStruct(a.shape, a.dtype) for a in arrays]
    return Exchange(arrays, out_shape, N_CHIPS - 1, plan)


def adam_update(name, pieces, w, m, v):
    r, c = w.shape
    n_pieces = pieces.shape[0]
    tr = _row_tile(r, 4 * c, 1 << 19)

    def body(p_ref, w_ref, m_ref, v_ref, g_ref, d_ref, mo_ref, vo_ref):
        g = p_ref[0].astype(F32)
        for j in range(1, n_pieces):
            g = g + p_ref[j].astype(F32)
        m_new = ADAM_B1 * m_ref[...] + (1.0 - ADAM_B1) * g
        v_new = ADAM_B2 * v_ref[...] + (1.0 - ADAM_B2) * (g * g)
        m_hat = m_new / (1.0 - ADAM_B1 ** ADAM_STEP)
        v_hat = v_new / (1.0 - ADAM_B2 ** ADAM_STEP)
        g_ref[...] = g
        d_ref[...] = -ADAM_LR * (m_hat / (jnp.sqrt(v_hat) + ADAM_EPS) + ADAM_WD * w_ref[...])
        mo_ref[...] = m_new
        vo_ref[...] = v_new

    row = pl.BlockSpec((tr, c), lambda i: (i, 0))
    out = jax.ShapeDtypeStruct((r, c), F32)
    return pl.pallas_call(
        body, name=name, grid=(r // tr,), in_specs=[pl.BlockSpec((n_pieces, tr, c), lambda i: (0, i, 0)), row, row, row],
        out_specs=[row] * 4, out_shape=[out] * 4, compiler_params=_params(("parallel",)),
    )(pieces, w, m, v)


PACK_WIDTH = 1024
PACK_ROWS = 64


def _pack(arrays, dtype, lead=0):
    parts = []
    for a in arrays:
        head = a.shape[:lead]
        f = a.reshape(head + (-1,)).astype(dtype)
        pad = (-f.shape[-1]) % PACK_WIDTH
        if pad:
            f = jnp.pad(f, [(0, 0)] * lead + [(0, pad)])
        parts.append(f.reshape(head + (-1, PACK_WIDTH)))
    out = jnp.concatenate(parts, axis=lead)
    pad = (-out.shape[lead]) % PACK_ROWS
    if pad:
        out = jnp.pad(out, [(0, 0)] * lead + [(0, pad), (0, 0)])
    return out


def _unpack(packed, shapes, lead=0):
    head = packed.shape[:lead]
    out, row = [], 0
    for s in shapes:
        n = int(np.prod(s))
        rows = -(-n // PACK_WIDTH)
        chunk = lax.slice_in_dim(packed, row, row + rows, axis=lead).reshape(head + (-1,))
        out.append(lax.slice_in_dim(chunk, 0, n, axis=lead).reshape(head + tuple(s)))
        row += rows
    return out


def _to_natural(stacked, kind):
    if kind == "col":
        m = jnp.moveaxis(stacked, 0, -2)
        return m.reshape(m.shape[:-2] + (m.shape[-2] * m.shape[-1],))
    m = jnp.moveaxis(stacked, 0, -3)
    return m.reshape(m.shape[:-3] + (m.shape[-3] * m.shape[-2], m.shape[-1]))


def _to_stacked(natural, kind):
    if kind == "col":
        m = natural.reshape(natural.shape[:-1] + (N_DEV, natural.shape[-1] // N_DEV))
        return jnp.moveaxis(m, -2, 0)
    m = natural.reshape(natural.shape[:-2] + (N_DEV, natural.shape[-2] // N_DEV, natural.shape[-1]))
    return jnp.moveaxis(m, -3, 0)


def _f_rms(h, g):
    return (_rms(h, g).astype(BF16),)


def _f_swiglu(g, u):
    g, u = g.astype(F32), u.astype(F32)
    return ((g * _sigmoid(g) * u).astype(BF16),)


def _f_ple(h, pre, pp):
    return (h + _sigmoid(pre) * pp,)


def _head_rms(x, g):
    ms = _seg_sum(x * x, HEAD_DIM) * (1.0 / HEAD_DIM)
    return x * _seg_expand(lax.rsqrt(ms + RMS_EPS), HEAD_DIM) * _tile_lanes(g, x.shape[1] // HEAD_DIM)


def _f_attpre(q, k, q_gain, k_gain):
    return _head_rms(q, q_gain) * (HEAD_DIM ** -0.5), _head_rms(k, k_gain)


def _f_shift(z, z_prev, mu):
    return (z + (z_prev - z) * mu,)


def _f_rwkvpre(r, k, v, xw, xa, xg, w0, w_up, a0, a_up, g_up, k_k, k_a):
    del r, v
    w_log = -_softplus(-(w0 + _mm(jnp.tanh(xw), w_up))) - 0.5
    decay = jnp.exp(-jnp.exp(w_log))
    a = _sigmoid(a0 + _mm(xa, a_up))
    g = _mm(_sigmoid(xg), g_up)
    kk = k * k_k
    norm = jnp.sqrt(_seg_expand(_seg_sum(kk * kk, HEAD_DIM), HEAD_DIM))
    kk = kk / jnp.maximum(norm, 1e-12)
    return k * (1.0 + (a - 1.0) * k_a), decay, -kk, kk * a, g


def _f_rwkvpost(y, r, k, v, g, lnx_w, lnx_b, r_k):
    mean = _seg_expand(_seg_sum(y, HEAD_DIM) * (1.0 / HEAD_DIM), HEAD_DIM)
    yc = y - mean
    var = _seg_expand(_seg_sum(yc * yc, HEAD_DIM) * (1.0 / HEAD_DIM), HEAD_DIM)
    yn = yc * lax.rsqrt(var + GN_EPS) * lnx_w + lnx_b
    bonus = _seg_expand(_seg_sum(r * k * r_k, HEAD_DIM), HEAD_DIM) * v
    return ((yn + bonus) * g,)


def _f_s5disc(lam_re, lam_im, log_dt):
    dt = jnp.exp(log_dt)
    mag = jnp.exp(lam_re * dt)
    ab_re, ab_im = mag * jnp.cos(lam_im * dt), mag * jnp.sin(lam_im * dt)
    denom = lam_re * lam_re + lam_im * lam_im
    z_re = ((ab_re - 1.0) * lam_re + ab_im * lam_im) / denom
    z_im = (ab_im * lam_re - (ab_re - 1.0) * lam_im) / denom
    return ab_re, ab_im, z_re, z_im


def _f_s5b(z_re, z_im, b_re, b_im):
    return z_re * b_re - z_im * b_im, z_re * b_im + z_im * b_re


def _f_s5post(ypre, u, d_skip):
    return (_gelu_tanh(ypre + d_skip * u).astype(BF16),)


def _f_glu(h, z1, z2):
    return (h + z1 * _sigmoid(z2),)


def _heads(x):
    t = x.shape[0]
    return jnp.transpose(x.reshape(t, -1, HEAD_DIM), (1, 0, 2))


def _unheads(x):
    return jnp.transpose(x, (1, 0, 2)).reshape(x.shape[1], -1)


def _shift_down(x):
    return jnp.pad(x[:-1], ((1, 0), (0, 0)))


def _shift_up(x):
    return jnp.pad(x[1:], ((0, 1), (0, 0)))


def _block_diag(blocks):
    g, a, b = blocks.shape
    eye = jnp.eye(g, dtype=blocks.dtype)
    return (blocks[:, :, None, :] * eye[:, None, :, None]).reshape(g * a, g * b)


def _diag_blocks(dense, g):
    a, b = dense.shape[0] // g, dense.shape[1] // g
    return jnp.stack([dense[i * a:(i + 1) * a, i * b:(i + 1) * b] for i in range(g)], axis=0)


def _row_tile(t, width_bytes, budget=2 * 1024 * 1024):
    for tr in (512, 256, 128, 64, 32, 16, 8):
        if t % tr == 0 and tr * width_bytes <= budget:
            return tr
    return t


def _no_hosting(name):
    return None, None


def _hosting_matmul(hosting, name, a, b, **kw):
    exchange, done = hosting(name)
    if exchange is None:
        return matmul(name, a, b, **kw)
    out, travelled = matmul(name, a, b, hosted=exchange, **kw)
    done(travelled)
    return out


def _ffn_fwd(tag, h, norm_w, weight, hosting=_no_hosting):
    t, d = h.shape
    n = rowwise(f"{tag}_norm", _f_rms, [h], [norm_w], [(d, BF16)], _row_tile(t, 4 * d))[0]
    g = _hosting_matmul(hosting, f"{tag}_gate", n, weight("gate"), out_dtype=BF16)
    u = _hosting_matmul(hosting, f"{tag}_up", n, weight("up"), out_dtype=BF16)
    f = g.shape[1]
    a = rowwise(f"{tag}_act", _f_swiglu, [g, u], [], [(f, BF16)], _row_tile(t, 4 * f))[0]
    return _hosting_matmul(hosting, f"{tag}_down", a, weight("down"), alpha=0.5, res=h), (h, n, g, u, a)


def _ffn_bwd(tag, dh2, saved, norm_w, w_gate, w_up, w_down):
    h, n, g, u, a = saved
    t, d = h.shape
    f = w_gate.shape[1]
    da = matmul(f"{tag}_da", dh2, w_down, tb=True, alpha=0.5, out_dtype=BF16)
    d_down = matmul(f"{tag}_dwdown", a, dh2, ta=True, alpha=0.5, out_dtype=BF16)
    dg, du = rowwise_vjp(f"{tag}_dact", _f_swiglu, [g, u], [], [da], _row_tile(t, 4 * f, 1 << 20), [True, True], [],
                         grad_dtypes={0: BF16, 1: BF16})
    dn = matmul(f"{tag}_dn_gate", dg, w_gate, tb=True)
    dn = matmul(f"{tag}_dn_up", du, w_up, tb=True, res=dn)
    d_gate = matmul(f"{tag}_dwgate", n, dg, ta=True, out_dtype=BF16)
    d_up = matmul(f"{tag}_dwup", n, du, ta=True, out_dtype=BF16)
    dh, d_norm = rowwise_vjp(f"{tag}_dnorm", _f_rms, [h], [norm_w], [dn], _row_tile(t, 4 * d), [True], [True],
                             add_to={0: dh2})
    return dh, d_norm, d_gate, d_up, d_down


def _ple_fwd(tag, h, norm_w, w_gate, w_proj, p_i, hosting=_no_hosting):
    t, d = h.shape
    n = rowwise(f"{tag}_norm", _f_rms, [h], [norm_w], [(d, BF16)], _row_tile(t, 4 * d))[0]
    pre = _hosting_matmul(hosting, f"{tag}_gate", n, w_gate)
    pp = matmul(f"{tag}_proj", p_i, w_proj)
    h2 = rowwise(f"{tag}_out", _f_ple, [h, pre, pp], [], [(d, F32)], _row_tile(t, 4 * d))[0]
    return h2, (h, n, pre, pp)


def _ple_bwd(tag, dh2, saved, norm_w, w_gate, w_proj, p_i):
    h, n, pre, pp = saved
    t, d = h.shape
    dpre, dpp = rowwise_vjp(f"{tag}_dout", _f_ple, [h, pre, pp], [], [dh2], _row_tile(t, 4 * d), [False, True, True], [],
                            grad_dtypes={1: BF16, 2: BF16})
    d_proj = matmul(f"{tag}_dwproj", p_i, dpp, ta=True, out_dtype=BF16)
    d_gate = matmul(f"{tag}_dwgate", n, dpre, ta=True, out_dtype=BF16)
    dn = matmul(f"{tag}_dn", dpre, w_gate, tb=True)
    dh, d_norm = rowwise_vjp(f"{tag}_dnorm", _f_rms, [h], [norm_w], [dn], _row_tile(t, 4 * d), [True], [True],
                             add_to={0: dh2})
    return dh, d_norm, d_gate, d_proj


def _ab_fwd(h, w, hosting=_no_hosting):
    t, d = h.shape
    da = d // 2
    n = rowwise("ab_norm", _f_rms, [h], [w["mix_norm"]], [(d, BF16)], _row_tile(t, 4 * d))[0]
    proj = _hosting_matmul(hosting, "ab_in", n, w["ab_w_in"])
    q_raw, k_raw, v_att, z = proj[:, :da], proj[:, da:2 * da], proj[:, 2 * da:3 * da], proj[:, 3 * da:]
    tr = _row_tile(t, 4 * da, 1 << 19)
    qn, kn = rowwise("att_pre", _f_attpre, [q_raw, k_raw], [w["att_q_gain"], w["att_k_gain"]], [(da, F32)] * 2, tr)
    bias = jnp.transpose(bias_expand(w["att_rel_bias"]), (1, 0, 2))
    qh, kh, vh = _heads(qn), _heads(kn), _heads(v_att)
    exchange, done = hosting("attention_fwd")
    att_h = attention_fwd(qh, kh, vh, bias, hosted=exchange)
    if exchange is not None:
        att_h, travelled = att_h
        done(travelled)
    att = _unheads(att_h)
    z_prev = _shift_down(z)
    zz = rowwise("rwkv_shift", _f_shift, [z, z_prev], [w["rwkv_mu"]], [(z.shape[1], F32)], _row_tile(t, 4 * z.shape[1]))[0]
    o = [0, da, 2 * da, 3 * da, 3 * da + DECAY_LORA, 3 * da + DECAY_LORA + AAA_LORA, z.shape[1]]
    r, k, v, xw, xa, xg = (zz[:, o[i]:o[i + 1]] for i in range(6))
    pre_consts = [w[nm] for nm in ("rwkv_w0", "rwkv_w_up", "rwkv_a0", "rwkv_a_up", "rwkv_g_up", "rwkv_k_k", "rwkv_k_a")]
    k2, decay, ia, ib, g = rowwise("rwkv_pre", _f_rwkvpre, [r, k, v, xw, xa, xg], pre_consts, [(da, F32)] * 5, tr)
    heads = [_heads(a) for a in (r, k2, v, decay, ia, ib)]
    exchange, done = hosting("rwkv_fwd")
    y_h, states, *travelled = rwkv_fwd(*heads, hosted=exchange)
    if exchange is not None:
        done(travelled)
    y = _unheads(y_h)
    post_consts = [w["rwkv_lnx_w"], w["rwkv_lnx_b"], w["rwkv_r_k"]]
    rw = rowwise("rwkv_post", _f_rwkvpost, [y, r, k2, v, g], post_consts, [(da, F32)], tr)[0]
    cat = jnp.concatenate([att, rw], axis=1).astype(BF16)
    h2 = _hosting_matmul(hosting, "ab_out", cat, w["ab_w_out"], res=h)
    saved = dict(h=h, n=n, q_raw=q_raw, k_raw=k_raw, qh=qh, kh=kh, vh=vh, bias=bias, z=z, z_prev=z_prev,
                 rows=(r, k, v, xw, xa, xg), pre_consts=pre_consts, post=(y, r, k2, v, g), post_consts=post_consts,
                 heads=heads, states=states, cat=cat, tr=tr)
    return h2, saved


def _ab_bwd(dh2, s, w, make_hosted=None, hosted_done=None):
    h = s["h"]
    t, d = h.shape
    da = d // 2
    tr = s["tr"]
    grads = {}
    dcat = matmul("ab_dcat", dh2, w["ab_w_out"], tb=True)
    grads["ab_w_out"] = matmul("ab_dwout", s["cat"], dh2, ta=True, out_dtype=BF16)
    d_att, d_rw = dcat[:, :da], dcat[:, da:]
    dy, dr1, dk1, dv1, dg, grads["rwkv_lnx_w"], grads["rwkv_lnx_b"], grads["rwkv_r_k"] = rowwise_vjp(
        "rwkv_dpost", _f_rwkvpost, list(s["post"]), s["post_consts"], [d_rw], tr, [True] * 5, [True] * 3)
    hosted = None if make_hosted is None else make_hosted(grads["ab_w_out"])
    *d_heads, = rwkv_bwd(*s["heads"], s["states"], _heads(dy), hosted=hosted)
    if hosted is not None:
        hosted_done(d_heads[6:])
        d_heads = d_heads[:6]
    dr2, dk2, dv2, ddecay, dia, dib = (_unheads(a) for a in d_heads)
    pre = rowwise_vjp("rwkv_dpre", _f_rwkvpre, list(s["rows"]), s["pre_consts"], [(dk1, dk2), ddecay, dia, dib, dg], tr,
                      [True] * 6, [True] * 7, add_to={0: (dr1, dr2), 2: (dv1, dv2)})
    for nm, g in zip(("rwkv_w0", "rwkv_w_up", "rwkv_a0", "rwkv_a_up", "rwkv_g_up", "rwkv_k_k", "rwkv_k_a"), pre[6:]):
        grads[nm] = g
    dzz = jnp.concatenate(pre[:6], axis=1)
    trz = _row_tile(t, 4 * dzz.shape[1])
    grads["rwkv_mu"] = rowwise_vjp("rwkv_dmu", _f_shift, [s["z"], s["z_prev"]], [w["rwkv_mu"]], [dzz], trz,
                                   [False, False], [True])[0]
    dz = rowwise("rwkv_dshift", _f_shift, [dzz, _shift_up(dzz)], [w["rwkv_mu"]], [(dzz.shape[1], F32)], trz)[0]
    dqh, dkh, dvh, dbias = attention_bwd(s["qh"], s["kh"], s["vh"], s["bias"], _heads(d_att))
    grads["att_rel_bias"] = bias_reduce(jnp.transpose(dbias, (1, 0, 2)))
    dq_raw, dk_raw, grads["att_q_gain"], grads["att_k_gain"] = rowwise_vjp(
        "att_dpre", _f_attpre, [s["q_raw"], s["k_raw"]], [w["att_q_gain"], w["att_k_gain"]],
        [_unheads(dqh), _unheads(dkh)], tr, [True, True], [True, True])
    dproj = jnp.concatenate([dq_raw, dk_raw, _unheads(dvh), dz], axis=1).astype(BF16)
    dn = matmul("ab_dn", dproj, w["ab_w_in"], tb=True)
    grads["ab_w_in"] = matmul("ab_dwin", s["n"], dproj, ta=True, out_dtype=BF16)
    dh, grads["mix_norm"] = rowwise_vjp("ab_dnorm", _f_rms, [h], [w["mix_norm"]], [dn], _row_tile(t, 4 * d), [True], [True],
                                        add_to={0: dh2})
    return dh, grads


def _s5_fwd(h, w):
    t, d = h.shape
    n_groups, n_state = w["ssm_lambda_re"].shape
    gp = n_groups * n_state
    n = rowwise("s5_norm", _f_rms, [h], [w["mix_norm"]], [(d, BF16)], _row_tile(t, 4 * d))[0]
    u = matmul("s5_in", n, w["ssm_w_in"])
    disc_rows = [w["ssm_lambda_re"], w["ssm_lambda_im"], w["ssm_log_dt"]]
    ab_re, ab_im, z_re, z_im = rowwise("s5_disc", _f_s5disc, disc_rows, [], [(n_state, F32)] * 4, n_groups)
    b_rows = [z_re.reshape(gp, 1), z_im.reshape(gp, 1), w["ssm_b_re"], w["ssm_b_im"]]
    trb = _row_tile(gp, 512)
    bb_re, bb_im = rowwise("s5_bbar", _f_s5b, b_rows, [], [(SSM_GROUP, F32)] * 2, trb)
    to_dense = lambda bb: _block_diag(jnp.transpose(bb.reshape(n_groups, n_state, SSM_GROUP), (0, 2, 1)))
    bd_re, bd_im = to_dense(bb_re), to_dense(bb_im)
    cd_re = _block_diag(jnp.transpose(w["ssm_c_re"], (0, 2, 1)))
    cd_im = -_block_diag(jnp.transpose(w["ssm_c_im"], (0, 2, 1)))
    a_re, a_im = ab_re.reshape(1, gp), ab_im.reshape(1, gp)
    bu_re = matmul("s5_bu_re", u, bd_re)
    bu_im = matmul("s5_bu_im", u, bd_im)
    h_re, h_im = s5_scan(bu_re, bu_im, a_re, a_im)
    ypre = matmul("s5_y_re", h_re, cd_re)
    ypre = matmul("s5_y_im", h_im, cd_im, res=ypre)
    tru = _row_tile(t, 4 * u.shape[1])
    yg = rowwise("s5_post", _f_s5post, [ypre, u], [w["ssm_d"]], [(u.shape[1], BF16)], tru)[0]
    w_out1, w_out2 = w["ssm_w_out"][:, :d], w["ssm_w_out"][:, d:]
    z1 = matmul("s5_out1", yg, w_out1)
    z2 = matmul("s5_out2", yg, w_out2)
    h2 = rowwise("s5_glu", _f_glu, [h, z1, z2], [], [(d, F32)], _row_tile(t, 4 * d))[0]
    saved = dict(h=h, n=n, u=u, disc_rows=disc_rows, b_rows=b_rows, trb=trb, bd=(bd_re, bd_im), cd=(cd_re, cd_im),
                 a=(a_re, a_im), hs=(h_re, h_im), ypre=ypre, yg=yg, w_out=(w_out1, w_out2), z=(z1, z2), tru=tru)
    return h2, saved


def _s5_bwd(dh2, s, w):
    h, u = s["h"], s["u"]
    t, d = h.shape
    n_groups, n_state = w["ssm_lambda_re"].shape
    gp = n_groups * n_state
    grads = {}
    z1, z2 = s["z"]
    w_out1, w_out2 = s["w_out"]
    dz1, dz2 = rowwise_vjp("s5_dglu", _f_glu, [h, z1, z2], [], [dh2], _row_tile(t, 4 * d), [False, True, True], [],
                           grad_dtypes={1: BF16, 2: BF16})
    dyg = matmul("s5_dyg1", dz1, w_out1, tb=True)
    dyg = matmul("s5_dyg2", dz2, w_out2, tb=True, res=dyg)
    grads["ssm_w_out"] = jnp.concatenate([matmul("s5_dwout1", s["yg"], dz1, ta=True, out_dtype=BF16),
                                          matmul("s5_dwout2", s["yg"], dz2, ta=True, out_dtype=BF16)], axis=1)
    dypre, du1, grads["ssm_d"] = rowwise_vjp("s5_dpost", _f_s5post, [s["ypre"], u], [w["ssm_d"]], [dyg], s["tru"],
                                             [True, True], [True])
    cd_re, cd_im = s["cd"]
    h_re, h_im = s["hs"]
    dh_re = matmul("s5_dh_re", dypre, cd_re, tb=True)
    dh_im = matmul("s5_dh_im", dypre, cd_im, tb=True)
    dcd_re = matmul("s5_dc_re", h_re, dypre, ta=True)
    dcd_im = matmul("s5_dc_im", h_im, dypre, ta=True)
    a_re, a_im = s["a"]
    g_re, g_im, da_re, da_im = s5_scan(dh_re, dh_im, a_re, a_im, reverse=True,
                                       h_prev=(_shift_down(h_re), _shift_down(h_im)))
    bd_re, bd_im = s["bd"]
    du = matmul("s5_du_re", g_re, bd_re, tb=True, res=du1)
    du = matmul("s5_du_im", g_im, bd_im, tb=True, res=du)
    dbd_re = matmul("s5_db_re", u, g_re, ta=True)
    dbd_im = matmul("s5_db_im", u, g_im, ta=True)
    grads["ssm_w_in"] = matmul("s5_dwin", s["n"], du, ta=True, out_dtype=BF16)
    dn = matmul("s5_dn", du, w["ssm_w_in"], tb=True)
    dh, grads["mix_norm"] = rowwise_vjp("s5_dnorm", _f_rms, [h], [w["mix_norm"]], [dn], _row_tile(t, 4 * d), [True], [True],
                                        add_to={0: dh2})
    from_dense = lambda m: jnp.transpose(_diag_blocks(m, n_groups), (0, 2, 1)).reshape(gp, SSM_GROUP)
    dz_re, dz_im, grads["ssm_b_re"], grads["ssm_b_im"] = rowwise_vjp(
        "s5_dbbar", _f_s5b, s["b_rows"], [], [from_dense(dbd_re), from_dense(dbd_im)], s["trb"], [True] * 4, [])
    disc_cots = [da_re.reshape(n_groups, n_state), da_im.reshape(n_groups, n_state),
                 dz_re.reshape(n_groups, n_state), dz_im.reshape(n_groups, n_state)]
    grads["ssm_lambda_re"], grads["ssm_lambda_im"], grads["ssm_log_dt"] = rowwise_vjp(
        "s5_ddisc", _f_s5disc, s["disc_rows"], [], disc_cots, n_groups, [True] * 3, [])
    grads["ssm_c_re"] = jnp.transpose(_diag_blocks(dcd_re, n_groups), (0, 2, 1))
    grads["ssm_c_im"] = -jnp.transpose(_diag_blocks(dcd_im, n_groups), (0, 2, 1))
    return dh, grads


WEIGHTS = ["ffn1_norm", "ffn1_w_gate", "ffn1_w_up", "ffn1_w_down", "mix_norm", "ffn2_norm", "ffn2_w_gate", "ffn2_w_up",
           "ffn2_w_down", "ple_norm", "ple_w_gate", "ple_w_proj", "ab_w_in", "att_q_gain", "att_k_gain", "att_rel_bias",
           "rwkv_mu", "rwkv_w0", "rwkv_w_up", "rwkv_a0", "rwkv_a_up", "rwkv_g_up", "rwkv_k_k", "rwkv_k_a", "rwkv_r_k",
           "rwkv_lnx_w", "rwkv_lnx_b", "ab_w_out", "ssm_w_in", "ssm_lambda_re", "ssm_lambda_im", "ssm_log_dt", "ssm_b_re",
           "ssm_b_im", "ssm_c_re", "ssm_c_im", "ssm_d", "ssm_w_out"]
BIG = {"ffn1_w_gate": "col", "ffn1_w_up": "col", "ffn1_w_down": "row", "ffn2_w_gate": "col", "ffn2_w_up": "col",
       "ffn2_w_down": "row", "ple_w_gate": "row", "ple_w_proj": "col", "ab_w_in": "col", "ab_w_out": "row",
       "ssm_w_in": "row", "ssm_w_out": "col"}
SMALL_CUT = {"rwkv_w_up": "col", "rwkv_a_up": "col", "rwkv_g_up": "col", "ssm_d": "col"}
REPLICATED = [n for n in WEIGHTS if n not in BIG and n not in SMALL_CUT]


def kernel(x, p, ffn1_norm, ffn1_w_gate, ffn1_w_up, ffn1_w_down, mix_norm, ffn2_norm, ffn2_w_gate, ffn2_w_up, ffn2_w_down, ple_norm, ple_w_gate, ple_w_proj, ab_w_in, att_q_gain, att_k_gain, att_rel_bias, rwkv_mu, rwkv_w0, rwkv_w_up, rwkv_a0, rwkv_a_up, rwkv_g_up, rwkv_k_k, rwkv_k_a, rwkv_r_k, rwkv_lnx_w, rwkv_lnx_b, ab_w_out, ssm_w_in, ssm_lambda_re, ssm_lambda_im, ssm_log_dt, ssm_b_re, ssm_b_im, ssm_c_re, ssm_c_im, ssm_d, ssm_w_out, loss_target, m_ffn1_norm, m_ffn1_w_gate, m_ffn1_w_up, m_ffn1_w_down, m_mix_norm, m_ffn2_norm, m_ffn2_w_gate, m_ffn2_w_up, m_ffn2_w_down, m_ple_norm, m_ple_w_gate, m_ple_w_proj, m_ab_w_in, m_att_q_gain, m_att_k_gain, m_att_rel_bias, m_rwkv_mu, m_rwkv_w0, m_rwkv_w_up, m_rwkv_a0, m_rwkv_a_up, m_rwkv_g_up, m_rwkv_k_k, m_rwkv_k_a, m_rwkv_r_k, m_rwkv_lnx_w, m_rwkv_lnx_b, m_ab_w_out, m_ssm_w_in, m_ssm_lambda_re, m_ssm_lambda_im, m_ssm_log_dt, m_ssm_b_re, m_ssm_b_im, m_ssm_c_re, m_ssm_c_im, m_ssm_d, m_ssm_w_out, v_ffn1_norm, v_ffn1_w_gate, v_ffn1_w_up, v_ffn1_w_down, v_mix_norm, v_ffn2_norm, v_ffn2_w_gate, v_ffn2_w_up, v_ffn2_w_down, v_ple_norm, v_ple_w_gate, v_ple_w_proj, v_ab_w_in, v_att_q_gain, v_att_k_gain, v_att_rel_bias, v_rwkv_mu, v_rwkv_w0, v_rwkv_w_up, v_rwkv_a0, v_rwkv_a_up, v_rwkv_g_up, v_rwkv_k_k, v_rwkv_k_a, v_rwkv_r_k, v_rwkv_lnx_w, v_rwkv_lnx_b, v_ab_w_out, v_ssm_w_in, v_ssm_lambda_re, v_ssm_lambda_im, v_ssm_log_dt, v_ssm_b_re, v_ssm_b_im, v_ssm_c_re, v_ssm_c_im, v_ssm_d, v_ssm_w_out):
    vals = dict(locals())
    depth = ffn1_norm.shape[0]
    n_groups, n_state = ssm_lambda_re.shape[1:]

    kinds = {**BIG, **SMALL_CUT}
    travel = {n: (BF16 if n in BIG else F32) for n in kinds}

    def rows(a, lead=0):
        return a.reshape(a.shape[:lead] + (-1, a.shape[-1]))

    def model_layer(n, j):
        if vals[n].shape[0] == depth:
            return j
        return 2 * j if n.startswith(("ab_", "rwkv_")) else 2 * j + 1

    units = [(n, j) for n in kinds for j in range(vals[n].shape[0])]
    early_names = ("ffn1_w_gate", "ffn1_w_up", "ffn1_w_down", "ab_w_in", "rwkv_w_up", "rwkv_a_up", "rwkv_g_up")
    early = [u for u in units if model_layer(*u) == 0 and u[0] in early_names]
    late = [u for u in units if u not in early]
    full = {}

    def shard(unit):
        n, j = unit
        return rows(vals[n][j]).astype(travel[n])

    def take_gathered(which, got):
        for (n, j), g in zip(which, got):
            full[(n, j)] = _to_natural(g.reshape((N_DEV,) + vals[n].shape[1:]), kinds[n])

    first = [("ffn1_w_gate", 0), ("ffn1_w_up", 0)]
    beside = {"l0_ffn1_gate": [("ffn1_w_down", 0)],
              "l0_ffn1_up": [("ab_w_in", 0), ("rwkv_w_up", 0), ("rwkv_a_up", 0), ("rwkv_g_up", 0)],
              "l0_ffn1_down": [("ab_w_out", 0), ("ffn2_w_gate", 0)],
              "ab_in": [("ffn2_w_up", 0)],
              "attention_fwd": [("ffn2_w_down", 0), ("ple_w_gate", 0), ("ple_w_proj", 0)],
              "l0_ffn2_gate": [("ffn2_w_gate", 1)], "l0_ffn2_up": [("ffn2_w_up", 1)], "l0_ffn2_down": [("ffn2_w_down", 1)],
              "l0_ple_gate": [("ple_w_gate", 1), ("ple_w_proj", 1)]}
    planned = first + [u for us in beside.values() for u in us]
    beside["rwkv_fwd"] = [u for u in units if u not in planned]
    assert depth == 2 and sorted(planned + beside["rwkv_fwd"]) == sorted(units)
    take_gathered(first, gather_all([shard(u) for u in first]).run("gather_first"))

    def hosting(name):
        which = beside.get(name)
        if not which:
            return None, None
        return gather_all([shard(u) for u in which]), functools.partial(take_gathered, which)

    def row(name, j):
        return vals[name][j].reshape(1, -1)

    def ffn_weights(which, i):
        return (row(f"{which}_norm", i), full[(f"{which}_w_gate", i)], full[(f"{which}_w_up", i)],
                full[(f"{which}_w_down", i)])

    def mixer_weights(i):
        j = i // 2
        if i % 2 == 0:
            w = {n: row(n, j) for n in ("att_q_gain", "att_k_gain", "rwkv_mu", "rwkv_w0", "rwkv_a0", "rwkv_k_k", "rwkv_k_a",
                                        "rwkv_r_k", "rwkv_lnx_w", "rwkv_lnx_b")}
            w.update({n: full.get((n, j)) for n in ("ab_w_in", "ab_w_out", "rwkv_w_up", "rwkv_a_up", "rwkv_g_up")})
            w["att_rel_bias"] = att_rel_bias[j]
        else:
            w = {"ssm_lambda_re": ssm_lambda_re[j], "ssm_lambda_im": ssm_lambda_im[j],
                 "ssm_log_dt": ssm_log_dt[j].reshape(n_groups, 1),
                 "ssm_b_re": ssm_b_re[j].reshape(n_groups * n_state, -1),
                 "ssm_b_im": ssm_b_im[j].reshape(n_groups * n_state, -1),
                 "ssm_c_re": ssm_c_re[j], "ssm_c_im": ssm_c_im[j], "ssm_d": full[("ssm_d", j)].reshape(1, -1),
                 "ssm_w_in": full[("ssm_w_in", j)], "ssm_w_out": full[("ssm_w_out", j)]}
        w["mix_norm"] = row("mix_norm", i)
        return w

    def ple_weights(i):
        return (row("ple_norm", i), full[("ple_w_gate", i)], full[("ple_w_proj", i)], p[i, 0])

    h = x[0]
    saved = []
    for i in range(depth):
        h, s1 = _ffn_fwd(f"l{i}_ffn1", h, row("ffn1_norm", i), lambda kind, i=i: full[(f"ffn1_w_{kind}", i)], hosting)
        h, sm = _ab_fwd(h, mixer_weights(i), hosting) if i % 2 == 0 else _s5_fwd(h, mixer_weights(i))
        h, s2 = _ffn_fwd(f"l{i}_ffn2", h, row("ffn2_norm", i), lambda kind, i=i: full[(f"ffn2_w_{kind}", i)], hosting)
        h, sp = _ple_fwd(f"l{i}_ple", h, *ple_weights(i), hosting)
        saved.append((s1, sm, s2, sp))
    dh, loss_part = loss_head(h, loss_target[0])

    per_layer = {n: [] for n in WEIGHTS}
    received = {}

    def chip_sums(which, tag):
        pieces = [rows(_to_stacked(per_layer[n][j - vals[n].shape[0]], kinds[n]).astype(travel[n]), lead=1)
                  for n, j in which]
        from_sibling = pair_exchange(pieces).run(f"reduce_pair_{tag}")
        return [pair_sum(f"pair_sum_{n}{j}", a, b) for (n, j), a, b in zip(which, pieces, from_sibling)]

    def late_reduce(d_ab_w_out):
        per_layer["ab_w_out"].insert(0, d_ab_w_out)
        return chip_exchange(chip_sums(late, "late"))

    def late_reduce_done(got):
        received.update(zip(late, got))

    for i in reversed(range(depth)):
        s1, sm, s2, sp = saved[i]
        dh, d_norm, d_gate, d_proj = _ple_bwd(f"l{i}_ple", dh, sp, *ple_weights(i))
        for n, g in (("ple_norm", d_norm), ("ple_w_gate", d_gate), ("ple_w_proj", d_proj)):
            per_layer[n].insert(0, g)
        dh, d_norm, d_gate, d_up, d_down = _ffn_bwd(f"l{i}_ffn2", dh, s2, *ffn_weights("ffn2", i))
        for n, g in (("ffn2_norm", d_norm), ("ffn2_w_gate", d_gate), ("ffn2_w_up", d_up), ("ffn2_w_down", d_down)):
            per_layer[n].insert(0, g)
        if i == 0:
            dh, mixer_grads = _ab_bwd(dh, sm, mixer_weights(i), late_reduce, late_reduce_done)
        else:
            dh, mixer_grads = (_ab_bwd if i % 2 == 0 else _s5_bwd)(dh, sm, mixer_weights(i))
        for n, g in mixer_grads.items():
            if not (i == 0 and n == "ab_w_out"):
                per_layer[n].insert(0, g)
        dh, d_norm, d_gate, d_up, d_down = _ffn_bwd(f"l{i}_ffn1", dh, s1, *ffn_weights("ffn1", i))
        for n, g in (("ffn1_norm", d_norm), ("ffn1_w_gate", d_gate), ("ffn1_w_up", d_up), ("ffn1_w_down", d_down)):
            per_layer[n].insert(0, g)
    grad_x = dh[None]

    received.update(zip(early, chip_exchange(chip_sums(early, "early")).run("reduce_chips_early")))

    rep_mine = _pack([jnp.stack(per_layer[n], axis=0).reshape(vals[n].shape) for n in REPLICATED] + [loss_part], F32)
    rep_all = gather_all([rep_mine]).run("gather_replicated")[0]

    out = {}
    for n in kinds:
        got = [received[(n, j)] for j in range(vals[n].shape[0])]
        got = got[0] if len(got) == 1 else jnp.concatenate(got, axis=1)
        state = [rows(vals[pre + n]) for pre in ("", "m_", "v_")]
        out[n] = tuple(r.reshape(vals[n].shape) for r in adam_update(f"adam_{n}", got, *state))
    zero = jnp.zeros((1, 1), F32)
    state = [_pack([vals[pre + n] for n in REPLICATED] + [zero], F32) for pre in ("", "m_", "v_")]
    shapes = [vals[n].shape for n in REPLICATED] + [zero.shape]
    results = [_unpack(r, shapes) for r in adam_update("adam_replicated", rep_all, *state)]
    for q, n in enumerate(REPLICATED):
        out[n] = tuple(r[q] for r in results)
    loss = results[0][-1].reshape(())
    return (loss, grad_x, *[out[n][0] for n in WEIGHTS], *[out[n][1] for n in WEIGHTS], *[out[n][2] for n in WEIGHTS],
            *[out[n][3] for n in WEIGHTS])
```

```python
import functools
import math

import jax
import jax.numpy as jnp
import numpy as np
from jax import lax
from jax.experimental import pallas as pl
from jax.experimental.pallas import tpu as pltpu

F32 = jnp.float32
BF16 = jnp.bfloat16
HIGHEST = lax.Precision.HIGHEST
MESH_AXES = ("x", "y", "c")
N_DEV = 8

CHUNK = 64
N_LEFT_CHUNKS = 8
BAND = (N_LEFT_CHUNKS + 1) * CHUNK
HEAD_DIM = 64
REL_CLIP = 128
N_REL = (CHUNK - 1) + REL_CLIP + 1
DECAY_LORA = 64
AAA_LORA = 64
GATE_LORA = 128
SSM_GROUP = 16
SSM_STATE = 64
RMS_EPS = 1e-6
GN_EPS = 64e-5
ADAM_LR = 0.001
ADAM_B1 = 0.9
ADAM_B2 = 0.999
ADAM_EPS = 1e-08
ADAM_WD = 0.01
ADAM_STEP = 10

RWKV_CHUNK = 64
VMEM_LIMIT = 56 * 1024 * 1024
LANES = 128


def _params(semantics):
    return pltpu.CompilerParams(dimension_semantics=semantics, vmem_limit_bytes=VMEM_LIMIT)


def _pick(n, prefs):
    for t in prefs:
        if n % t == 0:
            return t
    return n


def _dot(a, b, dims):
    return lax.dot_general(a, b, (dims, ((), ())), precision=HIGHEST, preferred_element_type=F32)


def _mm(a, b):
    return _dot(a, b, ((1,), (0,)))


def _mm_nt(a, b):
    return _dot(a, b, ((1,), (1,)))


def _mm_tn(a, b):
    return _dot(a, b, ((0,), (0,)))


def _split2(x):
    hi = x.astype(BF16)
    return hi, (x - hi.astype(F32)).astype(BF16)


def _dot3_raw(a, b, dims):
    a_hi, a_lo = _split2(a)
    b_hi, b_lo = _split2(b)
    dot = lambda p, q: lax.dot_general(p, q, (dims, ((), ())), preferred_element_type=F32)
    return dot(a_hi, b_hi) + (dot(a_hi, b_lo) + dot(a_lo, b_hi))


def _dot1_raw(a, b, dims):
    return lax.dot_general(a.astype(BF16), b.astype(BF16), (dims, ((), ())), preferred_element_type=F32)


NN, NT, TN = ((1,), (0,)), ((1,), (1,)), ((0,), (0,))


def _make_dot(raw):
    @functools.partial(jax.custom_vjp, nondiff_argnums=(2,))
    def dot(a, b, dims):
        return raw(a, b, dims)

    def fwd(a, b, dims):
        return raw(a, b, dims), (a, b)

    def bwd(dims, saved, g):
        a, b = saved
        if dims == NN:
            return raw(g, b, NT), raw(a, g, TN)
        if dims == NT:
            return raw(g, b, NN), raw(g, a, TN)
        return raw(b, g, NT), raw(a, g, NN)

    dot.defvjp(fwd, bwd)
    return dot


_dot3 = _make_dot(_dot3_raw)
_dot1 = _make_dot(_dot1_raw)


def _dot_ind_raw(x, ind, dims):
    hi = x.astype(BF16)
    rest = x - hi.astype(F32)
    mid = rest.astype(BF16)
    lo = (rest - mid.astype(F32)).astype(BF16)
    ind = ind.astype(BF16)
    dot = lambda p: lax.dot_general(p, ind, (dims, ((), ())), preferred_element_type=F32)
    return dot(hi) + (dot(mid) + dot(lo))


@jax.custom_vjp
def _mm_ind(x, ind):
    return _dot_ind_raw(x, ind, NN)


def _mm_ind_fwd(x, ind):
    return _dot_ind_raw(x, ind, NN), ind


def _mm_ind_bwd(ind, g):
    return _dot_ind_raw(g, ind, NT), jnp.zeros_like(ind)


_mm_ind.defvjp(_mm_ind_fwd, _mm_ind_bwd)


def _sigmoid(x):
    return 1.0 / (1.0 + jnp.exp(-x))


def _softplus(x):
    return jnp.maximum(x, 0.0) + jnp.log(1.0 + jnp.exp(-jnp.abs(x)))


def _gelu_tanh(x):
    return 0.5 * x * (1.0 + jnp.tanh(math.sqrt(2.0 / math.pi) * (x + 0.044715 * (x * x * x))))


def _seg_indicator(n, seg):
    r = lax.broadcasted_iota(jnp.int32, (n, n // seg), 0)
    c = lax.broadcasted_iota(jnp.int32, (n, n // seg), 1)
    return jnp.where((r >= c * seg) & (r < (c + 1) * seg), 1.0, 0.0).astype(F32)


def _seg_indicator_t(n, seg):
    c = lax.broadcasted_iota(jnp.int32, (n // seg, n), 0)
    r = lax.broadcasted_iota(jnp.int32, (n // seg, n), 1)
    return jnp.where((r >= c * seg) & (r < (c + 1) * seg), 1.0, 0.0).astype(F32)


def _seg_sum(x, seg):
    return _mm_ind(x, _seg_indicator(x.shape[1], seg))


def _seg_expand(s, seg):
    return _mm_ind(s, _seg_indicator_t(s.shape[1] * seg, seg))


def _tile_lanes(g, reps):
    n = g.shape[1]
    r = lax.broadcasted_iota(jnp.int32, (n, n * reps), 0)
    c = lax.broadcasted_iota(jnp.int32, (n, n * reps), 1)
    return _mm_ind(g, jnp.where((c & (n - 1)) == r, 1.0, 0.0).astype(F32))


def _rms(x, g):
    return x * lax.rsqrt(jnp.mean(x * x, axis=-1, keepdims=True) + RMS_EPS) * g


MATMUL_VMEM_BUDGET = 40 * 1024 * 1024
MATMUL_MAX_TILE = 2048
HBM_BYTES_PER_S = 1.2e12
MXU_FLOPS_PER_S = 8e14
GRID_STEP_S = 0.35e-6


def _tile_candidates(dim):
    c = [d for d in range(LANES, min(dim, MATMUL_MAX_TILE) + 1, LANES) if dim % d == 0]
    return c or [dim]


def _matmul_tiles(m, n, k, a_bytes, b_bytes, out_bytes, res_bytes):
    best = None
    for tm in _tile_candidates(m):
        for tn in _tile_candidates(n):
            for tk in _tile_candidates(k):
                casts = (tm * tk * 2 if a_bytes > 2 else 0) + (tk * tn * 2 if b_bytes > 2 else 0)
                vmem = (2 * (tm * tk * a_bytes + tk * tn * b_bytes + tm * tn * (out_bytes + res_bytes))
                        + 2 * tm * tn * 4 + casts)
                if vmem > MATMUL_VMEM_BUDGET:
                    continue
                steps = (m // tm) * (n // tn) * (k // tk)
                a_reads = 1 if k == tk else n // tn
                traffic = (m * k * a_bytes * a_reads + k * n * b_bytes * (m // tm) + m * n * (out_bytes + res_bytes))
                dma_s = traffic / HBM_BYTES_PER_S
                ends_s = (tm * tk * a_bytes + tk * tn * b_bytes + tm * tn * out_bytes) / HBM_BYTES_PER_S
                cost = max(dma_s, 2.0 * m * n * k / MXU_FLOPS_PER_S) + 0.2 * dma_s + steps * GRID_STEP_S + ends_s
                if best is None or cost < best[0]:
                    best = (cost, tm, tn, tk)
    return best[1:]


def matmul(name, a, b, *, ta=False, tb=False, alpha=1.0, res=None, out_dtype=F32, hosted=None):
    m, k = (a.shape[1], a.shape[0]) if ta else a.shape
    n = b.shape[0] if tb else b.shape[1]
    assert k == (b.shape[1] if tb else b.shape[0]), (name, a.shape, b.shape)
    tm, tn, tk = _matmul_tiles(m, n, k, a.dtype.itemsize, b.dtype.itemsize, jnp.dtype(out_dtype).itemsize,
                               0 if res is None else res.dtype.itemsize)
    nk = k // tk
    grid = (m // tm, n // tn, nk)
    dims = ((0 if ta else 1,), (1 if tb else 0,))
    h_arrays, h_specs, h_shapes, h_scratch = _hosted_call_args(hosted)
    nhosted = len(h_arrays)
    n_in = 2 + (res is not None)

    def body(*refs):
        a_ref, b_ref = refs[:2]
        res_ref = refs[2] if res is not None else None
        h_in, o_ref, h_out = refs[n_in:n_in + nhosted], refs[n_in + nhosted], refs[n_in + nhosted + 1:n_in + 2 * nhosted + 1]
        acc_ref, sems = refs[n_in + 2 * nhosted + 1], refs[n_in + 2 * nhosted + 2:]
        _hosted_steps(hosted, h_in, h_out, sems, grid, 0.6)
        kk = pl.program_id(2)

        @pl.when(kk == 0)
        def _():
            acc_ref[...] = jnp.zeros_like(acc_ref)

        acc_ref[...] += lax.dot_general(a_ref[...].astype(BF16), b_ref[...].astype(BF16), (dims, ((), ())),
                                        preferred_element_type=F32)

        @pl.when(kk == nk - 1)
        def _():
            out = acc_ref[...] * alpha
            if res_ref is not None:
                out = out + res_ref[...].astype(F32)
            o_ref[...] = out.astype(o_ref.dtype)

    a_spec = pl.BlockSpec((tk, tm), lambda i, j, kk: (kk, i)) if ta else pl.BlockSpec((tm, tk), lambda i, j, kk: (i, kk))
    b_spec = pl.BlockSpec((tn, tk), lambda i, j, kk: (j, kk)) if tb else pl.BlockSpec((tk, tn), lambda i, j, kk: (kk, j))
    o_spec = pl.BlockSpec((tm, tn), lambda i, j, kk: (i, j))
    in_specs = [a_spec, b_spec] + ([o_spec] if res is not None else [])
    args = (a, b) + ((res,) if res is not None else ())
    out_shape = jax.ShapeDtypeStruct((m, n), out_dtype)
    if hosted is None:
        return pl.pallas_call(
            body, name=name, grid=grid, in_specs=in_specs, out_specs=o_spec, out_shape=out_shape,
            scratch_shapes=[pltpu.VMEM((tm, tn), F32)], compiler_params=_params(("parallel", "parallel", "arbitrary")),
        )(*args)
    out, *travelled = pl.pallas_call(
        body, name=name, grid=grid, in_specs=in_specs + h_specs, out_specs=[o_spec] + h_specs,
        out_shape=[out_shape] + h_shapes, scratch_shapes=[pltpu.VMEM((tm, tn), F32)] + h_scratch,
        compiler_params=_params(("arbitrary", "arbitrary", "arbitrary")),
    )(*args, *h_arrays)
    return out, travelled


def _row_specs(arrays, tr):
    return [pl.BlockSpec((tr, a.shape[1]), lambda i: (i, 0)) for a in arrays]


def _whole_specs(arrays):
    return [pl.BlockSpec(a.shape, lambda i: (0, 0)) for a in arrays]


def rowwise(name, fn, rows, consts, outs, tr):
    t = rows[0].shape[0]
    nr, nc = len(rows), len(consts)

    def body(*refs):
        vals = [r[...] for r in refs[:nr + nc]]
        res = fn(*vals)
        for o_ref, o in zip(refs[nr + nc:], res):
            o_ref[...] = o.astype(o_ref.dtype)

    out_shape = [jax.ShapeDtypeStruct((t, w), d) for w, d in outs]
    return pl.pallas_call(
        body, name=name, grid=(t // tr,), in_specs=_row_specs(rows, tr) + _whole_specs(consts),
        out_specs=_row_specs(out_shape, tr), out_shape=out_shape, compiler_params=_params(("parallel",)),
    )(*rows, *consts)


def rowwise_vjp(name, fn, rows, consts, cots, tr, row_grad, const_grad, add_to=None, grad_dtypes=None):
    t = rows[0].shape[0]
    nr, nc = len(rows), len(consts)
    cot_groups = [c if isinstance(c, (tuple, list)) else (c,) for c in cots]
    flat_cots = [a for g in cot_groups for a in g]
    add_to = {i: (a if isinstance(a, (tuple, list)) else (a,)) for i, a in (add_to or {}).items()}
    add_idx = [(i, q) for i in sorted(add_to) for q in range(len(add_to[i]))]
    add_arrays = [add_to[i][q] for i, q in add_idx]
    r_idx = [i for i in range(nr) if row_grad[i]]
    c_idx = [i for i in range(nc) if const_grad[i]]
    grad_dtypes = grad_dtypes or {}
    n_in = nr + nc + len(flat_cots) + len(add_arrays)

    def body(*refs):
        vals = [r[...] for r in refs[:nr + nc]]
        pos = nr + nc
        cts = []
        for g in cot_groups:
            s = refs[pos][...].astype(F32)
            for q in range(1, len(g)):
                s = s + refs[pos + q][...].astype(F32)
            cts.append(s)
            pos += len(g)
        adds = {}
        for n_add, (i, _) in enumerate(add_idx):
            term = refs[pos + n_add][...].astype(F32)
            adds[i] = adds[i] + term if i in adds else term
        diff =[vals[i] for i in r_idx] + [vals[nr + i] for i in c_idx]

        def f(*d):
            full = list(vals)
            for q, i in enumerate(r_idx):
                full[i] = d[q]
            for q, i in enumerate(c_idx):
                full[nr + i] = d[len(r_idx) + q]
            return tuple(fn(*full))

        prim, vjp = jax.vjp(f, *diff)
        grads = vjp(tuple(c.astype(p.dtype) for c, p in zip(cts, prim)))
        o_refs = refs[n_in:]
        for q, i in enumerate(r_idx):
            g = grads[q].astype(F32)
            if i in adds:
                g = g + adds[i].astype(F32)
            o_refs[q][...] = g.astype(o_refs[q].dtype)
        step = pl.program_id(0)
        for q, i in enumerate(c_idx):
            o_ref = o_refs[len(r_idx) + q]

            @pl.when(step == 0)
            def _(o_ref=o_ref):
                o_ref[...] = jnp.zeros_like(o_ref)

            o_ref[...] += grads[len(r_idx) + q].astype(F32)

    row_out = [jax.ShapeDtypeStruct(rows[i].shape, grad_dtypes.get(i, F32)) for i in r_idx]
    const_out = [jax.ShapeDtypeStruct(consts[i].shape, F32) for i in c_idx]
    outs = pl.pallas_call(
        body, name=name, grid=(t // tr,),
        in_specs=_row_specs(rows, tr) + _whole_specs(consts) + _row_specs(flat_cots, tr) + _row_specs(add_arrays, tr),
        out_specs=_row_specs(row_out, tr) + _whole_specs(const_out), out_shape=row_out + const_out,
        compiler_params=_params(("arbitrary",)),
    )(*rows, *consts, *flat_cots, *add_arrays)
    return list(outs)


def loss_head(y, target):
    t, d = y.shape
    tr = _pick(t, (256, 128, 64, 32, 16, 8))

    def body(y_ref, t_ref, dy_ref, l_ref):
        diff = y_ref[...] - t_ref[...]
        dy_ref[...] = diff * (1.0 / d)

        @pl.when(pl.program_id(0) == 0)
        def _():
            l_ref[...] = jnp.zeros_like(l_ref)

        l_ref[...] += 0.5 * jnp.sum(jnp.mean(diff * diff, axis=-1, keepdims=True), axis=0, keepdims=True)

    dy, l = pl.pallas_call(
        body, name="loss_head", grid=(t // tr,), in_specs=_row_specs([y, target], tr),
        out_specs=[pl.BlockSpec((tr, d), lambda i: (i, 0)), pl.BlockSpec((1, 1), lambda i: (0, 0))],
        out_shape=[jax.ShapeDtypeStruct((t, d), F32), jax.ShapeDtypeStruct((1, 1), F32)],
        compiler_params=_params(("arbitrary",)),
    )(y, target)
    return dy, l


ATT_PAD = N_LEFT_CHUNKS * CHUNK
MASKED = -1e30
ATT_GROUP = 2


def _attn_chunks(q_c, k_b, v_b, bias, valid):
    s = [jnp.where(ok, _dot1(q, k, NT) + b, MASKED) for q, k, b, ok in zip(q_c, k_b, bias, valid)]
    e = [jnp.exp(x - lax.stop_gradient(jnp.max(x, axis=-1, keepdims=True))) for x in s]
    p = [x / jnp.sum(x, axis=-1, keepdims=True) for x in e]
    return tuple(_dot1(x, v, NN) for x, v in zip(p, v_b))


def _band_valid(c):
    return (c * CHUNK + lax.broadcasted_iota(jnp.int32, (1, BAND), 1)) >= ATT_PAD


HEAD_PAIR = LANES // HEAD_DIM


def _split_heads(x, n):
    return [x[:, h * HEAD_DIM:(h + 1) * HEAD_DIM] for h in range(n)]


def _pair_spec(t):
    return pl.BlockSpec((t, HEAD_PAIR * HEAD_DIM), lambda p: (0, p))


def _attn_operands(q_ref, kp, vp, bias, cs, q0):
    qs = [x for s in q0 for x in _split_heads(q_ref[pl.ds(s, CHUNK), :], HEAD_PAIR)]
    ks = [x for s in q0 for x in _split_heads(kp[pl.ds(s, BAND), :], HEAD_PAIR)]
    vs = [x for s in q0 for x in _split_heads(vp[pl.ds(s, BAND), :], HEAD_PAIR)]
    return tuple(qs), tuple(ks), tuple(vs), tuple(bias) * len(cs), tuple(_band_valid(c) for c in cs for _ in bias)


def attention_fwd(q, k, v, bias, hosted=None):
    t, width = q.shape
    pairs = width // (HEAD_PAIR * HEAD_DIM)
    nchunks = t // CHUNK
    group = _pick(nchunks, (ATT_GROUP, 1))
    h_arrays, h_specs, h_shapes, h_scratch = _hosted_call_args(hosted)
    nhosted = len(h_arrays)

    def body(q_ref, k_ref, v_ref, b_ref, *rest):
        h_in, o_ref, h_out = rest[:nhosted], rest[nhosted], rest[nhosted + 1:2 * nhosted + 1]
        kp, vp, sems = rest[2 * nhosted + 1], rest[2 * nhosted + 2], rest[2 * nhosted + 3:]
        _hosted_steps(hosted, h_in, h_out, sems, (pairs,), 0.6)
        zeros = jnp.zeros((ATT_PAD, HEAD_PAIR * HEAD_DIM), F32)
        kp[pl.ds(0, ATT_PAD), :] = zeros
        vp[pl.ds(0, ATT_PAD), :] = zeros
        kp[pl.ds(ATT_PAD, t), :] = k_ref[...]
        vp[pl.ds(ATT_PAD, t), :] = v_ref[...]
        bias = [b_ref[h] for h in range(HEAD_PAIR)]

        def step(g, carry):
            cs = [g * group + u for u in range(group)]
            q0 = [pl.multiple_of(c * CHUNK, CHUNK) for c in cs]
            outs = _attn_chunks(*_attn_operands(q_ref, kp, vp, bias, cs, q0))
            for u, s in enumerate(q0):
                o_ref[pl.ds(s, CHUNK), :] = jnp.concatenate(outs[HEAD_PAIR * u:HEAD_PAIR * (u + 1)], axis=1)
            return carry

        lax.fori_loop(0, nchunks // group, step, 0)

    out, *travelled = pl.pallas_call(
        body, name="attention_fwd", grid=(pairs,),
        in_specs=[_pair_spec(t)] * 3 + [pl.BlockSpec((HEAD_PAIR, CHUNK, BAND), lambda p: (p, 0, 0))] + h_specs,
        out_specs=[_pair_spec(t)] + h_specs, out_shape=[jax.ShapeDtypeStruct((t, width), F32)] + h_shapes,
        scratch_shapes=[pltpu.VMEM((t + ATT_PAD, HEAD_PAIR * HEAD_DIM), F32)] * 2 + h_scratch,
        compiler_params=_params(("arbitrary",)),
    )(q, k, v, bias, *h_arrays)
    return out if hosted is None else (out, travelled)


def attention_bwd(q, k, v, bias, dout):
    t, width = q.shape
    pairs = width // (HEAD_PAIR * HEAD_DIM)
    nchunks = t // CHUNK
    group = _pick(nchunks, (ATT_GROUP, 1))

    def body(q_ref, k_ref, v_ref, b_ref, do_ref, dq_ref, dk_ref, dv_ref, db_ref, kp, vp, dkp, dvp):
        zeros = jnp.zeros((ATT_PAD, HEAD_PAIR * HEAD_DIM), F32)
        kp[pl.ds(0, ATT_PAD), :] = zeros
        vp[pl.ds(0, ATT_PAD), :] = zeros
        kp[pl.ds(ATT_PAD, t), :] = k_ref[...]
        vp[pl.ds(ATT_PAD, t), :] = v_ref[...]
        dkp[...] = jnp.zeros_like(dkp)
        dvp[...] = jnp.zeros_like(dvp)
        db_ref[...] = jnp.zeros_like(db_ref)
        bias = [b_ref[h] for h in range(HEAD_PAIR)]

        def step(g, carry):
            cs = [g * group + u for u in range(group)]
            q0 = [pl.multiple_of(c * CHUNK, CHUNK) for c in cs]
            qs, ks, vs, bs, valid = _attn_operands(q_ref, kp, vp, bias, cs, q0)
            _, vjp = jax.vjp(lambda a, b, cc, d: _attn_chunks(a, b, cc, d, valid), qs, ks, vs, bs)
            dos = tuple(x for s in q0 for x in _split_heads(do_ref[pl.ds(s, CHUNK), :], HEAD_PAIR))
            dq, dk, dv, db = vjp(dos)
            for u, s in enumerate(q0):
                mine = slice(HEAD_PAIR * u, HEAD_PAIR * (u + 1))
                dq_ref[pl.ds(s, CHUNK), :] = jnp.concatenate(dq[mine], axis=1)
                dkp[pl.ds(s, BAND), :] += jnp.concatenate(dk[mine], axis=1)
                dvp[pl.ds(s, BAND), :] += jnp.concatenate(dv[mine], axis=1)
            for h in range(HEAD_PAIR):
                total = db[h]
                for u in range(1, group):
                    total = total + db[HEAD_PAIR * u + h]
                db_ref[h] += total
            return carry

        lax.fori_loop(0, nchunks // group, step, 0)
        dk_ref[...] = dkp[pl.ds(ATT_PAD, t), :]
        dv_ref[...] = dvp[pl.ds(ATT_PAD, t), :]

    bias_spec = pl.BlockSpec((HEAD_PAIR, CHUNK, BAND), lambda p: (p, 0, 0))
    ts = jax.ShapeDtypeStruct((t, width), F32)
    return pl.pallas_call(
        body, name="attention_bwd", grid=(pairs,), in_specs=[_pair_spec(t)] * 3 + [bias_spec, _pair_spec(t)],
        out_specs=[_pair_spec(t)] * 3 + [bias_spec],
        out_shape=[ts, ts, ts, jax.ShapeDtypeStruct(bias.shape, F32)],
        scratch_shapes=[pltpu.VMEM((t + ATT_PAD, HEAD_PAIR * HEAD_DIM), F32)] * 4, compiler_params=_params(("parallel",)),
    )(q, k, v, bias, dout)


def _rel_onehot_t(i):
    j = lax.broadcasted_iota(jnp.int32, (N_REL, BAND), 1)
    r = lax.broadcasted_iota(jnp.int32, (N_REL, BAND), 0)
    idx = jnp.clip(i + ATT_PAD - j, -(CHUNK - 1), REL_CLIP) + (CHUNK - 1)
    return jnp.where(r == idx, 1.0, 0.0).astype(F32)


def bias_expand(rel):
    nh = rel.shape[0]

    def body(rel_ref, o_ref):
        o_ref[...] = _mm(rel_ref[...], _rel_onehot_t(pl.program_id(0)))

    return pl.pallas_call(
        body, name="bias_expand", grid=(CHUNK,), in_specs=[pl.BlockSpec((nh, N_REL), lambda i: (0, 0))],
        out_specs=pl.BlockSpec((None, nh, BAND), lambda i: (i, 0, 0)),
        out_shape=jax.ShapeDtypeStruct((CHUNK, nh, BAND), F32), compiler_params=_params(("parallel",)),
    )(rel)


def bias_reduce(dbias):
    nh = dbias.shape[1]

    def body(d_ref, o_ref):
        @pl.when(pl.program_id(0) == 0)
        def _():
            o_ref[...] = jnp.zeros_like(o_ref)

        o_ref[...] += _mm_nt(d_ref[...], _rel_onehot_t(pl.program_id(0)))

    return pl.pallas_call(
        body, name="bias_reduce", grid=(CHUNK,), in_specs=[pl.BlockSpec((None, nh, BAND), lambda i: (i, 0, 0))],
        out_specs=pl.BlockSpec((nh, N_REL), lambda i: (0, 0)), out_shape=jax.ShapeDtypeStruct((nh, N_REL), F32),
        compiler_params=_params(("arbitrary",)),
    )(dbias)


def _tri(n, strict):
    r = lax.broadcasted_iota(jnp.int32, (n, n), 0)
    c = lax.broadcasted_iota(jnp.int32, (n, n), 1)
    return (c < r) if strict else (c <= r)


def _each(f, *lists):
    return [f(*args) for args in zip(*lists)]


def _rwkv_chunk(s, r, k, v, w, a, b):
    n = r[0].shape[0]
    strict, incl = _tri(n, True), _tri(n, False)
    ones = jnp.where(incl, 1.0, 0.0).astype(F32)
    eye = jnp.where(incl & ~strict, 1.0, 0.0).astype(F32)
    lw = _each(jnp.log, w)
    cum = _each(lambda x: _mm(ones, x), lw)
    p_incl = _each(jnp.exp, cum)
    p_inv = _each(lambda x: jnp.exp(-x), cum)
    a_t = _each(lambda x, c, l: x * jnp.exp(c - l), a, cum, lw)
    r_t = _each(jnp.multiply, r, p_incl)
    b_t = _each(jnp.multiply, b, p_inv)
    k_t = _each(jnp.multiply, k, p_inv)
    a_ab = _each(lambda x, y: jnp.where(strict, _dot3(x, y, NT), 0.0), a_t, b_t)
    a_ak = _each(lambda x, y: jnp.where(strict, _dot3(x, y, NT), 0.0), a_t, k_t)
    r_b = _each(lambda x, y: jnp.where(incl, _dot3(x, y, NT), 0.0), r_t, b_t)
    r_k = _each(lambda x, y: jnp.where(incl, _dot3(x, y, NT), 0.0), r_t, k_t)
    a_s = _each(lambda x, y: _dot3(x, y, NT), a_t, s)
    r_s = _each(lambda x, y: _dot3(x, y, NT), r_t, s)
    a_kv = _each(lambda x, y: _dot3(x, y, NN), a_ak, v)
    total = _each(lambda x: eye + x, a_ab)
    power = _each(lambda x: _dot3(x, x, NN), a_ab)
    rounds = int(math.log2(n)) - 1
    for i in range(rounds):
        total = _each(lambda t, p: t + _dot3(t, p, NN), total, power)
        if i < rounds - 1:
            power = _each(lambda p: _dot3(p, p, NN), power)
    sa = _each(lambda t, x, y: _dot3(t, x + y, NN), total, a_s, a_kv)
    y = _each(lambda rs, rb, x, rk, vv: rs + _dot3(rb, x, NN) + _dot3(rk, vv, NN), r_s, r_b, sa, r_k, v)
    s_new = _each(lambda ss, x, bt, vv, kt, p: (ss + _dot3(x, bt, TN) + _dot3(vv, kt, TN)) * p[n - 1:n, :],
                  s, sa, b_t, v, k_t, p_incl)
    return tuple(s_new), tuple(y)


RWKV_TILE = 512
RWKV_HEADS = 4


def _rwkv_specs(nh, t, reverse):
    tile = min(RWKV_TILE, t)
    hb = RWKV_HEADS if nh % RWKV_HEADS == 0 else nh
    nt = t // tile
    per = tile // RWKV_CHUNK
    pos = (lambda h, i: (nt - 1 - i, h)) if reverse else (lambda h, i: (i, h))
    pos4 = (lambda h, i: (h, nt - 1 - i, 0, 0)) if reverse else (lambda h, i: (h, i, 0, 0))
    return (hb, nt, per, pl.BlockSpec((tile, hb * HEAD_DIM), pos), pl.BlockSpec((hb, per, HEAD_DIM, HEAD_DIM), pos4))


def _hosted_steps(exchange, refs_in, refs_out, sems, grid, forward_share):
    if exchange is None:
        return
    start, forward, finish = exchange.plan(refs_in, refs_out, *sems)
    step, n_steps = 0, 1
    for axis, size in enumerate(grid):
        step = step * size + pl.program_id(axis)
        n_steps *= size
    pl.when(step == 0)(start)
    pl.when(step == int(forward_share * (n_steps - 1)))(forward)
    pl.when(step == n_steps - 1)(finish)


def _hosted_call_args(exchange):
    if exchange is None:
        return [], [], [], []
    nb = len(exchange.arrays)
    return exchange.arrays, [HBM_SPEC] * nb, exchange.out_shape, exchange.scratch()


def rwkv_fwd(r, k, v, w, a, b, hosted=None):
    t, width = r.shape
    nh = width // HEAD_DIM
    hb, nt, per, row_spec, s_spec = _rwkv_specs(nh, t, False)
    h_arrays, h_specs, h_shapes, h_scratch = _hosted_call_args(hosted)
    nhosted = len(h_arrays)
    n_steps = (nh // hb) * nt

    def body(r_ref, k_ref, v_ref, w_ref, a_ref, b_ref, *rest):
        h_in, (y_ref, s_ref), h_out = rest[:nhosted], rest[nhosted:nhosted + 2], rest[nhosted + 2:2 * nhosted + 2]
        state, sems = rest[2 * nhosted + 2], rest[2 * nhosted + 3:]
        _hosted_steps(hosted, h_in, h_out, sems, (nh // hb, nt), 0.75)

        @pl.when(pl.program_id(1) == 0)
        def _():
            state[...] = jnp.zeros_like(state)

        def step(c, states):
            rows = pl.ds(pl.multiple_of(c * RWKV_CHUNK, RWKV_CHUNK), RWKV_CHUNK)
            for h in range(hb):
                s_ref[h, c] = states[h]
            s_new, y = _rwkv_chunk(states, *(tuple(_split_heads(ref[rows, :], hb))
                                             for ref in (r_ref, k_ref, v_ref, w_ref, a_ref, b_ref)))
            y_ref[rows, :] = jnp.concatenate(y, axis=1)
            return s_new

        final = lax.fori_loop(0, per, step, tuple(state[h] for h in range(hb)))
        for h in range(hb):
            state[h] = final[h]

    return pl.pallas_call(
        body, name="rwkv_fwd", grid=(nh // hb, nt), in_specs=[row_spec] * 6 + h_specs,
        out_specs=[row_spec, s_spec] + h_specs,
        out_shape=[jax.ShapeDtypeStruct((t, width), F32),
                   jax.ShapeDtypeStruct((nh, t // RWKV_CHUNK, HEAD_DIM, HEAD_DIM), F32)] + h_shapes,
        scratch_shapes=[pltpu.VMEM((hb, HEAD_DIM, HEAD_DIM), F32)] + h_scratch,
        compiler_params=_params(("arbitrary", "arbitrary")),
    )(r, k, v, w, a, b, *h_arrays)


def rwkv_bwd(r, k, v, w, a, b, states, dy, hosted=None):
    t, width = r.shape
    nh = width // HEAD_DIM
    hb, nt, per, row_spec, s_spec = _rwkv_specs(nh, t, True)
    h_arrays, h_specs, h_shapes, h_scratch = _hosted_call_args(hosted)
    nhosted = len(h_arrays)
    n_steps = (nh // hb) * nt

    def body(r_ref, k_ref, v_ref, w_ref, a_ref, b_ref, s_ref, dy_ref, *rest):
        h_in, d_refs, h_out = rest[:nhosted], rest[nhosted:nhosted + 6], rest[nhosted + 6:2 * nhosted + 6]
        dstate, sems = rest[2 * nhosted + 6], rest[2 * nhosted + 7:]
        _hosted_steps(hosted, h_in, h_out, sems, (nh // hb, nt), 0.5)

        @pl.when(pl.program_id(1) == 0)
        def _():
            dstate[...] = jnp.zeros_like(dstate)

        def step(i, ds):
            c = per - 1 - i
            rows = pl.ds(pl.multiple_of(c * RWKV_CHUNK, RWKV_CHUNK), RWKV_CHUNK)
            _, vjp = jax.vjp(_rwkv_chunk, tuple(s_ref[h, c] for h in range(hb)),
                             *(tuple(_split_heads(ref[rows, :], hb))
                               for ref in (r_ref, k_ref, v_ref, w_ref, a_ref, b_ref)))
            grads = vjp((ds, tuple(_split_heads(dy_ref[rows, :], hb))))
            for d_ref, g in zip(d_refs, grads[1:]):
                d_ref[rows, :] = jnp.concatenate(g, axis=1)
            return grads[0]

        final = lax.fori_loop(0, per, step, tuple(dstate[h] for h in range(hb)))
        for h in range(hb):
            dstate[h] = final[h]

    hs = jax.ShapeDtypeStruct((t, width), F32)
    return pl.pallas_call(
        body, name="rwkv_bwd", grid=(nh // hb, nt), in_specs=[row_spec] * 6 + [s_spec, row_spec] + h_specs,
        out_specs=[row_spec] * 6 + h_specs, out_shape=[hs] * 6 + h_shapes,
        scratch_shapes=[pltpu.VMEM((hb, HEAD_DIM, HEAD_DIM), F32)] + h_scratch,
        compiler_params=_params(("arbitrary", "arbitrary")),
    )(r, k, v, w, a, b, states, dy, *h_arrays)


SUBLANES = 8


def s5_scan(bu_re, bu_im, a_re, a_im, *, reverse=False, h_prev=None):
    t, n = bu_re.shape
    tt = _pick(t, (512, 256, 128, 64, 32, 16, 8))
    tc = _pick(n, (1024, 512, 256, 128))
    nt = t // tt
    with_da = h_prev is not None
    sign = -1.0 if reverse else 1.0

    def body(*refs):
        if with_da:
            br, bi, ar_ref, ai_ref, pr, pi, hr_ref, hi_ref, dar_ref, dai_ref, sr, si = refs
        else:
            br, bi, ar_ref, ai_ref, hr_ref, hi_ref, sr, si = refs
        ti = pl.program_id(1)

        @pl.when(ti == 0)
        def _():
            sr[...] = jnp.zeros_like(sr)
            si[...] = jnp.zeros_like(si)
            if with_da:
                dar_ref[...] = jnp.zeros_like(dar_ref)
                dai_ref[...] = jnp.zeros_like(dai_ref)

        ar = ar_ref[...]
        ai = ai_ref[...] * sign

        def group(gi, carry):
            hr, hi, dar, dai = carry
            g0 = pl.multiple_of((tt // SUBLANES - 1 - gi if reverse else gi) * SUBLANES, SUBLANES)
            rows = pl.ds(g0, SUBLANES)
            xr, xi = br[rows, :], bi[rows, :]
            if with_da:
                qr, qi = pr[rows, :], pi[rows, :]
            out_r, out_i = [None] * SUBLANES, [None] * SUBLANES
            for s in (range(SUBLANES - 1, -1, -1) if reverse else range(SUBLANES)):
                hr, hi = ar * hr - ai * hi + xr[s:s + 1, :], ar * hi + ai * hr + xi[s:s + 1, :]
                out_r[s], out_i[s] = hr, hi
                if with_da:
                    dar = dar + hr * qr[s:s + 1, :] + hi * qi[s:s + 1, :]
                    dai = dai + hi * qr[s:s + 1, :] - hr * qi[s:s + 1, :]
            hr_ref[rows, :] = jnp.concatenate(out_r, axis=0)
            hi_ref[rows, :] = jnp.concatenate(out_i, axis=0)
            return hr, hi, dar, dai

        zero = jnp.zeros((1, tc), F32)
        hr, hi, dar, dai = lax.fori_loop(0, tt // SUBLANES, group, (sr[...], si[...], zero, zero))
        sr[...] = hr
        si[...] = hi
        if with_da:
            dar_ref[...] += dar
            dai_ref[...] += dai

    tile = pl.BlockSpec((tt, tc), (lambda ci, ti: (nt - 1 - ti, ci)) if reverse else (lambda ci, ti: (ti, ci)))
    col = pl.BlockSpec((1, tc), lambda ci, ti: (0, ci))
    hs = jax.ShapeDtypeStruct((t, n), F32)
    cs = jax.ShapeDtypeStruct((1, n), F32)
    ins = [bu_re, bu_im, a_re, a_im] + (list(h_prev) if with_da else [])
    return pl.pallas_call(
        body, name="s5_scan_bwd" if reverse else "s5_scan_fwd", grid=(n // tc, nt),
        in_specs=[tile, tile, col, col] + ([tile, tile] if with_da else []),
        out_specs=[tile, tile] + ([col, col] if with_da else []), out_shape=[hs, hs] + ([cs, cs] if with_da else []),
        scratch_shapes=[pltpu.VMEM((1, tc), F32)] * 2, compiler_params=_params(("parallel", "arbitrary")),
    )(*ins)


N_CHIPS = 4
HBM_SPEC = pl.BlockSpec(memory_space=pl.ANY)


def _remote(src, dst, send_sem, recv_sem, device):
    return pltpu.make_async_remote_copy(src_ref=src, dst_ref=dst, send_sem=send_sem, recv_sem=recv_sem,
                                        device_id=device, device_id_type=pl.DeviceIdType.MESH)


class Exchange:
    def __init__(self, arrays, out_shape, n_sems, plan):
        self.arrays, self.out_shape, self.n_sems, self.plan = list(arrays), out_shape, n_sems, plan

    def scratch(self):
        nb = len(self.arrays)
        return [pltpu.SemaphoreType.DMA((nb, self.n_sems)), pltpu.SemaphoreType.DMA((nb, self.n_sems)),
                pltpu.SemaphoreType.DMA((nb,))]

    def run(self, name):
        nb = len(self.arrays)

        def body(*refs):
            for step in self.plan(refs[:nb], refs[nb:2 * nb], *refs[2 * nb:]):
                step()

        return pl.pallas_call(
            body, name=name, in_specs=[HBM_SPEC] * nb, out_specs=[HBM_SPEC] * nb, out_shape=self.out_shape,
            scratch_shapes=self.scratch(), compiler_params=pltpu.CompilerParams(has_side_effects=True),
        )(*self.arrays)


def gather_all(arrays):
    nb = len(arrays)

    def plan(ins, outs, send_sems, recv_sems, local_sems):
        x, y, c = (lax.axis_index(n) for n in MESH_AXES)
        me, sibling = 4 * x + 2 * y + c, 4 * x + 2 * y + 1 - c
        chips = [(1 - x, y), (x, 1 - y), (1 - x, 1 - y)]

        def copy(b, k, slot, to, src=None):
            block = outs[b].at[slot]
            return _remote(block if src is None else src, block, send_sems.at[b, k], recv_sems.at[b, k], to)

        def local():
            return [pltpu.make_async_copy(ins[b], outs[b].at[me], local_sems.at[b]) for b in range(nb)]

        def first():
            return [cp for b in range(nb) for cp in
                    [copy(b, 0, me, (x, y, 1 - c), src=ins[b])]
                    + [copy(b, 1 + j, me, (px, py, c), src=ins[b]) for j, (px, py) in enumerate(chips)]]

        def passed():
            return [copy(b, 4 + j, 4 * px + 2 * py + c, (x, y, 1 - c)) for j, (px, py) in enumerate(chips) for b in range(nb)]

        def start():
            for cp in local() + first():
                cp.start()

        def forward():
            for j, (px, py) in enumerate(chips):
                for b in range(nb):
                    copy(b, 1 + j, 4 * px + 2 * py + c, (px, py, c)).wait_recv()
                    copy(b, 4 + j, 4 * px + 2 * py + c, (x, y, 1 - c)).start()

        def finish():
            for b in range(nb):
                copy(b, 0, sibling, (x, y, 1 - c)).wait_recv()
                for j, (px, py) in enumerate(chips):
                    copy(b, 4 + j, 4 * px + 2 * py + 1 - c, (x, y, 1 - c)).wait_recv()
            for cp in first() + passed():
                cp.wait_send()
            for cp in local():
                cp.wait()

        return start, forward, finish

    out_shape = [jax.ShapeDtypeStruct((N_DEV,) + tuple(a.shape), a.dtype) for a in arrays]
    return Exchange(arrays, out_shape, N_DEV - 1, plan)


def _nothing():
    pass


def pair_exchange(arrays):
    nb = len(arrays)

    def plan(ins, outs, send_sems, recv_sems, local_sems):
        x, y, c = (lax.axis_index(n) for n in MESH_AXES)

        def copies():
            return [_remote(ins[b].at[2 * z + 1 - c], outs[b].at[z], send_sems.at[b, z], recv_sems.at[b, z], (x, y, 1 - c))
                    for b in range(nb) for z in range(N_CHIPS)]

        def start():
            for cp in copies():
                cp.start()

        def finish():
            for cp in copies():
                cp.wait_send()
            for cp in copies():
                cp.wait_recv()

        return start, _nothing, finish

    out_shape = [jax.ShapeDtypeStruct((N_CHIPS,) + tuple(a.shape[1:]), a.dtype) for a in arrays]
    return Exchange(arrays, out_shape, N_CHIPS, plan)


def pair_sum(name, pieces, from_sibling):
    _, r, c = pieces.shape
    tr = _row_tile(r, c * pieces.dtype.itemsize, 1 << 20)

    def body(p_ref, s_ref, o_ref):
        mine = p_ref[lax.axis_index("c")]
        o_ref[...] = (mine.astype(F32) + s_ref[...].astype(F32)).astype(o_ref.dtype)

    return pl.pallas_call(
        body, name=name, grid=(N_CHIPS, r // tr),
        in_specs=[pl.BlockSpec((None, 2, tr, c), lambda z, i: (z, 0, i, 0)), pl.BlockSpec((None, tr, c), lambda z, i: (z, i, 0))],
        out_specs=pl.BlockSpec((None, tr, c), lambda z, i: (z, i, 0)),
        out_shape=jax.ShapeDtypeStruct((N_CHIPS, r, c), pieces.dtype), compiler_params=_params(("parallel", "parallel")),
    )(pieces.reshape(N_CHIPS, 2, r, c), from_sibling)


def chip_exchange(arrays):
    nb = len(arrays)

    def plan(ins, outs, send_sems, recv_sems, local_sems):
        x, y, c = (lax.axis_index(n) for n in MESH_AXES)
        my_chip = 2 * x + y
        chips = [(1 - x, y), (x, 1 - y), (1 - x, 1 - y)]

        def local():
            return [pltpu.make_async_copy(ins[b].at[my_chip], outs[b].at[my_chip], local_sems.at[b]) for b in range(nb)]

        def copies():
            return [_remote(ins[b].at[2 * px + py], outs[b].at[my_chip], send_sems.at[b, j], recv_sems.at[b, j], (px, py, c))
                    for b in range(nb) for j, (px, py) in enumerate(chips)]

        def start():
            for cp in local() + copies():
                cp.start()

        def finish():
            for cp in copies():
                cp.wait_send()
            for cp in copies():
                cp.wait_recv()
            for cp in local():
                cp.wait()

        return start, _nothing, finish

    out_shape = [jax.ShapeDtypeStruct(a.shape, a.dtype) for a in arrays]
    return Exchange(arrays, out_shape, N_CHIPS - 1, plan)


def adam_update(name, pieces, w, m, v):
    r, c = w.shape
    n_pieces = pieces.shape[0]
    tr = _row_tile(r, 4 * c, 1 << 19)

    def body(p_ref, w_ref, m_ref, v_ref, g_ref, d_ref, mo_ref, vo_ref):
        g = p_ref[0].astype(F32)
        for j in range(1, n_pieces):
            g = g + p_ref[j].astype(F32)
        m_new = ADAM_B1 * m_ref[...] + (1.0 - ADAM_B1) * g
        v_new = ADAM_B2 * v_ref[...] + (1.0 - ADAM_B2) * (g * g)
        m_hat = m_new / (1.0 - ADAM_B1 ** ADAM_STEP)
        v_hat = v_new / (1.0 - ADAM_B2 ** ADAM_STEP)
        g_ref[...] = g
        d_ref[...] = -ADAM_LR * (m_hat / (jnp.sqrt(v_hat) + ADAM_EPS) + ADAM_WD * w_ref[...])
        mo_ref[...] = m_new
        vo_ref[...] = v_new

    row = pl.BlockSpec((tr, c), lambda i: (i, 0))
    out = jax.ShapeDtypeStruct((r, c), F32)
    return pl.pallas_call(
        body, name=name, grid=(r // tr,), in_specs=[pl.BlockSpec((n_pieces, tr, c), lambda i: (0, i, 0)), row, row, row],
        out_specs=[row] * 4, out_shape=[out] * 4, compiler_params=_params(("parallel",)),
    )(pieces, w, m, v)


PACK_WIDTH = 1024
PACK_ROWS = 64


def _pack(arrays, dtype, lead=0):
    parts = []
    for a in arrays:
        head = a.shape[:lead]
        f = a.reshape(head + (-1,)).astype(dtype)
        pad = (-f.shape[-1]) % PACK_WIDTH
        if pad:
            f = jnp.pad(f, [(0, 0)] * lead + [(0, pad)])
        parts.append(f.reshape(head + (-1, PACK_WIDTH)))
    out = jnp.concatenate(parts, axis=lead)
    pad = (-out.shape[lead]) % PACK_ROWS
    if pad:
        out = jnp.pad(out, [(0, 0)] * lead + [(0, pad), (0, 0)])
    return out


def _unpack(packed, shapes, lead=0):
    head = packed.shape[:lead]
    out, row = [], 0
    for s in shapes:
        n = int(np.prod(s))
        rows = -(-n // PACK_WIDTH)
        chunk = lax.slice_in_dim(packed, row, row + rows, axis=lead).reshape(head + (-1,))
        out.append(lax.slice_in_dim(chunk, 0, n, axis=lead).reshape(head + tuple(s)))
        row += rows
    return out


def _to_natural(stacked, kind):
    if kind == "col":
        m = jnp.moveaxis(stacked, 0, -2)
        return m.reshape(m.shape[:-2] + (m.shape[-2] * m.shape[-1],))
    m = jnp.moveaxis(stacked, 0, -3)
    return m.reshape(m.shape[:-3] + (m.shape[-3] * m.shape[-2], m.shape[-1]))


def _to_stacked(natural, kind):
    if kind == "col":
        m = natural.reshape(natural.shape[:-1] + (N_DEV, natural.shape[-1] // N_DEV))
        return jnp.moveaxis(m, -2, 0)
    m = natural.reshape(natural.shape[:-2] + (N_DEV, natural.shape[-2] // N_DEV, natural.shape[-1]))
    return jnp.moveaxis(m, -3, 0)


def _f_rms(h, g):
    return (_rms(h, g).astype(BF16),)


def _f_swiglu(g, u):
    g, u = g.astype(F32), u.astype(F32)
    return ((g * _sigmoid(g) * u).astype(BF16),)


def _f_ple(h, pre, pp):
    return (h + _sigmoid(pre) * pp,)


def _head_rms(x, g):
    ms = _seg_sum(x * x, HEAD_DIM) * (1.0 / HEAD_DIM)
    return x * _seg_expand(lax.rsqrt(ms + RMS_EPS), HEAD_DIM) * _tile_lanes(g, x.shape[1] // HEAD_DIM)


def _f_attpre(q, k, q_gain, k_gain):
    return _head_rms(q, q_gain) * (HEAD_DIM ** -0.5), _head_rms(k, k_gain)


def _f_shift(z, z_prev, mu):
    return (z + (z_prev - z) * mu,)


def _f_rwkvpre(r, k, v, xw, xa, xg, w0, w_up, a0, a_up, g_up, k_k, k_a):
    del r, v
    w_log = -_softplus(-(w0 + _mm(jnp.tanh(xw), w_up))) - 0.5
    decay = jnp.exp(-jnp.exp(w_log))
    a = _sigmoid(a0 + _mm(xa, a_up))
    g = _mm(_sigmoid(xg), g_up)
    kk = k * k_k
    norm = jnp.sqrt(_seg_expand(_seg_sum(kk * kk, HEAD_DIM), HEAD_DIM))
    kk = kk / jnp.maximum(norm, 1e-12)
    return k * (1.0 + (a - 1.0) * k_a), decay, -kk, kk * a, g


def _f_rwkvpost(y, r, k, v, g, lnx_w, lnx_b, r_k):
    mean = _seg_expand(_seg_sum(y, HEAD_DIM) * (1.0 / HEAD_DIM), HEAD_DIM)
    yc = y - mean
    var = _seg_expand(_seg_sum(yc * yc, HEAD_DIM) * (1.0 / HEAD_DIM), HEAD_DIM)
    yn = yc * lax.rsqrt(var + GN_EPS) * lnx_w + lnx_b
    bonus = _seg_expand(_seg_sum(r * k * r_k, HEAD_DIM), HEAD_DIM) * v
    return ((yn + bonus) * g,)


def _f_s5disc(lam_re, lam_im, log_dt):
    dt = jnp.exp(log_dt)
    mag = jnp.exp(lam_re * dt)
    ab_re, ab_im = mag * jnp.cos(lam_im * dt), mag * jnp.sin(lam_im * dt)
    denom = lam_re * lam_re + lam_im * lam_im
    z_re = ((ab_re - 1.0) * lam_re + ab_im * lam_im) / denom
    z_im = (ab_im * lam_re - (ab_re - 1.0) * lam_im) / denom
    return ab_re, ab_im, z_re, z_im


def _f_s5b(z_re, z_im, b_re, b_im):
    return z_re * b_re - z_im * b_im, z_re * b_im + z_im * b_re


def _f_s5post(ypre, u, d_skip):
    return (_gelu_tanh(ypre + d_skip * u).astype(BF16),)


def _f_glu(h, z1, z2):
    return (h + z1 * _sigmoid(z2),)


def _shift_down(x):
    return jnp.pad(x[:-1], ((1, 0), (0, 0)))


def _shift_up(x):
    return jnp.pad(x[1:], ((0, 1), (0, 0)))


def _block_diag(blocks):
    g, a, b = blocks.shape
    eye = jnp.eye(g, dtype=blocks.dtype)
    return (blocks[:, :, None, :] * eye[:, None, :, None]).reshape(g * a, g * b)


def _diag_blocks(dense, g):
    a, b = dense.shape[0] // g, dense.shape[1] // g
    return jnp.stack([dense[i * a:(i + 1) * a, i * b:(i + 1) * b] for i in range(g)], axis=0)


def _row_tile(t, width_bytes, budget=2 * 1024 * 1024):
    for tr in (512, 256, 128, 64, 32, 16, 8):
        if t % tr == 0 and tr * width_bytes <= budget:
            return tr
    return t


def _no_hosting(name):
    return None, None


def _hosting_matmul(hosting, name, a, b, **kw):
    exchange, done = hosting(name)
    if exchange is None:
        return matmul(name, a, b, **kw)
    out, travelled = matmul(name, a, b, hosted=exchange, **kw)
    done(travelled)
    return out


def _ffn_fwd(tag, h, norm_w, weight, hosting=_no_hosting):
    t, d = h.shape
    n = rowwise(f"{tag}_norm", _f_rms, [h], [norm_w], [(d, BF16)], _row_tile(t, 4 * d))[0]
    g = _hosting_matmul(hosting, f"{tag}_gate", n, weight("gate"), out_dtype=BF16)
    u = _hosting_matmul(hosting, f"{tag}_up", n, weight("up"), out_dtype=BF16)
    f = g.shape[1]
    a = rowwise(f"{tag}_act", _f_swiglu, [g, u], [], [(f, BF16)], _row_tile(t, 4 * f))[0]
    return _hosting_matmul(hosting, f"{tag}_down", a, weight("down"), alpha=0.5, res=h), (h, n, g, u, a)


def _ffn_bwd(tag, dh2, saved, norm_w, w_gate, w_up, w_down):
    h, n, g, u, a = saved
    t, d = h.shape
    f = w_gate.shape[1]
    da = matmul(f"{tag}_da", dh2, w_down, tb=True, alpha=0.5, out_dtype=BF16)
    d_down = matmul(f"{tag}_dwdown", a, dh2, ta=True, alpha=0.5, out_dtype=BF16)
    dg, du = rowwise_vjp(f"{tag}_dact", _f_swiglu, [g, u], [], [da], _row_tile(t, 4 * f, 1 << 20), [True, True], [],
                         grad_dtypes={0: BF16, 1: BF16})
    dn = matmul(f"{tag}_dn_gate", dg, w_gate, tb=True)
    dn = matmul(f"{tag}_dn_up", du, w_up, tb=True, res=dn)
    d_gate = matmul(f"{tag}_dwgate", n, dg, ta=True, out_dtype=BF16)
    d_up = matmul(f"{tag}_dwup", n, du, ta=True, out_dtype=BF16)
    dh, d_norm = rowwise_vjp(f"{tag}_dnorm", _f_rms, [h], [norm_w], [dn], _row_tile(t, 4 * d), [True], [True],
                             add_to={0: dh2})
    return dh, d_norm, d_gate, d_up, d_down


def _ple_fwd(tag, h, norm_w, w_gate, w_proj, p_i, hosting=_no_hosting):
    t, d = h.shape
    n = rowwise(f"{tag}_norm", _f_rms, [h], [norm_w], [(d, BF16)], _row_tile(t, 4 * d))[0]
    pre = _hosting_matmul(hosting, f"{tag}_gate", n, w_gate)
    pp = matmul(f"{tag}_proj", p_i, w_proj)
    h2 = rowwise(f"{tag}_out", _f_ple, [h, pre, pp], [], [(d, F32)], _row_tile(t, 4 * d))[0]
    return h2, (h, n, pre, pp)


def _ple_bwd(tag, dh2, saved, norm_w, w_gate, w_proj, p_i):
    h, n, pre, pp = saved
    t, d = h.shape
    dpre, dpp = rowwise_vjp(f"{tag}_dout", _f_ple, [h, pre, pp], [], [dh2], _row_tile(t, 4 * d), [False, True, True], [],
                            grad_dtypes={1: BF16, 2: BF16})
    d_proj = matmul(f"{tag}_dwproj", p_i, dpp, ta=True, out_dtype=BF16)
    d_gate = matmul(f"{tag}_dwgate", n, dpre, ta=True, out_dtype=BF16)
    dn = matmul(f"{tag}_dn", dpre, w_gate, tb=True)
    dh, d_norm = rowwise_vjp(f"{tag}_dnorm", _f_rms, [h], [norm_w], [dn], _row_tile(t, 4 * d), [True], [True],
                             add_to={0: dh2})
    return dh, d_norm, d_gate, d_proj


def _ab_fwd(h, w, hosting=_no_hosting):
    t, d = h.shape
    da = d // 2
    n = rowwise("ab_norm", _f_rms, [h], [w["mix_norm"]], [(d, BF16)], _row_tile(t, 4 * d))[0]
    proj = _hosting_matmul(hosting, "ab_in", n, w["ab_w_in"])
    q_raw, k_raw, v_att, z = proj[:, :da], proj[:, da:2 * da], proj[:, 2 * da:3 * da], proj[:, 3 * da:]
    tr = _row_tile(t, 4 * da, 1 << 19)
    qn, kn = rowwise("att_pre", _f_attpre, [q_raw, k_raw], [w["att_q_gain"], w["att_k_gain"]], [(da, F32)] * 2, tr)
    bias = jnp.transpose(bias_expand(w["att_rel_bias"]), (1, 0, 2))
    qh, kh, vh = qn, kn, v_att
    exchange, done = hosting("attention_fwd")
    att = attention_fwd(qh, kh, vh, bias, hosted=exchange)
    if exchange is not None:
        att, travelled = att
        done(travelled)
    z_prev = _shift_down(z)
    zz = rowwise("rwkv_shift", _f_shift, [z, z_prev], [w["rwkv_mu"]], [(z.shape[1], F32)], _row_tile(t, 4 * z.shape[1]))[0]
    o = [0, da, 2 * da, 3 * da, 3 * da + DECAY_LORA, 3 * da + DECAY_LORA + AAA_LORA, z.shape[1]]
    r, k, v, xw, xa, xg = (zz[:, o[i]:o[i + 1]] for i in range(6))
    pre_consts = [w[nm] for nm in ("rwkv_w0", "rwkv_w_up", "rwkv_a0", "rwkv_a_up", "rwkv_g_up", "rwkv_k_k", "rwkv_k_a")]
    k2, decay, ia, ib, g = rowwise("rwkv_pre", _f_rwkvpre, [r, k, v, xw, xa, xg], pre_consts, [(da, F32)] * 5, tr)
    heads = [r, k2, v, decay, ia, ib]
    exchange, done = hosting("rwkv_fwd")
    y, states, *travelled = rwkv_fwd(*heads, hosted=exchange)
    if exchange is not None:
        done(travelled)
    post_consts = [w["rwkv_lnx_w"], w["rwkv_lnx_b"], w["rwkv_r_k"]]
    rw = rowwise("rwkv_post", _f_rwkvpost, [y, r, k2, v, g], post_consts, [(da, F32)], tr)[0]
    cat = jnp.concatenate([att, rw], axis=1).astype(BF16)
    h2 = _hosting_matmul(hosting, "ab_out", cat, w["ab_w_out"], res=h)
    saved = dict(h=h, n=n, q_raw=q_raw, k_raw=k_raw, qh=qh, kh=kh, vh=vh, bias=bias, z=z, z_prev=z_prev,
                 rows=(r, k, v, xw, xa, xg), pre_consts=pre_consts, post=(y, r, k2, v, g), post_consts=post_consts,
                 heads=heads, states=states, cat=cat, tr=tr)
    return h2, saved


def _ab_bwd(dh2, s, w, make_hosted=None, hosted_done=None):
    h = s["h"]
    t, d = h.shape
    da = d // 2
    tr = s["tr"]
    grads = {}
    dcat = matmul("ab_dcat", dh2, w["ab_w_out"], tb=True)
    grads["ab_w_out"] = matmul("ab_dwout", s["cat"], dh2, ta=True, out_dtype=BF16)
    d_att, d_rw = dcat[:, :da], dcat[:, da:]
    dy, dr1, dk1, dv1, dg, grads["rwkv_lnx_w"], grads["rwkv_lnx_b"], grads["rwkv_r_k"] = rowwise_vjp(
        "rwkv_dpost", _f_rwkvpost, list(s["post"]), s["post_consts"], [d_rw], tr, [True] * 5, [True] * 3)
    hosted = None if make_hosted is None else make_hosted(grads["ab_w_out"])
    *d_heads, = rwkv_bwd(*s["heads"], s["states"], dy, hosted=hosted)
    if hosted is not None:
        hosted_done(d_heads[6:])
        d_heads = d_heads[:6]
    dr2, dk2, dv2, ddecay, dia, dib = d_heads
    pre = rowwise_vjp("rwkv_dpre", _f_rwkvpre, list(s["rows"]), s["pre_consts"], [(dk1, dk2), ddecay, dia, dib, dg], tr,
                      [True] * 6, [True] * 7, add_to={0: (dr1, dr2), 2: (dv1, dv2)})
    for nm, g in zip(("rwkv_w0", "rwkv_w_up", "rwkv_a0", "rwkv_a_up", "rwkv_g_up", "rwkv_k_k", "rwkv_k_a"), pre[6:]):
        grads[nm] = g
    dzz = jnp.concatenate(pre[:6], axis=1)
    trz = _row_tile(t, 4 * dzz.shape[1])
    grads["rwkv_mu"] = rowwise_vjp("rwkv_dmu", _f_shift, [s["z"], s["z_prev"]], [w["rwkv_mu"]], [dzz], trz,
                                   [False, False], [True])[0]
    dz = rowwise("rwkv_dshift", _f_shift, [dzz, _shift_up(dzz)], [w["rwkv_mu"]], [(dzz.shape[1], F32)], trz)[0]
    dqh, dkh, dvh, dbias = attention_bwd(s["qh"], s["kh"], s["vh"], s["bias"], d_att)
    grads["att_rel_bias"] = bias_reduce(jnp.transpose(dbias, (1, 0, 2)))
    dq_raw, dk_raw, grads["att_q_gain"], grads["att_k_gain"] = rowwise_vjp(
        "att_dpre", _f_attpre, [s["q_raw"], s["k_raw"]], [w["att_q_gain"], w["att_k_gain"]],
        [dqh, dkh], tr, [True, True], [True, True])
    dproj = jnp.concatenate([dq_raw, dk_raw, dvh, dz], axis=1).astype(BF16)
    dn = matmul("ab_dn", dproj, w["ab_w_in"], tb=True)
    grads["ab_w_in"] = matmul("ab_dwin", s["n"], dproj, ta=True, out_dtype=BF16)
    dh, grads["mix_norm"] = rowwise_vjp("ab_dnorm", _f_rms, [h], [w["mix_norm"]], [dn], _row_tile(t, 4 * d), [True], [True],
                                        add_to={0: dh2})
    return dh, grads


def _s5_fwd(h, w):
    t, d = h.shape
    n_groups, n_state = w["ssm_lambda_re"].shape
    gp = n_groups * n_state
    n = rowwise("s5_norm", _f_rms, [h], [w["mix_norm"]], [(d, BF16)], _row_tile(t, 4 * d))[0]
    u = matmul("s5_in", n, w["ssm_w_in"])
    disc_rows = [w["ssm_lambda_re"], w["ssm_lambda_im"], w["ssm_log_dt"]]
    ab_re, ab_im, z_re, z_im = rowwise("s5_disc", _f_s5disc, disc_rows, [], [(n_state, F32)] * 4, n_groups)
    b_rows = [z_re.reshape(gp, 1), z_im.reshape(gp, 1), w["ssm_b_re"], w["ssm_b_im"]]
    trb = _row_tile(gp, 512)
    bb_re, bb_im = rowwise("s5_bbar", _f_s5b, b_rows, [], [(SSM_GROUP, F32)] * 2, trb)
    to_dense = lambda bb: _block_diag(jnp.transpose(bb.reshape(n_groups, n_state, SSM_GROUP), (0, 2, 1)))
    bd_re, bd_im = to_dense(bb_re), to_dense(bb_im)
    cd_re = _block_diag(jnp.transpose(w["ssm_c_re"], (0, 2, 1)))
    cd_im = -_block_diag(jnp.transpose(w["ssm_c_im"], (0, 2, 1)))
    a_re, a_im = ab_re.reshape(1, gp), ab_im.reshape(1, gp)
    bu_re = matmul("s5_bu_re", u, bd_re)
    bu_im = matmul("s5_bu_im", u, bd_im)
    h_re, h_im = s5_scan(bu_re, bu_im, a_re, a_im)
    ypre = matmul("s5_y_re", h_re, cd_re)
    ypre = matmul("s5_y_im", h_im, cd_im, res=ypre)
    tru = _row_tile(t, 4 * u.shape[1])
    yg = rowwise("s5_post", _f_s5post, [ypre, u], [w["ssm_d"]], [(u.shape[1], BF16)], tru)[0]
    w_out1, w_out2 = w["ssm_w_out"][:, :d], w["ssm_w_out"][:, d:]
    z1 = matmul("s5_out1", yg, w_out1)
    z2 = matmul("s5_out2", yg, w_out2)
    h2 = rowwise("s5_glu", _f_glu, [h, z1, z2], [], [(d, F32)], _row_tile(t, 4 * d))[0]
    saved = dict(h=h, n=n, u=u, disc_rows=disc_rows, b_rows=b_rows, trb=trb, bd=(bd_re, bd_im), cd=(cd_re, cd_im),
                 a=(a_re, a_im), hs=(h_re, h_im), ypre=ypre, yg=yg, w_out=(w_out1, w_out2), z=(z1, z2), tru=tru)
    return h2, saved


def _s5_bwd(dh2, s, w):
    h, u = s["h"], s["u"]
    t, d = h.shape
    n_groups, n_state = w["ssm_lambda_re"].shape
    gp = n_groups * n_state
    grads = {}
    z1, z2 = s["z"]
    w_out1, w_out2 = s["w_out"]
    dz1, dz2 = rowwise_vjp("s5_dglu", _f_glu, [h, z1, z2], [], [dh2], _row_tile(t, 4 * d), [False, True, True], [],
                           grad_dtypes={1: BF16, 2: BF16})
    dyg = matmul("s5_dyg1", dz1, w_out1, tb=True)
    dyg = matmul("s5_dyg2", dz2, w_out2, tb=True, res=dyg)
    grads["ssm_w_out"] = jnp.concatenate([matmul("s5_dwout1", s["yg"], dz1, ta=True, out_dtype=BF16),
                                          matmul("s5_dwout2", s["yg"], dz2, ta=True, out_dtype=BF16)], axis=1)
    dypre, du1, grads["ssm_d"] = rowwise_vjp("s5_dpost", _f_s5post, [s["ypre"], u], [w["ssm_d"]], [dyg], s["tru"],
                                             [True, True], [True])
    cd_re, cd_im = s["cd"]
    h_re, h_im = s["hs"]
    dh_re = matmul("s5_dh_re", dypre, cd_re, tb=True)
    dh_im = matmul("s5_dh_im", dypre, cd_im, tb=True)
    dcd_re = matmul("s5_dc_re", h_re, dypre, ta=True)
    dcd_im = matmul("s5_dc_im", h_im, dypre, ta=True)
    a_re, a_im = s["a"]
    g_re, g_im, da_re, da_im = s5_scan(dh_re, dh_im, a_re, a_im, reverse=True,
                                       h_prev=(_shift_down(h_re), _shift_down(h_im)))
    bd_re, bd_im = s["bd"]
    du = matmul("s5_du_re", g_re, bd_re, tb=True, res=du1)
    du = matmul("s5_du_im", g_im, bd_im, tb=True, res=du)
    dbd_re = matmul("s5_db_re", u, g_re, ta=True)
    dbd_im = matmul("s5_db_im", u, g_im, ta=True)
    grads["ssm_w_in"] = matmul("s5_dwin", s["n"], du, ta=True, out_dtype=BF16)
    dn = matmul("s5_dn", du, w["ssm_w_in"], tb=True)
    dh, grads["mix_norm"] = rowwise_vjp("s5_dnorm", _f_rms, [h], [w["mix_norm"]], [dn], _row_tile(t, 4 * d), [True], [True],
                                        add_to={0: dh2})
    from_dense = lambda m: jnp.transpose(_diag_blocks(m, n_groups), (0, 2, 1)).reshape(gp, SSM_GROUP)
    dz_re, dz_im, grads["ssm_b_re"], grads["ssm_b_im"] = rowwise_vjp(
        "s5_dbbar", _f_s5b, s["b_rows"], [], [from_dense(dbd_re), from_dense(dbd_im)], s["trb"], [True] * 4, [])
    disc_cots = [da_re.reshape(n_groups, n_state), da_im.reshape(n_groups, n_state),
                 dz_re.reshape(n_groups, n_state), dz_im.reshape(n_groups, n_state)]
    grads["ssm_lambda_re"], grads["ssm_lambda_im"], grads["ssm_log_dt"] = rowwise_vjp(
        "s5_ddisc", _f_s5disc, s["disc_rows"], [], disc_cots, n_groups, [True] * 3, [])
    grads["ssm_c_re"] = jnp.transpose(_diag_blocks(dcd_re, n_groups), (0, 2, 1))
    grads["ssm_c_im"] = -jnp.transpose(_diag_blocks(dcd_im, n_groups), (0, 2, 1))
    return dh, grads


WEIGHTS = ["ffn1_norm", "ffn1_w_gate", "ffn1_w_up", "ffn1_w_down", "mix_norm", "ffn2_norm", "ffn2_w_gate", "ffn2_w_up",
           "ffn2_w_down", "ple_norm", "ple_w_gate", "ple_w_proj", "ab_w_in", "att_q_gain", "att_k_gain", "att_rel_bias",
           "rwkv_mu", "rwkv_w0", "rwkv_w_up", "rwkv_a0", "rwkv_a_up", "rwkv_g_up", "rwkv_k_k", "rwkv_k_a", "rwkv_r_k",
           "rwkv_lnx_w", "rwkv_lnx_b", "ab_w_out", "ssm_w_in", "ssm_lambda_re", "ssm_lambda_im", "ssm_log_dt", "ssm_b_re",
           "ssm_b_im", "ssm_c_re", "ssm_c_im", "ssm_d", "ssm_w_out"]
BIG = {"ffn1_w_gate": "col", "ffn1_w_up": "col", "ffn1_w_down": "row", "ffn2_w_gate": "col", "ffn2_w_up": "col",
       "ffn2_w_down": "row", "ple_w_gate": "row", "ple_w_proj": "col", "ab_w_in": "col", "ab_w_out": "row",
       "ssm_w_in": "row", "ssm_w_out": "col"}
SMALL_CUT = {"rwkv_w_up": "col", "rwkv_a_up": "col", "rwkv_g_up": "col", "ssm_d": "col"}
REPLICATED = [n for n in WEIGHTS if n not in BIG and n not in SMALL_CUT]


def kernel(x, p, ffn1_norm, ffn1_w_gate, ffn1_w_up, ffn1_w_down, mix_norm, ffn2_norm, ffn2_w_gate, ffn2_w_up, ffn2_w_down, ple_norm, ple_w_gate, ple_w_proj, ab_w_in, att_q_gain, att_k_gain, att_rel_bias, rwkv_mu, rwkv_w0, rwkv_w_up, rwkv_a0, rwkv_a_up, rwkv_g_up, rwkv_k_k, rwkv_k_a, rwkv_r_k, rwkv_lnx_w, rwkv_lnx_b, ab_w_out, ssm_w_in, ssm_lambda_re, ssm_lambda_im, ssm_log_dt, ssm_b_re, ssm_b_im, ssm_c_re, ssm_c_im, ssm_d, ssm_w_out, loss_target, m_ffn1_norm, m_ffn1_w_gate, m_ffn1_w_up, m_ffn1_w_down, m_mix_norm, m_ffn2_norm, m_ffn2_w_gate, m_ffn2_w_up, m_ffn2_w_down, m_ple_norm, m_ple_w_gate, m_ple_w_proj, m_ab_w_in, m_att_q_gain, m_att_k_gain, m_att_rel_bias, m_rwkv_mu, m_rwkv_w0, m_rwkv_w_up, m_rwkv_a0, m_rwkv_a_up, m_rwkv_g_up, m_rwkv_k_k, m_rwkv_k_a, m_rwkv_r_k, m_rwkv_lnx_w, m_rwkv_lnx_b, m_ab_w_out, m_ssm_w_in, m_ssm_lambda_re, m_ssm_lambda_im, m_ssm_log_dt, m_ssm_b_re, m_ssm_b_im, m_ssm_c_re, m_ssm_c_im, m_ssm_d, m_ssm_w_out, v_ffn1_norm, v_ffn1_w_gate, v_ffn1_w_up, v_ffn1_w_down, v_mix_norm, v_ffn2_norm, v_ffn2_w_gate, v_ffn2_w_up, v_ffn2_w_down, v_ple_norm, v_ple_w_gate, v_ple_w_proj, v_ab_w_in, v_att_q_gain, v_att_k_gain, v_att_rel_bias, v_rwkv_mu, v_rwkv_w0, v_rwkv_w_up, v_rwkv_a0, v_rwkv_a_up, v_rwkv_g_up, v_rwkv_k_k, v_rwkv_k_a, v_rwkv_r_k, v_rwkv_lnx_w, v_rwkv_lnx_b, v_ab_w_out, v_ssm_w_in, v_ssm_lambda_re, v_ssm_lambda_im, v_ssm_log_dt, v_ssm_b_re, v_ssm_b_im, v_ssm_c_re, v_ssm_c_im, v_ssm_d, v_ssm_w_out):
    vals = dict(locals())
    depth = ffn1_norm.shape[0]
    n_groups, n_state = ssm_lambda_re.shape[1:]

    kinds = {**BIG, **SMALL_CUT}
    travel = {n: (BF16 if n in BIG else F32) for n in kinds}

    def rows(a, lead=0):
        return a.reshape(a.shape[:lead] + (-1, a.shape[-1]))

    def model_layer(n, j):
        if vals[n].shape[0] == depth:
            return j
        return 2 * j if n.startswith(("ab_", "rwkv_")) else 2 * j + 1

    units = [(n, j) for n in kinds for j in range(vals[n].shape[0])]
    early_names = ("ffn1_w_gate", "ffn1_w_up", "ffn1_w_down", "ab_w_in", "rwkv_w_up", "rwkv_a_up", "rwkv_g_up")
    early = [u for u in units if model_layer(*u) == 0 and u[0] in early_names]
    late = [u for u in units if u not in early]
    full = {}

    def shard(unit):
        n, j = unit
        return rows(vals[n][j]).astype(travel[n])

    def take_gathered(which, got):
        for (n, j), g in zip(which, got):
            full[(n, j)] = _to_natural(g.reshape((N_DEV,) + vals[n].shape[1:]), kinds[n])

    first = [("ffn1_w_gate", 0), ("ffn1_w_up", 0)]
    beside = {"l0_ffn1_gate": [("ffn1_w_down", 0)],
              "l0_ffn1_up": [("ab_w_in", 0), ("rwkv_w_up", 0), ("rwkv_a_up", 0), ("rwkv_g_up", 0)],
              "l0_ffn1_down": [("ab_w_out", 0), ("ffn2_w_gate", 0)],
              "ab_in": [("ffn2_w_up", 0)],
              "attention_fwd": [("ffn2_w_down", 0), ("ple_w_gate", 0), ("ple_w_proj", 0)],
              "l0_ffn2_gate": [("ffn2_w_gate", 1)], "l0_ffn2_up": [("ffn2_w_up", 1)], "l0_ffn2_down": [("ffn2_w_down", 1)],
              "l0_ple_gate": [("ple_w_gate", 1), ("ple_w_proj", 1)]}
    planned = first + [u for us in beside.values() for u in us]
    beside["rwkv_fwd"] = [u for u in units if u not in planned]
    assert depth == 2 and sorted(planned + beside["rwkv_fwd"]) == sorted(units)
    take_gathered(first, gather_all([shard(u) for u in first]).run("gather_first"))

    def hosting(name):
        which = beside.get(name)
        if not which:
            return None, None
        return gather_all([shard(u) for u in which]), functools.partial(take_gathered, which)

    def row(name, j):
        return vals[name][j].reshape(1, -1)

    def ffn_weights(which, i):
        return (row(f"{which}_norm", i), full[(f"{which}_w_gate", i)], full[(f"{which}_w_up", i)],
                full[(f"{which}_w_down", i)])

    def mixer_weights(i):
        j = i // 2
        if i % 2 == 0:
            w = {n: row(n, j) for n in ("att_q_gain", "att_k_gain", "rwkv_mu", "rwkv_w0", "rwkv_a0", "rwkv_k_k", "rwkv_k_a",
                                        "rwkv_r_k", "rwkv_lnx_w", "rwkv_lnx_b")}
            w.update({n: full.get((n, j)) for n in ("ab_w_in", "ab_w_out", "rwkv_w_up", "rwkv_a_up", "rwkv_g_up")})
            w["att_rel_bias"] = att_rel_bias[j]
        else:
            w = {"ssm_lambda_re": ssm_lambda_re[j], "ssm_lambda_im": ssm_lambda_im[j],
                 "ssm_log_dt": ssm_log_dt[j].reshape(n_groups, 1),
                 "ssm_b_re": ssm_b_re[j].reshape(n_groups * n_state, -1),
                 "ssm_b_im": ssm_b_im[j].reshape(n_groups * n_state, -1),
                 "ssm_c_re": ssm_c_re[j], "ssm_c_im": ssm_c_im[j], "ssm_d": full[("ssm_d", j)].reshape(1, -1),
                 "ssm_w_in": full[("ssm_w_in", j)], "ssm_w_out": full[("ssm_w_out", j)]}
        w["mix_norm"] = row("mix_norm", i)
        return w

    def ple_weights(i):
        return (row("ple_norm", i), full[("ple_w_gate", i)], full[("ple_w_proj", i)], p[i, 0])

    h = x[0]
    saved = []
    for i in range(depth):
        h, s1 = _ffn_fwd(f"l{i}_ffn1", h, row("ffn1_norm", i), lambda kind, i=i: full[(f"ffn1_w_{kind}", i)], hosting)
        h, sm = _ab_fwd(h, mixer_weights(i), hosting) if i % 2 == 0 else _s5_fwd(h, mixer_weights(i))
        h, s2 = _ffn_fwd(f"l{i}_ffn2", h, row("ffn2_norm", i), lambda kind, i=i: full[(f"ffn2_w_{kind}", i)], hosting)
        h, sp = _ple_fwd(f"l{i}_ple", h, *ple_weights(i), hosting)
        saved.append((s1, sm, s2, sp))
    dh, loss_part = loss_head(h, loss_target[0])

    per_layer = {n: [] for n in WEIGHTS}
    received = {}

    def chip_sums(which, tag):
        pieces = [rows(_to_stacked(per_layer[n][j - vals[n].shape[0]], kinds[n]).astype(travel[n]), lead=1)
                  for n, j in which]
        from_sibling = pair_exchange(pieces).run(f"reduce_pair_{tag}")
        return [pair_sum(f"pair_sum_{n}{j}", a, b) for (n, j), a, b in zip(which, pieces, from_sibling)]

    def late_reduce(d_ab_w_out):
        per_layer["ab_w_out"].insert(0, d_ab_w_out)
        return chip_exchange(chip_sums(late, "late"))

    def late_reduce_done(got):
        received.update(zip(late, got))

    for i in reversed(range(depth)):
        s1, sm, s2, sp = saved[i]
        dh, d_norm, d_gate, d_proj = _ple_bwd(f"l{i}_ple", dh, sp, *ple_weights(i))
        for n, g in (("ple_norm", d_norm), ("ple_w_gate", d_gate), ("ple_w_proj", d_proj)):
            per_layer[n].insert(0, g)
        dh, d_norm, d_gate, d_up, d_down = _ffn_bwd(f"l{i}_ffn2", dh, s2, *ffn_weights("ffn2", i))
        for n, g in (("ffn2_norm", d_norm), ("ffn2_w_gate", d_gate), ("ffn2_w_up", d_up), ("ffn2_w_down", d_down)):
            per_layer[n].insert(0, g)
        if i == 0:
            dh, mixer_grads = _ab_bwd(dh, sm, mixer_weights(i), late_reduce, late_reduce_done)
        else:
            dh, mixer_grads = (_ab_bwd if i % 2 == 0 else _s5_bwd)(dh, sm, mixer_weights(i))
        for n, g in mixer_grads.items():
            if not (i == 0 and n == "ab_w_out"):
                per_layer[n].insert(0, g)
        dh, d_norm, d_gate, d_up, d_down = _ffn_bwd(f"l{i}_ffn1", dh, s1, *ffn_weights("ffn1", i))
        for n, g in (("ffn1_norm", d_norm), ("ffn1_w_gate", d_gate), ("ffn1_w_up", d_up), ("ffn1_w_down", d_down)):
            per_layer[n].insert(0, g)
    grad_x = dh[None]

    received.update(zip(early, chip_exchange(chip_sums(early, "early")).run("reduce_chips_early")))

    rep_mine = _pack([jnp.stack(per_layer[n], axis=0).reshape(vals[n].shape) for n in REPLICATED] + [loss_part], F32)
    rep_all = gather_all([rep_mine]).run("gather_replicated")[0]

    out = {}
    for n in kinds:
        got = [received[(n, j)] for j in range(vals[n].shape[0])]
        got = got[0] if len(got) == 1 else jnp.concatenate(got, axis=1)
        state = [rows(vals[pre + n]) for pre in ("", "m_", "v_")]
        out[n] = tuple(r.reshape(vals[n].shape) for r in adam_update(f"adam_{n}", got, *state))
    zero = jnp.zeros((1, 1), F32)
    state = [_pack([vals[pre + n] for n in REPLICATED] + [zero], F32) for pre in ("", "m_", "v_")]
    shapes = [vals[n].shape for n in REPLICATED] + [zero.shape]
    results = [_unpack(r, shapes) for r in adam_update("adam_replicated", rep_all, *state)]
    for q, n in enumerate(REPLICATED):
        out[n] = tuple(r[q] for r in results)
    loss = results[0][-1].reshape(())
    return (loss, grad_x, *[out[n][0] for n in WEIGHTS], *[out[n][1] for n in WEIGHTS], *[out[n][2] for n in WEIGHTS],
            *[out[n][3] for n in WEIGHTS])
```

```python
import functools
import math

import jax
import jax.numpy as jnp
import numpy as np
from jax import lax
from jax.experimental import pallas as pl
from jax.experimental.pallas import tpu as pltpu

F32 = jnp.float32
BF16 = jnp.bfloat16
HIGHEST = lax.Precision.HIGHEST
MESH_AXES = ("x", "y", "c")
N_DEV = 8

CHUNK = 64
N_LEFT_CHUNKS = 8
BAND = (N_LEFT_CHUNKS + 1) * CHUNK
HEAD_DIM = 64
REL_CLIP = 128
N_REL = (CHUNK - 1) + REL_CLIP + 1
DECAY_LORA = 64
AAA_LORA = 64
GATE_LORA = 128
SSM_GROUP = 16
SSM_STATE = 64
RMS_EPS = 1e-6
GN_EPS = 64e-5
ADAM_LR = 0.001
ADAM_B1 = 0.9
ADAM_B2 = 0.999
ADAM_EPS = 1e-08
ADAM_WD = 0.01
ADAM_STEP = 10

RWKV_CHUNK = 64
VMEM_LIMIT = 56 * 1024 * 1024
LANES = 128


def _params(semantics):
    return pltpu.CompilerParams(dimension_semantics=semantics, vmem_limit_bytes=VMEM_LIMIT)


def _pick(n, prefs):
    for t in prefs:
        if n % t == 0:
            return t
    return n


def _dot(a, b, dims):
    return lax.dot_general(a, b, (dims, ((), ())), precision=HIGHEST, preferred_element_type=F32)


def _mm(a, b):
    return _dot(a, b, ((1,), (0,)))


def _mm_nt(a, b):
    return _dot(a, b, ((1,), (1,)))


def _mm_tn(a, b):
    return _dot(a, b, ((0,), (0,)))


def _split2(x):
    hi = x.astype(BF16)
    return hi, (x - hi.astype(F32)).astype(BF16)


def _dot3_raw(a, b, dims):
    a_hi, a_lo = _split2(a)
    b_hi, b_lo = _split2(b)
    dot = lambda p, q: lax.dot_general(p, q, (dims, ((), ())), preferred_element_type=F32)
    return dot(a_hi, b_hi) + (dot(a_hi, b_lo) + dot(a_lo, b_hi))


def _dot1_raw(a, b, dims):
    return lax.dot_general(a.astype(BF16), b.astype(BF16), (dims, ((), ())), preferred_element_type=F32)


NN, NT, TN = ((1,), (0,)), ((1,), (1,)), ((0,), (0,))


def _make_dot(raw):
    @functools.partial(jax.custom_vjp, nondiff_argnums=(2,))
    def dot(a, b, dims):
        return raw(a, b, dims)

    def fwd(a, b, dims):
        return raw(a, b, dims), (a, b)

    def bwd(dims, saved, g):
        a, b = saved
        if dims == NN:
            return raw(g, b, NT), raw(a, g, TN)
        if dims == NT:
            return raw(g, b, NN), raw(g, a, TN)
        return raw(b, g, NT), raw(a, g, NN)

    dot.defvjp(fwd, bwd)
    return dot


_dot3 = _make_dot(_dot3_raw)
_dot1 = _make_dot(_dot1_raw)


def _dot_ind_raw(x, ind, dims):
    hi = x.astype(BF16)
    rest = x - hi.astype(F32)
    mid = rest.astype(BF16)
    lo = (rest - mid.astype(F32)).astype(BF16)
    ind = ind.astype(BF16)
    dot = lambda p: lax.dot_general(p, ind, (dims, ((), ())), preferred_element_type=F32)
    return dot(hi) + (dot(mid) + dot(lo))


@jax.custom_vjp
def _mm_ind(x, ind):
    return _dot_ind_raw(x, ind, NN)


def _mm_ind_fwd(x, ind):
    return _dot_ind_raw(x, ind, NN), ind


def _mm_ind_bwd(ind, g):
    return _dot_ind_raw(g, ind, NT), jnp.zeros_like(ind)


_mm_ind.defvjp(_mm_ind_fwd, _mm_ind_bwd)


def _sigmoid(x):
    return 1.0 / (1.0 + jnp.exp(-x))


def _softplus(x):
    return jnp.maximum(x, 0.0) + jnp.log(1.0 + jnp.exp(-jnp.abs(x)))


def _gelu_tanh(x):
    return 0.5 * x * (1.0 + jnp.tanh(math.sqrt(2.0 / math.pi) * (x + 0.044715 * (x * x * x))))


def _seg_indicator(n, seg):
    r = lax.broadcasted_iota(jnp.int32, (n, n // seg), 0)
    c = lax.broadcasted_iota(jnp.int32, (n, n // seg), 1)
    return jnp.where((r >= c * seg) & (r < (c + 1) * seg), 1.0, 0.0).astype(F32)


def _seg_indicator_t(n, seg):
    c = lax.broadcasted_iota(jnp.int32, (n // seg, n), 0)
    r = lax.broadcasted_iota(jnp.int32, (n // seg, n), 1)
    return jnp.where((r >= c * seg) & (r < (c + 1) * seg), 1.0, 0.0).astype(F32)


def _seg_sum(x, seg):
    return _mm_ind(x, _seg_indicator(x.shape[1], seg))


def _seg_expand(s, seg):
    return _mm_ind(s, _seg_indicator_t(s.shape[1] * seg, seg))


def _tile_lanes(g, reps):
    n = g.shape[1]
    r = lax.broadcasted_iota(jnp.int32, (n, n * reps), 0)
    c = lax.broadcasted_iota(jnp.int32, (n, n * reps), 1)
    return _mm_ind(g, jnp.where((c & (n - 1)) == r, 1.0, 0.0).astype(F32))


def _rms(x, g):
    return x * lax.rsqrt(jnp.mean(x * x, axis=-1, keepdims=True) + RMS_EPS) * g


MATMUL_VMEM_BUDGET = 40 * 1024 * 1024
MATMUL_MAX_TILE = 2048
HBM_BYTES_PER_S = 1.2e12
MXU_FLOPS_PER_S = 8e14
GRID_STEP_S = 0.35e-6


def _tile_candidates(dim):
    c = [d for d in range(LANES, min(dim, MATMUL_MAX_TILE) + 1, LANES) if dim % d == 0]
    return c or [dim]


def _matmul_tiles(m, n, k, a_bytes, b_bytes, out_bytes, res_bytes):
    best = None
    for tm in _tile_candidates(m):
        for tn in _tile_candidates(n):
            for tk in _tile_candidates(k):
                casts = (tm * tk * 2 if a_bytes > 2 else 0) + (tk * tn * 2 if b_bytes > 2 else 0)
                vmem = (2 * (tm * tk * a_bytes + tk * tn * b_bytes + tm * tn * (out_bytes + res_bytes))
                        + 2 * tm * tn * 4 + casts)
                if vmem > MATMUL_VMEM_BUDGET:
                    continue
                steps = (m // tm) * (n // tn) * (k // tk)
                a_reads = 1 if k == tk else n // tn
                traffic = (m * k * a_bytes * a_reads + k * n * b_bytes * (m // tm) + m * n * (out_bytes + res_bytes))
                dma_s = traffic / HBM_BYTES_PER_S
                ends_s = (tm * tk * a_bytes + tk * tn * b_bytes + tm * tn * out_bytes) / HBM_BYTES_PER_S
                cost = max(dma_s, 2.0 * m * n * k / MXU_FLOPS_PER_S) + 0.2 * dma_s + steps * GRID_STEP_S + ends_s
                if best is None or cost < best[0]:
                    best = (cost, tm, tn, tk)
    return best[1:]


def matmul(name, a, b, *, ta=False, tb=False, alpha=1.0, res=None, out_dtype=F32, hosted=None):
    m, k = (a.shape[1], a.shape[0]) if ta else a.shape
    n = b.shape[0] if tb else b.shape[1]
    assert k == (b.shape[1] if tb else b.shape[0]), (name, a.shape, b.shape)
    tm, tn, tk = _matmul_tiles(m, n, k, a.dtype.itemsize, b.dtype.itemsize, jnp.dtype(out_dtype).itemsize,
                               0 if res is None else res.dtype.itemsize)
    nk = k // tk
    grid = (m // tm, n // tn, nk)
    dims = ((0 if ta else 1,), (1 if tb else 0,))
    h_arrays, h_specs, h_shapes, h_scratch = _hosted_call_args(hosted)
    nhosted = len(h_arrays)
    n_in = 2 + (res is not None)

    def body(*refs):
        a_ref, b_ref = refs[:2]
        res_ref = refs[2] if res is not None else None
        h_in, o_ref, h_out = refs[n_in:n_in + nhosted], refs[n_in + nhosted], refs[n_in + nhosted + 1:n_in + 2 * nhosted + 1]
        acc_ref, sems = refs[n_in + 2 * nhosted + 1], refs[n_in + 2 * nhosted + 2:]
        _hosted_steps(hosted, h_in, h_out, sems, grid, 0.6)
        kk = pl.program_id(2)

        @pl.when(kk == 0)
        def _():
            acc_ref[...] = jnp.zeros_like(acc_ref)

        acc_ref[...] += lax.dot_general(a_ref[...].astype(BF16), b_ref[...].astype(BF16), (dims, ((), ())),
                                        preferred_element_type=F32)

        @pl.when(kk == nk - 1)
        def _():
            out = acc_ref[...] * alpha
            if res_ref is not None:
                out = out + res_ref[...].astype(F32)
            o_ref[...] = out.astype(o_ref.dtype)

    a_spec = pl.BlockSpec((tk, tm), lambda i, j, kk: (kk, i)) if ta else pl.BlockSpec((tm, tk), lambda i, j, kk: (i, kk))
    b_spec = pl.BlockSpec((tn, tk), lambda i, j, kk: (j, kk)) if tb else pl.BlockSpec((tk, tn), lambda i, j, kk: (kk, j))
    o_spec = pl.BlockSpec((tm, tn), lambda i, j, kk: (i, j))
    in_specs = [a_spec, b_spec] + ([o_spec] if res is not None else [])
    args = (a, b) + ((res,) if res is not None else ())
    out_shape = jax.ShapeDtypeStruct((m, n), out_dtype)
    if hosted is None:
        return pl.pallas_call(
            body, name=name, grid=grid, in_specs=in_specs, out_specs=o_spec, out_shape=out_shape,
            scratch_shapes=[pltpu.VMEM((tm, tn), F32)], compiler_params=_params(("parallel", "parallel", "arbitrary")),
        )(*args)
    out, *travelled = pl.pallas_call(
        body, name=name, grid=grid, in_specs=in_specs + h_specs, out_specs=[o_spec] + h_specs,
        out_shape=[out_shape] + h_shapes, scratch_shapes=[pltpu.VMEM((tm, tn), F32)] + h_scratch,
        compiler_params=_params(("arbitrary", "arbitrary", "arbitrary")),
    )(*args, *h_arrays)
    return out, travelled


def _row_specs(arrays, tr):
    return [pl.BlockSpec((tr, a.shape[1]), lambda i: (i, 0)) for a in arrays]


def _whole_specs(arrays):
    return [pl.BlockSpec(a.shape, lambda i: (0, 0)) for a in arrays]


def rowwise(name, fn, rows, consts, outs, tr):
    t = rows[0].shape[0]
    nr, nc = len(rows), len(consts)

    def body(*refs):
        vals = [r[...] for r in refs[:nr + nc]]
        res = fn(*vals)
        for o_ref, o in zip(refs[nr + nc:], res):
            o_ref[...] = o.astype(o_ref.dtype)

    out_shape = [jax.ShapeDtypeStruct((t, w), d) for w, d in outs]
    return pl.pallas_call(
        body, name=name, grid=(t // tr,), in_specs=_row_specs(rows, tr) + _whole_specs(consts),
        out_specs=_row_specs(out_shape, tr), out_shape=out_shape, compiler_params=_params(("parallel",)),
    )(*rows, *consts)


def rowwise_vjp(name, fn, rows, consts, cots, tr, row_grad, const_grad, add_to=None, grad_dtypes=None):
    t = rows[0].shape[0]
    nr, nc = len(rows), len(consts)
    cot_groups = [c if isinstance(c, (tuple, list)) else (c,) for c in cots]
    flat_cots = [a for g in cot_groups for a in g]
    add_to = {i: (a if isinstance(a, (tuple, list)) else (a,)) for i, a in (add_to or {}).items()}
    add_idx = [(i, q) for i in sorted(add_to) for q in range(len(add_to[i]))]
    add_arrays = [add_to[i][q] for i, q in add_idx]
    r_idx = [i for i in range(nr) if row_grad[i]]
    c_idx = [i for i in range(nc) if const_grad[i]]
    grad_dtypes = grad_dtypes or {}
    n_in = nr + nc + len(flat_cots) + len(add_arrays)

    def body(*refs):
        vals = [r[...] for r in refs[:nr + nc]]
        pos = nr + nc
        cts = []
        for g in cot_groups:
            s = refs[pos][...].astype(F32)
            for q in range(1, len(g)):
                s = s + refs[pos + q][...].astype(F32)
            cts.append(s)
            pos += len(g)
        adds = {}
        for n_add, (i, _) in enumerate(add_idx):
            term = refs[pos + n_add][...].astype(F32)
            adds[i] = adds[i] + term if i in adds else term
        diff =[vals[i] for i in r_idx] + [vals[nr + i] for i in c_idx]

        def f(*d):
            full = list(vals)
            for q, i in enumerate(r_idx):
                full[i] = d[q]
            for q, i in enumerate(c_idx):
                full[nr + i] = d[len(r_idx) + q]
            return tuple(fn(*full))

        prim, vjp = jax.vjp(f, *diff)
        grads = vjp(tuple(c.astype(p.dtype) for c, p in zip(cts, prim)))
        o_refs = refs[n_in:]
        for q, i in enumerate(r_idx):
            g = grads[q].astype(F32)
            if i in adds:
                g = g + adds[i].astype(F32)
            o_refs[q][...] = g.astype(o_refs[q].dtype)
        step = pl.program_id(0)
        for q, i in enumerate(c_idx):
            o_ref = o_refs[len(r_idx) + q]

            @pl.when(step == 0)
            def _(o_ref=o_ref):
                o_ref[...] = jnp.zeros_like(o_ref)

            o_ref[...] += grads[len(r_idx) + q].astype(F32)

    row_out = [jax.ShapeDtypeStruct(rows[i].shape, grad_dtypes.get(i, F32)) for i in r_idx]
    const_out = [jax.ShapeDtypeStruct(consts[i].shape, F32) for i in c_idx]
    outs = pl.pallas_call(
        body, name=name, grid=(t // tr,),
        in_specs=_row_specs(rows, tr) + _whole_specs(consts) + _row_specs(flat_cots, tr) + _row_specs(add_arrays, tr),
        out_specs=_row_specs(row_out, tr) + _whole_specs(const_out), out_shape=row_out + const_out,
        compiler_params=_params(("arbitrary",)),
    )(*rows, *consts, *flat_cots, *add_arrays)
    return list(outs)


def loss_head(y, target):
    t, d = y.shape
    tr = _pick(t, (256, 128, 64, 32, 16, 8))

    def body(y_ref, t_ref, dy_ref, l_ref):
        diff = y_ref[...] - t_ref[...]
        dy_ref[...] = diff * (1.0 / d)

        @pl.when(pl.program_id(0) == 0)
        def _():
            l_ref[...] = jnp.zeros_like(l_ref)

        l_ref[...] += 0.5 * jnp.sum(jnp.mean(diff * diff, axis=-1, keepdims=True), axis=0, keepdims=True)

    dy, l = pl.pallas_call(
        body, name="loss_head", grid=(t // tr,), in_specs=_row_specs([y, target], tr),
        out_specs=[pl.BlockSpec((tr, d), lambda i: (i, 0)), pl.BlockSpec((1, 1), lambda i: (0, 0))],
        out_shape=[jax.ShapeDtypeStruct((t, d), F32), jax.ShapeDtypeStruct((1, 1), F32)],
        compiler_params=_params(("arbitrary",)),
    )(y, target)
    return dy, l


ATT_PAD = N_LEFT_CHUNKS * CHUNK
MASKED = -1e30
ATT_GROUP = 2


def _attn_chunks(q_c, k_b, v_b, bias, valid):
    s = [jnp.where(ok, _dot1(q, k, NT) + b, MASKED) for q, k, b, ok in zip(q_c, k_b, bias, valid)]
    e = [jnp.exp(x - lax.stop_gradient(jnp.max(x, axis=-1, keepdims=True))) for x in s]
    p = [x / jnp.sum(x, axis=-1, keepdims=True) for x in e]
    return tuple(_dot1(x, v, NN) for x, v in zip(p, v_b))


def _band_valid(c):
    return (c * CHUNK + lax.broadcasted_iota(jnp.int32, (1, BAND), 1)) >= ATT_PAD


HEAD_PAIR = LANES // HEAD_DIM


def _split_heads(x, n):
    return [x[:, h * HEAD_DIM:(h + 1) * HEAD_DIM] for h in range(n)]


def _pair_spec(t):
    return pl.BlockSpec((t, HEAD_PAIR * HEAD_DIM), lambda p: (0, p))


def _attn_operands(q_ref, kp, vp, bias, cs, q0):
    qs = [x for s in q0 for x in _split_heads(q_ref[pl.ds(s, CHUNK), :], HEAD_PAIR)]
    ks = [x for s in q0 for x in _split_heads(kp[pl.ds(s, BAND), :], HEAD_PAIR)]
    vs = [x for s in q0 for x in _split_heads(vp[pl.ds(s, BAND), :], HEAD_PAIR)]
    return tuple(qs), tuple(ks), tuple(vs), tuple(bias) * len(cs), tuple(_band_valid(c) for c in cs for _ in bias)


def attention_fwd(q, k, v, bias, hosted=None):
    t, width = q.shape
    pairs = width // (HEAD_PAIR * HEAD_DIM)
    nchunks = t // CHUNK
    group = _pick(nchunks, (ATT_GROUP, 1))
    h_arrays, h_specs, h_shapes, h_scratch = _hosted_call_args(hosted)
    nhosted = len(h_arrays)

    def body(q_ref, k_ref, v_ref, b_ref, *rest):
        h_in, o_ref, h_out = rest[:nhosted], rest[nhosted], rest[nhosted + 1:2 * nhosted + 1]
        kp, vp, sems = rest[2 * nhosted + 1], rest[2 * nhosted + 2], rest[2 * nhosted + 3:]
        _hosted_steps(hosted, h_in, h_out, sems, (pairs,), 0.6)
        zeros = jnp.zeros((ATT_PAD, HEAD_PAIR * HEAD_DIM), F32)
        kp[pl.ds(0, ATT_PAD), :] = zeros
        vp[pl.ds(0, ATT_PAD), :] = zeros
        kp[pl.ds(ATT_PAD, t), :] = k_ref[...]
        vp[pl.ds(ATT_PAD, t), :] = v_ref[...]
        bias = [b_ref[h] for h in range(HEAD_PAIR)]

        def step(g, carry):
            cs = [g * group + u for u in range(group)]
            q0 = [pl.multiple_of(c * CHUNK, CHUNK) for c in cs]
            outs = _attn_chunks(*_attn_operands(q_ref, kp, vp, bias, cs, q0))
            for u, s in enumerate(q0):
                o_ref[pl.ds(s, CHUNK), :] = jnp.concatenate(outs[HEAD_PAIR * u:HEAD_PAIR * (u + 1)], axis=1)
            return carry

        lax.fori_loop(0, nchunks // group, step, 0)

    out, *travelled = pl.pallas_call(
        body, name="attention_fwd", grid=(pairs,),
        in_specs=[_pair_spec(t)] * 3 + [pl.BlockSpec((HEAD_PAIR, CHUNK, BAND), lambda p: (p, 0, 0))] + h_specs,
        out_specs=[_pair_spec(t)] + h_specs, out_shape=[jax.ShapeDtypeStruct((t, width), F32)] + h_shapes,
        scratch_shapes=[pltpu.VMEM((t + ATT_PAD, HEAD_PAIR * HEAD_DIM), F32)] * 2 + h_scratch,
        compiler_params=_params(("arbitrary",)),
    )(q, k, v, bias, *h_arrays)
    return out if hosted is None else (out, travelled)


def attention_bwd(q, k, v, bias, dout):
    t, width = q.shape
    pairs = width // (HEAD_PAIR * HEAD_DIM)
    nchunks = t // CHUNK
    group = _pick(nchunks, (ATT_GROUP, 1))

    def body(q_ref, k_ref, v_ref, b_ref, do_ref, dq_ref, dk_ref, dv_ref, db_ref, kp, vp, dkp, dvp):
        zeros = jnp.zeros((ATT_PAD, HEAD_PAIR * HEAD_DIM), F32)
        kp[pl.ds(0, ATT_PAD), :] = zeros
        vp[pl.ds(0, ATT_PAD), :] = zeros
        kp[pl.ds(ATT_PAD, t), :] = k_ref[...]
        vp[pl.ds(ATT_PAD, t), :] = v_ref[...]
        dkp[...] = jnp.zeros_like(dkp)
        dvp[...] = jnp.zeros_like(dvp)
        db_ref[...] = jnp.zeros_like(db_ref)
        bias = [b_ref[h] for h in range(HEAD_PAIR)]

        def step(g, carry):
            cs = [g * group + u for u in range(group)]
            q0 = [pl.multiple_of(c * CHUNK, CHUNK) for c in cs]
            qs, ks, vs, bs, valid = _attn_operands(q_ref, kp, vp, bias, cs, q0)
            _, vjp = jax.vjp(lambda a, b, cc, d: _attn_chunks(a, b, cc, d, valid), qs, ks, vs, bs)
            dos = tuple(x for s in q0 for x in _split_heads(do_ref[pl.ds(s, CHUNK), :], HEAD_PAIR))
            dq, dk, dv, db = vjp(dos)
            for u, s in enumerate(q0):
                mine = slice(HEAD_PAIR * u, HEAD_PAIR * (u + 1))
                dq_ref[pl.ds(s, CHUNK), :] = jnp.concatenate(dq[mine], axis=1)
                dkp[pl.ds(s, BAND), :] += jnp.concatenate(dk[mine], axis=1)
                dvp[pl.ds(s, BAND), :] += jnp.concatenate(dv[mine], axis=1)
            for h in range(HEAD_PAIR):
                total = db[h]
                for u in range(1, group):
                    total = total + db[HEAD_PAIR * u + h]
                db_ref[h] += total
            return carry

        lax.fori_loop(0, nchunks // group, step, 0)
        dk_ref[...] = dkp[pl.ds(ATT_PAD, t), :]
        dv_ref[...] = dvp[pl.ds(ATT_PAD, t), :]

    bias_spec = pl.BlockSpec((HEAD_PAIR, CHUNK, BAND), lambda p: (p, 0, 0))
    ts = jax.ShapeDtypeStruct((t, width), F32)
    return pl.pallas_call(
        body, name="attention_bwd", grid=(pairs,), in_specs=[_pair_spec(t)] * 3 + [bias_spec, _pair_spec(t)],
        out_specs=[_pair_spec(t)] * 3 + [bias_spec],
        out_shape=[ts, ts, ts, jax.ShapeDtypeStruct(bias.shape, F32)],
        scratch_shapes=[pltpu.VMEM((t + ATT_PAD, HEAD_PAIR * HEAD_DIM), F32)] * 4, compiler_params=_params(("parallel",)),
    )(q, k, v, bias, dout)


def _rel_onehot_t(i):
    j = lax.broadcasted_iota(jnp.int32, (N_REL, BAND), 1)
    r = lax.broadcasted_iota(jnp.int32, (N_REL, BAND), 0)
    idx = jnp.clip(i + ATT_PAD - j, -(CHUNK - 1), REL_CLIP) + (CHUNK - 1)
    return jnp.where(r == idx, 1.0, 0.0).astype(F32)


def bias_expand(rel):
    nh = rel.shape[0]

    def body(rel_ref, o_ref):
        o_ref[...] = _mm(rel_ref[...], _rel_onehot_t(pl.program_id(0)))

    return pl.pallas_call(
        body, name="bias_expand", grid=(CHUNK,), in_specs=[pl.BlockSpec((nh, N_REL), lambda i: (0, 0))],
        out_specs=pl.BlockSpec((None, nh, BAND), lambda i: (i, 0, 0)),
        out_shape=jax.ShapeDtypeStruct((CHUNK, nh, BAND), F32), compiler_params=_params(("parallel",)),
    )(rel)


def bias_reduce(dbias):
    nh = dbias.shape[1]

    def body(d_ref, o_ref):
        @pl.when(pl.program_id(0) == 0)
        def _():
            o_ref[...] = jnp.zeros_like(o_ref)

        o_ref[...] += _mm_nt(d_ref[...], _rel_onehot_t(pl.program_id(0)))

    return pl.pallas_call(
        body, name="bias_reduce", grid=(CHUNK,), in_specs=[pl.BlockSpec((None, nh, BAND), lambda i: (i, 0, 0))],
        out_specs=pl.BlockSpec((nh, N_REL), lambda i: (0, 0)), out_shape=jax.ShapeDtypeStruct((nh, N_REL), F32),
        compiler_params=_params(("arbitrary",)),
    )(dbias)


def _tri(n, strict):
    r = lax.broadcasted_iota(jnp.int32, (n, n), 0)
    c = lax.broadcasted_iota(jnp.int32, (n, n), 1)
    return (c < r) if strict else (c <= r)


def _each(f, *lists):
    return [f(*args) for args in zip(*lists)]


def _rwkv_chunk(s, r, k, v, w, a, b):
    n = r[0].shape[0]
    strict, incl = _tri(n, True), _tri(n, False)
    ones = jnp.where(incl, 1.0, 0.0).astype(F32)
    eye = jnp.where(incl & ~strict, 1.0, 0.0).astype(F32)
    lw = _each(jnp.log, w)
    cum = _each(lambda x: _mm(ones, x), lw)
    p_incl = _each(jnp.exp, cum)
    p_inv = _each(lambda x: jnp.exp(-x), cum)
    a_t = _each(lambda x, c, l: x * jnp.exp(c - l), a, cum, lw)
    r_t = _each(jnp.multiply, r, p_incl)
    b_t = _each(jnp.multiply, b, p_inv)
    k_t = _each(jnp.multiply, k, p_inv)
    a_ab = _each(lambda x, y: jnp.where(strict, _dot3(x, y, NT), 0.0), a_t, b_t)
    a_ak = _each(lambda x, y: jnp.where(strict, _dot3(x, y, NT), 0.0), a_t, k_t)
    r_b = _each(lambda x, y: jnp.where(incl, _dot3(x, y, NT), 0.0), r_t, b_t)
    r_k = _each(lambda x, y: jnp.where(incl, _dot3(x, y, NT), 0.0), r_t, k_t)
    a_s = _each(lambda x, y: _dot3(x, y, NT), a_t, s)
    r_s = _each(lambda x, y: _dot3(x, y, NT), r_t, s)
    a_kv = _each(lambda x, y: _dot3(x, y, NN), a_ak, v)
    total = _each(lambda x: eye + x, a_ab)
    power = _each(lambda x: _dot3(x, x, NN), a_ab)
    rounds = int(math.log2(n)) - 1
    for i in range(rounds):
        total = _each(lambda t, p: t + _dot3(t, p, NN), total, power)
        if i < rounds - 1:
            power = _each(lambda p: _dot3(p, p, NN), power)
    sa = _each(lambda t, x, y: _dot3(t, x + y, NN), total, a_s, a_kv)
    y = _each(lambda rs, rb, x, rk, vv: rs + _dot3(rb, x, NN) + _dot3(rk, vv, NN), r_s, r_b, sa, r_k, v)
    s_new = _each(lambda ss, x, bt, vv, kt, p: (ss + _dot3(x, bt, TN) + _dot3(vv, kt, TN)) * p[n - 1:n, :],
                  s, sa, b_t, v, k_t, p_incl)
    return tuple(s_new), tuple(y)


RWKV_TILE = 512
RWKV_HEADS = 4


def _rwkv_specs(nh, t, reverse):
    tile = min(RWKV_TILE, t)
    hb = RWKV_HEADS if nh % RWKV_HEADS == 0 else nh
    nt = t // tile
    per = tile // RWKV_CHUNK
    pos = (lambda h, i: (nt - 1 - i, h)) if reverse else (lambda h, i: (i, h))
    pos4 = (lambda h, i: (h, nt - 1 - i, 0, 0)) if reverse else (lambda h, i: (h, i, 0, 0))
    return (hb, nt, per, pl.BlockSpec((tile, hb * HEAD_DIM), pos), pl.BlockSpec((hb, per, HEAD_DIM, HEAD_DIM), pos4))


def _hosted_steps(exchange, refs_in, refs_out, sems, grid, forward_share):
    if exchange is None:
        return
    start, forward, finish = exchange.plan(refs_in, refs_out, *sems)
    step, n_steps = 0, 1
    for axis, size in enumerate(grid):
        step = step * size + pl.program_id(axis)
        n_steps *= size
    pl.when(step == 0)(start)
    pl.when(step == int(forward_share * (n_steps - 1)))(forward)
    pl.when(step == n_steps - 1)(finish)


def _hosted_call_args(exchange):
    if exchange is None:
        return [], [], [], []
    nb = len(exchange.arrays)
    return exchange.arrays, [HBM_SPEC] * nb, exchange.out_shape, exchange.scratch()


def rwkv_fwd(r, k, v, w, a, b, hosted=None):
    t, width = r.shape
    nh = width // HEAD_DIM
    hb, nt, per, row_spec, s_spec = _rwkv_specs(nh, t, False)
    h_arrays, h_specs, h_shapes, h_scratch = _hosted_call_args(hosted)
    nhosted = len(h_arrays)
    n_steps = (nh // hb) * nt

    def body(r_ref, k_ref, v_ref, w_ref, a_ref, b_ref, *rest):
        h_in, (y_ref, s_ref), h_out = rest[:nhosted], rest[nhosted:nhosted + 2], rest[nhosted + 2:2 * nhosted + 2]
        state, sems = rest[2 * nhosted + 2], rest[2 * nhosted + 3:]
        _hosted_steps(hosted, h_in, h_out, sems, (nh // hb, nt), 0.75)

        @pl.when(pl.program_id(1) == 0)
        def _():
            state[...] = jnp.zeros_like(state)

        def step(c, states):
            rows = pl.ds(pl.multiple_of(c * RWKV_CHUNK, RWKV_CHUNK), RWKV_CHUNK)
            for h in range(hb):
                s_ref[h, c] = states[h]
            s_new, y = _rwkv_chunk(states, *(tuple(_split_heads(ref[rows, :], hb))
                                             for ref in (r_ref, k_ref, v_ref, w_ref, a_ref, b_ref)))
            y_ref[rows, :] = jnp.concatenate(y, axis=1)
            return s_new

        final = lax.fori_loop(0, per, step, tuple(state[h] for h in range(hb)))
        for h in range(hb):
            state[h] = final[h]

    return pl.pallas_call(
        body, name="rwkv_fwd", grid=(nh // hb, nt), in_specs=[row_spec] * 6 + h_specs,
        out_specs=[row_spec, s_spec] + h_specs,
        out_shape=[jax.ShapeDtypeStruct((t, width), F32),
                   jax.ShapeDtypeStruct((nh, t // RWKV_CHUNK, HEAD_DIM, HEAD_DIM), F32)] + h_shapes,
        scratch_shapes=[pltpu.VMEM((hb, HEAD_DIM, HEAD_DIM), F32)] + h_scratch,
        compiler_params=_params(("arbitrary", "arbitrary")),
    )(r, k, v, w, a, b, *h_arrays)


def rwkv_bwd(r, k, v, w, a, b, states, dy, hosted=None):
    t, width = r.shape
    nh = width // HEAD_DIM
    hb, nt, per, row_spec, s_spec = _rwkv_specs(nh, t, True)
    h_arrays, h_specs, h_shapes, h_scratch = _hosted_call_args(hosted)
    nhosted = len(h_arrays)
    n_steps = (nh // hb) * nt

    def body(r_ref, k_ref, v_ref, w_ref, a_ref, b_ref, s_ref, dy_ref, *rest):
        h_in, d_refs, h_out = rest[:nhosted], rest[nhosted:nhosted + 6], rest[nhosted + 6:2 * nhosted + 6]
        dstate, sems = rest[2 * nhosted + 6], rest[2 * nhosted + 7:]
        _hosted_steps(hosted, h_in, h_out, sems, (nh // hb, nt), 0.5)

        @pl.when(pl.program_id(1) == 0)
        def _():
            dstate[...] = jnp.zeros_like(dstate)

        def step(i, ds):
            c = per - 1 - i
            rows = pl.ds(pl.multiple_of(c * RWKV_CHUNK, RWKV_CHUNK), RWKV_CHUNK)
            _, vjp = jax.vjp(_rwkv_chunk, tuple(s_ref[h, c] for h in range(hb)),
                             *(tuple(_split_heads(ref[rows, :], hb))
                               for ref in (r_ref, k_ref, v_ref, w_ref, a_ref, b_ref)))
            grads = vjp((ds, tuple(_split_heads(dy_ref[rows, :], hb))))
            for d_ref, g in zip(d_refs, grads[1:]):
                d_ref[rows, :] = jnp.concatenate(g, axis=1)
            return grads[0]

        final = lax.fori_loop(0, per, step, tuple(dstate[h] for h in range(hb)))
        for h in range(hb):
            dstate[h] = final[h]

    hs = jax.ShapeDtypeStruct((t, width), F32)
    return pl.pallas_call(
        body, name="rwkv_bwd", grid=(nh // hb, nt), in_specs=[row_spec] * 6 + [s_spec, row_spec] + h_specs,
        out_specs=[row_spec] * 6 + h_specs, out_shape=[hs] * 6 + h_shapes,
        scratch_shapes=[pltpu.VMEM((hb, HEAD_DIM, HEAD_DIM), F32)] + h_scratch,
        compiler_params=_params(("arbitrary", "arbitrary")),
    )(r, k, v, w, a, b, states, dy, *h_arrays)


SUBLANES = 8


def s5_scan(bu_re, bu_im, a_re, a_im, *, reverse=False, h_prev=None):
    t, n = bu_re.shape
    tt = _pick(t, (512, 256, 128, 64, 32, 16, 8))
    tc = _pick(n, (1024, 512, 256, 128))
    nt = t // tt
    with_da = h_prev is not None
    sign = -1.0 if reverse else 1.0

    def body(*refs):
        if with_da:
            br, bi, ar_ref, ai_ref, pr, pi, hr_ref, hi_ref, dar_ref, dai_ref, sr, si = refs
        else:
            br, bi, ar_ref, ai_ref, hr_ref, hi_ref, sr, si = refs
        ti = pl.program_id(1)

        @pl.when(ti == 0)
        def _():
            sr[...] = jnp.zeros_like(sr)
            si[...] = jnp.zeros_like(si)
            if with_da:
                dar_ref[...] = jnp.zeros_like(dar_ref)
                dai_ref[...] = jnp.zeros_like(dai_ref)

        ar = ar_ref[...]
        ai = ai_ref[...] * sign

        def group(gi, carry):
            hr, hi, dar, dai = carry
            g0 = pl.multiple_of((tt // SUBLANES - 1 - gi if reverse else gi) * SUBLANES, SUBLANES)
            rows = pl.ds(g0, SUBLANES)
            xr, xi = br[rows, :], bi[rows, :]
            if with_da:
                qr, qi = pr[rows, :], pi[rows, :]
            out_r, out_i = [None] * SUBLANES, [None] * SUBLANES
            for s in (range(SUBLANES - 1, -1, -1) if reverse else range(SUBLANES)):
                if with_da:
                    dar = dar + hr * qr[s:s + 1, :] + hi * qi[s:s + 1, :]
                    dai = dai + hi * qr[s:s + 1, :] - hr * qi[s:s + 1, :]
                hr, hi = ar * hr - ai * hi + xr[s:s + 1, :], ar * hi + ai * hr + xi[s:s + 1, :]
                out_r[s], out_i[s] = hr, hi
            hr_ref[rows, :] = jnp.concatenate(out_r, axis=0)
            hi_ref[rows, :] = jnp.concatenate(out_i, axis=0)
            return hr, hi, dar, dai

        zero = jnp.zeros((1, tc), F32)
        hr, hi, dar, dai = lax.fori_loop(0, tt // SUBLANES, group, (sr[...], si[...], zero, zero))
        sr[...] = hr
        si[...] = hi
        if with_da:
            dar_ref[...] += dar
            dai_ref[...] += dai

    tile = pl.BlockSpec((tt, tc), (lambda ci, ti: (nt - 1 - ti, ci)) if reverse else (lambda ci, ti: (ti, ci)))
    col = pl.BlockSpec((1, tc), lambda ci, ti: (0, ci))
    hs = jax.ShapeDtypeStruct((t, n), F32)
    cs = jax.ShapeDtypeStruct((1, n), F32)
    ins = [bu_re, bu_im, a_re, a_im] + (list(h_prev) if with_da else [])
    return pl.pallas_call(
        body, name="s5_scan_bwd" if reverse else "s5_scan_fwd", grid=(n // tc, nt),
        in_specs=[tile, tile, col, col] + ([tile, tile] if with_da else []),
        out_specs=[tile, tile] + ([col, col] if with_da else []), out_shape=[hs, hs] + ([cs, cs] if with_da else []),
        scratch_shapes=[pltpu.VMEM((1, tc), F32)] * 2, compiler_params=_params(("parallel", "arbitrary")),
    )(*ins)


N_CHIPS = 4
HBM_SPEC = pl.BlockSpec(memory_space=pl.ANY)


def _remote(src, dst, send_sem, recv_sem, device):
    return pltpu.make_async_remote_copy(src_ref=src, dst_ref=dst, send_sem=send_sem, recv_sem=recv_sem,
                                        device_id=device, device_id_type=pl.DeviceIdType.MESH)


class Exchange:
    def __init__(self, arrays, out_shape, n_sems, plan):
        self.arrays, self.out_shape, self.n_sems, self.plan = list(arrays), out_shape, n_sems, plan

    def scratch(self):
        nb = len(self.arrays)
        return [pltpu.SemaphoreType.DMA((nb, self.n_sems)), pltpu.SemaphoreType.DMA((nb, self.n_sems)),
                pltpu.SemaphoreType.DMA((nb,))]

    def run(self, name):
        nb = len(self.arrays)

        def body(*refs):
            for step in self.plan(refs[:nb], refs[nb:2 * nb], *refs[2 * nb:]):
                step()

        return pl.pallas_call(
            body, name=name, in_specs=[HBM_SPEC] * nb, out_specs=[HBM_SPEC] * nb, out_shape=self.out_shape,
            scratch_shapes=self.scratch(), compiler_params=pltpu.CompilerParams(has_side_effects=True),
        )(*self.arrays)


def gather_all(arrays):
    nb = len(arrays)

    def plan(ins, outs, send_sems, recv_sems, local_sems):
        x, y, c = (lax.axis_index(n) for n in MESH_AXES)
        me, sibling = 4 * x + 2 * y + c, 4 * x + 2 * y + 1 - c
        chips = [(1 - x, y), (x, 1 - y), (1 - x, 1 - y)]

        def copy(b, k, slot, to, src=None):
            block = outs[b].at[slot]
            return _remote(block if src is None else src, block, send_sems.at[b, k], recv_sems.at[b, k], to)

        def local():
            return [pltpu.make_async_copy(ins[b], outs[b].at[me], local_sems.at[b]) for b in range(nb)]

        def first():
            return [cp for b in range(nb) for cp in
                    [copy(b, 0, me, (x, y, 1 - c), src=ins[b])]
                    + [copy(b, 1 + j, me, (px, py, c), src=ins[b]) for j, (px, py) in enumerate(chips)]]

        def passed():
            return [copy(b, 4 + j, 4 * px + 2 * py + c, (x, y, 1 - c)) for j, (px, py) in enumerate(chips) for b in range(nb)]

        def start():
            for cp in local() + first():
                cp.start()

        def forward():
            for j, (px, py) in enumerate(chips):
                for b in range(nb):
                    copy(b, 1 + j, 4 * px + 2 * py + c, (px, py, c)).wait_recv()
                    copy(b, 4 + j, 4 * px + 2 * py + c, (x, y, 1 - c)).start()

        def finish():
            for b in range(nb):
                copy(b, 0, sibling, (x, y, 1 - c)).wait_recv()
                for j, (px, py) in enumerate(chips):
                    copy(b, 4 + j, 4 * px + 2 * py + 1 - c, (x, y, 1 - c)).wait_recv()
            for cp in first() + passed():
                cp.wait_send()
            for cp in local():
                cp.wait()

        return start, forward, finish

    out_shape = [jax.ShapeDtypeStruct((N_DEV,) + tuple(a.shape), a.dtype) for a in arrays]
    return Exchange(arrays, out_shape, N_DEV - 1, plan)


def _nothing():
    pass


def pair_exchange(arrays):
    nb = len(arrays)

    def plan(ins, outs, send_sems, recv_sems, local_sems):
        x, y, c = (lax.axis_index(n) for n in MESH_AXES)

        def copies():
            return [_remote(ins[b].at[2 * z + 1 - c], outs[b].at[z], send_sems.at[b, z], recv_sems.at[b, z], (x, y, 1 - c))
                    for b in range(nb) for z in range(N_CHIPS)]

        def start():
            for cp in copies():
                cp.start()

        def finish():
            for cp in copies():
                cp.wait_send()
            for cp in copies():
                cp.wait_recv()

        return start, _nothing, finish

    out_shape = [jax.ShapeDtypeStruct((N_CHIPS,) + tuple(a.shape[1:]), a.dtype) for a in arrays]
    return Exchange(arrays, out_shape, N_CHIPS, plan)


def pair_sum(name, pieces, from_sibling):
    _, r, c = pieces.shape
    tr = _row_tile(r, c * pieces.dtype.itemsize, 1 << 20)

    def body(p_ref, s_ref, o_ref):
        mine = p_ref[lax.axis_index("c")]
        o_ref[...] = (mine.astype(F32) + s_ref[...].astype(F32)).astype(o_ref.dtype)

    return pl.pallas_call(
        body, name=name, grid=(N_CHIPS, r // tr),
        in_specs=[pl.BlockSpec((None, 2, tr, c), lambda z, i: (z, 0, i, 0)), pl.BlockSpec((None, tr, c), lambda z, i: (z, i, 0))],
        out_specs=pl.BlockSpec((None, tr, c), lambda z, i: (z, i, 0)),
        out_shape=jax.ShapeDtypeStruct((N_CHIPS, r, c), pieces.dtype), compiler_params=_params(("parallel", "parallel")),
    )(pieces.reshape(N_CHIPS, 2, r, c), from_sibling)


def chip_exchange(arrays):
    nb = len(arrays)

    def plan(ins, outs, send_sems, recv_sems, local_sems):
        x, y, c = (lax.axis_index(n) for n in MESH_AXES)
        my_chip = 2 * x + y
        chips = [(1 - x, y), (x, 1 - y), (1 - x, 1 - y)]

        def local():
            return [pltpu.make_async_copy(ins[b].at[my_chip], outs[b].at[my_chip], local_sems.at[b]) for b in range(nb)]

        def copies():
            return [_remote(ins[b].at[2 * px + py], outs[b].at[my_chip], send_sems.at[b, j], recv_sems.at[b, j], (px, py, c))
                    for b in range(nb) for j, (px, py) in enumerate(chips)]

        def start():
            for cp in local() + copies():
                cp.start()

        def finish():
            for cp in copies():
                cp.wait_send()
            for cp in copies():
                cp.wait_recv()
            for cp in local():
                cp.wait()

        return start, _nothing, finish

    out_shape = [jax.ShapeDtypeStruct(a.shape, a.dtype) for a in arrays]
    return Exchange(arrays, out_shape, N_CHIPS - 1, plan)


def adam_update(name, pieces, w, m, v):
    r, c = w.shape
    n_pieces = pieces.shape[0]
    tr = _row_tile(r, 4 * c, 1 << 19)

    def body(p_ref, w_ref, m_ref, v_ref, g_ref, d_ref, mo_ref, vo_ref):
        g = p_ref[0].astype(F32)
        for j in range(1, n_pieces):
            g = g + p_ref[j].astype(F32)
        m_new = ADAM_B1 * m_ref[...] + (1.0 - ADAM_B1) * g
        v_new = ADAM_B2 * v_ref[...] + (1.0 - ADAM_B2) * (g * g)
        m_hat = m_new / (1.0 - ADAM_B1 ** ADAM_STEP)
        v_hat = v_new / (1.0 - ADAM_B2 ** ADAM_STEP)
        g_ref[...] = g
        d_ref[...] = -ADAM_LR * (m_hat / (jnp.sqrt(v_hat) + ADAM_EPS) + ADAM_WD * w_ref[...])
        mo_ref[...] = m_new
        vo_ref[...] = v_new

    row = pl.BlockSpec((tr, c), lambda i: (i, 0))
    out = jax.ShapeDtypeStruct((r, c), F32)
    return pl.pallas_call(
        body, name=name, grid=(r // tr,), in_specs=[pl.BlockSpec((n_pieces, tr, c), lambda i: (0, i, 0)), row, row, row],
        out_specs=[row] * 4, out_shape=[out] * 4, compiler_params=_params(("parallel",)),
    )(pieces, w, m, v)


PACK_WIDTH = 1024
PACK_ROWS = 64


def _pack(arrays, dtype, lead=0):
    parts = []
    for a in arrays:
        head = a.shape[:lead]
        f = a.reshape(head + (-1,)).astype(dtype)
        pad = (-f.shape[-1]) % PACK_WIDTH
        if pad:
            f = jnp.pad(f, [(0, 0)] * lead + [(0, pad)])
        parts.append(f.reshape(head + (-1, PACK_WIDTH)))
    out = jnp.concatenate(parts, axis=lead)
    pad = (-out.shape[lead]) % PACK_ROWS
    if pad:
        out = jnp.pad(out, [(0, 0)] * lead + [(0, pad), (0, 0)])
    return out


def _unpack(packed, shapes, lead=0):
    head = packed.shape[:lead]
    out, row = [], 0
    for s in shapes:
        n = int(np.prod(s))
        rows = -(-n // PACK_WIDTH)
        chunk = lax.slice_in_dim(packed, row, row + rows, axis=lead).reshape(head + (-1,))
        out.append(lax.slice_in_dim(chunk, 0, n, axis=lead).reshape(head + tuple(s)))
        row += rows
    return out


def _to_natural(stacked, kind):
    if kind == "col":
        m = jnp.moveaxis(stacked, 0, -2)
        return m.reshape(m.shape[:-2] + (m.shape[-2] * m.shape[-1],))
    m = jnp.moveaxis(stacked, 0, -3)
    return m.reshape(m.shape[:-3] + (m.shape[-3] * m.shape[-2], m.shape[-1]))


def _to_stacked(natural, kind):
    if kind == "col":
        m = natural.reshape(natural.shape[:-1] + (N_DEV, natural.shape[-1] // N_DEV))
        return jnp.moveaxis(m, -2, 0)
    m = natural.reshape(natural.shape[:-2] + (N_DEV, natural.shape[-2] // N_DEV, natural.shape[-1]))
    return jnp.moveaxis(m, -3, 0)


def _f_rms(h, g):
    return (_rms(h, g).astype(BF16),)


@jax.custom_vjp
def _swiglu(g, u):
    return g * _sigmoid(g) * u


def _swiglu_fwd(g, u):
    s = _sigmoid(g)
    return g * s * u, (g, u, s)


def _swiglu_bwd(saved, ct):
    g, u, s = saved
    return ct * u * (s * (1.0 + g * (1.0 - s))), ct * (g * s)


_swiglu.defvjp(_swiglu_fwd, _swiglu_bwd)


def _f_swiglu(g, u):
    return (_swiglu(g.astype(F32), u.astype(F32)).astype(BF16),)


def _f_ple(h, pre, pp):
    return (h + _sigmoid(pre) * pp,)


def _head_rms(x, g):
    ms = _seg_sum(x * x, HEAD_DIM) * (1.0 / HEAD_DIM)
    return x * _seg_expand(lax.rsqrt(ms + RMS_EPS), HEAD_DIM) * _tile_lanes(g, x.shape[1] // HEAD_DIM)


def _f_attpre(q, k, q_gain, k_gain):
    return _head_rms(q, q_gain) * (HEAD_DIM ** -0.5), _head_rms(k, k_gain)


def _f_shift(z, z_prev, mu):
    return (z + (z_prev - z) * mu,)


def _f_rwkvpre(r, k, v, xw, xa, xg, w0, w_up, a0, a_up, g_up, k_k, k_a):
    del r, v
    w_log = -_softplus(-(w0 + _mm(jnp.tanh(xw), w_up))) - 0.5
    decay = jnp.exp(-jnp.exp(w_log))
    a = _sigmoid(a0 + _mm(xa, a_up))
    g = _mm(_sigmoid(xg), g_up)
    kk = k * k_k
    norm = jnp.sqrt(_seg_expand(_seg_sum(kk * kk, HEAD_DIM), HEAD_DIM))
    kk = kk / jnp.maximum(norm, 1e-12)
    return k * (1.0 + (a - 1.0) * k_a), decay, -kk, kk * a, g


def _f_rwkvpost(y, r, k, v, g, lnx_w, lnx_b, r_k):
    mean = _seg_expand(_seg_sum(y, HEAD_DIM) * (1.0 / HEAD_DIM), HEAD_DIM)
    yc = y - mean
    var = _seg_expand(_seg_sum(yc * yc, HEAD_DIM) * (1.0 / HEAD_DIM), HEAD_DIM)
    yn = yc * lax.rsqrt(var + GN_EPS) * lnx_w + lnx_b
    bonus = _seg_expand(_seg_sum(r * k * r_k, HEAD_DIM), HEAD_DIM) * v
    return ((yn + bonus) * g,)


def _f_s5disc(lam_re, lam_im, log_dt):
    dt = jnp.exp(log_dt)
    mag = jnp.exp(lam_re * dt)
    ab_re, ab_im = mag * jnp.cos(lam_im * dt), mag * jnp.sin(lam_im * dt)
    denom = lam_re * lam_re + lam_im * lam_im
    z_re = ((ab_re - 1.0) * lam_re + ab_im * lam_im) / denom
    z_im = (ab_im * lam_re - (ab_re - 1.0) * lam_im) / denom
    return ab_re, ab_im, z_re, z_im


def _f_s5b(z_re, z_im, b_re, b_im):
    return z_re * b_re - z_im * b_im, z_re * b_im + z_im * b_re


def _f_s5post(ypre, u, d_skip):
    return (_gelu_tanh(ypre + d_skip * u).astype(BF16),)


def _f_glu(h, z1, z2):
    return (h + z1 * _sigmoid(z2),)


def _shift_down(x):
    return jnp.pad(x[:-1], ((1, 0), (0, 0)))


def _shift_up(x):
    return jnp.pad(x[1:], ((0, 1), (0, 0)))


def _block_diag(blocks):
    g, a, b = blocks.shape
    eye = jnp.eye(g, dtype=blocks.dtype)
    return (blocks[:, :, None, :] * eye[:, None, :, None]).reshape(g * a, g * b)


def _diag_blocks(dense, g):
    a, b = dense.shape[0] // g, dense.shape[1] // g
    return jnp.stack([dense[i * a:(i + 1) * a, i * b:(i + 1) * b] for i in range(g)], axis=0)


def _row_tile(t, width_bytes, budget=2 * 1024 * 1024):
    for tr in (512, 256, 128, 64, 32, 16, 8):
        if t % tr == 0 and tr * width_bytes <= budget:
            return tr
    return t


def _no_hosting(name):
    return None, None


def _hosting_matmul(hosting, name, a, b, **kw):
    exchange, done = hosting(name)
    if exchange is None:
        return matmul(name, a, b, **kw)
    out, travelled = matmul(name, a, b, hosted=exchange, **kw)
    done(travelled)
    return out


def _ffn_fwd(tag, h, norm_w, weight, hosting=_no_hosting):
    t, d = h.shape
    n = rowwise(f"{tag}_norm", _f_rms, [h], [norm_w], [(d, BF16)], _row_tile(t, 4 * d))[0]
    g = _hosting_matmul(hosting, f"{tag}_gate", n, weight("gate"), out_dtype=BF16)
    u = _hosting_matmul(hosting, f"{tag}_up", n, weight("up"), out_dtype=BF16)
    f = g.shape[1]
    a = rowwise(f"{tag}_act", _f_swiglu, [g, u], [], [(f, BF16)], _row_tile(t, 4 * f))[0]
    return _hosting_matmul(hosting, f"{tag}_down", a, weight("down"), alpha=0.5, res=h), (h, n, g, u, a)


def _ffn_bwd(tag, dh2, saved, norm_w, w_gate, w_up, w_down):
    h, n, g, u, a = saved
    t, d = h.shape
    f = w_gate.shape[1]
    da = matmul(f"{tag}_da", dh2, w_down, tb=True, alpha=0.5, out_dtype=BF16)
    d_down = matmul(f"{tag}_dwdown", a, dh2, ta=True, alpha=0.5, out_dtype=BF16)
    dg, du = rowwise_vjp(f"{tag}_dact", _f_swiglu, [g, u], [], [da], _row_tile(t, 4 * f, 1 << 20), [True, True], [],
                         grad_dtypes={0: BF16, 1: BF16})
    dn = matmul(f"{tag}_dn_gate", dg, w_gate, tb=True)
    dn = matmul(f"{tag}_dn_up", du, w_up, tb=True, res=dn)
    d_gate = matmul(f"{tag}_dwgate", n, dg, ta=True, out_dtype=BF16)
    d_up = matmul(f"{tag}_dwup", n, du, ta=True, out_dtype=BF16)
    dh, d_norm = rowwise_vjp(f"{tag}_dnorm", _f_rms, [h], [norm_w], [dn], _row_tile(t, 4 * d), [True], [True],
                             add_to={0: dh2})
    return dh, d_norm, d_gate, d_up, d_down


def _ple_fwd(tag, h, norm_w, w_gate, w_proj, p_i, hosting=_no_hosting):
    t, d = h.shape
    n = rowwise(f"{tag}_norm", _f_rms, [h], [norm_w], [(d, BF16)], _row_tile(t, 4 * d))[0]
    pre = _hosting_matmul(hosting, f"{tag}_gate", n, w_gate)
    pp = matmul(f"{tag}_proj", p_i, w_proj)
    h2 = rowwise(f"{tag}_out", _f_ple, [h, pre, pp], [], [(d, F32)], _row_tile(t, 4 * d))[0]
    return h2, (h, n, pre, pp)


def _ple_bwd(tag, dh2, saved, norm_w, w_gate, w_proj, p_i):
    h, n, pre, pp = saved
    t, d = h.shape
    dpre, dpp = rowwise_vjp(f"{tag}_dout", _f_ple, [h, pre, pp], [], [dh2], _row_tile(t, 4 * d), [False, True, True], [],
                            grad_dtypes={1: BF16, 2: BF16})
    d_proj = matmul(f"{tag}_dwproj", p_i, dpp, ta=True, out_dtype=BF16)
    d_gate = matmul(f"{tag}_dwgate", n, dpre, ta=True, out_dtype=BF16)
    dn = matmul(f"{tag}_dn", dpre, w_gate, tb=True)
    dh, d_norm = rowwise_vjp(f"{tag}_dnorm", _f_rms, [h], [norm_w], [dn], _row_tile(t, 4 * d), [True], [True],
                             add_to={0: dh2})
    return dh, d_norm, d_gate, d_proj


def _ab_fwd(h, w, hosting=_no_hosting):
    t, d = h.shape
    da = d // 2
    n = rowwise("ab_norm", _f_rms, [h], [w["mix_norm"]], [(d, BF16)], _row_tile(t, 4 * d))[0]
    proj = _hosting_matmul(hosting, "ab_in", n, w["ab_w_in"])
    q_raw, k_raw, v_att, z = proj[:, :da], proj[:, da:2 * da], proj[:, 2 * da:3 * da], proj[:, 3 * da:]
    tr = _row_tile(t, 4 * da, 1 << 19)
    qn, kn = rowwise("att_pre", _f_attpre, [q_raw, k_raw], [w["att_q_gain"], w["att_k_gain"]], [(da, F32)] * 2, tr)
    bias = jnp.transpose(bias_expand(w["att_rel_bias"]), (1, 0, 2))
    qh, kh, vh = qn, kn, v_att
    exchange, done = hosting("attention_fwd")
    att = attention_fwd(qh, kh, vh, bias, hosted=exchange)
    if exchange is not None:
        att, travelled = att
        done(travelled)
    z_prev = _shift_down(z)
    zz = rowwise("rwkv_shift", _f_shift, [z, z_prev], [w["rwkv_mu"]], [(z.shape[1], F32)], _row_tile(t, 4 * z.shape[1]))[0]
    o = [0, da, 2 * da, 3 * da, 3 * da + DECAY_LORA, 3 * da + DECAY_LORA + AAA_LORA, z.shape[1]]
    r, k, v, xw, xa, xg = (zz[:, o[i]:o[i + 1]] for i in range(6))
    pre_consts = [w[nm] for nm in ("rwkv_w0", "rwkv_w_up", "rwkv_a0", "rwkv_a_up", "rwkv_g_up", "rwkv_k_k", "rwkv_k_a")]
    k2, decay, ia, ib, g = rowwise("rwkv_pre", _f_rwkvpre, [r, k, v, xw, xa, xg], pre_consts, [(da, F32)] * 5, tr)
    heads = [r, k2, v, decay, ia, ib]
    exchange, done = hosting("rwkv_fwd")
    y, states, *travelled = rwkv_fwd(*heads, hosted=exchange)
    if exchange is not None:
        done(travelled)
    post_consts = [w["rwkv_lnx_w"], w["rwkv_lnx_b"], w["rwkv_r_k"]]
    rw = rowwise("rwkv_post", _f_rwkvpost, [y, r, k2, v, g], post_consts, [(da, F32)], tr)[0]
    cat = jnp.concatenate([att, rw], axis=1).astype(BF16)
    h2 = _hosting_matmul(hosting, "ab_out", cat, w["ab_w_out"], res=h)
    saved = dict(h=h, n=n, q_raw=q_raw, k_raw=k_raw, qh=qh, kh=kh, vh=vh, bias=bias, z=z, z_prev=z_prev,
                 rows=(r, k, v, xw, xa, xg), pre_consts=pre_consts, post=(y, r, k2, v, g), post_consts=post_consts,
                 heads=heads, states=states, cat=cat, tr=tr)
    return h2, saved


def _ab_bwd(dh2, s, w, make_hosted=None, hosted_done=None):
    h = s["h"]
    t, d = h.shape
    da = d // 2
    tr = s["tr"]
    grads = {}
    dcat = matmul("ab_dcat", dh2, w["ab_w_out"], tb=True)
    grads["ab_w_out"] = matmul("ab_dwout", s["cat"], dh2, ta=True, out_dtype=BF16)
    d_att, d_rw = dcat[:, :da], dcat[:, da:]
    dy, dr1, dk1, dv1, dg, grads["rwkv_lnx_w"], grads["rwkv_lnx_b"], grads["rwkv_r_k"] = rowwise_vjp(
        "rwkv_dpost", _f_rwkvpost, list(s["post"]), s["post_consts"], [d_rw], tr, [True] * 5, [True] * 3)
    hosted = None if make_hosted is None else make_hosted(grads["ab_w_out"])
    *d_heads, = rwkv_bwd(*s["heads"], s["states"], dy, hosted=hosted)
    if hosted is not None:
        hosted_done(d_heads[6:])
        d_heads = d_heads[:6]
    dr2, dk2, dv2, ddecay, dia, dib = d_heads
    pre = rowwise_vjp("rwkv_dpre", _f_rwkvpre, list(s["rows"]), s["pre_consts"], [(dk1, dk2), ddecay, dia, dib, dg], tr,
                      [True] * 6, [True] * 7, add_to={0: (dr1, dr2), 2: (dv1, dv2)})
    for nm, g in zip(("rwkv_w0", "rwkv_w_up", "rwkv_a0", "rwkv_a_up", "rwkv_g_up", "rwkv_k_k", "rwkv_k_a"), pre[6:]):
        grads[nm] = g
    dzz = jnp.concatenate(pre[:6], axis=1)
    trz = _row_tile(t, 4 * dzz.shape[1])
    grads["rwkv_mu"] = rowwise_vjp("rwkv_dmu", _f_shift, [s["z"], s["z_prev"]], [w["rwkv_mu"]], [dzz], trz,
                                   [False, False], [True])[0]
    dz = rowwise("rwkv_dshift", _f_shift, [dzz, _shift_up(dzz)], [w["rwkv_mu"]], [(dzz.shape[1], F32)], trz)[0]
    dqh, dkh, dvh, dbias = attention_bwd(s["qh"], s["kh"], s["vh"], s["bias"], d_att)
    grads["att_rel_bias"] = bias_reduce(jnp.transpose(dbias, (1, 0, 2)))
    dq_raw, dk_raw, grads["att_q_gain"], grads["att_k_gain"] = rowwise_vjp(
        "att_dpre", _f_attpre, [s["q_raw"], s["k_raw"]], [w["att_q_gain"], w["att_k_gain"]],
        [dqh, dkh], tr, [True, True], [True, True])
    dproj = jnp.concatenate([dq_raw, dk_raw, dvh, dz], axis=1).astype(BF16)
    dn = matmul("ab_dn", dproj, w["ab_w_in"], tb=True)
    grads["ab_w_in"] = matmul("ab_dwin", s["n"], dproj, ta=True, out_dtype=BF16)
    dh, grads["mix_norm"] = rowwise_vjp("ab_dnorm", _f_rms, [h], [w["mix_norm"]], [dn], _row_tile(t, 4 * d), [True], [True],
                                        add_to={0: dh2})
    return dh, grads


def _s5_fwd(h, w):
    t, d = h.shape
    n_groups, n_state = w["ssm_lambda_re"].shape
    gp = n_groups * n_state
    n = rowwise("s5_norm", _f_rms, [h], [w["mix_norm"]], [(d, BF16)], _row_tile(t, 4 * d))[0]
    u = matmul("s5_in", n, w["ssm_w_in"])
    disc_rows = [w["ssm_lambda_re"], w["ssm_lambda_im"], w["ssm_log_dt"]]
    ab_re, ab_im, z_re, z_im = rowwise("s5_disc", _f_s5disc, disc_rows, [], [(n_state, F32)] * 4, n_groups)
    b_rows = [z_re.reshape(gp, 1), z_im.reshape(gp, 1), w["ssm_b_re"], w["ssm_b_im"]]
    trb = _row_tile(gp, 512)
    bb_re, bb_im = rowwise("s5_bbar", _f_s5b, b_rows, [], [(SSM_GROUP, F32)] * 2, trb)
    to_dense = lambda bb: _block_diag(jnp.transpose(bb.reshape(n_groups, n_state, SSM_GROUP), (0, 2, 1)))
    bd_re, bd_im = to_dense(bb_re), to_dense(bb_im)
    cd_re = _block_diag(jnp.transpose(w["ssm_c_re"], (0, 2, 1)))
    cd_im = -_block_diag(jnp.transpose(w["ssm_c_im"], (0, 2, 1)))
    a_re, a_im = ab_re.reshape(1, gp), ab_im.reshape(1, gp)
    bu_re = matmul("s5_bu_re", u, bd_re)
    bu_im = matmul("s5_bu_im", u, bd_im)
    h_re, h_im = s5_scan(bu_re, bu_im, a_re, a_im)
    ypre = matmul("s5_y_re", h_re, cd_re)
    ypre = matmul("s5_y_im", h_im, cd_im, res=ypre)
    tru = _row_tile(t, 4 * u.shape[1])
    yg = rowwise("s5_post", _f_s5post, [ypre, u], [w["ssm_d"]], [(u.shape[1], BF16)], tru)[0]
    w_out1, w_out2 = w["ssm_w_out"][:, :d], w["ssm_w_out"][:, d:]
    z1 = matmul("s5_out1", yg, w_out1)
    z2 = matmul("s5_out2", yg, w_out2)
    h2 = rowwise("s5_glu", _f_glu, [h, z1, z2], [], [(d, F32)], _row_tile(t, 4 * d))[0]
    saved = dict(h=h, n=n, u=u, disc_rows=disc_rows, b_rows=b_rows, trb=trb, bd=(bd_re, bd_im), cd=(cd_re, cd_im),
                 a=(a_re, a_im), hs=(h_re, h_im), ypre=ypre, yg=yg, w_out=(w_out1, w_out2), z=(z1, z2), tru=tru)
    return h2, saved


def _s5_bwd(dh2, s, w):
    h, u = s["h"], s["u"]
    t, d = h.shape
    n_groups, n_state = w["ssm_lambda_re"].shape
    gp = n_groups * n_state
    grads = {}
    z1, z2 = s["z"]
    w_out1, w_out2 = s["w_out"]
    dz1, dz2 = rowwise_vjp("s5_dglu", _f_glu, [h, z1, z2], [], [dh2], _row_tile(t, 4 * d), [False, True, True], [],
                           grad_dtypes={1: BF16, 2: BF16})
    dyg = matmul("s5_dyg1", dz1, w_out1, tb=True)
    dyg = matmul("s5_dyg2", dz2, w_out2, tb=True, res=dyg)
    grads["ssm_w_out"] = jnp.concatenate([matmul("s5_dwout1", s["yg"], dz1, ta=True, out_dtype=BF16),
                                          matmul("s5_dwout2", s["yg"], dz2, ta=True, out_dtype=BF16)], axis=1)
    dypre, du1, grads["ssm_d"] = rowwise_vjp("s5_dpost", _f_s5post, [s["ypre"], u], [w["ssm_d"]], [dyg], s["tru"],
                                             [True, True], [True])
    cd_re, cd_im = s["cd"]
    h_re, h_im = s["hs"]
    dh_re = matmul("s5_dh_re", dypre, cd_re, tb=True)
    dh_im = matmul("s5_dh_im", dypre, cd_im, tb=True)
    dcd_re = matmul("s5_dc_re", h_re, dypre, ta=True)
    dcd_im = matmul("s5_dc_im", h_im, dypre, ta=True)
    a_re, a_im = s["a"]
    g_re, g_im, da_re, da_im = s5_scan(dh_re, dh_im, a_re, a_im, reverse=True, h_prev=(h_re, h_im))
    bd_re, bd_im = s["bd"]
    du = matmul("s5_du_re", g_re, bd_re, tb=True, res=du1)
    du = matmul("s5_du_im", g_im, bd_im, tb=True, res=du)
    dbd_re = matmul("s5_db_re", u, g_re, ta=True)
    dbd_im = matmul("s5_db_im", u, g_im, ta=True)
    grads["ssm_w_in"] = matmul("s5_dwin", s["n"], du, ta=True, out_dtype=BF16)
    dn = matmul("s5_dn", du, w["ssm_w_in"], tb=True)
    dh, grads["mix_norm"] = rowwise_vjp("s5_dnorm", _f_rms, [h], [w["mix_norm"]], [dn], _row_tile(t, 4 * d), [True], [True],
                                        add_to={0: dh2})
    from_dense = lambda m: jnp.transpose(_diag_blocks(m, n_groups), (0, 2, 1)).reshape(gp, SSM_GROUP)
    dz_re, dz_im, grads["ssm_b_re"], grads["ssm_b_im"] = rowwise_vjp(
        "s5_dbbar", _f_s5b, s["b_rows"], [], [from_dense(dbd_re), from_dense(dbd_im)], s["trb"], [True] * 4, [])
    disc_cots = [da_re.reshape(n_groups, n_state), da_im.reshape(n_groups, n_state),
                 dz_re.reshape(n_groups, n_state), dz_im.reshape(n_groups, n_state)]
    grads["ssm_lambda_re"], grads["ssm_lambda_im"], grads["ssm_log_dt"] = rowwise_vjp(
        "s5_ddisc", _f_s5disc, s["disc_rows"], [], disc_cots, n_groups, [True] * 3, [])
    grads["ssm_c_re"] = jnp.transpose(_diag_blocks(dcd_re, n_groups), (0, 2, 1))
    grads["ssm_c_im"] = -jnp.transpose(_diag_blocks(dcd_im, n_groups), (0, 2, 1))
    return dh, grads


WEIGHTS = ["ffn1_norm", "ffn1_w_gate", "ffn1_w_up", "ffn1_w_down", "mix_norm", "ffn2_norm", "ffn2_w_gate", "ffn2_w_up",
           "ffn2_w_down", "ple_norm", "ple_w_gate", "ple_w_proj", "ab_w_in", "att_q_gain", "att_k_gain", "att_rel_bias",
           "rwkv_mu", "rwkv_w0", "rwkv_w_up", "rwkv_a0", "rwkv_a_up", "rwkv_g_up", "rwkv_k_k", "rwkv_k_a", "rwkv_r_k",
           "rwkv_lnx_w", "rwkv_lnx_b", "ab_w_out", "ssm_w_in", "ssm_lambda_re", "ssm_lambda_im", "ssm_log_dt", "ssm_b_re",
           "ssm_b_im", "ssm_c_re", "ssm_c_im", "ssm_d", "ssm_w_out"]
BIG = {"ffn1_w_gate": "col", "ffn1_w_up": "col", "ffn1_w_down": "row", "ffn2_w_gate": "col", "ffn2_w_up": "col",
       "ffn2_w_down": "row", "ple_w_gate": "row", "ple_w_proj": "col", "ab_w_in": "col", "ab_w_out": "row",
       "ssm_w_in": "row", "ssm_w_out": "col"}
SMALL_CUT = {"rwkv_w_up": "col", "rwkv_a_up": "col", "rwkv_g_up": "col", "ssm_d": "col"}
REPLICATED = [n for n in WEIGHTS if n not in BIG and n not in SMALL_CUT]


def kernel(x, p, ffn1_norm, ffn1_w_gate, ffn1_w_up, ffn1_w_down, mix_norm, ffn2_norm, ffn2_w_gate, ffn2_w_up, ffn2_w_down, ple_norm, ple_w_gate, ple_w_proj, ab_w_in, att_q_gain, att_k_gain, att_rel_bias, rwkv_mu, rwkv_w0, rwkv_w_up, rwkv_a0, rwkv_a_up, rwkv_g_up, rwkv_k_k, rwkv_k_a, rwkv_r_k, rwkv_lnx_w, rwkv_lnx_b, ab_w_out, ssm_w_in, ssm_lambda_re, ssm_lambda_im, ssm_log_dt, ssm_b_re, ssm_b_im, ssm_c_re, ssm_c_im, ssm_d, ssm_w_out, loss_target, m_ffn1_norm, m_ffn1_w_gate, m_ffn1_w_up, m_ffn1_w_down, m_mix_norm, m_ffn2_norm, m_ffn2_w_gate, m_ffn2_w_up, m_ffn2_w_down, m_ple_norm, m_ple_w_gate, m_ple_w_proj, m_ab_w_in, m_att_q_gain, m_att_k_gain, m_att_rel_bias, m_rwkv_mu, m_rwkv_w0, m_rwkv_w_up, m_rwkv_a0, m_rwkv_a_up, m_rwkv_g_up, m_rwkv_k_k, m_rwkv_k_a, m_rwkv_r_k, m_rwkv_lnx_w, m_rwkv_lnx_b, m_ab_w_out, m_ssm_w_in, m_ssm_lambda_re, m_ssm_lambda_im, m_ssm_log_dt, m_ssm_b_re, m_ssm_b_im, m_ssm_c_re, m_ssm_c_im, m_ssm_d, m_ssm_w_out, v_ffn1_norm, v_ffn1_w_gate, v_ffn1_w_up, v_ffn1_w_down, v_mix_norm, v_ffn2_norm, v_ffn2_w_gate, v_ffn2_w_up, v_ffn2_w_down, v_ple_norm, v_ple_w_gate, v_ple_w_proj, v_ab_w_in, v_att_q_gain, v_att_k_gain, v_att_rel_bias, v_rwkv_mu, v_rwkv_w0, v_rwkv_w_up, v_rwkv_a0, v_rwkv_a_up, v_rwkv_g_up, v_rwkv_k_k, v_rwkv_k_a, v_rwkv_r_k, v_rwkv_lnx_w, v_rwkv_lnx_b, v_ab_w_out, v_ssm_w_in, v_ssm_lambda_re, v_ssm_lambda_im, v_ssm_log_dt, v_ssm_b_re, v_ssm_b_im, v_ssm_c_re, v_ssm_c_im, v_ssm_d, v_ssm_w_out):
    vals = dict(locals())
    depth = ffn1_norm.shape[0]
    n_groups, n_state = ssm_lambda_re.shape[1:]

    kinds = {**BIG, **SMALL_CUT}
    travel = {n: (BF16 if n in BIG else F32) for n in kinds}

    def rows(a, lead=0):
        return a.reshape(a.shape[:lead] + (-1, a.shape[-1]))

    def model_layer(n, j):
        if vals[n].shape[0] == depth:
            return j
        return 2 * j if n.startswith(("ab_", "rwkv_")) else 2 * j + 1

    units = [(n, j) for n in kinds for j in range(vals[n].shape[0])]
    early_names = ("ffn1_w_gate", "ffn1_w_up", "ffn1_w_down", "ab_w_in", "rwkv_w_up", "rwkv_a_up", "rwkv_g_up")
    early = [u for u in units if model_layer(*u) == 0 and u[0] in early_names]
    late = [u for u in units if u not in early]
    full = {}

    def shard(unit):
        n, j = unit
        return rows(vals[n][j]).astype(travel[n])

    def take_gathered(which, got):
        for (n, j), g in zip(which, got):
            full[(n, j)] = _to_natural(g.reshape((N_DEV,) + vals[n].shape[1:]), kinds[n])

    first = [("ffn1_w_gate", 0), ("ffn1_w_up", 0)]
    beside = {"l0_ffn1_gate": [("ffn1_w_down", 0)],
              "l0_ffn1_up": [("ab_w_in", 0), ("rwkv_w_up", 0), ("rwkv_a_up", 0), ("rwkv_g_up", 0)],
              "l0_ffn1_down": [("ab_w_out", 0), ("ffn2_w_gate", 0)],
              "ab_in": [("ffn2_w_up", 0)],
              "attention_fwd": [("ffn2_w_down", 0), ("ple_w_gate", 0), ("ple_w_proj", 0)],
              "l0_ffn2_gate": [("ffn2_w_gate", 1)], "l0_ffn2_up": [("ffn2_w_up", 1)], "l0_ffn2_down": [("ffn2_w_down", 1)],
              "l0_ple_gate": [("ple_w_gate", 1), ("ple_w_proj", 1)]}
    planned = first + [u for us in beside.values() for u in us]
    beside["rwkv_fwd"] = [u for u in units if u not in planned]
    assert depth == 2 and sorted(planned + beside["rwkv_fwd"]) == sorted(units)
    take_gathered(first, gather_all([shard(u) for u in first]).run("gather_first"))

    def hosting(name):
        which = beside.get(name)
        if not which:
            return None, None
        return gather_all([shard(u) for u in which]), functools.partial(take_gathered, which)

    def row(name, j):
        return vals[name][j].reshape(1, -1)

    def ffn_weights(which, i):
        return (row(f"{which}_norm", i), full[(f"{which}_w_gate", i)], full[(f"{which}_w_up", i)],
                full[(f"{which}_w_down", i)])

    def mixer_weights(i):
        j = i // 2
        if i % 2 == 0:
            w = {n: row(n, j) for n in ("att_q_gain", "att_k_gain", "rwkv_mu", "rwkv_w0", "rwkv_a0", "rwkv_k_k", "rwkv_k_a",
                                        "rwkv_r_k", "rwkv_lnx_w", "rwkv_lnx_b")}
            w.update({n: full.get((n, j)) for n in ("ab_w_in", "ab_w_out", "rwkv_w_up", "rwkv_a_up", "rwkv_g_up")})
            w["att_rel_bias"] = att_rel_bias[j]
        else:
            w = {"ssm_lambda_re": ssm_lambda_re[j], "ssm_lambda_im": ssm_lambda_im[j],
                 "ssm_log_dt": ssm_log_dt[j].reshape(n_groups, 1),
                 "ssm_b_re": ssm_b_re[j].reshape(n_groups * n_state, -1),
                 "ssm_b_im": ssm_b_im[j].reshape(n_groups * n_state, -1),
                 "ssm_c_re": ssm_c_re[j], "ssm_c_im": ssm_c_im[j], "ssm_d": full[("ssm_d", j)].reshape(1, -1),
                 "ssm_w_in": full[("ssm_w_in", j)], "ssm_w_out": full[("ssm_w_out", j)]}
        w["mix_norm"] = row("mix_norm", i)
        return w

    def ple_weights(i):
        return (row("ple_norm", i), full[("ple_w_gate", i)], full[("ple_w_proj", i)], p[i, 0])

    h = x[0]
    saved = []
    for i in range(depth):
        h, s1 = _ffn_fwd(f"l{i}_ffn1", h, row("ffn1_norm", i), lambda kind, i=i: full[(f"ffn1_w_{kind}", i)], hosting)
        h, sm = _ab_fwd(h, mixer_weights(i), hosting) if i % 2 == 0 else _s5_fwd(h, mixer_weights(i))
        h, s2 = _ffn_fwd(f"l{i}_ffn2", h, row("ffn2_norm", i), lambda kind, i=i: full[(f"ffn2_w_{kind}", i)], hosting)
        h, sp = _ple_fwd(f"l{i}_ple", h, *ple_weights(i), hosting)
        saved.append((s1, sm, s2, sp))
    dh, loss_part = loss_head(h, loss_target[0])

    per_layer = {n: [] for n in WEIGHTS}
    received = {}

    def chip_sums(which, tag):
        pieces = [rows(_to_stacked(per_layer[n][j - vals[n].shape[0]], kinds[n]).astype(travel[n]), lead=1)
                  for n, j in which]
        from_sibling = pair_exchange(pieces).run(f"reduce_pair_{tag}")
        return [pair_sum(f"pair_sum_{n}{j}", a, b) for (n, j), a, b in zip(which, pieces, from_sibling)]

    def late_reduce(d_ab_w_out):
        per_layer["ab_w_out"].insert(0, d_ab_w_out)
        return chip_exchange(chip_sums(late, "late"))

    def late_reduce_done(got):
        received.update(zip(late, got))

    for i in reversed(range(depth)):
        s1, sm, s2, sp = saved[i]
        dh, d_norm, d_gate, d_proj = _ple_bwd(f"l{i}_ple", dh, sp, *ple_weights(i))
        for n, g in (("ple_norm", d_norm), ("ple_w_gate", d_gate), ("ple_w_proj", d_proj)):
            per_layer[n].insert(0, g)
        dh, d_norm, d_gate, d_up, d_down = _ffn_bwd(f"l{i}_ffn2", dh, s2, *ffn_weights("ffn2", i))
        for n, g in (("ffn2_norm", d_norm), ("ffn2_w_gate", d_gate), ("ffn2_w_up", d_up), ("ffn2_w_down", d_down)):
            per_layer[n].insert(0, g)
        if i == 0:
            dh, mixer_grads = _ab_bwd(dh, sm, mixer_weights(i), late_reduce, late_reduce_done)
        else:
            dh, mixer_grads = (_ab_bwd if i % 2 == 0 else _s5_bwd)(dh, sm, mixer_weights(i))
        for n, g in mixer_grads.items():
            if not (i == 0 and n == "ab_w_out"):
                per_layer[n].insert(0, g)
        dh, d_norm, d_gate, d_up, d_down = _ffn_bwd(f"l{i}_ffn1", dh, s1, *ffn_weights("ffn1", i))
        for n, g in (("ffn1_norm", d_norm), ("ffn1_w_gate", d_gate), ("ffn1_w_up", d_up), ("ffn1_w_down", d_down)):
            per_layer[n].insert(0, g)
    grad_x = dh[None]

    received.update(zip(early, chip_exchange(chip_sums(early, "early")).run("reduce_chips_early")))

    rep_mine = _pack([jnp.stack(per_layer[n], axis=0).reshape(vals[n].shape) for n in REPLICATED] + [loss_part], F32)
    rep_all = gather_all([rep_mine]).run("gather_replicated")[0]

    out = {}
    for n in kinds:
        got = [received[(n, j)] for j in range(vals[n].shape[0])]
        got = got[0] if len(got) == 1 else jnp.concatenate(got, axis=1)
        state = [rows(vals[pre + n]) for pre in ("", "m_", "v_")]
        out[n] = tuple(r.reshape(vals[n].shape) for r in adam_update(f"adam_{n}", got, *state))
    zero = jnp.zeros((1, 1), F32)
    state = [_pack([vals[pre + n] for n in REPLICATED] + [zero], F32) for pre in ("", "m_", "v_")]
    shapes = [vals[n].shape for n in REPLICATED] + [zero.shape]
    results = [_unpack(r, shapes) for r in adam_update("adam_replicated", rep_all, *state)]
    for q, n in enumerate(REPLICATED):
        out[n] = tuple(r[q] for r in results)
    loss = results[0][-1].reshape(())
    return (loss, grad_x, *[out[n][0] for n in WEIGHTS], *[out[n][1] for n in WEIGHTS], *[out[n][2] for n in WEIGHTS],
            *[out[n][3] for n in WEIGHTS])
```

```python
import functools
import math

import jax
import jax.numpy as jnp
import numpy as np
from jax import lax
from jax.experimental import pallas as pl
from jax.experimental.pallas import tpu as pltpu

F32 = jnp.float32
BF16 = jnp.bfloat16
HIGHEST = lax.Precision.HIGHEST
MESH_AXES = ("x", "y", "c")
N_DEV = 8

CHUNK = 64
N_LEFT_CHUNKS = 8
BAND = (N_LEFT_CHUNKS + 1) * CHUNK
HEAD_DIM = 64
REL_CLIP = 128
N_REL = (CHUNK - 1) + REL_CLIP + 1
DECAY_LORA = 64
AAA_LORA = 64
GATE_LORA = 128
SSM_GROUP = 16
SSM_STATE = 64
RMS_EPS = 1e-6
GN_EPS = 64e-5
ADAM_LR = 0.001
ADAM_B1 = 0.9
ADAM_B2 = 0.999
ADAM_EPS = 1e-08
ADAM_WD = 0.01
ADAM_STEP = 10

RWKV_CHUNK = 64
VMEM_LIMIT = 56 * 1024 * 1024
LANES = 128


def _params(semantics):
    return pltpu.CompilerParams(dimension_semantics=semantics, vmem_limit_bytes=VMEM_LIMIT)


def _pick(n, prefs):
    for t in prefs:
        if n % t == 0:
            return t
    return n


def _dot(a, b, dims):
    return lax.dot_general(a, b, (dims, ((), ())), precision=HIGHEST, preferred_element_type=F32)


def _mm(a, b):
    return _dot(a, b, ((1,), (0,)))


def _mm_nt(a, b):
    return _dot(a, b, ((1,), (1,)))


def _mm_tn(a, b):
    return _dot(a, b, ((0,), (0,)))


def _split2(x):
    hi = x.astype(BF16)
    return hi, (x - hi.astype(F32)).astype(BF16)


def _dot3_raw(a, b, dims):
    a_hi, a_lo = _split2(a)
    b_hi, b_lo = _split2(b)
    dot = lambda p, q: lax.dot_general(p, q, (dims, ((), ())), preferred_element_type=F32)
    return dot(a_hi, b_hi) + (dot(a_hi, b_lo) + dot(a_lo, b_hi))


def _dot1_raw(a, b, dims):
    return lax.dot_general(a.astype(BF16), b.astype(BF16), (dims, ((), ())), preferred_element_type=F32)


NN, NT, TN = ((1,), (0,)), ((1,), (1,)), ((0,), (0,))


def _make_dot(raw):
    @functools.partial(jax.custom_vjp, nondiff_argnums=(2,))
    def dot(a, b, dims):
        return raw(a, b, dims)

    def fwd(a, b, dims):
        return raw(a, b, dims), (a, b)

    def bwd(dims, saved, g):
        a, b = saved
        if dims == NN:
            return raw(g, b, NT), raw(a, g, TN)
        if dims == NT:
            return raw(g, b, NN), raw(g, a, TN)
        return raw(b, g, NT), raw(a, g, NN)

    dot.defvjp(fwd, bwd)
    return dot


_dot3 = _make_dot(_dot3_raw)
_dot1 = _make_dot(_dot1_raw)


def _dot_ind_raw(x, ind, dims):
    hi = x.astype(BF16)
    rest = x - hi.astype(F32)
    mid = rest.astype(BF16)
    lo = (rest - mid.astype(F32)).astype(BF16)
    ind = ind.astype(BF16)
    dot = lambda p: lax.dot_general(p, ind, (dims, ((), ())), preferred_element_type=F32)
    return dot(hi) + (dot(mid) + dot(lo))


@jax.custom_vjp
def _mm_ind(x, ind):
    return _dot_ind_raw(x, ind, NN)


def _mm_ind_fwd(x, ind):
    return _dot_ind_raw(x, ind, NN), ind


def _mm_ind_bwd(ind, g):
    return _dot_ind_raw(g, ind, NT), jnp.zeros_like(ind)


_mm_ind.defvjp(_mm_ind_fwd, _mm_ind_bwd)


def _sigmoid(x):
    return 1.0 / (1.0 + jnp.exp(-x))


def _softplus(x):
    return jnp.maximum(x, 0.0) + jnp.log(1.0 + jnp.exp(-jnp.abs(x)))


def _gelu_tanh(x):
    return 0.5 * x * (1.0 + jnp.tanh(math.sqrt(2.0 / math.pi) * (x + 0.044715 * (x * x * x))))


def _seg_indicator(n, seg):
    r = lax.broadcasted_iota(jnp.int32, (n, n // seg), 0)
    c = lax.broadcasted_iota(jnp.int32, (n, n // seg), 1)
    return jnp.where((r >= c * seg) & (r < (c + 1) * seg), 1.0, 0.0).astype(F32)


def _seg_indicator_t(n, seg):
    c = lax.broadcasted_iota(jnp.int32, (n // seg, n), 0)
    r = lax.broadcasted_iota(jnp.int32, (n // seg, n), 1)
    return jnp.where((r >= c * seg) & (r < (c + 1) * seg), 1.0, 0.0).astype(F32)


def _seg_sum(x, seg):
    return _mm_ind(x, _seg_indicator(x.shape[1], seg))


def _seg_expand(s, seg):
    return _mm_ind(s, _seg_indicator_t(s.shape[1] * seg, seg))


def _tile_lanes(g, reps):
    n = g.shape[1]
    r = lax.broadcasted_iota(jnp.int32, (n, n * reps), 0)
    c = lax.broadcasted_iota(jnp.int32, (n, n * reps), 1)
    return _mm_ind(g, jnp.where((c & (n - 1)) == r, 1.0, 0.0).astype(F32))


def _rms(x, g):
    return x * lax.rsqrt(jnp.mean(x * x, axis=-1, keepdims=True) + RMS_EPS) * g


MATMUL_VMEM_BUDGET = 40 * 1024 * 1024
MATMUL_MAX_TILE = 2048
HBM_BYTES_PER_S = 1.2e12
MXU_FLOPS_PER_S = 8e14
GRID_STEP_S = 0.35e-6


def _tile_candidates(dim):
    c = [d for d in range(LANES, min(dim, MATMUL_MAX_TILE) + 1, LANES) if dim % d == 0]
    return c or [dim]


def _matmul_tiles(m, n, k, a_bytes, b_bytes, out_bytes, res_bytes):
    best = None
    for tm in _tile_candidates(m):
        for tn in _tile_candidates(n):
            for tk in _tile_candidates(k):
                casts = (tm * tk * 2 if a_bytes > 2 else 0) + (tk * tn * 2 if b_bytes > 2 else 0)
                vmem = (2 * (tm * tk * a_bytes + tk * tn * b_bytes + tm * tn * (out_bytes + res_bytes))
                        + 2 * tm * tn * 4 + casts)
                if vmem > MATMUL_VMEM_BUDGET:
                    continue
                steps = (m // tm) * (n // tn) * (k // tk)
                a_reads = 1 if k == tk else n // tn
                traffic = (m * k * a_bytes * a_reads + k * n * b_bytes * (m // tm) + m * n * (out_bytes + res_bytes))
                dma_s = traffic / HBM_BYTES_PER_S
                ends_s = (tm * tk * a_bytes + tk * tn * b_bytes + tm * tn * out_bytes) / HBM_BYTES_PER_S
                cost = max(dma_s, 2.0 * m * n * k / MXU_FLOPS_PER_S) + 0.2 * dma_s + steps * GRID_STEP_S + ends_s
                if best is None or cost < best[0]:
                    best = (cost, tm, tn, tk)
    return best[1:]


def matmul(name, a, b, *, ta=False, tb=False, alpha=1.0, res=None, out_dtype=F32, hosted=None):
    m, k = (a.shape[1], a.shape[0]) if ta else a.shape
    n = b.shape[0] if tb else b.shape[1]
    assert k == (b.shape[1] if tb else b.shape[0]), (name, a.shape, b.shape)
    tm, tn, tk = _matmul_tiles(m, n, k, a.dtype.itemsize, b.dtype.itemsize, jnp.dtype(out_dtype).itemsize,
                               0 if res is None else res.dtype.itemsize)
    nk = k // tk
    grid = (m // tm, n // tn, nk)
    dims = ((0 if ta else 1,), (1 if tb else 0,))
    h_arrays, h_specs, h_shapes, h_scratch = _hosted_call_args(hosted)
    nhosted = len(h_arrays)
    n_in = 2 + (res is not None)

    def body(*refs):
        a_ref, b_ref = refs[:2]
        res_ref = refs[2] if res is not None else None
        h_in, o_ref, h_out = refs[n_in:n_in + nhosted], refs[n_in + nhosted], refs[n_in + nhosted + 1:n_in + 2 * nhosted + 1]
        acc_ref, sems = refs[n_in + 2 * nhosted + 1], refs[n_in + 2 * nhosted + 2:]
        _hosted_steps(hosted, h_in, h_out, sems, grid, 0.6)
        kk = pl.program_id(2)

        @pl.when(kk == 0)
        def _():
            acc_ref[...] = jnp.zeros_like(acc_ref)

        acc_ref[...] += lax.dot_general(a_ref[...].astype(BF16), b_ref[...].astype(BF16), (dims, ((), ())),
                                        preferred_element_type=F32)

        @pl.when(kk == nk - 1)
        def _():
            out = acc_ref[...] * alpha
            if res_ref is not None:
                out = out + res_ref[...].astype(F32)
            o_ref[...] = out.astype(o_ref.dtype)

    a_spec = pl.BlockSpec((tk, tm), lambda i, j, kk: (kk, i)) if ta else pl.BlockSpec((tm, tk), lambda i, j, kk: (i, kk))
    b_spec = pl.BlockSpec((tn, tk), lambda i, j, kk: (j, kk)) if tb else pl.BlockSpec((tk, tn), lambda i, j, kk: (kk, j))
    o_spec = pl.BlockSpec((tm, tn), lambda i, j, kk: (i, j))
    in_specs = [a_spec, b_spec] + ([o_spec] if res is not None else [])
    args = (a, b) + ((res,) if res is not None else ())
    out_shape = jax.ShapeDtypeStruct((m, n), out_dtype)
    if hosted is None:
        return pl.pallas_call(
            body, name=name, grid=grid, in_specs=in_specs, out_specs=o_spec, out_shape=out_shape,
            scratch_shapes=[pltpu.VMEM((tm, tn), F32)], compiler_params=_params(("parallel", "parallel", "arbitrary")),
        )(*args)
    out, *travelled = pl.pallas_call(
        body, name=name, grid=grid, in_specs=in_specs + h_specs, out_specs=[o_spec] + h_specs,
        out_shape=[out_shape] + h_shapes, scratch_shapes=[pltpu.VMEM((tm, tn), F32)] + h_scratch,
        compiler_params=_params(("arbitrary", "arbitrary", "arbitrary")),
    )(*args, *h_arrays)
    return out, travelled


def _row_specs(arrays, tr):
    return [pl.BlockSpec((tr, a.shape[1]), lambda i: (i, 0)) for a in arrays]


def _whole_specs(arrays):
    return [pl.BlockSpec(a.shape, lambda i: (0, 0)) for a in arrays]


def rowwise(name, fn, rows, consts, outs, tr):
    t = rows[0].shape[0]
    nr, nc = len(rows), len(consts)

    def body(*refs):
        vals = [r[...] for r in refs[:nr + nc]]
        res = fn(*vals)
        for o_ref, o in zip(refs[nr + nc:], res):
            o_ref[...] = o.astype(o_ref.dtype)

    out_shape = [jax.ShapeDtypeStruct((t, w), d) for w, d in outs]
    return pl.pallas_call(
        body, name=name, grid=(t // tr,), in_specs=_row_specs(rows, tr) + _whole_specs(consts),
        out_specs=_row_specs(out_shape, tr), out_shape=out_shape, compiler_params=_params(("parallel",)),
    )(*rows, *consts)


def rowwise_vjp(name, fn, rows, consts, cots, tr, row_grad, const_grad, add_to=None, grad_dtypes=None):
    t = rows[0].shape[0]
    nr, nc = len(rows), len(consts)
    cot_groups = [c if isinstance(c, (tuple, list)) else (c,) for c in cots]
    flat_cots = [a for g in cot_groups for a in g]
    add_to = {i: (a if isinstance(a, (tuple, list)) else (a,)) for i, a in (add_to or {}).items()}
    add_idx = [(i, q) for i in sorted(add_to) for q in range(len(add_to[i]))]
    add_arrays = [add_to[i][q] for i, q in add_idx]
    r_idx = [i for i in range(nr) if row_grad[i]]
    c_idx = [i for i in range(nc) if const_grad[i]]
    grad_dtypes = grad_dtypes or {}
    n_in = nr + nc + len(flat_cots) + len(add_arrays)

    def body(*refs):
        vals = [r[...] for r in refs[:nr + nc]]
        pos = nr + nc
        cts = []
        for g in cot_groups:
            s = refs[pos][...].astype(F32)
            for q in range(1, len(g)):
                s = s + refs[pos + q][...].astype(F32)
            cts.append(s)
            pos += len(g)
        adds = {}
        for n_add, (i, _) in enumerate(add_idx):
            term = refs[pos + n_add][...].astype(F32)
            adds[i] = adds[i] + term if i in adds else term
        diff =[vals[i] for i in r_idx] + [vals[nr + i] for i in c_idx]

        def f(*d):
            full = list(vals)
            for q, i in enumerate(r_idx):
                full[i] = d[q]
            for q, i in enumerate(c_idx):
                full[nr + i] = d[len(r_idx) + q]
            return tuple(fn(*full))

        prim, vjp = jax.vjp(f, *diff)
        grads = vjp(tuple(c.astype(p.dtype) for c, p in zip(cts, prim)))
        o_refs = refs[n_in:]
        for q, i in enumerate(r_idx):
            g = grads[q].astype(F32)
            if i in adds:
                g = g + adds[i].astype(F32)
            o_refs[q][...] = g.astype(o_refs[q].dtype)
        step = pl.program_id(0)
        for q, i in enumerate(c_idx):
            o_ref = o_refs[len(r_idx) + q]

            @pl.when(step == 0)
            def _(o_ref=o_ref):
                o_ref[...] = jnp.zeros_like(o_ref)

            o_ref[...] += grads[len(r_idx) + q].astype(F32)

    row_out = [jax.ShapeDtypeStruct(rows[i].shape, grad_dtypes.get(i, F32)) for i in r_idx]
    const_out = [jax.ShapeDtypeStruct(consts[i].shape, F32) for i in c_idx]
    outs = pl.pallas_call(
        body, name=name, grid=(t // tr,),
        in_specs=_row_specs(rows, tr) + _whole_specs(consts) + _row_specs(flat_cots, tr) + _row_specs(add_arrays, tr),
        out_specs=_row_specs(row_out, tr) + _whole_specs(const_out), out_shape=row_out + const_out,
        compiler_params=_params(("arbitrary",)),
    )(*rows, *consts, *flat_cots, *add_arrays)
    return list(outs)


def loss_head(y, target):
    t, d = y.shape
    tr = _pick(t, (256, 128, 64, 32, 16, 8))

    def body(y_ref, t_ref, dy_ref, l_ref):
        diff = y_ref[...] - t_ref[...]
        dy_ref[...] = diff * (1.0 / d)

        @pl.when(pl.program_id(0) == 0)
        def _():
            l_ref[...] = jnp.zeros_like(l_ref)

        l_ref[...] += 0.5 * jnp.sum(jnp.mean(diff * diff, axis=-1, keepdims=True), axis=0, keepdims=True)

    dy, l = pl.pallas_call(
        body, name="loss_head", grid=(t // tr,), in_specs=_row_specs([y, target], tr),
        out_specs=[pl.BlockSpec((tr, d), lambda i: (i, 0)), pl.BlockSpec((1, 1), lambda i: (0, 0))],
        out_shape=[jax.ShapeDtypeStruct((t, d), F32), jax.ShapeDtypeStruct((1, 1), F32)],
        compiler_params=_params(("arbitrary",)),
    )(y, target)
    return dy, l


ATT_PAD = N_LEFT_CHUNKS * CHUNK
MASKED = -1e30
ATT_GROUP = 2


def _attn_chunks(q_c, k_b, v_b, bias, valid):
    s = [jnp.where(ok, _dot1(q, k, NT) + b, MASKED) for q, k, b, ok in zip(q_c, k_b, bias, valid)]
    e = [jnp.exp(x - lax.stop_gradient(jnp.max(x, axis=-1, keepdims=True))) for x in s]
    p = [x / jnp.sum(x, axis=-1, keepdims=True) for x in e]
    return tuple(_dot1(x, v, NN) for x, v in zip(p, v_b))


def _band_valid(c):
    return (c * CHUNK + lax.broadcasted_iota(jnp.int32, (1, BAND), 1)) >= ATT_PAD


HEAD_PAIR = LANES // HEAD_DIM


def _split_heads(x, n):
    return [x[:, h * HEAD_DIM:(h + 1) * HEAD_DIM] for h in range(n)]


def _pair_spec(t):
    return pl.BlockSpec((t, HEAD_PAIR * HEAD_DIM), lambda p: (0, p))


def _attn_operands(q_ref, kp, vp, bias, cs, q0):
    qs = [x for s in q0 for x in _split_heads(q_ref[pl.ds(s, CHUNK), :], HEAD_PAIR)]
    ks = [x for s in q0 for x in _split_heads(kp[pl.ds(s, BAND), :], HEAD_PAIR)]
    vs = [x for s in q0 for x in _split_heads(vp[pl.ds(s, BAND), :], HEAD_PAIR)]
    return tuple(qs), tuple(ks), tuple(vs), tuple(bias) * len(cs), tuple(_band_valid(c) for c in cs for _ in bias)


def attention_fwd(q, k, v, bias, hosted=None):
    t, width = q.shape
    pairs = width // (HEAD_PAIR * HEAD_DIM)
    nchunks = t // CHUNK
    group = _pick(nchunks, (ATT_GROUP, 1))
    h_arrays, h_specs, h_shapes, h_scratch = _hosted_call_args(hosted)
    nhosted = len(h_arrays)

    def body(q_ref, k_ref, v_ref, b_ref, *rest):
        h_in, o_ref, h_out = rest[:nhosted], rest[nhosted], rest[nhosted + 1:2 * nhosted + 1]
        kp, vp, sems = rest[2 * nhosted + 1], rest[2 * nhosted + 2], rest[2 * nhosted + 3:]
        _hosted_steps(hosted, h_in, h_out, sems, (pairs,), 0.6)
        zeros = jnp.zeros((ATT_PAD, HEAD_PAIR * HEAD_DIM), F32)
        kp[pl.ds(0, ATT_PAD), :] = zeros
        vp[pl.ds(0, ATT_PAD), :] = zeros
        kp[pl.ds(ATT_PAD, t), :] = k_ref[...]
        vp[pl.ds(ATT_PAD, t), :] = v_ref[...]
        bias = [b_ref[h] for h in range(HEAD_PAIR)]

        def step(g, carry):
            cs = [g * group + u for u in range(group)]
            q0 = [pl.multiple_of(c * CHUNK, CHUNK) for c in cs]
            outs = _attn_chunks(*_attn_operands(q_ref, kp, vp, bias, cs, q0))
            for u, s in enumerate(q0):
                o_ref[pl.ds(s, CHUNK), :] = jnp.concatenate(outs[HEAD_PAIR * u:HEAD_PAIR * (u + 1)], axis=1)
            return carry

        lax.fori_loop(0, nchunks // group, step, 0)

    out, *travelled = pl.pallas_call(
        body, name="attention_fwd", grid=(pairs,),
        in_specs=[_pair_spec(t)] * 3 + [pl.BlockSpec((HEAD_PAIR, CHUNK, BAND), lambda p: (p, 0, 0))] + h_specs,
        out_specs=[_pair_spec(t)] + h_specs, out_shape=[jax.ShapeDtypeStruct((t, width), F32)] + h_shapes,
        scratch_shapes=[pltpu.VMEM((t + ATT_PAD, HEAD_PAIR * HEAD_DIM), F32)] * 2 + h_scratch,
        compiler_params=_params(("arbitrary",)),
    )(q, k, v, bias, *h_arrays)
    return out if hosted is None else (out, travelled)


def attention_bwd(q, k, v, bias, dout):
    t, width = q.shape
    pairs = width // (HEAD_PAIR * HEAD_DIM)
    nchunks = t // CHUNK
    group = _pick(nchunks, (ATT_GROUP, 1))

    def body(q_ref, k_ref, v_ref, b_ref, do_ref, dq_ref, dk_ref, dv_ref, db_ref, kp, vp, dkp, dvp):
        zeros = jnp.zeros((ATT_PAD, HEAD_PAIR * HEAD_DIM), F32)
        kp[pl.ds(0, ATT_PAD), :] = zeros
        vp[pl.ds(0, ATT_PAD), :] = zeros
        kp[pl.ds(ATT_PAD, t), :] = k_ref[...]
        vp[pl.ds(ATT_PAD, t), :] = v_ref[...]
        dkp[...] = jnp.zeros_like(dkp)
        dvp[...] = jnp.zeros_like(dvp)
        db_ref[...] = jnp.zeros_like(db_ref)
        bias = [b_ref[h] for h in range(HEAD_PAIR)]

        def step(g, carry):
            cs = [g * group + u for u in range(group)]
            q0 = [pl.multiple_of(c * CHUNK, CHUNK) for c in cs]
            qs, ks, vs, bs, valid = _attn_operands(q_ref, kp, vp, bias, cs, q0)
            _, vjp = jax.vjp(lambda a, b, cc, d: _attn_chunks(a, b, cc, d, valid), qs, ks, vs, bs)
            dos = tuple(x for s in q0 for x in _split_heads(do_ref[pl.ds(s, CHUNK), :], HEAD_PAIR))
            dq, dk, dv, db = vjp(dos)
            for u, s in enumerate(q0):
                mine = slice(HEAD_PAIR * u, HEAD_PAIR * (u + 1))
                dq_ref[pl.ds(s, CHUNK), :] = jnp.concatenate(dq[mine], axis=1)
                dkp[pl.ds(s, BAND), :] += jnp.concatenate(dk[mine], axis=1)
                dvp[pl.ds(s, BAND), :] += jnp.concatenate(dv[mine], axis=1)
            for h in range(HEAD_PAIR):
                total = db[h]
                for u in range(1, group):
                    total = total + db[HEAD_PAIR * u + h]
                db_ref[h] += total
            return carry

        lax.fori_loop(0, nchunks // group, step, 0)
        dk_ref[...] = dkp[pl.ds(ATT_PAD, t), :]
        dv_ref[...] = dvp[pl.ds(ATT_PAD, t), :]

    bias_spec = pl.BlockSpec((HEAD_PAIR, CHUNK, BAND), lambda p: (p, 0, 0))
    ts = jax.ShapeDtypeStruct((t, width), F32)
    return pl.pallas_call(
        body, name="attention_bwd", grid=(pairs,), in_specs=[_pair_spec(t)] * 3 + [bias_spec, _pair_spec(t)],
        out_specs=[_pair_spec(t)] * 3 + [bias_spec],
        out_shape=[ts, ts, ts, jax.ShapeDtypeStruct(bias.shape, F32)],
        scratch_shapes=[pltpu.VMEM((t + ATT_PAD, HEAD_PAIR * HEAD_DIM), F32)] * 4, compiler_params=_params(("parallel",)),
    )(q, k, v, bias, dout)


def _rel_onehot_t(i):
    j = lax.broadcasted_iota(jnp.int32, (N_REL, BAND), 1)
    r = lax.broadcasted_iota(jnp.int32, (N_REL, BAND), 0)
    idx = jnp.clip(i + ATT_PAD - j, -(CHUNK - 1), REL_CLIP) + (CHUNK - 1)
    return jnp.where(r == idx, 1.0, 0.0).astype(F32)


def bias_expand(rel):
    nh = rel.shape[0]

    def body(rel_ref, o_ref):
        o_ref[...] = _mm(rel_ref[...], _rel_onehot_t(pl.program_id(0)))

    return pl.pallas_call(
        body, name="bias_expand", grid=(CHUNK,), in_specs=[pl.BlockSpec((nh, N_REL), lambda i: (0, 0))],
        out_specs=pl.BlockSpec((None, nh, BAND), lambda i: (i, 0, 0)),
        out_shape=jax.ShapeDtypeStruct((CHUNK, nh, BAND), F32), compiler_params=_params(("parallel",)),
    )(rel)


def bias_reduce(dbias):
    nh = dbias.shape[1]

    def body(d_ref, o_ref):
        @pl.when(pl.program_id(0) == 0)
        def _():
            o_ref[...] = jnp.zeros_like(o_ref)

        o_ref[...] += _mm_nt(d_ref[...], _rel_onehot_t(pl.program_id(0)))

    return pl.pallas_call(
        body, name="bias_reduce", grid=(CHUNK,), in_specs=[pl.BlockSpec((None, nh, BAND), lambda i: (i, 0, 0))],
        out_specs=pl.BlockSpec((nh, N_REL), lambda i: (0, 0)), out_shape=jax.ShapeDtypeStruct((nh, N_REL), F32),
        compiler_params=_params(("arbitrary",)),
    )(dbias)


def _tri(n, strict):
    r = lax.broadcasted_iota(jnp.int32, (n, n), 0)
    c = lax.broadcasted_iota(jnp.int32, (n, n), 1)
    return (c < r) if strict else (c <= r)


def _each(f, *lists):
    return [f(*args) for args in zip(*lists)]


def _rwkv_chunk(s, r, k, v, w, a, b):
    n = r[0].shape[0]
    strict, incl = _tri(n, True), _tri(n, False)
    ones = jnp.where(incl, 1.0, 0.0).astype(F32)
    eye = jnp.where(incl & ~strict, 1.0, 0.0).astype(F32)
    lw = _each(jnp.log, w)
    cum = _each(lambda x: _mm(ones, x), lw)
    p_incl = _each(jnp.exp, cum)
    p_inv = _each(lambda x: jnp.exp(-x), cum)
    a_t = _each(lambda x, c, l: x * jnp.exp(c - l), a, cum, lw)
    r_t = _each(jnp.multiply, r, p_incl)
    b_t = _each(jnp.multiply, b, p_inv)
    k_t = _each(jnp.multiply, k, p_inv)
    a_ab = _each(lambda x, y: jnp.where(strict, _dot3(x, y, NT), 0.0), a_t, b_t)
    a_ak = _each(lambda x, y: jnp.where(strict, _dot3(x, y, NT), 0.0), a_t, k_t)
    r_b = _each(lambda x, y: jnp.where(incl, _dot3(x, y, NT), 0.0), r_t, b_t)
    r_k = _each(lambda x, y: jnp.where(incl, _dot3(x, y, NT), 0.0), r_t, k_t)
    a_s = _each(lambda x, y: _dot3(x, y, NT), a_t, s)
    r_s = _each(lambda x, y: _dot3(x, y, NT), r_t, s)
    a_kv = _each(lambda x, y: _dot3(x, y, NN), a_ak, v)
    total = _each(lambda x: eye + x, a_ab)
    power = _each(lambda x: _dot3(x, x, NN), a_ab)
    rounds = int(math.log2(n)) - 1
    for i in range(rounds):
        total = _each(lambda t, p: t + _dot3(t, p, NN), total, power)
        if i < rounds - 1:
            power = _each(lambda p: _dot3(p, p, NN), power)
    sa = _each(lambda t, x, y: _dot3(t, x + y, NN), total, a_s, a_kv)
    y = _each(lambda rs, rb, x, rk, vv: rs + _dot3(rb, x, NN) + _dot3(rk, vv, NN), r_s, r_b, sa, r_k, v)
    s_new = _each(lambda ss, x, bt, vv, kt, p: (ss + _dot3(x, bt, TN) + _dot3(vv, kt, TN)) * p[n - 1:n, :],
                  s, sa, b_t, v, k_t, p_incl)
    return tuple(s_new), tuple(y)


RWKV_TILE = 256
RWKV_HEADS = 8


def _rwkv_specs(nh, t, reverse):
    tile = min(RWKV_TILE, t)
    hb = RWKV_HEADS if nh % RWKV_HEADS == 0 else nh
    nt = t // tile
    per = tile // RWKV_CHUNK
    pos = (lambda h, i: (nt - 1 - i, h)) if reverse else (lambda h, i: (i, h))
    pos4 = (lambda h, i: (h, nt - 1 - i, 0, 0)) if reverse else (lambda h, i: (h, i, 0, 0))
    return (hb, nt, per, pl.BlockSpec((tile, hb * HEAD_DIM), pos), pl.BlockSpec((hb, per, HEAD_DIM, HEAD_DIM), pos4))


def _hosted_steps(exchange, refs_in, refs_out, sems, grid, forward_share):
    if exchange is None:
        return
    start, forward, finish = exchange.plan(refs_in, refs_out, *sems)
    step, n_steps = 0, 1
    for axis, size in enumerate(grid):
        step = step * size + pl.program_id(axis)
        n_steps *= size
    pl.when(step == 0)(start)
    pl.when(step == int(forward_share * (n_steps - 1)))(forward)
    pl.when(step == n_steps - 1)(finish)


def _hosted_call_args(exchange):
    if exchange is None:
        return [], [], [], []
    nb = len(exchange.arrays)
    return exchange.arrays, [HBM_SPEC] * nb, exchange.out_shape, exchange.scratch()


def rwkv_fwd(r, k, v, w, a, b, hosted=None):
    t, width = r.shape
    nh = width // HEAD_DIM
    hb, nt, per, row_spec, s_spec = _rwkv_specs(nh, t, False)
    h_arrays, h_specs, h_shapes, h_scratch = _hosted_call_args(hosted)
    nhosted = len(h_arrays)
    n_steps = (nh // hb) * nt

    def body(r_ref, k_ref, v_ref, w_ref, a_ref, b_ref, *rest):
        h_in, (y_ref, s_ref), h_out = rest[:nhosted], rest[nhosted:nhosted + 2], rest[nhosted + 2:2 * nhosted + 2]
        state, sems = rest[2 * nhosted + 2], rest[2 * nhosted + 3:]
        _hosted_steps(hosted, h_in, h_out, sems, (nh // hb, nt), 0.75)

        @pl.when(pl.program_id(1) == 0)
        def _():
            state[...] = jnp.zeros_like(state)

        def step(c, states):
            rows = pl.ds(pl.multiple_of(c * RWKV_CHUNK, RWKV_CHUNK), RWKV_CHUNK)
            for h in range(hb):
                s_ref[h, c] = states[h]
            s_new, y = _rwkv_chunk(states, *(tuple(_split_heads(ref[rows, :], hb))
                                             for ref in (r_ref, k_ref, v_ref, w_ref, a_ref, b_ref)))
            y_ref[rows, :] = jnp.concatenate(y, axis=1)
            return s_new

        final = lax.fori_loop(0, per, step, tuple(state[h] for h in range(hb)))
        for h in range(hb):
            state[h] = final[h]

    return pl.pallas_call(
        body, name="rwkv_fwd", grid=(nh // hb, nt), in_specs=[row_spec] * 6 + h_specs,
        out_specs=[row_spec, s_spec] + h_specs,
        out_shape=[jax.ShapeDtypeStruct((t, width), F32),
                   jax.ShapeDtypeStruct((nh, t // RWKV_CHUNK, HEAD_DIM, HEAD_DIM), F32)] + h_shapes,
        scratch_shapes=[pltpu.VMEM((hb, HEAD_DIM, HEAD_DIM), F32)] + h_scratch,
        compiler_params=_params(("arbitrary", "arbitrary")),
    )(r, k, v, w, a, b, *h_arrays)


def rwkv_bwd(r, k, v, w, a, b, states, dy, hosted=None):
    t, width = r.shape
    nh = width // HEAD_DIM
    hb, nt, per, row_spec, s_spec = _rwkv_specs(nh, t, True)
    h_arrays, h_specs, h_shapes, h_scratch = _hosted_call_args(hosted)
    nhosted = len(h_arrays)
    n_steps = (nh // hb) * nt

    def body(r_ref, k_ref, v_ref, w_ref, a_ref, b_ref, s_ref, dy_ref, *rest):
        h_in, d_refs, h_out = rest[:nhosted], rest[nhosted:nhosted + 6], rest[nhosted + 6:2 * nhosted + 6]
        dstate, sems = rest[2 * nhosted + 6], rest[2 * nhosted + 7:]
        _hosted_steps(hosted, h_in, h_out, sems, (nh // hb, nt), 0.5)

        @pl.when(pl.program_id(1) == 0)
        def _():
            dstate[...] = jnp.zeros_like(dstate)

        def step(i, ds):
            c = per - 1 - i
            rows = pl.ds(pl.multiple_of(c * RWKV_CHUNK, RWKV_CHUNK), RWKV_CHUNK)
            _, vjp = jax.vjp(_rwkv_chunk, tuple(s_ref[h, c] for h in range(hb)),
                             *(tuple(_split_heads(ref[rows, :], hb))
                               for ref in (r_ref, k_ref, v_ref, w_ref, a_ref, b_ref)))
            grads = vjp((ds, tuple(_split_heads(dy_ref[rows, :], hb))))
            for d_ref, g in zip(d_refs, grads[1:]):
                d_ref[rows, :] = jnp.concatenate(g, axis=1)
            return grads[0]

        final = lax.fori_loop(0, per, step, tuple(dstate[h] for h in range(hb)))
        for h in range(hb):
            dstate[h] = final[h]

    hs = jax.ShapeDtypeStruct((t, width), F32)
    return pl.pallas_call(
        body, name="rwkv_bwd", grid=(nh // hb, nt), in_specs=[row_spec] * 6 + [s_spec, row_spec] + h_specs,
        out_specs=[row_spec] * 6 + h_specs, out_shape=[hs] * 6 + h_shapes,
        scratch_shapes=[pltpu.VMEM((hb, HEAD_DIM, HEAD_DIM), F32)] + h_scratch,
        compiler_params=_params(("arbitrary", "arbitrary")),
    )(r, k, v, w, a, b, states, dy, *h_arrays)


SUBLANES = 8


def s5_scan(bu_re, bu_im, a_re, a_im, *, reverse=False, h_prev=None):
    t, n = bu_re.shape
    tt = _pick(t, (512, 256, 128, 64, 32, 16, 8))
    tc = _pick(n, (1024, 512, 256, 128))
    nt = t // tt
    with_da = h_prev is not None
    sign = -1.0 if reverse else 1.0

    def body(*refs):
        if with_da:
            br, bi, ar_ref, ai_ref, pr, pi, hr_ref, hi_ref, dar_ref, dai_ref, sr, si = refs
        else:
            br, bi, ar_ref, ai_ref, hr_ref, hi_ref, sr, si = refs
        ti = pl.program_id(1)

        @pl.when(ti == 0)
        def _():
            sr[...] = jnp.zeros_like(sr)
            si[...] = jnp.zeros_like(si)
            if with_da:
                dar_ref[...] = jnp.zeros_like(dar_ref)
                dai_ref[...] = jnp.zeros_like(dai_ref)

        ar = ar_ref[...]
        ai = ai_ref[...] * sign

        def group(gi, carry):
            hr, hi, dar, dai = carry
            g0 = pl.multiple_of((tt // SUBLANES - 1 - gi if reverse else gi) * SUBLANES, SUBLANES)
            rows = pl.ds(g0, SUBLANES)
            xr, xi = br[rows, :], bi[rows, :]
            if with_da:
                qr, qi = pr[rows, :], pi[rows, :]
            out_r, out_i = [None] * SUBLANES, [None] * SUBLANES
            for s in (range(SUBLANES - 1, -1, -1) if reverse else range(SUBLANES)):
                if with_da:
                    dar = dar + hr * qr[s:s + 1, :] + hi * qi[s:s + 1, :]
                    dai = dai + hi * qr[s:s + 1, :] - hr * qi[s:s + 1, :]
                hr, hi = ar * hr - ai * hi + xr[s:s + 1, :], ar * hi + ai * hr + xi[s:s + 1, :]
                out_r[s], out_i[s] = hr, hi
            hr_ref[rows, :] = jnp.concatenate(out_r, axis=0)
            hi_ref[rows, :] = jnp.concatenate(out_i, axis=0)
            return hr, hi, dar, dai

        zero = jnp.zeros((1, tc), F32)
        hr, hi, dar, dai = lax.fori_loop(0, tt // SUBLANES, group, (sr[...], si[...], zero, zero))
        sr[...] = hr
        si[...] = hi
        if with_da:
            dar_ref[...] += dar
            dai_ref[...] += dai

    tile = pl.BlockSpec((tt, tc), (lambda ci, ti: (nt - 1 - ti, ci)) if reverse else (lambda ci, ti: (ti, ci)))
    col = pl.BlockSpec((1, tc), lambda ci, ti: (0, ci))
    hs = jax.ShapeDtypeStruct((t, n), F32)
    cs = jax.ShapeDtypeStruct((1, n), F32)
    ins = [bu_re, bu_im, a_re, a_im] + (list(h_prev) if with_da else [])
    return pl.pallas_call(
        body, name="s5_scan_bwd" if reverse else "s5_scan_fwd", grid=(n // tc, nt),
        in_specs=[tile, tile, col, col] + ([tile, tile] if with_da else []),
        out_specs=[tile, tile] + ([col, col] if with_da else []), out_shape=[hs, hs] + ([cs, cs] if with_da else []),
        scratch_shapes=[pltpu.VMEM((1, tc), F32)] * 2, compiler_params=_params(("parallel", "arbitrary")),
    )(*ins)


N_CHIPS = 4
HBM_SPEC = pl.BlockSpec(memory_space=pl.ANY)


def _remote(src, dst, send_sem, recv_sem, device):
    return pltpu.make_async_remote_copy(src_ref=src, dst_ref=dst, send_sem=send_sem, recv_sem=recv_sem,
                                        device_id=device, device_id_type=pl.DeviceIdType.MESH)


class Exchange:
    def __init__(self, arrays, out_shape, n_sems, plan):
        self.arrays, self.out_shape, self.n_sems, self.plan = list(arrays), out_shape, n_sems, plan

    def scratch(self):
        nb = len(self.arrays)
        return [pltpu.SemaphoreType.DMA((nb, self.n_sems)), pltpu.SemaphoreType.DMA((nb, self.n_sems)),
                pltpu.SemaphoreType.DMA((nb,))]

    def run(self, name):
        nb = len(self.arrays)

        def body(*refs):
            for step in self.plan(refs[:nb], refs[nb:2 * nb], *refs[2 * nb:]):
                step()

        return pl.pallas_call(
            body, name=name, in_specs=[HBM_SPEC] * nb, out_specs=[HBM_SPEC] * nb, out_shape=self.out_shape,
            scratch_shapes=self.scratch(), compiler_params=pltpu.CompilerParams(has_side_effects=True),
        )(*self.arrays)


def gather_all(arrays):
    nb = len(arrays)

    def plan(ins, outs, send_sems, recv_sems, local_sems):
        x, y, c = (lax.axis_index(n) for n in MESH_AXES)
        me, sibling = 4 * x + 2 * y + c, 4 * x + 2 * y + 1 - c
        chips = [(1 - x, y), (x, 1 - y), (1 - x, 1 - y)]

        def copy(b, k, slot, to, src=None):
            block = outs[b].at[slot]
            return _remote(block if src is None else src, block, send_sems.at[b, k], recv_sems.at[b, k], to)

        def local():
            return [pltpu.make_async_copy(ins[b], outs[b].at[me], local_sems.at[b]) for b in range(nb)]

        def first():
            return [cp for b in range(nb) for cp in
                    [copy(b, 0, me, (x, y, 1 - c), src=ins[b])]
                    + [copy(b, 1 + j, me, (px, py, c), src=ins[b]) for j, (px, py) in enumerate(chips)]]

        def passed():
            return [copy(b, 4 + j, 4 * px + 2 * py + c, (x, y, 1 - c)) for j, (px, py) in enumerate(chips) for b in range(nb)]

        def start():
            for cp in local() + first():
                cp.start()

        def forward():
            for j, (px, py) in enumerate(chips):
                for b in range(nb):
                    copy(b, 1 + j, 4 * px + 2 * py + c, (px, py, c)).wait_recv()
                    copy(b, 4 + j, 4 * px + 2 * py + c, (x, y, 1 - c)).start()

        def finish():
            for b in range(nb):
                copy(b, 0, sibling, (x, y, 1 - c)).wait_recv()
                for j, (px, py) in enumerate(chips):
                    copy(b, 4 + j, 4 * px + 2 * py + 1 - c, (x, y, 1 - c)).wait_recv()
            for cp in first() + passed():
                cp.wait_send()
            for cp in local():
                cp.wait()

        return start, forward, finish

    out_shape = [jax.ShapeDtypeStruct((N_DEV,) + tuple(a.shape), a.dtype) for a in arrays]
    return Exchange(arrays, out_shape, N_DEV - 1, plan)


def _nothing():
    pass


def pair_exchange(arrays):
    nb = len(arrays)

    def plan(ins, outs, send_sems, recv_sems, local_sems):
        x, y, c = (lax.axis_index(n) for n in MESH_AXES)

        def copies():
            return [_remote(ins[b].at[2 * z + 1 - c], outs[b].at[z], send_sems.at[b, z], recv_sems.at[b, z], (x, y, 1 - c))
                    for b in range(nb) for z in range(N_CHIPS)]

        def start():
            for cp in copies():
                cp.start()

        def finish():
            for cp in copies():
                cp.wait_send()
            for cp in copies():
                cp.wait_recv()

        return start, _nothing, finish

    out_shape = [jax.ShapeDtypeStruct((N_CHIPS,) + tuple(a.shape[1:]), a.dtype) for a in arrays]
    return Exchange(arrays, out_shape, N_CHIPS, plan)


def pair_sum(name, pieces, from_sibling):
    _, r, c = pieces.shape
    tr = _row_tile(r, c * pieces.dtype.itemsize, 1 << 20)

    def body(p_ref, s_ref, o_ref):
        mine = p_ref[lax.axis_index("c")]
        o_ref[...] = (mine.astype(F32) + s_ref[...].astype(F32)).astype(o_ref.dtype)

    return pl.pallas_call(
        body, name=name, grid=(N_CHIPS, r // tr),
        in_specs=[pl.BlockSpec((None, 2, tr, c), lambda z, i: (z, 0, i, 0)), pl.BlockSpec((None, tr, c), lambda z, i: (z, i, 0))],
        out_specs=pl.BlockSpec((None, tr, c), lambda z, i: (z, i, 0)),
        out_shape=jax.ShapeDtypeStruct((N_CHIPS, r, c), pieces.dtype), compiler_params=_params(("parallel", "parallel")),
    )(pieces.reshape(N_CHIPS, 2, r, c), from_sibling)


def chip_exchange(arrays):
    nb = len(arrays)

    def plan(ins, outs, send_sems, recv_sems, local_sems):
        x, y, c = (lax.axis_index(n) for n in MESH_AXES)
        my_chip = 2 * x + y
        chips = [(1 - x, y), (x, 1 - y), (1 - x, 1 - y)]

        def local():
            return [pltpu.make_async_copy(ins[b].at[my_chip], outs[b].at[my_chip], local_sems.at[b]) for b in range(nb)]

        def copies():
            return [_remote(ins[b].at[2 * px + py], outs[b].at[my_chip], send_sems.at[b, j], recv_sems.at[b, j], (px, py, c))
                    for b in range(nb) for j, (px, py) in enumerate(chips)]

        def start():
            for cp in local() + copies():
                cp.start()

        def finish():
            for cp in copies():
                cp.wait_send()
            for cp in copies():
                cp.wait_recv()
            for cp in local():
                cp.wait()

        return start, _nothing, finish

    out_shape = [jax.ShapeDtypeStruct(a.shape, a.dtype) for a in arrays]
    return Exchange(arrays, out_shape, N_CHIPS - 1, plan)


def adam_update(name, pieces, w, m, v):
    r, c = w.shape
    n_pieces = pieces.shape[0]
    tr = _row_tile(r, 4 * c, 1 << 19)

    def body(p_ref, w_ref, m_ref, v_ref, g_ref, d_ref, mo_ref, vo_ref):
        g = p_ref[0].astype(F32)
        for j in range(1, n_pieces):
            g = g + p_ref[j].astype(F32)
        m_new = ADAM_B1 * m_ref[...] + (1.0 - ADAM_B1) * g
        v_new = ADAM_B2 * v_ref[...] + (1.0 - ADAM_B2) * (g * g)
        m_hat = m_new / (1.0 - ADAM_B1 ** ADAM_STEP)
        v_hat = v_new / (1.0 - ADAM_B2 ** ADAM_STEP)
        g_ref[...] = g
        d_ref[...] = -ADAM_LR * (m_hat / (jnp.sqrt(v_hat) + ADAM_EPS) + ADAM_WD * w_ref[...])
        mo_ref[...] = m_new
        vo_ref[...] = v_new

    row = pl.BlockSpec((tr, c), lambda i: (i, 0))
    out = jax.ShapeDtypeStruct((r, c), F32)
    return pl.pallas_call(
        body, name=name, grid=(r // tr,), in_specs=[pl.BlockSpec((n_pieces, tr, c), lambda i: (0, i, 0)), row, row, row],
        out_specs=[row] * 4, out_shape=[out] * 4, compiler_params=_params(("parallel",)),
    )(pieces, w, m, v)


PACK_WIDTH = 1024
PACK_ROWS = 64


def _pack(arrays, dtype, lead=0):
    parts = []
    for a in arrays:
        head = a.shape[:lead]
        f = a.reshape(head + (-1,)).astype(dtype)
        pad = (-f.shape[-1]) % PACK_WIDTH
        if pad:
            f = jnp.pad(f, [(0, 0)] * lead + [(0, pad)])
        parts.append(f.reshape(head + (-1, PACK_WIDTH)))
    out = jnp.concatenate(parts, axis=lead)
    pad = (-out.shape[lead]) % PACK_ROWS
    if pad:
        out = jnp.pad(out, [(0, 0)] * lead + [(0, pad), (0, 0)])
    return out


def _unpack(packed, shapes, lead=0):
    head = packed.shape[:lead]
    out, row = [], 0
    for s in shapes:
        n = int(np.prod(s))
        rows = -(-n // PACK_WIDTH)
        chunk = lax.slice_in_dim(packed, row, row + rows, axis=lead).reshape(head + (-1,))
        out.append(lax.slice_in_dim(chunk, 0, n, axis=lead).reshape(head + tuple(s)))
        row += rows
    return out


def _to_natural(stacked, kind):
    if kind == "col":
        m = jnp.moveaxis(stacked, 0, -2)
        return m.reshape(m.shape[:-2] + (m.shape[-2] * m.shape[-1],))
    m = jnp.moveaxis(stacked, 0, -3)
    return m.reshape(m.shape[:-3] + (m.shape[-3] * m.shape[-2], m.shape[-1]))


def _to_stacked(natural, kind):
    if kind == "col":
        m = natural.reshape(natural.shape[:-1] + (N_DEV, natural.shape[-1] // N_DEV))
        return jnp.moveaxis(m, -2, 0)
    m = natural.reshape(natural.shape[:-2] + (N_DEV, natural.shape[-2] // N_DEV, natural.shape[-1]))
    return jnp.moveaxis(m, -3, 0)


def _f_rms(h, g):
    return (_rms(h, g).astype(BF16),)


@jax.custom_vjp
def _swiglu(g, u):
    return g * _sigmoid(g) * u


def _swiglu_fwd(g, u):
    s = _sigmoid(g)
    return g * s * u, (g, u, s)


def _swiglu_bwd(saved, ct):
    g, u, s = saved
    return ct * u * (s * (1.0 + g * (1.0 - s))), ct * (g * s)


_swiglu.defvjp(_swiglu_fwd, _swiglu_bwd)


def _f_swiglu(g, u):
    return (_swiglu(g.astype(F32), u.astype(F32)).astype(BF16),)


def _f_ple(h, pre, pp):
    return (h + _sigmoid(pre) * pp,)


def _head_rms(x, g):
    ms = _seg_sum(x * x, HEAD_DIM) * (1.0 / HEAD_DIM)
    return x * _seg_expand(lax.rsqrt(ms + RMS_EPS), HEAD_DIM) * _tile_lanes(g, x.shape[1] // HEAD_DIM)


def _f_attpre(q, k, q_gain, k_gain):
    return _head_rms(q, q_gain) * (HEAD_DIM ** -0.5), _head_rms(k, k_gain)


def _f_shift(z, z_prev, mu):
    return (z + (z_prev - z) * mu,)


def _f_rwkvpre(r, k, v, xw, xa, xg, w0, w_up, a0, a_up, g_up, k_k, k_a):
    del r, v
    w_log = -_softplus(-(w0 + _mm(jnp.tanh(xw), w_up))) - 0.5
    decay = jnp.exp(-jnp.exp(w_log))
    a = _sigmoid(a0 + _mm(xa, a_up))
    g = _mm(_sigmoid(xg), g_up)
    kk = k * k_k
    norm = jnp.sqrt(_seg_expand(_seg_sum(kk * kk, HEAD_DIM), HEAD_DIM))
    kk = kk / jnp.maximum(norm, 1e-12)
    return k * (1.0 + (a - 1.0) * k_a), decay, -kk, kk * a, g


def _f_rwkvpost(y, r, k, v, g, lnx_w, lnx_b, r_k):
    mean = _seg_expand(_seg_sum(y, HEAD_DIM) * (1.0 / HEAD_DIM), HEAD_DIM)
    yc = y - mean
    var = _seg_expand(_seg_sum(yc * yc, HEAD_DIM) * (1.0 / HEAD_DIM), HEAD_DIM)
    yn = yc * lax.rsqrt(var + GN_EPS) * lnx_w + lnx_b
    bonus = _seg_expand(_seg_sum(r * k * r_k, HEAD_DIM), HEAD_DIM) * v
    return ((yn + bonus) * g,)


def _f_s5disc(lam_re, lam_im, log_dt):
    dt = jnp.exp(log_dt)
    mag = jnp.exp(lam_re * dt)
    ab_re, ab_im = mag * jnp.cos(lam_im * dt), mag * jnp.sin(lam_im * dt)
    denom = lam_re * lam_re + lam_im * lam_im
    z_re = ((ab_re - 1.0) * lam_re + ab_im * lam_im) / denom
    z_im = (ab_im * lam_re - (ab_re - 1.0) * lam_im) / denom
    return ab_re, ab_im, z_re, z_im


def _f_s5b(z_re, z_im, b_re, b_im):
    return z_re * b_re - z_im * b_im, z_re * b_im + z_im * b_re


def _f_s5post(ypre, u, d_skip):
    return (_gelu_tanh(ypre + d_skip * u).astype(BF16),)


def _f_glu(h, z1, z2):
    return (h + z1 * _sigmoid(z2),)


def _shift_down(x):
    return jnp.pad(x[:-1], ((1, 0), (0, 0)))


def _shift_up(x):
    return jnp.pad(x[1:], ((0, 1), (0, 0)))


def _block_diag(blocks):
    g, a, b = blocks.shape
    eye = jnp.eye(g, dtype=blocks.dtype)
    return (blocks[:, :, None, :] * eye[:, None, :, None]).reshape(g * a, g * b)


def _diag_blocks(dense, g):
    a, b = dense.shape[0] // g, dense.shape[1] // g
    return jnp.stack([dense[i * a:(i + 1) * a, i * b:(i + 1) * b] for i in range(g)], axis=0)


def _row_tile(t, width_bytes, budget=2 * 1024 * 1024):
    for tr in (512, 256, 128, 64, 32, 16, 8):
        if t % tr == 0 and tr * width_bytes <= budget:
            return tr
    return t


def _no_hosting(name):
    return None, None


def _hosting_matmul(hosting, name, a, b, **kw):
    exchange, done = hosting(name)
    if exchange is None:
        return matmul(name, a, b, **kw)
    out, travelled = matmul(name, a, b, hosted=exchange, **kw)
    done(travelled)
    return out


def _ffn_fwd(tag, h, norm_w, weight, hosting=_no_hosting):
    t, d = h.shape
    n = rowwise(f"{tag}_norm", _f_rms, [h], [norm_w], [(d, BF16)], _row_tile(t, 4 * d))[0]
    g = _hosting_matmul(hosting, f"{tag}_gate", n, weight("gate"), out_dtype=BF16)
    u = _hosting_matmul(hosting, f"{tag}_up", n, weight("up"), out_dtype=BF16)
    f = g.shape[1]
    a = rowwise(f"{tag}_act", _f_swiglu, [g, u], [], [(f, BF16)], _row_tile(t, 4 * f))[0]
    return _hosting_matmul(hosting, f"{tag}_down", a, weight("down"), alpha=0.5, res=h), (h, n, g, u, a)


def _ffn_bwd(tag, dh2, saved, norm_w, w_gate, w_up, w_down):
    h, n, g, u, a = saved
    t, d = h.shape
    f = w_gate.shape[1]
    da = matmul(f"{tag}_da", dh2, w_down, tb=True, alpha=0.5, out_dtype=BF16)
    d_down = matmul(f"{tag}_dwdown", a, dh2, ta=True, alpha=0.5, out_dtype=BF16)
    dg, du = rowwise_vjp(f"{tag}_dact", _f_swiglu, [g, u], [], [da], _row_tile(t, 4 * f, 1 << 20), [True, True], [],
                         grad_dtypes={0: BF16, 1: BF16})
    dn = matmul(f"{tag}_dn_gate", dg, w_gate, tb=True)
    dn = matmul(f"{tag}_dn_up", du, w_up, tb=True, res=dn)
    d_gate = matmul(f"{tag}_dwgate", n, dg, ta=True, out_dtype=BF16)
    d_up = matmul(f"{tag}_dwup", n, du, ta=True, out_dtype=BF16)
    dh, d_norm = rowwise_vjp(f"{tag}_dnorm", _f_rms, [h], [norm_w], [dn], _row_tile(t, 4 * d), [True], [True],
                             add_to={0: dh2})
    return dh, d_norm, d_gate, d_up, d_down


def _ple_fwd(tag, h, norm_w, w_gate, w_proj, p_i, hosting=_no_hosting):
    t, d = h.shape
    n = rowwise(f"{tag}_norm", _f_rms, [h], [norm_w], [(d, BF16)], _row_tile(t, 4 * d))[0]
    pre = _hosting_matmul(hosting, f"{tag}_gate", n, w_gate)
    pp = matmul(f"{tag}_proj", p_i, w_proj)
    h2 = rowwise(f"{tag}_out", _f_ple, [h, pre, pp], [], [(d, F32)], _row_tile(t, 4 * d))[0]
    return h2, (h, n, pre, pp)


def _ple_bwd(tag, dh2, saved, norm_w, w_gate, w_proj, p_i):
    h, n, pre, pp = saved
    t, d = h.shape
    dpre, dpp = rowwise_vjp(f"{tag}_dout", _f_ple, [h, pre, pp], [], [dh2], _row_tile(t, 4 * d), [False, True, True], [],
                            grad_dtypes={1: BF16, 2: BF16})
    d_proj = matmul(f"{tag}_dwproj", p_i, dpp, ta=True, out_dtype=BF16)
    d_gate = matmul(f"{tag}_dwgate", n, dpre, ta=True, out_dtype=BF16)
    dn = matmul(f"{tag}_dn", dpre, w_gate, tb=True)
    dh, d_norm = rowwise_vjp(f"{tag}_dnorm", _f_rms, [h], [norm_w], [dn], _row_tile(t, 4 * d), [True], [True],
                             add_to={0: dh2})
    return dh, d_norm, d_gate, d_proj


def _ab_fwd(h, w, hosting=_no_hosting):
    t, d = h.shape
    da = d // 2
    n = rowwise("ab_norm", _f_rms, [h], [w["mix_norm"]], [(d, BF16)], _row_tile(t, 4 * d))[0]
    proj = _hosting_matmul(hosting, "ab_in", n, w["ab_w_in"])
    q_raw, k_raw, v_att, z = proj[:, :da], proj[:, da:2 * da], proj[:, 2 * da:3 * da], proj[:, 3 * da:]
    tr = _row_tile(t, 4 * da, 1 << 19)
    qn, kn = rowwise("att_pre", _f_attpre, [q_raw, k_raw], [w["att_q_gain"], w["att_k_gain"]], [(da, F32)] * 2, tr)
    bias = jnp.transpose(bias_expand(w["att_rel_bias"]), (1, 0, 2))
    qh, kh, vh = qn, kn, v_att
    exchange, done = hosting("attention_fwd")
    att = attention_fwd(qh, kh, vh, bias, hosted=exchange)
    if exchange is not None:
        att, travelled = att
        done(travelled)
    z_prev = _shift_down(z)
    zz = rowwise("rwkv_shift", _f_shift, [z, z_prev], [w["rwkv_mu"]], [(z.shape[1], F32)], _row_tile(t, 4 * z.shape[1]))[0]
    o = [0, da, 2 * da, 3 * da, 3 * da + DECAY_LORA, 3 * da + DECAY_LORA + AAA_LORA, z.shape[1]]
    r, k, v, xw, xa, xg = (zz[:, o[i]:o[i + 1]] for i in range(6))
    pre_consts = [w[nm] for nm in ("rwkv_w0", "rwkv_w_up", "rwkv_a0", "rwkv_a_up", "rwkv_g_up", "rwkv_k_k", "rwkv_k_a")]
    k2, decay, ia, ib, g = rowwise("rwkv_pre", _f_rwkvpre, [r, k, v, xw, xa, xg], pre_consts, [(da, F32)] * 5, tr)
    heads = [r, k2, v, decay, ia, ib]
    exchange, done = hosting("rwkv_fwd")
    y, states, *travelled = rwkv_fwd(*heads, hosted=exchange)
    if exchange is not None:
        done(travelled)
    post_consts = [w["rwkv_lnx_w"], w["rwkv_lnx_b"], w["rwkv_r_k"]]
    rw = rowwise("rwkv_post", _f_rwkvpost, [y, r, k2, v, g], post_consts, [(da, F32)], tr)[0]
    cat = jnp.concatenate([att, rw], axis=1).astype(BF16)
    h2 = _hosting_matmul(hosting, "ab_out", cat, w["ab_w_out"], res=h)
    saved = dict(h=h, n=n, q_raw=q_raw, k_raw=k_raw, qh=qh, kh=kh, vh=vh, bias=bias, z=z, z_prev=z_prev,
                 rows=(r, k, v, xw, xa, xg), pre_consts=pre_consts, post=(y, r, k2, v, g), post_consts=post_consts,
                 heads=heads, states=states, cat=cat, tr=tr)
    return h2, saved


def _ab_bwd(dh2, s, w, make_hosted=None, hosted_done=None):
    h = s["h"]
    t, d = h.shape
    da = d // 2
    tr = s["tr"]
    grads = {}
    dcat = matmul("ab_dcat", dh2, w["ab_w_out"], tb=True)
    grads["ab_w_out"] = matmul("ab_dwout", s["cat"], dh2, ta=True, out_dtype=BF16)
    d_att, d_rw = dcat[:, :da], dcat[:, da:]
    dy, dr1, dk1, dv1, dg, grads["rwkv_lnx_w"], grads["rwkv_lnx_b"], grads["rwkv_r_k"] = rowwise_vjp(
        "rwkv_dpost", _f_rwkvpost, list(s["post"]), s["post_consts"], [d_rw], tr, [True] * 5, [True] * 3)
    hosted = None if make_hosted is None else make_hosted(grads["ab_w_out"])
    *d_heads, = rwkv_bwd(*s["heads"], s["states"], dy, hosted=hosted)
    if hosted is not None:
        hosted_done(d_heads[6:])
        d_heads = d_heads[:6]
    dr2, dk2, dv2, ddecay, dia, dib = d_heads
    pre = rowwise_vjp("rwkv_dpre", _f_rwkvpre, list(s["rows"]), s["pre_consts"], [(dk1, dk2), ddecay, dia, dib, dg], tr,
                      [True] * 6, [True] * 7, add_to={0: (dr1, dr2), 2: (dv1, dv2)})
    for nm, g in zip(("rwkv_w0", "rwkv_w_up", "rwkv_a0", "rwkv_a_up", "rwkv_g_up", "rwkv_k_k", "rwkv_k_a"), pre[6:]):
        grads[nm] = g
    dzz = jnp.concatenate(pre[:6], axis=1)
    trz = _row_tile(t, 4 * dzz.shape[1])
    grads["rwkv_mu"] = rowwise_vjp("rwkv_dmu", _f_shift, [s["z"], s["z_prev"]], [w["rwkv_mu"]], [dzz], trz,
                                   [False, False], [True])[0]
    dz = rowwise("rwkv_dshift", _f_shift, [dzz, _shift_up(dzz)], [w["rwkv_mu"]], [(dzz.shape[1], F32)], trz)[0]
    dqh, dkh, dvh, dbias = attention_bwd(s["qh"], s["kh"], s["vh"], s["bias"], d_att)
    grads["att_rel_bias"] = bias_reduce(jnp.transpose(dbias, (1, 0, 2)))
    dq_raw, dk_raw, grads["att_q_gain"], grads["att_k_gain"] = rowwise_vjp(
        "att_dpre", _f_attpre, [s["q_raw"], s["k_raw"]], [w["att_q_gain"], w["att_k_gain"]],
        [dqh, dkh], tr, [True, True], [True, True])
    dproj = jnp.concatenate([dq_raw, dk_raw, dvh, dz], axis=1).astype(BF16)
    dn = matmul("ab_dn", dproj, w["ab_w_in"], tb=True)
    grads["ab_w_in"] = matmul("ab_dwin", s["n"], dproj, ta=True, out_dtype=BF16)
    dh, grads["mix_norm"] = rowwise_vjp("ab_dnorm", _f_rms, [h], [w["mix_norm"]], [dn], _row_tile(t, 4 * d), [True], [True],
                                        add_to={0: dh2})
    return dh, grads


def _s5_fwd(h, w):
    t, d = h.shape
    n_groups, n_state = w["ssm_lambda_re"].shape
    gp = n_groups * n_state
    n = rowwise("s5_norm", _f_rms, [h], [w["mix_norm"]], [(d, BF16)], _row_tile(t, 4 * d))[0]
    u = matmul("s5_in", n, w["ssm_w_in"])
    disc_rows = [w["ssm_lambda_re"], w["ssm_lambda_im"], w["ssm_log_dt"]]
    ab_re, ab_im, z_re, z_im = rowwise("s5_disc", _f_s5disc, disc_rows, [], [(n_state, F32)] * 4, n_groups)
    b_rows = [z_re.reshape(gp, 1), z_im.reshape(gp, 1), w["ssm_b_re"], w["ssm_b_im"]]
    trb = _row_tile(gp, 512)
    bb_re, bb_im = rowwise("s5_bbar", _f_s5b, b_rows, [], [(SSM_GROUP, F32)] * 2, trb)
    to_dense = lambda bb: _block_diag(jnp.transpose(bb.reshape(n_groups, n_state, SSM_GROUP), (0, 2, 1)))
    bd_re, bd_im = to_dense(bb_re), to_dense(bb_im)
    cd_re = _block_diag(jnp.transpose(w["ssm_c_re"], (0, 2, 1)))
    cd_im = -_block_diag(jnp.transpose(w["ssm_c_im"], (0, 2, 1)))
    a_re, a_im = ab_re.reshape(1, gp), ab_im.reshape(1, gp)
    bu_re = matmul("s5_bu_re", u, bd_re)
    bu_im = matmul("s5_bu_im", u, bd_im)
    h_re, h_im = s5_scan(bu_re, bu_im, a_re, a_im)
    ypre = matmul("s5_y_re", h_re, cd_re)
    ypre = matmul("s5_y_im", h_im, cd_im, res=ypre)
    tru = _row_tile(t, 4 * u.shape[1])
    yg = rowwise("s5_post", _f_s5post, [ypre, u], [w["ssm_d"]], [(u.shape[1], BF16)], tru)[0]
    w_out1, w_out2 = w["ssm_w_out"][:, :d], w["ssm_w_out"][:, d:]
    z1 = matmul("s5_out1", yg, w_out1)
    z2 = matmul("s5_out2", yg, w_out2)
    h2 = rowwise("s5_glu", _f_glu, [h, z1, z2], [], [(d, F32)], _row_tile(t, 4 * d))[0]
    saved = dict(h=h, n=n, u=u, disc_rows=disc_rows, b_rows=b_rows, trb=trb, bd=(bd_re, bd_im), cd=(cd_re, cd_im),
                 a=(a_re, a_im), hs=(h_re, h_im), ypre=ypre, yg=yg, w_out=(w_out1, w_out2), z=(z1, z2), tru=tru)
    return h2, saved


def _s5_bwd(dh2, s, w):
    h, u = s["h"], s["u"]
    t, d = h.shape
    n_groups, n_state = w["ssm_lambda_re"].shape
    gp = n_groups * n_state
    grads = {}
    z1, z2 = s["z"]
    w_out1, w_out2 = s["w_out"]
    dz1, dz2 = rowwise_vjp("s5_dglu", _f_glu, [h, z1, z2], [], [dh2], _row_tile(t, 4 * d), [False, True, True], [],
                           grad_dtypes={1: BF16, 2: BF16})
    dyg = matmul("s5_dyg1", dz1, w_out1, tb=True)
    dyg = matmul("s5_dyg2", dz2, w_out2, tb=True, res=dyg)
    grads["ssm_w_out"] = jnp.concatenate([matmul("s5_dwout1", s["yg"], dz1, ta=True, out_dtype=BF16),
                                          matmul("s5_dwout2", s["yg"], dz2, ta=True, out_dtype=BF16)], axis=1)
    dypre, du1, grads["ssm_d"] = rowwise_vjp("s5_dpost", _f_s5post, [s["ypre"], u], [w["ssm_d"]], [dyg], s["tru"],
                                             [True, True], [True])
    cd_re, cd_im = s["cd"]
    h_re, h_im = s["hs"]
    dh_re = matmul("s5_dh_re", dypre, cd_re, tb=True)
    dh_im = matmul("s5_dh_im", dypre, cd_im, tb=True)
    dcd_re = matmul("s5_dc_re", h_re, dypre, ta=True)
    dcd_im = matmul("s5_dc_im", h_im, dypre, ta=True)
    a_re, a_im = s["a"]
    g_re, g_im, da_re, da_im = s5_scan(dh_re, dh_im, a_re, a_im, reverse=True, h_prev=(h_re, h_im))
    bd_re, bd_im = s["bd"]
    du = matmul("s5_du_re", g_re, bd_re, tb=True, res=du1)
    du = matmul("s5_du_im", g_im, bd_im, tb=True, res=du)
    dbd_re = matmul("s5_db_re", u, g_re, ta=True)
    dbd_im = matmul("s5_db_im", u, g_im, ta=True)
    grads["ssm_w_in"] = matmul("s5_dwin", s["n"], du, ta=True, out_dtype=BF16)
    dn = matmul("s5_dn", du, w["ssm_w_in"], tb=True)
    dh, grads["mix_norm"] = rowwise_vjp("s5_dnorm", _f_rms, [h], [w["mix_norm"]], [dn], _row_tile(t, 4 * d), [True], [True],
                                        add_to={0: dh2})
    from_dense = lambda m: jnp.transpose(_diag_blocks(m, n_groups), (0, 2, 1)).reshape(gp, SSM_GROUP)
    dz_re, dz_im, grads["ssm_b_re"], grads["ssm_b_im"] = rowwise_vjp(
        "s5_dbbar", _f_s5b, s["b_rows"], [], [from_dense(dbd_re), from_dense(dbd_im)], s["trb"], [True] * 4, [])
    disc_cots = [da_re.reshape(n_groups, n_state), da_im.reshape(n_groups, n_state),
                 dz_re.reshape(n_groups, n_state), dz_im.reshape(n_groups, n_state)]
    grads["ssm_lambda_re"], grads["ssm_lambda_im"], grads["ssm_log_dt"] = rowwise_vjp(
        "s5_ddisc", _f_s5disc, s["disc_rows"], [], disc_cots, n_groups, [True] * 3, [])
    grads["ssm_c_re"] = jnp.transpose(_diag_blocks(dcd_re, n_groups), (0, 2, 1))
    grads["ssm_c_im"] = -jnp.transpose(_diag_blocks(dcd_im, n_groups), (0, 2, 1))
    return dh, grads


WEIGHTS = ["ffn1_norm", "ffn1_w_gate", "ffn1_w_up", "ffn1_w_down", "mix_norm", "ffn2_norm", "ffn2_w_gate", "ffn2_w_up",
           "ffn2_w_down", "ple_norm", "ple_w_gate", "ple_w_proj", "ab_w_in", "att_q_gain", "att_k_gain", "att_rel_bias",
           "rwkv_mu", "rwkv_w0", "rwkv_w_up", "rwkv_a0", "rwkv_a_up", "rwkv_g_up", "rwkv_k_k", "rwkv_k_a", "rwkv_r_k",
           "rwkv_lnx_w", "rwkv_lnx_b", "ab_w_out", "ssm_w_in", "ssm_lambda_re", "ssm_lambda_im", "ssm_log_dt", "ssm_b_re",
           "ssm_b_im", "ssm_c_re", "ssm_c_im", "ssm_d", "ssm_w_out"]
BIG = {"ffn1_w_gate": "col", "ffn1_w_up": "col", "ffn1_w_down": "row", "ffn2_w_gate": "col", "ffn2_w_up": "col",
       "ffn2_w_down": "row", "ple_w_gate": "row", "ple_w_proj": "col", "ab_w_in": "col", "ab_w_out": "row",
       "ssm_w_in": "row", "ssm_w_out": "col"}
SMALL_CUT = {"rwkv_w_up": "col", "rwkv_a_up": "col", "rwkv_g_up": "col", "ssm_d": "col"}
REPLICATED = [n for n in WEIGHTS if n not in BIG and n not in SMALL_CUT]


def kernel(x, p, ffn1_norm, ffn1_w_gate, ffn1_w_up, ffn1_w_down, mix_norm, ffn2_norm, ffn2_w_gate, ffn2_w_up, ffn2_w_down, ple_norm, ple_w_gate, ple_w_proj, ab_w_in, att_q_gain, att_k_gain, att_rel_bias, rwkv_mu, rwkv_w0, rwkv_w_up, rwkv_a0, rwkv_a_up, rwkv_g_up, rwkv_k_k, rwkv_k_a, rwkv_r_k, rwkv_lnx_w, rwkv_lnx_b, ab_w_out, ssm_w_in, ssm_lambda_re, ssm_lambda_im, ssm_log_dt, ssm_b_re, ssm_b_im, ssm_c_re, ssm_c_im, ssm_d, ssm_w_out, loss_target, m_ffn1_norm, m_ffn1_w_gate, m_ffn1_w_up, m_ffn1_w_down, m_mix_norm, m_ffn2_norm, m_ffn2_w_gate, m_ffn2_w_up, m_ffn2_w_down, m_ple_norm, m_ple_w_gate, m_ple_w_proj, m_ab_w_in, m_att_q_gain, m_att_k_gain, m_att_rel_bias, m_rwkv_mu, m_rwkv_w0, m_rwkv_w_up, m_rwkv_a0, m_rwkv_a_up, m_rwkv_g_up, m_rwkv_k_k, m_rwkv_k_a, m_rwkv_r_k, m_rwkv_lnx_w, m_rwkv_lnx_b, m_ab_w_out, m_ssm_w_in, m_ssm_lambda_re, m_ssm_lambda_im, m_ssm_log_dt, m_ssm_b_re, m_ssm_b_im, m_ssm_c_re, m_ssm_c_im, m_ssm_d, m_ssm_w_out, v_ffn1_norm, v_ffn1_w_gate, v_ffn1_w_up, v_ffn1_w_down, v_mix_norm, v_ffn2_norm, v_ffn2_w_gate, v_ffn2_w_up, v_ffn2_w_down, v_ple_norm, v_ple_w_gate, v_ple_w_proj, v_ab_w_in, v_att_q_gain, v_att_k_gain, v_att_rel_bias, v_rwkv_mu, v_rwkv_w0, v_rwkv_w_up, v_rwkv_a0, v_rwkv_a_up, v_rwkv_g_up, v_rwkv_k_k, v_rwkv_k_a, v_rwkv_r_k, v_rwkv_lnx_w, v_rwkv_lnx_b, v_ab_w_out, v_ssm_w_in, v_ssm_lambda_re, v_ssm_lambda_im, v_ssm_log_dt, v_ssm_b_re, v_ssm_b_im, v_ssm_c_re, v_ssm_c_im, v_ssm_d, v_ssm_w_out):
    vals = dict(locals())
    depth = ffn1_norm.shape[0]
    n_groups, n_state = ssm_lambda_re.shape[1:]

    kinds = {**BIG, **SMALL_CUT}
    travel = {n: (BF16 if n in BIG else F32) for n in kinds}

    def rows(a, lead=0):
        return a.reshape(a.shape[:lead] + (-1, a.shape[-1]))

    def model_layer(n, j):
        if vals[n].shape[0] == depth:
            return j
        return 2 * j if n.startswith(("ab_", "rwkv_")) else 2 * j + 1

    units = [(n, j) for n in kinds for j in range(vals[n].shape[0])]
    early_names = ("ffn1_w_gate", "ffn1_w_up", "ffn1_w_down", "ab_w_in", "rwkv_w_up", "rwkv_a_up", "rwkv_g_up")
    early = [u for u in units if model_layer(*u) == 0 and u[0] in early_names]
    late = [u for u in units if u not in early]
    full = {}

    def shard(unit):
        n, j = unit
        return rows(vals[n][j]).astype(travel[n])

    def take_gathered(which, got):
        for (n, j), g in zip(which, got):
            full[(n, j)] = _to_natural(g.reshape((N_DEV,) + vals[n].shape[1:]), kinds[n])

    first = [("ffn1_w_gate", 0), ("ffn1_w_up", 0)]
    beside = {"l0_ffn1_gate": [("ffn1_w_down", 0)],
              "l0_ffn1_up": [("ab_w_in", 0), ("rwkv_w_up", 0), ("rwkv_a_up", 0), ("rwkv_g_up", 0)],
              "l0_ffn1_down": [("ab_w_out", 0), ("ffn2_w_gate", 0)],
              "ab_in": [("ffn2_w_up", 0)],
              "attention_fwd": [("ffn2_w_down", 0), ("ple_w_gate", 0), ("ple_w_proj", 0)],
              "l0_ffn2_gate": [("ffn2_w_gate", 1)], "l0_ffn2_up": [("ffn2_w_up", 1)], "l0_ffn2_down": [("ffn2_w_down", 1)],
              "l0_ple_gate": [("ple_w_gate", 1), ("ple_w_proj", 1)]}
    planned = first + [u for us in beside.values() for u in us]
    beside["rwkv_fwd"] = [u for u in units if u not in planned]
    assert depth == 2 and sorted(planned + beside["rwkv_fwd"]) == sorted(units)
    take_gathered(first, gather_all([shard(u) for u in first]).run("gather_first"))

    def hosting(name):
        which = beside.get(name)
        if not which:
            return None, None
        return gather_all([shard(u) for u in which]), functools.partial(take_gathered, which)

    def row(name, j):
        return vals[name][j].reshape(1, -1)

    def ffn_weights(which, i):
        return (row(f"{which}_norm", i), full[(f"{which}_w_gate", i)], full[(f"{which}_w_up", i)],
                full[(f"{which}_w_down", i)])

    def mixer_weights(i):
        j = i // 2
        if i % 2 == 0:
            w = {n: row(n, j) for n in ("att_q_gain", "att_k_gain", "rwkv_mu", "rwkv_w0", "rwkv_a0", "rwkv_k_k", "rwkv_k_a",
                                        "rwkv_r_k", "rwkv_lnx_w", "rwkv_lnx_b")}
            w.update({n: full.get((n, j)) for n in ("ab_w_in", "ab_w_out", "rwkv_w_up", "rwkv_a_up", "rwkv_g_up")})
            w["att_rel_bias"] = att_rel_bias[j]
        else:
            w = {"ssm_lambda_re": ssm_lambda_re[j], "ssm_lambda_im": ssm_lambda_im[j],
                 "ssm_log_dt": ssm_log_dt[j].reshape(n_groups, 1),
                 "ssm_b_re": ssm_b_re[j].reshape(n_groups * n_state, -1),
                 "ssm_b_im": ssm_b_im[j].reshape(n_groups * n_state, -1),
                 "ssm_c_re": ssm_c_re[j], "ssm_c_im": ssm_c_im[j], "ssm_d": full[("ssm_d", j)].reshape(1, -1),
                 "ssm_w_in": full[("ssm_w_in", j)], "ssm_w_out": full[("ssm_w_out", j)]}
        w["mix_norm"] = row("mix_norm", i)
        return w

    def ple_weights(i):
        return (row("ple_norm", i), full[("ple_w_gate", i)], full[("ple_w_proj", i)], p[i, 0])

    h = x[0]
    saved = []
    for i in range(depth):
        h, s1 = _ffn_fwd(f"l{i}_ffn1", h, row("ffn1_norm", i), lambda kind, i=i: full[(f"ffn1_w_{kind}", i)], hosting)
        h, sm = _ab_fwd(h, mixer_weights(i), hosting) if i % 2 == 0 else _s5_fwd(h, mixer_weights(i))
        h, s2 = _ffn_fwd(f"l{i}_ffn2", h, row("ffn2_norm", i), lambda kind, i=i: full[(f"ffn2_w_{kind}", i)], hosting)
        h, sp = _ple_fwd(f"l{i}_ple", h, *ple_weights(i), hosting)
        saved.append((s1, sm, s2, sp))
    dh, loss_part = loss_head(h, loss_target[0])

    per_layer = {n: [] for n in WEIGHTS}
    received = {}

    def chip_sums(which, tag):
        pieces = [rows(_to_stacked(per_layer[n][j - vals[n].shape[0]], kinds[n]).astype(travel[n]), lead=1)
                  for n, j in which]
        from_sibling = pair_exchange(pieces).run(f"reduce_pair_{tag}")
        return [pair_sum(f"pair_sum_{n}{j}", a, b) for (n, j), a, b in zip(which, pieces, from_sibling)]

    def late_reduce(d_ab_w_out):
        per_layer["ab_w_out"].insert(0, d_ab_w_out)
        return chip_exchange(chip_sums(late, "late"))

    def late_reduce_done(got):
        received.update(zip(late, got))

    for i in reversed(range(depth)):
        s1, sm, s2, sp = saved[i]
        dh, d_norm, d_gate, d_proj = _ple_bwd(f"l{i}_ple", dh, sp, *ple_weights(i))
        for n, g in (("ple_norm", d_norm), ("ple_w_gate", d_gate), ("ple_w_proj", d_proj)):
            per_layer[n].insert(0, g)
        dh, d_norm, d_gate, d_up, d_down = _ffn_bwd(f"l{i}_ffn2", dh, s2, *ffn_weights("ffn2", i))
        for n, g in (("ffn2_norm", d_norm), ("ffn2_w_gate", d_gate), ("ffn2_w_up", d_up), ("ffn2_w_down", d_down)):
            per_layer[n].insert(0, g)
        if i == 0:
            dh, mixer_grads = _ab_bwd(dh, sm, mixer_weights(i), late_reduce, late_reduce_done)
        else:
            dh, mixer_grads = (_ab_bwd if i % 2 == 0 else _s5_bwd)(dh, sm, mixer_weights(i))
        for n, g in mixer_grads.items():
            if not (i == 0 and n == "ab_w_out"):
                per_layer[n].insert(0, g)
        dh, d_norm, d_gate, d_up, d_down = _ffn_bwd(f"l{i}_ffn1", dh, s1, *ffn_weights("ffn1", i))
        for n, g in (("ffn1_norm", d_norm), ("ffn1_w_gate", d_gate), ("ffn1_w_up", d_up), ("ffn1_w_down", d_down)):
            per_layer[n].insert(0, g)
    grad_x = dh[None]

    received.update(zip(early, chip_exchange(chip_sums(early, "early")).run("reduce_chips_early")))

    rep_mine = _pack([jnp.stack(per_layer[n], axis=0).reshape(vals[n].shape) for n in REPLICATED] + [loss_part], F32)
    rep_all = gather_all([rep_mine]).run("gather_replicated")[0]

    out = {}
    for n in kinds:
        got = [received[(n, j)] for j in range(vals[n].shape[0])]
        got = got[0] if len(got) == 1 else jnp.concatenate(got, axis=1)
        state = [rows(vals[pre + n]) for pre in ("", "m_", "v_")]
        out[n] = tuple(r.reshape(vals[n].shape) for r in adam_update(f"adam_{n}", got, *state))
    zero = jnp.zeros((1, 1), F32)
    state = [_pack([vals[pre + n] for n in REPLICATED] + [zero], F32) for pre in ("", "m_", "v_")]
    shapes = [vals[n].shape for n in REPLICATED] + [zero.shape]
    results = [_unpack(r, shapes) for r in adam_update("adam_replicated", rep_all, *state)]
    for q, n in enumerate(REPLICATED):
        out[n] = tuple(r[q] for r in results)
    loss = results[0][-1].reshape(())
    return (loss, grad_x, *[out[n][0] for n in WEIGHTS], *[out[n][1] for n in WEIGHTS], *[out[n][2] for n in WEIGHTS],
            *[out[n][3] for n in WEIGHTS])
```

```python
import functools
import math

import jax
import jax.numpy as jnp
import numpy as np
from jax import lax
from jax.experimental import pallas as pl
from jax.experimental.pallas import tpu as pltpu

F32 = jnp.float32
BF16 = jnp.bfloat16
HIGHEST = lax.Precision.HIGHEST
MESH_AXES = ("x", "y", "c")
N_DEV = 8

CHUNK = 64
N_LEFT_CHUNKS = 8
BAND = (N_LEFT_CHUNKS + 1) * CHUNK
HEAD_DIM = 64
REL_CLIP = 128
N_REL = (CHUNK - 1) + REL_CLIP + 1
DECAY_LORA = 64
AAA_LORA = 64
GATE_LORA = 128
SSM_GROUP = 16
SSM_STATE = 64
RMS_EPS = 1e-6
GN_EPS = 64e-5
ADAM_LR = 0.001
ADAM_B1 = 0.9
ADAM_B2 = 0.999
ADAM_EPS = 1e-08
ADAM_WD = 0.01
ADAM_STEP = 10

RWKV_CHUNK = 64
VMEM_LIMIT = 56 * 1024 * 1024
LANES = 128


def _params(semantics):
    return pltpu.CompilerParams(dimension_semantics=semantics, vmem_limit_bytes=VMEM_LIMIT)


def _pick(n, prefs):
    for t in prefs:
        if n % t == 0:
            return t
    return n


def _dot(a, b, dims):
    return lax.dot_general(a, b, (dims, ((), ())), precision=HIGHEST, preferred_element_type=F32)


def _mm(a, b):
    return _dot(a, b, ((1,), (0,)))


def _mm_nt(a, b):
    return _dot(a, b, ((1,), (1,)))


def _mm_tn(a, b):
    return _dot(a, b, ((0,), (0,)))


def _split2(x):
    hi = x.astype(BF16)
    return hi, (x - hi.astype(F32)).astype(BF16)


def _dot3_raw(a, b, dims):
    a_hi, a_lo = _split2(a)
    b_hi, b_lo = _split2(b)
    dot = lambda p, q: lax.dot_general(p, q, (dims, ((), ())), preferred_element_type=F32)
    return dot(a_hi, b_hi) + (dot(a_hi, b_lo) + dot(a_lo, b_hi))


def _dot1_raw(a, b, dims):
    return lax.dot_general(a.astype(BF16), b.astype(BF16), (dims, ((), ())), preferred_element_type=F32)


NN, NT, TN = ((1,), (0,)), ((1,), (1,)), ((0,), (0,))


def _make_dot(raw):
    @functools.partial(jax.custom_vjp, nondiff_argnums=(2,))
    def dot(a, b, dims):
        return raw(a, b, dims)

    def fwd(a, b, dims):
        return raw(a, b, dims), (a, b)

    def bwd(dims, saved, g):
        a, b = saved
        if dims == NN:
            return raw(g, b, NT), raw(a, g, TN)
        if dims == NT:
            return raw(g, b, NN), raw(g, a, TN)
        return raw(b, g, NT), raw(a, g, NN)

    dot.defvjp(fwd, bwd)
    return dot


_dot3 = _make_dot(_dot3_raw)
_dot1 = _make_dot(_dot1_raw)


def _dot_ind_raw(x, ind, dims):
    hi = x.astype(BF16)
    rest = x - hi.astype(F32)
    mid = rest.astype(BF16)
    lo = (rest - mid.astype(F32)).astype(BF16)
    ind = ind.astype(BF16)
    dot = lambda p: lax.dot_general(p, ind, (dims, ((), ())), preferred_element_type=F32)
    return dot(hi) + (dot(mid) + dot(lo))


@jax.custom_vjp
def _mm_ind(x, ind):
    return _dot_ind_raw(x, ind, NN)


def _mm_ind_fwd(x, ind):
    return _dot_ind_raw(x, ind, NN), ind


def _mm_ind_bwd(ind, g):
    return _dot_ind_raw(g, ind, NT), jnp.zeros_like(ind)


_mm_ind.defvjp(_mm_ind_fwd, _mm_ind_bwd)


def _sigmoid(x):
    return 1.0 / (1.0 + jnp.exp(-x))


def _softplus(x):
    return jnp.maximum(x, 0.0) + jnp.log(1.0 + jnp.exp(-jnp.abs(x)))


def _gelu_tanh(x):
    return 0.5 * x * (1.0 + jnp.tanh(math.sqrt(2.0 / math.pi) * (x + 0.044715 * (x * x * x))))


def _seg_indicator(n, seg):
    r = lax.broadcasted_iota(jnp.int32, (n, n // seg), 0)
    c = lax.broadcasted_iota(jnp.int32, (n, n // seg), 1)
    return jnp.where((r >= c * seg) & (r < (c + 1) * seg), 1.0, 0.0).astype(F32)


def _seg_indicator_t(n, seg):
    c = lax.broadcasted_iota(jnp.int32, (n // seg, n), 0)
    r = lax.broadcasted_iota(jnp.int32, (n // seg, n), 1)
    return jnp.where((r >= c * seg) & (r < (c + 1) * seg), 1.0, 0.0).astype(F32)


def _seg_sum(x, seg):
    return _mm_ind(x, _seg_indicator(x.shape[1], seg))


def _seg_expand(s, seg):
    return _mm_ind(s, _seg_indicator_t(s.shape[1] * seg, seg))


def _tile_lanes(g, reps):
    n = g.shape[1]
    r = lax.broadcasted_iota(jnp.int32, (n, n * reps), 0)
    c = lax.broadcasted_iota(jnp.int32, (n, n * reps), 1)
    return _mm_ind(g, jnp.where((c & (n - 1)) == r, 1.0, 0.0).astype(F32))


def _rms(x, g):
    return x * lax.rsqrt(jnp.mean(x * x, axis=-1, keepdims=True) + RMS_EPS) * g


MATMUL_VMEM_BUDGET = 40 * 1024 * 1024
MATMUL_MAX_TILE = 2048
HBM_BYTES_PER_S = 1.2e12
MXU_FLOPS_PER_S = 8e14
GRID_STEP_S = 0.35e-6


def _tile_candidates(dim):
    c = [d for d in range(LANES, min(dim, MATMUL_MAX_TILE) + 1, LANES) if dim % d == 0]
    return c or [dim]


def _matmul_tiles(m, n, k, a_bytes, b_bytes, out_bytes, res_bytes):
    best = None
    for tm in _tile_candidates(m):
        for tn in _tile_candidates(n):
            for tk in _tile_candidates(k):
                casts = (tm * tk * 2 if a_bytes > 2 else 0) + (tk * tn * 2 if b_bytes > 2 else 0)
                vmem = (2 * (tm * tk * a_bytes + tk * tn * b_bytes + tm * tn * (out_bytes + res_bytes))
                        + 2 * tm * tn * 4 + casts)
                if vmem > MATMUL_VMEM_BUDGET:
                    continue
                steps = (m // tm) * (n // tn) * (k // tk)
                a_reads = 1 if k == tk else n // tn
                traffic = (m * k * a_bytes * a_reads + k * n * b_bytes * (m // tm) + m * n * (out_bytes + res_bytes))
                dma_s = traffic / HBM_BYTES_PER_S
                ends_s = (tm * tk * a_bytes + tk * tn * b_bytes + tm * tn * out_bytes) / HBM_BYTES_PER_S
                cost = max(dma_s, 2.0 * m * n * k / MXU_FLOPS_PER_S) + 0.2 * dma_s + steps * GRID_STEP_S + ends_s
                if best is None or cost < best[0]:
                    best = (cost, tm, tn, tk)
    return best[1:]


def matmul(name, a, b, *, ta=False, tb=False, alpha=1.0, res=None, out_dtype=F32, hosted=None):
    m, k = (a.shape[1], a.shape[0]) if ta else a.shape
    n = b.shape[0] if tb else b.shape[1]
    assert k == (b.shape[1] if tb else b.shape[0]), (name, a.shape, b.shape)
    tm, tn, tk = _matmul_tiles(m, n, k, a.dtype.itemsize, b.dtype.itemsize, jnp.dtype(out_dtype).itemsize,
                               0 if res is None else res.dtype.itemsize)
    nk = k // tk
    grid = (m // tm, n // tn, nk)
    dims = ((0 if ta else 1,), (1 if tb else 0,))
    h_arrays, h_specs, h_shapes, h_scratch = _hosted_call_args(hosted)
    nhosted = len(h_arrays)
    n_in = 2 + (res is not None)

    def body(*refs):
        a_ref, b_ref = refs[:2]
        res_ref = refs[2] if res is not None else None
        h_in, o_ref, h_out = refs[n_in:n_in + nhosted], refs[n_in + nhosted], refs[n_in + nhosted + 1:n_in + 2 * nhosted + 1]
        acc_ref, sems = refs[n_in + 2 * nhosted + 1], refs[n_in + 2 * nhosted + 2:]
        _hosted_steps(hosted, h_in, h_out, sems, grid, 0.6)
        kk = pl.program_id(2)

        @pl.when(kk == 0)
        def _():
            acc_ref[...] = jnp.zeros_like(acc_ref)

        acc_ref[...] += lax.dot_general(a_ref[...].astype(BF16), b_ref[...].astype(BF16), (dims, ((), ())),
                                        preferred_element_type=F32)

        @pl.when(kk == nk - 1)
        def _():
            out = acc_ref[...] * alpha
            if res_ref is not None:
                out = out + res_ref[...].astype(F32)
            o_ref[...] = out.astype(o_ref.dtype)

    a_spec = pl.BlockSpec((tk, tm), lambda i, j, kk: (kk, i)) if ta else pl.BlockSpec((tm, tk), lambda i, j, kk: (i, kk))
    b_spec = pl.BlockSpec((tn, tk), lambda i, j, kk: (j, kk)) if tb else pl.BlockSpec((tk, tn), lambda i, j, kk: (kk, j))
    o_spec = pl.BlockSpec((tm, tn), lambda i, j, kk: (i, j))
    in_specs = [a_spec, b_spec] + ([o_spec] if res is not None else [])
    args = (a, b) + ((res,) if res is not None else ())
    out_shape = jax.ShapeDtypeStruct((m, n), out_dtype)
    if hosted is None:
        return pl.pallas_call(
            body, name=name, grid=grid, in_specs=in_specs, out_specs=o_spec, out_shape=out_shape,
            scratch_shapes=[pltpu.VMEM((tm, tn), F32)], compiler_params=_params(("parallel", "parallel", "arbitrary")),
        )(*args)
    out, *travelled = pl.pallas_call(
        body, name=name, grid=grid, in_specs=in_specs + h_specs, out_specs=[o_spec] + h_specs,
        out_shape=[out_shape] + h_shapes, scratch_shapes=[pltpu.VMEM((tm, tn), F32)] + h_scratch,
        compiler_params=_params(("arbitrary", "arbitrary", "arbitrary")),
    )(*args, *h_arrays)
    return out, travelled


def _row_specs(arrays, tr):
    return [pl.BlockSpec((tr, a.shape[1]), lambda i: (i, 0)) for a in arrays]


def _whole_specs(arrays):
    return [pl.BlockSpec(a.shape, lambda i: (0, 0)) for a in arrays]


def rowwise(name, fn, rows, consts, outs, tr):
    t = rows[0].shape[0]
    nr, nc = len(rows), len(consts)

    def body(*refs):
        vals = [r[...] for r in refs[:nr + nc]]
        res = fn(*vals)
        for o_ref, o in zip(refs[nr + nc:], res):
            o_ref[...] = o.astype(o_ref.dtype)

    out_shape = [jax.ShapeDtypeStruct((t, w), d) for w, d in outs]
    return pl.pallas_call(
        body, name=name, grid=(t // tr,), in_specs=_row_specs(rows, tr) + _whole_specs(consts),
        out_specs=_row_specs(out_shape, tr), out_shape=out_shape, compiler_params=_params(("parallel",)),
    )(*rows, *consts)


def rowwise_vjp(name, fn, rows, consts, cots, tr, row_grad, const_grad, add_to=None, grad_dtypes=None):
    t = rows[0].shape[0]
    nr, nc = len(rows), len(consts)
    cot_groups = [c if isinstance(c, (tuple, list)) else (c,) for c in cots]
    flat_cots = [a for g in cot_groups for a in g]
    add_to = {i: (a if isinstance(a, (tuple, list)) else (a,)) for i, a in (add_to or {}).items()}
    add_idx = [(i, q) for i in sorted(add_to) for q in range(len(add_to[i]))]
    add_arrays = [add_to[i][q] for i, q in add_idx]
    r_idx = [i for i in range(nr) if row_grad[i]]
    c_idx = [i for i in range(nc) if const_grad[i]]
    grad_dtypes = grad_dtypes or {}
    n_in = nr + nc + len(flat_cots) + len(add_arrays)

    def body(*refs):
        vals = [r[...] for r in refs[:nr + nc]]
        pos = nr + nc
        cts = []
        for g in cot_groups:
            s = refs[pos][...].astype(F32)
            for q in range(1, len(g)):
                s = s + refs[pos + q][...].astype(F32)
            cts.append(s)
            pos += len(g)
        adds = {}
        for n_add, (i, _) in enumerate(add_idx):
            term = refs[pos + n_add][...].astype(F32)
            adds[i] = adds[i] + term if i in adds else term
        diff =[vals[i] for i in r_idx] + [vals[nr + i] for i in c_idx]

        def f(*d):
            full = list(vals)
            for q, i in enumerate(r_idx):
                full[i] = d[q]
            for q, i in enumerate(c_idx):
                full[nr + i] = d[len(r_idx) + q]
            return tuple(fn(*full))

        prim, vjp = jax.vjp(f, *diff)
        grads = vjp(tuple(c.astype(p.dtype) for c, p in zip(cts, prim)))
        o_refs = refs[n_in:]
        for q, i in enumerate(r_idx):
            g = grads[q].astype(F32)
            if i in adds:
                g = g + adds[i].astype(F32)
            o_refs[q][...] = g.astype(o_refs[q].dtype)
        step = pl.program_id(0)
        for q, i in enumerate(c_idx):
            o_ref = o_refs[len(r_idx) + q]

            @pl.when(step == 0)
            def _(o_ref=o_ref):
                o_ref[...] = jnp.zeros_like(o_ref)

            o_ref[...] += grads[len(r_idx) + q].astype(F32)

    row_out = [jax.ShapeDtypeStruct(rows[i].shape, grad_dtypes.get(i, F32)) for i in r_idx]
    const_out = [jax.ShapeDtypeStruct(consts[i].shape, F32) for i in c_idx]
    outs = pl.pallas_call(
        body, name=name, grid=(t // tr,),
        in_specs=_row_specs(rows, tr) + _whole_specs(consts) + _row_specs(flat_cots, tr) + _row_specs(add_arrays, tr),
        out_specs=_row_specs(row_out, tr) + _whole_specs(const_out), out_shape=row_out + const_out,
        compiler_params=_params(("arbitrary",)),
    )(*rows, *consts, *flat_cots, *add_arrays)
    return list(outs)


def loss_head(y, target):
    t, d = y.shape
    tr = _pick(t, (256, 128, 64, 32, 16, 8))

    def body(y_ref, t_ref, dy_ref, l_ref):
        diff = y_ref[...] - t_ref[...]
        dy_ref[...] = diff * (1.0 / d)

        @pl.when(pl.program_id(0) == 0)
        def _():
            l_ref[...] = jnp.zeros_like(l_ref)

        l_ref[...] += 0.5 * jnp.sum(jnp.mean(diff * diff, axis=-1, keepdims=True), axis=0, keepdims=True)

    dy, l = pl.pallas_call(
        body, name="loss_head", grid=(t // tr,), in_specs=_row_specs([y, target], tr),
        out_specs=[pl.BlockSpec((tr, d), lambda i: (i, 0)), pl.BlockSpec((1, 1), lambda i: (0, 0))],
        out_shape=[jax.ShapeDtypeStruct((t, d), F32), jax.ShapeDtypeStruct((1, 1), F32)],
        compiler_params=_params(("arbitrary",)),
    )(y, target)
    return dy, l


ATT_PAD = N_LEFT_CHUNKS * CHUNK
MASKED = -1e30
ATT_GROUP = 2


def _attn_chunks(q_c, k_b, v_b, bias, valid):
    s = [jnp.where(ok, _dot1(q, k, NT) + b, MASKED) for q, k, b, ok in zip(q_c, k_b, bias, valid)]
    e = [jnp.exp(x - lax.stop_gradient(jnp.max(x, axis=-1, keepdims=True))) for x in s]
    p = [x / jnp.sum(x, axis=-1, keepdims=True) for x in e]
    return tuple(_dot1(x, v, NN) for x, v in zip(p, v_b))


def _band_valid(c):
    return (c * CHUNK + lax.broadcasted_iota(jnp.int32, (1, BAND), 1)) >= ATT_PAD


HEAD_PAIR = LANES // HEAD_DIM


def _split_heads(x, n):
    return [x[:, h * HEAD_DIM:(h + 1) * HEAD_DIM] for h in range(n)]


def _pair_spec(t):
    return pl.BlockSpec((t, HEAD_PAIR * HEAD_DIM), lambda p: (0, p))


def _attn_operands(q_ref, kp, vp, bias, cs, q0):
    qs = [x for s in q0 for x in _split_heads(q_ref[pl.ds(s, CHUNK), :], HEAD_PAIR)]
    ks = [x for s in q0 for x in _split_heads(kp[pl.ds(s, BAND), :], HEAD_PAIR)]
    vs = [x for s in q0 for x in _split_heads(vp[pl.ds(s, BAND), :], HEAD_PAIR)]
    return tuple(qs), tuple(ks), tuple(vs), tuple(bias) * len(cs), tuple(_band_valid(c) for c in cs for _ in bias)


def attention_fwd(q, k, v, bias, hosted=None):
    t, width = q.shape
    pairs = width // (HEAD_PAIR * HEAD_DIM)
    nchunks = t // CHUNK
    group = _pick(nchunks, (ATT_GROUP, 1))
    h_arrays, h_specs, h_shapes, h_scratch = _hosted_call_args(hosted)
    nhosted = len(h_arrays)

    def body(q_ref, k_ref, v_ref, b_ref, *rest):
        h_in, o_ref, h_out = rest[:nhosted], rest[nhosted], rest[nhosted + 1:2 * nhosted + 1]
        kp, vp, sems = rest[2 * nhosted + 1], rest[2 * nhosted + 2], rest[2 * nhosted + 3:]
        _hosted_steps(hosted, h_in, h_out, sems, (pairs,), 0.6)
        zeros = jnp.zeros((ATT_PAD, HEAD_PAIR * HEAD_DIM), F32)
        kp[pl.ds(0, ATT_PAD), :] = zeros
        vp[pl.ds(0, ATT_PAD), :] = zeros
        kp[pl.ds(ATT_PAD, t), :] = k_ref[...]
        vp[pl.ds(ATT_PAD, t), :] = v_ref[...]
        bias = [b_ref[h] for h in range(HEAD_PAIR)]

        def step(g, carry):
            cs = [g * group + u for u in range(group)]
            q0 = [pl.multiple_of(c * CHUNK, CHUNK) for c in cs]
            outs = _attn_chunks(*_attn_operands(q_ref, kp, vp, bias, cs, q0))
            for u, s in enumerate(q0):
                o_ref[pl.ds(s, CHUNK), :] = jnp.concatenate(outs[HEAD_PAIR * u:HEAD_PAIR * (u + 1)], axis=1)
            return carry

        lax.fori_loop(0, nchunks // group, step, 0)

    out, *travelled = pl.pallas_call(
        body, name="attention_fwd", grid=(pairs,),
        in_specs=[_pair_spec(t)] * 3 + [pl.BlockSpec((HEAD_PAIR, CHUNK, BAND), lambda p: (p, 0, 0))] + h_specs,
        out_specs=[_pair_spec(t)] + h_specs, out_shape=[jax.ShapeDtypeStruct((t, width), F32)] + h_shapes,
        scratch_shapes=[pltpu.VMEM((t + ATT_PAD, HEAD_PAIR * HEAD_DIM), F32)] * 2 + h_scratch,
        compiler_params=_params(("arbitrary",)),
    )(q, k, v, bias, *h_arrays)
    return out if hosted is None else (out, travelled)


def attention_bwd(q, k, v, bias, dout):
    t, width = q.shape
    pairs = width // (HEAD_PAIR * HEAD_DIM)
    nchunks = t // CHUNK
    group = _pick(nchunks, (ATT_GROUP, 1))

    def body(q_ref, k_ref, v_ref, b_ref, do_ref, dq_ref, dk_ref, dv_ref, db_ref, kp, vp, dkp, dvp):
        zeros = jnp.zeros((ATT_PAD, HEAD_PAIR * HEAD_DIM), F32)
        kp[pl.ds(0, ATT_PAD), :] = zeros
        vp[pl.ds(0, ATT_PAD), :] = zeros
        kp[pl.ds(ATT_PAD, t), :] = k_ref[...]
        vp[pl.ds(ATT_PAD, t), :] = v_ref[...]
        dkp[...] = jnp.zeros_like(dkp)
        dvp[...] = jnp.zeros_like(dvp)
        db_ref[...] = jnp.zeros_like(db_ref)
        bias = [b_ref[h] for h in range(HEAD_PAIR)]

        def step(g, carry):
            cs = [g * group + u for u in range(group)]
            q0 = [pl.multiple_of(c * CHUNK, CHUNK) for c in cs]
            qs, ks, vs, bs, valid = _attn_operands(q_ref, kp, vp, bias, cs, q0)
            _, vjp = jax.vjp(lambda a, b, cc, d: _attn_chunks(a, b, cc, d, valid), qs, ks, vs, bs)
            dos = tuple(x for s in q0 for x in _split_heads(do_ref[pl.ds(s, CHUNK), :], HEAD_PAIR))
            dq, dk, dv, db = vjp(dos)
            for u, s in enumerate(q0):
                mine = slice(HEAD_PAIR * u, HEAD_PAIR * (u + 1))
                dq_ref[pl.ds(s, CHUNK), :] = jnp.concatenate(dq[mine], axis=1)
                dkp[pl.ds(s, BAND), :] += jnp.concatenate(dk[mine], axis=1)
                dvp[pl.ds(s, BAND), :] += jnp.concatenate(dv[mine], axis=1)
            for h in range(HEAD_PAIR):
                total = db[h]
                for u in range(1, group):
                    total = total + db[HEAD_PAIR * u + h]
                db_ref[h] += total
            return carry

        lax.fori_loop(0, nchunks // group, step, 0)
        dk_ref[...] = dkp[pl.ds(ATT_PAD, t), :]
        dv_ref[...] = dvp[pl.ds(ATT_PAD, t), :]

    bias_spec = pl.BlockSpec((HEAD_PAIR, CHUNK, BAND), lambda p: (p, 0, 0))
    ts = jax.ShapeDtypeStruct((t, width), F32)
    return pl.pallas_call(
        body, name="attention_bwd", grid=(pairs,), in_specs=[_pair_spec(t)] * 3 + [bias_spec, _pair_spec(t)],
        out_specs=[_pair_spec(t)] * 3 + [bias_spec],
        out_shape=[ts, ts, ts, jax.ShapeDtypeStruct(bias.shape, F32)],
        scratch_shapes=[pltpu.VMEM((t + ATT_PAD, HEAD_PAIR * HEAD_DIM), F32)] * 4, compiler_params=_params(("parallel",)),
    )(q, k, v, bias, dout)


def _rel_onehot_t(i):
    j = lax.broadcasted_iota(jnp.int32, (N_REL, BAND), 1)
    r = lax.broadcasted_iota(jnp.int32, (N_REL, BAND), 0)
    idx = jnp.clip(i + ATT_PAD - j, -(CHUNK - 1), REL_CLIP) + (CHUNK - 1)
    return jnp.where(r == idx, 1.0, 0.0).astype(F32)


def bias_expand(rel):
    nh = rel.shape[0]

    def body(rel_ref, o_ref):
        o_ref[...] = _mm(rel_ref[...], _rel_onehot_t(pl.program_id(0)))

    return pl.pallas_call(
        body, name="bias_expand", grid=(CHUNK,), in_specs=[pl.BlockSpec((nh, N_REL), lambda i: (0, 0))],
        out_specs=pl.BlockSpec((None, nh, BAND), lambda i: (i, 0, 0)),
        out_shape=jax.ShapeDtypeStruct((CHUNK, nh, BAND), F32), compiler_params=_params(("parallel",)),
    )(rel)


def bias_reduce(dbias):
    nh = dbias.shape[1]

    def body(d_ref, o_ref):
        @pl.when(pl.program_id(0) == 0)
        def _():
            o_ref[...] = jnp.zeros_like(o_ref)

        o_ref[...] += _mm_nt(d_ref[...], _rel_onehot_t(pl.program_id(0)))

    return pl.pallas_call(
        body, name="bias_reduce", grid=(CHUNK,), in_specs=[pl.BlockSpec((None, nh, BAND), lambda i: (i, 0, 0))],
        out_specs=pl.BlockSpec((nh, N_REL), lambda i: (0, 0)), out_shape=jax.ShapeDtypeStruct((nh, N_REL), F32),
        compiler_params=_params(("arbitrary",)),
    )(dbias)


def _tri(n, strict):
    r = lax.broadcasted_iota(jnp.int32, (n, n), 0)
    c = lax.broadcasted_iota(jnp.int32, (n, n), 1)
    return (c < r) if strict else (c <= r)


def _each(f, *lists):
    return [f(*args) for args in zip(*lists)]


def _rwkv_chunk(s, r, k, v, w, a, b):
    n = r[0].shape[0]
    strict, incl = _tri(n, True), _tri(n, False)
    ones = jnp.where(incl, 1.0, 0.0).astype(F32)
    eye = jnp.where(incl & ~strict, 1.0, 0.0).astype(F32)
    lw = _each(jnp.log, w)
    cum = _each(lambda x: _mm(ones, x), lw)
    p_incl = _each(jnp.exp, cum)
    p_inv = _each(lambda x: jnp.exp(-x), cum)
    a_t = _each(lambda x, c, l: x * jnp.exp(c - l), a, cum, lw)
    r_t = _each(jnp.multiply, r, p_incl)
    b_t = _each(jnp.multiply, b, p_inv)
    k_t = _each(jnp.multiply, k, p_inv)
    a_ab = _each(lambda x, y: jnp.where(strict, _dot3(x, y, NT), 0.0), a_t, b_t)
    a_ak = _each(lambda x, y: jnp.where(strict, _dot3(x, y, NT), 0.0), a_t, k_t)
    r_b = _each(lambda x, y: jnp.where(incl, _dot3(x, y, NT), 0.0), r_t, b_t)
    r_k = _each(lambda x, y: jnp.where(incl, _dot3(x, y, NT), 0.0), r_t, k_t)
    a_s = _each(lambda x, y: _dot3(x, y, NT), a_t, s)
    r_s = _each(lambda x, y: _dot3(x, y, NT), r_t, s)
    a_kv = _each(lambda x, y: _dot3(x, y, NN), a_ak, v)
    total = _each(lambda x: eye + x, a_ab)
    power = _each(lambda x: _dot3(x, x, NN), a_ab)
    rounds = int(math.log2(n)) - 1
    for i in range(rounds):
        total = _each(lambda t, p: t + _dot3(t, p, NN), total, power)
        if i < rounds - 1:
            power = _each(lambda p: _dot3(p, p, NN), power)
    sa = _each(lambda t, x, y: _dot3(t, x + y, NN), total, a_s, a_kv)
    y = _each(lambda rs, rb, x, rk, vv: rs + _dot3(rb, x, NN) + _dot3(rk, vv, NN), r_s, r_b, sa, r_k, v)
    s_new = _each(lambda ss, x, bt, vv, kt, p: (ss + _dot3(x, bt, TN) + _dot3(vv, kt, TN)) * p[n - 1:n, :],
                  s, sa, b_t, v, k_t, p_incl)
    return tuple(s_new), tuple(y)


RWKV_TILE = 256
RWKV_HEADS = 8


def _rwkv_specs(nh, t, reverse):
    tile = min(RWKV_TILE, t)
    hb = RWKV_HEADS if nh % RWKV_HEADS == 0 else nh
    nt = t // tile
    per = tile // RWKV_CHUNK
    pos = (lambda h, i: (nt - 1 - i, h)) if reverse else (lambda h, i: (i, h))
    pos4 = (lambda h, i: (h, nt - 1 - i, 0, 0)) if reverse else (lambda h, i: (h, i, 0, 0))
    return (hb, nt, per, pl.BlockSpec((tile, hb * HEAD_DIM), pos), pl.BlockSpec((hb, per, HEAD_DIM, HEAD_DIM), pos4))


def _hosted_steps(exchange, refs_in, refs_out, sems, grid, forward_share):
    if exchange is None:
        return
    start, forward, finish = exchange.plan(refs_in, refs_out, *sems)
    step, n_steps = 0, 1
    for axis, size in enumerate(grid):
        step = step * size + pl.program_id(axis)
        n_steps *= size
    pl.when(step == 0)(start)
    pl.when(step == int(forward_share * (n_steps - 1)))(forward)
    pl.when(step == n_steps - 1)(finish)


def _hosted_call_args(exchange):
    if exchange is None:
        return [], [], [], []
    nb = len(exchange.arrays)
    return exchange.arrays, [HBM_SPEC] * nb, exchange.out_shape, exchange.scratch()


def rwkv_fwd(r, k, v, w, a, b, hosted=None):
    t, width = r.shape
    nh = width // HEAD_DIM
    hb, nt, per, row_spec, s_spec = _rwkv_specs(nh, t, False)
    h_arrays, h_specs, h_shapes, h_scratch = _hosted_call_args(hosted)
    nhosted = len(h_arrays)
    n_steps = (nh // hb) * nt

    def body(r_ref, k_ref, v_ref, w_ref, a_ref, b_ref, *rest):
        h_in, (y_ref, s_ref), h_out = rest[:nhosted], rest[nhosted:nhosted + 2], rest[nhosted + 2:2 * nhosted + 2]
        state, sems = rest[2 * nhosted + 2], rest[2 * nhosted + 3:]
        _hosted_steps(hosted, h_in, h_out, sems, (nh // hb, nt), 0.75)

        @pl.when(pl.program_id(1) == 0)
        def _():
            state[...] = jnp.zeros_like(state)

        def step(c, states):
            rows = pl.ds(pl.multiple_of(c * RWKV_CHUNK, RWKV_CHUNK), RWKV_CHUNK)
            for h in range(hb):
                s_ref[h, c] = states[h]
            s_new, y = _rwkv_chunk(states, *(tuple(_split_heads(ref[rows, :], hb))
                                             for ref in (r_ref, k_ref, v_ref, w_ref, a_ref, b_ref)))
            y_ref[rows, :] = jnp.concatenate(y, axis=1)
            return s_new

        final = lax.fori_loop(0, per, step, tuple(state[h] for h in range(hb)))
        for h in range(hb):
            state[h] = final[h]

    return pl.pallas_call(
        body, name="rwkv_fwd", grid=(nh // hb, nt), in_specs=[row_spec] * 6 + h_specs,
        out_specs=[row_spec, s_spec] + h_specs,
        out_shape=[jax.ShapeDtypeStruct((t, width), F32),
                   jax.ShapeDtypeStruct((nh, t // RWKV_CHUNK, HEAD_DIM, HEAD_DIM), F32)] + h_shapes,
        scratch_shapes=[pltpu.VMEM((hb, HEAD_DIM, HEAD_DIM), F32)] + h_scratch,
        compiler_params=_params(("arbitrary", "arbitrary")),
    )(r, k, v, w, a, b, *h_arrays)


def rwkv_bwd(r, k, v, w, a, b, states, dy, hosted=None):
    t, width = r.shape
    nh = width // HEAD_DIM
    hb, nt, per, row_spec, s_spec = _rwkv_specs(nh, t, True)
    h_arrays, h_specs, h_shapes, h_scratch = _hosted_call_args(hosted)
    nhosted = len(h_arrays)
    n_steps = (nh // hb) * nt

    def body(r_ref, k_ref, v_ref, w_ref, a_ref, b_ref, s_ref, dy_ref, *rest):
        h_in, d_refs, h_out = rest[:nhosted], rest[nhosted:nhosted + 6], rest[nhosted + 6:2 * nhosted + 6]
        dstate, sems = rest[2 * nhosted + 6], rest[2 * nhosted + 7:]
        _hosted_steps(hosted, h_in, h_out, sems, (nh // hb, nt), 0.5)

        @pl.when(pl.program_id(1) == 0)
        def _():
            dstate[...] = jnp.zeros_like(dstate)

        def step(i, ds):
            c = per - 1 - i
            rows = pl.ds(pl.multiple_of(c * RWKV_CHUNK, RWKV_CHUNK), RWKV_CHUNK)
            _, vjp = jax.vjp(_rwkv_chunk, tuple(s_ref[h, c] for h in range(hb)),
                             *(tuple(_split_heads(ref[rows, :], hb))
                               for ref in (r_ref, k_ref, v_ref, w_ref, a_ref, b_ref)))
            grads = vjp((ds, tuple(_split_heads(dy_ref[rows, :], hb))))
            for d_ref, g in zip(d_refs, grads[1:]):
                d_ref[rows, :] = jnp.concatenate(g, axis=1)
            return grads[0]

        final = lax.fori_loop(0, per, step, tuple(dstate[h] for h in range(hb)))
        for h in range(hb):
            dstate[h] = final[h]

    hs = jax.ShapeDtypeStruct((t, width), F32)
    return pl.pallas_call(
        body, name="rwkv_bwd", grid=(nh // hb, nt), in_specs=[row_spec] * 6 + [s_spec, row_spec] + h_specs,
        out_specs=[row_spec] * 6 + h_specs, out_shape=[hs] * 6 + h_shapes,
        scratch_shapes=[pltpu.VMEM((hb, HEAD_DIM, HEAD_DIM), F32)] + h_scratch,
        compiler_params=_params(("arbitrary", "arbitrary")),
    )(r, k, v, w, a, b, states, dy, *h_arrays)


SUBLANES = 8


def s5_scan(bu_re, bu_im, a_re, a_im, *, reverse=False, h_prev=None):
    t, n = bu_re.shape
    tt = _pick(t, (512, 256, 128, 64, 32, 16, 8))
    tc = _pick(n, (1024, 512, 256, 128))
    nt = t // tt
    with_da = h_prev is not None
    sign = -1.0 if reverse else 1.0

    def body(*refs):
        if with_da:
            br, bi, ar_ref, ai_ref, pr, pi, hr_ref, hi_ref, dar_ref, dai_ref, sr, si = refs
        else:
            br, bi, ar_ref, ai_ref, hr_ref, hi_ref, sr, si = refs
        ti = pl.program_id(1)

        @pl.when(ti == 0)
        def _():
            sr[...] = jnp.zeros_like(sr)
            si[...] = jnp.zeros_like(si)
            if with_da:
                dar_ref[...] = jnp.zeros_like(dar_ref)
                dai_ref[...] = jnp.zeros_like(dai_ref)

        ar = ar_ref[...]
        ai = ai_ref[...] * sign

        def group(gi, carry):
            hr, hi, dar, dai = carry
            g0 = pl.multiple_of((tt // SUBLANES - 1 - gi if reverse else gi) * SUBLANES, SUBLANES)
            rows = pl.ds(g0, SUBLANES)
            xr, xi = br[rows, :], bi[rows, :]
            if with_da:
                qr, qi = pr[rows, :], pi[rows, :]
            out_r, out_i = [None] * SUBLANES, [None] * SUBLANES
            for s in (range(SUBLANES - 1, -1, -1) if reverse else range(SUBLANES)):
                if with_da:
                    dar = dar + hr * qr[s:s + 1, :] + hi * qi[s:s + 1, :]
                    dai = dai + hi * qr[s:s + 1, :] - hr * qi[s:s + 1, :]
                hr, hi = ar * hr - ai * hi + xr[s:s + 1, :], ar * hi + ai * hr + xi[s:s + 1, :]
                out_r[s], out_i[s] = hr, hi
            hr_ref[rows, :] = jnp.concatenate(out_r, axis=0)
            hi_ref[rows, :] = jnp.concatenate(out_i, axis=0)
            return hr, hi, dar, dai

        zero = jnp.zeros((1, tc), F32)
        hr, hi, dar, dai = lax.fori_loop(0, tt // SUBLANES, group, (sr[...], si[...], zero, zero))
        sr[...] = hr
        si[...] = hi
        if with_da:
            dar_ref[...] += dar
            dai_ref[...] += dai

    tile = pl.BlockSpec((tt, tc), (lambda ci, ti: (nt - 1 - ti, ci)) if reverse else (lambda ci, ti: (ti, ci)))
    col = pl.BlockSpec((1, tc), lambda ci, ti: (0, ci))
    hs = jax.ShapeDtypeStruct((t, n), F32)
    cs = jax.ShapeDtypeStruct((1, n), F32)
    ins = [bu_re, bu_im, a_re, a_im] + (list(h_prev) if with_da else [])
    return pl.pallas_call(
        body, name="s5_scan_bwd" if reverse else "s5_scan_fwd", grid=(n // tc, nt),
        in_specs=[tile, tile, col, col] + ([tile, tile] if with_da else []),
        out_specs=[tile, tile] + ([col, col] if with_da else []), out_shape=[hs, hs] + ([cs, cs] if with_da else []),
        scratch_shapes=[pltpu.VMEM((1, tc), F32)] * 2, compiler_params=_params(("parallel", "arbitrary")),
    )(*ins)


N_CHIPS = 4
HBM_SPEC = pl.BlockSpec(memory_space=pl.ANY)


def _remote(src, dst, send_sem, recv_sem, device):
    return pltpu.make_async_remote_copy(src_ref=src, dst_ref=dst, send_sem=send_sem, recv_sem=recv_sem,
                                        device_id=device, device_id_type=pl.DeviceIdType.MESH)


class Exchange:
    def __init__(self, arrays, out_shape, n_sems, plan):
        self.arrays, self.out_shape, self.n_sems, self.plan = list(arrays), out_shape, n_sems, plan

    def scratch(self):
        nb = len(self.arrays)
        return [pltpu.SemaphoreType.DMA((nb, self.n_sems)), pltpu.SemaphoreType.DMA((nb, self.n_sems)),
                pltpu.SemaphoreType.DMA((nb,))]

    def run(self, name):
        nb = len(self.arrays)

        def body(*refs):
            for step in self.plan(refs[:nb], refs[nb:2 * nb], *refs[2 * nb:]):
                step()

        return pl.pallas_call(
            body, name=name, in_specs=[HBM_SPEC] * nb, out_specs=[HBM_SPEC] * nb, out_shape=self.out_shape,
            scratch_shapes=self.scratch(), compiler_params=pltpu.CompilerParams(has_side_effects=True),
        )(*self.arrays)


def gather_all(arrays):
    nb = len(arrays)

    def plan(ins, outs, send_sems, recv_sems, local_sems):
        x, y, c = (lax.axis_index(n) for n in MESH_AXES)
        me, sibling = 4 * x + 2 * y + c, 4 * x + 2 * y + 1 - c
        chips = [(1 - x, y), (x, 1 - y), (1 - x, 1 - y)]

        def copy(b, k, slot, to, src=None):
            block = outs[b].at[slot]
            return _remote(block if src is None else src, block, send_sems.at[b, k], recv_sems.at[b, k], to)

        def local():
            return [pltpu.make_async_copy(ins[b], outs[b].at[me], local_sems.at[b]) for b in range(nb)]

        def first():
            return [cp for b in range(nb) for cp in
                    [copy(b, 0, me, (x, y, 1 - c), src=ins[b])]
                    + [copy(b, 1 + j, me, (px, py, c), src=ins[b]) for j, (px, py) in enumerate(chips)]]

        def passed():
            return [copy(b, 4 + j, 4 * px + 2 * py + c, (x, y, 1 - c)) for j, (px, py) in enumerate(chips) for b in range(nb)]

        def start():
            for cp in local() + first():
                cp.start()

        def forward():
            for j, (px, py) in enumerate(chips):
                for b in range(nb):
                    copy(b, 1 + j, 4 * px + 2 * py + c, (px, py, c)).wait_recv()
                    copy(b, 4 + j, 4 * px + 2 * py + c, (x, y, 1 - c)).start()

        def finish():
            for b in range(nb):
                copy(b, 0, sibling, (x, y, 1 - c)).wait_recv()
                for j, (px, py) in enumerate(chips):
                    copy(b, 4 + j, 4 * px + 2 * py + 1 - c, (x, y, 1 - c)).wait_recv()
            for cp in first() + passed():
                cp.wait_send()
            for cp in local():
                cp.wait()

        return start, forward, finish

    out_shape = [jax.ShapeDtypeStruct((N_DEV,) + tuple(a.shape), a.dtype) for a in arrays]
    return Exchange(arrays, out_shape, N_DEV - 1, plan)


def _nothing():
    pass


def pair_exchange(arrays):
    nb = len(arrays)

    def plan(ins, outs, send_sems, recv_sems, local_sems):
        x, y, c = (lax.axis_index(n) for n in MESH_AXES)

        def copies():
            return [_remote(ins[b].at[2 * z + 1 - c], outs[b].at[z], send_sems.at[b, z], recv_sems.at[b, z], (x, y, 1 - c))
                    for b in range(nb) for z in range(N_CHIPS)]

        def start():
            for cp in copies():
                cp.start()

        def finish():
            for cp in copies():
                cp.wait_send()
            for cp in copies():
                cp.wait_recv()

        return start, _nothing, finish

    out_shape = [jax.ShapeDtypeStruct((N_CHIPS,) + tuple(a.shape[1:]), a.dtype) for a in arrays]
    return Exchange(arrays, out_shape, N_CHIPS, plan)


def pair_sum(name, pieces, from_sibling):
    _, r, c = pieces.shape
    tr = _row_tile(r, c * pieces.dtype.itemsize, 1 << 20)

    def body(p_ref, s_ref, o_ref):
        mine = p_ref[lax.axis_index("c")]
        o_ref[...] = (mine.astype(F32) + s_ref[...].astype(F32)).astype(o_ref.dtype)

    return pl.pallas_call(
        body, name=name, grid=(N_CHIPS, r // tr),
        in_specs=[pl.BlockSpec((None, 2, tr, c), lambda z, i: (z, 0, i, 0)), pl.BlockSpec((None, tr, c), lambda z, i: (z, i, 0))],
        out_specs=pl.BlockSpec((None, tr, c), lambda z, i: (z, i, 0)),
        out_shape=jax.ShapeDtypeStruct((N_CHIPS, r, c), pieces.dtype), compiler_params=_params(("parallel", "parallel")),
    )(pieces.reshape(N_CHIPS, 2, r, c), from_sibling)


def chip_exchange(arrays):
    nb = len(arrays)

    def plan(ins, outs, send_sems, recv_sems, local_sems):
        x, y, c = (lax.axis_index(n) for n in MESH_AXES)
        my_chip = 2 * x + y
        chips = [(1 - x, y), (x, 1 - y), (1 - x, 1 - y)]

        def local():
            return [pltpu.make_async_copy(ins[b].at[my_chip], outs[b].at[my_chip], local_sems.at[b]) for b in range(nb)]

        def copies():
            return [_remote(ins[b].at[2 * px + py], outs[b].at[my_chip], send_sems.at[b, j], recv_sems.at[b, j], (px, py, c))
                    for b in range(nb) for j, (px, py) in enumerate(chips)]

        def start():
            for cp in local() + copies():
                cp.start()

        def finish():
            for cp in copies():
                cp.wait_send()
            for cp in copies():
                cp.wait_recv()
            for cp in local():
                cp.wait()

        return start, _nothing, finish

    out_shape = [jax.ShapeDtypeStruct(a.shape, a.dtype) for a in arrays]
    return Exchange(arrays, out_shape, N_CHIPS - 1, plan)


def adam_update(name, pieces, w, m, v):
    r, c = w.shape
    n_pieces = pieces.shape[0]
    tr = _row_tile(r, 4 * c, 1 << 19)

    def body(p_ref, w_ref, m_ref, v_ref, g_ref, d_ref, mo_ref, vo_ref):
        g = p_ref[0].astype(F32)
        for j in range(1, n_pieces):
            g = g + p_ref[j].astype(F32)
        m_new = ADAM_B1 * m_ref[...] + (1.0 - ADAM_B1) * g
        v_new = ADAM_B2 * v_ref[...] + (1.0 - ADAM_B2) * (g * g)
        m_hat = m_new / (1.0 - ADAM_B1 ** ADAM_STEP)
        v_hat = v_new / (1.0 - ADAM_B2 ** ADAM_STEP)
        g_ref[...] = g
        d_ref[...] = -ADAM_LR * (m_hat / (jnp.sqrt(v_hat) + ADAM_EPS) + ADAM_WD * w_ref[...])
        mo_ref[...] = m_new
        vo_ref[...] = v_new

    row = pl.BlockSpec((tr, c), lambda i: (i, 0))
    out = jax.ShapeDtypeStruct((r, c), F32)
    return pl.pallas_call(
        body, name=name, grid=(r // tr,), in_specs=[pl.BlockSpec((n_pieces, tr, c), lambda i: (0, i, 0)), row, row, row],
        out_specs=[row] * 4, out_shape=[out] * 4, compiler_params=_params(("parallel",)),
    )(pieces, w, m, v)


PACK_WIDTH = 1024
PACK_ROWS = 64


def _pack(arrays, dtype, lead=0):
    parts = []
    for a in arrays:
        head = a.shape[:lead]
        f = a.reshape(head + (-1,)).astype(dtype)
        pad = (-f.shape[-1]) % PACK_WIDTH
        if pad:
            f = jnp.pad(f, [(0, 0)] * lead + [(0, pad)])
        parts.append(f.reshape(head + (-1, PACK_WIDTH)))
    out = jnp.concatenate(parts, axis=lead)
    pad = (-out.shape[lead]) % PACK_ROWS
    if pad:
        out = jnp.pad(out, [(0, 0)] * lead + [(0, pad), (0, 0)])
    return out


def _unpack(packed, shapes, lead=0):
    head = packed.shape[:lead]
    out, row = [], 0
    for s in shapes:
        n = int(np.prod(s))
        rows = -(-n // PACK_WIDTH)
        chunk = lax.slice_in_dim(packed, row, row + rows, axis=lead).reshape(head + (-1,))
        out.append(lax.slice_in_dim(chunk, 0, n, axis=lead).reshape(head + tuple(s)))
        row += rows
    return out


def _to_natural(stacked, kind):
    if kind == "col":
        m = jnp.moveaxis(stacked, 0, -2)
        return m.reshape(m.shape[:-2] + (m.shape[-2] * m.shape[-1],))
    m = jnp.moveaxis(stacked, 0, -3)
    return m.reshape(m.shape[:-3] + (m.shape[-3] * m.shape[-2], m.shape[-1]))


def _to_stacked(natural, kind):
    if kind == "col":
        m = natural.reshape(natural.shape[:-1] + (N_DEV, natural.shape[-1] // N_DEV))
        return jnp.moveaxis(m, -2, 0)
    m = natural.reshape(natural.shape[:-2] + (N_DEV, natural.shape[-2] // N_DEV, natural.shape[-1]))
    return jnp.moveaxis(m, -3, 0)


def _f_rms(h, g):
    return (_rms(h, g).astype(BF16),)


@jax.custom_vjp
def _swiglu(g, u):
    return g * _sigmoid(g) * u


def _swiglu_fwd(g, u):
    s = _sigmoid(g)
    return g * s * u, (g, u, s)


def _swiglu_bwd(saved, ct):
    g, u, s = saved
    return ct * u * (s * (1.0 + g * (1.0 - s))), ct * (g * s)


_swiglu.defvjp(_swiglu_fwd, _swiglu_bwd)


def _f_swiglu(g, u):
    return (_swiglu(g.astype(F32), u.astype(F32)).astype(BF16),)


def _f_ple(h, pre, pp):
    return (h + _sigmoid(pre) * pp,)


def _head_rms(x, g):
    ms = _seg_sum(x * x, HEAD_DIM) * (1.0 / HEAD_DIM)
    return x * _seg_expand(lax.rsqrt(ms + RMS_EPS), HEAD_DIM) * _tile_lanes(g, x.shape[1] // HEAD_DIM)


def _f_attpre(q, k, q_gain, k_gain):
    return _head_rms(q, q_gain) * (HEAD_DIM ** -0.5), _head_rms(k, k_gain)


def _f_shift(z, z_prev, mu):
    return (z + (z_prev - z) * mu,)


def _f_rwkvpre(r, k, v, xw, xa, xg, w0, w_up, a0, a_up, g_up, k_k, k_a):
    del r, v
    w_log = -_softplus(-(w0 + _mm(jnp.tanh(xw), w_up))) - 0.5
    decay = jnp.exp(-jnp.exp(w_log))
    a = _sigmoid(a0 + _mm(xa, a_up))
    g = _mm(_sigmoid(xg), g_up)
    kk = k * k_k
    norm = jnp.sqrt(_seg_expand(_seg_sum(kk * kk, HEAD_DIM), HEAD_DIM))
    kk = kk / jnp.maximum(norm, 1e-12)
    return k * (1.0 + (a - 1.0) * k_a), decay, -kk, kk * a, g


def _f_rwkvpost(y, r, k, v, g, lnx_w, lnx_b, r_k):
    mean = _seg_expand(_seg_sum(y, HEAD_DIM) * (1.0 / HEAD_DIM), HEAD_DIM)
    yc = y - mean
    var = _seg_expand(_seg_sum(yc * yc, HEAD_DIM) * (1.0 / HEAD_DIM), HEAD_DIM)
    yn = yc * lax.rsqrt(var + GN_EPS) * lnx_w + lnx_b
    bonus = _seg_expand(_seg_sum(r * k * r_k, HEAD_DIM), HEAD_DIM) * v
    return ((yn + bonus) * g,)


def _f_s5disc(lam_re, lam_im, log_dt):
    dt = jnp.exp(log_dt)
    mag = jnp.exp(lam_re * dt)
    ab_re, ab_im = mag * jnp.cos(lam_im * dt), mag * jnp.sin(lam_im * dt)
    denom = lam_re * lam_re + lam_im * lam_im
    z_re = ((ab_re - 1.0) * lam_re + ab_im * lam_im) / denom
    z_im = (ab_im * lam_re - (ab_re - 1.0) * lam_im) / denom
    return ab_re, ab_im, z_re, z_im


def _f_s5b(z_re, z_im, b_re, b_im):
    return z_re * b_re - z_im * b_im, z_re * b_im + z_im * b_re


def _f_s5post(ypre, u, d_skip):
    return (_gelu_tanh(ypre + d_skip * u).astype(BF16),)


def _f_glu(h, z1, z2):
    return (h + z1 * _sigmoid(z2),)


def _shift_down(x):
    return jnp.pad(x[:-1], ((1, 0), (0, 0)))


def _shift_up(x):
    return jnp.pad(x[1:], ((0, 1), (0, 0)))


def _block_diag(blocks):
    g, a, b = blocks.shape
    eye = jnp.eye(g, dtype=blocks.dtype)
    return (blocks[:, :, None, :] * eye[:, None, :, None]).reshape(g * a, g * b)


def _diag_blocks(dense, g):
    a, b = dense.shape[0] // g, dense.shape[1] // g
    return jnp.stack([dense[i * a:(i + 1) * a, i * b:(i + 1) * b] for i in range(g)], axis=0)


def _row_tile(t, width_bytes, budget=2 * 1024 * 1024):
    for tr in (512, 256, 128, 64, 32, 16, 8):
        if t % tr == 0 and tr * width_bytes <= budget:
            return tr
    return t


def _no_hosting(name):
    return None, None


def _hosting_matmul(hosting, name, a, b, **kw):
    exchange, done = hosting(name)
    if exchange is None:
        return matmul(name, a, b, **kw)
    out, travelled = matmul(name, a, b, hosted=exchange, **kw)
    done(travelled)
    return out


def _ffn_fwd(tag, h, norm_w, weight, hosting=_no_hosting):
    t, d = h.shape
    n = rowwise(f"{tag}_norm", _f_rms, [h], [norm_w], [(d, BF16)], _row_tile(t, 4 * d))[0]
    g = _hosting_matmul(hosting, f"{tag}_gate", n, weight("gate"), out_dtype=BF16)
    u = _hosting_matmul(hosting, f"{tag}_up", n, weight("up"), out_dtype=BF16)
    f = g.shape[1]
    a = rowwise(f"{tag}_act", _f_swiglu, [g, u], [], [(f, BF16)], _row_tile(t, 4 * f))[0]
    return _hosting_matmul(hosting, f"{tag}_down", a, weight("down"), alpha=0.5, res=h), (h, n, g, u, a)


def _ffn_bwd(tag, dh2, saved, norm_w, w_gate, w_up, w_down):
    h, n, g, u, a = saved
    t, d = h.shape
    f = w_gate.shape[1]
    da = matmul(f"{tag}_da", dh2, w_down, tb=True, alpha=0.5, out_dtype=BF16)
    d_down = matmul(f"{tag}_dwdown", a, dh2, ta=True, alpha=0.5, out_dtype=BF16)
    dg, du = rowwise_vjp(f"{tag}_dact", _f_swiglu, [g, u], [], [da], _row_tile(t, 4 * f, 1 << 22), [True, True], [],
                         grad_dtypes={0: BF16, 1: BF16})
    dn = matmul(f"{tag}_dn_gate", dg, w_gate, tb=True)
    dn = matmul(f"{tag}_dn_up", du, w_up, tb=True, res=dn)
    d_gate = matmul(f"{tag}_dwgate", n, dg, ta=True, out_dtype=BF16)
    d_up = matmul(f"{tag}_dwup", n, du, ta=True, out_dtype=BF16)
    dh, d_norm = rowwise_vjp(f"{tag}_dnorm", _f_rms, [h], [norm_w], [dn], _row_tile(t, 4 * d), [True], [True],
                             add_to={0: dh2})
    return dh, d_norm, d_gate, d_up, d_down


def _ple_fwd(tag, h, norm_w, w_gate, w_proj, p_i, hosting=_no_hosting):
    t, d = h.shape
    n = rowwise(f"{tag}_norm", _f_rms, [h], [norm_w], [(d, BF16)], _row_tile(t, 4 * d))[0]
    pre = _hosting_matmul(hosting, f"{tag}_gate", n, w_gate)
    pp = matmul(f"{tag}_proj", p_i, w_proj)
    h2 = rowwise(f"{tag}_out", _f_ple, [h, pre, pp], [], [(d, F32)], _row_tile(t, 4 * d))[0]
    return h2, (h, n, pre, pp)


def _ple_bwd(tag, dh2, saved, norm_w, w_gate, w_proj, p_i):
    h, n, pre, pp = saved
    t, d = h.shape
    dpre, dpp = rowwise_vjp(f"{tag}_dout", _f_ple, [h, pre, pp], [], [dh2], _row_tile(t, 4 * d), [False, True, True], [],
                            grad_dtypes={1: BF16, 2: BF16})
    d_proj = matmul(f"{tag}_dwproj", p_i, dpp, ta=True, out_dtype=BF16)
    d_gate = matmul(f"{tag}_dwgate", n, dpre, ta=True, out_dtype=BF16)
    dn = matmul(f"{tag}_dn", dpre, w_gate, tb=True)
    dh, d_norm = rowwise_vjp(f"{tag}_dnorm", _f_rms, [h], [norm_w], [dn], _row_tile(t, 4 * d), [True], [True],
                             add_to={0: dh2})
    return dh, d_norm, d_gate, d_proj


def _ab_fwd(h, w, hosting=_no_hosting):
    t, d = h.shape
    da = d // 2
    n = rowwise("ab_norm", _f_rms, [h], [w["mix_norm"]], [(d, BF16)], _row_tile(t, 4 * d))[0]
    proj = _hosting_matmul(hosting, "ab_in", n, w["ab_w_in"])
    q_raw, k_raw, v_att, z = proj[:, :da], proj[:, da:2 * da], proj[:, 2 * da:3 * da], proj[:, 3 * da:]
    tr = _row_tile(t, 4 * da, 1 << 19)
    qn, kn = rowwise("att_pre", _f_attpre, [q_raw, k_raw], [w["att_q_gain"], w["att_k_gain"]], [(da, F32)] * 2, tr)
    bias = jnp.transpose(bias_expand(w["att_rel_bias"]), (1, 0, 2))
    qh, kh, vh = qn, kn, v_att
    exchange, done = hosting("attention_fwd")
    att = attention_fwd(qh, kh, vh, bias, hosted=exchange)
    if exchange is not None:
        att, travelled = att
        done(travelled)
    z_prev = _shift_down(z)
    zz = rowwise("rwkv_shift", _f_shift, [z, z_prev], [w["rwkv_mu"]], [(z.shape[1], F32)], _row_tile(t, 4 * z.shape[1]))[0]
    o = [0, da, 2 * da, 3 * da, 3 * da + DECAY_LORA, 3 * da + DECAY_LORA + AAA_LORA, z.shape[1]]
    r, k, v, xw, xa, xg = (zz[:, o[i]:o[i + 1]] for i in range(6))
    pre_consts = [w[nm] for nm in ("rwkv_w0", "rwkv_w_up", "rwkv_a0", "rwkv_a_up", "rwkv_g_up", "rwkv_k_k", "rwkv_k_a")]
    k2, decay, ia, ib, g = rowwise("rwkv_pre", _f_rwkvpre, [r, k, v, xw, xa, xg], pre_consts, [(da, F32)] * 5, tr)
    heads = [r, k2, v, decay, ia, ib]
    exchange, done = hosting("rwkv_fwd")
    y, states, *travelled = rwkv_fwd(*heads, hosted=exchange)
    if exchange is not None:
        done(travelled)
    post_consts = [w["rwkv_lnx_w"], w["rwkv_lnx_b"], w["rwkv_r_k"]]
    rw = rowwise("rwkv_post", _f_rwkvpost, [y, r, k2, v, g], post_consts, [(da, F32)], tr)[0]
    cat = jnp.concatenate([att, rw], axis=1).astype(BF16)
    h2 = _hosting_matmul(hosting, "ab_out", cat, w["ab_w_out"], res=h)
    saved = dict(h=h, n=n, q_raw=q_raw, k_raw=k_raw, qh=qh, kh=kh, vh=vh, bias=bias, z=z, z_prev=z_prev,
                 rows=(r, k, v, xw, xa, xg), pre_consts=pre_consts, post=(y, r, k2, v, g), post_consts=post_consts,
                 heads=heads, states=states, cat=cat, tr=tr)
    return h2, saved


def _ab_bwd(dh2, s, w, make_hosted=None, hosted_done=None):
    h = s["h"]
    t, d = h.shape
    da = d // 2
    tr = s["tr"]
    grads = {}
    dcat = matmul("ab_dcat", dh2, w["ab_w_out"], tb=True)
    grads["ab_w_out"] = matmul("ab_dwout", s["cat"], dh2, ta=True, out_dtype=BF16)
    d_att, d_rw = dcat[:, :da], dcat[:, da:]
    dy, dr1, dk1, dv1, dg, grads["rwkv_lnx_w"], grads["rwkv_lnx_b"], grads["rwkv_r_k"] = rowwise_vjp(
        "rwkv_dpost", _f_rwkvpost, list(s["post"]), s["post_consts"], [d_rw], tr, [True] * 5, [True] * 3)
    hosted = None if make_hosted is None else make_hosted(grads["ab_w_out"])
    *d_heads, = rwkv_bwd(*s["heads"], s["states"], dy, hosted=hosted)
    if hosted is not None:
        hosted_done(d_heads[6:])
        d_heads = d_heads[:6]
    dr2, dk2, dv2, ddecay, dia, dib = d_heads
    pre = rowwise_vjp("rwkv_dpre", _f_rwkvpre, list(s["rows"]), s["pre_consts"], [(dk1, dk2), ddecay, dia, dib, dg], tr,
                      [True] * 6, [True] * 7, add_to={0: (dr1, dr2), 2: (dv1, dv2)})
    for nm, g in zip(("rwkv_w0", "rwkv_w_up", "rwkv_a0", "rwkv_a_up", "rwkv_g_up", "rwkv_k_k", "rwkv_k_a"), pre[6:]):
        grads[nm] = g
    dzz = jnp.concatenate(pre[:6], axis=1)
    trz = _row_tile(t, 4 * dzz.shape[1])
    grads["rwkv_mu"] = rowwise_vjp("rwkv_dmu", _f_shift, [s["z"], s["z_prev"]], [w["rwkv_mu"]], [dzz], trz,
                                   [False, False], [True])[0]
    dz = rowwise("rwkv_dshift", _f_shift, [dzz, _shift_up(dzz)], [w["rwkv_mu"]], [(dzz.shape[1], F32)], trz)[0]
    dqh, dkh, dvh, dbias = attention_bwd(s["qh"], s["kh"], s["vh"], s["bias"], d_att)
    grads["att_rel_bias"] = bias_reduce(jnp.transpose(dbias, (1, 0, 2)))
    dq_raw, dk_raw, grads["att_q_gain"], grads["att_k_gain"] = rowwise_vjp(
        "att_dpre", _f_attpre, [s["q_raw"], s["k_raw"]], [w["att_q_gain"], w["att_k_gain"]],
        [dqh, dkh], tr, [True, True], [True, True])
    dproj = jnp.concatenate([dq_raw, dk_raw, dvh, dz], axis=1).astype(BF16)
    dn = matmul("ab_dn", dproj, w["ab_w_in"], tb=True)
    grads["ab_w_in"] = matmul("ab_dwin", s["n"], dproj, ta=True, out_dtype=BF16)
    dh, grads["mix_norm"] = rowwise_vjp("ab_dnorm", _f_rms, [h], [w["mix_norm"]], [dn], _row_tile(t, 4 * d), [True], [True],
                                        add_to={0: dh2})
    return dh, grads


def _s5_fwd(h, w):
    t, d = h.shape
    n_groups, n_state = w["ssm_lambda_re"].shape
    gp = n_groups * n_state
    n = rowwise("s5_norm", _f_rms, [h], [w["mix_norm"]], [(d, BF16)], _row_tile(t, 4 * d))[0]
    u = matmul("s5_in", n, w["ssm_w_in"])
    disc_rows = [w["ssm_lambda_re"], w["ssm_lambda_im"], w["ssm_log_dt"]]
    ab_re, ab_im, z_re, z_im = rowwise("s5_disc", _f_s5disc, disc_rows, [], [(n_state, F32)] * 4, n_groups)
    b_rows = [z_re.reshape(gp, 1), z_im.reshape(gp, 1), w["ssm_b_re"], w["ssm_b_im"]]
    trb = _row_tile(gp, 512)
    bb_re, bb_im = rowwise("s5_bbar", _f_s5b, b_rows, [], [(SSM_GROUP, F32)] * 2, trb)
    to_dense = lambda bb: _block_diag(jnp.transpose(bb.reshape(n_groups, n_state, SSM_GROUP), (0, 2, 1)))
    bd_re, bd_im = to_dense(bb_re), to_dense(bb_im)
    cd_re = _block_diag(jnp.transpose(w["ssm_c_re"], (0, 2, 1)))
    cd_im = -_block_diag(jnp.transpose(w["ssm_c_im"], (0, 2, 1)))
    a_re, a_im = ab_re.reshape(1, gp), ab_im.reshape(1, gp)
    bu_re = matmul("s5_bu_re", u, bd_re)
    bu_im = matmul("s5_bu_im", u, bd_im)
    h_re, h_im = s5_scan(bu_re, bu_im, a_re, a_im)
    ypre = matmul("s5_y_re", h_re, cd_re)
    ypre = matmul("s5_y_im", h_im, cd_im, res=ypre)
    tru = _row_tile(t, 4 * u.shape[1])
    yg = rowwise("s5_post", _f_s5post, [ypre, u], [w["ssm_d"]], [(u.shape[1], BF16)], tru)[0]
    w_out1, w_out2 = w["ssm_w_out"][:, :d], w["ssm_w_out"][:, d:]
    z1 = matmul("s5_out1", yg, w_out1)
    z2 = matmul("s5_out2", yg, w_out2)
    h2 = rowwise("s5_glu", _f_glu, [h, z1, z2], [], [(d, F32)], _row_tile(t, 4 * d))[0]
    saved = dict(h=h, n=n, u=u, disc_rows=disc_rows, b_rows=b_rows, trb=trb, bd=(bd_re, bd_im), cd=(cd_re, cd_im),
                 a=(a_re, a_im), hs=(h_re, h_im), ypre=ypre, yg=yg, w_out=(w_out1, w_out2), z=(z1, z2), tru=tru)
    return h2, saved


def _s5_bwd(dh2, s, w):
    h, u = s["h"], s["u"]
    t, d = h.shape
    n_groups, n_state = w["ssm_lambda_re"].shape
    gp = n_groups * n_state
    grads = {}
    z1, z2 = s["z"]
    w_out1, w_out2 = s["w_out"]
    dz1, dz2 = rowwise_vjp("s5_dglu", _f_glu, [h, z1, z2], [], [dh2], _row_tile(t, 4 * d), [False, True, True], [],
                           grad_dtypes={1: BF16, 2: BF16})
    dyg = matmul("s5_dyg1", dz1, w_out1, tb=True)
    dyg = matmul("s5_dyg2", dz2, w_out2, tb=True, res=dyg)
    grads["ssm_w_out"] = jnp.concatenate([matmul("s5_dwout1", s["yg"], dz1, ta=True, out_dtype=BF16),
                                          matmul("s5_dwout2", s["yg"], dz2, ta=True, out_dtype=BF16)], axis=1)
    dypre, du1, grads["ssm_d"] = rowwise_vjp("s5_dpost", _f_s5post, [s["ypre"], u], [w["ssm_d"]], [dyg], s["tru"],
                                             [True, True], [True])
    cd_re, cd_im = s["cd"]
    h_re, h_im = s["hs"]
    dh_re = matmul("s5_dh_re", dypre, cd_re, tb=True)
    dh_im = matmul("s5_dh_im", dypre, cd_im, tb=True)
    dcd_re = matmul("s5_dc_re", h_re, dypre, ta=True)
    dcd_im = matmul("s5_dc_im", h_im, dypre, ta=True)
    a_re, a_im = s["a"]
    g_re, g_im, da_re, da_im = s5_scan(dh_re, dh_im, a_re, a_im, reverse=True, h_prev=(h_re, h_im))
    bd_re, bd_im = s["bd"]
    du = matmul("s5_du_re", g_re, bd_re, tb=True, res=du1)
    du = matmul("s5_du_im", g_im, bd_im, tb=True, res=du)
    dbd_re = matmul("s5_db_re", u, g_re, ta=True)
    dbd_im = matmul("s5_db_im", u, g_im, ta=True)
    grads["ssm_w_in"] = matmul("s5_dwin", s["n"], du, ta=True, out_dtype=BF16)
    dn = matmul("s5_dn", du, w["ssm_w_in"], tb=True)
    dh, grads["mix_norm"] = rowwise_vjp("s5_dnorm", _f_rms, [h], [w["mix_norm"]], [dn], _row_tile(t, 4 * d), [True], [True],
                                        add_to={0: dh2})
    from_dense = lambda m: jnp.transpose(_diag_blocks(m, n_groups), (0, 2, 1)).reshape(gp, SSM_GROUP)
    dz_re, dz_im, grads["ssm_b_re"], grads["ssm_b_im"] = rowwise_vjp(
        "s5_dbbar", _f_s5b, s["b_rows"], [], [from_dense(dbd_re), from_dense(dbd_im)], s["trb"], [True] * 4, [])
    disc_cots = [da_re.reshape(n_groups, n_state), da_im.reshape(n_groups, n_state),
                 dz_re.reshape(n_groups, n_state), dz_im.reshape(n_groups, n_state)]
    grads["ssm_lambda_re"], grads["ssm_lambda_im"], grads["ssm_log_dt"] = rowwise_vjp(
        "s5_ddisc", _f_s5disc, s["disc_rows"], [], disc_cots, n_groups, [True] * 3, [])
    grads["ssm_c_re"] = jnp.transpose(_diag_blocks(dcd_re, n_groups), (0, 2, 1))
    grads["ssm_c_im"] = -jnp.transpose(_diag_blocks(dcd_im, n_groups), (0, 2, 1))
    return dh, grads


WEIGHTS = ["ffn1_norm", "ffn1_w_gate", "ffn1_w_up", "ffn1_w_down", "mix_norm", "ffn2_norm", "ffn2_w_gate", "ffn2_w_up",
           "ffn2_w_down", "ple_norm", "ple_w_gate", "ple_w_proj", "ab_w_in", "att_q_gain", "att_k_gain", "att_rel_bias",
           "rwkv_mu", "rwkv_w0", "rwkv_w_up", "rwkv_a0", "rwkv_a_up", "rwkv_g_up", "rwkv_k_k", "rwkv_k_a", "rwkv_r_k",
           "rwkv_lnx_w", "rwkv_lnx_b", "ab_w_out", "ssm_w_in", "ssm_lambda_re", "ssm_lambda_im", "ssm_log_dt", "ssm_b_re",
           "ssm_b_im", "ssm_c_re", "ssm_c_im", "ssm_d", "ssm_w_out"]
BIG = {"ffn1_w_gate": "col", "ffn1_w_up": "col", "ffn1_w_down": "row", "ffn2_w_gate": "col", "ffn2_w_up": "col",
       "ffn2_w_down": "row", "ple_w_gate": "row", "ple_w_proj": "col", "ab_w_in": "col", "ab_w_out": "row",
       "ssm_w_in": "row", "ssm_w_out": "col"}
SMALL_CUT = {"rwkv_w_up": "col", "rwkv_a_up": "col", "rwkv_g_up": "col", "ssm_d": "col"}
REPLICATED = [n for n in WEIGHTS if n not in BIG and n not in SMALL_CUT]


def kernel(x, p, ffn1_norm, ffn1_w_gate, ffn1_w_up, ffn1_w_down, mix_norm, ffn2_norm, ffn2_w_gate, ffn2_w_up, ffn2_w_down, ple_norm, ple_w_gate, ple_w_proj, ab_w_in, att_q_gain, att_k_gain, att_rel_bias, rwkv_mu, rwkv_w0, rwkv_w_up, rwkv_a0, rwkv_a_up, rwkv_g_up, rwkv_k_k, rwkv_k_a, rwkv_r_k, rwkv_lnx_w, rwkv_lnx_b, ab_w_out, ssm_w_in, ssm_lambda_re, ssm_lambda_im, ssm_log_dt, ssm_b_re, ssm_b_im, ssm_c_re, ssm_c_im, ssm_d, ssm_w_out, loss_target, m_ffn1_norm, m_ffn1_w_gate, m_ffn1_w_up, m_ffn1_w_down, m_mix_norm, m_ffn2_norm, m_ffn2_w_gate, m_ffn2_w_up, m_ffn2_w_down, m_ple_norm, m_ple_w_gate, m_ple_w_proj, m_ab_w_in, m_att_q_gain, m_att_k_gain, m_att_rel_bias, m_rwkv_mu, m_rwkv_w0, m_rwkv_w_up, m_rwkv_a0, m_rwkv_a_up, m_rwkv_g_up, m_rwkv_k_k, m_rwkv_k_a, m_rwkv_r_k, m_rwkv_lnx_w, m_rwkv_lnx_b, m_ab_w_out, m_ssm_w_in, m_ssm_lambda_re, m_ssm_lambda_im, m_ssm_log_dt, m_ssm_b_re, m_ssm_b_im, m_ssm_c_re, m_ssm_c_im, m_ssm_d, m_ssm_w_out, v_ffn1_norm, v_ffn1_w_gate, v_ffn1_w_up, v_ffn1_w_down, v_mix_norm, v_ffn2_norm, v_ffn2_w_gate, v_ffn2_w_up, v_ffn2_w_down, v_ple_norm, v_ple_w_gate, v_ple_w_proj, v_ab_w_in, v_att_q_gain, v_att_k_gain, v_att_rel_bias, v_rwkv_mu, v_rwkv_w0, v_rwkv_w_up, v_rwkv_a0, v_rwkv_a_up, v_rwkv_g_up, v_rwkv_k_k, v_rwkv_k_a, v_rwkv_r_k, v_rwkv_lnx_w, v_rwkv_lnx_b, v_ab_w_out, v_ssm_w_in, v_ssm_lambda_re, v_ssm_lambda_im, v_ssm_log_dt, v_ssm_b_re, v_ssm_b_im, v_ssm_c_re, v_ssm_c_im, v_ssm_d, v_ssm_w_out):
    vals = dict(locals())
    depth = ffn1_norm.shape[0]
    n_groups, n_state = ssm_lambda_re.shape[1:]

    kinds = {**BIG, **SMALL_CUT}
    travel = {n: (BF16 if n in BIG else F32) for n in kinds}

    def rows(a, lead=0):
        return a.reshape(a.shape[:lead] + (-1, a.shape[-1]))

    def model_layer(n, j):
        if vals[n].shape[0] == depth:
            return j
        return 2 * j if n.startswith(("ab_", "rwkv_")) else 2 * j + 1

    units = [(n, j) for n in kinds for j in range(vals[n].shape[0])]
    early_names = ("ffn1_w_gate", "ffn1_w_up", "ffn1_w_down", "ab_w_in", "rwkv_w_up", "rwkv_a_up", "rwkv_g_up")
    early = [u for u in units if model_layer(*u) == 0 and u[0] in early_names]
    late = [u for u in units if u not in early]
    full = {}

    def shard(unit):
        n, j = unit
        return rows(vals[n][j]).astype(travel[n])

    def take_gathered(which, got):
        for (n, j), g in zip(which, got):
            full[(n, j)] = _to_natural(g.reshape((N_DEV,) + vals[n].shape[1:]), kinds[n])

    first = [("ffn1_w_gate", 0), ("ffn1_w_up", 0)]
    beside = {"l0_ffn1_gate": [("ffn1_w_down", 0)],
              "l0_ffn1_up": [("ab_w_in", 0), ("rwkv_w_up", 0), ("rwkv_a_up", 0), ("rwkv_g_up", 0)],
              "l0_ffn1_down": [("ab_w_out", 0), ("ffn2_w_gate", 0)],
              "ab_in": [("ffn2_w_up", 0)],
              "attention_fwd": [("ffn2_w_down", 0), ("ple_w_gate", 0), ("ple_w_proj", 0)],
              "l0_ffn2_gate": [("ffn2_w_gate", 1)], "l0_ffn2_up": [("ffn2_w_up", 1)], "l0_ffn2_down": [("ffn2_w_down", 1)],
              "l0_ple_gate": [("ple_w_gate", 1), ("ple_w_proj", 1)]}
    planned = first + [u for us in beside.values() for u in us]
    beside["rwkv_fwd"] = [u for u in units if u not in planned]
    assert depth == 2 and sorted(planned + beside["rwkv_fwd"]) == sorted(units)
    take_gathered(first, gather_all([shard(u) for u in first]).run("gather_first"))

    def hosting(name):
        which = beside.get(name)
        if not which:
            return None, None
        return gather_all([shard(u) for u in which]), functools.partial(take_gathered, which)

    def row(name, j):
        return vals[name][j].reshape(1, -1)

    def ffn_weights(which, i):
        return (row(f"{which}_norm", i), full[(f"{which}_w_gate", i)], full[(f"{which}_w_up", i)],
                full[(f"{which}_w_down", i)])

    def mixer_weights(i):
        j = i // 2
        if i % 2 == 0:
            w = {n: row(n, j) for n in ("att_q_gain", "att_k_gain", "rwkv_mu", "rwkv_w0", "rwkv_a0", "rwkv_k_k", "rwkv_k_a",
                                        "rwkv_r_k", "rwkv_lnx_w", "rwkv_lnx_b")}
            w.update({n: full.get((n, j)) for n in ("ab_w_in", "ab_w_out", "rwkv_w_up", "rwkv_a_up", "rwkv_g_up")})
            w["att_rel_bias"] = att_rel_bias[j]
        else:
            w = {"ssm_lambda_re": ssm_lambda_re[j], "ssm_lambda_im": ssm_lambda_im[j],
                 "ssm_log_dt": ssm_log_dt[j].reshape(n_groups, 1),
                 "ssm_b_re": ssm_b_re[j].reshape(n_groups * n_state, -1),
                 "ssm_b_im": ssm_b_im[j].reshape(n_groups * n_state, -1),
                 "ssm_c_re": ssm_c_re[j], "ssm_c_im": ssm_c_im[j], "ssm_d": full[("ssm_d", j)].reshape(1, -1),
                 "ssm_w_in": full[("ssm_w_in", j)], "ssm_w_out": full[("ssm_w_out", j)]}
        w["mix_norm"] = row("mix_norm", i)
        return w

    def ple_weights(i):
        return (row("ple_norm", i), full[("ple_w_gate", i)], full[("ple_w_proj", i)], p[i, 0])

    h = x[0]
    saved = []
    for i in range(depth):
        h, s1 = _ffn_fwd(f"l{i}_ffn1", h, row("ffn1_norm", i), lambda kind, i=i: full[(f"ffn1_w_{kind}", i)], hosting)
        h, sm = _ab_fwd(h, mixer_weights(i), hosting) if i % 2 == 0 else _s5_fwd(h, mixer_weights(i))
        h, s2 = _ffn_fwd(f"l{i}_ffn2", h, row("ffn2_norm", i), lambda kind, i=i: full[(f"ffn2_w_{kind}", i)], hosting)
        h, sp = _ple_fwd(f"l{i}_ple", h, *ple_weights(i), hosting)
        saved.append((s1, sm, s2, sp))
    dh, loss_part = loss_head(h, loss_target[0])

    per_layer = {n: [] for n in WEIGHTS}
    received = {}

    def chip_sums(which, tag):
        pieces = [rows(_to_stacked(per_layer[n][j - vals[n].shape[0]], kinds[n]).astype(travel[n]), lead=1)
                  for n, j in which]
        from_sibling = pair_exchange(pieces).run(f"reduce_pair_{tag}")
        return [pair_sum(f"pair_sum_{n}{j}", a, b) for (n, j), a, b in zip(which, pieces, from_sibling)]

    def late_reduce(d_ab_w_out):
        per_layer["ab_w_out"].insert(0, d_ab_w_out)
        return chip_exchange(chip_sums(late, "late"))

    def late_reduce_done(got):
        received.update(zip(late, got))

    for i in reversed(range(depth)):
        s1, sm, s2, sp = saved[i]
        dh, d_norm, d_gate, d_proj = _ple_bwd(f"l{i}_ple", dh, sp, *ple_weights(i))
        for n, g in (("ple_norm", d_norm), ("ple_w_gate", d_gate), ("ple_w_proj", d_proj)):
            per_layer[n].insert(0, g)
        dh, d_norm, d_gate, d_up, d_down = _ffn_bwd(f"l{i}_ffn2", dh, s2, *ffn_weights("ffn2", i))
        for n, g in (("ffn2_norm", d_norm), ("ffn2_w_gate", d_gate), ("ffn2_w_up", d_up), ("ffn2_w_down", d_down)):
            per_layer[n].insert(0, g)
        if i == 0:
            dh, mixer_grads = _ab_bwd(dh, sm, mixer_weights(i), late_reduce, late_reduce_done)
        else:
            dh, mixer_grads = (_ab_bwd if i % 2 == 0 else _s5_bwd)(dh, sm, mixer_weights(i))
        for n, g in mixer_grads.items():
            if not (i == 0 and n == "ab_w_out"):
                per_layer[n].insert(0, g)
        dh, d_norm, d_gate, d_up, d_down = _ffn_bwd(f"l{i}_ffn1", dh, s1, *ffn_weights("ffn1", i))
        for n, g in (("ffn1_norm", d_norm), ("ffn1_w_gate", d_gate), ("ffn1_w_up", d_up), ("ffn1_w_down", d_down)):
            per_layer[n].insert(0, g)
    grad_x = dh[None]

    received.update(zip(early, chip_exchange(chip_sums(early, "early")).run("reduce_chips_early")))

    rep_mine = _pack([jnp.stack(per_layer[n], axis=0).reshape(vals[n].shape) for n in REPLICATED] + [loss_part], F32)
    rep_all = gather_all([rep_mine]).run("gather_replicated")[0]

    out = {}
    for n in kinds:
        got = [received[(n, j)] for j in range(vals[n].shape[0])]
        got = got[0] if len(got) == 1 else jnp.concatenate(got, axis=1)
        state = [rows(vals[pre + n]) for pre in ("", "m_", "v_")]
        out[n] = tuple(r.reshape(vals[n].shape) for r in adam_update(f"adam_{n}", got, *state))
    zero = jnp.zeros((1, 1), F32)
    state = [_pack([vals[pre + n] for n in REPLICATED] + [zero], F32) for pre in ("", "m_", "v_")]
    shapes = [vals[n].shape for n in REPLICATED] + [zero.shape]
    results = [_unpack(r, shapes) for r in adam_update("adam_replicated", rep_all, *state)]
    for q, n in enumerate(REPLICATED):
        out[n] = tuple(r[q] for r in results)
    loss = results[0][-1].reshape(())
    return (loss, grad_x, *[out[n][0] for n in WEIGHTS], *[out[n][1] for n in WEIGHTS], *[out[n][2] for n in WEIGHTS],
            *[out[n][3] for n in WEIGHTS])
```

```python
import functools
import math

import jax
import jax.numpy as jnp
import numpy as np
from jax import lax
from jax.experimental import pallas as pl
from jax.experimental.pallas import tpu as pltpu

F32 = jnp.float32
BF16 = jnp.bfloat16
HIGHEST = lax.Precision.HIGHEST
MESH_AXES = ("x", "y", "c")
N_DEV = 8

CHUNK = 64
N_LEFT_CHUNKS = 8
BAND = (N_LEFT_CHUNKS + 1) * CHUNK
HEAD_DIM = 64
REL_CLIP = 128
N_REL = (CHUNK - 1) + REL_CLIP + 1
DECAY_LORA = 64
AAA_LORA = 64
GATE_LORA = 128
SSM_GROUP = 16
SSM_STATE = 64
RMS_EPS = 1e-6
GN_EPS = 64e-5
ADAM_LR = 0.001
ADAM_B1 = 0.9
ADAM_B2 = 0.999
ADAM_EPS = 1e-08
ADAM_WD = 0.01
ADAM_STEP = 10

RWKV_CHUNK = 64
VMEM_LIMIT = 56 * 1024 * 1024
LANES = 128


def _params(semantics):
    return pltpu.CompilerParams(dimension_semantics=semantics, vmem_limit_bytes=VMEM_LIMIT)


def _pick(n, prefs):
    for t in prefs:
        if n % t == 0:
            return t
    return n


def _dot(a, b, dims):
    return lax.dot_general(a, b, (dims, ((), ())), precision=HIGHEST, preferred_element_type=F32)


def _mm(a, b):
    return _dot(a, b, ((1,), (0,)))


def _mm_nt(a, b):
    return _dot(a, b, ((1,), (1,)))


def _mm_tn(a, b):
    return _dot(a, b, ((0,), (0,)))


def _split2(x):
    hi = x.astype(BF16)
    return hi, (x - hi.astype(F32)).astype(BF16)


def _dot3_raw(a, b, dims):
    a_hi, a_lo = _split2(a)
    b_hi, b_lo = _split2(b)
    dot = lambda p, q: lax.dot_general(p, q, (dims, ((), ())), preferred_element_type=F32)
    return dot(a_hi, b_hi) + (dot(a_hi, b_lo) + dot(a_lo, b_hi))


def _dot1_raw(a, b, dims):
    return lax.dot_general(a.astype(BF16), b.astype(BF16), (dims, ((), ())), preferred_element_type=F32)


NN, NT, TN = ((1,), (0,)), ((1,), (1,)), ((0,), (0,))


def _make_dot(raw):
    @functools.partial(jax.custom_vjp, nondiff_argnums=(2,))
    def dot(a, b, dims):
        return raw(a, b, dims)

    def fwd(a, b, dims):
        return raw(a, b, dims), (a, b)

    def bwd(dims, saved, g):
        a, b = saved
        if dims == NN:
            return raw(g, b, NT), raw(a, g, TN)
        if dims == NT:
            return raw(g, b, NN), raw(g, a, TN)
        return raw(b, g, NT), raw(a, g, NN)

    dot.defvjp(fwd, bwd)
    return dot


_dot3 = _make_dot(_dot3_raw)
_dot1 = _make_dot(_dot1_raw)


def _dot_ind_raw(x, ind, dims):
    hi = x.astype(BF16)
    rest = x - hi.astype(F32)
    mid = rest.astype(BF16)
    lo = (rest - mid.astype(F32)).astype(BF16)
    ind = ind.astype(BF16)
    dot = lambda p: lax.dot_general(p, ind, (dims, ((), ())), preferred_element_type=F32)
    return dot(hi) + (dot(mid) + dot(lo))


@jax.custom_vjp
def _mm_ind(x, ind):
    return _dot_ind_raw(x, ind, NN)


def _mm_ind_fwd(x, ind):
    return _dot_ind_raw(x, ind, NN), ind


def _mm_ind_bwd(ind, g):
    return _dot_ind_raw(g, ind, NT), jnp.zeros_like(ind)


_mm_ind.defvjp(_mm_ind_fwd, _mm_ind_bwd)


def _sigmoid(x):
    return 1.0 / (1.0 + jnp.exp(-x))


def _softplus(x):
    return jnp.maximum(x, 0.0) + jnp.log(1.0 + jnp.exp(-jnp.abs(x)))


def _gelu_tanh(x):
    return 0.5 * x * (1.0 + jnp.tanh(math.sqrt(2.0 / math.pi) * (x + 0.044715 * (x * x * x))))


def _seg_indicator(n, seg):
    r = lax.broadcasted_iota(jnp.int32, (n, n // seg), 0)
    c = lax.broadcasted_iota(jnp.int32, (n, n // seg), 1)
    return jnp.where((r >= c * seg) & (r < (c + 1) * seg), 1.0, 0.0).astype(F32)


def _seg_indicator_t(n, seg):
    c = lax.broadcasted_iota(jnp.int32, (n // seg, n), 0)
    r = lax.broadcasted_iota(jnp.int32, (n // seg, n), 1)
    return jnp.where((r >= c * seg) & (r < (c + 1) * seg), 1.0, 0.0).astype(F32)


def _seg_sum(x, seg):
    return _mm_ind(x, _seg_indicator(x.shape[1], seg))


def _seg_expand(s, seg):
    return _mm_ind(s, _seg_indicator_t(s.shape[1] * seg, seg))


def _tile_lanes(g, reps):
    n = g.shape[1]
    r = lax.broadcasted_iota(jnp.int32, (n, n * reps), 0)
    c = lax.broadcasted_iota(jnp.int32, (n, n * reps), 1)
    return _mm_ind(g, jnp.where((c & (n - 1)) == r, 1.0, 0.0).astype(F32))


def _rms(x, g):
    return x * lax.rsqrt(jnp.mean(x * x, axis=-1, keepdims=True) + RMS_EPS) * g


MATMUL_VMEM_BUDGET = 40 * 1024 * 1024
MATMUL_MAX_TILE = 2048
HBM_BYTES_PER_S = 1.2e12
MXU_FLOPS_PER_S = 8e14
GRID_STEP_S = 0.35e-6


def _tile_candidates(dim):
    c = [d for d in range(LANES, min(dim, MATMUL_MAX_TILE) + 1, LANES) if dim % d == 0]
    return c or [dim]


def _matmul_tiles(m, n, k, a_bytes, b_bytes, out_bytes, res_bytes):
    best = None
    for tm in _tile_candidates(m):
        for tn in _tile_candidates(n):
            for tk in _tile_candidates(k):
                casts = (tm * tk * 2 if a_bytes > 2 else 0) + (tk * tn * 2 if b_bytes > 2 else 0)
                vmem = (2 * (tm * tk * a_bytes + tk * tn * b_bytes + tm * tn * (out_bytes + res_bytes))
                        + 2 * tm * tn * 4 + casts)
                if vmem > MATMUL_VMEM_BUDGET:
                    continue
                steps = (m // tm) * (n // tn) * (k // tk)
                a_reads = 1 if k == tk else n // tn
                traffic = (m * k * a_bytes * a_reads + k * n * b_bytes * (m // tm) + m * n * (out_bytes + res_bytes))
                dma_s = traffic / HBM_BYTES_PER_S
                ends_s = (tm * tk * a_bytes + tk * tn * b_bytes + tm * tn * out_bytes) / HBM_BYTES_PER_S
                cost = max(dma_s, 2.0 * m * n * k / MXU_FLOPS_PER_S) + 0.2 * dma_s + steps * GRID_STEP_S + ends_s
                if best is None or cost < best[0]:
                    best = (cost, tm, tn, tk)
    return best[1:]


def matmul(name, a, b, *, ta=False, tb=False, alpha=1.0, res=None, out_dtype=F32, hosted=None):
    m, k = (a.shape[1], a.shape[0]) if ta else a.shape
    n = b.shape[0] if tb else b.shape[1]
    assert k == (b.shape[1] if tb else b.shape[0]), (name, a.shape, b.shape)
    tm, tn, tk = _matmul_tiles(m, n, k, a.dtype.itemsize, b.dtype.itemsize, jnp.dtype(out_dtype).itemsize,
                               0 if res is None else res.dtype.itemsize)
    nk = k // tk
    grid = (m // tm, n // tn, nk)
    dims = ((0 if ta else 1,), (1 if tb else 0,))
    h_arrays, h_specs, h_shapes, h_scratch = _hosted_call_args(hosted)
    nhosted = len(h_arrays)
    n_in = 2 + (res is not None)

    def body(*refs):
        a_ref, b_ref = refs[:2]
        res_ref = refs[2] if res is not None else None
        h_in, o_ref, h_out = refs[n_in:n_in + nhosted], refs[n_in + nhosted], refs[n_in + nhosted + 1:n_in + 2 * nhosted + 1]
        acc_ref, sems = refs[n_in + 2 * nhosted + 1], refs[n_in + 2 * nhosted + 2:]
        _hosted_steps(hosted, h_in, h_out, sems, grid, 0.6)
        kk = pl.program_id(2)

        @pl.when(kk == 0)
        def _():
            acc_ref[...] = jnp.zeros_like(acc_ref)

        acc_ref[...] += lax.dot_general(a_ref[...].astype(BF16), b_ref[...].astype(BF16), (dims, ((), ())),
                                        preferred_element_type=F32)

        @pl.when(kk == nk - 1)
        def _():
            out = acc_ref[...] * alpha
            if res_ref is not None:
                out = out + res_ref[...].astype(F32)
            o_ref[...] = out.astype(o_ref.dtype)

    a_spec = pl.BlockSpec((tk, tm), lambda i, j, kk: (kk, i)) if ta else pl.BlockSpec((tm, tk), lambda i, j, kk: (i, kk))
    b_spec = pl.BlockSpec((tn, tk), lambda i, j, kk: (j, kk)) if tb else pl.BlockSpec((tk, tn), lambda i, j, kk: (kk, j))
    o_spec = pl.BlockSpec((tm, tn), lambda i, j, kk: (i, j))
    in_specs = [a_spec, b_spec] + ([o_spec] if res is not None else [])
    args = (a, b) + ((res,) if res is not None else ())
    out_shape = jax.ShapeDtypeStruct((m, n), out_dtype)
    if hosted is None:
        return pl.pallas_call(
            body, name=name, grid=grid, in_specs=in_specs, out_specs=o_spec, out_shape=out_shape,
            scratch_shapes=[pltpu.VMEM((tm, tn), F32)], compiler_params=_params(("parallel", "parallel", "arbitrary")),
        )(*args)
    out, *travelled = pl.pallas_call(
        body, name=name, grid=grid, in_specs=in_specs + h_specs, out_specs=[o_spec] + h_specs,
        out_shape=[out_shape] + h_shapes, scratch_shapes=[pltpu.VMEM((tm, tn), F32)] + h_scratch,
        compiler_params=_params(("arbitrary", "arbitrary", "arbitrary")),
    )(*args, *h_arrays)
    return out, travelled


def _row_specs(arrays, tr):
    return [pl.BlockSpec((tr, a.shape[1]), lambda i: (i, 0)) for a in arrays]


def _whole_specs(arrays):
    return [pl.BlockSpec(a.shape, lambda i: (0, 0)) for a in arrays]


def rowwise(name, fn, rows, consts, outs, tr):
    t = rows[0].shape[0]
    nr, nc = len(rows), len(consts)

    def body(*refs):
        vals = [r[...] for r in refs[:nr + nc]]
        res = fn(*vals)
        for o_ref, o in zip(refs[nr + nc:], res):
            o_ref[...] = o.astype(o_ref.dtype)

    out_shape = [jax.ShapeDtypeStruct((t, w), d) for w, d in outs]
    return pl.pallas_call(
        body, name=name, grid=(t // tr,), in_specs=_row_specs(rows, tr) + _whole_specs(consts),
        out_specs=_row_specs(out_shape, tr), out_shape=out_shape, compiler_params=_params(("parallel",)),
    )(*rows, *consts)


def rowwise_vjp(name, fn, rows, consts, cots, tr, row_grad, const_grad, add_to=None, grad_dtypes=None):
    t = rows[0].shape[0]
    nr, nc = len(rows), len(consts)
    cot_groups = [c if isinstance(c, (tuple, list)) else (c,) for c in cots]
    flat_cots = [a for g in cot_groups for a in g]
    add_to = {i: (a if isinstance(a, (tuple, list)) else (a,)) for i, a in (add_to or {}).items()}
    add_idx = [(i, q) for i in sorted(add_to) for q in range(len(add_to[i]))]
    add_arrays = [add_to[i][q] for i, q in add_idx]
    r_idx = [i for i in range(nr) if row_grad[i]]
    c_idx = [i for i in range(nc) if const_grad[i]]
    grad_dtypes = grad_dtypes or {}
    n_in = nr + nc + len(flat_cots) + len(add_arrays)

    def body(*refs):
        vals = [r[...] for r in refs[:nr + nc]]
        pos = nr + nc
        cts = []
        for g in cot_groups:
            s = refs[pos][...].astype(F32)
            for q in range(1, len(g)):
                s = s + refs[pos + q][...].astype(F32)
            cts.append(s)
            pos += len(g)
        adds = {}
        for n_add, (i, _) in enumerate(add_idx):
            term = refs[pos + n_add][...].astype(F32)
            adds[i] = adds[i] + term if i in adds else term
        diff =[vals[i] for i in r_idx] + [vals[nr + i] for i in c_idx]

        def f(*d):
            full = list(vals)
            for q, i in enumerate(r_idx):
                full[i] = d[q]
            for q, i in enumerate(c_idx):
                full[nr + i] = d[len(r_idx) + q]
            return tuple(fn(*full))

        prim, vjp = jax.vjp(f, *diff)
        grads = vjp(tuple(c.astype(p.dtype) for c, p in zip(cts, prim)))
        o_refs = refs[n_in:]
        for q, i in enumerate(r_idx):
            g = grads[q].astype(F32)
            if i in adds:
                g = g + adds[i].astype(F32)
            o_refs[q][...] = g.astype(o_refs[q].dtype)
        step = pl.program_id(0)
        for q, i in enumerate(c_idx):
            o_ref = o_refs[len(r_idx) + q]

            @pl.when(step == 0)
            def _(o_ref=o_ref):
                o_ref[...] = jnp.zeros_like(o_ref)

            o_ref[...] += grads[len(r_idx) + q].astype(F32)

    row_out = [jax.ShapeDtypeStruct(rows[i].shape, grad_dtypes.get(i, F32)) for i in r_idx]
    const_out = [jax.ShapeDtypeStruct(consts[i].shape, F32) for i in c_idx]
    outs = pl.pallas_call(
        body, name=name, grid=(t // tr,),
        in_specs=_row_specs(rows, tr) + _whole_specs(consts) + _row_specs(flat_cots, tr) + _row_specs(add_arrays, tr),
        out_specs=_row_specs(row_out, tr) + _whole_specs(const_out), out_shape=row_out + const_out,
        compiler_params=_params(("arbitrary",)),
    )(*rows, *consts, *flat_cots, *add_arrays)
    return list(outs)


def loss_head(y, target):
    t, d = y.shape
    tr = _pick(t, (256, 128, 64, 32, 16, 8))

    def body(y_ref, t_ref, dy_ref, l_ref):
        diff = y_ref[...] - t_ref[...]
        dy_ref[...] = diff * (1.0 / d)

        @pl.when(pl.program_id(0) == 0)
        def _():
            l_ref[...] = jnp.zeros_like(l_ref)

        l_ref[...] += 0.5 * jnp.sum(jnp.mean(diff * diff, axis=-1, keepdims=True), axis=0, keepdims=True)

    dy, l = pl.pallas_call(
        body, name="loss_head", grid=(t // tr,), in_specs=_row_specs([y, target], tr),
        out_specs=[pl.BlockSpec((tr, d), lambda i: (i, 0)), pl.BlockSpec((1, 1), lambda i: (0, 0))],
        out_shape=[jax.ShapeDtypeStruct((t, d), F32), jax.ShapeDtypeStruct((1, 1), F32)],
        compiler_params=_params(("arbitrary",)),
    )(y, target)
    return dy, l


ATT_PAD = N_LEFT_CHUNKS * CHUNK
MASKED = -1e30
ATT_GROUP = 2


def _attn_chunks(q_c, k_b, v_b, bias, valid):
    s = [jnp.where(ok, _dot1(q, k, NT) + b, MASKED) for q, k, b, ok in zip(q_c, k_b, bias, valid)]
    e = [jnp.exp(x - lax.stop_gradient(jnp.max(x, axis=-1, keepdims=True))) for x in s]
    p = [x / jnp.sum(x, axis=-1, keepdims=True) for x in e]
    return tuple(_dot1(x, v, NN) for x, v in zip(p, v_b))


def _band_valid(c):
    return (c * CHUNK + lax.broadcasted_iota(jnp.int32, (1, BAND), 1)) >= ATT_PAD


HEAD_PAIR = LANES // HEAD_DIM


def _split_heads(x, n):
    return [x[:, h * HEAD_DIM:(h + 1) * HEAD_DIM] for h in range(n)]


def _pair_spec(t):
    return pl.BlockSpec((t, HEAD_PAIR * HEAD_DIM), lambda p: (0, p))


def _attn_operands(q_ref, kp, vp, bias, cs, q0):
    qs = [x for s in q0 for x in _split_heads(q_ref[pl.ds(s, CHUNK), :], HEAD_PAIR)]
    ks = [x for s in q0 for x in _split_heads(kp[pl.ds(s, BAND), :], HEAD_PAIR)]
    vs = [x for s in q0 for x in _split_heads(vp[pl.ds(s, BAND), :], HEAD_PAIR)]
    return tuple(qs), tuple(ks), tuple(vs), tuple(bias) * len(cs), tuple(_band_valid(c) for c in cs for _ in bias)


def attention_fwd(q, k, v, bias, hosted=None):
    t, width = q.shape
    pairs = width // (HEAD_PAIR * HEAD_DIM)
    nchunks = t // CHUNK
    group = _pick(nchunks, (ATT_GROUP, 1))
    h_arrays, h_specs, h_shapes, h_scratch = _hosted_call_args(hosted)
    nhosted = len(h_arrays)

    def body(q_ref, k_ref, v_ref, b_ref, *rest):
        h_in, o_ref, h_out = rest[:nhosted], rest[nhosted], rest[nhosted + 1:2 * nhosted + 1]
        kp, vp, sems = rest[2 * nhosted + 1], rest[2 * nhosted + 2], rest[2 * nhosted + 3:]
        _hosted_steps(hosted, h_in, h_out, sems, (pairs,), 0.6)
        zeros = jnp.zeros((ATT_PAD, HEAD_PAIR * HEAD_DIM), F32)
        kp[pl.ds(0, ATT_PAD), :] = zeros
        vp[pl.ds(0, ATT_PAD), :] = zeros
        kp[pl.ds(ATT_PAD, t), :] = k_ref[...]
        vp[pl.ds(ATT_PAD, t), :] = v_ref[...]
        bias = [b_ref[h] for h in range(HEAD_PAIR)]

        def step(g, carry):
            cs = [g * group + u for u in range(group)]
            q0 = [pl.multiple_of(c * CHUNK, CHUNK) for c in cs]
            outs = _attn_chunks(*_attn_operands(q_ref, kp, vp, bias, cs, q0))
            for u, s in enumerate(q0):
                o_ref[pl.ds(s, CHUNK), :] = jnp.concatenate(outs[HEAD_PAIR * u:HEAD_PAIR * (u + 1)], axis=1)
            return carry

        lax.fori_loop(0, nchunks // group, step, 0)

    out, *travelled = pl.pallas_call(
        body, name="attention_fwd", grid=(pairs,),
        in_specs=[_pair_spec(t)] * 3 + [pl.BlockSpec((HEAD_PAIR, CHUNK, BAND), lambda p: (p, 0, 0))] + h_specs,
        out_specs=[_pair_spec(t)] + h_specs, out_shape=[jax.ShapeDtypeStruct((t, width), F32)] + h_shapes,
        scratch_shapes=[pltpu.VMEM((t + ATT_PAD, HEAD_PAIR * HEAD_DIM), F32)] * 2 + h_scratch,
        compiler_params=_params(("arbitrary",)),
    )(q, k, v, bias, *h_arrays)
    return out if hosted is None else (out, travelled)


def attention_bwd(q, k, v, bias, dout):
    t, width = q.shape
    pairs = width // (HEAD_PAIR * HEAD_DIM)
    nchunks = t // CHUNK
    group = _pick(nchunks, (ATT_GROUP, 1))

    def body(q_ref, k_ref, v_ref, b_ref, do_ref, dq_ref, dk_ref, dv_ref, db_ref, kp, vp, dkp, dvp):
        zeros = jnp.zeros((ATT_PAD, HEAD_PAIR * HEAD_DIM), F32)
        kp[pl.ds(0, ATT_PAD), :] = zeros
        vp[pl.ds(0, ATT_PAD), :] = zeros
        kp[pl.ds(ATT_PAD, t), :] = k_ref[...]
        vp[pl.ds(ATT_PAD, t), :] = v_ref[...]
        dkp[...] = jnp.zeros_like(dkp)
        dvp[...] = jnp.zeros_like(dvp)
        db_ref[...] = jnp.zeros_like(db_ref)
        bias = [b_ref[h] for h in range(HEAD_PAIR)]

        def step(g, carry):
            cs = [g * group + u for u in range(group)]
            q0 = [pl.multiple_of(c * CHUNK, CHUNK) for c in cs]
            qs, ks, vs, bs, valid = _attn_operands(q_ref, kp, vp, bias, cs, q0)
            _, vjp = jax.vjp(lambda a, b, cc, d: _attn_chunks(a, b, cc, d, valid), qs, ks, vs, bs)
            dos = tuple(x for s in q0 for x in _split_heads(do_ref[pl.ds(s, CHUNK), :], HEAD_PAIR))
            dq, dk, dv, db = vjp(dos)
            for u, s in enumerate(q0):
                mine = slice(HEAD_PAIR * u, HEAD_PAIR * (u + 1))
                dq_ref[pl.ds(s, CHUNK), :] = jnp.concatenate(dq[mine], axis=1)
                dkp[pl.ds(s, BAND), :] += jnp.concatenate(dk[mine], axis=1)
                dvp[pl.ds(s, BAND), :] += jnp.concatenate(dv[mine], axis=1)
            for h in range(HEAD_PAIR):
                total = db[h]
                for u in range(1, group):
                    total = total + db[HEAD_PAIR * u + h]
                db_ref[h] += total
            return carry

        lax.fori_loop(0, nchunks // group, step, 0)
        dk_ref[...] = dkp[pl.ds(ATT_PAD, t), :]
        dv_ref[...] = dvp[pl.ds(ATT_PAD, t), :]

    bias_spec = pl.BlockSpec((HEAD_PAIR, CHUNK, BAND), lambda p: (p, 0, 0))
    ts = jax.ShapeDtypeStruct((t, width), F32)
    return pl.pallas_call(
        body, name="attention_bwd", grid=(pairs,), in_specs=[_pair_spec(t)] * 3 + [bias_spec, _pair_spec(t)],
        out_specs=[_pair_spec(t)] * 3 + [bias_spec],
        out_shape=[ts, ts, ts, jax.ShapeDtypeStruct(bias.shape, F32)],
        scratch_shapes=[pltpu.VMEM((t + ATT_PAD, HEAD_PAIR * HEAD_DIM), F32)] * 4, compiler_params=_params(("parallel",)),
    )(q, k, v, bias, dout)


def _rel_onehot_t(i):
    j = lax.broadcasted_iota(jnp.int32, (N_REL, BAND), 1)
    r = lax.broadcasted_iota(jnp.int32, (N_REL, BAND), 0)
    idx = jnp.clip(i + ATT_PAD - j, -(CHUNK - 1), REL_CLIP) + (CHUNK - 1)
    return jnp.where(r == idx, 1.0, 0.0).astype(F32)


def bias_expand(rel):
    nh = rel.shape[0]

    def body(rel_ref, o_ref):
        o_ref[...] = _mm(rel_ref[...], _rel_onehot_t(pl.program_id(0)))

    return pl.pallas_call(
        body, name="bias_expand", grid=(CHUNK,), in_specs=[pl.BlockSpec((nh, N_REL), lambda i: (0, 0))],
        out_specs=pl.BlockSpec((None, nh, BAND), lambda i: (i, 0, 0)),
        out_shape=jax.ShapeDtypeStruct((CHUNK, nh, BAND), F32), compiler_params=_params(("parallel",)),
    )(rel)


def bias_reduce(dbias):
    nh = dbias.shape[1]

    def body(d_ref, o_ref):
        @pl.when(pl.program_id(0) == 0)
        def _():
            o_ref[...] = jnp.zeros_like(o_ref)

        o_ref[...] += _mm_nt(d_ref[...], _rel_onehot_t(pl.program_id(0)))

    return pl.pallas_call(
        body, name="bias_reduce", grid=(CHUNK,), in_specs=[pl.BlockSpec((None, nh, BAND), lambda i: (i, 0, 0))],
        out_specs=pl.BlockSpec((nh, N_REL), lambda i: (0, 0)), out_shape=jax.ShapeDtypeStruct((nh, N_REL), F32),
        compiler_params=_params(("arbitrary",)),
    )(dbias)


def _tri(n, strict):
    r = lax.broadcasted_iota(jnp.int32, (n, n), 0)
    c = lax.broadcasted_iota(jnp.int32, (n, n), 1)
    return (c < r) if strict else (c <= r)


def _each(f, *lists):
    return [f(*args) for args in zip(*lists)]


def _rwkv_chunk(s, r, k, v, w, a, b):
    n = r[0].shape[0]
    strict, incl = _tri(n, True), _tri(n, False)
    ones = jnp.where(incl, 1.0, 0.0).astype(F32)
    eye = jnp.where(incl & ~strict, 1.0, 0.0).astype(F32)
    lw = _each(jnp.log, w)
    cum = _each(lambda x: _mm(ones, x), lw)
    p_incl = _each(jnp.exp, cum)
    p_inv = _each(lambda x: jnp.exp(-x), cum)
    a_t = _each(lambda x, c, l: x * jnp.exp(c - l), a, cum, lw)
    r_t = _each(jnp.multiply, r, p_incl)
    b_t = _each(jnp.multiply, b, p_inv)
    k_t = _each(jnp.multiply, k, p_inv)
    a_ab = _each(lambda x, y: jnp.where(strict, _dot3(x, y, NT), 0.0), a_t, b_t)
    a_ak = _each(lambda x, y: jnp.where(strict, _dot3(x, y, NT), 0.0), a_t, k_t)
    r_b = _each(lambda x, y: jnp.where(incl, _dot3(x, y, NT), 0.0), r_t, b_t)
    r_k = _each(lambda x, y: jnp.where(incl, _dot3(x, y, NT), 0.0), r_t, k_t)
    a_s = _each(lambda x, y: _dot3(x, y, NT), a_t, s)
    r_s = _each(lambda x, y: _dot3(x, y, NT), r_t, s)
    a_kv = _each(lambda x, y: _dot3(x, y, NN), a_ak, v)
    total = _each(lambda x: eye + x, a_ab)
    power = _each(lambda x: _dot3(x, x, NN), a_ab)
    rounds = int(math.log2(n)) - 1
    for i in range(rounds):
        total = _each(lambda t, p: t + _dot3(t, p, NN), total, power)
        if i < rounds - 1:
            power = _each(lambda p: _dot3(p, p, NN), power)
    sa = _each(lambda t, x, y: _dot3(t, x + y, NN), total, a_s, a_kv)
    y = _each(lambda rs, rb, x, rk, vv: rs + _dot3(rb, x, NN) + _dot3(rk, vv, NN), r_s, r_b, sa, r_k, v)
    s_new = _each(lambda ss, x, bt, vv, kt, p: (ss + _dot3(x, bt, TN) + _dot3(vv, kt, TN)) * p[n - 1:n, :],
                  s, sa, b_t, v, k_t, p_incl)
    return tuple(s_new), tuple(y)


RWKV_TILE = 256
RWKV_HEADS = 8


def _rwkv_specs(nh, t, reverse):
    tile = min(RWKV_TILE, t)
    hb = RWKV_HEADS if nh % RWKV_HEADS == 0 else nh
    nt = t // tile
    per = tile // RWKV_CHUNK
    pos = (lambda h, i: (nt - 1 - i, h)) if reverse else (lambda h, i: (i, h))
    pos4 = (lambda h, i: (h, nt - 1 - i, 0, 0)) if reverse else (lambda h, i: (h, i, 0, 0))
    return (hb, nt, per, pl.BlockSpec((tile, hb * HEAD_DIM), pos), pl.BlockSpec((hb, per, HEAD_DIM, HEAD_DIM), pos4))


def _hosted_steps(exchange, refs_in, refs_out, sems, grid, forward_share):
    if exchange is None:
        return
    start, forward, finish = exchange.plan(refs_in, refs_out, *sems)
    step, n_steps = 0, 1
    for axis, size in enumerate(grid):
        step = step * size + pl.program_id(axis)
        n_steps *= size
    pl.when(step == 0)(start)
    pl.when(step == int(forward_share * (n_steps - 1)))(forward)
    pl.when(step == n_steps - 1)(finish)


def _hosted_call_args(exchange):
    if exchange is None:
        return [], [], [], []
    nb = len(exchange.arrays)
    return exchange.arrays, [HBM_SPEC] * nb, exchange.out_shape, exchange.scratch()


def rwkv_fwd(r, k, v, w, a, b, hosted=None):
    t, width = r.shape
    nh = width // HEAD_DIM
    hb, nt, per, row_spec, s_spec = _rwkv_specs(nh, t, False)
    h_arrays, h_specs, h_shapes, h_scratch = _hosted_call_args(hosted)
    nhosted = len(h_arrays)
    n_steps = (nh // hb) * nt

    def body(r_ref, k_ref, v_ref, w_ref, a_ref, b_ref, *rest):
        h_in, (y_ref, s_ref), h_out = rest[:nhosted], rest[nhosted:nhosted + 2], rest[nhosted + 2:2 * nhosted + 2]
        state, sems = rest[2 * nhosted + 2], rest[2 * nhosted + 3:]
        _hosted_steps(hosted, h_in, h_out, sems, (nh // hb, nt), 0.75)

        @pl.when(pl.program_id(1) == 0)
        def _():
            state[...] = jnp.zeros_like(state)

        def step(c, states):
            rows = pl.ds(pl.multiple_of(c * RWKV_CHUNK, RWKV_CHUNK), RWKV_CHUNK)
            for h in range(hb):
                s_ref[h, c] = states[h]
            s_new, y = _rwkv_chunk(states, *(tuple(_split_heads(ref[rows, :], hb))
                                             for ref in (r_ref, k_ref, v_ref, w_ref, a_ref, b_ref)))
            y_ref[rows, :] = jnp.concatenate(y, axis=1)
            return s_new

        final = lax.fori_loop(0, per, step, tuple(state[h] for h in range(hb)))
        for h in range(hb):
            state[h] = final[h]

    return pl.pallas_call(
        body, name="rwkv_fwd", grid=(nh // hb, nt), in_specs=[row_spec] * 6 + h_specs,
        out_specs=[row_spec, s_spec] + h_specs,
        out_shape=[jax.ShapeDtypeStruct((t, width), F32),
                   jax.ShapeDtypeStruct((nh, t // RWKV_CHUNK, HEAD_DIM, HEAD_DIM), F32)] + h_shapes,
        scratch_shapes=[pltpu.VMEM((hb, HEAD_DIM, HEAD_DIM), F32)] + h_scratch,
        compiler_params=_params(("arbitrary", "arbitrary")),
    )(r, k, v, w, a, b, *h_arrays)


def rwkv_bwd(r, k, v, w, a, b, states, dy, hosted=None):
    t, width = r.shape
    nh = width // HEAD_DIM
    hb, nt, per, row_spec, s_spec = _rwkv_specs(nh, t, True)
    h_arrays, h_specs, h_shapes, h_scratch = _hosted_call_args(hosted)
    nhosted = len(h_arrays)
    n_steps = (nh // hb) * nt

    def body(r_ref, k_ref, v_ref, w_ref, a_ref, b_ref, s_ref, dy_ref, *rest):
        h_in, d_refs, h_out = rest[:nhosted], rest[nhosted:nhosted + 6], rest[nhosted + 6:2 * nhosted + 6]
        dstate, sems = rest[2 * nhosted + 6], rest[2 * nhosted + 7:]
        _hosted_steps(hosted, h_in, h_out, sems, (nh // hb, nt), 0.5)

        @pl.when(pl.program_id(1) == 0)
        def _():
            dstate[...] = jnp.zeros_like(dstate)

        def step(i, ds):
            c = per - 1 - i
            rows = pl.ds(pl.multiple_of(c * RWKV_CHUNK, RWKV_CHUNK), RWKV_CHUNK)
            _, vjp = jax.vjp(_rwkv_chunk, tuple(s_ref[h, c] for h in range(hb)),
                             *(tuple(_split_heads(ref[rows, :], hb))
                               for ref in (r_ref, k_ref, v_ref, w_ref, a_ref, b_ref)))
            grads = vjp((ds, tuple(_split_heads(dy_ref[rows, :], hb))))
            for d_ref, g in zip(d_refs, grads[1:]):
                d_ref[rows, :] = jnp.concatenate(g, axis=1)
            return grads[0]

        final = lax.fori_loop(0, per, step, tuple(dstate[h] for h in range(hb)))
        for h in range(hb):
            dstate[h] = final[h]

    hs = jax.ShapeDtypeStruct((t, width), F32)
    return pl.pallas_call(
        body, name="rwkv_bwd", grid=(nh // hb, nt), in_specs=[row_spec] * 6 + [s_spec, row_spec] + h_specs,
        out_specs=[row_spec] * 6 + h_specs, out_shape=[hs] * 6 + h_shapes,
        scratch_shapes=[pltpu.VMEM((hb, HEAD_DIM, HEAD_DIM), F32)] + h_scratch,
        compiler_params=_params(("arbitrary", "arbitrary")),
    )(r, k, v, w, a, b, states, dy, *h_arrays)


SUBLANES = 8


def s5_scan(bu_re, bu_im, a_re, a_im, *, reverse=False, h_prev=None):
    t, n = bu_re.shape
    tt = _pick(t, (512, 256, 128, 64, 32, 16, 8))
    tc = _pick(n, (1024, 512, 256, 128))
    nt = t // tt
    with_da = h_prev is not None
    sign = -1.0 if reverse else 1.0

    def body(*refs):
        if with_da:
            br, bi, ar_ref, ai_ref, pr, pi, hr_ref, hi_ref, dar_ref, dai_ref, sr, si = refs
        else:
            br, bi, ar_ref, ai_ref, hr_ref, hi_ref, sr, si = refs
        ti = pl.program_id(1)

        @pl.when(ti == 0)
        def _():
            sr[...] = jnp.zeros_like(sr)
            si[...] = jnp.zeros_like(si)
            if with_da:
                dar_ref[...] = jnp.zeros_like(dar_ref)
                dai_ref[...] = jnp.zeros_like(dai_ref)

        ar = ar_ref[...]
        ai = ai_ref[...] * sign

        def group(gi, carry):
            hr, hi, dar, dai = carry
            g0 = pl.multiple_of((tt // SUBLANES - 1 - gi if reverse else gi) * SUBLANES, SUBLANES)
            rows = pl.ds(g0, SUBLANES)
            xr, xi = br[rows, :], bi[rows, :]
            if with_da:
                qr, qi = pr[rows, :], pi[rows, :]
            out_r, out_i = [None] * SUBLANES, [None] * SUBLANES
            for s in (range(SUBLANES - 1, -1, -1) if reverse else range(SUBLANES)):
                if with_da:
                    dar = dar + hr * qr[s:s + 1, :] + hi * qi[s:s + 1, :]
                    dai = dai + hi * qr[s:s + 1, :] - hr * qi[s:s + 1, :]
                hr, hi = ar * hr - ai * hi + xr[s:s + 1, :], ar * hi + ai * hr + xi[s:s + 1, :]
                out_r[s], out_i[s] = hr, hi
            hr_ref[rows, :] = jnp.concatenate(out_r, axis=0)
            hi_ref[rows, :] = jnp.concatenate(out_i, axis=0)
            return hr, hi, dar, dai

        zero = jnp.zeros((1, tc), F32)
        hr, hi, dar, dai = lax.fori_loop(0, tt // SUBLANES, group, (sr[...], si[...], zero, zero))
        sr[...] = hr
        si[...] = hi
        if with_da:
            dar_ref[...] += dar
            dai_ref[...] += dai

    tile = pl.BlockSpec((tt, tc), (lambda ci, ti: (nt - 1 - ti, ci)) if reverse else (lambda ci, ti: (ti, ci)))
    col = pl.BlockSpec((1, tc), lambda ci, ti: (0, ci))
    hs = jax.ShapeDtypeStruct((t, n), F32)
    cs = jax.ShapeDtypeStruct((1, n), F32)
    ins = [bu_re, bu_im, a_re, a_im] + (list(h_prev) if with_da else [])
    return pl.pallas_call(
        body, name="s5_scan_bwd" if reverse else "s5_scan_fwd", grid=(n // tc, nt),
        in_specs=[tile, tile, col, col] + ([tile, tile] if with_da else []),
        out_specs=[tile, tile] + ([col, col] if with_da else []), out_shape=[hs, hs] + ([cs, cs] if with_da else []),
        scratch_shapes=[pltpu.VMEM((1, tc), F32)] * 2, compiler_params=_params(("parallel", "arbitrary")),
    )(*ins)


N_CHIPS = 4
HBM_SPEC = pl.BlockSpec(memory_space=pl.ANY)


def _remote(src, dst, send_sem, recv_sem, device):
    return pltpu.make_async_remote_copy(src_ref=src, dst_ref=dst, send_sem=send_sem, recv_sem=recv_sem,
                                        device_id=device, device_id_type=pl.DeviceIdType.MESH)


class Exchange:
    def __init__(self, arrays, out_shape, n_sems, plan):
        self.arrays, self.out_shape, self.n_sems, self.plan = list(arrays), out_shape, n_sems, plan

    def scratch(self):
        nb = len(self.arrays)
        return [pltpu.SemaphoreType.DMA((nb, self.n_sems)), pltpu.SemaphoreType.DMA((nb, self.n_sems)),
                pltpu.SemaphoreType.DMA((nb,))]

    def run(self, name):
        nb = len(self.arrays)

        def body(*refs):
            for step in self.plan(refs[:nb], refs[nb:2 * nb], *refs[2 * nb:]):
                step()

        return pl.pallas_call(
            body, name=name, in_specs=[HBM_SPEC] * nb, out_specs=[HBM_SPEC] * nb, out_shape=self.out_shape,
            scratch_shapes=self.scratch(), compiler_params=pltpu.CompilerParams(has_side_effects=True),
        )(*self.arrays)


def gather_all(arrays):
    nb = len(arrays)

    def plan(ins, outs, send_sems, recv_sems, local_sems):
        x, y, c = (lax.axis_index(n) for n in MESH_AXES)
        me, sibling = 4 * x + 2 * y + c, 4 * x + 2 * y + 1 - c
        chips = [(1 - x, y), (x, 1 - y), (1 - x, 1 - y)]

        def copy(b, k, slot, to, src=None):
            block = outs[b].at[slot]
            return _remote(block if src is None else src, block, send_sems.at[b, k], recv_sems.at[b, k], to)

        def local():
            return [pltpu.make_async_copy(ins[b], outs[b].at[me], local_sems.at[b]) for b in range(nb)]

        def first():
            return [cp for b in range(nb) for cp in
                    [copy(b, 0, me, (x, y, 1 - c), src=ins[b])]
                    + [copy(b, 1 + j, me, (px, py, c), src=ins[b]) for j, (px, py) in enumerate(chips)]]

        def passed():
            return [copy(b, 4 + j, 4 * px + 2 * py + c, (x, y, 1 - c)) for j, (px, py) in enumerate(chips) for b in range(nb)]

        def start():
            for cp in local() + first():
                cp.start()

        def forward():
            for j, (px, py) in enumerate(chips):
                for b in range(nb):
                    copy(b, 1 + j, 4 * px + 2 * py + c, (px, py, c)).wait_recv()
                    copy(b, 4 + j, 4 * px + 2 * py + c, (x, y, 1 - c)).start()

        def finish():
            for b in range(nb):
                copy(b, 0, sibling, (x, y, 1 - c)).wait_recv()
                for j, (px, py) in enumerate(chips):
                    copy(b, 4 + j, 4 * px + 2 * py + 1 - c, (x, y, 1 - c)).wait_recv()
            for cp in first() + passed():
                cp.wait_send()
            for cp in local():
                cp.wait()

        return start, forward, finish

    out_shape = [jax.ShapeDtypeStruct((N_DEV,) + tuple(a.shape), a.dtype) for a in arrays]
    return Exchange(arrays, out_shape, N_DEV - 1, plan)


def _nothing():
    pass


def pair_exchange(arrays):
    nb = len(arrays)

    def plan(ins, outs, send_sems, recv_sems, local_sems):
        x, y, c = (lax.axis_index(n) for n in MESH_AXES)

        def copies():
            return [_remote(ins[b].at[2 * z + 1 - c], outs[b].at[z], send_sems.at[b, z], recv_sems.at[b, z], (x, y, 1 - c))
                    for b in range(nb) for z in range(N_CHIPS)]

        def start():
            for cp in copies():
                cp.start()

        def finish():
            for cp in copies():
                cp.wait_send()
            for cp in copies():
                cp.wait_recv()

        return start, _nothing, finish

    out_shape = [jax.ShapeDtypeStruct((N_CHIPS,) + tuple(a.shape[1:]), a.dtype) for a in arrays]
    return Exchange(arrays, out_shape, N_CHIPS, plan)


def pair_sum(name, pieces, from_sibling):
    _, r, c = pieces.shape
    tr = _row_tile(r, c * pieces.dtype.itemsize, 1 << 20)

    def body(p_ref, s_ref, o_ref):
        mine = p_ref[lax.axis_index("c")]
        o_ref[...] = (mine.astype(F32) + s_ref[...].astype(F32)).astype(o_ref.dtype)

    return pl.pallas_call(
        body, name=name, grid=(N_CHIPS, r // tr),
        in_specs=[pl.BlockSpec((None, 2, tr, c), lambda z, i: (z, 0, i, 0)), pl.BlockSpec((None, tr, c), lambda z, i: (z, i, 0))],
        out_specs=pl.BlockSpec((None, tr, c), lambda z, i: (z, i, 0)),
        out_shape=jax.ShapeDtypeStruct((N_CHIPS, r, c), pieces.dtype), compiler_params=_params(("parallel", "parallel")),
    )(pieces.reshape(N_CHIPS, 2, r, c), from_sibling)


def chip_exchange(arrays):
    nb = len(arrays)

    def plan(ins, outs, send_sems, recv_sems, local_sems):
        x, y, c = (lax.axis_index(n) for n in MESH_AXES)
        my_chip = 2 * x + y
        chips = [(1 - x, y), (x, 1 - y), (1 - x, 1 - y)]

        def local():
            return [pltpu.make_async_copy(ins[b].at[my_chip], outs[b].at[my_chip], local_sems.at[b]) for b in range(nb)]

        def copies():
            return [_remote(ins[b].at[2 * px + py], outs[b].at[my_chip], send_sems.at[b, j], recv_sems.at[b, j], (px, py, c))
                    for b in range(nb) for j, (px, py) in enumerate(chips)]

        def start():
            for cp in local() + copies():
                cp.start()

        def finish():
            for cp in copies():
                cp.wait_send()
            for cp in copies():
                cp.wait_recv()
            for cp in local():
                cp.wait()

        return start, _nothing, finish

    out_shape = [jax.ShapeDtypeStruct(a.shape, a.dtype) for a in arrays]
    return Exchange(arrays, out_shape, N_CHIPS - 1, plan)


def adam_update(name, pieces, w, m, v):
    r, c = w.shape
    n_pieces = pieces.shape[0]
    tr = _row_tile(r, 4 * c, 1 << 20)

    def body(p_ref, w_ref, m_ref, v_ref, g_ref, d_ref, mo_ref, vo_ref):
        g = p_ref[0].astype(F32)
        for j in range(1, n_pieces):
            g = g + p_ref[j].astype(F32)
        m_new = ADAM_B1 * m_ref[...] + (1.0 - ADAM_B1) * g
        v_new = ADAM_B2 * v_ref[...] + (1.0 - ADAM_B2) * (g * g)
        m_hat = m_new / (1.0 - ADAM_B1 ** ADAM_STEP)
        v_hat = v_new / (1.0 - ADAM_B2 ** ADAM_STEP)
        g_ref[...] = g
        d_ref[...] = -ADAM_LR * (m_hat / (jnp.sqrt(v_hat) + ADAM_EPS) + ADAM_WD * w_ref[...])
        mo_ref[...] = m_new
        vo_ref[...] = v_new

    row = pl.BlockSpec((tr, c), lambda i: (i, 0))
    out = jax.ShapeDtypeStruct((r, c), F32)
    return pl.pallas_call(
        body, name=name, grid=(r // tr,), in_specs=[pl.BlockSpec((n_pieces, tr, c), lambda i: (0, i, 0)), row, row, row],
        out_specs=[row] * 4, out_shape=[out] * 4, compiler_params=_params(("parallel",)),
    )(pieces, w, m, v)


PACK_WIDTH = 1024
PACK_ROWS = 64


def _pack(arrays, dtype, lead=0):
    parts = []
    for a in arrays:
        head = a.shape[:lead]
        f = a.reshape(head + (-1,)).astype(dtype)
        pad = (-f.shape[-1]) % PACK_WIDTH
        if pad:
            f = jnp.pad(f, [(0, 0)] * lead + [(0, pad)])
        parts.append(f.reshape(head + (-1, PACK_WIDTH)))
    out = jnp.concatenate(parts, axis=lead)
    pad = (-out.shape[lead]) % PACK_ROWS
    if pad:
        out = jnp.pad(out, [(0, 0)] * lead + [(0, pad), (0, 0)])
    return out


def _unpack(packed, shapes, lead=0):
    head = packed.shape[:lead]
    out, row = [], 0
    for s in shapes:
        n = int(np.prod(s))
        rows = -(-n // PACK_WIDTH)
        chunk = lax.slice_in_dim(packed, row, row + rows, axis=lead).reshape(head + (-1,))
        out.append(lax.slice_in_dim(chunk, 0, n, axis=lead).reshape(head + tuple(s)))
        row += rows
    return out


def _to_natural(stacked, kind):
    if kind == "col":
        m = jnp.moveaxis(stacked, 0, -2)
        return m.reshape(m.shape[:-2] + (m.shape[-2] * m.shape[-1],))
    m = jnp.moveaxis(stacked, 0, -3)
    return m.reshape(m.shape[:-3] + (m.shape[-3] * m.shape[-2], m.shape[-1]))


def _to_stacked(natural, kind):
    if kind == "col":
        m = natural.reshape(natural.shape[:-1] + (N_DEV, natural.shape[-1] // N_DEV))
        return jnp.moveaxis(m, -2, 0)
    m = natural.reshape(natural.shape[:-2] + (N_DEV, natural.shape[-2] // N_DEV, natural.shape[-1]))
    return jnp.moveaxis(m, -3, 0)


def _f_rms(h, g):
    return (_rms(h, g).astype(BF16),)


@jax.custom_vjp
def _swiglu(g, u):
    return g * _sigmoid(g) * u


def _swiglu_fwd(g, u):
    s = _sigmoid(g)
    return g * s * u, (g, u, s)


def _swiglu_bwd(saved, ct):
    g, u, s = saved
    return ct * u * (s * (1.0 + g * (1.0 - s))), ct * (g * s)


_swiglu.defvjp(_swiglu_fwd, _swiglu_bwd)


def _f_swiglu(g, u):
    return (_swiglu(g.astype(F32), u.astype(F32)).astype(BF16),)


def _f_ple(h, pre, pp):
    return (h + _sigmoid(pre) * pp,)


def _head_rms(x, g):
    ms = _seg_sum(x * x, HEAD_DIM) * (1.0 / HEAD_DIM)
    return x * _seg_expand(lax.rsqrt(ms + RMS_EPS), HEAD_DIM) * _tile_lanes(g, x.shape[1] // HEAD_DIM)


def _f_attpre(q, k, q_gain, k_gain):
    return _head_rms(q, q_gain) * (HEAD_DIM ** -0.5), _head_rms(k, k_gain)


def _f_shift(z, z_prev, mu):
    return (z + (z_prev - z) * mu,)


def _f_rwkvpre(r, k, v, xw, xa, xg, w0, w_up, a0, a_up, g_up, k_k, k_a):
    del r, v
    w_log = -_softplus(-(w0 + _mm(jnp.tanh(xw), w_up))) - 0.5
    decay = jnp.exp(-jnp.exp(w_log))
    a = _sigmoid(a0 + _mm(xa, a_up))
    g = _mm(_sigmoid(xg), g_up)
    kk = k * k_k
    norm = jnp.sqrt(_seg_expand(_seg_sum(kk * kk, HEAD_DIM), HEAD_DIM))
    kk = kk / jnp.maximum(norm, 1e-12)
    return k * (1.0 + (a - 1.0) * k_a), decay, -kk, kk * a, g


def _f_rwkvpost(y, r, k, v, g, lnx_w, lnx_b, r_k):
    mean = _seg_expand(_seg_sum(y, HEAD_DIM) * (1.0 / HEAD_DIM), HEAD_DIM)
    yc = y - mean
    var = _seg_expand(_seg_sum(yc * yc, HEAD_DIM) * (1.0 / HEAD_DIM), HEAD_DIM)
    yn = yc * lax.rsqrt(var + GN_EPS) * lnx_w + lnx_b
    bonus = _seg_expand(_seg_sum(r * k * r_k, HEAD_DIM), HEAD_DIM) * v
    return ((yn + bonus) * g,)


def _f_s5disc(lam_re, lam_im, log_dt):
    dt = jnp.exp(log_dt)
    mag = jnp.exp(lam_re * dt)
    ab_re, ab_im = mag * jnp.cos(lam_im * dt), mag * jnp.sin(lam_im * dt)
    denom = lam_re * lam_re + lam_im * lam_im
    z_re = ((ab_re - 1.0) * lam_re + ab_im * lam_im) / denom
    z_im = (ab_im * lam_re - (ab_re - 1.0) * lam_im) / denom
    return ab_re, ab_im, z_re, z_im


def _f_s5b(z_re, z_im, b_re, b_im):
    return z_re * b_re - z_im * b_im, z_re * b_im + z_im * b_re


def _f_s5post(ypre, u, d_skip):
    return (_gelu_tanh(ypre + d_skip * u).astype(BF16),)


def _f_glu(h, z1, z2):
    return (h + z1 * _sigmoid(z2),)


def _shift_down(x):
    return jnp.pad(x[:-1], ((1, 0), (0, 0)))


def _shift_up(x):
    return jnp.pad(x[1:], ((0, 1), (0, 0)))


def _block_diag(blocks):
    g, a, b = blocks.shape
    eye = jnp.eye(g, dtype=blocks.dtype)
    return (blocks[:, :, None, :] * eye[:, None, :, None]).reshape(g * a, g * b)


def _diag_blocks(dense, g):
    a, b = dense.shape[0] // g, dense.shape[1] // g
    return jnp.stack([dense[i * a:(i + 1) * a, i * b:(i + 1) * b] for i in range(g)], axis=0)


def _row_tile(t, width_bytes, budget=2 * 1024 * 1024):
    for tr in (512, 256, 128, 64, 32, 16, 8):
        if t % tr == 0 and tr * width_bytes <= budget:
            return tr
    return t


def _no_hosting(name):
    return None, None


def _hosting_matmul(hosting, name, a, b, **kw):
    exchange, done = hosting(name)
    if exchange is None:
        return matmul(name, a, b, **kw)
    out, travelled = matmul(name, a, b, hosted=exchange, **kw)
    done(travelled)
    return out


def _ffn_fwd(tag, h, norm_w, weight, hosting=_no_hosting):
    t, d = h.shape
    n = rowwise(f"{tag}_norm", _f_rms, [h], [norm_w], [(d, BF16)], _row_tile(t, 4 * d))[0]
    g = _hosting_matmul(hosting, f"{tag}_gate", n, weight("gate"), out_dtype=BF16)
    u = _hosting_matmul(hosting, f"{tag}_up", n, weight("up"), out_dtype=BF16)
    f = g.shape[1]
    a = rowwise(f"{tag}_act", _f_swiglu, [g, u], [], [(f, BF16)], _row_tile(t, 4 * f, 1 << 22))[0]
    return _hosting_matmul(hosting, f"{tag}_down", a, weight("down"), alpha=0.5, res=h), (h, n, g, u, a)


def _ffn_bwd(tag, dh2, saved, norm_w, w_gate, w_up, w_down):
    h, n, g, u, a = saved
    t, d = h.shape
    f = w_gate.shape[1]
    da = matmul(f"{tag}_da", dh2, w_down, tb=True, alpha=0.5, out_dtype=BF16)
    d_down = matmul(f"{tag}_dwdown", a, dh2, ta=True, alpha=0.5, out_dtype=BF16)
    dg, du = rowwise_vjp(f"{tag}_dact", _f_swiglu, [g, u], [], [da], _row_tile(t, 4 * f, 1 << 22), [True, True], [],
                         grad_dtypes={0: BF16, 1: BF16})
    dn = matmul(f"{tag}_dn_gate", dg, w_gate, tb=True)
    dn = matmul(f"{tag}_dn_up", du, w_up, tb=True, res=dn)
    d_gate = matmul(f"{tag}_dwgate", n, dg, ta=True, out_dtype=BF16)
    d_up = matmul(f"{tag}_dwup", n, du, ta=True, out_dtype=BF16)
    dh, d_norm = rowwise_vjp(f"{tag}_dnorm", _f_rms, [h], [norm_w], [dn], _row_tile(t, 4 * d), [True], [True],
                             add_to={0: dh2})
    return dh, d_norm, d_gate, d_up, d_down


def _ple_fwd(tag, h, norm_w, w_gate, w_proj, p_i, hosting=_no_hosting):
    t, d = h.shape
    n = rowwise(f"{tag}_norm", _f_rms, [h], [norm_w], [(d, BF16)], _row_tile(t, 4 * d))[0]
    pre = _hosting_matmul(hosting, f"{tag}_gate", n, w_gate)
    pp = matmul(f"{tag}_proj", p_i, w_proj)
    h2 = rowwise(f"{tag}_out", _f_ple, [h, pre, pp], [], [(d, F32)], _row_tile(t, 4 * d))[0]
    return h2, (h, n, pre, pp)


def _ple_bwd(tag, dh2, saved, norm_w, w_gate, w_proj, p_i):
    h, n, pre, pp = saved
    t, d = h.shape
    dpre, dpp = rowwise_vjp(f"{tag}_dout", _f_ple, [h, pre, pp], [], [dh2], _row_tile(t, 4 * d), [False, True, True], [],
                            grad_dtypes={1: BF16, 2: BF16})
    d_proj = matmul(f"{tag}_dwproj", p_i, dpp, ta=True, out_dtype=BF16)
    d_gate = matmul(f"{tag}_dwgate", n, dpre, ta=True, out_dtype=BF16)
    dn = matmul(f"{tag}_dn", dpre, w_gate, tb=True)
    dh, d_norm = rowwise_vjp(f"{tag}_dnorm", _f_rms, [h], [norm_w], [dn], _row_tile(t, 4 * d), [True], [True],
                             add_to={0: dh2})
    return dh, d_norm, d_gate, d_proj


def _ab_fwd(h, w, hosting=_no_hosting):
    t, d = h.shape
    da = d // 2
    n = rowwise("ab_norm", _f_rms, [h], [w["mix_norm"]], [(d, BF16)], _row_tile(t, 4 * d))[0]
    proj = _hosting_matmul(hosting, "ab_in", n, w["ab_w_in"])
    q_raw, k_raw, v_att, z = proj[:, :da], proj[:, da:2 * da], proj[:, 2 * da:3 * da], proj[:, 3 * da:]
    tr = _row_tile(t, 4 * da, 1 << 19)
    qn, kn = rowwise("att_pre", _f_attpre, [q_raw, k_raw], [w["att_q_gain"], w["att_k_gain"]], [(da, F32)] * 2, tr)
    bias = jnp.transpose(bias_expand(w["att_rel_bias"]), (1, 0, 2))
    qh, kh, vh = qn, kn, v_att
    exchange, done = hosting("attention_fwd")
    att = attention_fwd(qh, kh, vh, bias, hosted=exchange)
    if exchange is not None:
        att, travelled = att
        done(travelled)
    z_prev = _shift_down(z)
    zz = rowwise("rwkv_shift", _f_shift, [z, z_prev], [w["rwkv_mu"]], [(z.shape[1], F32)], _row_tile(t, 4 * z.shape[1]))[0]
    o = [0, da, 2 * da, 3 * da, 3 * da + DECAY_LORA, 3 * da + DECAY_LORA + AAA_LORA, z.shape[1]]
    r, k, v, xw, xa, xg = (zz[:, o[i]:o[i + 1]] for i in range(6))
    pre_consts = [w[nm] for nm in ("rwkv_w0", "rwkv_w_up", "rwkv_a0", "rwkv_a_up", "rwkv_g_up", "rwkv_k_k", "rwkv_k_a")]
    k2, decay, ia, ib, g = rowwise("rwkv_pre", _f_rwkvpre, [r, k, v, xw, xa, xg], pre_consts, [(da, F32)] * 5, tr)
    heads = [r, k2, v, decay, ia, ib]
    exchange, done = hosting("rwkv_fwd")
    y, states, *travelled = rwkv_fwd(*heads, hosted=exchange)
    if exchange is not None:
        done(travelled)
    post_consts = [w["rwkv_lnx_w"], w["rwkv_lnx_b"], w["rwkv_r_k"]]
    rw = rowwise("rwkv_post", _f_rwkvpost, [y, r, k2, v, g], post_consts, [(da, F32)], tr)[0]
    cat = jnp.concatenate([att, rw], axis=1).astype(BF16)
    h2 = _hosting_matmul(hosting, "ab_out", cat, w["ab_w_out"], res=h)
    saved = dict(h=h, n=n, q_raw=q_raw, k_raw=k_raw, qh=qh, kh=kh, vh=vh, bias=bias, z=z, z_prev=z_prev,
                 rows=(r, k, v, xw, xa, xg), pre_consts=pre_consts, post=(y, r, k2, v, g), post_consts=post_consts,
                 heads=heads, states=states, cat=cat, tr=tr)
    return h2, saved


def _ab_bwd(dh2, s, w, make_hosted=None, hosted_done=None):
    h = s["h"]
    t, d = h.shape
    da = d // 2
    tr = s["tr"]
    grads = {}
    dcat = matmul("ab_dcat", dh2, w["ab_w_out"], tb=True)
    grads["ab_w_out"] = matmul("ab_dwout", s["cat"], dh2, ta=True, out_dtype=BF16)
    d_att, d_rw = dcat[:, :da], dcat[:, da:]
    dy, dr1, dk1, dv1, dg, grads["rwkv_lnx_w"], grads["rwkv_lnx_b"], grads["rwkv_r_k"] = rowwise_vjp(
        "rwkv_dpost", _f_rwkvpost, list(s["post"]), s["post_consts"], [d_rw], tr, [True] * 5, [True] * 3)
    hosted = None if make_hosted is None else make_hosted(grads["ab_w_out"])
    *d_heads, = rwkv_bwd(*s["heads"], s["states"], dy, hosted=hosted)
    if hosted is not None:
        hosted_done(d_heads[6:])
        d_heads = d_heads[:6]
    dr2, dk2, dv2, ddecay, dia, dib = d_heads
    pre = rowwise_vjp("rwkv_dpre", _f_rwkvpre, list(s["rows"]), s["pre_consts"], [(dk1, dk2), ddecay, dia, dib, dg], tr,
                      [True] * 6, [True] * 7, add_to={0: (dr1, dr2), 2: (dv1, dv2)})
    for nm, g in zip(("rwkv_w0", "rwkv_w_up", "rwkv_a0", "rwkv_a_up", "rwkv_g_up", "rwkv_k_k", "rwkv_k_a"), pre[6:]):
        grads[nm] = g
    dzz = jnp.concatenate(pre[:6], axis=1)
    trz = _row_tile(t, 4 * dzz.shape[1])
    grads["rwkv_mu"] = rowwise_vjp("rwkv_dmu", _f_shift, [s["z"], s["z_prev"]], [w["rwkv_mu"]], [dzz], trz,
                                   [False, False], [True])[0]
    dz = rowwise("rwkv_dshift", _f_shift, [dzz, _shift_up(dzz)], [w["rwkv_mu"]], [(dzz.shape[1], F32)], trz)[0]
    dqh, dkh, dvh, dbias = attention_bwd(s["qh"], s["kh"], s["vh"], s["bias"], d_att)
    grads["att_rel_bias"] = bias_reduce(jnp.transpose(dbias, (1, 0, 2)))
    dq_raw, dk_raw, grads["att_q_gain"], grads["att_k_gain"] = rowwise_vjp(
        "att_dpre", _f_attpre, [s["q_raw"], s["k_raw"]], [w["att_q_gain"], w["att_k_gain"]],
        [dqh, dkh], tr, [True, True], [True, True])
    dproj = jnp.concatenate([dq_raw, dk_raw, dvh, dz], axis=1).astype(BF16)
    dn = matmul("ab_dn", dproj, w["ab_w_in"], tb=True)
    grads["ab_w_in"] = matmul("ab_dwin", s["n"], dproj, ta=True, out_dtype=BF16)
    dh, grads["mix_norm"] = rowwise_vjp("ab_dnorm", _f_rms, [h], [w["mix_norm"]], [dn], _row_tile(t, 4 * d), [True], [True],
                                        add_to={0: dh2})
    return dh, grads


def _s5_fwd(h, w):
    t, d = h.shape
    n_groups, n_state = w["ssm_lambda_re"].shape
    gp = n_groups * n_state
    n = rowwise("s5_norm", _f_rms, [h], [w["mix_norm"]], [(d, BF16)], _row_tile(t, 4 * d))[0]
    u = matmul("s5_in", n, w["ssm_w_in"])
    disc_rows = [w["ssm_lambda_re"], w["ssm_lambda_im"], w["ssm_log_dt"]]
    ab_re, ab_im, z_re, z_im = rowwise("s5_disc", _f_s5disc, disc_rows, [], [(n_state, F32)] * 4, n_groups)
    b_rows = [z_re.reshape(gp, 1), z_im.reshape(gp, 1), w["ssm_b_re"], w["ssm_b_im"]]
    trb = _row_tile(gp, 512)
    bb_re, bb_im = rowwise("s5_bbar", _f_s5b, b_rows, [], [(SSM_GROUP, F32)] * 2, trb)
    to_dense = lambda bb: _block_diag(jnp.transpose(bb.reshape(n_groups, n_state, SSM_GROUP), (0, 2, 1)))
    bd_re, bd_im = to_dense(bb_re), to_dense(bb_im)
    cd_re = _block_diag(jnp.transpose(w["ssm_c_re"], (0, 2, 1)))
    cd_im = -_block_diag(jnp.transpose(w["ssm_c_im"], (0, 2, 1)))
    a_re, a_im = ab_re.reshape(1, gp), ab_im.reshape(1, gp)
    bu_re = matmul("s5_bu_re", u, bd_re)
    bu_im = matmul("s5_bu_im", u, bd_im)
    h_re, h_im = s5_scan(bu_re, bu_im, a_re, a_im)
    ypre = matmul("s5_y_re", h_re, cd_re)
    ypre = matmul("s5_y_im", h_im, cd_im, res=ypre)
    tru = _row_tile(t, 4 * u.shape[1])
    yg = rowwise("s5_post", _f_s5post, [ypre, u], [w["ssm_d"]], [(u.shape[1], BF16)], tru)[0]
    w_out1, w_out2 = w["ssm_w_out"][:, :d], w["ssm_w_out"][:, d:]
    z1 = matmul("s5_out1", yg, w_out1)
    z2 = matmul("s5_out2", yg, w_out2)
    h2 = rowwise("s5_glu", _f_glu, [h, z1, z2], [], [(d, F32)], _row_tile(t, 4 * d))[0]
    saved = dict(h=h, n=n, u=u, disc_rows=disc_rows, b_rows=b_rows, trb=trb, bd=(bd_re, bd_im), cd=(cd_re, cd_im),
                 a=(a_re, a_im), hs=(h_re, h_im), ypre=ypre, yg=yg, w_out=(w_out1, w_out2), z=(z1, z2), tru=tru)
    return h2, saved


def _s5_bwd(dh2, s, w):
    h, u = s["h"], s["u"]
    t, d = h.shape
    n_groups, n_state = w["ssm_lambda_re"].shape
    gp = n_groups * n_state
    grads = {}
    z1, z2 = s["z"]
    w_out1, w_out2 = s["w_out"]
    dz1, dz2 = rowwise_vjp("s5_dglu", _f_glu, [h, z1, z2], [], [dh2], _row_tile(t, 4 * d), [False, True, True], [],
                           grad_dtypes={1: BF16, 2: BF16})
    dyg = matmul("s5_dyg1", dz1, w_out1, tb=True)
    dyg = matmul("s5_dyg2", dz2, w_out2, tb=True, res=dyg)
    grads["ssm_w_out"] = jnp.concatenate([matmul("s5_dwout1", s["yg"], dz1, ta=True, out_dtype=BF16),
                                          matmul("s5_dwout2", s["yg"], dz2, ta=True, out_dtype=BF16)], axis=1)
    dypre, du1, grads["ssm_d"] = rowwise_vjp("s5_dpost", _f_s5post, [s["ypre"], u], [w["ssm_d"]], [dyg], s["tru"],
                                             [True, True], [True])
    cd_re, cd_im = s["cd"]
    h_re, h_im = s["hs"]
    dh_re = matmul("s5_dh_re", dypre, cd_re, tb=True)
    dh_im = matmul("s5_dh_im", dypre, cd_im, tb=True)
    dcd_re = matmul("s5_dc_re", h_re, dypre, ta=True)
    dcd_im = matmul("s5_dc_im", h_im, dypre, ta=True)
    a_re, a_im = s["a"]
    g_re, g_im, da_re, da_im = s5_scan(dh_re, dh_im, a_re, a_im, reverse=True, h_prev=(h_re, h_im))
    bd_re, bd_im = s["bd"]
    du = matmul("s5_du_re", g_re, bd_re, tb=True, res=du1)
    du = matmul("s5_du_im", g_im, bd_im, tb=True, res=du)
    dbd_re = matmul("s5_db_re", u, g_re, ta=True)
    dbd_im = matmul("s5_db_im", u, g_im, ta=True)
    grads["ssm_w_in"] = matmul("s5_dwin", s["n"], du, ta=True, out_dtype=BF16)
    dn = matmul("s5_dn", du, w["ssm_w_in"], tb=True)
    dh, grads["mix_norm"] = rowwise_vjp("s5_dnorm", _f_rms, [h], [w["mix_norm"]], [dn], _row_tile(t, 4 * d), [True], [True],
                                        add_to={0: dh2})
    from_dense = lambda m: jnp.transpose(_diag_blocks(m, n_groups), (0, 2, 1)).reshape(gp, SSM_GROUP)
    dz_re, dz_im, grads["ssm_b_re"], grads["ssm_b_im"] = rowwise_vjp(
        "s5_dbbar", _f_s5b, s["b_rows"], [], [from_dense(dbd_re), from_dense(dbd_im)], s["trb"], [True] * 4, [])
    disc_cots = [da_re.reshape(n_groups, n_state), da_im.reshape(n_groups, n_state),
                 dz_re.reshape(n_groups, n_state), dz_im.reshape(n_groups, n_state)]
    grads["ssm_lambda_re"], grads["ssm_lambda_im"], grads["ssm_log_dt"] = rowwise_vjp(
        "s5_ddisc", _f_s5disc, s["disc_rows"], [], disc_cots, n_groups, [True] * 3, [])
    grads["ssm_c_re"] = jnp.transpose(_diag_blocks(dcd_re, n_groups), (0, 2, 1))
    grads["ssm_c_im"] = -jnp.transpose(_diag_blocks(dcd_im, n_groups), (0, 2, 1))
    return dh, grads


WEIGHTS = ["ffn1_norm", "ffn1_w_gate", "ffn1_w_up", "ffn1_w_down", "mix_norm", "ffn2_norm", "ffn2_w_gate", "ffn2_w_up",
           "ffn2_w_down", "ple_norm", "ple_w_gate", "ple_w_proj", "ab_w_in", "att_q_gain", "att_k_gain", "att_rel_bias",
           "rwkv_mu", "rwkv_w0", "rwkv_w_up", "rwkv_a0", "rwkv_a_up", "rwkv_g_up", "rwkv_k_k", "rwkv_k_a", "rwkv_r_k",
           "rwkv_lnx_w", "rwkv_lnx_b", "ab_w_out", "ssm_w_in", "ssm_lambda_re", "ssm_lambda_im", "ssm_log_dt", "ssm_b_re",
           "ssm_b_im", "ssm_c_re", "ssm_c_im", "ssm_d", "ssm_w_out"]
BIG = {"ffn1_w_gate": "col", "ffn1_w_up": "col", "ffn1_w_down": "row", "ffn2_w_gate": "col", "ffn2_w_up": "col",
       "ffn2_w_down": "row", "ple_w_gate": "row", "ple_w_proj": "col", "ab_w_in": "col", "ab_w_out": "row",
       "ssm_w_in": "row", "ssm_w_out": "col"}
SMALL_CUT = {"rwkv_w_up": "col", "rwkv_a_up": "col", "rwkv_g_up": "col", "ssm_d": "col"}
REPLICATED = [n for n in WEIGHTS if n not in BIG and n not in SMALL_CUT]


def kernel(x, p, ffn1_norm, ffn1_w_gate, ffn1_w_up, ffn1_w_down, mix_norm, ffn2_norm, ffn2_w_gate, ffn2_w_up, ffn2_w_down, ple_norm, ple_w_gate, ple_w_proj, ab_w_in, att_q_gain, att_k_gain, att_rel_bias, rwkv_mu, rwkv_w0, rwkv_w_up, rwkv_a0, rwkv_a_up, rwkv_g_up, rwkv_k_k, rwkv_k_a, rwkv_r_k, rwkv_lnx_w, rwkv_lnx_b, ab_w_out, ssm_w_in, ssm_lambda_re, ssm_lambda_im, ssm_log_dt, ssm_b_re, ssm_b_im, ssm_c_re, ssm_c_im, ssm_d, ssm_w_out, loss_target, m_ffn1_norm, m_ffn1_w_gate, m_ffn1_w_up, m_ffn1_w_down, m_mix_norm, m_ffn2_norm, m_ffn2_w_gate, m_ffn2_w_up, m_ffn2_w_down, m_ple_norm, m_ple_w_gate, m_ple_w_proj, m_ab_w_in, m_att_q_gain, m_att_k_gain, m_att_rel_bias, m_rwkv_mu, m_rwkv_w0, m_rwkv_w_up, m_rwkv_a0, m_rwkv_a_up, m_rwkv_g_up, m_rwkv_k_k, m_rwkv_k_a, m_rwkv_r_k, m_rwkv_lnx_w, m_rwkv_lnx_b, m_ab_w_out, m_ssm_w_in, m_ssm_lambda_re, m_ssm_lambda_im, m_ssm_log_dt, m_ssm_b_re, m_ssm_b_im, m_ssm_c_re, m_ssm_c_im, m_ssm_d, m_ssm_w_out, v_ffn1_norm, v_ffn1_w_gate, v_ffn1_w_up, v_ffn1_w_down, v_mix_norm, v_ffn2_norm, v_ffn2_w_gate, v_ffn2_w_up, v_ffn2_w_down, v_ple_norm, v_ple_w_gate, v_ple_w_proj, v_ab_w_in, v_att_q_gain, v_att_k_gain, v_att_rel_bias, v_rwkv_mu, v_rwkv_w0, v_rwkv_w_up, v_rwkv_a0, v_rwkv_a_up, v_rwkv_g_up, v_rwkv_k_k, v_rwkv_k_a, v_rwkv_r_k, v_rwkv_lnx_w, v_rwkv_lnx_b, v_ab_w_out, v_ssm_w_in, v_ssm_lambda_re, v_ssm_lambda_im, v_ssm_log_dt, v_ssm_b_re, v_ssm_b_im, v_ssm_c_re, v_ssm_c_im, v_ssm_d, v_ssm_w_out):
    vals = dict(locals())
    depth = ffn1_norm.shape[0]
    n_groups, n_state = ssm_lambda_re.shape[1:]

    kinds = {**BIG, **SMALL_CUT}
    travel = {n: (BF16 if n in BIG else F32) for n in kinds}

    def rows(a, lead=0):
        return a.reshape(a.shape[:lead] + (-1, a.shape[-1]))

    def model_layer(n, j):
        if vals[n].shape[0] == depth:
            return j
        return 2 * j if n.startswith(("ab_", "rwkv_")) else 2 * j + 1

    units = [(n, j) for n in kinds for j in range(vals[n].shape[0])]
    early_names = ("ffn1_w_gate", "ffn1_w_up", "ffn1_w_down", "ab_w_in", "rwkv_w_up", "rwkv_a_up", "rwkv_g_up")
    early = [u for u in units if model_layer(*u) == 0 and u[0] in early_names]
    late = [u for u in units if u not in early]
    full = {}

    def shard(unit):
        n, j = unit
        return rows(vals[n][j]).astype(travel[n])

    def take_gathered(which, got):
        for (n, j), g in zip(which, got):
            full[(n, j)] = _to_natural(g.reshape((N_DEV,) + vals[n].shape[1:]), kinds[n])

    first = [("ffn1_w_gate", 0), ("ffn1_w_up", 0)]
    beside = {"l0_ffn1_gate": [("ffn1_w_down", 0)],
              "l0_ffn1_up": [("ab_w_in", 0), ("rwkv_w_up", 0), ("rwkv_a_up", 0), ("rwkv_g_up", 0)],
              "l0_ffn1_down": [("ab_w_out", 0), ("ffn2_w_gate", 0)],
              "ab_in": [("ffn2_w_up", 0)],
              "attention_fwd": [("ffn2_w_down", 0), ("ple_w_gate", 0), ("ple_w_proj", 0)],
              "l0_ffn2_gate": [("ffn2_w_gate", 1)], "l0_ffn2_up": [("ffn2_w_up", 1)], "l0_ffn2_down": [("ffn2_w_down", 1)],
              "l0_ple_gate": [("ple_w_gate", 1), ("ple_w_proj", 1)]}
    planned = first + [u for us in beside.values() for u in us]
    beside["rwkv_fwd"] = [u for u in units if u not in planned]
    assert depth == 2 and sorted(planned + beside["rwkv_fwd"]) == sorted(units)
    take_gathered(first, gather_all([shard(u) for u in first]).run("gather_first"))

    def hosting(name):
        which = beside.get(name)
        if not which:
            return None, None
        return gather_all([shard(u) for u in which]), functools.partial(take_gathered, which)

    def row(name, j):
        return vals[name][j].reshape(1, -1)

    def ffn_weights(which, i):
        return (row(f"{which}_norm", i), full[(f"{which}_w_gate", i)], full[(f"{which}_w_up", i)],
                full[(f"{which}_w_down", i)])

    def mixer_weights(i):
        j = i // 2
        if i % 2 == 0:
            w = {n: row(n, j) for n in ("att_q_gain", "att_k_gain", "rwkv_mu", "rwkv_w0", "rwkv_a0", "rwkv_k_k", "rwkv_k_a",
                                        "rwkv_r_k", "rwkv_lnx_w", "rwkv_lnx_b")}
            w.update({n: full.get((n, j)) for n in ("ab_w_in", "ab_w_out", "rwkv_w_up", "rwkv_a_up", "rwkv_g_up")})
            w["att_rel_bias"] = att_rel_bias[j]
        else:
            w = {"ssm_lambda_re": ssm_lambda_re[j], "ssm_lambda_im": ssm_lambda_im[j],
                 "ssm_log_dt": ssm_log_dt[j].reshape(n_groups, 1),
                 "ssm_b_re": ssm_b_re[j].reshape(n_groups * n_state, -1),
                 "ssm_b_im": ssm_b_im[j].reshape(n_groups * n_state, -1),
                 "ssm_c_re": ssm_c_re[j], "ssm_c_im": ssm_c_im[j], "ssm_d": full[("ssm_d", j)].reshape(1, -1),
                 "ssm_w_in": full[("ssm_w_in", j)], "ssm_w_out": full[("ssm_w_out", j)]}
        w["mix_norm"] = row("mix_norm", i)
        return w

    def ple_weights(i):
        return (row("ple_norm", i), full[("ple_w_gate", i)], full[("ple_w_proj", i)], p[i, 0])

    h = x[0]
    saved = []
    for i in range(depth):
        h, s1 = _ffn_fwd(f"l{i}_ffn1", h, row("ffn1_norm", i), lambda kind, i=i: full[(f"ffn1_w_{kind}", i)], hosting)
        h, sm = _ab_fwd(h, mixer_weights(i), hosting) if i % 2 == 0 else _s5_fwd(h, mixer_weights(i))
        h, s2 = _ffn_fwd(f"l{i}_ffn2", h, row("ffn2_norm", i), lambda kind, i=i: full[(f"ffn2_w_{kind}", i)], hosting)
        h, sp = _ple_fwd(f"l{i}_ple", h, *ple_weights(i), hosting)
        saved.append((s1, sm, s2, sp))
    dh, loss_part = loss_head(h, loss_target[0])

    per_layer = {n: [] for n in WEIGHTS}
    received = {}

    def chip_sums(which, tag):
        pieces = [rows(_to_stacked(per_layer[n][j - vals[n].shape[0]], kinds[n]).astype(travel[n]), lead=1)
                  for n, j in which]
        from_sibling = pair_exchange(pieces).run(f"reduce_pair_{tag}")
        return [pair_sum(f"pair_sum_{n}{j}", a, b) for (n, j), a, b in zip(which, pieces, from_sibling)]

    def late_reduce(d_ab_w_out):
        per_layer["ab_w_out"].insert(0, d_ab_w_out)
        return chip_exchange(chip_sums(late, "late"))

    def late_reduce_done(got):
        received.update(zip(late, got))

    for i in reversed(range(depth)):
        s1, sm, s2, sp = saved[i]
        dh, d_norm, d_gate, d_proj = _ple_bwd(f"l{i}_ple", dh, sp, *ple_weights(i))
        for n, g in (("ple_norm", d_norm), ("ple_w_gate", d_gate), ("ple_w_proj", d_proj)):
            per_layer[n].insert(0, g)
        dh, d_norm, d_gate, d_up, d_down = _ffn_bwd(f"l{i}_ffn2", dh, s2, *ffn_weights("ffn2", i))
        for n, g in (("ffn2_norm", d_norm), ("ffn2_w_gate", d_gate), ("ffn2_w_up", d_up), ("ffn2_w_down", d_down)):
            per_layer[n].insert(0, g)
        if i == 0:
            dh, mixer_grads = _ab_bwd(dh, sm, mixer_weights(i), late_reduce, late_reduce_done)
        else:
            dh, mixer_grads = (_ab_bwd if i % 2 == 0 else _s5_bwd)(dh, sm, mixer_weights(i))
        for n, g in mixer_grads.items():
            if not (i == 0 and n == "ab_w_out"):
                per_layer[n].insert(0, g)
        dh, d_norm, d_gate, d_up, d_down = _ffn_bwd(f"l{i}_ffn1", dh, s1, *ffn_weights("ffn1", i))
        for n, g in (("ffn1_norm", d_norm), ("ffn1_w_gate", d_gate), ("ffn1_w_up", d_up), ("ffn1_w_down", d_down)):
            per_layer[n].insert(0, g)
    grad_x = dh[None]

    received.update(zip(early, chip_exchange(chip_sums(early, "early")).run("reduce_chips_early")))

    rep_mine = _pack([jnp.stack(per_layer[n], axis=0).reshape(vals[n].shape) for n in REPLICATED] + [loss_part], F32)
    rep_all = gather_all([rep_mine]).run("gather_replicated")[0]

    out = {}
    for n in kinds:
        got = [received[(n, j)] for j in range(vals[n].shape[0])]
        got = got[0] if len(got) == 1 else jnp.concatenate(got, axis=1)
        state = [rows(vals[pre + n]) for pre in ("", "m_", "v_")]
        out[n] = tuple(r.reshape(vals[n].shape) for r in adam_update(f"adam_{n}", got, *state))
    zero = jnp.zeros((1, 1), F32)
    state = [_pack([vals[pre + n] for n in REPLICATED] + [zero], F32) for pre in ("", "m_", "v_")]
    shapes = [vals[n].shape for n in REPLICATED] + [zero.shape]
    results = [_unpack(r, shapes) for r in adam_update("adam_replicated", rep_all, *state)]
    for q, n in enumerate(REPLICATED):
        out[n] = tuple(r[q] for r in results)
    loss = results[0][-1].reshape(())
    return (loss, grad_x, *[out[n][0] for n in WEIGHTS], *[out[n][1] for n in WEIGHTS], *[out[n][2] for n in WEIGHTS],
            *[out[n][3] for n in WEIGHTS])
```

```python
import functools
import math

import jax
import jax.numpy as jnp
import numpy as np
from jax import lax
from jax.experimental import pallas as pl
from jax.experimental.pallas import tpu as pltpu

F32 = jnp.float32
BF16 = jnp.bfloat16
HIGHEST = lax.Precision.HIGHEST
MESH_AXES = ("x", "y", "c")
N_DEV = 8

CHUNK = 64
N_LEFT_CHUNKS = 8
BAND = (N_LEFT_CHUNKS + 1) * CHUNK
HEAD_DIM = 64
REL_CLIP = 128
N_REL = (CHUNK - 1) + REL_CLIP + 1
DECAY_LORA = 64
AAA_LORA = 64
GATE_LORA = 128
SSM_GROUP = 16
SSM_STATE = 64
RMS_EPS = 1e-6
GN_EPS = 64e-5
ADAM_LR = 0.001
ADAM_B1 = 0.9
ADAM_B2 = 0.999
ADAM_EPS = 1e-08
ADAM_WD = 0.01
ADAM_STEP = 10

RWKV_CHUNK = 64
VMEM_LIMIT = 56 * 1024 * 1024
LANES = 128


def _params(semantics):
    return pltpu.CompilerParams(dimension_semantics=semantics, vmem_limit_bytes=VMEM_LIMIT)


def _pick(n, prefs):
    for t in prefs:
        if n % t == 0:
            return t
    return n


def _dot(a, b, dims):
    return lax.dot_general(a, b, (dims, ((), ())), precision=HIGHEST, preferred_element_type=F32)


def _mm(a, b):
    return _dot(a, b, ((1,), (0,)))


def _mm_nt(a, b):
    return _dot(a, b, ((1,), (1,)))


def _mm_tn(a, b):
    return _dot(a, b, ((0,), (0,)))


def _split2(x):
    hi = x.astype(BF16)
    return hi, (x - hi.astype(F32)).astype(BF16)


def _dot3_raw(a, b, dims):
    a_hi, a_lo = _split2(a)
    b_hi, b_lo = _split2(b)
    dot = lambda p, q: lax.dot_general(p, q, (dims, ((), ())), preferred_element_type=F32)
    return dot(a_hi, b_hi) + (dot(a_hi, b_lo) + dot(a_lo, b_hi))


def _dot1_raw(a, b, dims):
    return lax.dot_general(a.astype(BF16), b.astype(BF16), (dims, ((), ())), preferred_element_type=F32)


NN, NT, TN = ((1,), (0,)), ((1,), (1,)), ((0,), (0,))


def _make_dot(raw):
    @functools.partial(jax.custom_vjp, nondiff_argnums=(2,))
    def dot(a, b, dims):
        return raw(a, b, dims)

    def fwd(a, b, dims):
        return raw(a, b, dims), (a, b)

    def bwd(dims, saved, g):
        a, b = saved
        if dims == NN:
            return raw(g, b, NT), raw(a, g, TN)
        if dims == NT:
            return raw(g, b, NN), raw(g, a, TN)
        return raw(b, g, NT), raw(a, g, NN)

    dot.defvjp(fwd, bwd)
    return dot


_dot3 = _make_dot(_dot3_raw)
_dot1 = _make_dot(_dot1_raw)


def _dot_ind_raw(x, ind, dims):
    hi = x.astype(BF16)
    rest = x - hi.astype(F32)
    mid = rest.astype(BF16)
    lo = (rest - mid.astype(F32)).astype(BF16)
    ind = ind.astype(BF16)
    dot = lambda p: lax.dot_general(p, ind, (dims, ((), ())), preferred_element_type=F32)
    return dot(hi) + (dot(mid) + dot(lo))


@jax.custom_vjp
def _mm_ind(x, ind):
    return _dot_ind_raw(x, ind, NN)


def _mm_ind_fwd(x, ind):
    return _dot_ind_raw(x, ind, NN), ind


def _mm_ind_bwd(ind, g):
    return _dot_ind_raw(g, ind, NT), jnp.zeros_like(ind)


_mm_ind.defvjp(_mm_ind_fwd, _mm_ind_bwd)


def _sigmoid(x):
    return 1.0 / (1.0 + jnp.exp(-x))


def _softplus(x):
    return jnp.maximum(x, 0.0) + jnp.log(1.0 + jnp.exp(-jnp.abs(x)))


def _gelu_tanh(x):
    return 0.5 * x * (1.0 + jnp.tanh(math.sqrt(2.0 / math.pi) * (x + 0.044715 * (x * x * x))))


def _seg_indicator(n, seg):
    r = lax.broadcasted_iota(jnp.int32, (n, n // seg), 0)
    c = lax.broadcasted_iota(jnp.int32, (n, n // seg), 1)
    return jnp.where((r >= c * seg) & (r < (c + 1) * seg), 1.0, 0.0).astype(F32)


def _seg_indicator_t(n, seg):
    c = lax.broadcasted_iota(jnp.int32, (n // seg, n), 0)
    r = lax.broadcasted_iota(jnp.int32, (n // seg, n), 1)
    return jnp.where((r >= c * seg) & (r < (c + 1) * seg), 1.0, 0.0).astype(F32)


def _seg_sum(x, seg):
    return _mm_ind(x, _seg_indicator(x.shape[1], seg))


def _seg_expand(s, seg):
    return _mm_ind(s, _seg_indicator_t(s.shape[1] * seg, seg))


def _tile_lanes(g, reps):
    n = g.shape[1]
    r = lax.broadcasted_iota(jnp.int32, (n, n * reps), 0)
    c = lax.broadcasted_iota(jnp.int32, (n, n * reps), 1)
    return _mm_ind(g, jnp.where((c & (n - 1)) == r, 1.0, 0.0).astype(F32))


def _rms(x, g):
    return x * lax.rsqrt(jnp.mean(x * x, axis=-1, keepdims=True) + RMS_EPS) * g


MATMUL_VMEM_BUDGET = 40 * 1024 * 1024
MATMUL_MAX_TILE = 2048
HBM_BYTES_PER_S = 1.2e12
MXU_FLOPS_PER_S = 8e14
GRID_STEP_S = 0.35e-6


def _tile_candidates(dim):
    c = [d for d in range(LANES, min(dim, MATMUL_MAX_TILE) + 1, LANES) if dim % d == 0]
    return c or [dim]


def _matmul_tiles(m, n, k, a_bytes, b_bytes, out_bytes, res_bytes):
    best = None
    for tm in _tile_candidates(m):
        for tn in _tile_candidates(n):
            for tk in _tile_candidates(k):
                casts = (tm * tk * 2 if a_bytes > 2 else 0) + (tk * tn * 2 if b_bytes > 2 else 0)
                vmem = (2 * (tm * tk * a_bytes + tk * tn * b_bytes + tm * tn * (out_bytes + res_bytes))
                        + 2 * tm * tn * 4 + casts)
                if vmem > MATMUL_VMEM_BUDGET:
                    continue
                steps = (m // tm) * (n // tn) * (k // tk)
                a_reads = 1 if k == tk else n // tn
                traffic = (m * k * a_bytes * a_reads + k * n * b_bytes * (m // tm) + m * n * (out_bytes + res_bytes))
                dma_s = traffic / HBM_BYTES_PER_S
                ends_s = (tm * tk * a_bytes + tk * tn * b_bytes + tm * tn * out_bytes) / HBM_BYTES_PER_S
                cost = max(dma_s, 2.0 * m * n * k / MXU_FLOPS_PER_S) + 0.2 * dma_s + steps * GRID_STEP_S + ends_s
                if best is None or cost < best[0]:
                    best = (cost, tm, tn, tk)
    return best[1:]


def matmul(name, a, b, *, ta=False, tb=False, alpha=1.0, res=None, out_dtype=F32, hosted=None):
    m, k = (a.shape[1], a.shape[0]) if ta else a.shape
    n = b.shape[0] if tb else b.shape[1]
    assert k == (b.shape[1] if tb else b.shape[0]), (name, a.shape, b.shape)
    tm, tn, tk = _matmul_tiles(m, n, k, a.dtype.itemsize, b.dtype.itemsize, jnp.dtype(out_dtype).itemsize,
                               0 if res is None else res.dtype.itemsize)
    nk = k // tk
    grid = (m // tm, n // tn, nk)
    dims = ((0 if ta else 1,), (1 if tb else 0,))
    h_arrays, h_specs, h_shapes, h_scratch = _hosted_call_args(hosted)
    nhosted = len(h_arrays)
    n_in = 2 + (res is not None)

    def body(*refs):
        a_ref, b_ref = refs[:2]
        res_ref = refs[2] if res is not None else None
        h_in, o_ref, h_out = refs[n_in:n_in + nhosted], refs[n_in + nhosted], refs[n_in + nhosted + 1:n_in + 2 * nhosted + 1]
        acc_ref, sems = refs[n_in + 2 * nhosted + 1], refs[n_in + 2 * nhosted + 2:]
        _hosted_steps(hosted, h_in, h_out, sems, grid, 1.0)
        kk = pl.program_id(2)

        @pl.when(kk == 0)
        def _():
            acc_ref[...] = jnp.zeros_like(acc_ref)

        acc_ref[...] += lax.dot_general(a_ref[...].astype(BF16), b_ref[...].astype(BF16), (dims, ((), ())),
                                        preferred_element_type=F32)

        @pl.when(kk == nk - 1)
        def _():
            out = acc_ref[...] * alpha
            if res_ref is not None:
                out = out + res_ref[...].astype(F32)
            o_ref[...] = out.astype(o_ref.dtype)

    a_spec = pl.BlockSpec((tk, tm), lambda i, j, kk: (kk, i)) if ta else pl.BlockSpec((tm, tk), lambda i, j, kk: (i, kk))
    b_spec = pl.BlockSpec((tn, tk), lambda i, j, kk: (j, kk)) if tb else pl.BlockSpec((tk, tn), lambda i, j, kk: (kk, j))
    o_spec = pl.BlockSpec((tm, tn), lambda i, j, kk: (i, j))
    in_specs = [a_spec, b_spec] + ([o_spec] if res is not None else [])
    args = (a, b) + ((res,) if res is not None else ())
    out_shape = jax.ShapeDtypeStruct((m, n), out_dtype)
    if hosted is None:
        return pl.pallas_call(
            body, name=name, grid=grid, in_specs=in_specs, out_specs=o_spec, out_shape=out_shape,
            scratch_shapes=[pltpu.VMEM((tm, tn), F32)], compiler_params=_params(("parallel", "parallel", "arbitrary")),
        )(*args)
    out, *travelled = pl.pallas_call(
        body, name=name, grid=grid, in_specs=in_specs + h_specs, out_specs=[o_spec] + h_specs,
        out_shape=[out_shape] + h_shapes, scratch_shapes=[pltpu.VMEM((tm, tn), F32)] + h_scratch,
        compiler_params=_params(("arbitrary", "arbitrary", "arbitrary")),
    )(*args, *h_arrays)
    return out, travelled


def _row_specs(arrays, tr):
    return [pl.BlockSpec((tr, a.shape[1]), lambda i: (i, 0)) for a in arrays]


def _whole_specs(arrays):
    return [pl.BlockSpec(a.shape, lambda i: (0, 0)) for a in arrays]


def rowwise(name, fn, rows, consts, outs, tr):
    t = rows[0].shape[0]
    nr, nc = len(rows), len(consts)

    def body(*refs):
        vals = [r[...] for r in refs[:nr + nc]]
        res = fn(*vals)
        for o_ref, o in zip(refs[nr + nc:], res):
            o_ref[...] = o.astype(o_ref.dtype)

    out_shape = [jax.ShapeDtypeStruct((t, w), d) for w, d in outs]
    return pl.pallas_call(
        body, name=name, grid=(t // tr,), in_specs=_row_specs(rows, tr) + _whole_specs(consts),
        out_specs=_row_specs(out_shape, tr), out_shape=out_shape, compiler_params=_params(("parallel",)),
    )(*rows, *consts)


def rowwise_vjp(name, fn, rows, consts, cots, tr, row_grad, const_grad, add_to=None, grad_dtypes=None):
    t = rows[0].shape[0]
    nr, nc = len(rows), len(consts)
    cot_groups = [c if isinstance(c, (tuple, list)) else (c,) for c in cots]
    flat_cots = [a for g in cot_groups for a in g]
    add_to = {i: (a if isinstance(a, (tuple, list)) else (a,)) for i, a in (add_to or {}).items()}
    add_idx = [(i, q) for i in sorted(add_to) for q in range(len(add_to[i]))]
    add_arrays = [add_to[i][q] for i, q in add_idx]
    r_idx = [i for i in range(nr) if row_grad[i]]
    c_idx = [i for i in range(nc) if const_grad[i]]
    grad_dtypes = grad_dtypes or {}
    n_in = nr + nc + len(flat_cots) + len(add_arrays)

    def body(*refs):
        vals = [r[...] for r in refs[:nr + nc]]
        pos = nr + nc
        cts = []
        for g in cot_groups:
            s = refs[pos][...].astype(F32)
            for q in range(1, len(g)):
                s = s + refs[pos + q][...].astype(F32)
            cts.append(s)
            pos += len(g)
        adds = {}
        for n_add, (i, _) in enumerate(add_idx):
            term = refs[pos + n_add][...].astype(F32)
            adds[i] = adds[i] + term if i in adds else term
        diff =[vals[i] for i in r_idx] + [vals[nr + i] for i in c_idx]

        def f(*d):
            full = list(vals)
            for q, i in enumerate(r_idx):
                full[i] = d[q]
            for q, i in enumerate(c_idx):
                full[nr + i] = d[len(r_idx) + q]
            return tuple(fn(*full))

        prim, vjp = jax.vjp(f, *diff)
        grads = vjp(tuple(c.astype(p.dtype) for c, p in zip(cts, prim)))
        o_refs = refs[n_in:]
        for q, i in enumerate(r_idx):
            g = grads[q].astype(F32)
            if i in adds:
                g = g + adds[i].astype(F32)
            o_refs[q][...] = g.astype(o_refs[q].dtype)
        step = pl.program_id(0)
        for q, i in enumerate(c_idx):
            o_ref = o_refs[len(r_idx) + q]

            @pl.when(step == 0)
            def _(o_ref=o_ref):
                o_ref[...] = jnp.zeros_like(o_ref)

            o_ref[...] += grads[len(r_idx) + q].astype(F32)

    row_out = [jax.ShapeDtypeStruct(rows[i].shape, grad_dtypes.get(i, F32)) for i in r_idx]
    const_out = [jax.ShapeDtypeStruct(consts[i].shape, F32) for i in c_idx]
    outs = pl.pallas_call(
        body, name=name, grid=(t // tr,),
        in_specs=_row_specs(rows, tr) + _whole_specs(consts) + _row_specs(flat_cots, tr) + _row_specs(add_arrays, tr),
        out_specs=_row_specs(row_out, tr) + _whole_specs(const_out), out_shape=row_out + const_out,
        compiler_params=_params(("arbitrary",)),
    )(*rows, *consts, *flat_cots, *add_arrays)
    return list(outs)


def loss_head(y, target):
    t, d = y.shape
    tr = _pick(t, (256, 128, 64, 32, 16, 8))

    def body(y_ref, t_ref, dy_ref, l_ref):
        diff = y_ref[...] - t_ref[...]
        dy_ref[...] = diff * (1.0 / d)

        @pl.when(pl.program_id(0) == 0)
        def _():
            l_ref[...] = jnp.zeros_like(l_ref)

        l_ref[...] += 0.5 * jnp.sum(jnp.mean(diff * diff, axis=-1, keepdims=True), axis=0, keepdims=True)

    dy, l = pl.pallas_call(
        body, name="loss_head", grid=(t // tr,), in_specs=_row_specs([y, target], tr),
        out_specs=[pl.BlockSpec((tr, d), lambda i: (i, 0)), pl.BlockSpec((1, 1), lambda i: (0, 0))],
        out_shape=[jax.ShapeDtypeStruct((t, d), F32), jax.ShapeDtypeStruct((1, 1), F32)],
        compiler_params=_params(("arbitrary",)),
    )(y, target)
    return dy, l


ATT_PAD = N_LEFT_CHUNKS * CHUNK
MASKED = -1e30
ATT_GROUP = 2


def _attn_chunks(q_c, k_b, v_b, bias, valid):
    s = [jnp.where(ok, _dot1(q, k, NT) + b, MASKED) for q, k, b, ok in zip(q_c, k_b, bias, valid)]
    e = [jnp.exp(x - lax.stop_gradient(jnp.max(x, axis=-1, keepdims=True))) for x in s]
    p = [x / jnp.sum(x, axis=-1, keepdims=True) for x in e]
    return tuple(_dot1(x, v, NN) for x, v in zip(p, v_b))


def _band_valid(c):
    return (c * CHUNK + lax.broadcasted_iota(jnp.int32, (1, BAND), 1)) >= ATT_PAD


HEAD_PAIR = LANES // HEAD_DIM


def _split_heads(x, n):
    return [x[:, h * HEAD_DIM:(h + 1) * HEAD_DIM] for h in range(n)]


def _pair_spec(t):
    return pl.BlockSpec((t, HEAD_PAIR * HEAD_DIM), lambda p: (0, p))


def _attn_operands(q_ref, kp, vp, bias, cs, q0):
    qs = [x for s in q0 for x in _split_heads(q_ref[pl.ds(s, CHUNK), :], HEAD_PAIR)]
    ks = [x for s in q0 for x in _split_heads(kp[pl.ds(s, BAND), :], HEAD_PAIR)]
    vs = [x for s in q0 for x in _split_heads(vp[pl.ds(s, BAND), :], HEAD_PAIR)]
    return tuple(qs), tuple(ks), tuple(vs), tuple(bias) * len(cs), tuple(_band_valid(c) for c in cs for _ in bias)


def attention_fwd(q, k, v, bias, hosted=None):
    t, width = q.shape
    pairs = width // (HEAD_PAIR * HEAD_DIM)
    nchunks = t // CHUNK
    group = _pick(nchunks, (ATT_GROUP, 1))
    h_arrays, h_specs, h_shapes, h_scratch = _hosted_call_args(hosted)
    nhosted = len(h_arrays)

    def body(q_ref, k_ref, v_ref, b_ref, *rest):
        h_in, o_ref, h_out = rest[:nhosted], rest[nhosted], rest[nhosted + 1:2 * nhosted + 1]
        kp, vp, sems = rest[2 * nhosted + 1], rest[2 * nhosted + 2], rest[2 * nhosted + 3:]
        _hosted_steps(hosted, h_in, h_out, sems, (pairs,), 0.6)
        zeros = jnp.zeros((ATT_PAD, HEAD_PAIR * HEAD_DIM), F32)
        kp[pl.ds(0, ATT_PAD), :] = zeros
        vp[pl.ds(0, ATT_PAD), :] = zeros
        kp[pl.ds(ATT_PAD, t), :] = k_ref[...]
        vp[pl.ds(ATT_PAD, t), :] = v_ref[...]
        bias = [b_ref[h] for h in range(HEAD_PAIR)]

        def step(g, carry):
            cs = [g * group + u for u in range(group)]
            q0 = [pl.multiple_of(c * CHUNK, CHUNK) for c in cs]
            outs = _attn_chunks(*_attn_operands(q_ref, kp, vp, bias, cs, q0))
            for u, s in enumerate(q0):
                o_ref[pl.ds(s, CHUNK), :] = jnp.concatenate(outs[HEAD_PAIR * u:HEAD_PAIR * (u + 1)], axis=1)
            return carry

        lax.fori_loop(0, nchunks // group, step, 0)

    out, *travelled = pl.pallas_call(
        body, name="attention_fwd", grid=(pairs,),
        in_specs=[_pair_spec(t)] * 3 + [pl.BlockSpec((HEAD_PAIR, CHUNK, BAND), lambda p: (p, 0, 0))] + h_specs,
        out_specs=[_pair_spec(t)] + h_specs, out_shape=[jax.ShapeDtypeStruct((t, width), F32)] + h_shapes,
        scratch_shapes=[pltpu.VMEM((t + ATT_PAD, HEAD_PAIR * HEAD_DIM), F32)] * 2 + h_scratch,
        compiler_params=_params(("arbitrary",)),
    )(q, k, v, bias, *h_arrays)
    return out if hosted is None else (out, travelled)


def attention_bwd(q, k, v, bias, dout):
    t, width = q.shape
    pairs = width // (HEAD_PAIR * HEAD_DIM)
    nchunks = t // CHUNK
    group = _pick(nchunks, (ATT_GROUP, 1))

    def body(q_ref, k_ref, v_ref, b_ref, do_ref, dq_ref, dk_ref, dv_ref, db_ref, kp, vp, dkp, dvp):
        zeros = jnp.zeros((ATT_PAD, HEAD_PAIR * HEAD_DIM), F32)
        kp[pl.ds(0, ATT_PAD), :] = zeros
        vp[pl.ds(0, ATT_PAD), :] = zeros
        kp[pl.ds(ATT_PAD, t), :] = k_ref[...]
        vp[pl.ds(ATT_PAD, t), :] = v_ref[...]
        dkp[...] = jnp.zeros_like(dkp)
        dvp[...] = jnp.zeros_like(dvp)
        db_ref[...] = jnp.zeros_like(db_ref)
        bias = [b_ref[h] for h in range(HEAD_PAIR)]

        def step(g, carry):
            cs = [g * group + u for u in range(group)]
            q0 = [pl.multiple_of(c * CHUNK, CHUNK) for c in cs]
            qs, ks, vs, bs, valid = _attn_operands(q_ref, kp, vp, bias, cs, q0)
            _, vjp = jax.vjp(lambda a, b, cc, d: _attn_chunks(a, b, cc, d, valid), qs, ks, vs, bs)
            dos = tuple(x for s in q0 for x in _split_heads(do_ref[pl.ds(s, CHUNK), :], HEAD_PAIR))
            dq, dk, dv, db = vjp(dos)
            for u, s in enumerate(q0):
                mine = slice(HEAD_PAIR * u, HEAD_PAIR * (u + 1))
                dq_ref[pl.ds(s, CHUNK), :] = jnp.concatenate(dq[mine], axis=1)
                dkp[pl.ds(s, BAND), :] += jnp.concatenate(dk[mine], axis=1)
                dvp[pl.ds(s, BAND), :] += jnp.concatenate(dv[mine], axis=1)
            for h in range(HEAD_PAIR):
                total = db[h]
                for u in range(1, group):
                    total = total + db[HEAD_PAIR * u + h]
                db_ref[h] += total
            return carry

        lax.fori_loop(0, nchunks // group, step, 0)
        dk_ref[...] = dkp[pl.ds(ATT_PAD, t), :]
        dv_ref[...] = dvp[pl.ds(ATT_PAD, t), :]

    bias_spec = pl.BlockSpec((HEAD_PAIR, CHUNK, BAND), lambda p: (p, 0, 0))
    ts = jax.ShapeDtypeStruct((t, width), F32)
    return pl.pallas_call(
        body, name="attention_bwd", grid=(pairs,), in_specs=[_pair_spec(t)] * 3 + [bias_spec, _pair_spec(t)],
        out_specs=[_pair_spec(t)] * 3 + [bias_spec],
        out_shape=[ts, ts, ts, jax.ShapeDtypeStruct(bias.shape, F32)],
        scratch_shapes=[pltpu.VMEM((t + ATT_PAD, HEAD_PAIR * HEAD_DIM), F32)] * 4, compiler_params=_params(("parallel",)),
    )(q, k, v, bias, dout)


def _rel_onehot_t(i):
    j = lax.broadcasted_iota(jnp.int32, (N_REL, BAND), 1)
    r = lax.broadcasted_iota(jnp.int32, (N_REL, BAND), 0)
    idx = jnp.clip(i + ATT_PAD - j, -(CHUNK - 1), REL_CLIP) + (CHUNK - 1)
    return jnp.where(r == idx, 1.0, 0.0).astype(F32)


def bias_expand(rel):
    nh = rel.shape[0]

    def body(rel_ref, o_ref):
        o_ref[...] = _mm(rel_ref[...], _rel_onehot_t(pl.program_id(0)))

    return pl.pallas_call(
        body, name="bias_expand", grid=(CHUNK,), in_specs=[pl.BlockSpec((nh, N_REL), lambda i: (0, 0))],
        out_specs=pl.BlockSpec((None, nh, BAND), lambda i: (i, 0, 0)),
        out_shape=jax.ShapeDtypeStruct((CHUNK, nh, BAND), F32), compiler_params=_params(("parallel",)),
    )(rel)


def bias_reduce(dbias):
    nh = dbias.shape[1]

    def body(d_ref, o_ref):
        @pl.when(pl.program_id(0) == 0)
        def _():
            o_ref[...] = jnp.zeros_like(o_ref)

        o_ref[...] += _mm_nt(d_ref[...], _rel_onehot_t(pl.program_id(0)))

    return pl.pallas_call(
        body, name="bias_reduce", grid=(CHUNK,), in_specs=[pl.BlockSpec((None, nh, BAND), lambda i: (i, 0, 0))],
        out_specs=pl.BlockSpec((nh, N_REL), lambda i: (0, 0)), out_shape=jax.ShapeDtypeStruct((nh, N_REL), F32),
        compiler_params=_params(("arbitrary",)),
    )(dbias)


def _tri(n, strict):
    r = lax.broadcasted_iota(jnp.int32, (n, n), 0)
    c = lax.broadcasted_iota(jnp.int32, (n, n), 1)
    return (c < r) if strict else (c <= r)


def _each(f, *lists):
    return [f(*args) for args in zip(*lists)]


def _rwkv_chunk(s, r, k, v, w, a, b):
    n = r[0].shape[0]
    strict, incl = _tri(n, True), _tri(n, False)
    ones = jnp.where(incl, 1.0, 0.0).astype(F32)
    eye = jnp.where(incl & ~strict, 1.0, 0.0).astype(F32)
    lw = _each(jnp.log, w)
    cum = _each(lambda x: _mm(ones, x), lw)
    p_incl = _each(jnp.exp, cum)
    p_inv = _each(lambda x: jnp.exp(-x), cum)
    a_t = _each(lambda x, c, l: x * jnp.exp(c - l), a, cum, lw)
    r_t = _each(jnp.multiply, r, p_incl)
    b_t = _each(jnp.multiply, b, p_inv)
    k_t = _each(jnp.multiply, k, p_inv)
    a_ab = _each(lambda x, y: jnp.where(strict, _dot3(x, y, NT), 0.0), a_t, b_t)
    a_ak = _each(lambda x, y: jnp.where(strict, _dot3(x, y, NT), 0.0), a_t, k_t)
    r_b = _each(lambda x, y: jnp.where(incl, _dot3(x, y, NT), 0.0), r_t, b_t)
    r_k = _each(lambda x, y: jnp.where(incl, _dot3(x, y, NT), 0.0), r_t, k_t)
    a_s = _each(lambda x, y: _dot3(x, y, NT), a_t, s)
    r_s = _each(lambda x, y: _dot3(x, y, NT), r_t, s)
    a_kv = _each(lambda x, y: _dot3(x, y, NN), a_ak, v)
    total = _each(lambda x: eye + x, a_ab)
    power = _each(lambda x: _dot3(x, x, NN), a_ab)
    rounds = int(math.log2(n)) - 1
    for i in range(rounds):
        total = _each(lambda t, p: t + _dot3(t, p, NN), total, power)
        if i < rounds - 1:
            power = _each(lambda p: _dot3(p, p, NN), power)
    sa = _each(lambda t, x, y: _dot3(t, x + y, NN), total, a_s, a_kv)
    y = _each(lambda rs, rb, x, rk, vv: rs + _dot3(rb, x, NN) + _dot3(rk, vv, NN), r_s, r_b, sa, r_k, v)
    s_new = _each(lambda ss, x, bt, vv, kt, p: (ss + _dot3(x, bt, TN) + _dot3(vv, kt, TN)) * p[n - 1:n, :],
                  s, sa, b_t, v, k_t, p_incl)
    return tuple(s_new), tuple(y)


RWKV_TILE = 256
RWKV_HEADS = 8


def _rwkv_specs(nh, t, reverse):
    tile = min(RWKV_TILE, t)
    hb = RWKV_HEADS if nh % RWKV_HEADS == 0 else nh
    nt = t // tile
    per = tile // RWKV_CHUNK
    pos = (lambda h, i: (nt - 1 - i, h)) if reverse else (lambda h, i: (i, h))
    pos4 = (lambda h, i: (h, nt - 1 - i, 0, 0)) if reverse else (lambda h, i: (h, i, 0, 0))
    return (hb, nt, per, pl.BlockSpec((tile, hb * HEAD_DIM), pos), pl.BlockSpec((hb, per, HEAD_DIM, HEAD_DIM), pos4))


def _hosted_steps(exchange, refs_in, refs_out, sems, grid, forward_share):
    if exchange is None:
        return
    start, forward, finish = exchange.plan(refs_in, refs_out, *sems)
    step, n_steps = 0, 1
    for axis, size in enumerate(grid):
        step = step * size + pl.program_id(axis)
        n_steps *= size
    pl.when(step == 0)(start)
    pl.when(step == int(forward_share * (n_steps - 1)))(forward)
    pl.when(step == n_steps - 1)(finish)


def _hosted_call_args(exchange):
    if exchange is None:
        return [], [], [], []
    nb = len(exchange.arrays)
    return exchange.arrays, [HBM_SPEC] * nb, exchange.out_shape, exchange.scratch()


def rwkv_fwd(r, k, v, w, a, b, hosted=None):
    t, width = r.shape
    nh = width // HEAD_DIM
    hb, nt, per, row_spec, s_spec = _rwkv_specs(nh, t, False)
    h_arrays, h_specs, h_shapes, h_scratch = _hosted_call_args(hosted)
    nhosted = len(h_arrays)
    n_steps = (nh // hb) * nt

    def body(r_ref, k_ref, v_ref, w_ref, a_ref, b_ref, *rest):
        h_in, (y_ref, s_ref), h_out = rest[:nhosted], rest[nhosted:nhosted + 2], rest[nhosted + 2:2 * nhosted + 2]
        state, sems = rest[2 * nhosted + 2], rest[2 * nhosted + 3:]
        _hosted_steps(hosted, h_in, h_out, sems, (nh // hb, nt), 0.9)

        @pl.when(pl.program_id(1) == 0)
        def _():
            state[...] = jnp.zeros_like(state)

        def step(c, states):
            rows = pl.ds(pl.multiple_of(c * RWKV_CHUNK, RWKV_CHUNK), RWKV_CHUNK)
            for h in range(hb):
                s_ref[h, c] = states[h]
            s_new, y = _rwkv_chunk(states, *(tuple(_split_heads(ref[rows, :], hb))
                                             for ref in (r_ref, k_ref, v_ref, w_ref, a_ref, b_ref)))
            y_ref[rows, :] = jnp.concatenate(y, axis=1)
            return s_new

        final = lax.fori_loop(0, per, step, tuple(state[h] for h in range(hb)))
        for h in range(hb):
            state[h] = final[h]

    return pl.pallas_call(
        body, name="rwkv_fwd", grid=(nh // hb, nt), in_specs=[row_spec] * 6 + h_specs,
        out_specs=[row_spec, s_spec] + h_specs,
        out_shape=[jax.ShapeDtypeStruct((t, width), F32),
                   jax.ShapeDtypeStruct((nh, t // RWKV_CHUNK, HEAD_DIM, HEAD_DIM), F32)] + h_shapes,
        scratch_shapes=[pltpu.VMEM((hb, HEAD_DIM, HEAD_DIM), F32)] + h_scratch,
        compiler_params=_params(("arbitrary", "arbitrary")),
    )(r, k, v, w, a, b, *h_arrays)


def rwkv_bwd(r, k, v, w, a, b, states, dy, hosted=None):
    t, width = r.shape
    nh = width // HEAD_DIM
    hb, nt, per, row_spec, s_spec = _rwkv_specs(nh, t, True)
    h_arrays, h_specs, h_shapes, h_scratch = _hosted_call_args(hosted)
    nhosted = len(h_arrays)
    n_steps = (nh // hb) * nt

    def body(r_ref, k_ref, v_ref, w_ref, a_ref, b_ref, s_ref, dy_ref, *rest):
        h_in, d_refs, h_out = rest[:nhosted], rest[nhosted:nhosted + 6], rest[nhosted + 6:2 * nhosted + 6]
        dstate, sems = rest[2 * nhosted + 6], rest[2 * nhosted + 7:]
        _hosted_steps(hosted, h_in, h_out, sems, (nh // hb, nt), 0.5)

        @pl.when(pl.program_id(1) == 0)
        def _():
            dstate[...] = jnp.zeros_like(dstate)

        def step(i, ds):
            c = per - 1 - i
            rows = pl.ds(pl.multiple_of(c * RWKV_CHUNK, RWKV_CHUNK), RWKV_CHUNK)
            _, vjp = jax.vjp(_rwkv_chunk, tuple(s_ref[h, c] for h in range(hb)),
                             *(tuple(_split_heads(ref[rows, :], hb))
                               for ref in (r_ref, k_ref, v_ref, w_ref, a_ref, b_ref)))
            grads = vjp((ds, tuple(_split_heads(dy_ref[rows, :], hb))))
            for d_ref, g in zip(d_refs, grads[1:]):
                d_ref[rows, :] = jnp.concatenate(g, axis=1)
            return grads[0]

        final = lax.fori_loop(0, per, step, tuple(dstate[h] for h in range(hb)))
        for h in range(hb):
            dstate[h] = final[h]

    hs = jax.ShapeDtypeStruct((t, width), F32)
    return pl.pallas_call(
        body, name="rwkv_bwd", grid=(nh // hb, nt), in_specs=[row_spec] * 6 + [s_spec, row_spec] + h_specs,
        out_specs=[row_spec] * 6 + h_specs, out_shape=[hs] * 6 + h_shapes,
        scratch_shapes=[pltpu.VMEM((hb, HEAD_DIM, HEAD_DIM), F32)] + h_scratch,
        compiler_params=_params(("arbitrary", "arbitrary")),
    )(r, k, v, w, a, b, states, dy, *h_arrays)


SUBLANES = 8


def s5_scan(bu_re, bu_im, a_re, a_im, *, reverse=False, h_prev=None):
    t, n = bu_re.shape
    tt = _pick(t, (512, 256, 128, 64, 32, 16, 8))
    tc = _pick(n, (1024, 512, 256, 128))
    nt = t // tt
    with_da = h_prev is not None
    sign = -1.0 if reverse else 1.0

    def body(*refs):
        if with_da:
            br, bi, ar_ref, ai_ref, pr, pi, hr_ref, hi_ref, dar_ref, dai_ref, sr, si = refs
        else:
            br, bi, ar_ref, ai_ref, hr_ref, hi_ref, sr, si = refs
        ti = pl.program_id(1)

        @pl.when(ti == 0)
        def _():
            sr[...] = jnp.zeros_like(sr)
            si[...] = jnp.zeros_like(si)
            if with_da:
                dar_ref[...] = jnp.zeros_like(dar_ref)
                dai_ref[...] = jnp.zeros_like(dai_ref)

        ar = ar_ref[...]
        ai = ai_ref[...] * sign

        def group(gi, carry):
            hr, hi, dar, dai = carry
            g0 = pl.multiple_of((tt // SUBLANES - 1 - gi if reverse else gi) * SUBLANES, SUBLANES)
            rows = pl.ds(g0, SUBLANES)
            xr, xi = br[rows, :], bi[rows, :]
            if with_da:
                qr, qi = pr[rows, :], pi[rows, :]
            out_r, out_i = [None] * SUBLANES, [None] * SUBLANES
            for s in (range(SUBLANES - 1, -1, -1) if reverse else range(SUBLANES)):
                if with_da:
                    dar = dar + hr * qr[s:s + 1, :] + hi * qi[s:s + 1, :]
                    dai = dai + hi * qr[s:s + 1, :] - hr * qi[s:s + 1, :]
                hr, hi = ar * hr - ai * hi + xr[s:s + 1, :], ar * hi + ai * hr + xi[s:s + 1, :]
                out_r[s], out_i[s] = hr, hi
            hr_ref[rows, :] = jnp.concatenate(out_r, axis=0)
            hi_ref[rows, :] = jnp.concatenate(out_i, axis=0)
            return hr, hi, dar, dai

        zero = jnp.zeros((1, tc), F32)
        hr, hi, dar, dai = lax.fori_loop(0, tt // SUBLANES, group, (sr[...], si[...], zero, zero))
        sr[...] = hr
        si[...] = hi
        if with_da:
            dar_ref[...] += dar
            dai_ref[...] += dai

    tile = pl.BlockSpec((tt, tc), (lambda ci, ti: (nt - 1 - ti, ci)) if reverse else (lambda ci, ti: (ti, ci)))
    col = pl.BlockSpec((1, tc), lambda ci, ti: (0, ci))
    hs = jax.ShapeDtypeStruct((t, n), F32)
    cs = jax.ShapeDtypeStruct((1, n), F32)
    ins = [bu_re, bu_im, a_re, a_im] + (list(h_prev) if with_da else [])
    return pl.pallas_call(
        body, name="s5_scan_bwd" if reverse else "s5_scan_fwd", grid=(n // tc, nt),
        in_specs=[tile, tile, col, col] + ([tile, tile] if with_da else []),
        out_specs=[tile, tile] + ([col, col] if with_da else []), out_shape=[hs, hs] + ([cs, cs] if with_da else []),
        scratch_shapes=[pltpu.VMEM((1, tc), F32)] * 2, compiler_params=_params(("parallel", "arbitrary")),
    )(*ins)


N_CHIPS = 4
HBM_SPEC = pl.BlockSpec(memory_space=pl.ANY)


def _remote(src, dst, send_sem, recv_sem, device):
    return pltpu.make_async_remote_copy(src_ref=src, dst_ref=dst, send_sem=send_sem, recv_sem=recv_sem,
                                        device_id=device, device_id_type=pl.DeviceIdType.MESH)


class Exchange:
    def __init__(self, arrays, out_shape, n_sems, plan):
        self.arrays, self.out_shape, self.n_sems, self.plan = list(arrays), out_shape, n_sems, plan

    def scratch(self):
        nb = len(self.arrays)
        return [pltpu.SemaphoreType.DMA((nb, self.n_sems)), pltpu.SemaphoreType.DMA((nb, self.n_sems)),
                pltpu.SemaphoreType.DMA((nb,))]

    def run(self, name):
        nb = len(self.arrays)

        def body(*refs):
            for step in self.plan(refs[:nb], refs[nb:2 * nb], *refs[2 * nb:]):
                step()

        return pl.pallas_call(
            body, name=name, in_specs=[HBM_SPEC] * nb, out_specs=[HBM_SPEC] * nb, out_shape=self.out_shape,
            scratch_shapes=self.scratch(), compiler_params=pltpu.CompilerParams(has_side_effects=True),
        )(*self.arrays)


def gather_all(arrays):
    nb = len(arrays)

    def plan(ins, outs, send_sems, recv_sems, local_sems):
        x, y, c = (lax.axis_index(n) for n in MESH_AXES)
        me, sibling = 4 * x + 2 * y + c, 4 * x + 2 * y + 1 - c
        chips = [(1 - x, y), (x, 1 - y), (1 - x, 1 - y)]

        def copy(b, k, slot, to, src=None):
            block = outs[b].at[slot]
            return _remote(block if src is None else src, block, send_sems.at[b, k], recv_sems.at[b, k], to)

        def local():
            return [pltpu.make_async_copy(ins[b], outs[b].at[me], local_sems.at[b]) for b in range(nb)]

        def first():
            return [cp for b in range(nb) for cp in
                    [copy(b, 0, me, (x, y, 1 - c), src=ins[b])]
                    + [copy(b, 1 + j, me, (px, py, c), src=ins[b]) for j, (px, py) in enumerate(chips)]]

        def passed():
            return [copy(b, 4 + j, 4 * px + 2 * py + c, (x, y, 1 - c)) for j, (px, py) in enumerate(chips) for b in range(nb)]

        def start():
            for cp in local() + first():
                cp.start()

        def forward():
            for j, (px, py) in enumerate(chips):
                for b in range(nb):
                    copy(b, 1 + j, 4 * px + 2 * py + c, (px, py, c)).wait_recv()
                    copy(b, 4 + j, 4 * px + 2 * py + c, (x, y, 1 - c)).start()

        def finish():
            for b in range(nb):
                copy(b, 0, sibling, (x, y, 1 - c)).wait_recv()
                for j, (px, py) in enumerate(chips):
                    copy(b, 4 + j, 4 * px + 2 * py + 1 - c, (x, y, 1 - c)).wait_recv()
            for cp in first() + passed():
                cp.wait_send()
            for cp in local():
                cp.wait()

        return start, forward, finish

    out_shape = [jax.ShapeDtypeStruct((N_DEV,) + tuple(a.shape), a.dtype) for a in arrays]
    return Exchange(arrays, out_shape, N_DEV - 1, plan)


def _nothing():
    pass


def pair_exchange(arrays):
    nb = len(arrays)

    def plan(ins, outs, send_sems, recv_sems, local_sems):
        x, y, c = (lax.axis_index(n) for n in MESH_AXES)

        def copies():
            return [_remote(ins[b].at[2 * z + 1 - c], outs[b].at[z], send_sems.at[b, z], recv_sems.at[b, z], (x, y, 1 - c))
                    for b in range(nb) for z in range(N_CHIPS)]

        def start():
            for cp in copies():
                cp.start()

        def finish():
            for cp in copies():
                cp.wait_send()
            for cp in copies():
                cp.wait_recv()

        return start, _nothing, finish

    out_shape = [jax.ShapeDtypeStruct((N_CHIPS,) + tuple(a.shape[1:]), a.dtype) for a in arrays]
    return Exchange(arrays, out_shape, N_CHIPS, plan)


def pair_sum(name, pieces, from_sibling):
    _, r, c = pieces.shape
    tr = _row_tile(r, c * pieces.dtype.itemsize, 1 << 20)

    def body(p_ref, s_ref, o_ref):
        mine = p_ref[lax.axis_index("c")]
        o_ref[...] = (mine.astype(F32) + s_ref[...].astype(F32)).astype(o_ref.dtype)

    return pl.pallas_call(
        body, name=name, grid=(N_CHIPS, r // tr),
        in_specs=[pl.BlockSpec((None, 2, tr, c), lambda z, i: (z, 0, i, 0)), pl.BlockSpec((None, tr, c), lambda z, i: (z, i, 0))],
        out_specs=pl.BlockSpec((None, tr, c), lambda z, i: (z, i, 0)),
        out_shape=jax.ShapeDtypeStruct((N_CHIPS, r, c), pieces.dtype), compiler_params=_params(("parallel", "parallel")),
    )(pieces.reshape(N_CHIPS, 2, r, c), from_sibling)


def chip_exchange(arrays):
    nb = len(arrays)

    def plan(ins, outs, send_sems, recv_sems, local_sems):
        x, y, c = (lax.axis_index(n) for n in MESH_AXES)
        my_chip = 2 * x + y
        chips = [(1 - x, y), (x, 1 - y), (1 - x, 1 - y)]

        def local():
            return [pltpu.make_async_copy(ins[b].at[my_chip], outs[b].at[my_chip], local_sems.at[b]) for b in range(nb)]

        def copies():
            return [_remote(ins[b].at[2 * px + py], outs[b].at[my_chip], send_sems.at[b, j], recv_sems.at[b, j], (px, py, c))
                    for b in range(nb) for j, (px, py) in enumerate(chips)]

        def start():
            for cp in local() + copies():
                cp.start()

        def finish():
            for cp in copies():
                cp.wait_send()
            for cp in copies():
                cp.wait_recv()
            for cp in local():
                cp.wait()

        return start, _nothing, finish

    out_shape = [jax.ShapeDtypeStruct(a.shape, a.dtype) for a in arrays]
    return Exchange(arrays, out_shape, N_CHIPS - 1, plan)


def adam_update(name, pieces, w, m, v):
    r, c = w.shape
    n_pieces = pieces.shape[0]
    tr = _row_tile(r, 4 * c, 1 << 20)

    def body(p_ref, w_ref, m_ref, v_ref, g_ref, d_ref, mo_ref, vo_ref):
        g = p_ref[0].astype(F32)
        for j in range(1, n_pieces):
            g = g + p_ref[j].astype(F32)
        m_new = ADAM_B1 * m_ref[...] + (1.0 - ADAM_B1) * g
        v_new = ADAM_B2 * v_ref[...] + (1.0 - ADAM_B2) * (g * g)
        m_hat = m_new / (1.0 - ADAM_B1 ** ADAM_STEP)
        v_hat = v_new / (1.0 - ADAM_B2 ** ADAM_STEP)
        g_ref[...] = g
        d_ref[...] = -ADAM_LR * (m_hat / (jnp.sqrt(v_hat) + ADAM_EPS) + ADAM_WD * w_ref[...])
        mo_ref[...] = m_new
        vo_ref[...] = v_new

    row = pl.BlockSpec((tr, c), lambda i: (i, 0))
    out = jax.ShapeDtypeStruct((r, c), F32)
    return pl.pallas_call(
        body, name=name, grid=(r // tr,), in_specs=[pl.BlockSpec((n_pieces, tr, c), lambda i: (0, i, 0)), row, row, row],
        out_specs=[row] * 4, out_shape=[out] * 4, compiler_params=_params(("parallel",)),
    )(pieces, w, m, v)


PACK_WIDTH = 1024
PACK_ROWS = 64


def _pack(arrays, dtype, lead=0):
    parts = []
    for a in arrays:
        head = a.shape[:lead]
        f = a.reshape(head + (-1,)).astype(dtype)
        pad = (-f.shape[-1]) % PACK_WIDTH
        if pad:
            f = jnp.pad(f, [(0, 0)] * lead + [(0, pad)])
        parts.append(f.reshape(head + (-1, PACK_WIDTH)))
    out = jnp.concatenate(parts, axis=lead)
    pad = (-out.shape[lead]) % PACK_ROWS
    if pad:
        out = jnp.pad(out, [(0, 0)] * lead + [(0, pad), (0, 0)])
    return out


def _unpack(packed, shapes, lead=0):
    head = packed.shape[:lead]
    out, row = [], 0
    for s in shapes:
        n = int(np.prod(s))
        rows = -(-n // PACK_WIDTH)
        chunk = lax.slice_in_dim(packed, row, row + rows, axis=lead).reshape(head + (-1,))
        out.append(lax.slice_in_dim(chunk, 0, n, axis=lead).reshape(head + tuple(s)))
        row += rows
    return out


def _to_natural(stacked, kind):
    if kind == "col":
        m = jnp.moveaxis(stacked, 0, -2)
        return m.reshape(m.shape[:-2] + (m.shape[-2] * m.shape[-1],))
    m = jnp.moveaxis(stacked, 0, -3)
    return m.reshape(m.shape[:-3] + (m.shape[-3] * m.shape[-2], m.shape[-1]))


def _to_stacked(natural, kind):
    if kind == "col":
        m = natural.reshape(natural.shape[:-1] + (N_DEV, natural.shape[-1] // N_DEV))
        return jnp.moveaxis(m, -2, 0)
    m = natural.reshape(natural.shape[:-2] + (N_DEV, natural.shape[-2] // N_DEV, natural.shape[-1]))
    return jnp.moveaxis(m, -3, 0)


def _f_rms(h, g):
    return (_rms(h, g).astype(BF16),)


@jax.custom_vjp
def _swiglu(g, u):
    return g * _sigmoid(g) * u


def _swiglu_fwd(g, u):
    s = _sigmoid(g)
    return g * s * u, (g, u, s)


def _swiglu_bwd(saved, ct):
    g, u, s = saved
    return ct * u * (s * (1.0 + g * (1.0 - s))), ct * (g * s)


_swiglu.defvjp(_swiglu_fwd, _swiglu_bwd)


def _f_swiglu(g, u):
    return (_swiglu(g.astype(F32), u.astype(F32)).astype(BF16),)


def _f_ple(h, pre, pp):
    return (h + _sigmoid(pre) * pp,)


def _head_rms(x, g):
    ms = _seg_sum(x * x, HEAD_DIM) * (1.0 / HEAD_DIM)
    return x * _seg_expand(lax.rsqrt(ms + RMS_EPS), HEAD_DIM) * _tile_lanes(g, x.shape[1] // HEAD_DIM)


def _f_attpre(q, k, q_gain, k_gain):
    return _head_rms(q, q_gain) * (HEAD_DIM ** -0.5), _head_rms(k, k_gain)


def _f_shift(z, z_prev, mu):
    return (z + (z_prev - z) * mu,)


def _f_rwkvpre(r, k, v, xw, xa, xg, w0, w_up, a0, a_up, g_up, k_k, k_a):
    del r, v
    w_log = -_softplus(-(w0 + _mm(jnp.tanh(xw), w_up))) - 0.5
    decay = jnp.exp(-jnp.exp(w_log))
    a = _sigmoid(a0 + _mm(xa, a_up))
    g = _mm(_sigmoid(xg), g_up)
    kk = k * k_k
    norm = jnp.sqrt(_seg_expand(_seg_sum(kk * kk, HEAD_DIM), HEAD_DIM))
    kk = kk / jnp.maximum(norm, 1e-12)
    return k * (1.0 + (a - 1.0) * k_a), decay, -kk, kk * a, g


def _f_rwkvpost(y, r, k, v, g, lnx_w, lnx_b, r_k):
    mean = _seg_expand(_seg_sum(y, HEAD_DIM) * (1.0 / HEAD_DIM), HEAD_DIM)
    yc = y - mean
    var = _seg_expand(_seg_sum(yc * yc, HEAD_DIM) * (1.0 / HEAD_DIM), HEAD_DIM)
    yn = yc * lax.rsqrt(var + GN_EPS) * lnx_w + lnx_b
    bonus = _seg_expand(_seg_sum(r * k * r_k, HEAD_DIM), HEAD_DIM) * v
    return ((yn + bonus) * g,)


def _f_s5disc(lam_re, lam_im, log_dt):
    dt = jnp.exp(log_dt)
    mag = jnp.exp(lam_re * dt)
    ab_re, ab_im = mag * jnp.cos(lam_im * dt), mag * jnp.sin(lam_im * dt)
    denom = lam_re * lam_re + lam_im * lam_im
    z_re = ((ab_re - 1.0) * lam_re + ab_im * lam_im) / denom
    z_im = (ab_im * lam_re - (ab_re - 1.0) * lam_im) / denom
    return ab_re, ab_im, z_re, z_im


def _f_s5b(z_re, z_im, b_re, b_im):
    return z_re * b_re - z_im * b_im, z_re * b_im + z_im * b_re


def _f_s5post(ypre, u, d_skip):
    return (_gelu_tanh(ypre + d_skip * u).astype(BF16),)


def _f_glu(h, z1, z2):
    return (h + z1 * _sigmoid(z2),)


def _shift_down(x):
    return jnp.pad(x[:-1], ((1, 0), (0, 0)))


def _shift_up(x):
    return jnp.pad(x[1:], ((0, 1), (0, 0)))


def _block_diag(blocks):
    g, a, b = blocks.shape
    eye = jnp.eye(g, dtype=blocks.dtype)
    return (blocks[:, :, None, :] * eye[:, None, :, None]).reshape(g * a, g * b)


def _diag_blocks(dense, g):
    a, b = dense.shape[0] // g, dense.shape[1] // g
    return jnp.stack([dense[i * a:(i + 1) * a, i * b:(i + 1) * b] for i in range(g)], axis=0)


def _row_tile(t, width_bytes, budget=2 * 1024 * 1024):
    for tr in (512, 256, 128, 64, 32, 16, 8):
        if t % tr == 0 and tr * width_bytes <= budget:
            return tr
    return t


def _no_hosting(name):
    return None, None


def _hosting_matmul(hosting, name, a, b, **kw):
    exchange, done = hosting(name)
    if exchange is None:
        return matmul(name, a, b, **kw)
    out, travelled = matmul(name, a, b, hosted=exchange, **kw)
    done(travelled)
    return out


def _ffn_fwd(tag, h, norm_w, weight, hosting=_no_hosting):
    t, d = h.shape
    n = rowwise(f"{tag}_norm", _f_rms, [h], [norm_w], [(d, BF16)], _row_tile(t, 4 * d))[0]
    g = _hosting_matmul(hosting, f"{tag}_gate", n, weight("gate"), out_dtype=BF16)
    u = _hosting_matmul(hosting, f"{tag}_up", n, weight("up"), out_dtype=BF16)
    f = g.shape[1]
    a = rowwise(f"{tag}_act", _f_swiglu, [g, u], [], [(f, BF16)], _row_tile(t, 4 * f, 1 << 22))[0]
    return _hosting_matmul(hosting, f"{tag}_down", a, weight("down"), alpha=0.5, res=h), (h, n, g, u, a)


def _ffn_bwd(tag, dh2, saved, norm_w, w_gate, w_up, w_down):
    h, n, g, u, a = saved
    t, d = h.shape
    f = w_gate.shape[1]
    da = matmul(f"{tag}_da", dh2, w_down, tb=True, alpha=0.5, out_dtype=BF16)
    d_down = matmul(f"{tag}_dwdown", a, dh2, ta=True, alpha=0.5, out_dtype=BF16)
    dg, du = rowwise_vjp(f"{tag}_dact", _f_swiglu, [g, u], [], [da], _row_tile(t, 4 * f, 1 << 22), [True, True], [],
                         grad_dtypes={0: BF16, 1: BF16})
    dn = matmul(f"{tag}_dn_gate", dg, w_gate, tb=True)
    dn = matmul(f"{tag}_dn_up", du, w_up, tb=True, res=dn)
    d_gate = matmul(f"{tag}_dwgate", n, dg, ta=True, out_dtype=BF16)
    d_up = matmul(f"{tag}_dwup", n, du, ta=True, out_dtype=BF16)
    dh, d_norm = rowwise_vjp(f"{tag}_dnorm", _f_rms, [h], [norm_w], [dn], _row_tile(t, 4 * d), [True], [True],
                             add_to={0: dh2})
    return dh, d_norm, d_gate, d_up, d_down


def _ple_fwd(tag, h, norm_w, w_gate, w_proj, p_i, hosting=_no_hosting):
    t, d = h.shape
    n = rowwise(f"{tag}_norm", _f_rms, [h], [norm_w], [(d, BF16)], _row_tile(t, 4 * d))[0]
    pre = _hosting_matmul(hosting, f"{tag}_gate", n, w_gate)
    pp = matmul(f"{tag}_proj", p_i, w_proj)
    h2 = rowwise(f"{tag}_out", _f_ple, [h, pre, pp], [], [(d, F32)], _row_tile(t, 4 * d))[0]
    return h2, (h, n, pre, pp)


def _ple_bwd(tag, dh2, saved, norm_w, w_gate, w_proj, p_i):
    h, n, pre, pp = saved
    t, d = h.shape
    dpre, dpp = rowwise_vjp(f"{tag}_dout", _f_ple, [h, pre, pp], [], [dh2], _row_tile(t, 4 * d), [False, True, True], [],
                            grad_dtypes={1: BF16, 2: BF16})
    d_proj = matmul(f"{tag}_dwproj", p_i, dpp, ta=True, out_dtype=BF16)
    d_gate = matmul(f"{tag}_dwgate", n, dpre, ta=True, out_dtype=BF16)
    dn = matmul(f"{tag}_dn", dpre, w_gate, tb=True)
    dh, d_norm = rowwise_vjp(f"{tag}_dnorm", _f_rms, [h], [norm_w], [dn], _row_tile(t, 4 * d), [True], [True],
                             add_to={0: dh2})
    return dh, d_norm, d_gate, d_proj


def _ab_fwd(h, w, hosting=_no_hosting):
    t, d = h.shape
    da = d // 2
    n = rowwise("ab_norm", _f_rms, [h], [w["mix_norm"]], [(d, BF16)], _row_tile(t, 4 * d))[0]
    proj = _hosting_matmul(hosting, "ab_in", n, w["ab_w_in"])
    q_raw, k_raw, v_att, z = proj[:, :da], proj[:, da:2 * da], proj[:, 2 * da:3 * da], proj[:, 3 * da:]
    tr = _row_tile(t, 4 * da, 1 << 19)
    qn, kn = rowwise("att_pre", _f_attpre, [q_raw, k_raw], [w["att_q_gain"], w["att_k_gain"]], [(da, F32)] * 2, tr)
    bias = jnp.transpose(bias_expand(w["att_rel_bias"]), (1, 0, 2))
    qh, kh, vh = qn, kn, v_att
    exchange, done = hosting("attention_fwd")
    att = attention_fwd(qh, kh, vh, bias, hosted=exchange)
    if exchange is not None:
        att, travelled = att
        done(travelled)
    z_prev = _shift_down(z)
    zz = rowwise("rwkv_shift", _f_shift, [z, z_prev], [w["rwkv_mu"]], [(z.shape[1], F32)], _row_tile(t, 4 * z.shape[1]))[0]
    o = [0, da, 2 * da, 3 * da, 3 * da + DECAY_LORA, 3 * da + DECAY_LORA + AAA_LORA, z.shape[1]]
    r, k, v, xw, xa, xg = (zz[:, o[i]:o[i + 1]] for i in range(6))
    pre_consts = [w[nm] for nm in ("rwkv_w0", "rwkv_w_up", "rwkv_a0", "rwkv_a_up", "rwkv_g_up", "rwkv_k_k", "rwkv_k_a")]
    k2, decay, ia, ib, g = rowwise("rwkv_pre", _f_rwkvpre, [r, k, v, xw, xa, xg], pre_consts, [(da, F32)] * 5, tr)
    heads = [r, k2, v, decay, ia, ib]
    exchange, done = hosting("rwkv_fwd")
    y, states, *travelled = rwkv_fwd(*heads, hosted=exchange)
    if exchange is not None:
        done(travelled)
    post_consts = [w["rwkv_lnx_w"], w["rwkv_lnx_b"], w["rwkv_r_k"]]
    rw = rowwise("rwkv_post", _f_rwkvpost, [y, r, k2, v, g], post_consts, [(da, F32)], tr)[0]
    cat = jnp.concatenate([att, rw], axis=1).astype(BF16)
    h2 = _hosting_matmul(hosting, "ab_out", cat, w["ab_w_out"], res=h)
    saved = dict(h=h, n=n, q_raw=q_raw, k_raw=k_raw, qh=qh, kh=kh, vh=vh, bias=bias, z=z, z_prev=z_prev,
                 rows=(r, k, v, xw, xa, xg), pre_consts=pre_consts, post=(y, r, k2, v, g), post_consts=post_consts,
                 heads=heads, states=states, cat=cat, tr=tr)
    return h2, saved


def _ab_bwd(dh2, s, w, make_hosted=None, hosted_done=None):
    h = s["h"]
    t, d = h.shape
    da = d // 2
    tr = s["tr"]
    grads = {}
    dcat = matmul("ab_dcat", dh2, w["ab_w_out"], tb=True)
    grads["ab_w_out"] = matmul("ab_dwout", s["cat"], dh2, ta=True, out_dtype=BF16)
    d_att, d_rw = dcat[:, :da], dcat[:, da:]
    dy, dr1, dk1, dv1, dg, grads["rwkv_lnx_w"], grads["rwkv_lnx_b"], grads["rwkv_r_k"] = rowwise_vjp(
        "rwkv_dpost", _f_rwkvpost, list(s["post"]), s["post_consts"], [d_rw], tr, [True] * 5, [True] * 3)
    hosted = None if make_hosted is None else make_hosted(grads["ab_w_out"])
    *d_heads, = rwkv_bwd(*s["heads"], s["states"], dy, hosted=hosted)
    if hosted is not None:
        hosted_done(d_heads[6:])
        d_heads = d_heads[:6]
    dr2, dk2, dv2, ddecay, dia, dib = d_heads
    pre = rowwise_vjp("rwkv_dpre", _f_rwkvpre, list(s["rows"]), s["pre_consts"], [(dk1, dk2), ddecay, dia, dib, dg], tr,
                      [True] * 6, [True] * 7, add_to={0: (dr1, dr2), 2: (dv1, dv2)})
    for nm, g in zip(("rwkv_w0", "rwkv_w_up", "rwkv_a0", "rwkv_a_up", "rwkv_g_up", "rwkv_k_k", "rwkv_k_a"), pre[6:]):
        grads[nm] = g
    dzz = jnp.concatenate(pre[:6], axis=1)
    trz = _row_tile(t, 4 * dzz.shape[1])
    grads["rwkv_mu"] = rowwise_vjp("rwkv_dmu", _f_shift, [s["z"], s["z_prev"]], [w["rwkv_mu"]], [dzz], trz,
                                   [False, False], [True])[0]
    dz = rowwise("rwkv_dshift", _f_shift, [dzz, _shift_up(dzz)], [w["rwkv_mu"]], [(dzz.shape[1], F32)], trz)[0]
    dqh, dkh, dvh, dbias = attention_bwd(s["qh"], s["kh"], s["vh"], s["bias"], d_att)
    grads["att_rel_bias"] = bias_reduce(jnp.transpose(dbias, (1, 0, 2)))
    dq_raw, dk_raw, grads["att_q_gain"], grads["att_k_gain"] = rowwise_vjp(
        "att_dpre", _f_attpre, [s["q_raw"], s["k_raw"]], [w["att_q_gain"], w["att_k_gain"]],
        [dqh, dkh], tr, [True, True], [True, True])
    dproj = jnp.concatenate([dq_raw, dk_raw, dvh, dz], axis=1).astype(BF16)
    dn = matmul("ab_dn", dproj, w["ab_w_in"], tb=True)
    grads["ab_w_in"] = matmul("ab_dwin", s["n"], dproj, ta=True, out_dtype=BF16)
    dh, grads["mix_norm"] = rowwise_vjp("ab_dnorm", _f_rms, [h], [w["mix_norm"]], [dn], _row_tile(t, 4 * d), [True], [True],
                                        add_to={0: dh2})
    return dh, grads


def _s5_fwd(h, w):
    t, d = h.shape
    n_groups, n_state = w["ssm_lambda_re"].shape
    gp = n_groups * n_state
    n = rowwise("s5_norm", _f_rms, [h], [w["mix_norm"]], [(d, BF16)], _row_tile(t, 4 * d))[0]
    u = matmul("s5_in", n, w["ssm_w_in"])
    disc_rows = [w["ssm_lambda_re"], w["ssm_lambda_im"], w["ssm_log_dt"]]
    ab_re, ab_im, z_re, z_im = rowwise("s5_disc", _f_s5disc, disc_rows, [], [(n_state, F32)] * 4, n_groups)
    b_rows = [z_re.reshape(gp, 1), z_im.reshape(gp, 1), w["ssm_b_re"], w["ssm_b_im"]]
    trb = _row_tile(gp, 512)
    bb_re, bb_im = rowwise("s5_bbar", _f_s5b, b_rows, [], [(SSM_GROUP, F32)] * 2, trb)
    to_dense = lambda bb: _block_diag(jnp.transpose(bb.reshape(n_groups, n_state, SSM_GROUP), (0, 2, 1)))
    bd_re, bd_im = to_dense(bb_re), to_dense(bb_im)
    cd_re = _block_diag(jnp.transpose(w["ssm_c_re"], (0, 2, 1)))
    cd_im = -_block_diag(jnp.transpose(w["ssm_c_im"], (0, 2, 1)))
    a_re, a_im = ab_re.reshape(1, gp), ab_im.reshape(1, gp)
    bu_re = matmul("s5_bu_re", u, bd_re)
    bu_im = matmul("s5_bu_im", u, bd_im)
    h_re, h_im = s5_scan(bu_re, bu_im, a_re, a_im)
    ypre = matmul("s5_y_re", h_re, cd_re)
    ypre = matmul("s5_y_im", h_im, cd_im, res=ypre)
    tru = _row_tile(t, 4 * u.shape[1])
    yg = rowwise("s5_post", _f_s5post, [ypre, u], [w["ssm_d"]], [(u.shape[1], BF16)], tru)[0]
    w_out1, w_out2 = w["ssm_w_out"][:, :d], w["ssm_w_out"][:, d:]
    z1 = matmul("s5_out1", yg, w_out1)
    z2 = matmul("s5_out2", yg, w_out2)
    h2 = rowwise("s5_glu", _f_glu, [h, z1, z2], [], [(d, F32)], _row_tile(t, 4 * d))[0]
    saved = dict(h=h, n=n, u=u, disc_rows=disc_rows, b_rows=b_rows, trb=trb, bd=(bd_re, bd_im), cd=(cd_re, cd_im),
                 a=(a_re, a_im), hs=(h_re, h_im), ypre=ypre, yg=yg, w_out=(w_out1, w_out2), z=(z1, z2), tru=tru)
    return h2, saved


def _s5_bwd(dh2, s, w):
    h, u = s["h"], s["u"]
    t, d = h.shape
    n_groups, n_state = w["ssm_lambda_re"].shape
    gp = n_groups * n_state
    grads = {}
    z1, z2 = s["z"]
    w_out1, w_out2 = s["w_out"]
    dz1, dz2 = rowwise_vjp("s5_dglu", _f_glu, [h, z1, z2], [], [dh2], _row_tile(t, 4 * d), [False, True, True], [],
                           grad_dtypes={1: BF16, 2: BF16})
    dyg = matmul("s5_dyg1", dz1, w_out1, tb=True)
    dyg = matmul("s5_dyg2", dz2, w_out2, tb=True, res=dyg)
    grads["ssm_w_out"] = jnp.concatenate([matmul("s5_dwout1", s["yg"], dz1, ta=True, out_dtype=BF16),
                                          matmul("s5_dwout2", s["yg"], dz2, ta=True, out_dtype=BF16)], axis=1)
    dypre, du1, grads["ssm_d"] = rowwise_vjp("s5_dpost", _f_s5post, [s["ypre"], u], [w["ssm_d"]], [dyg], s["tru"],
                                             [True, True], [True])
    cd_re, cd_im = s["cd"]
    h_re, h_im = s["hs"]
    dh_re = matmul("s5_dh_re", dypre, cd_re, tb=True)
    dh_im = matmul("s5_dh_im", dypre, cd_im, tb=True)
    dcd_re = matmul("s5_dc_re", h_re, dypre, ta=True)
    dcd_im = matmul("s5_dc_im", h_im, dypre, ta=True)
    a_re, a_im = s["a"]
    g_re, g_im, da_re, da_im = s5_scan(dh_re, dh_im, a_re, a_im, reverse=True, h_prev=(h_re, h_im))
    bd_re, bd_im = s["bd"]
    du = matmul("s5_du_re", g_re, bd_re, tb=True, res=du1)
    du = matmul("s5_du_im", g_im, bd_im, tb=True, res=du)
    dbd_re = matmul("s5_db_re", u, g_re, ta=True)
    dbd_im = matmul("s5_db_im", u, g_im, ta=True)
    grads["ssm_w_in"] = matmul("s5_dwin", s["n"], du, ta=True, out_dtype=BF16)
    dn = matmul("s5_dn", du, w["ssm_w_in"], tb=True)
    dh, grads["mix_norm"] = rowwise_vjp("s5_dnorm", _f_rms, [h], [w["mix_norm"]], [dn], _row_tile(t, 4 * d), [True], [True],
                                        add_to={0: dh2})
    from_dense = lambda m: jnp.transpose(_diag_blocks(m, n_groups), (0, 2, 1)).reshape(gp, SSM_GROUP)
    dz_re, dz_im, grads["ssm_b_re"], grads["ssm_b_im"] = rowwise_vjp(
        "s5_dbbar", _f_s5b, s["b_rows"], [], [from_dense(dbd_re), from_dense(dbd_im)], s["trb"], [True] * 4, [])
    disc_cots = [da_re.reshape(n_groups, n_state), da_im.reshape(n_groups, n_state),
                 dz_re.reshape(n_groups, n_state), dz_im.reshape(n_groups, n_state)]
    grads["ssm_lambda_re"], grads["ssm_lambda_im"], grads["ssm_log_dt"] = rowwise_vjp(
        "s5_ddisc", _f_s5disc, s["disc_rows"], [], disc_cots, n_groups, [True] * 3, [])
    grads["ssm_c_re"] = jnp.transpose(_diag_blocks(dcd_re, n_groups), (0, 2, 1))
    grads["ssm_c_im"] = -jnp.transpose(_diag_blocks(dcd_im, n_groups), (0, 2, 1))
    return dh, grads


WEIGHTS = ["ffn1_norm", "ffn1_w_gate", "ffn1_w_up", "ffn1_w_down", "mix_norm", "ffn2_norm", "ffn2_w_gate", "ffn2_w_up",
           "ffn2_w_down", "ple_norm", "ple_w_gate", "ple_w_proj", "ab_w_in", "att_q_gain", "att_k_gain", "att_rel_bias",
           "rwkv_mu", "rwkv_w0", "rwkv_w_up", "rwkv_a0", "rwkv_a_up", "rwkv_g_up", "rwkv_k_k", "rwkv_k_a", "rwkv_r_k",
           "rwkv_lnx_w", "rwkv_lnx_b", "ab_w_out", "ssm_w_in", "ssm_lambda_re", "ssm_lambda_im", "ssm_log_dt", "ssm_b_re",
           "ssm_b_im", "ssm_c_re", "ssm_c_im", "ssm_d", "ssm_w_out"]
BIG = {"ffn1_w_gate": "col", "ffn1_w_up": "col", "ffn1_w_down": "row", "ffn2_w_gate": "col", "ffn2_w_up": "col",
       "ffn2_w_down": "row", "ple_w_gate": "row", "ple_w_proj": "col", "ab_w_in": "col", "ab_w_out": "row",
       "ssm_w_in": "row", "ssm_w_out": "col"}
SMALL_CUT = {"rwkv_w_up": "col", "rwkv_a_up": "col", "rwkv_g_up": "col", "ssm_d": "col"}
REPLICATED = [n for n in WEIGHTS if n not in BIG and n not in SMALL_CUT]


def kernel(x, p, ffn1_norm, ffn1_w_gate, ffn1_w_up, ffn1_w_down, mix_norm, ffn2_norm, ffn2_w_gate, ffn2_w_up, ffn2_w_down, ple_norm, ple_w_gate, ple_w_proj, ab_w_in, att_q_gain, att_k_gain, att_rel_bias, rwkv_mu, rwkv_w0, rwkv_w_up, rwkv_a0, rwkv_a_up, rwkv_g_up, rwkv_k_k, rwkv_k_a, rwkv_r_k, rwkv_lnx_w, rwkv_lnx_b, ab_w_out, ssm_w_in, ssm_lambda_re, ssm_lambda_im, ssm_log_dt, ssm_b_re, ssm_b_im, ssm_c_re, ssm_c_im, ssm_d, ssm_w_out, loss_target, m_ffn1_norm, m_ffn1_w_gate, m_ffn1_w_up, m_ffn1_w_down, m_mix_norm, m_ffn2_norm, m_ffn2_w_gate, m_ffn2_w_up, m_ffn2_w_down, m_ple_norm, m_ple_w_gate, m_ple_w_proj, m_ab_w_in, m_att_q_gain, m_att_k_gain, m_att_rel_bias, m_rwkv_mu, m_rwkv_w0, m_rwkv_w_up, m_rwkv_a0, m_rwkv_a_up, m_rwkv_g_up, m_rwkv_k_k, m_rwkv_k_a, m_rwkv_r_k, m_rwkv_lnx_w, m_rwkv_lnx_b, m_ab_w_out, m_ssm_w_in, m_ssm_lambda_re, m_ssm_lambda_im, m_ssm_log_dt, m_ssm_b_re, m_ssm_b_im, m_ssm_c_re, m_ssm_c_im, m_ssm_d, m_ssm_w_out, v_ffn1_norm, v_ffn1_w_gate, v_ffn1_w_up, v_ffn1_w_down, v_mix_norm, v_ffn2_norm, v_ffn2_w_gate, v_ffn2_w_up, v_ffn2_w_down, v_ple_norm, v_ple_w_gate, v_ple_w_proj, v_ab_w_in, v_att_q_gain, v_att_k_gain, v_att_rel_bias, v_rwkv_mu, v_rwkv_w0, v_rwkv_w_up, v_rwkv_a0, v_rwkv_a_up, v_rwkv_g_up, v_rwkv_k_k, v_rwkv_k_a, v_rwkv_r_k, v_rwkv_lnx_w, v_rwkv_lnx_b, v_ab_w_out, v_ssm_w_in, v_ssm_lambda_re, v_ssm_lambda_im, v_ssm_log_dt, v_ssm_b_re, v_ssm_b_im, v_ssm_c_re, v_ssm_c_im, v_ssm_d, v_ssm_w_out):
    vals = dict(locals())
    depth = ffn1_norm.shape[0]
    n_groups, n_state = ssm_lambda_re.shape[1:]

    kinds = {**BIG, **SMALL_CUT}
    travel = {n: (BF16 if n in BIG else F32) for n in kinds}

    def rows(a, lead=0):
        return a.reshape(a.shape[:lead] + (-1, a.shape[-1]))

    def model_layer(n, j):
        if vals[n].shape[0] == depth:
            return j
        return 2 * j if n.startswith(("ab_", "rwkv_")) else 2 * j + 1

    units = [(n, j) for n in kinds for j in range(vals[n].shape[0])]
    early_names = ("ffn1_w_gate", "ffn1_w_up", "ffn1_w_down", "ab_w_in", "rwkv_w_up", "rwkv_a_up", "rwkv_g_up")
    early = [u for u in units if model_layer(*u) == 0 and u[0] in early_names]
    late = [u for u in units if u not in early]
    full = {}

    def shard(unit):
        n, j = unit
        return rows(vals[n][j]).astype(travel[n])

    def take_gathered(which, got):
        for (n, j), g in zip(which, got):
            full[(n, j)] = _to_natural(g.reshape((N_DEV,) + vals[n].shape[1:]), kinds[n])

    first = [("ffn1_w_gate", 0), ("ffn1_w_up", 0)]
    beside = {"l0_ffn1_gate": [("ffn1_w_down", 0)],
              "l0_ffn1_up": [("ab_w_in", 0), ("rwkv_w_up", 0), ("rwkv_a_up", 0), ("rwkv_g_up", 0)],
              "l0_ffn1_down": [("ab_w_out", 0), ("ffn2_w_gate", 0)],
              "ab_in": [("ffn2_w_up", 0)],
              "attention_fwd": [("ffn2_w_down", 0), ("ple_w_gate", 0), ("ple_w_proj", 0)],
              "l0_ffn2_gate": [("ffn2_w_gate", 1)], "l0_ffn2_up": [("ffn2_w_up", 1)], "l0_ffn2_down": [("ffn2_w_down", 1)],
              "l0_ple_gate": [("ple_w_gate", 1), ("ple_w_proj", 1)]}
    planned = first + [u for us in beside.values() for u in us]
    beside["rwkv_fwd"] = [u for u in units if u not in planned]
    assert depth == 2 and sorted(planned + beside["rwkv_fwd"]) == sorted(units)
    take_gathered(first, gather_all([shard(u) for u in first]).run("gather_first"))

    def hosting(name):
        which = beside.get(name)
        if not which:
            return None, None
        return gather_all([shard(u) for u in which]), functools.partial(take_gathered, which)

    def row(name, j):
        return vals[name][j].reshape(1, -1)

    def ffn_weights(which, i):
        return (row(f"{which}_norm", i), full[(f"{which}_w_gate", i)], full[(f"{which}_w_up", i)],
                full[(f"{which}_w_down", i)])

    def mixer_weights(i):
        j = i // 2
        if i % 2 == 0:
            w = {n: row(n, j) for n in ("att_q_gain", "att_k_gain", "rwkv_mu", "rwkv_w0", "rwkv_a0", "rwkv_k_k", "rwkv_k_a",
                                        "rwkv_r_k", "rwkv_lnx_w", "rwkv_lnx_b")}
            w.update({n: full.get((n, j)) for n in ("ab_w_in", "ab_w_out", "rwkv_w_up", "rwkv_a_up", "rwkv_g_up")})
            w["att_rel_bias"] = att_rel_bias[j]
        else:
            w = {"ssm_lambda_re": ssm_lambda_re[j], "ssm_lambda_im": ssm_lambda_im[j],
                 "ssm_log_dt": ssm_log_dt[j].reshape(n_groups, 1),
                 "ssm_b_re": ssm_b_re[j].reshape(n_groups * n_state, -1),
                 "ssm_b_im": ssm_b_im[j].reshape(n_groups * n_state, -1),
                 "ssm_c_re": ssm_c_re[j], "ssm_c_im": ssm_c_im[j], "ssm_d": full[("ssm_d", j)].reshape(1, -1),
                 "ssm_w_in": full[("ssm_w_in", j)], "ssm_w_out": full[("ssm_w_out", j)]}
        w["mix_norm"] = row("mix_norm", i)
        return w

    def ple_weights(i):
        return (row("ple_norm", i), full[("ple_w_gate", i)], full[("ple_w_proj", i)], p[i, 0])

    h = x[0]
    saved = []
    for i in range(depth):
        h, s1 = _ffn_fwd(f"l{i}_ffn1", h, row("ffn1_norm", i), lambda kind, i=i: full[(f"ffn1_w_{kind}", i)], hosting)
        h, sm = _ab_fwd(h, mixer_weights(i), hosting) if i % 2 == 0 else _s5_fwd(h, mixer_weights(i))
        h, s2 = _ffn_fwd(f"l{i}_ffn2", h, row("ffn2_norm", i), lambda kind, i=i: full[(f"ffn2_w_{kind}", i)], hosting)
        h, sp = _ple_fwd(f"l{i}_ple", h, *ple_weights(i), hosting)
        saved.append((s1, sm, s2, sp))
    dh, loss_part = loss_head(h, loss_target[0])

    per_layer = {n: [] for n in WEIGHTS}
    received = {}

    def chip_sums(which, tag):
        pieces = [rows(_to_stacked(per_layer[n][j - vals[n].shape[0]], kinds[n]).astype(travel[n]), lead=1)
                  for n, j in which]
        from_sibling = pair_exchange(pieces).run(f"reduce_pair_{tag}")
        return [pair_sum(f"pair_sum_{n}{j}", a, b) for (n, j), a, b in zip(which, pieces, from_sibling)]

    def late_reduce(d_ab_w_out):
        per_layer["ab_w_out"].insert(0, d_ab_w_out)
        return chip_exchange(chip_sums(late, "late"))

    def late_reduce_done(got):
        received.update(zip(late, got))

    for i in reversed(range(depth)):
        s1, sm, s2, sp = saved[i]
        dh, d_norm, d_gate, d_proj = _ple_bwd(f"l{i}_ple", dh, sp, *ple_weights(i))
        for n, g in (("ple_norm", d_norm), ("ple_w_gate", d_gate), ("ple_w_proj", d_proj)):
            per_layer[n].insert(0, g)
        dh, d_norm, d_gate, d_up, d_down = _ffn_bwd(f"l{i}_ffn2", dh, s2, *ffn_weights("ffn2", i))
        for n, g in (("ffn2_norm", d_norm), ("ffn2_w_gate", d_gate), ("ffn2_w_up", d_up), ("ffn2_w_down", d_down)):
            per_layer[n].insert(0, g)
        if i == 0:
            dh, mixer_grads = _ab_bwd(dh, sm, mixer_weights(i), late_reduce, late_reduce_done)
        else:
            dh, mixer_grads = (_ab_bwd if i % 2 == 0 else _s5_bwd)(dh, sm, mixer_weights(i))
        for n, g in mixer_grads.items():
            if not (i == 0 and n == "ab_w_out"):
                per_layer[n].insert(0, g)
        dh, d_norm, d_gate, d_up, d_down = _ffn_bwd(f"l{i}_ffn1", dh, s1, *ffn_weights("ffn1", i))
        for n, g in (("ffn1_norm", d_norm), ("ffn1_w_gate", d_gate), ("ffn1_w_up", d_up), ("ffn1_w_down", d_down)):
            per_layer[n].insert(0, g)
    grad_x = dh[None]

    received.update(zip(early, chip_exchange(chip_sums(early, "early")).run("reduce_chips_early")))

    rep_mine = _pack([jnp.stack(per_layer[n], axis=0).reshape(vals[n].shape) for n in REPLICATED] + [loss_part], F32)
    rep_all = gather_all([rep_mine]).run("gather_replicated")[0]

    out = {}
    for n in kinds:
        got = [received[(n, j)] for j in range(vals[n].shape[0])]
        got = got[0] if len(got) == 1 else jnp.concatenate(got, axis=1)
        state = [rows(vals[pre + n]) for pre in ("", "m_", "v_")]
        out[n] = tuple(r.reshape(vals[n].shape) for r in adam_update(f"adam_{n}", got, *state))
    zero = jnp.zeros((1, 1), F32)
    state = [_pack([vals[pre + n] for n in REPLICATED] + [zero], F32) for pre in ("", "m_", "v_")]
    shapes = [vals[n].shape for n in REPLICATED] + [zero.shape]
    results = [_unpack(r, shapes) for r in adam_update("adam_replicated", rep_all, *state)]
    for q, n in enumerate(REPLICATED):
        out[n] = tuple(r[q] for r in results)
    loss = results[0][-1].reshape(())
    return (loss, grad_x, *[out[n][0] for n in WEIGHTS], *[out[n][1] for n in WEIGHTS], *[out[n][2] for n in WEIGHTS],
            *[out[n][3] for n in WEIGHTS])
```
